```python
import functools
import jax, jax.numpy as jnp
from jax import lax
import numpy as np

D_MODEL = 1024
BATCH = 2
SEQ = 16384
DEPTH = 1
DEC_BATCH = 16
DEC_SEQ = 32
PAST_LEN = 2048

CHUNK = 64
MIX = D_MODEL
A_HEADS = 8
A_DH = 64
A_WIDTH = A_HEADS * A_DH
BAND_CHUNKS = 8
WINDOW = BAND_CHUNKS * CHUNK
MAX_REL = 128
ATT_SCALE = A_DH ** -0.5
M_HEADS = 4
M_WIDTH = MIX - A_WIDTH
M_DH = M_WIDTH // M_HEADS
CONV_W = 4
IN_COLS = 3 * A_WIDTH + 4 * M_WIDTH + 2 * M_HEADS
P_HEADS = 8
P_DKEY = 256
N_KEYS = 128
N_EXPERTS = N_KEYS * N_KEYS
P_TOPK = 16
TOK_BLOCK = 256
EPS = 1e-6

kernel_name = 'hymba_chunkattn_mlstm_peer_stream_step'


def rmsnorm(x, g):
    xf = x.astype(jnp.float32)
    y = xf * lax.rsqrt(jnp.mean(xf * xf, axis=-1, keepdims=True) + EPS)
    return (y * g.astype(jnp.float32)).astype(x.dtype)


def rel_bias_lookup(rel_bias, rel):
    return rel_bias[:, jnp.clip(rel, -MAX_REL, MAX_REL) + MAX_REL].astype(jnp.float32)


def attn_prompt(q, k, v, rel_bias):
    B, S = q.shape[:2]
    nc = S // CHUNK
    nband = (BAND_CHUNKS + 1) * CHUNK
    qc = q.reshape(B, nc, CHUNK, A_HEADS, A_DH)
    pad = ((0, 0), (BAND_CHUNKS, 0), (0, 0), (0, 0), (0, 0))
    kp = jnp.pad(k.reshape(B, nc, CHUNK, A_HEADS, A_DH), pad)
    vp = jnp.pad(v.reshape(B, nc, CHUNK, A_HEADS, A_DH), pad)
    band = jnp.arange(nc)[:, None] + jnp.arange(BAND_CHUNKS + 1)[None, :]
    kb = kp[:, band].reshape(B, nc, nband, A_HEADS, A_DH)
    vb = vp[:, band].reshape(B, nc, nband, A_HEADS, A_DH)
    rel = BAND_CHUNKS * CHUNK + jnp.arange(CHUNK)[:, None] - jnp.arange(nband)[None, :]
    bias = rel_bias_lookup(rel_bias, rel)
    s = jnp.einsum('bnqhd,bnkhd->bnhqk', qc, kb).astype(jnp.float32) * ATT_SCALE + bias
    valid = (jnp.arange(nc)[:, None] - BAND_CHUNKS + jnp.arange(nband)[None, :] // CHUNK) >= 0
    s = jnp.where(valid[None, :, None, None, :], s, -jnp.inf)
    p = jax.nn.softmax(s, axis=-1).astype(vb.dtype)
    o = jnp.einsum('bnhqk,bnkhd->bnqhd', p, vb).reshape(B, S, A_WIDTH)
    keep = min(WINDOW, S)
    return o, k[:, S - keep:], v[:, S - keep:]


def attn_step(q, k, v, ck, cv, rel_bias):
    L, T = ck.shape[1], q.shape[1]
    B = q.shape[0]
    kk = jnp.concatenate([ck.astype(k.dtype), k], axis=1)
    vv = jnp.concatenate([cv.astype(v.dtype), v], axis=1)
    rel = L + jnp.arange(T)[:, None] - jnp.arange(L + T)[None, :]
    bias = rel_bias_lookup(rel_bias, rel)
    s = jnp.einsum('bqhd,bkhd->bhqk', q, kk).astype(jnp.float32) * ATT_SCALE + bias
    p = jax.nn.softmax(s, axis=-1).astype(vv.dtype)
    o = jnp.einsum('bhqk,bkhd->bqhd', p, vv).reshape(B, T, A_WIDTH)
    return o, k, v


def causal_conv(a, buf, w, b):
    full = jnp.concatenate([buf.astype(a.dtype), a], axis=1)
    out = lax.conv_general_dilated(full, w.astype(a.dtype)[:, None, :], window_strides=(1,), padding='VALID',
                                   dimension_numbers=('NWC', 'WIO', 'NWC'), feature_group_count=a.shape[-1])
    return out + b.astype(a.dtype), full[:, -(CONV_W - 1):]


def mlstm(q, k, v, i_pre, log_f, C0, n0, m0):
    B, T, H, d = q.shape
    lc = min(CHUNK, T)
    nc = T // lc
    f32 = jnp.float32

    def blocks(t):
        return t.astype(f32).reshape(B, nc, lc, H, d).transpose(1, 0, 3, 2, 4)

    def gblocks(t):
        return t.astype(f32).reshape(B, nc, lc, H).transpose(1, 0, 3, 2)

    causal = jnp.tril(jnp.ones((lc, lc), dtype=bool))

    def step(carry, inp):
        C, n, m = carry
        qc, kc, vc, ic, fc = inp
        bcum = jnp.cumsum(fc, axis=-1)
        dmat = jnp.where(causal, bcum[..., :, None] - bcum[..., None, :] + ic[..., None, :], -jnp.inf)
        inter = bcum + m[..., None]
        mt = jnp.maximum(jnp.max(dmat, axis=-1), inter)
        w_intra = jnp.exp(dmat - mt[..., None])
        w_inter = jnp.exp(inter - mt)
        s = jnp.einsum('bhtd,bhsd->bhts', qc, kc) * w_intra
        num = w_inter[..., None] * jnp.einsum('bhtd,bhde->bhte', qc, C) + jnp.einsum('bhts,bhse->bhte', s, vc)
        den = w_inter * jnp.einsum('bhtd,bhd->bht', qc, n) + jnp.sum(s, axis=-1)
        h = num / jnp.maximum(jnp.abs(den), jnp.exp(-mt))[..., None]
        m_new = mt[..., -1]
        w_s = jnp.exp(bcum[..., -1:] - bcum + ic - m_new[..., None])
        decay = jnp.exp(bcum[..., -1] + m - m_new)
        C = decay[..., None, None] * C + jnp.einsum('bhs,bhsd,bhse->bhde', w_s, kc, vc)
        n = decay[..., None] * n + jnp.einsum('bhs,bhsd->bhd', w_s, kc)
        return (C, n, m_new), h

    (C, n, m), h = lax.scan(step, (C0.astype(f32), n0.astype(f32), m0.astype(f32)),
                            (blocks(q), blocks(k), blocks(v), gblocks(i_pre), gblocks(log_f)))
    h = h.transpose(1, 0, 3, 2, 4).reshape(B, T, H, d)
    return h, C, n, m


def peer(xn, wq, keys, u_tab, v_tab):
    B, T, D = xn.shape
    ntok = B * T
    blk = min(TOK_BLOCK, ntok)
    nb = -(-ntok // blk)
    xt = jnp.pad(xn.reshape(ntok, D), ((0, nb * blk - ntok), (0, 0))).reshape(nb, blk, D)

    def block(xb):
        q = (xb @ wq).reshape(blk, P_HEADS, 2, P_DKEY // 2)
        sc = jnp.einsum('thpd,hpnd->thpn', q, keys).astype(jnp.float32)
        sv, si = lax.top_k(sc, P_TOPK)
        cand = (sv[:, :, 0, :, None] + sv[:, :, 1, None, :]).reshape(blk, P_HEADS, P_TOPK * P_TOPK)
        cidx = (si[:, :, 0, :, None] * N_KEYS + si[:, :, 1, None, :]).reshape(blk, P_HEADS, P_TOPK * P_TOPK)
        top_s, pos = lax.top_k(cand, P_TOPK)
        eidx = jnp.take_along_axis(cidx, pos, axis=-1)
        g = jax.nn.softmax(top_s, axis=-1)
        act = jax.nn.gelu(jnp.einsum('td,thkd->thk', xb, u_tab[eidx]))
        return jnp.einsum('thk,thkd->td', (g * act).astype(xb.dtype), v_tab[eidx])

    out = lax.map(block, xt).reshape(nb * blk, D)[:ntok]
    return out.reshape(B, T, D)


def layer(x, attn_fn, conv_buf, C0, n0, m0, g1, w_in, b_gates, conv_w, conv_b, mh_g, w_out, g2,
          peer_wq, peer_keys, peer_u, peer_v):
    B, T, _ = x.shape
    f32 = jnp.float32
    xn = rmsnorm(x, g1)
    cuts = [int(c) for c in np.cumsum([A_WIDTH] * 3 + [M_WIDTH] * 4 + [M_HEADS])]
    aq, ak, av, mq, mk, mv, mo, mi, mf = jnp.split(xn @ w_in, cuts, axis=-1)
    att, k_rows, v_rows = attn_fn(aq.reshape(B, T, A_HEADS, A_DH), ak.reshape(B, T, A_HEADS, A_DH),
                                  av.reshape(B, T, A_HEADS, A_DH))
    qk, conv_new = causal_conv(jnp.concatenate([mq, mk], axis=-1), conv_buf, conv_w, conv_b)
    qk = jax.nn.silu(qk)
    q_m = qk[..., :M_WIDTH].reshape(B, T, M_HEADS, M_DH)
    k_m = qk[..., M_WIDTH:].reshape(B, T, M_HEADS, M_DH) * (M_DH ** -0.5)
    i_pre = mi.astype(f32) + b_gates[:M_HEADS].astype(f32)
    log_f = jax.nn.log_sigmoid(mf.astype(f32) + b_gates[M_HEADS:].astype(f32))
    h, C, n, m = mlstm(q_m, k_m, mv.reshape(B, T, M_HEADS, M_DH), i_pre, log_f, C0, n0, m0)
    h = h * lax.rsqrt(jnp.mean(h * h, axis=-1, keepdims=True) + EPS)
    h = (h.reshape(B, T, M_WIDTH) * mh_g.astype(f32) * jax.nn.sigmoid(mo.astype(f32))).astype(x.dtype)
    x = x + jnp.concatenate([att.astype(x.dtype), h], axis=-1) @ w_out
    x = x + peer(rmsnorm(x, g2), peer_wq, peer_keys, peer_u, peer_v)
    dt = x.dtype
    return x, (k_rows, v_rows, C.astype(dt), n.astype(dt), m.astype(dt), conv_new)


def stack_layers(per_layer):
    return [jnp.stack(t) for t in zip(*per_layer)]


def setup_inputs(seed: int = 0) -> dict:
    key = jax.random.key(seed)
    ks = jax.random.split(key, 24)

    def nrm(k, shape, s):
        return jax.random.normal(k, shape, jnp.float32) * s

    att_cache = min(WINDOW, PAST_LEN)
    fgate_bias = jnp.linspace(3.0, 6.0, M_HEADS, dtype=jnp.float32)
    return {
        'x_prompt': nrm(ks[0], (BATCH, SEQ, D_MODEL), 1.0),
        'x_sample': nrm(ks[1], (DEC_BATCH, DEC_SEQ, D_MODEL), 1.0),
        'cache_k': nrm(ks[2], (DEPTH, DEC_BATCH, att_cache, A_HEADS, A_DH), 1.0),
        'cache_v': nrm(ks[3], (DEPTH, DEC_BATCH, att_cache, A_HEADS, A_DH), 1.0),
        'state_C': nrm(ks[4], (DEPTH, DEC_BATCH, M_HEADS, M_DH, M_DH), 0.1),
        'state_n': nrm(ks[5], (DEPTH, DEC_BATCH, M_HEADS, M_DH), 0.1),
        'state_m': nrm(ks[6], (DEPTH, DEC_BATCH, M_HEADS), 1.0),
        'state_conv': nrm(ks[7], (DEPTH, DEC_BATCH, CONV_W - 1, 2 * M_WIDTH), 1.0),
        'norm1_g': 1.0 + nrm(ks[8], (DEPTH, D_MODEL), 0.01),
        'w_in': nrm(ks[9], (DEPTH, D_MODEL, IN_COLS), D_MODEL ** -0.5),
        'b_gates': jnp.concatenate([nrm(ks[10], (DEPTH, M_HEADS), 0.1),
                                    fgate_bias + nrm(ks[11], (DEPTH, M_HEADS), 0.1)], axis=-1),
        'rel_bias': nrm(ks[12], (DEPTH, A_HEADS, 2 * MAX_REL + 1), 0.1),
        'conv_w': nrm(ks[13], (DEPTH, CONV_W, 2 * M_WIDTH), CONV_W ** -0.5),
        'conv_b': nrm(ks[14], (DEPTH, 2 * M_WIDTH), 0.02),
        'mh_norm_g': 1.0 + nrm(ks[15], (DEPTH, M_WIDTH), 0.01),
        'w_out': nrm(ks[16], (DEPTH, MIX, D_MODEL), MIX ** -0.5),
        'norm2_g': 1.0 + nrm(ks[17], (DEPTH, D_MODEL), 0.01),
        'peer_wq': nrm(ks[18], (DEPTH, D_MODEL, P_HEADS * P_DKEY), D_MODEL ** -0.5),
        'peer_keys': nrm(ks[19], (DEPTH, P_HEADS, 2, N_KEYS, P_DKEY // 2), (P_DKEY // 2) ** -0.5),
        'peer_u': nrm(ks[20], (DEPTH, N_EXPERTS, D_MODEL), D_MODEL ** -0.5),
        'peer_v': nrm(ks[21], (DEPTH, N_EXPERTS, D_MODEL), P_HEADS ** -0.5),
        'final_g': 1.0 + nrm(ks[22], (D_MODEL,), 0.01),
    }


def reference(x_prompt, x_sample, cache_k, cache_v, state_C, state_n, state_m, state_conv,
              norm1_g, w_in, b_gates, rel_bias, conv_w, conv_b, mh_norm_g, w_out, norm2_g,
              peer_wq, peer_keys, peer_u, peer_v, final_g):
    f32 = jnp.float32
    B = x_prompt.shape[0]
    hp, hs = x_prompt, x_sample
    prompt_states, sample_states = [], []
    for l in range(DEPTH):
        params_l = (norm1_g[l], w_in[l], b_gates[l], conv_w[l], conv_b[l], mh_norm_g[l], w_out[l],
                    norm2_g[l], peer_wq[l], peer_keys[l], peer_u[l], peer_v[l])
        hp, st_p = layer(hp, functools.partial(attn_prompt, rel_bias=rel_bias[l]),
                         jnp.zeros((B, CONV_W - 1, 2 * M_WIDTH), hp.dtype),
                         jnp.zeros((B, M_HEADS, M_DH, M_DH), f32), jnp.zeros((B, M_HEADS, M_DH), f32),
                         jnp.zeros((B, M_HEADS), f32), *params_l)
        hs, st_s = layer(hs, functools.partial(attn_step, ck=cache_k[l], cv=cache_v[l], rel_bias=rel_bias[l]),
                         state_conv[l], state_C[l], state_n[l], state_m[l], *params_l)
        prompt_states.append(st_p)
        sample_states.append(st_s)
    y_prompt = rmsnorm(hp, final_g)
    y_sample = rmsnorm(hs, final_g)
    k_p, v_p, C_p, n_p, m_p, conv_p = stack_layers(prompt_states)
    k_s, v_s, C_s, n_s, m_s, conv_s = stack_layers(sample_states)
    return (y_prompt, y_sample, k_p, v_p, C_p, n_p, m_p, conv_p, k_s, v_s, C_s, n_s, m_s, conv_s)
```

```python
import functools
import math

import jax
import jax.numpy as jnp
from jax import lax
from jax.experimental import pallas as pl
from jax.experimental.pallas import tpu as pltpu

f32 = jnp.float32
bf16 = jnp.bfloat16
i32 = jnp.int32

D_MODEL = 1024
CHUNK = 64
A_HEADS = 8
A_DH = 64
A_WIDTH = A_HEADS * A_DH
BAND_CHUNKS = 8
WINDOW = BAND_CHUNKS * CHUNK
MAX_REL = 128
ATT_SCALE = A_DH ** -0.5
M_HEADS = 4
M_DH = 128
M_WIDTH = M_HEADS * M_DH
CONV_W = 4
P_HEADS = 8
P_DKEY = 256
N_KEYS = 128
P_TOPK = 16
P_SLOTS = P_HEADS * P_TOPK
EPS = 1e-6
NEG = -1e30

LANES = 128
SUBLANES = 8
ROW_BLOCK = 256
ATT_TILE = 512
ATT_SUB = 128
ATT_KEYS = ATT_SUB + WINDOW
ROW_TILES = D_MODEL // LANES
G_ROWS = P_SLOTS * ROW_TILES
TOKEN_UNROLL = 8
VMEM_LIMIT = 56 * 1024 * 1024


def _rms(x, g):
    return x * lax.rsqrt(jnp.mean(x * x, axis=-1, keepdims=True) + EPS) * g


def _split_bf16(x):
    hi = x.astype(bf16)
    lo = (x - hi.astype(f32)).astype(bf16)
    return hi, lo


def _inproj_kernel(x_ref, g_ref, w_ref, wgh_ref, wgl_ref,
                   aq_ref, ak_ref, av_ref, mqk_ref, mv_ref, mo_ref, gate_ref):
    xn = _rms(x_ref[...], g_ref[...])
    xh, xl = _split_bf16(xn)

    def proj(lo, hi):
        return jnp.dot(xh, w_ref[:, lo:hi], preferred_element_type=f32)

    aq_ref[...] = proj(0, 512)
    ak_ref[...] = proj(512, 1024)
    av_ref[...] = proj(1024, 1536)
    mqk_ref[...] = proj(1536, 2560)
    mv_ref[...] = proj(2560, 3072)
    mo_ref[...] = proj(3072, 3584)
    gate_ref[...] = (jnp.dot(xh, wgh_ref[...], preferred_element_type=f32)
                     + jnp.dot(xl, wgh_ref[...], preferred_element_type=f32)
                     + jnp.dot(xh, wgl_ref[...], preferred_element_type=f32))


def _inproj(x, g1, w_in):
    n = x.shape[0]
    main = 3 * A_WIDTH + 4 * M_WIDTH
    w_main = w_in[:, :main].astype(bf16)
    wg = jnp.pad(w_in[:, main:], ((0, 0), (0, LANES - 2 * M_HEADS)))
    wgh, wgl = _split_bf16(wg)
    widths = (512, 512, 512, 1024, 512, 512, LANES)
    row = lambda w: pl.BlockSpec((ROW_BLOCK, w), lambda i: (i, 0))
    full = lambda a: pl.BlockSpec(a.shape, lambda i: (0,) * a.ndim)
    g = g1.reshape(1, D_MODEL)
    return pl.pallas_call(
        _inproj_kernel,
        grid=(n // ROW_BLOCK,),
        in_specs=[row(D_MODEL), full(g), full(w_main), full(wgh), full(wgl)],
        out_specs=[row(w) for w in widths],
        out_shape=[jax.ShapeDtypeStruct((n, w), f32) for w in widths],
        compiler_params=pltpu.CompilerParams(dimension_semantics=("parallel",), vmem_limit_bytes=VMEM_LIMIT),
        name="inproj",
    )(x, g, w_main, wgh, wgl)


def _attn_heads(q, k, v, bias_ref, key_ok):
    outs = []
    for h in range(A_HEADS):
        sl = slice(h * A_DH, (h + 1) * A_DH)
        s = lax.dot_general(q[:, sl], k[:, sl], (((1,), (1,)), ((), ())), preferred_element_type=f32)
        s = s * ATT_SCALE + bias_ref[h]
        if key_ok is not None:
            s = jnp.where(key_ok, s, NEG)
        m = jnp.max(s, axis=-1, keepdims=True)
        p = jnp.exp(s - m)
        l = jnp.sum(p, axis=-1, keepdims=True)
        o = jnp.dot(p.astype(bf16), v[:, sl], preferred_element_type=f32)
        outs.append(o / l)
    return jnp.concatenate(outs, axis=-1)


def _attn_prompt_kernel(q_ref, k0_ref, k1_ref, v0_ref, v1_ref, bias_ref, o_ref):
    t = pl.program_id(1)
    q = q_ref[0].astype(bf16)
    k = jnp.concatenate([k0_ref[0], k1_ref[0]], axis=0).astype(bf16)
    v = jnp.concatenate([v0_ref[0], v1_ref[0]], axis=0).astype(bf16)
    col = lax.broadcasted_iota(i32, (1, ATT_KEYS), 1)
    for s in range(ATT_TILE // ATT_SUB):
        lo = s * ATT_SUB
        key_ok = (t * ATT_TILE + lo + col) >= WINDOW
        o_ref[0, lo:lo + ATT_SUB, :] = _attn_heads(q[lo:lo + ATT_SUB], k[lo:lo + ATT_KEYS], v[lo:lo + ATT_KEYS],
                                                    bias_ref, key_ok)


def _attn_sample_kernel(q_ref, k_ref, v_ref, bias_ref, o_ref):
    o_ref[0] = _attn_heads(q_ref[0].astype(bf16), k_ref[0].astype(bf16), v_ref[0].astype(bf16), bias_ref, None)


def _rel_bias_table(rel_bias, rows, cols, offset, valid):
    span = rows + cols - 1
    rel = offset + rows - 1 - jnp.arange(span)
    diag = rel_bias[:, jnp.clip(rel, -MAX_REL, MAX_REL) + MAX_REL].astype(f32)
    diag = jnp.pad(diag, ((0, 0), (0, 1)))
    flat = jnp.tile(diag, (1, rows))[:, rows - 1:rows - 1 + rows * span]
    return jnp.where(valid[None], flat.reshape(-1, rows, span)[:, :, :cols], NEG)


def _attn_prompt(q, k, v, rel_bias):
    bsz, s, _ = q.shape
    assert s % ATT_TILE == 0
    pad = ((0, 0), (WINDOW, 0), (0, 0))
    kp, vp = jnp.pad(k, pad), jnp.pad(v, pad)
    i = jnp.arange(ATT_SUB)[:, None]
    j = jnp.arange(ATT_KEYS)[None, :]
    off = j - (i // CHUNK) * CHUNK
    bias = _rel_bias_table(rel_bias, ATT_SUB, ATT_KEYS, WINDOW, (off >= 0) & (off < WINDOW + CHUNK))
    blk = lambda d: pl.BlockSpec((1, ATT_TILE, A_WIDTH), lambda b, t: (b, t + d, 0))
    return pl.pallas_call(
        _attn_prompt_kernel,
        grid=(bsz, s // ATT_TILE),
        in_specs=[blk(0), blk(0), blk(1), blk(0), blk(1),
                  pl.BlockSpec(bias.shape, lambda b, t: (0, 0, 0))],
        out_specs=blk(0),
        out_shape=jax.ShapeDtypeStruct((bsz, s, A_WIDTH), f32),
        compiler_params=pltpu.CompilerParams(dimension_semantics=("parallel", "parallel"),
                                             vmem_limit_bytes=VMEM_LIMIT),
        name="attn_prompt",
    )(q, kp, kp, vp, vp, bias)


def _attn_sample(q, k, v, ck, cv, rel_bias):
    bsz, t, _ = q.shape
    l = ck.shape[1]
    keys = -(-(l + t) // LANES) * LANES
    padk = ((0, 0), (0, keys - l - t), (0, 0))
    kk = jnp.pad(jnp.concatenate([ck, k], axis=1), padk)
    vv = jnp.pad(jnp.concatenate([cv, v], axis=1), padk)
    j = jnp.arange(keys)[None, :]
    bias = _rel_bias_table(rel_bias, t, keys, l, jnp.broadcast_to(j < l + t, (t, keys)))
    return pl.pallas_call(
        _attn_sample_kernel,
        grid=(bsz,),
        in_specs=[pl.BlockSpec((1, t, A_WIDTH), lambda b: (b, 0, 0)),
                  pl.BlockSpec((1, keys, A_WIDTH), lambda b: (b, 0, 0)),
                  pl.BlockSpec((1, keys, A_WIDTH), lambda b: (b, 0, 0)),
                  pl.BlockSpec(bias.shape, lambda b: (0, 0, 0))],
        out_specs=pl.BlockSpec((1, t, A_WIDTH), lambda b: (b, 0, 0)),
        out_shape=jax.ShapeDtypeStruct((bsz, t, A_WIDTH), f32),
        compiler_params=pltpu.CompilerParams(dimension_semantics=("parallel",), vmem_limit_bytes=VMEM_LIMIT),
        name="attn_sample",
    )(q, kk, vv, bias)


def _mlstm_kernel(qk_ref, v_ref, o_ref, gate_ref, c0_ref, n0_ref, m0_ref, cbuf_ref,
                  cw_ref, cb_ref, bg_ref, mhg_ref,
                  h_ref, cout_ref, nout_ref, mout_ref,
                  c_s, n_s, m_s, prev_s, *, lc):
    c = pl.program_id(1)

    @pl.when(c == 0)
    def _():
        c_s[...] = c0_ref[0]
        n_s[...] = n0_ref[0]
        m_s[...] = m0_ref[0]
        prev_s[...] = cbuf_ref[0]

    a = qk_ref[0]
    ext = jnp.concatenate([prev_s[...], a], axis=0)
    conv = cb_ref[...]
    for j in range(CONV_W):
        lo = SUBLANES - (CONV_W - 1) + j
        conv = conv + cw_ref[j:j + 1, :] * ext[lo:lo + lc]
    prev_s[...] = a[lc - SUBLANES:lc]
    qk = conv * jax.nn.sigmoid(conv)

    z = gate_ref[0] + bg_ref[...]
    lane = lax.broadcasted_iota(i32, (lc, LANES), 1)
    row = lax.broadcasted_iota(i32, (lc, LANES), 0)
    logf = jnp.minimum(z, 0.0) - jnp.log1p(jnp.exp(-jnp.abs(z)))
    cum = jnp.where((lane >= M_HEADS) & (lane < 2 * M_HEADS), logf, 0.0)
    shift = 1
    while shift < lc:
        cum = cum + jnp.where(row >= shift, pltpu.roll(cum, shift, axis=0), 0.0)
        shift *= 2
    zc = jnp.where(lane < M_HEADS, z, cum)
    zt = jnp.concatenate([zc, jnp.zeros((LANES - lc, LANES), f32)], axis=0).T[:, :lc]

    ri = lax.broadcasted_iota(i32, (lc, lc), 0)
    ci = lax.broadcasted_iota(i32, (lc, lc), 1)
    causal = ri >= ci
    vall = v_ref[0]
    hs = []
    for h in range(M_HEADS):
        sl = slice(h * M_DH, (h + 1) * M_DH)
        q = qk[:, sl]
        k = qk[:, M_WIDTH + h * M_DH:M_WIDTH + (h + 1) * M_DH] * (M_DH ** -0.5)
        v = vall[:, sl]
        i_col = zc[:, h:h + 1]
        b_col = zc[:, M_HEADS + h:M_HEADS + h + 1]
        i_row = zt[h:h + 1, :]
        b_row = zt[M_HEADS + h:M_HEADS + h + 1, :]
        m_prev = m_s[h:h + 1, 0:1]
        c_prev = c_s[h]
        n_prev = n_s[h:h + 1, :]

        dmat = jnp.where(causal, b_col - b_row + i_row, NEG)
        inter = b_col + m_prev
        mt = jnp.maximum(jnp.max(dmat, axis=-1, keepdims=True), inter)
        w_intra = jnp.exp(dmat - mt)
        w_inter = jnp.exp(inter - mt)
        qb, kb, vb = q.astype(bf16), k.astype(bf16), v.astype(bf16)
        s = lax.dot_general(qb, kb, (((1,), (1,)), ((), ())), preferred_element_type=f32) * w_intra
        num = (w_inter * jnp.dot(qb, c_prev.astype(bf16), preferred_element_type=f32)
               + jnp.dot(s.astype(bf16), vb, preferred_element_type=f32))
        den = w_inter * jnp.sum(q * n_prev, axis=-1, keepdims=True) + jnp.sum(s, axis=-1, keepdims=True)
        hh = num / jnp.maximum(jnp.abs(den), jnp.exp(-mt))
        m_new = mt[lc - 1:lc, :]
        b_last = b_col[lc - 1:lc, :]
        w_s = jnp.exp(b_last - b_col + i_col - m_new)
        decay = jnp.exp(b_last + m_prev - m_new)
        kw = k * w_s
        c_s[h] = decay * c_prev + lax.dot_general(kw.astype(bf16), vb, (((0,), (0,)), ((), ())),
                                                   preferred_element_type=f32)
        n_s[h:h + 1, :] = decay * n_prev + jnp.sum(kw, axis=0, keepdims=True)
        m_s[h:h + 1, :] = jnp.broadcast_to(m_new, (1, LANES))
        hs.append(hh * lax.rsqrt(jnp.mean(hh * hh, axis=-1, keepdims=True) + EPS))

    h_all = jnp.concatenate(hs, axis=-1)
    h_ref[0] = h_all * mhg_ref[...] * jax.nn.sigmoid(o_ref[0])

    @pl.when(c == pl.num_programs(1) - 1)
    def _():
        cout_ref[0] = c_s[...]
        nout_ref[0] = n_s[...]
        mout_ref[0] = m_s[...]


def _mlstm(mqk, mv, mo, gates, c0, n0, m0, cbuf, conv_w, conv_b, b_gates, mh_g):
    bsz, t, _ = mqk.shape
    lc = min(CHUNK, t)
    nc = t // lc
    assert t % lc == 0 and lc % SUBLANES == 0
    n0p = jnp.pad(n0.astype(f32), ((0, 0), (0, SUBLANES - M_HEADS), (0, 0)))
    m0p = jnp.pad(jnp.broadcast_to(m0.astype(f32)[:, :, None], (bsz, M_HEADS, LANES)),
                  ((0, 0), (0, SUBLANES - M_HEADS), (0, 0)))
    cbp = jnp.pad(cbuf.astype(f32), ((0, 0), (SUBLANES - (CONV_W - 1), 0), (0, 0)))
    bg = jnp.pad(b_gates.astype(f32), (0, LANES - 2 * M_HEADS)).reshape(1, LANES)
    seq = lambda w: pl.BlockSpec((1, lc, w), lambda b, c: (b, c, 0))
    per_b = lambda shp: pl.BlockSpec((1,) + shp, lambda b, c: (b,) + (0,) * len(shp))
    full = lambda a: pl.BlockSpec(a.shape, lambda b, c: (0,) * a.ndim)
    cb = conv_b.reshape(1, -1)
    mhg = mh_g.reshape(1, -1)
    return pl.pallas_call(
        functools.partial(_mlstm_kernel, lc=lc),
        grid=(bsz, nc),
        in_specs=[seq(2 * M_WIDTH), seq(M_WIDTH), seq(M_WIDTH), seq(LANES),
                  per_b((M_HEADS, M_DH, M_DH)), per_b((SUBLANES, M_DH)), per_b((SUBLANES, LANES)),
                  per_b((SUBLANES, 2 * M_WIDTH)),
                  full(conv_w), full(cb), full(bg), full(mhg)],
        out_specs=[seq(M_WIDTH), per_b((M_HEADS, M_DH, M_DH)), per_b((SUBLANES, M_DH)), per_b((SUBLANES, LANES))],
        out_shape=[jax.ShapeDtypeStruct((bsz, t, M_WIDTH), f32),
                   jax.ShapeDtypeStruct((bsz, M_HEADS, M_DH, M_DH), f32),
                   jax.ShapeDtypeStruct((bsz, SUBLANES, M_DH), f32),
                   jax.ShapeDtypeStruct((bsz, SUBLANES, LANES), f32)],
        scratch_shapes=[pltpu.VMEM((M_HEADS, M_DH, M_DH), f32), pltpu.VMEM((SUBLANES, M_DH), f32),
                        pltpu.VMEM((SUBLANES, LANES), f32), pltpu.VMEM((SUBLANES, 2 * M_WIDTH), f32)],
        compiler_params=pltpu.CompilerParams(dimension_semantics=("parallel", "arbitrary"),
                                             vmem_limit_bytes=VMEM_LIMIT),
        name="mlstm",
    )(mqk, mv, mo, gates, c0.astype(f32), n0p, m0p, cbp, conv_w, cb, bg, mhg)


def _topk_rows(s, k, payload=None):
    n = s.shape[0]
    rows = lax.broadcasted_iota(i32, s.shape, 0).astype(f32)
    vals, ids = [], []
    for _ in range(k):
        m = jnp.max(s, axis=0, keepdims=True)
        pos = jnp.min(jnp.where(s == m, rows, float(n)), axis=0, keepdims=True)
        sel = rows == pos
        vals.append(m)
        ids.append(pos if payload is None else jnp.max(jnp.where(sel, payload, -1.0), axis=0, keepdims=True))
        s = jnp.where(sel, -jnp.inf, s)
    return jnp.concatenate(vals, axis=0), jnp.concatenate(ids, axis=0)


def _mid_kernel(x_ref, att_ref, h_ref, wo_ref, g2_ref, wq_ref, keys_ref,
                x1_ref, xn_ref, eidx_ref, gate_ref):
    cat = jnp.concatenate([att_ref[...], h_ref[...]], axis=-1).astype(bf16)
    x1 = x_ref[...] + jnp.dot(cat, wo_ref[...], preferred_element_type=f32)
    x1_ref[...] = x1
    xn = _rms(x1, g2_ref[...])
    xn_ref[...] = xn
    xb = xn.astype(bf16)
    e_rows, g_rows = [], []
    for h in range(P_HEADS):
        q = jnp.dot(xb, wq_ref[:, h * P_DKEY:(h + 1) * P_DKEY], preferred_element_type=f32).astype(bf16)
        half = []
        for p in range(2):
            st = lax.dot_general(keys_ref[h, p], q[:, p * N_KEYS:(p + 1) * N_KEYS], (((1,), (1,)), ((), ())),
                                 preferred_element_type=f32)
            half.append(_topk_rows(st, P_TOPK))
        (v0, i0), (v1, i1) = half
        cand = jnp.concatenate([v0[a:a + 1] + v1 for a in range(P_TOPK)], axis=0)
        cidx = jnp.concatenate([i0[a:a + 1] * float(N_KEYS) + i1 for a in range(P_TOPK)], axis=0)
        top_s, eid = _topk_rows(cand, P_TOPK, payload=cidx)
        ex = jnp.exp(top_s - top_s[0:1])
        e_rows.append(eid)
        g_rows.append(ex / jnp.sum(ex, axis=0, keepdims=True))
    eidx_ref[...] = jnp.concatenate(e_rows, axis=0).T.astype(i32)
    gate_ref[...] = jnp.concatenate(g_rows, axis=0).T


def _mid(x, att, h, w_out, g2, wq, keys):
    n = x.shape[0]
    row = lambda w: pl.BlockSpec((ROW_BLOCK, w), lambda i: (i, 0))
    full = lambda a: pl.BlockSpec(a.shape, lambda i: (0,) * a.ndim)
    wo = w_out.astype(bf16)
    wqb = wq.astype(bf16)
    kb = keys.astype(bf16)
    g = g2.reshape(1, D_MODEL)
    return pl.pallas_call(
        _mid_kernel,
        grid=(n // ROW_BLOCK,),
        in_specs=[row(D_MODEL), row(A_WIDTH), row(M_WIDTH), full(wo), full(g), full(wqb), full(kb)],
        out_specs=[row(D_MODEL), row(D_MODEL), row(P_SLOTS), row(P_SLOTS)],
        out_shape=[jax.ShapeDtypeStruct((n, D_MODEL), f32), jax.ShapeDtypeStruct((n, D_MODEL), f32),
                   jax.ShapeDtypeStruct((n, P_SLOTS), i32), jax.ShapeDtypeStruct((n, P_SLOTS), f32)],
        compiler_params=pltpu.CompilerParams(dimension_semantics=("parallel",), vmem_limit_bytes=VMEM_LIMIT),
        name="outproj_retrieve",
    )(x, att, h, wo, g, wqb, kb)


def _gather_rows(eidx_ref, tab_ref, g_ref, t):
    for r in range(P_SLOTS):
        g_ref[pl.ds(r * ROW_TILES, ROW_TILES), :] = tab_ref[eidx_ref[t, r]]


def _tile_rows(t, rows=SUBLANES):
    return pl.ds(pl.multiple_of(t * rows, rows), rows)


def _pipelined_tokens(nt, eidx_ref, tab_ref, g0_s, g1_s, compute):
    bufs = (g0_s, g1_s)
    _gather_rows(eidx_ref, tab_ref, g0_s, 0)

    def body(j, carry):
        for u in range(TOKEN_UNROLL):
            t = TOKEN_UNROLL * j + u
            _gather_rows(eidx_ref, tab_ref, bufs[(u + 1) % 2], jnp.minimum(t + 1, nt - 1))
            compute(t, bufs[u % 2])
        return carry

    lax.fori_loop(0, nt // TOKEN_UNROLL, body, 0)


def _gelu_tanh(x):
    return 0.5 * x * (1.0 + jnp.tanh(math.sqrt(2.0 / math.pi) * (x + 0.044715 * (x * x * x))))


def _peer_u_kernel(eidx_ref, xn_ref, g_ref, tab_ref, w_ref, xl_s, r_s, g0_s, g1_s):
    nt = xn_ref.shape[0]
    xn = xn_ref[...]
    xh = xn.astype(bf16).astype(f32)
    xl = xn - xh
    for k in range(ROW_TILES):
        xl_s[pl.ds(k, nt, stride=2 * SUBLANES), :] = xh[:, k * LANES:(k + 1) * LANES]
        xl_s[pl.ds(SUBLANES + k, nt, stride=2 * SUBLANES), :] = xl[:, k * LANES:(k + 1) * LANES]
    diag = (lax.broadcasted_iota(i32, (SUBLANES, G_ROWS), 1) % ROW_TILES
            == lax.broadcasted_iota(i32, (SUBLANES, G_ROWS), 0))

    def compute(t, g_s):
        lhs = xl_s[_tile_rows(t, 2 * SUBLANES), :].astype(bf16)
        out = lax.dot_general(lhs, g_s[...], (((1,), (1,)), ((), ())), preferred_element_type=f32)
        part = jnp.where(diag, out[:SUBLANES] + out[SUBLANES:], 0.0)
        for c in range(ROW_TILES):
            r_s[c, _tile_rows(t), :] = part[:, c * LANES:(c + 1) * LANES]

    _pipelined_tokens(nt, eidx_ref, tab_ref, g0_s, g1_s, compute)
    cols = []
    for c in range(ROW_TILES):
        acc = r_s[c, pl.ds(0, nt, stride=SUBLANES), :]
        for k in range(1, SUBLANES):
            acc = acc + r_s[c, pl.ds(k, nt, stride=SUBLANES), :]
        cols.append(acc)
    s = jnp.concatenate(cols, axis=-1)
    fold = (lax.broadcasted_iota(i32, (G_ROWS, P_SLOTS), 0) // ROW_TILES
            == lax.broadcasted_iota(i32, (G_ROWS, P_SLOTS), 1)).astype(bf16)
    sh, sl = _split_bf16(s)
    act = jnp.dot(sh, fold, preferred_element_type=f32) + jnp.dot(sl, fold, preferred_element_type=f32)
    w_ref[...] = g_ref[...] * _gelu_tanh(act)


def _peer_v_kernel(eidx_ref, w_ref, x1_ref, gf_ref, tab_ref, y_ref, wl_s, o_s, g0_s, g1_s):
    nt = w_ref.shape[0]
    spread = (lax.broadcasted_iota(i32, (P_SLOTS, G_ROWS), 1) // ROW_TILES
              == lax.broadcasted_iota(i32, (P_SLOTS, G_ROWS), 0)).astype(bf16)
    wexp = jnp.dot(w_ref[...].astype(bf16), spread, preferred_element_type=f32)
    lane = lax.broadcasted_iota(i32, (nt, LANES), 1)
    for c in range(ROW_TILES):
        wc = wexp[:, c * LANES:(c + 1) * LANES]
        for k in range(SUBLANES):
            wl_s[c, pl.ds(k, nt, stride=SUBLANES), :] = jnp.where(lane % ROW_TILES == k, wc, 0.0)

    def compute(t, g_s):
        lhs = jnp.concatenate([wl_s[c, _tile_rows(t), :] for c in range(ROW_TILES)], axis=-1).astype(bf16)
        o_s[_tile_rows(t), :] = jnp.dot(lhs, g_s[...], preferred_element_type=f32)

    _pipelined_tokens(nt, eidx_ref, tab_ref, g0_s, g1_s, compute)
    peer = jnp.concatenate([o_s[pl.ds(k, nt, stride=SUBLANES), :] for k in range(ROW_TILES)], axis=-1)
    y_ref[...] = _rms(x1_ref[...] + peer, gf_ref[...])


def _peer_specs():
    row = lambda w: pl.BlockSpec((ROW_BLOCK, w), lambda i: (i, 0))
    idx = pl.BlockSpec((ROW_BLOCK, P_SLOTS), lambda i: (i, 0), memory_space=pltpu.SMEM)
    tab = pl.BlockSpec(memory_space=pltpu.VMEM)
    gscr = pltpu.VMEM((G_ROWS, LANES), bf16)
    params = pltpu.CompilerParams(dimension_semantics=("arbitrary",), vmem_limit_bytes=VMEM_LIMIT)
    return row, idx, tab, gscr, params


def _expert_table(tab):
    return tab.astype(bf16).reshape(tab.shape[0], ROW_TILES, LANES)


def _peer_u(eidx, xn, g, utab):
    n = xn.shape[0]
    row, idx, tab, gscr, params = _peer_specs()
    return pl.pallas_call(
        _peer_u_kernel,
        grid=(n // ROW_BLOCK,),
        in_specs=[idx, row(D_MODEL), row(P_SLOTS), tab],
        out_specs=row(P_SLOTS),
        out_shape=jax.ShapeDtypeStruct((n, P_SLOTS), f32),
        scratch_shapes=[pltpu.VMEM((ROW_BLOCK * 2 * SUBLANES, LANES), f32),
                        pltpu.VMEM((ROW_TILES, ROW_BLOCK * SUBLANES, LANES), f32), gscr, gscr],
        compiler_params=params,
        name="peer_u",
    )(eidx, xn, g, utab)


def _peer_v(eidx, w, x1, gf, vtab):
    n = x1.shape[0]
    row, idx, tab, gscr, params = _peer_specs()
    g = gf.reshape(1, D_MODEL)
    return pl.pallas_call(
        _peer_v_kernel,
        grid=(n // ROW_BLOCK,),
        in_specs=[idx, row(P_SLOTS), row(D_MODEL), pl.BlockSpec(g.shape, lambda i: (0, 0)), tab],
        out_specs=row(D_MODEL),
        out_shape=jax.ShapeDtypeStruct((n, D_MODEL), f32),
        scratch_shapes=[pltpu.VMEM((ROW_TILES, ROW_BLOCK * SUBLANES, LANES), f32),
                        pltpu.VMEM((ROW_BLOCK * SUBLANES, LANES), f32), gscr, gscr],
        compiler_params=params,
        name="peer_v",
    )(eidx, w, x1, g, vtab)


def kernel(x_prompt, x_sample, cache_k, cache_v, state_C, state_n, state_m, state_conv, norm1_g, w_in, b_gates, rel_bias, conv_w, conv_b, mh_norm_g, w_out, norm2_g, peer_wq, peer_keys, peer_u, peer_v, final_g):
    bp, sp, d = x_prompt.shape
    bs, ts, _ = x_sample.shape
    n_p, n_s = bp * sp, bs * ts
    n = n_p + n_s
    assert n % ROW_BLOCK == 0 and d == D_MODEL
    depth = w_in.shape[0]
    assert depth == 1, "the final norm is fused into the last layer's PEER pass"
    l = 0
    x = jnp.concatenate([x_prompt.reshape(n_p, d), x_sample.reshape(n_s, d)], axis=0)

    aq, ak, av, mqk, mv, mo, gates = _inproj(x, norm1_g[l], w_in[l])
    split = lambda a: (a[:n_p].reshape(bp, sp, -1), a[n_p:].reshape(bs, ts, -1))
    (aq_p, aq_s), (ak_p, ak_s), (av_p, av_s) = split(aq), split(ak), split(av)
    (mqk_p, mqk_s), (mv_p, mv_s), (mo_p, mo_s), (gt_p, gt_s) = split(mqk), split(mv), split(mo), split(gates)

    att_p = _attn_prompt(aq_p, ak_p, av_p, rel_bias[l])
    lcache = cache_k.shape[2]
    att_s = _attn_sample(aq_s, ak_s, av_s, cache_k[l].reshape(bs, lcache, A_WIDTH),
                         cache_v[l].reshape(bs, lcache, A_WIDTH), rel_bias[l])

    zeros = lambda *shp: jnp.zeros(shp, f32)
    h_p, c_p, nn_p, mm_p = _mlstm(mqk_p, mv_p, mo_p, gt_p, zeros(bp, M_HEADS, M_DH, M_DH), zeros(bp, M_HEADS, M_DH),
                                  zeros(bp, M_HEADS), zeros(bp, CONV_W - 1, 2 * M_WIDTH),
                                  conv_w[l], conv_b[l], b_gates[l], mh_norm_g[l])
    h_s, c_s, nn_s, mm_s = _mlstm(mqk_s, mv_s, mo_s, gt_s, state_C[l], state_n[l], state_m[l], state_conv[l],
                                  conv_w[l], conv_b[l], b_gates[l], mh_norm_g[l])

    att = jnp.concatenate([att_p.reshape(n_p, A_WIDTH), att_s.reshape(n_s, A_WIDTH)], axis=0)
    hm = jnp.concatenate([h_p.reshape(n_p, M_WIDTH), h_s.reshape(n_s, M_WIDTH)], axis=0)
    x1, xn2, eidx, gate = _mid(x, att, hm, w_out[l], norm2_g[l], peer_wq[l], peer_keys[l])
    w = _peer_u(eidx, xn2, gate, _expert_table(peer_u[l]))
    y = _peer_v(eidx, w, x1, final_g, _expert_table(peer_v[l]))

    keep = min(WINDOW, sp)
    heads = lambda a: a.reshape(a.shape[0], a.shape[1], A_HEADS, A_DH)
    conv_tail = lambda buf, a: jnp.concatenate([buf.astype(a.dtype), a], axis=1)[:, -(CONV_W - 1):]
    st = lambda a: a[None]
    return (y[:n_p].reshape(bp, sp, d), y[n_p:].reshape(bs, ts, d),
            st(heads(ak_p[:, sp - keep:])), st(heads(av_p[:, sp - keep:])),
            st(c_p), st(nn_p[:, :M_HEADS]), st(mm_p[:, :M_HEADS, 0]),
            st(conv_tail(zeros(bp, CONV_W - 1, 2 * M_WIDTH), mqk_p)),
            st(heads(ak_s)), st(heads(av_s)),
            st(c_s), st(nn_s[:, :M_HEADS]), st(mm_s[:, :M_HEADS, 0]),
            st(conv_tail(state_conv[l], mqk_s)))
```

```python
import functools
import math

import jax
import jax.numpy as jnp
from jax import lax
from jax.experimental import pallas as pl
from jax.experimental.pallas import tpu as pltpu

f32 = jnp.float32
bf16 = jnp.bfloat16
i32 = jnp.int32

D_MODEL = 1024
CHUNK = 64
A_HEADS = 8
A_DH = 64
A_WIDTH = A_HEADS * A_DH
BAND_CHUNKS = 8
WINDOW = BAND_CHUNKS * CHUNK
MAX_REL = 128
ATT_SCALE = A_DH ** -0.5
M_HEADS = 4
M_DH = 128
M_WIDTH = M_HEADS * M_DH
CONV_W = 4
P_HEADS = 8
P_DKEY = 256
N_KEYS = 128
P_TOPK = 16
P_SLOTS = P_HEADS * P_TOPK
EPS = 1e-6
NEG = -1e30

LANES = 128
SUBLANES = 8
ROW_BLOCK = 256
ATT_TILE = 512
ATT_SUB = 128
ATT_KEYS = ATT_SUB + WINDOW
ROW_TILES = D_MODEL // LANES
G_ROWS = P_SLOTS * ROW_TILES
TOKEN_UNROLL = 8
VMEM_LIMIT = 56 * 1024 * 1024


def _rms(x, g):
    return x * lax.rsqrt(jnp.mean(x * x, axis=-1, keepdims=True) + EPS) * g


def _split_bf16(x):
    hi = x.astype(bf16)
    lo = (x - hi.astype(f32)).astype(bf16)
    return hi, lo


def _inproj_kernel(x_ref, g_ref, w_ref, wgh_ref, wgl_ref,
                   aq_ref, ak_ref, av_ref, mqk_ref, mv_ref, mo_ref, gate_ref):
    xn = _rms(x_ref[...], g_ref[...])
    xh, xl = _split_bf16(xn)

    def proj(lo, hi):
        return jnp.dot(xh, w_ref[:, lo:hi], preferred_element_type=f32)

    aq_ref[...] = proj(0, 512)
    ak_ref[...] = proj(512, 1024)
    av_ref[...] = proj(1024, 1536)
    mqk_ref[...] = proj(1536, 2560)
    mv_ref[...] = proj(2560, 3072)
    mo_ref[...] = proj(3072, 3584)
    gate_ref[...] = (jnp.dot(xh, wgh_ref[...], preferred_element_type=f32)
                     + jnp.dot(xl, wgh_ref[...], preferred_element_type=f32)
                     + jnp.dot(xh, wgl_ref[...], preferred_element_type=f32))


def _inproj(x, g1, w_in):
    n = x.shape[0]
    main = 3 * A_WIDTH + 4 * M_WIDTH
    w_main = w_in[:, :main].astype(bf16)
    wg = jnp.pad(w_in[:, main:], ((0, 0), (0, LANES - 2 * M_HEADS)))
    wgh, wgl = _split_bf16(wg)
    widths = (512, 512, 512, 1024, 512, 512, LANES)
    row = lambda w: pl.BlockSpec((ROW_BLOCK, w), lambda i: (i, 0))
    full = lambda a: pl.BlockSpec(a.shape, lambda i: (0,) * a.ndim)
    g = g1.reshape(1, D_MODEL)
    return pl.pallas_call(
        _inproj_kernel,
        grid=(n // ROW_BLOCK,),
        in_specs=[row(D_MODEL), full(g), full(w_main), full(wgh), full(wgl)],
        out_specs=[row(w) for w in widths],
        out_shape=[jax.ShapeDtypeStruct((n, w), f32) for w in widths],
        compiler_params=pltpu.CompilerParams(dimension_semantics=("parallel",), vmem_limit_bytes=VMEM_LIMIT),
        name="inproj",
    )(x, g, w_main, wgh, wgl)


def _attn_heads(q, k, v, bias_ref, key_ok):
    outs = []
    for h in range(A_HEADS):
        sl = slice(h * A_DH, (h + 1) * A_DH)
        s = lax.dot_general(q[:, sl], k[:, sl], (((1,), (1,)), ((), ())), preferred_element_type=f32)
        s = s * ATT_SCALE + bias_ref[h]
        if key_ok is not None:
            s = jnp.where(key_ok, s, NEG)
        m = jnp.max(s, axis=-1, keepdims=True)
        p = jnp.exp(s - m)
        l = jnp.sum(p, axis=-1, keepdims=True)
        o = jnp.dot(p.astype(bf16), v[:, sl], preferred_element_type=f32)
        outs.append(o / l)
    return jnp.concatenate(outs, axis=-1)


def _attn_prompt_kernel(q_ref, k0_ref, k1_ref, v0_ref, v1_ref, bias_ref, o_ref):
    t = pl.program_id(1)
    q = q_ref[...].astype(bf16)
    k = jnp.concatenate([k0_ref[...], k1_ref[...]], axis=0).astype(bf16)
    v = jnp.concatenate([v0_ref[...], v1_ref[...]], axis=0).astype(bf16)
    col = lax.broadcasted_iota(i32, (1, ATT_KEYS), 1)
    for s in range(ATT_TILE // ATT_SUB):
        lo = s * ATT_SUB
        key_ok = (t * ATT_TILE + lo + col) >= WINDOW
        o_ref[lo:lo + ATT_SUB, :] = _attn_heads(q[lo:lo + ATT_SUB], k[lo:lo + ATT_KEYS], v[lo:lo + ATT_KEYS],
                                                 bias_ref, key_ok)


def _attn_sample_kernel(q_ref, k_ref, v_ref, bias_ref, o_ref):
    o_ref[...] = _attn_heads(q_ref[...].astype(bf16), k_ref[0].astype(bf16), v_ref[0].astype(bf16), bias_ref, None)


def _rel_bias_table(rel_bias, rows, cols, offset, valid):
    span = rows + cols - 1
    rel = offset + rows - 1 - jnp.arange(span)
    diag = rel_bias[:, jnp.clip(rel, -MAX_REL, MAX_REL) + MAX_REL].astype(f32)
    diag = jnp.pad(diag, ((0, 0), (0, 1)))
    flat = jnp.tile(diag, (1, rows))[:, rows - 1:rows - 1 + rows * span]
    return jnp.where(valid[None], flat.reshape(-1, rows, span)[:, :, :cols], NEG)


def _attn_prompt(q, k, v, rel_bias, bsz, s):
    assert s % ATT_TILE == 0 and WINDOW == ATT_TILE
    nt = s // ATT_TILE
    i = jnp.arange(ATT_SUB)[:, None]
    j = jnp.arange(ATT_KEYS)[None, :]
    off = j - (i // CHUNK) * CHUNK
    bias = _rel_bias_table(rel_bias, ATT_SUB, ATT_KEYS, WINDOW, (off >= 0) & (off < WINDOW + CHUNK))
    cur = pl.BlockSpec((ATT_TILE, A_WIDTH), lambda b, t: (b * nt + t, 0))
    prev = pl.BlockSpec((ATT_TILE, A_WIDTH), lambda b, t: (b * nt + jnp.maximum(t - 1, 0), 0))
    return pl.pallas_call(
        _attn_prompt_kernel,
        grid=(bsz, nt),
        in_specs=[cur, prev, cur, prev, cur, pl.BlockSpec(bias.shape, lambda b, t: (0, 0, 0))],
        out_specs=cur,
        out_shape=jax.ShapeDtypeStruct((bsz * s, A_WIDTH), f32),
        compiler_params=pltpu.CompilerParams(dimension_semantics=("parallel", "parallel"),
                                             vmem_limit_bytes=VMEM_LIMIT),
        name="attn_prompt",
    )(q, k, k, v, v, bias)


def _attn_sample(q, k, v, ck, cv, rel_bias, row0, bsz, t):
    l = ck.shape[1]
    assert row0 % t == 0
    keys = -(-(l + t) // LANES) * LANES
    padk = ((0, 0), (0, keys - l - t), (0, 0))
    kk = jnp.pad(jnp.concatenate([ck, k[row0:].reshape(bsz, t, A_WIDTH)], axis=1), padk)
    vv = jnp.pad(jnp.concatenate([cv, v[row0:].reshape(bsz, t, A_WIDTH)], axis=1), padk)
    j = jnp.arange(keys)[None, :]
    bias = _rel_bias_table(rel_bias, t, keys, l, jnp.broadcast_to(j < l + t, (t, keys)))
    return pl.pallas_call(
        _attn_sample_kernel,
        grid=(bsz,),
        in_specs=[pl.BlockSpec((t, A_WIDTH), lambda b: (row0 // t + b, 0)),
                  pl.BlockSpec((1, keys, A_WIDTH), lambda b: (b, 0, 0)),
                  pl.BlockSpec((1, keys, A_WIDTH), lambda b: (b, 0, 0)),
                  pl.BlockSpec(bias.shape, lambda b: (0, 0, 0))],
        out_specs=pl.BlockSpec((t, A_WIDTH), lambda b: (b, 0)),
        out_shape=jax.ShapeDtypeStruct((bsz * t, A_WIDTH), f32),
        compiler_params=pltpu.CompilerParams(dimension_semantics=("parallel",), vmem_limit_bytes=VMEM_LIMIT),
        name="attn_sample",
    )(q, kk, vv, bias)


def _mlstm_kernel(qk_ref, v_ref, o_ref, gate_ref, c0_ref, n0_ref, m0_ref, cbuf_ref,
                  cw_ref, cb_ref, bg_ref, mhg_ref,
                  h_ref, cout_ref, nout_ref, mout_ref,
                  c_s, n_s, m_s, prev_s, *, lc):
    c = pl.program_id(1)

    @pl.when(c == 0)
    def _():
        c_s[...] = c0_ref[0]
        n_s[...] = n0_ref[0]
        m_s[...] = m0_ref[0]
        prev_s[...] = cbuf_ref[0]

    a = qk_ref[...]
    ext = jnp.concatenate([prev_s[...], a], axis=0)
    conv = cb_ref[...]
    for j in range(CONV_W):
        lo = SUBLANES - (CONV_W - 1) + j
        conv = conv + cw_ref[j:j + 1, :] * ext[lo:lo + lc]
    prev_s[...] = a[lc - SUBLANES:lc]
    qk = conv * jax.nn.sigmoid(conv)

    z = gate_ref[...] + bg_ref[...]
    lane = lax.broadcasted_iota(i32, (lc, LANES), 1)
    row = lax.broadcasted_iota(i32, (lc, LANES), 0)
    logf = jnp.minimum(z, 0.0) - jnp.log1p(jnp.exp(-jnp.abs(z)))
    cum = jnp.where((lane >= M_HEADS) & (lane < 2 * M_HEADS), logf, 0.0)
    shift = 1
    while shift < lc:
        cum = cum + jnp.where(row >= shift, pltpu.roll(cum, shift, axis=0), 0.0)
        shift *= 2
    zc = jnp.where(lane < M_HEADS, z, cum)
    zt = jnp.concatenate([zc, jnp.zeros((LANES - lc, LANES), f32)], axis=0).T[:, :lc]

    ri = lax.broadcasted_iota(i32, (lc, lc), 0)
    ci = lax.broadcasted_iota(i32, (lc, lc), 1)
    causal = ri >= ci
    vall = v_ref[...]
    hs = []
    for h in range(M_HEADS):
        sl = slice(h * M_DH, (h + 1) * M_DH)
        q = qk[:, sl]
        k = qk[:, M_WIDTH + h * M_DH:M_WIDTH + (h + 1) * M_DH] * (M_DH ** -0.5)
        v = vall[:, sl]
        i_col = zc[:, h:h + 1]
        b_col = zc[:, M_HEADS + h:M_HEADS + h + 1]
        i_row = zt[h:h + 1, :]
        b_row = zt[M_HEADS + h:M_HEADS + h + 1, :]
        m_prev = m_s[h:h + 1, 0:1]
        c_prev = c_s[h]
        n_prev = n_s[h:h + 1, :]

        dmat = jnp.where(causal, b_col - b_row + i_row, NEG)
        inter = b_col + m_prev
        mt = jnp.maximum(jnp.max(dmat, axis=-1, keepdims=True), inter)
        w_intra = jnp.exp(dmat - mt)
        w_inter = jnp.exp(inter - mt)
        qb, kb, vb = q.astype(bf16), k.astype(bf16), v.astype(bf16)
        s = lax.dot_general(qb, kb, (((1,), (1,)), ((), ())), preferred_element_type=f32) * w_intra
        num = (w_inter * jnp.dot(qb, c_prev.astype(bf16), preferred_element_type=f32)
               + jnp.dot(s.astype(bf16), vb, preferred_element_type=f32))
        den = w_inter * jnp.sum(q * n_prev, axis=-1, keepdims=True) + jnp.sum(s, axis=-1, keepdims=True)
        hh = num / jnp.maximum(jnp.abs(den), jnp.exp(-mt))
        m_new = mt[lc - 1:lc, :]
        b_last = b_col[lc - 1:lc, :]
        w_s = jnp.exp(b_last - b_col + i_col - m_new)
        decay = jnp.exp(b_last + m_prev - m_new)
        kw = k * w_s
        c_s[h] = decay * c_prev + lax.dot_general(kw.astype(bf16), vb, (((0,), (0,)), ((), ())),
                                                   preferred_element_type=f32)
        n_s[h:h + 1, :] = decay * n_prev + jnp.sum(kw, axis=0, keepdims=True)
        m_s[h:h + 1, :] = jnp.broadcast_to(m_new, (1, LANES))
        hs.append(hh * lax.rsqrt(jnp.mean(hh * hh, axis=-1, keepdims=True) + EPS))

    h_all = jnp.concatenate(hs, axis=-1)
    h_ref[...] = h_all * mhg_ref[...] * jax.nn.sigmoid(o_ref[...])

    @pl.when(c == pl.num_programs(1) - 1)
    def _():
        cout_ref[0] = c_s[...]
        nout_ref[0] = n_s[...]
        mout_ref[0] = m_s[...]


def _mlstm(mqk, mv, mo, gates, row0, bsz, t, c0, n0, m0, cbuf, conv_w, conv_b, b_gates, mh_g):
    lc = min(CHUNK, t)
    nc = t // lc
    assert t % lc == 0 and lc % SUBLANES == 0 and row0 % lc == 0
    n0p = jnp.pad(n0.astype(f32), ((0, 0), (0, SUBLANES - M_HEADS), (0, 0)))
    m0p = jnp.pad(jnp.broadcast_to(m0.astype(f32)[:, :, None], (bsz, M_HEADS, LANES)),
                  ((0, 0), (0, SUBLANES - M_HEADS), (0, 0)))
    cbp = jnp.pad(cbuf.astype(f32), ((0, 0), (SUBLANES - (CONV_W - 1), 0), (0, 0)))
    bg = jnp.pad(b_gates.astype(f32), (0, LANES - 2 * M_HEADS)).reshape(1, LANES)
    seq = lambda w: pl.BlockSpec((lc, w), lambda b, c: (row0 // lc + b * nc + c, 0))
    out_seq = pl.BlockSpec((lc, M_WIDTH), lambda b, c: (b * nc + c, 0))
    per_b = lambda shp: pl.BlockSpec((1,) + shp, lambda b, c: (b,) + (0,) * len(shp))
    full = lambda a: pl.BlockSpec(a.shape, lambda b, c: (0,) * a.ndim)
    cb = conv_b.reshape(1, -1)
    mhg = mh_g.reshape(1, -1)
    return pl.pallas_call(
        functools.partial(_mlstm_kernel, lc=lc),
        grid=(bsz, nc),
        in_specs=[seq(2 * M_WIDTH), seq(M_WIDTH), seq(M_WIDTH), seq(LANES),
                  per_b((M_HEADS, M_DH, M_DH)), per_b((SUBLANES, M_DH)), per_b((SUBLANES, LANES)),
                  per_b((SUBLANES, 2 * M_WIDTH)),
                  full(conv_w), full(cb), full(bg), full(mhg)],
        out_specs=[out_seq, per_b((M_HEADS, M_DH, M_DH)), per_b((SUBLANES, M_DH)), per_b((SUBLANES, LANES))],
        out_shape=[jax.ShapeDtypeStruct((bsz * t, M_WIDTH), f32),
                   jax.ShapeDtypeStruct((bsz, M_HEADS, M_DH, M_DH), f32),
                   jax.ShapeDtypeStruct((bsz, SUBLANES, M_DH), f32),
                   jax.ShapeDtypeStruct((bsz, SUBLANES, LANES), f32)],
        scratch_shapes=[pltpu.VMEM((M_HEADS, M_DH, M_DH), f32), pltpu.VMEM((SUBLANES, M_DH), f32),
                        pltpu.VMEM((SUBLANES, LANES), f32), pltpu.VMEM((SUBLANES, 2 * M_WIDTH), f32)],
        compiler_params=pltpu.CompilerParams(dimension_semantics=("parallel", "arbitrary"),
                                             vmem_limit_bytes=VMEM_LIMIT),
        name="mlstm",
    )(mqk, mv, mo, gates, c0.astype(f32), n0p, m0p, cbp, conv_w, cb, bg, mhg)


def _topk_rows(s, k, payload=None):
    n = s.shape[0]
    rows = lax.broadcasted_iota(i32, s.shape, 0).astype(f32)
    vals, ids = [], []
    for _ in range(k):
        m = jnp.max(s, axis=0, keepdims=True)
        pos = jnp.min(jnp.where(s == m, rows, float(n)), axis=0, keepdims=True)
        sel = rows == pos
        vals.append(m)
        ids.append(pos if payload is None else jnp.max(jnp.where(sel, payload, -1.0), axis=0, keepdims=True))
        s = jnp.where(sel, -jnp.inf, s)
    return jnp.concatenate(vals, axis=0), jnp.concatenate(ids, axis=0)


def _mid_kernel(x_ref, att_ref, h_ref, wo_ref, g2_ref, wq_ref, keys_ref,
                x1_ref, xn_ref, eidx_ref, gate_ref):
    cat = jnp.concatenate([att_ref[...], h_ref[...]], axis=-1).astype(bf16)
    x1 = x_ref[...] + jnp.dot(cat, wo_ref[...], preferred_element_type=f32)
    x1_ref[...] = x1
    xn = _rms(x1, g2_ref[...])
    xn_ref[...] = xn
    xb = xn.astype(bf16)
    e_rows, g_rows = [], []
    for h in range(P_HEADS):
        q = jnp.dot(xb, wq_ref[:, h * P_DKEY:(h + 1) * P_DKEY], preferred_element_type=f32).astype(bf16)
        half = []
        for p in range(2):
            st = lax.dot_general(keys_ref[h, p], q[:, p * N_KEYS:(p + 1) * N_KEYS], (((1,), (1,)), ((), ())),
                                 preferred_element_type=f32)
            half.append(_topk_rows(st, P_TOPK))
        (v0, i0), (v1, i1) = half
        width = [P_TOPK // (a + 1) for a in range(P_TOPK)]
        fill = -sum(width) % SUBLANES
        cand = jnp.concatenate([v0[a:a + 1] + v1[:width[a]] for a in range(P_TOPK)]
                               + [jnp.full((fill, v0.shape[1]), -jnp.inf, f32)], axis=0)
        cidx = jnp.concatenate([i0[a:a + 1] * float(N_KEYS) + i1[:width[a]] for a in range(P_TOPK)]
                               + [jnp.zeros((fill, v0.shape[1]), f32)], axis=0)
        top_s, eid = _topk_rows(cand, P_TOPK, payload=cidx)
        ex = jnp.exp(top_s - top_s[0:1])
        e_rows.append(eid)
        g_rows.append(ex / jnp.sum(ex, axis=0, keepdims=True))
    eidx_ref[...] = jnp.concatenate(e_rows, axis=0).T.astype(i32)
    gate_ref[...] = jnp.concatenate(g_rows, axis=0).T


def _mid(x, att, h, w_out, g2, wq, keys):
    n = x.shape[0]
    row = lambda w: pl.BlockSpec((ROW_BLOCK, w), lambda i: (i, 0))
    full = lambda a: pl.BlockSpec(a.shape, lambda i: (0,) * a.ndim)
    wo = w_out.astype(bf16)
    wqb = wq.astype(bf16)
    kb = keys.astype(bf16)
    g = g2.reshape(1, D_MODEL)
    return pl.pallas_call(
        _mid_kernel,
        grid=(n // ROW_BLOCK,),
        in_specs=[row(D_MODEL), row(A_WIDTH), row(M_WIDTH), full(wo), full(g), full(wqb), full(kb)],
        out_specs=[row(D_MODEL), row(D_MODEL), row(P_SLOTS), row(P_SLOTS)],
        out_shape=[jax.ShapeDtypeStruct((n, D_MODEL), f32), jax.ShapeDtypeStruct((n, D_MODEL), f32),
                   jax.ShapeDtypeStruct((n, P_SLOTS), i32), jax.ShapeDtypeStruct((n, P_SLOTS), f32)],
        compiler_params=pltpu.CompilerParams(dimension_semantics=("parallel",), vmem_limit_bytes=VMEM_LIMIT),
        name="outproj_retrieve",
    )(x, att, h, wo, g, wqb, kb)


def _gather_rows(eidx_ref, tab_ref, g_ref, t):
    for r in range(P_SLOTS):
        g_ref[pl.ds(r * ROW_TILES, ROW_TILES), :] = tab_ref[eidx_ref[t, r]]


def _tile_rows(t, rows=SUBLANES):
    return pl.ds(pl.multiple_of(t * rows, rows), rows)


def _pipelined_tokens(nt, eidx_ref, tab_ref, g0_s, g1_s, compute):
    bufs = (g0_s, g1_s)
    _gather_rows(eidx_ref, tab_ref, g0_s, 0)

    def body(j, carry):
        for u in range(TOKEN_UNROLL):
            t = TOKEN_UNROLL * j + u
            _gather_rows(eidx_ref, tab_ref, bufs[(u + 1) % 2], jnp.minimum(t + 1, nt - 1))
            compute(t, bufs[u % 2])
        return carry

    lax.fori_loop(0, nt // TOKEN_UNROLL, body, 0)


def _gelu_tanh(x):
    return 0.5 * x * (1.0 + jnp.tanh(math.sqrt(2.0 / math.pi) * (x + 0.044715 * (x * x * x))))


def _peer_u_kernel(eidx_ref, xn_ref, g_ref, tab_ref, w_ref, xl_s, r_s, g0_s, g1_s):
    nt = xn_ref.shape[0]
    xn = xn_ref[...]
    xh = xn.astype(bf16).astype(f32)
    xl = xn - xh
    for k in range(ROW_TILES):
        xl_s[pl.ds(k, nt, stride=2 * SUBLANES), :] = xh[:, k * LANES:(k + 1) * LANES]
        xl_s[pl.ds(SUBLANES + k, nt, stride=2 * SUBLANES), :] = xl[:, k * LANES:(k + 1) * LANES]
    diag = (lax.broadcasted_iota(i32, (SUBLANES, G_ROWS), 1) % ROW_TILES
            == lax.broadcasted_iota(i32, (SUBLANES, G_ROWS), 0))

    def compute(t, g_s):
        lhs = xl_s[_tile_rows(t, 2 * SUBLANES), :].astype(bf16)
        out = lax.dot_general(lhs, g_s[...], (((1,), (1,)), ((), ())), preferred_element_type=f32)
        part = jnp.where(diag, out[:SUBLANES] + out[SUBLANES:], 0.0)
        for c in range(ROW_TILES):
            r_s[c, _tile_rows(t), :] = part[:, c * LANES:(c + 1) * LANES]

    _pipelined_tokens(nt, eidx_ref, tab_ref, g0_s, g1_s, compute)
    cols = []
    for c in range(ROW_TILES):
        acc = r_s[c, pl.ds(0, nt, stride=SUBLANES), :]
        for k in range(1, SUBLANES):
            acc = acc + r_s[c, pl.ds(k, nt, stride=SUBLANES), :]
        cols.append(acc)
    s = jnp.concatenate(cols, axis=-1)
    fold = (lax.broadcasted_iota(i32, (G_ROWS, P_SLOTS), 0) // ROW_TILES
            == lax.broadcasted_iota(i32, (G_ROWS, P_SLOTS), 1)).astype(bf16)
    sh, sl = _split_bf16(s)
    act = jnp.dot(sh, fold, preferred_element_type=f32) + jnp.dot(sl, fold, preferred_element_type=f32)
    w_ref[...] = g_ref[...] * _gelu_tanh(act)


def _peer_v_kernel(eidx_ref, w_ref, x1_ref, gf_ref, tab_ref, y_ref, wl_s, o_s, g0_s, g1_s):
    nt = w_ref.shape[0]
    spread = (lax.broadcasted_iota(i32, (P_SLOTS, G_ROWS), 1) // ROW_TILES
              == lax.broadcasted_iota(i32, (P_SLOTS, G_ROWS), 0)).astype(bf16)
    wexp = jnp.dot(w_ref[...].astype(bf16), spread, preferred_element_type=f32)
    lane = lax.broadcasted_iota(i32, (nt, LANES), 1)
    for c in range(ROW_TILES):
        wc = wexp[:, c * LANES:(c + 1) * LANES]
        for k in range(SUBLANES):
            wl_s[c, pl.ds(k, nt, stride=SUBLANES), :] = jnp.where(lane % ROW_TILES == k, wc, 0.0)

    def compute(t, g_s):
        lhs = jnp.concatenate([wl_s[c, _tile_rows(t), :] for c in range(ROW_TILES)], axis=-1).astype(bf16)
        o_s[_tile_rows(t), :] = jnp.dot(lhs, g_s[...], preferred_element_type=f32)

    _pipelined_tokens(nt, eidx_ref, tab_ref, g0_s, g1_s, compute)
    peer = jnp.concatenate([o_s[pl.ds(k, nt, stride=SUBLANES), :] for k in range(ROW_TILES)], axis=-1)
    y_ref[...] = _rms(x1_ref[...] + peer, gf_ref[...])


def _peer_specs():
    row = lambda w: pl.BlockSpec((ROW_BLOCK, w), lambda i: (i, 0))
    idx = pl.BlockSpec((ROW_BLOCK, P_SLOTS), lambda i: (i, 0), memory_space=pltpu.SMEM)
    tab = pl.BlockSpec(memory_space=pltpu.VMEM)
    gscr = pltpu.VMEM((G_ROWS, LANES), bf16)
    params = pltpu.CompilerParams(dimension_semantics=("arbitrary",), vmem_limit_bytes=VMEM_LIMIT)
    return row, idx, tab, gscr, params


def _expert_table(tab):
    return tab.astype(bf16).reshape(tab.shape[0], ROW_TILES, LANES)


def _peer_u(eidx, xn, g, utab):
    n = xn.shape[0]
    row, idx, tab, gscr, params = _peer_specs()
    return pl.pallas_call(
        _peer_u_kernel,
        grid=(n // ROW_BLOCK,),
        in_specs=[idx, row(D_MODEL), row(P_SLOTS), tab],
        out_specs=row(P_SLOTS),
        out_shape=jax.ShapeDtypeStruct((n, P_SLOTS), f32),
        scratch_shapes=[pltpu.VMEM((ROW_BLOCK * 2 * SUBLANES, LANES), f32),
                        pltpu.VMEM((ROW_TILES, ROW_BLOCK * SUBLANES, LANES), f32), gscr, gscr],
        compiler_params=params,
        name="peer_u",
    )(eidx, xn, g, utab)


def _peer_v(eidx, w, x1, gf, vtab):
    n = x1.shape[0]
    row, idx, tab, gscr, params = _peer_specs()
    g = gf.reshape(1, D_MODEL)
    return pl.pallas_call(
        _peer_v_kernel,
        grid=(n // ROW_BLOCK,),
        in_specs=[idx, row(P_SLOTS), row(D_MODEL), pl.BlockSpec(g.shape, lambda i: (0, 0)), tab],
        out_specs=row(D_MODEL),
        out_shape=jax.ShapeDtypeStruct((n, D_MODEL), f32),
        scratch_shapes=[pltpu.VMEM((ROW_TILES, ROW_BLOCK * SUBLANES, LANES), f32),
                        pltpu.VMEM((ROW_BLOCK * SUBLANES, LANES), f32), gscr, gscr],
        compiler_params=params,
        name="peer_v",
    )(eidx, w, x1, g, vtab)


def kernel(x_prompt, x_sample, cache_k, cache_v, state_C, state_n, state_m, state_conv, norm1_g, w_in, b_gates, rel_bias, conv_w, conv_b, mh_norm_g, w_out, norm2_g, peer_wq, peer_keys, peer_u, peer_v, final_g):
    bp, sp, d = x_prompt.shape
    bs, ts, _ = x_sample.shape
    n_p, n_s = bp * sp, bs * ts
    n = n_p + n_s
    assert n % ROW_BLOCK == 0 and d == D_MODEL
    depth = w_in.shape[0]
    assert depth == 1, "the final norm is fused into the last layer's PEER pass"
    l = 0
    x = jnp.concatenate([x_prompt.reshape(n_p, d), x_sample.reshape(n_s, d)], axis=0)

    aq, ak, av, mqk, mv, mo, gates = _inproj(x, norm1_g[l], w_in[l])
    att_p = _attn_prompt(aq, ak, av, rel_bias[l], bp, sp)
    lcache = cache_k.shape[2]
    att_s = _attn_sample(aq, ak, av, cache_k[l].reshape(bs, lcache, A_WIDTH),
                         cache_v[l].reshape(bs, lcache, A_WIDTH), rel_bias[l], n_p, bs, ts)

    zeros = lambda *shp: jnp.zeros(shp, f32)
    mparams = (conv_w[l], conv_b[l], b_gates[l], mh_norm_g[l])
    h_p, c_p, nn_p, mm_p = _mlstm(mqk, mv, mo, gates, 0, bp, sp, zeros(bp, M_HEADS, M_DH, M_DH),
                                  zeros(bp, M_HEADS, M_DH), zeros(bp, M_HEADS), zeros(bp, CONV_W - 1, 2 * M_WIDTH),
                                  *mparams)
    h_s, c_s, nn_s, mm_s = _mlstm(mqk, mv, mo, gates, n_p, bs, ts, state_C[l], state_n[l], state_m[l],
                                  state_conv[l], *mparams)

    att = jnp.concatenate([att_p, att_s], axis=0)
    hm = jnp.concatenate([h_p, h_s], axis=0)
    split = lambda a: (a[:n_p].reshape(bp, sp, -1), a[n_p:].reshape(bs, ts, -1))
    (ak_p, ak_s), (av_p, av_s), (mqk_p, mqk_s) = split(ak), split(av), split(mqk)
    x1, xn2, eidx, gate = _mid(x, att, hm, w_out[l], norm2_g[l], peer_wq[l], peer_keys[l])
    w = _peer_u(eidx, xn2, gate, _expert_table(peer_u[l]))
    y = _peer_v(eidx, w, x1, final_g, _expert_table(peer_v[l]))

    keep = min(WINDOW, sp)
    heads = lambda a: a.reshape(a.shape[0], a.shape[1], A_HEADS, A_DH)
    conv_tail = lambda buf, a: jnp.concatenate([buf.astype(a.dtype), a], axis=1)[:, -(CONV_W - 1):]
    st = lambda a: a[None]
    return (y[:n_p].reshape(bp, sp, d), y[n_p:].reshape(bs, ts, d),
            st(heads(ak_p[:, sp - keep:])), st(heads(av_p[:, sp - keep:])),
            st(c_p), st(nn_p[:, :M_HEADS]), st(mm_p[:, :M_HEADS, 0]),
            st(conv_tail(zeros(bp, CONV_W - 1, 2 * M_WIDTH), mqk_p)),
            st(heads(ak_s)), st(heads(av_s)),
            st(c_s), st(nn_s[:, :M_HEADS]), st(mm_s[:, :M_HEADS, 0]),
            st(conv_tail(state_conv[l], mqk_s)))
```

```python
import functools
import math

import jax
import jax.numpy as jnp
from jax import lax
from jax.experimental import pallas as pl
from jax.experimental.pallas import tpu as pltpu

f32 = jnp.float32
bf16 = jnp.bfloat16
i32 = jnp.int32

D_MODEL = 1024
CHUNK = 64
A_HEADS = 8
A_DH = 64
A_WIDTH = A_HEADS * A_DH
BAND_CHUNKS = 8
WINDOW = BAND_CHUNKS * CHUNK
MAX_REL = 128
ATT_SCALE = A_DH ** -0.5
M_HEADS = 4
M_DH = 128
M_WIDTH = M_HEADS * M_DH
CONV_W = 4
P_HEADS = 8
P_DKEY = 256
N_KEYS = 128
P_TOPK = 16
P_SLOTS = P_HEADS * P_TOPK
EPS = 1e-6
NEG = -1e30

LANES = 128
SUBLANES = 8
ROW_BLOCK = 256
ATT_TILE = 512
ATT_SUB = 128
ATT_KEYS = ATT_SUB + WINDOW
ROW_TILES = D_MODEL // LANES
G_ROWS = P_SLOTS * ROW_TILES
TOKEN_UNROLL = 8
VMEM_LIMIT = 56 * 1024 * 1024


def _rms(x, g):
    return x * lax.rsqrt(jnp.mean(x * x, axis=-1, keepdims=True) + EPS) * g


def _split_bf16(x):
    hi = x.astype(bf16)
    lo = (x - hi.astype(f32)).astype(bf16)
    return hi, lo


def _pair_specs(nbp, width):
    return (pl.BlockSpec((ROW_BLOCK, width), lambda i: (jnp.minimum(i, nbp - 1), 0)),
            pl.BlockSpec((ROW_BLOCK, width), lambda i: (jnp.maximum(i - nbp, 0), 0)))


def _pick(nbp, p_ref, s_ref):
    return jnp.where(pl.program_id(0) < nbp, p_ref[...], s_ref[...])


def _inproj_kernel(xp_ref, xs_ref, g_ref, w_ref, wgh_ref, wgl_ref,
                   aq_ref, ak_ref, av_ref, mqk_ref, mv_ref, mo_ref, gate_ref, *, nbp):
    xn = _rms(_pick(nbp, xp_ref, xs_ref), g_ref[...])
    xh, xl = _split_bf16(xn)

    def proj(lo, hi):
        return jnp.dot(xh, w_ref[:, lo:hi], preferred_element_type=f32)

    aq_ref[...] = proj(0, 512)
    ak_ref[...] = proj(512, 1024)
    av_ref[...] = proj(1024, 1536)
    mqk_ref[...] = proj(1536, 2560)
    mv_ref[...] = proj(2560, 3072)
    mo_ref[...] = proj(3072, 3584)
    gate_ref[...] = (jnp.dot(xh, wgh_ref[...], preferred_element_type=f32)
                     + jnp.dot(xl, wgh_ref[...], preferred_element_type=f32)
                     + jnp.dot(xh, wgl_ref[...], preferred_element_type=f32))


def _inproj(xp, xs, g1, w_in):
    n = xp.shape[0] + xs.shape[0]
    nbp = xp.shape[0] // ROW_BLOCK
    main = 3 * A_WIDTH + 4 * M_WIDTH
    w_main = w_in[:, :main].astype(bf16)
    wg = jnp.pad(w_in[:, main:], ((0, 0), (0, LANES - 2 * M_HEADS)))
    wgh, wgl = _split_bf16(wg)
    widths = (512, 512, 512, 1024, 512, 512, LANES)
    row = lambda w: pl.BlockSpec((ROW_BLOCK, w), lambda i: (i, 0))
    full = lambda a: pl.BlockSpec(a.shape, lambda i: (0,) * a.ndim)
    g = g1.reshape(1, D_MODEL)
    return pl.pallas_call(
        functools.partial(_inproj_kernel, nbp=nbp),
        grid=(n // ROW_BLOCK,),
        in_specs=[*_pair_specs(nbp, D_MODEL), full(g), full(w_main), full(wgh), full(wgl)],
        out_specs=[row(w) for w in widths],
        out_shape=[jax.ShapeDtypeStruct((n, w), f32) for w in widths],
        compiler_params=pltpu.CompilerParams(dimension_semantics=("parallel",), vmem_limit_bytes=VMEM_LIMIT),
        name="inproj",
    )(xp, xs, g, w_main, wgh, wgl)


def _attn_heads(q, k, v, bias_ref, key_ok):
    outs = []
    for h in range(A_HEADS):
        sl = slice(h * A_DH, (h + 1) * A_DH)
        s = lax.dot_general(q[:, sl], k[:, sl], (((1,), (1,)), ((), ())), preferred_element_type=f32)
        s = s * ATT_SCALE + bias_ref[h]
        if key_ok is not None:
            s = jnp.where(key_ok, s, NEG)
        m = jnp.max(s, axis=-1, keepdims=True)
        p = jnp.exp(s - m)
        l = jnp.sum(p, axis=-1, keepdims=True)
        o = jnp.dot(p.astype(bf16), v[:, sl], preferred_element_type=f32)
        outs.append(o / l)
    return jnp.concatenate(outs, axis=-1)


def _attn_prompt_kernel(q_ref, k0_ref, k1_ref, v0_ref, v1_ref, bias_ref, o_ref):
    t = pl.program_id(1)
    q = q_ref[...].astype(bf16)
    k = jnp.concatenate([k0_ref[...], k1_ref[...]], axis=0).astype(bf16)
    v = jnp.concatenate([v0_ref[...], v1_ref[...]], axis=0).astype(bf16)
    col = lax.broadcasted_iota(i32, (1, ATT_KEYS), 1)
    for s in range(ATT_TILE // ATT_SUB):
        lo = s * ATT_SUB
        key_ok = (t * ATT_TILE + lo + col) >= WINDOW
        o_ref[lo:lo + ATT_SUB, :] = _attn_heads(q[lo:lo + ATT_SUB], k[lo:lo + ATT_KEYS], v[lo:lo + ATT_KEYS],
                                                 bias_ref, key_ok)


def _attn_sample_kernel(q_ref, k_ref, v_ref, bias_ref, o_ref):
    o_ref[...] = _attn_heads(q_ref[...].astype(bf16), k_ref[0].astype(bf16), v_ref[0].astype(bf16), bias_ref, None)


def _rel_bias_table(rel_bias, rows, cols, offset, valid):
    span = rows + cols - 1
    rel = offset + rows - 1 - jnp.arange(span)
    diag = rel_bias[:, jnp.clip(rel, -MAX_REL, MAX_REL) + MAX_REL].astype(f32)
    diag = jnp.pad(diag, ((0, 0), (0, 1)))
    flat = jnp.tile(diag, (1, rows))[:, rows - 1:rows - 1 + rows * span]
    return jnp.where(valid[None], flat.reshape(-1, rows, span)[:, :, :cols], NEG)


def _attn_prompt(q, k, v, rel_bias, bsz, s):
    assert s % ATT_TILE == 0 and WINDOW == ATT_TILE
    nt = s // ATT_TILE
    i = jnp.arange(ATT_SUB)[:, None]
    j = jnp.arange(ATT_KEYS)[None, :]
    off = j - (i // CHUNK) * CHUNK
    bias = _rel_bias_table(rel_bias, ATT_SUB, ATT_KEYS, WINDOW, (off >= 0) & (off < WINDOW + CHUNK))
    cur = pl.BlockSpec((ATT_TILE, A_WIDTH), lambda b, t: (b * nt + t, 0))
    prev = pl.BlockSpec((ATT_TILE, A_WIDTH), lambda b, t: (b * nt + jnp.maximum(t - 1, 0), 0))
    return pl.pallas_call(
        _attn_prompt_kernel,
        grid=(bsz, nt),
        in_specs=[cur, prev, cur, prev, cur, pl.BlockSpec(bias.shape, lambda b, t: (0, 0, 0))],
        out_specs=cur,
        out_shape=jax.ShapeDtypeStruct((bsz * s, A_WIDTH), f32),
        compiler_params=pltpu.CompilerParams(dimension_semantics=("parallel", "parallel"),
                                             vmem_limit_bytes=VMEM_LIMIT),
        name="attn_prompt",
    )(q, k, k, v, v, bias)


def _attn_sample(q, k, v, ck, cv, rel_bias, row0, bsz, t):
    l = ck.shape[1]
    assert row0 % t == 0
    keys = -(-(l + t) // LANES) * LANES
    padk = ((0, 0), (0, keys - l - t), (0, 0))
    kk = jnp.pad(jnp.concatenate([ck, k[row0:].reshape(bsz, t, A_WIDTH)], axis=1), padk)
    vv = jnp.pad(jnp.concatenate([cv, v[row0:].reshape(bsz, t, A_WIDTH)], axis=1), padk)
    j = jnp.arange(keys)[None, :]
    bias = _rel_bias_table(rel_bias, t, keys, l, jnp.broadcast_to(j < l + t, (t, keys)))
    return pl.pallas_call(
        _attn_sample_kernel,
        grid=(bsz,),
        in_specs=[pl.BlockSpec((t, A_WIDTH), lambda b: (row0 // t + b, 0)),
                  pl.BlockSpec((1, keys, A_WIDTH), lambda b: (b, 0, 0)),
                  pl.BlockSpec((1, keys, A_WIDTH), lambda b: (b, 0, 0)),
                  pl.BlockSpec(bias.shape, lambda b: (0, 0, 0))],
        out_specs=pl.BlockSpec((t, A_WIDTH), lambda b: (b, 0)),
        out_shape=jax.ShapeDtypeStruct((bsz * t, A_WIDTH), f32),
        compiler_params=pltpu.CompilerParams(dimension_semantics=("parallel",), vmem_limit_bytes=VMEM_LIMIT),
        name="attn_sample",
    )(q, kk, vv, bias)


def _mlstm_kernel(qk_ref, v_ref, o_ref, gate_ref, c0_ref, n0_ref, m0_ref, cbuf_ref,
                  cw_ref, cb_ref, bg_ref, mhg_ref,
                  h_ref, cout_ref, nout_ref, mout_ref,
                  c_s, n_s, m_s, prev_s, *, lc):
    c = pl.program_id(1)

    @pl.when(c == 0)
    def _():
        c_s[...] = c0_ref[0]
        n_s[...] = n0_ref[0]
        m_s[...] = m0_ref[0]
        prev_s[...] = cbuf_ref[0]

    a = qk_ref[...]
    ext = jnp.concatenate([prev_s[...], a], axis=0)
    conv = cb_ref[...]
    for j in range(CONV_W):
        lo = SUBLANES - (CONV_W - 1) + j
        conv = conv + cw_ref[j:j + 1, :] * ext[lo:lo + lc]
    prev_s[...] = a[lc - SUBLANES:lc]
    qk = conv * jax.nn.sigmoid(conv)

    z = gate_ref[...] + bg_ref[...]
    lane = lax.broadcasted_iota(i32, (lc, LANES), 1)
    row = lax.broadcasted_iota(i32, (lc, LANES), 0)
    logf = jnp.minimum(z, 0.0) - jnp.log1p(jnp.exp(-jnp.abs(z)))
    cum = jnp.where((lane >= M_HEADS) & (lane < 2 * M_HEADS), logf, 0.0)
    shift = 1
    while shift < lc:
        cum = cum + jnp.where(row >= shift, pltpu.roll(cum, shift, axis=0), 0.0)
        shift *= 2
    zc = jnp.where(lane < M_HEADS, z, cum)
    zt = jnp.concatenate([zc, jnp.zeros((LANES - lc, LANES), f32)], axis=0).T[:, :lc]

    ri = lax.broadcasted_iota(i32, (lc, lc), 0)
    ci = lax.broadcasted_iota(i32, (lc, lc), 1)
    causal = ri >= ci
    vall = v_ref[...]
    hs = []
    for h in range(M_HEADS):
        sl = slice(h * M_DH, (h + 1) * M_DH)
        q = qk[:, sl]
        k = qk[:, M_WIDTH + h * M_DH:M_WIDTH + (h + 1) * M_DH] * (M_DH ** -0.5)
        v = vall[:, sl]
        i_col = zc[:, h:h + 1]
        b_col = zc[:, M_HEADS + h:M_HEADS + h + 1]
        i_row = zt[h:h + 1, :]
        b_row = zt[M_HEADS + h:M_HEADS + h + 1, :]
        m_prev = m_s[h:h + 1, 0:1]
        c_prev = c_s[h]
        n_prev = n_s[h:h + 1, :]

        dmat = jnp.where(causal, b_col - b_row + i_row, NEG)
        inter = b_col + m_prev
        mt = jnp.maximum(jnp.max(dmat, axis=-1, keepdims=True), inter)
        w_intra = jnp.exp(dmat - mt)
        w_inter = jnp.exp(inter - mt)
        qb, kb, vb = q.astype(bf16), k.astype(bf16), v.astype(bf16)
        s = lax.dot_general(qb, kb, (((1,), (1,)), ((), ())), preferred_element_type=f32) * w_intra
        num = (w_inter * jnp.dot(qb, c_prev.astype(bf16), preferred_element_type=f32)
               + jnp.dot(s.astype(bf16), vb, preferred_element_type=f32))
        den = w_inter * jnp.sum(q * n_prev, axis=-1, keepdims=True) + jnp.sum(s, axis=-1, keepdims=True)
        hh = num / jnp.maximum(jnp.abs(den), jnp.exp(-mt))
        m_new = mt[lc - 1:lc, :]
        b_last = b_col[lc - 1:lc, :]
        w_s = jnp.exp(b_last - b_col + i_col - m_new)
        decay = jnp.exp(b_last + m_prev - m_new)
        kw = k * w_s
        c_s[h] = decay * c_prev + lax.dot_general(kw.astype(bf16), vb, (((0,), (0,)), ((), ())),
                                                   preferred_element_type=f32)
        n_s[h:h + 1, :] = decay * n_prev + jnp.sum(kw, axis=0, keepdims=True)
        m_s[h:h + 1, :] = jnp.broadcast_to(m_new, (1, LANES))
        hs.append(hh * lax.rsqrt(jnp.mean(hh * hh, axis=-1, keepdims=True) + EPS))

    h_all = jnp.concatenate(hs, axis=-1)
    h_ref[...] = h_all * mhg_ref[...] * jax.nn.sigmoid(o_ref[...])

    @pl.when(c == pl.num_programs(1) - 1)
    def _():
        cout_ref[0] = c_s[...]
        nout_ref[0] = n_s[...]
        mout_ref[0] = m_s[...]


def _mlstm(mqk, mv, mo, gates, row0, bsz, t, c0, n0, m0, cbuf, conv_w, conv_b, b_gates, mh_g):
    lc = min(CHUNK, t)
    nc = t // lc
    assert t % lc == 0 and lc % SUBLANES == 0 and row0 % lc == 0
    n0p = jnp.pad(n0.astype(f32), ((0, 0), (0, SUBLANES - M_HEADS), (0, 0)))
    m0p = jnp.pad(jnp.broadcast_to(m0.astype(f32)[:, :, None], (bsz, M_HEADS, LANES)),
                  ((0, 0), (0, SUBLANES - M_HEADS), (0, 0)))
    cbp = jnp.pad(cbuf.astype(f32), ((0, 0), (SUBLANES - (CONV_W - 1), 0), (0, 0)))
    bg = jnp.pad(b_gates.astype(f32), (0, LANES - 2 * M_HEADS)).reshape(1, LANES)
    seq = lambda w: pl.BlockSpec((lc, w), lambda b, c: (row0 // lc + b * nc + c, 0))
    out_seq = pl.BlockSpec((lc, M_WIDTH), lambda b, c: (b * nc + c, 0))
    per_b = lambda shp: pl.BlockSpec((1,) + shp, lambda b, c: (b,) + (0,) * len(shp))
    full = lambda a: pl.BlockSpec(a.shape, lambda b, c: (0,) * a.ndim)
    cb = conv_b.reshape(1, -1)
    mhg = mh_g.reshape(1, -1)
    return pl.pallas_call(
        functools.partial(_mlstm_kernel, lc=lc),
        grid=(bsz, nc),
        in_specs=[seq(2 * M_WIDTH), seq(M_WIDTH), seq(M_WIDTH), seq(LANES),
                  per_b((M_HEADS, M_DH, M_DH)), per_b((SUBLANES, M_DH)), per_b((SUBLANES, LANES)),
                  per_b((SUBLANES, 2 * M_WIDTH)),
                  full(conv_w), full(cb), full(bg), full(mhg)],
        out_specs=[out_seq, per_b((M_HEADS, M_DH, M_DH)), per_b((SUBLANES, M_DH)), per_b((SUBLANES, LANES))],
        out_shape=[jax.ShapeDtypeStruct((bsz * t, M_WIDTH), f32),
                   jax.ShapeDtypeStruct((bsz, M_HEADS, M_DH, M_DH), f32),
                   jax.ShapeDtypeStruct((bsz, SUBLANES, M_DH), f32),
                   jax.ShapeDtypeStruct((bsz, SUBLANES, LANES), f32)],
        scratch_shapes=[pltpu.VMEM((M_HEADS, M_DH, M_DH), f32), pltpu.VMEM((SUBLANES, M_DH), f32),
                        pltpu.VMEM((SUBLANES, LANES), f32), pltpu.VMEM((SUBLANES, 2 * M_WIDTH), f32)],
        compiler_params=pltpu.CompilerParams(dimension_semantics=("parallel", "arbitrary"),
                                             vmem_limit_bytes=VMEM_LIMIT),
        name="mlstm",
    )(mqk, mv, mo, gates, c0.astype(f32), n0p, m0p, cbp, conv_w, cb, bg, mhg)


def _topk_rows(s, k, payload=None):
    n = s.shape[0]
    rows = lax.broadcasted_iota(i32, s.shape, 0).astype(f32)
    vals, ids = [], []
    for _ in range(k):
        m = jnp.max(s, axis=0, keepdims=True)
        pos = jnp.min(jnp.where(s == m, rows, float(n)), axis=0, keepdims=True)
        sel = rows == pos
        vals.append(m)
        ids.append(pos if payload is None else jnp.max(jnp.where(sel, payload, -1.0), axis=0, keepdims=True))
        s = jnp.where(sel, -jnp.inf, s)
    return jnp.concatenate(vals, axis=0), jnp.concatenate(ids, axis=0)


def _mid_kernel(xp_ref, xs_ref, attp_ref, atts_ref, hp_ref, hs_ref, wo_ref, g2_ref, wq_ref, keys_ref,
                x1_ref, xn_ref, eidx_ref, gate_ref, *, nbp):
    cat = jnp.concatenate([_pick(nbp, attp_ref, atts_ref), _pick(nbp, hp_ref, hs_ref)], axis=-1).astype(bf16)
    x1 = _pick(nbp, xp_ref, xs_ref) + jnp.dot(cat, wo_ref[...], preferred_element_type=f32)
    x1_ref[...] = x1
    xn = _rms(x1, g2_ref[...])
    xn_ref[...] = xn
    xb = xn.astype(bf16)
    e_rows, g_rows = [], []
    for h in range(P_HEADS):
        q = jnp.dot(xb, wq_ref[:, h * P_DKEY:(h + 1) * P_DKEY], preferred_element_type=f32).astype(bf16)
        half = []
        for p in range(2):
            st = lax.dot_general(keys_ref[h, p], q[:, p * N_KEYS:(p + 1) * N_KEYS], (((1,), (1,)), ((), ())),
                                 preferred_element_type=f32)
            half.append(_topk_rows(st, P_TOPK))
        (v0, i0), (v1, i1) = half
        width = [P_TOPK // (a + 1) for a in range(P_TOPK)]
        fill = -sum(width) % SUBLANES
        cand = jnp.concatenate([v0[a:a + 1] + v1[:width[a]] for a in range(P_TOPK)]
                               + [jnp.full((fill, v0.shape[1]), -jnp.inf, f32)], axis=0)
        cidx = jnp.concatenate([i0[a:a + 1] * float(N_KEYS) + i1[:width[a]] for a in range(P_TOPK)]
                               + [jnp.zeros((fill, v0.shape[1]), f32)], axis=0)
        top_s, eid = _topk_rows(cand, P_TOPK, payload=cidx)
        ex = jnp.exp(top_s - top_s[0:1])
        e_rows.append(eid)
        g_rows.append(ex / jnp.sum(ex, axis=0, keepdims=True))
    eidx_ref[...] = jnp.concatenate(e_rows, axis=0).T.astype(i32)
    gate_ref[...] = jnp.concatenate(g_rows, axis=0).T


def _mid(xp, xs, attp, atts, hp, hs, w_out, g2, wq, keys):
    n = xp.shape[0] + xs.shape[0]
    nbp = xp.shape[0] // ROW_BLOCK
    row = lambda w: pl.BlockSpec((ROW_BLOCK, w), lambda i: (i, 0))
    full = lambda a: pl.BlockSpec(a.shape, lambda i: (0,) * a.ndim)
    wo = w_out.astype(bf16)
    wqb = wq.astype(bf16)
    kb = keys.astype(bf16)
    g = g2.reshape(1, D_MODEL)
    return pl.pallas_call(
        functools.partial(_mid_kernel, nbp=nbp),
        grid=(n // ROW_BLOCK,),
        in_specs=[*_pair_specs(nbp, D_MODEL), *_pair_specs(nbp, A_WIDTH), *_pair_specs(nbp, M_WIDTH),
                  full(wo), full(g), full(wqb), full(kb)],
        out_specs=[row(D_MODEL), row(D_MODEL), row(P_SLOTS), row(P_SLOTS)],
        out_shape=[jax.ShapeDtypeStruct((n, D_MODEL), f32), jax.ShapeDtypeStruct((n, D_MODEL), f32),
                   jax.ShapeDtypeStruct((n, P_SLOTS), i32), jax.ShapeDtypeStruct((n, P_SLOTS), f32)],
        compiler_params=pltpu.CompilerParams(dimension_semantics=("parallel",), vmem_limit_bytes=VMEM_LIMIT),
        name="outproj_retrieve",
    )(xp, xs, attp, atts, hp, hs, wo, g, wqb, kb)


def _gather_rows(eidx_ref, tab_ref, g_ref, t):
    for r in range(P_SLOTS):
        g_ref[pl.ds(r * ROW_TILES, ROW_TILES), :] = tab_ref[eidx_ref[t, r]]


def _tile_rows(t, rows=SUBLANES):
    return pl.ds(pl.multiple_of(t * rows, rows), rows)


def _pipelined_tokens(nt, eidx_ref, tab_ref, g0_s, g1_s, compute):
    bufs = (g0_s, g1_s)
    _gather_rows(eidx_ref, tab_ref, g0_s, 0)

    def body(j, carry):
        for u in range(TOKEN_UNROLL):
            t = TOKEN_UNROLL * j + u
            _gather_rows(eidx_ref, tab_ref, bufs[(u + 1) % 2], jnp.minimum(t + 1, nt - 1))
            compute(t, bufs[u % 2])
        return carry

    lax.fori_loop(0, nt // TOKEN_UNROLL, body, 0)


def _gelu_tanh(x):
    return 0.5 * x * (1.0 + jnp.tanh(math.sqrt(2.0 / math.pi) * (x + 0.044715 * (x * x * x))))


def _peer_u_kernel(eidx_ref, xn_ref, g_ref, tab_ref, w_ref, xl_s, r_s, g0_s, g1_s):
    nt = xn_ref.shape[0]
    xn = xn_ref[...]
    xh = xn.astype(bf16).astype(f32)
    xl = xn - xh
    for k in range(ROW_TILES):
        xl_s[pl.ds(k, nt, stride=2 * SUBLANES), :] = xh[:, k * LANES:(k + 1) * LANES]
        xl_s[pl.ds(SUBLANES + k, nt, stride=2 * SUBLANES), :] = xl[:, k * LANES:(k + 1) * LANES]
    diag = (lax.broadcasted_iota(i32, (SUBLANES, G_ROWS), 1) % ROW_TILES
            == lax.broadcasted_iota(i32, (SUBLANES, G_ROWS), 0))

    def compute(t, g_s):
        lhs = xl_s[_tile_rows(t, 2 * SUBLANES), :].astype(bf16)
        out = lax.dot_general(lhs, g_s[...], (((1,), (1,)), ((), ())), preferred_element_type=f32)
        part = jnp.where(diag, out[:SUBLANES] + out[SUBLANES:], 0.0)
        for c in range(ROW_TILES):
            r_s[c, _tile_rows(t), :] = part[:, c * LANES:(c + 1) * LANES]

    _pipelined_tokens(nt, eidx_ref, tab_ref, g0_s, g1_s, compute)
    cols = []
    for c in range(ROW_TILES):
        acc = r_s[c, pl.ds(0, nt, stride=SUBLANES), :]
        for k in range(1, SUBLANES):
            acc = acc + r_s[c, pl.ds(k, nt, stride=SUBLANES), :]
        cols.append(acc)
    s = jnp.concatenate(cols, axis=-1)
    fold = (lax.broadcasted_iota(i32, (G_ROWS, P_SLOTS), 0) // ROW_TILES
            == lax.broadcasted_iota(i32, (G_ROWS, P_SLOTS), 1)).astype(bf16)
    sh, sl = _split_bf16(s)
    act = jnp.dot(sh, fold, preferred_element_type=f32) + jnp.dot(sl, fold, preferred_element_type=f32)
    w_ref[...] = g_ref[...] * _gelu_tanh(act)


def _peer_v_kernel(eidx_ref, w_ref, x1_ref, gf_ref, tab_ref, yp_ref, ys_ref, wl_s, o_s, g0_s, g1_s, *, nbp):
    nt = w_ref.shape[0]
    spread = (lax.broadcasted_iota(i32, (P_SLOTS, G_ROWS), 1) // ROW_TILES
              == lax.broadcasted_iota(i32, (P_SLOTS, G_ROWS), 0)).astype(bf16)
    wexp = jnp.dot(w_ref[...].astype(bf16), spread, preferred_element_type=f32)
    lane = lax.broadcasted_iota(i32, (nt, LANES), 1)
    for c in range(ROW_TILES):
        wc = wexp[:, c * LANES:(c + 1) * LANES]
        for k in range(SUBLANES):
            wl_s[c, pl.ds(k, nt, stride=SUBLANES), :] = jnp.where(lane % ROW_TILES == k, wc, 0.0)

    def compute(t, g_s):
        lhs = jnp.concatenate([wl_s[c, _tile_rows(t), :] for c in range(ROW_TILES)], axis=-1).astype(bf16)
        o_s[_tile_rows(t), :] = jnp.dot(lhs, g_s[...], preferred_element_type=f32)

    _pipelined_tokens(nt, eidx_ref, tab_ref, g0_s, g1_s, compute)
    peer = jnp.concatenate([o_s[pl.ds(k, nt, stride=SUBLANES), :] for k in range(ROW_TILES)], axis=-1)
    y = _rms(x1_ref[...] + peer, gf_ref[...])

    @pl.when(pl.program_id(0) < nbp)
    def _():
        yp_ref[...] = y

    @pl.when(pl.program_id(0) >= nbp)
    def _():
        ys_ref[...] = y


def _peer_specs():
    row = lambda w: pl.BlockSpec((ROW_BLOCK, w), lambda i: (i, 0))
    idx = pl.BlockSpec((ROW_BLOCK, P_SLOTS), lambda i: (i, 0), memory_space=pltpu.SMEM)
    tab = pl.BlockSpec(memory_space=pltpu.VMEM)
    gscr = pltpu.VMEM((G_ROWS, LANES), bf16)
    params = pltpu.CompilerParams(dimension_semantics=("arbitrary",), vmem_limit_bytes=VMEM_LIMIT)
    return row, idx, tab, gscr, params


def _expert_table(tab):
    return tab.astype(bf16).reshape(tab.shape[0], ROW_TILES, LANES)


def _peer_u(eidx, xn, g, utab):
    n = xn.shape[0]
    row, idx, tab, gscr, params = _peer_specs()
    return pl.pallas_call(
        _peer_u_kernel,
        grid=(n // ROW_BLOCK,),
        in_specs=[idx, row(D_MODEL), row(P_SLOTS), tab],
        out_specs=row(P_SLOTS),
        out_shape=jax.ShapeDtypeStruct((n, P_SLOTS), f32),
        scratch_shapes=[pltpu.VMEM((ROW_BLOCK * 2 * SUBLANES, LANES), f32),
                        pltpu.VMEM((ROW_TILES, ROW_BLOCK * SUBLANES, LANES), f32), gscr, gscr],
        compiler_params=params,
        name="peer_u",
    )(eidx, xn, g, utab)


def _peer_v(eidx, w, x1, gf, vtab, n_p):
    n = x1.shape[0]
    nbp = n_p // ROW_BLOCK
    row, idx, tab, gscr, params = _peer_specs()
    g = gf.reshape(1, D_MODEL)
    return pl.pallas_call(
        functools.partial(_peer_v_kernel, nbp=nbp),
        grid=(n // ROW_BLOCK,),
        in_specs=[idx, row(P_SLOTS), row(D_MODEL), pl.BlockSpec(g.shape, lambda i: (0, 0)), tab],
        out_specs=list(_pair_specs(nbp, D_MODEL)),
        out_shape=[jax.ShapeDtypeStruct((n_p, D_MODEL), f32), jax.ShapeDtypeStruct((n - n_p, D_MODEL), f32)],
        scratch_shapes=[pltpu.VMEM((ROW_TILES, ROW_BLOCK * SUBLANES, LANES), f32),
                        pltpu.VMEM((ROW_BLOCK * SUBLANES, LANES), f32), gscr, gscr],
        compiler_params=params,
        name="peer_v",
    )(eidx, w, x1, g, vtab)


def kernel(x_prompt, x_sample, cache_k, cache_v, state_C, state_n, state_m, state_conv, norm1_g, w_in, b_gates, rel_bias, conv_w, conv_b, mh_norm_g, w_out, norm2_g, peer_wq, peer_keys, peer_u, peer_v, final_g):
    bp, sp, d = x_prompt.shape
    bs, ts, _ = x_sample.shape
    n_p, n_s = bp * sp, bs * ts
    n = n_p + n_s
    assert n_p % ROW_BLOCK == 0 and n_s % ROW_BLOCK == 0 and d == D_MODEL
    depth = w_in.shape[0]
    assert depth == 1, "the final norm is fused into the last layer's PEER pass"
    l = 0
    xp, xs = x_prompt.reshape(n_p, d), x_sample.reshape(n_s, d)

    aq, ak, av, mqk, mv, mo, gates = _inproj(xp, xs, norm1_g[l], w_in[l])
    att_p = _attn_prompt(aq, ak, av, rel_bias[l], bp, sp)
    lcache = cache_k.shape[2]
    att_s = _attn_sample(aq, ak, av, cache_k[l].reshape(bs, lcache, A_WIDTH),
                         cache_v[l].reshape(bs, lcache, A_WIDTH), rel_bias[l], n_p, bs, ts)

    zeros = lambda *shp: jnp.zeros(shp, f32)
    mparams = (conv_w[l], conv_b[l], b_gates[l], mh_norm_g[l])
    h_p, c_p, nn_p, mm_p = _mlstm(mqk, mv, mo, gates, 0, bp, sp, zeros(bp, M_HEADS, M_DH, M_DH),
                                  zeros(bp, M_HEADS, M_DH), zeros(bp, M_HEADS), zeros(bp, CONV_W - 1, 2 * M_WIDTH),
                                  *mparams)
    h_s, c_s, nn_s, mm_s = _mlstm(mqk, mv, mo, gates, n_p, bs, ts, state_C[l], state_n[l], state_m[l],
                                  state_conv[l], *mparams)

    x1, xn2, eidx, gate = _mid(xp, xs, att_p, att_s, h_p, h_s, w_out[l], norm2_g[l], peer_wq[l], peer_keys[l])
    w = _peer_u(eidx, xn2, gate, _expert_table(peer_u[l]))
    y_p, y_s = _peer_v(eidx, w, x1, final_g, _expert_table(peer_v[l]), n_p)

    def tail(a, row0, bsz, t, keep):
        return jnp.stack([a[row0 + (b + 1) * t - keep:row0 + (b + 1) * t] for b in range(bsz)])

    keep = min(WINDOW, sp)
    heads = lambda a: a.reshape(a.shape[0], a.shape[1], A_HEADS, A_DH)
    ctail = CONV_W - 1
    conv_tail = lambda buf, a, row0, bsz, t: jnp.concatenate([buf.astype(a.dtype), tail(a, row0, bsz, t, min(ctail, t))],
                                                             axis=1)[:, -ctail:]
    st = lambda a: a[None]
    return (y_p.reshape(bp, sp, d), y_s.reshape(bs, ts, d),
            st(heads(tail(ak, 0, bp, sp, keep))), st(heads(tail(av, 0, bp, sp, keep))),
            st(c_p), st(nn_p[:, :M_HEADS]), st(mm_p[:, :M_HEADS, 0]),
            st(conv_tail(zeros(bp, ctail, 2 * M_WIDTH), mqk, 0, bp, sp)),
            st(heads(ak[n_p:].reshape(bs, ts, A_WIDTH))), st(heads(av[n_p:].reshape(bs, ts, A_WIDTH))),
            st(c_s), st(nn_s[:, :M_HEADS]), st(mm_s[:, :M_HEADS, 0]),
            st(conv_tail(state_conv[l], mqk, n_p, bs, ts)))
```

```python
import functools
import math

import jax
import jax.numpy as jnp
from jax import lax
from jax.experimental import pallas as pl
from jax.experimental.pallas import tpu as pltpu
from jax.experimental.pallas import tpu_sc as plsc

f32 = jnp.float32
bf16 = jnp.bfloat16
i32 = jnp.int32

D_MODEL = 1024
CHUNK = 64
A_HEADS = 8
A_DH = 64
A_WIDTH = A_HEADS * A_DH
BAND_CHUNKS = 8
WINDOW = BAND_CHUNKS * CHUNK
MAX_REL = 128
ATT_SCALE = A_DH ** -0.5
M_HEADS = 4
M_DH = 128
M_WIDTH = M_HEADS * M_DH
CONV_W = 4
P_HEADS = 8
P_DKEY = 256
N_KEYS = 128
P_TOPK = 16
P_SLOTS = P_HEADS * P_TOPK
EPS = 1e-6
NEG = -1e30

LANES = 128
SUBLANES = 8
ROW_BLOCK = 256
ATT_TILE = 512
ATT_SUB = 128
ATT_KEYS = ATT_SUB + WINDOW
ROW_TILES = D_MODEL // LANES
G_ROWS = P_SLOTS * ROW_TILES
TOKEN_UNROLL = 8
VMEM_LIMIT = 56 * 1024 * 1024

SC_WORKERS = 32
SC_LANES = 16
SC_UNIT_ROWS = P_SLOTS // 2
SC_ROW_GROUP = 16
SC_WORDS = D_MODEL // 2
SC_CHUNKS = SC_WORDS // SC_LANES
SC_CHUNK_GROUP = 8
SC_TOKENS = 16384


def _rms(x, g):
    return x * lax.rsqrt(jnp.mean(x * x, axis=-1, keepdims=True) + EPS) * g


def _split_bf16(x):
    hi = x.astype(bf16)
    lo = (x - hi.astype(f32)).astype(bf16)
    return hi, lo


def _pair_specs(nbp, width):
    return (pl.BlockSpec((ROW_BLOCK, width), lambda i: (jnp.minimum(i, nbp - 1), 0)),
            pl.BlockSpec((ROW_BLOCK, width), lambda i: (jnp.maximum(i - nbp, 0), 0)))


def _pick(nbp, p_ref, s_ref):
    return jnp.where(pl.program_id(0) < nbp, p_ref[...], s_ref[...])


def _inproj_kernel(xp_ref, xs_ref, g_ref, w_ref, wgh_ref, wgl_ref,
                   aq_ref, ak_ref, av_ref, mqk_ref, mv_ref, mo_ref, gate_ref, *, nbp):
    xn = _rms(_pick(nbp, xp_ref, xs_ref), g_ref[...])
    xh, xl = _split_bf16(xn)

    def proj(lo, hi):
        return jnp.dot(xh, w_ref[:, lo:hi], preferred_element_type=f32)

    aq_ref[...] = proj(0, 512)
    ak_ref[...] = proj(512, 1024)
    av_ref[...] = proj(1024, 1536)
    mqk_ref[...] = proj(1536, 2560)
    mv_ref[...] = proj(2560, 3072)
    mo_ref[...] = proj(3072, 3584)
    gate_ref[...] = (jnp.dot(xh, wgh_ref[...], preferred_element_type=f32)
                     + jnp.dot(xl, wgh_ref[...], preferred_element_type=f32)
                     + jnp.dot(xh, wgl_ref[...], preferred_element_type=f32))


def _inproj(xp, xs, g1, w_in):
    n = xp.shape[0] + xs.shape[0]
    nbp = xp.shape[0] // ROW_BLOCK
    main = 3 * A_WIDTH + 4 * M_WIDTH
    w_main = w_in[:, :main].astype(bf16)
    wg = jnp.pad(w_in[:, main:], ((0, 0), (0, LANES - 2 * M_HEADS)))
    wgh, wgl = _split_bf16(wg)
    widths = (512, 512, 512, 1024, 512, 512, LANES)
    row = lambda w: pl.BlockSpec((ROW_BLOCK, w), lambda i: (i, 0))
    full = lambda a: pl.BlockSpec(a.shape, lambda i: (0,) * a.ndim)
    g = g1.reshape(1, D_MODEL)
    return pl.pallas_call(
        functools.partial(_inproj_kernel, nbp=nbp),
        grid=(n // ROW_BLOCK,),
        in_specs=[*_pair_specs(nbp, D_MODEL), full(g), full(w_main), full(wgh), full(wgl)],
        out_specs=[row(w) for w in widths],
        out_shape=[jax.ShapeDtypeStruct((n, w), f32) for w in widths],
        compiler_params=pltpu.CompilerParams(dimension_semantics=("parallel",), vmem_limit_bytes=VMEM_LIMIT),
        name="inproj",
    )(xp, xs, g, w_main, wgh, wgl)


def _attn_heads(q, k, v, bias_ref, key_ok):
    outs = []
    for h in range(A_HEADS):
        sl = slice(h * A_DH, (h + 1) * A_DH)
        s = lax.dot_general(q[:, sl], k[:, sl], (((1,), (1,)), ((), ())), preferred_element_type=f32)
        s = s * ATT_SCALE + bias_ref[h]
        if key_ok is not None:
            s = jnp.where(key_ok, s, NEG)
        m = jnp.max(s, axis=-1, keepdims=True)
        p = jnp.exp(s - m)
        l = jnp.sum(p, axis=-1, keepdims=True)
        o = jnp.dot(p.astype(bf16), v[:, sl], preferred_element_type=f32)
        outs.append(o / l)
    return jnp.concatenate(outs, axis=-1)


def _attn_prompt_kernel(q_ref, k0_ref, k1_ref, v0_ref, v1_ref, bias_ref, o_ref):
    t = pl.program_id(1)
    q = q_ref[...].astype(bf16)
    k = jnp.concatenate([k0_ref[...], k1_ref[...]], axis=0).astype(bf16)
    v = jnp.concatenate([v0_ref[...], v1_ref[...]], axis=0).astype(bf16)
    col = lax.broadcasted_iota(i32, (1, ATT_KEYS), 1)
    for s in range(ATT_TILE // ATT_SUB):
        lo = s * ATT_SUB
        key_ok = (t * ATT_TILE + lo + col) >= WINDOW
        o_ref[lo:lo + ATT_SUB, :] = _attn_heads(q[lo:lo + ATT_SUB], k[lo:lo + ATT_KEYS], v[lo:lo + ATT_KEYS],
                                                 bias_ref, key_ok)


def _attn_sample_kernel(q_ref, k_ref, v_ref, bias_ref, o_ref):
    o_ref[...] = _attn_heads(q_ref[...].astype(bf16), k_ref[0].astype(bf16), v_ref[0].astype(bf16), bias_ref, None)


def _rel_bias_table(rel_bias, rows, cols, offset, valid):
    span = rows + cols - 1
    rel = offset + rows - 1 - jnp.arange(span)
    diag = rel_bias[:, jnp.clip(rel, -MAX_REL, MAX_REL) + MAX_REL].astype(f32)
    diag = jnp.pad(diag, ((0, 0), (0, 1)))
    flat = jnp.tile(diag, (1, rows))[:, rows - 1:rows - 1 + rows * span]
    return jnp.where(valid[None], flat.reshape(-1, rows, span)[:, :, :cols], NEG)


def _attn_prompt(q, k, v, rel_bias, bsz, s):
    assert s % ATT_TILE == 0 and WINDOW == ATT_TILE
    nt = s // ATT_TILE
    i = jnp.arange(ATT_SUB)[:, None]
    j = jnp.arange(ATT_KEYS)[None, :]
    off = j - (i // CHUNK) * CHUNK
    bias = _rel_bias_table(rel_bias, ATT_SUB, ATT_KEYS, WINDOW, (off >= 0) & (off < WINDOW + CHUNK))
    cur = pl.BlockSpec((ATT_TILE, A_WIDTH), lambda b, t: (b * nt + t, 0))
    prev = pl.BlockSpec((ATT_TILE, A_WIDTH), lambda b, t: (b * nt + jnp.maximum(t - 1, 0), 0))
    return pl.pallas_call(
        _attn_prompt_kernel,
        grid=(bsz, nt),
        in_specs=[cur, prev, cur, prev, cur, pl.BlockSpec(bias.shape, lambda b, t: (0, 0, 0))],
        out_specs=cur,
        out_shape=jax.ShapeDtypeStruct((bsz * s, A_WIDTH), f32),
        compiler_params=pltpu.CompilerParams(dimension_semantics=("parallel", "parallel"),
                                             vmem_limit_bytes=VMEM_LIMIT),
        name="attn_prompt",
    )(q, k, k, v, v, bias)


def _attn_sample(q, k, v, ck, cv, rel_bias, row0, bsz, t):
    l = ck.shape[1]
    assert row0 % t == 0
    keys = -(-(l + t) // LANES) * LANES
    padk = ((0, 0), (0, keys - l - t), (0, 0))
    kk = jnp.pad(jnp.concatenate([ck, k[row0:].reshape(bsz, t, A_WIDTH)], axis=1), padk)
    vv = jnp.pad(jnp.concatenate([cv, v[row0:].reshape(bsz, t, A_WIDTH)], axis=1), padk)
    j = jnp.arange(keys)[None, :]
    bias = _rel_bias_table(rel_bias, t, keys, l, jnp.broadcast_to(j < l + t, (t, keys)))
    return pl.pallas_call(
        _attn_sample_kernel,
        grid=(bsz,),
        in_specs=[pl.BlockSpec((t, A_WIDTH), lambda b: (row0 // t + b, 0)),
                  pl.BlockSpec((1, keys, A_WIDTH), lambda b: (b, 0, 0)),
                  pl.BlockSpec((1, keys, A_WIDTH), lambda b: (b, 0, 0)),
                  pl.BlockSpec(bias.shape, lambda b: (0, 0, 0))],
        out_specs=pl.BlockSpec((t, A_WIDTH), lambda b: (b, 0)),
        out_shape=jax.ShapeDtypeStruct((bsz * t, A_WIDTH), f32),
        compiler_params=pltpu.CompilerParams(dimension_semantics=("parallel",), vmem_limit_bytes=VMEM_LIMIT),
        name="attn_sample",
    )(q, kk, vv, bias)


def _mlstm_kernel(qk_ref, v_ref, o_ref, gate_ref, c0_ref, n0_ref, m0_ref, cbuf_ref,
                  cw_ref, cb_ref, bg_ref, mhg_ref,
                  h_ref, cout_ref, nout_ref, mout_ref,
                  c_s, n_s, m_s, prev_s, *, lc):
    c = pl.program_id(1)

    @pl.when(c == 0)
    def _():
        c_s[...] = c0_ref[0]
        n_s[...] = n0_ref[0]
        m_s[...] = m0_ref[0]
        prev_s[...] = cbuf_ref[0]

    a = qk_ref[...]
    ext = jnp.concatenate([prev_s[...], a], axis=0)
    conv = cb_ref[...]
    for j in range(CONV_W):
        lo = SUBLANES - (CONV_W - 1) + j
        conv = conv + cw_ref[j:j + 1, :] * ext[lo:lo + lc]
    prev_s[...] = a[lc - SUBLANES:lc]
    qk = conv * jax.nn.sigmoid(conv)

    z = gate_ref[...] + bg_ref[...]
    lane = lax.broadcasted_iota(i32, (lc, LANES), 1)
    row = lax.broadcasted_iota(i32, (lc, LANES), 0)
    logf = jnp.minimum(z, 0.0) - jnp.log1p(jnp.exp(-jnp.abs(z)))
    cum = jnp.where((lane >= M_HEADS) & (lane < 2 * M_HEADS), logf, 0.0)
    shift = 1
    while shift < lc:
        cum = cum + jnp.where(row >= shift, pltpu.roll(cum, shift, axis=0), 0.0)
        shift *= 2
    zc = jnp.where(lane < M_HEADS, z, cum)
    zt = jnp.concatenate([zc, jnp.zeros((LANES - lc, LANES), f32)], axis=0).T[:, :lc]

    ri = lax.broadcasted_iota(i32, (lc, lc), 0)
    ci = lax.broadcasted_iota(i32, (lc, lc), 1)
    causal = ri >= ci
    vall = v_ref[...]
    hs = []
    for h in range(M_HEADS):
        sl = slice(h * M_DH, (h + 1) * M_DH)
        q = qk[:, sl]
        k = qk[:, M_WIDTH + h * M_DH:M_WIDTH + (h + 1) * M_DH] * (M_DH ** -0.5)
        v = vall[:, sl]
        i_col = zc[:, h:h + 1]
        b_col = zc[:, M_HEADS + h:M_HEADS + h + 1]
        i_row = zt[h:h + 1, :]
        b_row = zt[M_HEADS + h:M_HEADS + h + 1, :]
        m_prev = m_s[h:h + 1, 0:1]
        c_prev = c_s[h]
        n_prev = n_s[h:h + 1, :]

        dmat = jnp.where(causal, b_col - b_row + i_row, NEG)
        inter = b_col + m_prev
        mt = jnp.maximum(jnp.max(dmat, axis=-1, keepdims=True), inter)
        w_intra = jnp.exp(dmat - mt)
        w_inter = jnp.exp(inter - mt)
        qb, kb, vb = q.astype(bf16), k.astype(bf16), v.astype(bf16)
        s = lax.dot_general(qb, kb, (((1,), (1,)), ((), ())), preferred_element_type=f32) * w_intra
        num = (w_inter * jnp.dot(qb, c_prev.astype(bf16), preferred_element_type=f32)
               + jnp.dot(s.astype(bf16), vb, preferred_element_type=f32))
        den = w_inter * jnp.sum(q * n_prev, axis=-1, keepdims=True) + jnp.sum(s, axis=-1, keepdims=True)
        hh = num / jnp.maximum(jnp.abs(den), jnp.exp(-mt))
        m_new = mt[lc - 1:lc, :]
        b_last = b_col[lc - 1:lc, :]
        w_s = jnp.exp(b_last - b_col + i_col - m_new)
        decay = jnp.exp(b_last + m_prev - m_new)
        kw = k * w_s
        c_s[h] = decay * c_prev + lax.dot_general(kw.astype(bf16), vb, (((0,), (0,)), ((), ())),
                                                   preferred_element_type=f32)
        n_s[h:h + 1, :] = decay * n_prev + jnp.sum(kw, axis=0, keepdims=True)
        m_s[h:h + 1, :] = jnp.broadcast_to(m_new, (1, LANES))
        hs.append(hh * lax.rsqrt(jnp.mean(hh * hh, axis=-1, keepdims=True) + EPS))

    h_all = jnp.concatenate(hs, axis=-1)
    h_ref[...] = h_all * mhg_ref[...] * jax.nn.sigmoid(o_ref[...])

    @pl.when(c == pl.num_programs(1) - 1)
    def _():
        cout_ref[0] = c_s[...]
        nout_ref[0] = n_s[...]
        mout_ref[0] = m_s[...]


def _mlstm(mqk, mv, mo, gates, row0, bsz, t, c0, n0, m0, cbuf, conv_w, conv_b, b_gates, mh_g):
    lc = min(CHUNK, t)
    nc = t // lc
    assert t % lc == 0 and lc % SUBLANES == 0 and row0 % lc == 0
    n0p = jnp.pad(n0.astype(f32), ((0, 0), (0, SUBLANES - M_HEADS), (0, 0)))
    m0p = jnp.pad(jnp.broadcast_to(m0.astype(f32)[:, :, None], (bsz, M_HEADS, LANES)),
                  ((0, 0), (0, SUBLANES - M_HEADS), (0, 0)))
    cbp = jnp.pad(cbuf.astype(f32), ((0, 0), (SUBLANES - (CONV_W - 1), 0), (0, 0)))
    bg = jnp.pad(b_gates.astype(f32), (0, LANES - 2 * M_HEADS)).reshape(1, LANES)
    seq = lambda w: pl.BlockSpec((lc, w), lambda b, c: (row0 // lc + b * nc + c, 0))
    out_seq = pl.BlockSpec((lc, M_WIDTH), lambda b, c: (b * nc + c, 0))
    per_b = lambda shp: pl.BlockSpec((1,) + shp, lambda b, c: (b,) + (0,) * len(shp))
    full = lambda a: pl.BlockSpec(a.shape, lambda b, c: (0,) * a.ndim)
    cb = conv_b.reshape(1, -1)
    mhg = mh_g.reshape(1, -1)
    return pl.pallas_call(
        functools.partial(_mlstm_kernel, lc=lc),
        grid=(bsz, nc),
        in_specs=[seq(2 * M_WIDTH), seq(M_WIDTH), seq(M_WIDTH), seq(LANES),
                  per_b((M_HEADS, M_DH, M_DH)), per_b((SUBLANES, M_DH)), per_b((SUBLANES, LANES)),
                  per_b((SUBLANES, 2 * M_WIDTH)),
                  full(conv_w), full(cb), full(bg), full(mhg)],
        out_specs=[out_seq, per_b((M_HEADS, M_DH, M_DH)), per_b((SUBLANES, M_DH)), per_b((SUBLANES, LANES))],
        out_shape=[jax.ShapeDtypeStruct((bsz * t, M_WIDTH), f32),
                   jax.ShapeDtypeStruct((bsz, M_HEADS, M_DH, M_DH), f32),
                   jax.ShapeDtypeStruct((bsz, SUBLANES, M_DH), f32),
                   jax.ShapeDtypeStruct((bsz, SUBLANES, LANES), f32)],
        scratch_shapes=[pltpu.VMEM((M_HEADS, M_DH, M_DH), f32), pltpu.VMEM((SUBLANES, M_DH), f32),
                        pltpu.VMEM((SUBLANES, LANES), f32), pltpu.VMEM((SUBLANES, 2 * M_WIDTH), f32)],
        compiler_params=pltpu.CompilerParams(dimension_semantics=("parallel", "arbitrary"),
                                             vmem_limit_bytes=VMEM_LIMIT),
        name="mlstm",
    )(mqk, mv, mo, gates, c0.astype(f32), n0p, m0p, cbp, conv_w, cb, bg, mhg)


def _topk_rows(s, k, payload=None):
    n = s.shape[0]
    rows = lax.broadcasted_iota(i32, s.shape, 0).astype(f32)
    vals, ids = [], []
    for _ in range(k):
        m = jnp.max(s, axis=0, keepdims=True)
        pos = jnp.min(jnp.where(s == m, rows, float(n)), axis=0, keepdims=True)
        sel = rows == pos
        vals.append(m)
        ids.append(pos if payload is None else jnp.max(jnp.where(sel, payload, -1.0), axis=0, keepdims=True))
        s = jnp.where(sel, -jnp.inf, s)
    return jnp.concatenate(vals, axis=0), jnp.concatenate(ids, axis=0)


def _mid_kernel(xp_ref, xs_ref, attp_ref, atts_ref, hp_ref, hs_ref, wo_ref, g2_ref, wq_ref, keys_ref,
                x1_ref, xn_ref, eidx_ref, gate_ref, *, nbp):
    cat = jnp.concatenate([_pick(nbp, attp_ref, atts_ref), _pick(nbp, hp_ref, hs_ref)], axis=-1).astype(bf16)
    x1 = _pick(nbp, xp_ref, xs_ref) + jnp.dot(cat, wo_ref[...], preferred_element_type=f32)
    x1_ref[...] = x1
    xn = _rms(x1, g2_ref[...])
    xn_ref[...] = xn
    xb = xn.astype(bf16)
    e_rows, g_rows = [], []
    for h in range(P_HEADS):
        q = jnp.dot(xb, wq_ref[:, h * P_DKEY:(h + 1) * P_DKEY], preferred_element_type=f32).astype(bf16)
        half = []
        for p in range(2):
            st = lax.dot_general(keys_ref[h, p], q[:, p * N_KEYS:(p + 1) * N_KEYS], (((1,), (1,)), ((), ())),
                                 preferred_element_type=f32)
            half.append(_topk_rows(st, P_TOPK))
        (v0, i0), (v1, i1) = half
        width = [P_TOPK // (a + 1) for a in range(P_TOPK)]
        fill = -sum(width) % SUBLANES
        cand = jnp.concatenate([v0[a:a + 1] + v1[:width[a]] for a in range(P_TOPK)]
                               + [jnp.full((fill, v0.shape[1]), -jnp.inf, f32)], axis=0)
        cidx = jnp.concatenate([i0[a:a + 1] * float(N_KEYS) + i1[:width[a]] for a in range(P_TOPK)]
                               + [jnp.zeros((fill, v0.shape[1]), f32)], axis=0)
        top_s, eid = _topk_rows(cand, P_TOPK, payload=cidx)
        ex = jnp.exp(top_s - top_s[0:1])
        e_rows.append(eid)
        g_rows.append(ex / jnp.sum(ex, axis=0, keepdims=True))
    eidx_ref[...] = jnp.concatenate(e_rows, axis=0).T.astype(i32)
    gate_ref[...] = jnp.concatenate(g_rows, axis=0).T


def _mid(xp, xs, attp, atts, hp, hs, w_out, g2, wq, keys):
    n = xp.shape[0] + xs.shape[0]
    nbp = xp.shape[0] // ROW_BLOCK
    row = lambda w: pl.BlockSpec((ROW_BLOCK, w), lambda i: (i, 0))
    full = lambda a: pl.BlockSpec(a.shape, lambda i: (0,) * a.ndim)
    wo = w_out.astype(bf16)
    wqb = wq.astype(bf16)
    kb = keys.astype(bf16)
    g = g2.reshape(1, D_MODEL)
    return pl.pallas_call(
        functools.partial(_mid_kernel, nbp=nbp),
        grid=(n // ROW_BLOCK,),
        in_specs=[*_pair_specs(nbp, D_MODEL), *_pair_specs(nbp, A_WIDTH), *_pair_specs(nbp, M_WIDTH),
                  full(wo), full(g), full(wqb), full(kb)],
        out_specs=[row(D_MODEL), row(D_MODEL), row(P_SLOTS), row(P_SLOTS)],
        out_shape=[jax.ShapeDtypeStruct((n, D_MODEL), f32), jax.ShapeDtypeStruct((n, D_MODEL), f32),
                   jax.ShapeDtypeStruct((n, P_SLOTS), i32), jax.ShapeDtypeStruct((n, P_SLOTS), f32)],
        compiler_params=pltpu.CompilerParams(dimension_semantics=("parallel",), vmem_limit_bytes=VMEM_LIMIT),
        name="outproj_retrieve",
    )(xp, xs, attp, atts, hp, hs, wo, g, wqb, kb)


def _gather_rows(eidx_ref, tab_ref, g_ref, t):
    for r in range(P_SLOTS):
        g_ref[pl.ds(r * ROW_TILES, ROW_TILES), :] = tab_ref[eidx_ref[t, r]]


def _tile_rows(t, rows=SUBLANES):
    return pl.ds(pl.multiple_of(t * rows, rows), rows)


def _pipelined_tokens(nt, eidx_ref, tab_ref, g0_s, g1_s, compute):
    bufs = (g0_s, g1_s)
    _gather_rows(eidx_ref, tab_ref, g0_s, 0)

    def body(j, carry):
        for u in range(TOKEN_UNROLL):
            t = TOKEN_UNROLL * j + u
            _gather_rows(eidx_ref, tab_ref, bufs[(u + 1) % 2], jnp.minimum(t + 1, nt - 1))
            compute(t, bufs[u % 2])
        return carry

    lax.fori_loop(0, nt // TOKEN_UNROLL, body, 0)


def _gelu_tanh(x):
    return 0.5 * x * (1.0 + jnp.tanh(math.sqrt(2.0 / math.pi) * (x + 0.044715 * (x * x * x))))


def _peer_u_kernel(eidx_ref, xn_ref, g_ref, tab_ref, w_ref, xl_s, r_s, g0_s, g1_s):
    nt = xn_ref.shape[0]
    xn = xn_ref[...]
    xh = xn.astype(bf16).astype(f32)
    xl = xn - xh
    for k in range(ROW_TILES):
        xl_s[pl.ds(k, nt, stride=2 * SUBLANES), :] = xh[:, k * LANES:(k + 1) * LANES]
        xl_s[pl.ds(SUBLANES + k, nt, stride=2 * SUBLANES), :] = xl[:, k * LANES:(k + 1) * LANES]
    diag = (lax.broadcasted_iota(i32, (SUBLANES, G_ROWS), 1) % ROW_TILES
            == lax.broadcasted_iota(i32, (SUBLANES, G_ROWS), 0))

    def compute(t, g_s):
        lhs = xl_s[_tile_rows(t, 2 * SUBLANES), :].astype(bf16)
        out = lax.dot_general(lhs, g_s[...], (((1,), (1,)), ((), ())), preferred_element_type=f32)
        part = jnp.where(diag, out[:SUBLANES] + out[SUBLANES:], 0.0)
        for c in range(ROW_TILES):
            r_s[c, _tile_rows(t), :] = part[:, c * LANES:(c + 1) * LANES]

    _pipelined_tokens(nt, eidx_ref, tab_ref, g0_s, g1_s, compute)
    cols = []
    for c in range(ROW_TILES):
        acc = r_s[c, pl.ds(0, nt, stride=SUBLANES), :]
        for k in range(1, SUBLANES):
            acc = acc + r_s[c, pl.ds(k, nt, stride=SUBLANES), :]
        cols.append(acc)
    s = jnp.concatenate(cols, axis=-1)
    fold = (lax.broadcasted_iota(i32, (G_ROWS, P_SLOTS), 0) // ROW_TILES
            == lax.broadcasted_iota(i32, (G_ROWS, P_SLOTS), 1)).astype(bf16)
    sh, sl = _split_bf16(s)
    act = jnp.dot(sh, fold, preferred_element_type=f32) + jnp.dot(sl, fold, preferred_element_type=f32)
    w_ref[...] = g_ref[...] * _gelu_tanh(act)


def _peer_v_kernel(eidx_ref, w_ref, tab_ref, peer_ref, wl_s, o_s, g0_s, g1_s):
    nt = w_ref.shape[0]
    spread = (lax.broadcasted_iota(i32, (P_SLOTS, G_ROWS), 1) // ROW_TILES
              == lax.broadcasted_iota(i32, (P_SLOTS, G_ROWS), 0)).astype(bf16)
    wexp = jnp.dot(w_ref[...].astype(bf16), spread, preferred_element_type=f32)
    lane = lax.broadcasted_iota(i32, (nt, LANES), 1)
    for c in range(ROW_TILES):
        wc = wexp[:, c * LANES:(c + 1) * LANES]
        for k in range(SUBLANES):
            wl_s[c, pl.ds(k, nt, stride=SUBLANES), :] = jnp.where(lane % ROW_TILES == k, wc, 0.0)

    def compute(t, g_s):
        lhs = jnp.concatenate([wl_s[c, _tile_rows(t), :] for c in range(ROW_TILES)], axis=-1).astype(bf16)
        o_s[_tile_rows(t), :] = jnp.dot(lhs, g_s[...], preferred_element_type=f32)

    _pipelined_tokens(nt, eidx_ref, tab_ref, g0_s, g1_s, compute)
    for k in range(ROW_TILES):
        peer_ref[:, k * LANES:(k + 1) * LANES] = o_s[pl.ds(k, nt, stride=SUBLANES), :]


def _final_kernel(pa_ref, pb_ref, x1_ref, gf_ref, yp_ref, ys_ref, *, nba, nbp):
    y = _rms(x1_ref[...] + _pick(nba, pa_ref, pb_ref), gf_ref[...])

    @pl.when(pl.program_id(0) < nbp)
    def _():
        yp_ref[...] = y

    @pl.when(pl.program_id(0) >= nbp)
    def _():
        ys_ref[...] = y


def _final(peer_a, peer_b, x1, gf, n_p):
    n = x1.shape[0]
    nba, nbp = peer_a.shape[0] // ROW_BLOCK, n_p // ROW_BLOCK
    g = gf.reshape(1, D_MODEL)
    return pl.pallas_call(
        functools.partial(_final_kernel, nba=nba, nbp=nbp),
        grid=(n // ROW_BLOCK,),
        in_specs=[*_pair_specs(nba, D_MODEL), pl.BlockSpec((ROW_BLOCK, D_MODEL), lambda i: (i, 0)),
                  pl.BlockSpec(g.shape, lambda i: (0, 0))],
        out_specs=list(_pair_specs(nbp, D_MODEL)),
        out_shape=[jax.ShapeDtypeStruct((n_p, D_MODEL), f32), jax.ShapeDtypeStruct((n - n_p, D_MODEL), f32)],
        compiler_params=pltpu.CompilerParams(dimension_semantics=("arbitrary",), vmem_limit_bytes=VMEM_LIMIT),
        name="final_norm",
    )(peer_a, peer_b, x1, g)


def _peer_specs():
    row = lambda w: pl.BlockSpec((ROW_BLOCK, w), lambda i: (i, 0))
    idx = pl.BlockSpec((ROW_BLOCK, P_SLOTS), lambda i: (i, 0), memory_space=pltpu.SMEM)
    tab = pl.BlockSpec(memory_space=pltpu.VMEM)
    gscr = pltpu.VMEM((G_ROWS, LANES), bf16)
    params = pltpu.CompilerParams(dimension_semantics=("arbitrary",), vmem_limit_bytes=VMEM_LIMIT)
    return row, idx, tab, gscr, params


def _expert_table(tab):
    return tab.astype(bf16).reshape(tab.shape[0], ROW_TILES, LANES)


def _peer_u(eidx, xn, g, utab, n):
    row, idx, tab, gscr, params = _peer_specs()
    return pl.pallas_call(
        _peer_u_kernel,
        grid=(n // ROW_BLOCK,),
        in_specs=[idx, row(D_MODEL), row(P_SLOTS), tab],
        out_specs=row(P_SLOTS),
        out_shape=jax.ShapeDtypeStruct((n, P_SLOTS), f32),
        scratch_shapes=[pltpu.VMEM((ROW_BLOCK * 2 * SUBLANES, LANES), f32),
                        pltpu.VMEM((ROW_TILES, ROW_BLOCK * SUBLANES, LANES), f32), gscr, gscr],
        compiler_params=params,
        name="peer_u",
    )(eidx, xn, g, utab)


def _peer_v(eidx, w, vtab):
    n = w.shape[0]
    row, idx, tab, gscr, params = _peer_specs()
    return pl.pallas_call(
        _peer_v_kernel,
        grid=(n // ROW_BLOCK,),
        in_specs=[idx, row(P_SLOTS), tab],
        out_specs=row(D_MODEL),
        out_shape=jax.ShapeDtypeStruct((n, D_MODEL), f32),
        scratch_shapes=[pltpu.VMEM((ROW_TILES, ROW_BLOCK * SUBLANES, LANES), f32),
                        pltpu.VMEM((ROW_BLOCK * SUBLANES, LANES), f32), gscr, gscr],
        compiler_params=params,
        name="peer_v",
    )(eidx, w, vtab)


def _sc_table(tab):
    e = tab.shape[0]
    t = tab.astype(bf16).reshape(e, SC_CHUNKS, 2, SC_LANES).transpose(0, 1, 3, 2)
    return lax.bitcast_convert_type(t, i32).reshape(e, SC_WORDS)


def _sc_token_loop(row0, per, tab_hbm, eidx_hbm, tok_srcs, idx_v, tok_bufs, rows_v, sems, compute, finish):
    base = (lax.axis_index("s") * 2 + lax.axis_index("c")) * per
    units = 2 * per

    def load_token(tok, tslot):
        pltpu.sync_copy(eidx_hbm.at[row0 + base + tok], idx_v.at[tslot])
        for (src, first), buf in zip(tok_srcs, tok_bufs):
            pltpu.sync_copy(src.at[first + base + tok], buf.at[tslot])

    def gather(u):
        tok, half = u // 2, u % 2
        return pltpu.make_async_copy(tab_hbm.at[idx_v.at[tok % 2, pl.ds(half * SC_UNIT_ROWS, SC_UNIT_ROWS)]],
                                     rows_v.at[half], sems.at[half])

    load_token(0, 0)
    gather(0).start()

    @pl.loop(0, units)
    def _(u):
        tok, half = u // 2, u % 2

        @pl.when(jnp.logical_and(half == 1, u + 1 < units))
        def _():
            load_token(tok + 1, (tok + 1) % 2)

        @pl.when(u + 1 < units)
        def _():
            gather(u + 1).start()

        gather(u).wait()
        compute(tok % 2, half, rows_v.at[half])

        @pl.when(half == 1)
        def _():
            finish(base + tok)


def _sc_unpack(words):
    return plsc.unpack(plsc.bitcast(words, bf16), format=plsc.PackFormat.INTERLEAVED)


def _sc_call(body, out_width, n_sc, tok_width, name):
    mesh = plsc.VectorSubcoreMesh(core_axis_name="c", subcore_axis_name="s")
    return pl.kernel(
        body, mesh=mesh,
        out_type=jax.ShapeDtypeStruct((n_sc, out_width), f32),
        scratch_types=[pltpu.VMEM((2, P_SLOTS), i32), pltpu.VMEM((2, tok_width), f32),
                       pltpu.VMEM((2, SC_UNIT_ROWS, SC_WORDS), i32), pltpu.VMEM((out_width,), f32),
                       pltpu.SemaphoreType.DMA((2,))],
        compiler_params=pltpu.CompilerParams(needs_layout_passes=False),
        name=name,
    )


def _sc_peer_u(tab32, eidx, xn, row0, n_sc):
    per = n_sc // SC_WORKERS

    def body(tab_hbm, eidx_hbm, x_hbm, out_hbm, idx_v, x_v, rows_v, acc_v, sems):
        def compute(tslot, half, rows):
            @pl.loop(0, SC_UNIT_ROWS // SC_ROW_GROUP)
            def _(rg):
                accs = [jnp.zeros((SC_LANES,), f32) for _ in range(SC_ROW_GROUP)]
                for c in range(SC_CHUNKS):
                    xa = x_v[tslot, pl.ds(2 * c * SC_LANES, SC_LANES)]
                    xb = x_v[tslot, pl.ds((2 * c + 1) * SC_LANES, SC_LANES)]
                    for r in range(SC_ROW_GROUP):
                        a, b = _sc_unpack(rows[rg * SC_ROW_GROUP + r, pl.ds(c * SC_LANES, SC_LANES)])
                        accs[r] = accs[r] + a * xa + b * xb
                for r in range(SC_ROW_GROUP):
                    acc_v[pl.ds((half * SC_UNIT_ROWS + rg * SC_ROW_GROUP + r) * SC_LANES, SC_LANES)] = accs[r]

        def finish(t):
            pltpu.sync_copy(acc_v, out_hbm.at[t])

        _sc_token_loop(row0, per, tab_hbm, eidx_hbm, ((x_hbm, row0),), idx_v, (x_v,), rows_v, sems, compute, finish)

    return _sc_call(body, P_SLOTS * SC_LANES, n_sc, D_MODEL, "sc_peer_u")(tab32, eidx, xn)


def _sc_peer_v(tab32, eidx, w16, row0, n_sc):
    per = n_sc // SC_WORKERS

    def body(tab_hbm, eidx_hbm, w_hbm, out_hbm, idx_v, w_v, rows_v, out_v, sems):
        def compute(tslot, half, rows):
            @pl.loop(0, SC_CHUNKS // SC_CHUNK_GROUP)
            def _(cg):
                keep = (half == 1).astype(f32)
                first = cg * SC_CHUNK_GROUP
                accs = [out_v[pl.ds((2 * first + j) * SC_LANES, SC_LANES)] * keep for j in range(2 * SC_CHUNK_GROUP)]
                for r in range(SC_UNIT_ROWS):
                    wv = w_v[tslot, pl.ds((half * SC_UNIT_ROWS + r) * SC_LANES, SC_LANES)]
                    for c in range(SC_CHUNK_GROUP):
                        a, b = _sc_unpack(rows[r, pl.ds((first + c) * SC_LANES, SC_LANES)])
                        accs[2 * c] = accs[2 * c] + a * wv
                        accs[2 * c + 1] = accs[2 * c + 1] + b * wv
                for j in range(2 * SC_CHUNK_GROUP):
                    out_v[pl.ds((2 * first + j) * SC_LANES, SC_LANES)] = accs[j]

        def finish(t):
            pltpu.sync_copy(out_v, out_hbm.at[t])

        _sc_token_loop(row0, per, tab_hbm, eidx_hbm, ((w_hbm, 0),), idx_v, (w_v,), rows_v, sems, compute, finish)

    return _sc_call(body, D_MODEL, n_sc, P_SLOTS * SC_LANES, "sc_peer_v")(tab32, eidx, w16)


def _sc_gate_kernel(part_ref, g_ref, w16_ref):
    width = P_SLOTS * SC_LANES
    fold = (lax.broadcasted_iota(i32, (width, P_SLOTS), 0) // SC_LANES
            == lax.broadcasted_iota(i32, (width, P_SLOTS), 1)).astype(bf16)
    ph, plo = _split_bf16(part_ref[...])
    act = jnp.dot(ph, fold, preferred_element_type=f32) + jnp.dot(plo, fold, preferred_element_type=f32)
    w = g_ref[...] * _gelu_tanh(act)
    spread = (lax.broadcasted_iota(i32, (P_SLOTS, width), 1) // SC_LANES
              == lax.broadcasted_iota(i32, (P_SLOTS, width), 0)).astype(bf16)
    wh, wlo = _split_bf16(w)
    w16_ref[...] = jnp.dot(wh, spread, preferred_element_type=f32) + jnp.dot(wlo, spread, preferred_element_type=f32)


def _sc_gate(part, gate, row0):
    n_sc, width = part.shape
    assert row0 % ROW_BLOCK == 0
    return pl.pallas_call(
        _sc_gate_kernel,
        grid=(n_sc // ROW_BLOCK,),
        in_specs=[pl.BlockSpec((ROW_BLOCK, width), lambda i: (i, 0)),
                  pl.BlockSpec((ROW_BLOCK, P_SLOTS), lambda i: (row0 // ROW_BLOCK + i, 0))],
        out_specs=pl.BlockSpec((ROW_BLOCK, width), lambda i: (i, 0)),
        out_shape=jax.ShapeDtypeStruct((n_sc, width), f32),
        compiler_params=pltpu.CompilerParams(dimension_semantics=("parallel",), vmem_limit_bytes=VMEM_LIMIT),
        name="sc_gate",
    )(part, gate)


def kernel(x_prompt, x_sample, cache_k, cache_v, state_C, state_n, state_m, state_conv, norm1_g, w_in, b_gates, rel_bias, conv_w, conv_b, mh_norm_g, w_out, norm2_g, peer_wq, peer_keys, peer_u, peer_v, final_g):
    bp, sp, d = x_prompt.shape
    bs, ts, _ = x_sample.shape
    n_p, n_s = bp * sp, bs * ts
    n = n_p + n_s
    assert n_p % ROW_BLOCK == 0 and n_s % ROW_BLOCK == 0 and d == D_MODEL
    depth = w_in.shape[0]
    assert depth == 1, "the final norm is fused into the last layer's PEER pass"
    l = 0
    xp, xs = x_prompt.reshape(n_p, d), x_sample.reshape(n_s, d)

    aq, ak, av, mqk, mv, mo, gates = _inproj(xp, xs, norm1_g[l], w_in[l])
    att_p = _attn_prompt(aq, ak, av, rel_bias[l], bp, sp)
    lcache = cache_k.shape[2]
    att_s = _attn_sample(aq, ak, av, cache_k[l].reshape(bs, lcache, A_WIDTH),
                         cache_v[l].reshape(bs, lcache, A_WIDTH), rel_bias[l], n_p, bs, ts)

    zeros = lambda *shp: jnp.zeros(shp, f32)
    mparams = (conv_w[l], conv_b[l], b_gates[l], mh_norm_g[l])
    h_p, c_p, nn_p, mm_p = _mlstm(mqk, mv, mo, gates, 0, bp, sp, zeros(bp, M_HEADS, M_DH, M_DH),
                                  zeros(bp, M_HEADS, M_DH), zeros(bp, M_HEADS), zeros(bp, CONV_W - 1, 2 * M_WIDTH),
                                  *mparams)
    h_s, c_s, nn_s, mm_s = _mlstm(mqk, mv, mo, gates, n_p, bs, ts, state_C[l], state_n[l], state_m[l],
                                  state_conv[l], *mparams)

    x1, xn2, eidx, gate = _mid(xp, xs, att_p, att_s, h_p, h_s, w_out[l], norm2_g[l], peer_wq[l], peer_keys[l])
    n_sc = SC_TOKENS
    n_tc = n - n_sc
    assert n_tc % ROW_BLOCK == 0 and n_sc % (SC_WORKERS * SUBLANES) == 0 and n_sc % ROW_BLOCK == 0
    part = _sc_peer_u(_sc_table(peer_u[l]), eidx, xn2, n_tc, n_sc)
    w = _peer_u(eidx, xn2, gate, _expert_table(peer_u[l]), n_tc)
    peer_sc = _sc_peer_v(_sc_table(peer_v[l]), eidx, _sc_gate(part, gate, n_tc), n_tc, n_sc)
    peer_tc = _peer_v(eidx, w, _expert_table(peer_v[l]))
    y_p, y_s = _final(peer_tc, peer_sc, x1, final_g, n_p)

    def tail(a, row0, bsz, t, keep):
        return jnp.stack([a[row0 + (b + 1) * t - keep:row0 + (b + 1) * t] for b in range(bsz)])

    keep = min(WINDOW, sp)
    heads = lambda a: a.reshape(a.shape[0], a.shape[1], A_HEADS, A_DH)
    ctail = CONV_W - 1
    conv_tail = lambda buf, a, row0, bsz, t: jnp.concatenate([buf.astype(a.dtype), tail(a, row0, bsz, t, min(ctail, t))],
                                                             axis=1)[:, -ctail:]
    st = lambda a: a[None]
    return (y_p.reshape(bp, sp, d), y_s.reshape(bs, ts, d),
            st(heads(tail(ak, 0, bp, sp, keep))), st(heads(tail(av, 0, bp, sp, keep))),
            st(c_p), st(nn_p[:, :M_HEADS]), st(mm_p[:, :M_HEADS, 0]),
            st(conv_tail(zeros(bp, ctail, 2 * M_WIDTH), mqk, 0, bp, sp)),
            st(heads(ak[n_p:].reshape(bs, ts, A_WIDTH))), st(heads(av[n_p:].reshape(bs, ts, A_WIDTH))),
            st(c_s), st(nn_s[:, :M_HEADS]), st(mm_s[:, :M_HEADS, 0]),
            st(conv_tail(state_conv[l], mqk, n_p, bs, ts)))
```

```python
import functools
import math

import jax
import jax.numpy as jnp
from jax import lax
from jax.experimental import pallas as pl
from jax.experimental.pallas import tpu as pltpu
from jax.experimental.pallas import tpu_sc as plsc

f32 = jnp.float32
bf16 = jnp.bfloat16
i32 = jnp.int32

D_MODEL = 1024
CHUNK = 64
A_HEADS = 8
A_DH = 64
A_WIDTH = A_HEADS * A_DH
BAND_CHUNKS = 8
WINDOW = BAND_CHUNKS * CHUNK
MAX_REL = 128
ATT_SCALE = A_DH ** -0.5
M_HEADS = 4
M_DH = 128
M_WIDTH = M_HEADS * M_DH
CONV_W = 4
P_HEADS = 8
P_DKEY = 256
N_KEYS = 128
P_TOPK = 16
P_SLOTS = P_HEADS * P_TOPK
EPS = 1e-6
NEG = -1e30

LANES = 128
SUBLANES = 8
ROW_BLOCK = 256
ATT_TILE = 512
ATT_SUB = 128
ATT_KEYS = ATT_SUB + WINDOW
ROW_TILES = D_MODEL // LANES
G_ROWS = P_SLOTS * ROW_TILES
TOKEN_UNROLL = 8
VMEM_LIMIT = 56 * 1024 * 1024

SC_WORKERS = 32
SC_LANES = 16
SC_UNIT_ROWS = P_SLOTS // 2
SC_ROW_GROUP = 16
SC_WORDS = D_MODEL // 2
SC_CHUNKS = SC_WORDS // SC_LANES
SC_CHUNK_GROUP = 8
SC_TOKENS = 15616


def _rms(x, g):
    return x * lax.rsqrt(jnp.mean(x * x, axis=-1, keepdims=True) + EPS) * g


def _split_bf16(x):
    hi = x.astype(bf16)
    lo = (x - hi.astype(f32)).astype(bf16)
    return hi, lo


def _pair_specs(nbp, width):
    return (pl.BlockSpec((ROW_BLOCK, width), lambda i: (jnp.minimum(i, nbp - 1), 0)),
            pl.BlockSpec((ROW_BLOCK, width), lambda i: (jnp.maximum(i - nbp, 0), 0)))


def _pick(nbp, p_ref, s_ref):
    return jnp.where(pl.program_id(0) < nbp, p_ref[...], s_ref[...])


def _inproj_kernel(xp_ref, xs_ref, g_ref, w_ref, wgh_ref, wgl_ref,
                   aq_ref, ak_ref, av_ref, mqk_ref, mv_ref, mo_ref, gate_ref, *, nbp):
    xn = _rms(_pick(nbp, xp_ref, xs_ref), g_ref[...])
    xh, xl = _split_bf16(xn)

    def proj(lo, hi):
        return jnp.dot(xh, w_ref[:, lo:hi], preferred_element_type=f32)

    aq_ref[...] = proj(0, 512)
    ak_ref[...] = proj(512, 1024)
    av_ref[...] = proj(1024, 1536)
    mqk_ref[...] = proj(1536, 2560)
    mv_ref[...] = proj(2560, 3072)
    mo_ref[...] = proj(3072, 3584)
    gate_ref[...] = (jnp.dot(xh, wgh_ref[...], preferred_element_type=f32)
                     + jnp.dot(xl, wgh_ref[...], preferred_element_type=f32)
                     + jnp.dot(xh, wgl_ref[...], preferred_element_type=f32))


def _inproj(xp, xs, g1, w_in):
    n = xp.shape[0] + xs.shape[0]
    nbp = xp.shape[0] // ROW_BLOCK
    main = 3 * A_WIDTH + 4 * M_WIDTH
    w_main = w_in[:, :main].astype(bf16)
    wg = jnp.pad(w_in[:, main:], ((0, 0), (0, LANES - 2 * M_HEADS)))
    wgh, wgl = _split_bf16(wg)
    widths = (512, 512, 512, 1024, 512, 512, LANES)
    row = lambda w: pl.BlockSpec((ROW_BLOCK, w), lambda i: (i, 0))
    full = lambda a: pl.BlockSpec(a.shape, lambda i: (0,) * a.ndim)
    g = g1.reshape(1, D_MODEL)
    return pl.pallas_call(
        functools.partial(_inproj_kernel, nbp=nbp),
        grid=(n // ROW_BLOCK,),
        in_specs=[*_pair_specs(nbp, D_MODEL), full(g), full(w_main), full(wgh), full(wgl)],
        out_specs=[row(w) for w in widths],
        out_shape=[jax.ShapeDtypeStruct((n, w), f32) for w in widths],
        compiler_params=pltpu.CompilerParams(dimension_semantics=("parallel",), vmem_limit_bytes=VMEM_LIMIT),
        name="inproj",
    )(xp, xs, g, w_main, wgh, wgl)


def _attn_heads(q, k, v, bias_ref, key_ok):
    outs = []
    for h in range(A_HEADS):
        sl = slice(h * A_DH, (h + 1) * A_DH)
        s = lax.dot_general(q[:, sl], k[:, sl], (((1,), (1,)), ((), ())), preferred_element_type=f32)
        s = s * ATT_SCALE + bias_ref[h]
        if key_ok is not None:
            s = jnp.where(key_ok, s, NEG)
        m = jnp.max(s, axis=-1, keepdims=True)
        p = jnp.exp(s - m)
        l = jnp.sum(p, axis=-1, keepdims=True)
        o = jnp.dot(p.astype(bf16), v[:, sl], preferred_element_type=f32)
        outs.append(o / l)
    return jnp.concatenate(outs, axis=-1)


def _attn_prompt_kernel(q_ref, k0_ref, k1_ref, v0_ref, v1_ref, bias_ref, o_ref):
    t = pl.program_id(1)
    q = q_ref[...].astype(bf16)
    k = jnp.concatenate([k0_ref[...], k1_ref[...]], axis=0).astype(bf16)
    v = jnp.concatenate([v0_ref[...], v1_ref[...]], axis=0).astype(bf16)
    col = lax.broadcasted_iota(i32, (1, ATT_KEYS), 1)
    for s in range(ATT_TILE // ATT_SUB):
        lo = s * ATT_SUB
        key_ok = (t * ATT_TILE + lo + col) >= WINDOW
        o_ref[lo:lo + ATT_SUB, :] = _attn_heads(q[lo:lo + ATT_SUB], k[lo:lo + ATT_KEYS], v[lo:lo + ATT_KEYS],
                                                 bias_ref, key_ok)


def _attn_sample_kernel(q_ref, k_ref, v_ref, bias_ref, o_ref):
    o_ref[...] = _attn_heads(q_ref[...].astype(bf16), k_ref[0].astype(bf16), v_ref[0].astype(bf16), bias_ref, None)


def _rel_bias_table(rel_bias, rows, cols, offset, valid):
    span = rows + cols - 1
    rel = offset + rows - 1 - jnp.arange(span)
    diag = rel_bias[:, jnp.clip(rel, -MAX_REL, MAX_REL) + MAX_REL].astype(f32)
    diag = jnp.pad(diag, ((0, 0), (0, 1)))
    flat = jnp.tile(diag, (1, rows))[:, rows - 1:rows - 1 + rows * span]
    return jnp.where(valid[None], flat.reshape(-1, rows, span)[:, :, :cols], NEG)


def _attn_prompt(q, k, v, rel_bias, bsz, s):
    assert s % ATT_TILE == 0 and WINDOW == ATT_TILE
    nt = s // ATT_TILE
    i = jnp.arange(ATT_SUB)[:, None]
    j = jnp.arange(ATT_KEYS)[None, :]
    off = j - (i // CHUNK) * CHUNK
    bias = _rel_bias_table(rel_bias, ATT_SUB, ATT_KEYS, WINDOW, (off >= 0) & (off < WINDOW + CHUNK))
    cur = pl.BlockSpec((ATT_TILE, A_WIDTH), lambda b, t: (b * nt + t, 0))
    prev = pl.BlockSpec((ATT_TILE, A_WIDTH), lambda b, t: (b * nt + jnp.maximum(t - 1, 0), 0))
    return pl.pallas_call(
        _attn_prompt_kernel,
        grid=(bsz, nt),
        in_specs=[cur, prev, cur, prev, cur, pl.BlockSpec(bias.shape, lambda b, t: (0, 0, 0))],
        out_specs=cur,
        out_shape=jax.ShapeDtypeStruct((bsz * s, A_WIDTH), f32),
        compiler_params=pltpu.CompilerParams(dimension_semantics=("parallel", "parallel"),
                                             vmem_limit_bytes=VMEM_LIMIT),
        name="attn_prompt",
    )(q, k, k, v, v, bias)


def _attn_sample(q, k, v, ck, cv, rel_bias, row0, bsz, t):
    l = ck.shape[1]
    assert row0 % t == 0
    keys = -(-(l + t) // LANES) * LANES
    padk = ((0, 0), (0, keys - l - t), (0, 0))
    kk = jnp.pad(jnp.concatenate([ck, k[row0:].reshape(bsz, t, A_WIDTH)], axis=1), padk)
    vv = jnp.pad(jnp.concatenate([cv, v[row0:].reshape(bsz, t, A_WIDTH)], axis=1), padk)
    j = jnp.arange(keys)[None, :]
    bias = _rel_bias_table(rel_bias, t, keys, l, jnp.broadcast_to(j < l + t, (t, keys)))
    return pl.pallas_call(
        _attn_sample_kernel,
        grid=(bsz,),
        in_specs=[pl.BlockSpec((t, A_WIDTH), lambda b: (row0 // t + b, 0)),
                  pl.BlockSpec((1, keys, A_WIDTH), lambda b: (b, 0, 0)),
                  pl.BlockSpec((1, keys, A_WIDTH), lambda b: (b, 0, 0)),
                  pl.BlockSpec(bias.shape, lambda b: (0, 0, 0))],
        out_specs=pl.BlockSpec((t, A_WIDTH), lambda b: (b, 0)),
        out_shape=jax.ShapeDtypeStruct((bsz * t, A_WIDTH), f32),
        compiler_params=pltpu.CompilerParams(dimension_semantics=("parallel",), vmem_limit_bytes=VMEM_LIMIT),
        name="attn_sample",
    )(q, kk, vv, bias)


def _mlstm_kernel(qk_ref, v_ref, o_ref, gate_ref, c0_ref, n0_ref, m0_ref, cbuf_ref,
                  cw_ref, cb_ref, bg_ref, mhg_ref,
                  h_ref, cout_ref, nout_ref, mout_ref,
                  c_s, n_s, m_s, prev_s, *, lc):
    c = pl.program_id(1)

    @pl.when(c == 0)
    def _():
        c_s[...] = c0_ref[0]
        n_s[...] = n0_ref[0]
        m_s[...] = m0_ref[0]
        prev_s[...] = cbuf_ref[0]

    a = qk_ref[...]
    ext = jnp.concatenate([prev_s[...], a], axis=0)
    conv = cb_ref[...]
    for j in range(CONV_W):
        lo = SUBLANES - (CONV_W - 1) + j
        conv = conv + cw_ref[j:j + 1, :] * ext[lo:lo + lc]
    prev_s[...] = a[lc - SUBLANES:lc]
    qk = conv * jax.nn.sigmoid(conv)

    z = gate_ref[...] + bg_ref[...]
    lane = lax.broadcasted_iota(i32, (lc, LANES), 1)
    row = lax.broadcasted_iota(i32, (lc, LANES), 0)
    logf = jnp.minimum(z, 0.0) - jnp.log1p(jnp.exp(-jnp.abs(z)))
    cum = jnp.where((lane >= M_HEADS) & (lane < 2 * M_HEADS), logf, 0.0)
    shift = 1
    while shift < lc:
        cum = cum + jnp.where(row >= shift, pltpu.roll(cum, shift, axis=0), 0.0)
        shift *= 2
    zc = jnp.where(lane < M_HEADS, z, cum)
    zt = jnp.concatenate([zc, jnp.zeros((LANES - lc, LANES), f32)], axis=0).T[:, :lc]

    ri = lax.broadcasted_iota(i32, (lc, lc), 0)
    ci = lax.broadcasted_iota(i32, (lc, lc), 1)
    causal = ri >= ci
    vall = v_ref[...]
    hs = []
    for h in range(M_HEADS):
        sl = slice(h * M_DH, (h + 1) * M_DH)
        q = qk[:, sl]
        k = qk[:, M_WIDTH + h * M_DH:M_WIDTH + (h + 1) * M_DH] * (M_DH ** -0.5)
        v = vall[:, sl]
        i_col = zc[:, h:h + 1]
        b_col = zc[:, M_HEADS + h:M_HEADS + h + 1]
        i_row = zt[h:h + 1, :]
        b_row = zt[M_HEADS + h:M_HEADS + h + 1, :]
        m_prev = m_s[h:h + 1, 0:1]
        c_prev = c_s[h]
        n_prev = n_s[h:h + 1, :]

        dmat = jnp.where(causal, b_col - b_row + i_row, NEG)
        inter = b_col + m_prev
        mt = jnp.maximum(jnp.max(dmat, axis=-1, keepdims=True), inter)
        w_intra = jnp.exp(dmat - mt)
        w_inter = jnp.exp(inter - mt)
        qb, kb, vb = q.astype(bf16), k.astype(bf16), v.astype(bf16)
        s = lax.dot_general(qb, kb, (((1,), (1,)), ((), ())), preferred_element_type=f32) * w_intra
        num = (w_inter * jnp.dot(qb, c_prev.astype(bf16), preferred_element_type=f32)
               + jnp.dot(s.astype(bf16), vb, preferred_element_type=f32))
        den = w_inter * jnp.sum(q * n_prev, axis=-1, keepdims=True) + jnp.sum(s, axis=-1, keepdims=True)
        hh = num / jnp.maximum(jnp.abs(den), jnp.exp(-mt))
        m_new = mt[lc - 1:lc, :]
        b_last = b_col[lc - 1:lc, :]
        w_s = jnp.exp(b_last - b_col + i_col - m_new)
        decay = jnp.exp(b_last + m_prev - m_new)
        kw = k * w_s
        c_s[h] = decay * c_prev + lax.dot_general(kw.astype(bf16), vb, (((0,), (0,)), ((), ())),
                                                   preferred_element_type=f32)
        n_s[h:h + 1, :] = decay * n_prev + jnp.sum(kw, axis=0, keepdims=True)
        m_s[h:h + 1, :] = jnp.broadcast_to(m_new, (1, LANES))
        hs.append(hh * lax.rsqrt(jnp.mean(hh * hh, axis=-1, keepdims=True) + EPS))

    h_all = jnp.concatenate(hs, axis=-1)
    h_ref[...] = h_all * mhg_ref[...] * jax.nn.sigmoid(o_ref[...])

    @pl.when(c == pl.num_programs(1) - 1)
    def _():
        cout_ref[0] = c_s[...]
        nout_ref[0] = n_s[...]
        mout_ref[0] = m_s[...]


def _mlstm(mqk, mv, mo, gates, row0, bsz, t, c0, n0, m0, cbuf, conv_w, conv_b, b_gates, mh_g):
    lc = min(CHUNK, t)
    nc = t // lc
    assert t % lc == 0 and lc % SUBLANES == 0 and row0 % lc == 0
    n0p = jnp.pad(n0.astype(f32), ((0, 0), (0, SUBLANES - M_HEADS), (0, 0)))
    m0p = jnp.pad(jnp.broadcast_to(m0.astype(f32)[:, :, None], (bsz, M_HEADS, LANES)),
                  ((0, 0), (0, SUBLANES - M_HEADS), (0, 0)))
    cbp = jnp.pad(cbuf.astype(f32), ((0, 0), (SUBLANES - (CONV_W - 1), 0), (0, 0)))
    bg = jnp.pad(b_gates.astype(f32), (0, LANES - 2 * M_HEADS)).reshape(1, LANES)
    seq = lambda w: pl.BlockSpec((lc, w), lambda b, c: (row0 // lc + b * nc + c, 0))
    out_seq = pl.BlockSpec((lc, M_WIDTH), lambda b, c: (b * nc + c, 0))
    per_b = lambda shp: pl.BlockSpec((1,) + shp, lambda b, c: (b,) + (0,) * len(shp))
    full = lambda a: pl.BlockSpec(a.shape, lambda b, c: (0,) * a.ndim)
    cb = conv_b.reshape(1, -1)
    mhg = mh_g.reshape(1, -1)
    return pl.pallas_call(
        functools.partial(_mlstm_kernel, lc=lc),
        grid=(bsz, nc),
        in_specs=[seq(2 * M_WIDTH), seq(M_WIDTH), seq(M_WIDTH), seq(LANES),
                  per_b((M_HEADS, M_DH, M_DH)), per_b((SUBLANES, M_DH)), per_b((SUBLANES, LANES)),
                  per_b((SUBLANES, 2 * M_WIDTH)),
                  full(conv_w), full(cb), full(bg), full(mhg)],
        out_specs=[out_seq, per_b((M_HEADS, M_DH, M_DH)), per_b((SUBLANES, M_DH)), per_b((SUBLANES, LANES))],
        out_shape=[jax.ShapeDtypeStruct((bsz * t, M_WIDTH), f32),
                   jax.ShapeDtypeStruct((bsz, M_HEADS, M_DH, M_DH), f32),
                   jax.ShapeDtypeStruct((bsz, SUBLANES, M_DH), f32),
                   jax.ShapeDtypeStruct((bsz, SUBLANES, LANES), f32)],
        scratch_shapes=[pltpu.VMEM((M_HEADS, M_DH, M_DH), f32), pltpu.VMEM((SUBLANES, M_DH), f32),
                        pltpu.VMEM((SUBLANES, LANES), f32), pltpu.VMEM((SUBLANES, 2 * M_WIDTH), f32)],
        compiler_params=pltpu.CompilerParams(dimension_semantics=("parallel", "arbitrary"),
                                             vmem_limit_bytes=VMEM_LIMIT),
        name="mlstm",
    )(mqk, mv, mo, gates, c0.astype(f32), n0p, m0p, cbp, conv_w, cb, bg, mhg)


def _topk_rows(s, k, payload=None):
    n = s.shape[0]
    rows = lax.broadcasted_iota(i32, s.shape, 0).astype(f32)
    vals, ids = [], []
    for _ in range(k):
        m = jnp.max(s, axis=0, keepdims=True)
        pos = jnp.min(jnp.where(s == m, rows, float(n)), axis=0, keepdims=True)
        sel = rows == pos
        vals.append(m)
        ids.append(pos if payload is None else jnp.max(jnp.where(sel, payload, -1.0), axis=0, keepdims=True))
        s = jnp.where(sel, -jnp.inf, s)
    return jnp.concatenate(vals, axis=0), jnp.concatenate(ids, axis=0)


def _mid_kernel(xp_ref, xs_ref, attp_ref, atts_ref, hp_ref, hs_ref, wo_ref, g2_ref, wq_ref, keys_ref,
                x1_ref, xn_ref, eidx_ref, gate_ref, *, nbp):
    cat = jnp.concatenate([_pick(nbp, attp_ref, atts_ref), _pick(nbp, hp_ref, hs_ref)], axis=-1).astype(bf16)
    x1 = _pick(nbp, xp_ref, xs_ref) + jnp.dot(cat, wo_ref[...], preferred_element_type=f32)
    x1_ref[...] = x1
    xn = _rms(x1, g2_ref[...])
    xn_ref[...] = xn
    xb = xn.astype(bf16)
    e_rows, g_rows = [], []
    for h in range(P_HEADS):
        q = jnp.dot(xb, wq_ref[:, h * P_DKEY:(h + 1) * P_DKEY], preferred_element_type=f32).astype(bf16)
        half = []
        for p in range(2):
            st = lax.dot_general(keys_ref[h, p], q[:, p * N_KEYS:(p + 1) * N_KEYS], (((1,), (1,)), ((), ())),
                                 preferred_element_type=f32)
            half.append(_topk_rows(st, P_TOPK))
        (v0, i0), (v1, i1) = half
        width = [P_TOPK // (a + 1) for a in range(P_TOPK)]
        fill = -sum(width) % SUBLANES
        cand = jnp.concatenate([v0[a:a + 1] + v1[:width[a]] for a in range(P_TOPK)]
                               + [jnp.full((fill, v0.shape[1]), -jnp.inf, f32)], axis=0)
        cidx = jnp.concatenate([i0[a:a + 1] * float(N_KEYS) + i1[:width[a]] for a in range(P_TOPK)]
                               + [jnp.zeros((fill, v0.shape[1]), f32)], axis=0)
        top_s, eid = _topk_rows(cand, P_TOPK, payload=cidx)
        ex = jnp.exp(top_s - top_s[0:1])
        e_rows.append(eid)
        g_rows.append(ex / jnp.sum(ex, axis=0, keepdims=True))
    eidx_ref[...] = jnp.concatenate(e_rows, axis=0).T.astype(i32)
    gate_ref[...] = jnp.concatenate(g_rows, axis=0).T


def _mid(xp, xs, attp, atts, hp, hs, w_out, g2, wq, keys):
    n = xp.shape[0] + xs.shape[0]
    nbp = xp.shape[0] // ROW_BLOCK
    row = lambda w: pl.BlockSpec((ROW_BLOCK, w), lambda i: (i, 0))
    full = lambda a: pl.BlockSpec(a.shape, lambda i: (0,) * a.ndim)
    wo = w_out.astype(bf16)
    wqb = wq.astype(bf16)
    kb = keys.astype(bf16)
    g = g2.reshape(1, D_MODEL)
    return pl.pallas_call(
        functools.partial(_mid_kernel, nbp=nbp),
        grid=(n // ROW_BLOCK,),
        in_specs=[*_pair_specs(nbp, D_MODEL), *_pair_specs(nbp, A_WIDTH), *_pair_specs(nbp, M_WIDTH),
                  full(wo), full(g), full(wqb), full(kb)],
        out_specs=[row(D_MODEL), row(D_MODEL), row(P_SLOTS), row(P_SLOTS)],
        out_shape=[jax.ShapeDtypeStruct((n, D_MODEL), f32), jax.ShapeDtypeStruct((n, D_MODEL), f32),
                   jax.ShapeDtypeStruct((n, P_SLOTS), i32), jax.ShapeDtypeStruct((n, P_SLOTS), f32)],
        compiler_params=pltpu.CompilerParams(dimension_semantics=("parallel",), vmem_limit_bytes=VMEM_LIMIT),
        name="outproj_retrieve",
    )(xp, xs, attp, atts, hp, hs, wo, g, wqb, kb)


def _gather_rows(eidx_ref, tab_ref, g_ref, t):
    for r in range(P_SLOTS):
        g_ref[pl.ds(r * ROW_TILES, ROW_TILES), :] = tab_ref[eidx_ref[t, r]]


def _tile_rows(t, rows=SUBLANES):
    return pl.ds(pl.multiple_of(t * rows, rows), rows)


def _pipelined_tokens(nt, eidx_ref, tab_ref, g0_s, g1_s, compute):
    bufs = (g0_s, g1_s)
    _gather_rows(eidx_ref, tab_ref, g0_s, 0)

    def body(j, carry):
        for u in range(TOKEN_UNROLL):
            t = TOKEN_UNROLL * j + u
            _gather_rows(eidx_ref, tab_ref, bufs[(u + 1) % 2], jnp.minimum(t + 1, nt - 1))
            compute(t, bufs[u % 2])
        return carry

    lax.fori_loop(0, nt // TOKEN_UNROLL, body, 0)


def _gelu_tanh(x):
    return 0.5 * x * (1.0 + jnp.tanh(math.sqrt(2.0 / math.pi) * (x + 0.044715 * (x * x * x))))


def _peer_u_kernel(eidx_ref, xn_ref, g_ref, tab_ref, w_ref, xl_s, r_s, g0_s, g1_s):
    nt = xn_ref.shape[0]
    xn = xn_ref[...]
    xh = xn.astype(bf16).astype(f32)
    xl = xn - xh
    for k in range(ROW_TILES):
        xl_s[pl.ds(k, nt, stride=2 * SUBLANES), :] = xh[:, k * LANES:(k + 1) * LANES]
        xl_s[pl.ds(SUBLANES + k, nt, stride=2 * SUBLANES), :] = xl[:, k * LANES:(k + 1) * LANES]
    diag = (lax.broadcasted_iota(i32, (SUBLANES, G_ROWS), 1) % ROW_TILES
            == lax.broadcasted_iota(i32, (SUBLANES, G_ROWS), 0))

    def compute(t, g_s):
        lhs = xl_s[_tile_rows(t, 2 * SUBLANES), :].astype(bf16)
        out = lax.dot_general(lhs, g_s[...], (((1,), (1,)), ((), ())), preferred_element_type=f32)
        part = jnp.where(diag, out[:SUBLANES] + out[SUBLANES:], 0.0)
        for c in range(ROW_TILES):
            r_s[c, _tile_rows(t), :] = part[:, c * LANES:(c + 1) * LANES]

    _pipelined_tokens(nt, eidx_ref, tab_ref, g0_s, g1_s, compute)
    cols = []
    for c in range(ROW_TILES):
        acc = r_s[c, pl.ds(0, nt, stride=SUBLANES), :]
        for k in range(1, SUBLANES):
            acc = acc + r_s[c, pl.ds(k, nt, stride=SUBLANES), :]
        cols.append(acc)
    s = jnp.concatenate(cols, axis=-1)
    fold = (lax.broadcasted_iota(i32, (G_ROWS, P_SLOTS), 0) // ROW_TILES
            == lax.broadcasted_iota(i32, (G_ROWS, P_SLOTS), 1)).astype(bf16)
    sh, sl = _split_bf16(s)
    act = jnp.dot(sh, fold, preferred_element_type=f32) + jnp.dot(sl, fold, preferred_element_type=f32)
    w_ref[...] = g_ref[...] * _gelu_tanh(act)


def _peer_v_kernel(eidx_ref, w_ref, tab_ref, peer_ref, wl_s, o_s, g0_s, g1_s):
    nt = w_ref.shape[0]
    spread = (lax.broadcasted_iota(i32, (P_SLOTS, G_ROWS), 1) // ROW_TILES
              == lax.broadcasted_iota(i32, (P_SLOTS, G_ROWS), 0)).astype(bf16)
    wexp = jnp.dot(w_ref[...].astype(bf16), spread, preferred_element_type=f32)
    lane = lax.broadcasted_iota(i32, (nt, LANES), 1)
    for c in range(ROW_TILES):
        wc = wexp[:, c * LANES:(c + 1) * LANES]
        for k in range(SUBLANES):
            wl_s[c, pl.ds(k, nt, stride=SUBLANES), :] = jnp.where(lane % ROW_TILES == k, wc, 0.0)

    def compute(t, g_s):
        lhs = jnp.concatenate([wl_s[c, _tile_rows(t), :] for c in range(ROW_TILES)], axis=-1).astype(bf16)
        o_s[_tile_rows(t), :] = jnp.dot(lhs, g_s[...], preferred_element_type=f32)

    _pipelined_tokens(nt, eidx_ref, tab_ref, g0_s, g1_s, compute)
    for k in range(ROW_TILES):
        peer_ref[:, k * LANES:(k + 1) * LANES] = o_s[pl.ds(k, nt, stride=SUBLANES), :]


def _final_kernel(pa_ref, pb_ref, x1_ref, gf_ref, yp_ref, ys_ref, *, nba, nbp):
    y = _rms(x1_ref[...] + _pick(nba, pa_ref, pb_ref), gf_ref[...])

    @pl.when(pl.program_id(0) < nbp)
    def _():
        yp_ref[...] = y

    @pl.when(pl.program_id(0) >= nbp)
    def _():
        ys_ref[...] = y


def _final(peer_a, peer_b, x1, gf, n_p):
    n = x1.shape[0]
    nba, nbp = peer_a.shape[0] // ROW_BLOCK, n_p // ROW_BLOCK
    g = gf.reshape(1, D_MODEL)
    return pl.pallas_call(
        functools.partial(_final_kernel, nba=nba, nbp=nbp),
        grid=(n // ROW_BLOCK,),
        in_specs=[*_pair_specs(nba, D_MODEL), pl.BlockSpec((ROW_BLOCK, D_MODEL), lambda i: (i, 0)),
                  pl.BlockSpec(g.shape, lambda i: (0, 0))],
        out_specs=list(_pair_specs(nbp, D_MODEL)),
        out_shape=[jax.ShapeDtypeStruct((n_p, D_MODEL), f32), jax.ShapeDtypeStruct((n - n_p, D_MODEL), f32)],
        compiler_params=pltpu.CompilerParams(dimension_semantics=("arbitrary",), vmem_limit_bytes=VMEM_LIMIT),
        name="final_norm",
    )(peer_a, peer_b, x1, g)


def _peer_specs():
    row = lambda w: pl.BlockSpec((ROW_BLOCK, w), lambda i: (i, 0))
    idx = pl.BlockSpec((ROW_BLOCK, P_SLOTS), lambda i: (i, 0), memory_space=pltpu.SMEM)
    tab = pl.BlockSpec(memory_space=pltpu.VMEM)
    gscr = pltpu.VMEM((G_ROWS, LANES), bf16)
    params = pltpu.CompilerParams(dimension_semantics=("arbitrary",), vmem_limit_bytes=VMEM_LIMIT)
    return row, idx, tab, gscr, params


def _expert_table(tab):
    return tab.astype(bf16).reshape(tab.shape[0], ROW_TILES, LANES)


def _peer_u(eidx, xn, g, utab, n):
    row, idx, tab, gscr, params = _peer_specs()
    return pl.pallas_call(
        _peer_u_kernel,
        grid=(n // ROW_BLOCK,),
        in_specs=[idx, row(D_MODEL), row(P_SLOTS), tab],
        out_specs=row(P_SLOTS),
        out_shape=jax.ShapeDtypeStruct((n, P_SLOTS), f32),
        scratch_shapes=[pltpu.VMEM((ROW_BLOCK * 2 * SUBLANES, LANES), f32),
                        pltpu.VMEM((ROW_TILES, ROW_BLOCK * SUBLANES, LANES), f32), gscr, gscr],
        compiler_params=params,
        name="peer_u",
    )(eidx, xn, g, utab)


def _peer_v(eidx, w, vtab):
    n = w.shape[0]
    row, idx, tab, gscr, params = _peer_specs()
    return pl.pallas_call(
        _peer_v_kernel,
        grid=(n // ROW_BLOCK,),
        in_specs=[idx, row(P_SLOTS), tab],
        out_specs=row(D_MODEL),
        out_shape=jax.ShapeDtypeStruct((n, D_MODEL), f32),
        scratch_shapes=[pltpu.VMEM((ROW_TILES, ROW_BLOCK * SUBLANES, LANES), f32),
                        pltpu.VMEM((ROW_BLOCK * SUBLANES, LANES), f32), gscr, gscr],
        compiler_params=params,
        name="peer_v",
    )(eidx, w, vtab)


def _sc_table(tab):
    e = tab.shape[0]
    t = tab.astype(bf16).reshape(e, SC_CHUNKS, 2, SC_LANES).transpose(0, 1, 3, 2)
    return lax.bitcast_convert_type(t, i32).reshape(e, SC_WORDS)


def _sc_token_loop(row0, per, tab_hbm, eidx_hbm, tok_srcs, idx_v, tok_bufs, rows_v, sems, compute, finish):
    base = (lax.axis_index("s") * 2 + lax.axis_index("c")) * per
    units = 2 * per

    def load_token(tok, tslot):
        pltpu.sync_copy(eidx_hbm.at[row0 + base + tok], idx_v.at[tslot])
        for (src, first), buf in zip(tok_srcs, tok_bufs):
            pltpu.sync_copy(src.at[first + base + tok], buf.at[tslot])

    def gather(u):
        tok, half = u // 2, u % 2
        return pltpu.make_async_copy(tab_hbm.at[idx_v.at[tok % 2, pl.ds(half * SC_UNIT_ROWS, SC_UNIT_ROWS)]],
                                     rows_v.at[half], sems.at[half])

    load_token(0, 0)
    gather(0).start()

    @pl.loop(0, units)
    def _(u):
        tok, half = u // 2, u % 2

        @pl.when(jnp.logical_and(half == 1, u + 1 < units))
        def _():
            load_token(tok + 1, (tok + 1) % 2)

        @pl.when(u + 1 < units)
        def _():
            gather(u + 1).start()

        gather(u).wait()
        compute(tok % 2, half, rows_v.at[half])

        @pl.when(half == 1)
        def _():
            finish(base + tok)


def _sc_unpack(words):
    return plsc.unpack(plsc.bitcast(words, bf16), format=plsc.PackFormat.INTERLEAVED)


def _sc_call(body, out_width, n_sc, tok_width, name):
    mesh = plsc.VectorSubcoreMesh(core_axis_name="c", subcore_axis_name="s")
    return pl.kernel(
        body, mesh=mesh,
        out_type=jax.ShapeDtypeStruct((n_sc, out_width), f32),
        scratch_types=[pltpu.VMEM((2, P_SLOTS), i32), pltpu.VMEM((2, tok_width), f32),
                       pltpu.VMEM((2, SC_UNIT_ROWS, SC_WORDS), i32), pltpu.VMEM((out_width,), f32),
                       pltpu.SemaphoreType.DMA((2,))],
        compiler_params=pltpu.CompilerParams(needs_layout_passes=False),
        name=name,
    )


def _sc_peer_u(tab32, eidx, xn, row0, n_sc):
    per = n_sc // SC_WORKERS

    def body(tab_hbm, eidx_hbm, x_hbm, out_hbm, idx_v, x_v, rows_v, acc_v, sems):
        def compute(tslot, half, rows):
            @pl.loop(0, SC_UNIT_ROWS // SC_ROW_GROUP)
            def _(rg):
                accs = [jnp.zeros((SC_LANES,), f32) for _ in range(SC_ROW_GROUP)]
                for c in range(SC_CHUNKS):
                    xa = x_v[tslot, pl.ds(2 * c * SC_LANES, SC_LANES)]
                    xb = x_v[tslot, pl.ds((2 * c + 1) * SC_LANES, SC_LANES)]
                    for r in range(SC_ROW_GROUP):
                        a, b = _sc_unpack(rows[rg * SC_ROW_GROUP + r, pl.ds(c * SC_LANES, SC_LANES)])
                        accs[r] = accs[r] + a * xa + b * xb
                for r in range(SC_ROW_GROUP):
                    acc_v[pl.ds((half * SC_UNIT_ROWS + rg * SC_ROW_GROUP + r) * SC_LANES, SC_LANES)] = accs[r]

        def finish(t):
            pltpu.sync_copy(acc_v, out_hbm.at[t])

        _sc_token_loop(row0, per, tab_hbm, eidx_hbm, ((x_hbm, row0),), idx_v, (x_v,), rows_v, sems, compute, finish)

    return _sc_call(body, P_SLOTS * SC_LANES, n_sc, D_MODEL, "sc_peer_u")(tab32, eidx, xn)


def _sc_peer_v(tab32, eidx, w16, row0, n_sc):
    per = n_sc // SC_WORKERS

    def body(tab_hbm, eidx_hbm, w_hbm, out_hbm, idx_v, w_v, rows_v, out_v, sems):
        def compute(tslot, half, rows):
            @pl.loop(0, SC_CHUNKS // SC_CHUNK_GROUP)
            def _(cg):
                keep = (half == 1).astype(f32)
                first = cg * SC_CHUNK_GROUP
                accs = [out_v[pl.ds((2 * first + j) * SC_LANES, SC_LANES)] * keep for j in range(2 * SC_CHUNK_GROUP)]
                for r in range(SC_UNIT_ROWS):
                    wv = w_v[tslot, pl.ds((half * SC_UNIT_ROWS + r) * SC_LANES, SC_LANES)]
                    for c in range(SC_CHUNK_GROUP):
                        a, b = _sc_unpack(rows[r, pl.ds((first + c) * SC_LANES, SC_LANES)])
                        accs[2 * c] = accs[2 * c] + a * wv
                        accs[2 * c + 1] = accs[2 * c + 1] + b * wv
                for j in range(2 * SC_CHUNK_GROUP):
                    out_v[pl.ds((2 * first + j) * SC_LANES, SC_LANES)] = accs[j]

        def finish(t):
            pltpu.sync_copy(out_v, out_hbm.at[t])

        _sc_token_loop(row0, per, tab_hbm, eidx_hbm, ((w_hbm, 0),), idx_v, (w_v,), rows_v, sems, compute, finish)

    return _sc_call(body, D_MODEL, n_sc, P_SLOTS * SC_LANES, "sc_peer_v")(tab32, eidx, w16)


def _sc_gate_kernel(part_ref, g_ref, after_ref, w16_ref):
    del after_ref
    width = P_SLOTS * SC_LANES
    fold = (lax.broadcasted_iota(i32, (width, P_SLOTS), 0) // SC_LANES
            == lax.broadcasted_iota(i32, (width, P_SLOTS), 1)).astype(bf16)
    ph, plo = _split_bf16(part_ref[...])
    act = jnp.dot(ph, fold, preferred_element_type=f32) + jnp.dot(plo, fold, preferred_element_type=f32)
    w = g_ref[...] * _gelu_tanh(act)
    spread = (lax.broadcasted_iota(i32, (P_SLOTS, width), 1) // SC_LANES
              == lax.broadcasted_iota(i32, (P_SLOTS, width), 0)).astype(bf16)
    wh, wlo = _split_bf16(w)
    w16_ref[...] = jnp.dot(wh, spread, preferred_element_type=f32) + jnp.dot(wlo, spread, preferred_element_type=f32)


def _sc_gate(part, gate, row0, after):
    n_sc, width = part.shape
    assert row0 % ROW_BLOCK == 0
    return pl.pallas_call(
        _sc_gate_kernel,
        grid=(n_sc // ROW_BLOCK,),
        in_specs=[pl.BlockSpec((ROW_BLOCK, width), lambda i: (i, 0)),
                  pl.BlockSpec((ROW_BLOCK, P_SLOTS), lambda i: (row0 // ROW_BLOCK + i, 0)),
                  pl.BlockSpec((SUBLANES, P_SLOTS), lambda i: (0, 0))],
        out_specs=pl.BlockSpec((ROW_BLOCK, width), lambda i: (i, 0)),
        out_shape=jax.ShapeDtypeStruct((n_sc, width), f32),
        compiler_params=pltpu.CompilerParams(dimension_semantics=("parallel",), vmem_limit_bytes=VMEM_LIMIT),
        name="sc_gate",
    )(part, gate, after)


def kernel(x_prompt, x_sample, cache_k, cache_v, state_C, state_n, state_m, state_conv, norm1_g, w_in, b_gates, rel_bias, conv_w, conv_b, mh_norm_g, w_out, norm2_g, peer_wq, peer_keys, peer_u, peer_v, final_g):
    bp, sp, d = x_prompt.shape
    bs, ts, _ = x_sample.shape
    n_p, n_s = bp * sp, bs * ts
    n = n_p + n_s
    assert n_p % ROW_BLOCK == 0 and n_s % ROW_BLOCK == 0 and d == D_MODEL
    depth = w_in.shape[0]
    assert depth == 1, "the final norm is fused into the last layer's PEER pass"
    l = 0
    xp, xs = x_prompt.reshape(n_p, d), x_sample.reshape(n_s, d)

    aq, ak, av, mqk, mv, mo, gates = _inproj(xp, xs, norm1_g[l], w_in[l])
    att_p = _attn_prompt(aq, ak, av, rel_bias[l], bp, sp)
    lcache = cache_k.shape[2]
    att_s = _attn_sample(aq, ak, av, cache_k[l].reshape(bs, lcache, A_WIDTH),
                         cache_v[l].reshape(bs, lcache, A_WIDTH), rel_bias[l], n_p, bs, ts)

    zeros = lambda *shp: jnp.zeros(shp, f32)
    mparams = (conv_w[l], conv_b[l], b_gates[l], mh_norm_g[l])
    h_p, c_p, nn_p, mm_p = _mlstm(mqk, mv, mo, gates, 0, bp, sp, zeros(bp, M_HEADS, M_DH, M_DH),
                                  zeros(bp, M_HEADS, M_DH), zeros(bp, M_HEADS), zeros(bp, CONV_W - 1, 2 * M_WIDTH),
                                  *mparams)
    h_s, c_s, nn_s, mm_s = _mlstm(mqk, mv, mo, gates, n_p, bs, ts, state_C[l], state_n[l], state_m[l],
                                  state_conv[l], *mparams)

    x1, xn2, eidx, gate = _mid(xp, xs, att_p, att_s, h_p, h_s, w_out[l], norm2_g[l], peer_wq[l], peer_keys[l])
    n_sc = SC_TOKENS
    n_tc = n - n_sc
    assert n_tc % ROW_BLOCK == 0 and n_sc % (SC_WORKERS * SUBLANES) == 0 and n_sc % ROW_BLOCK == 0
    part = _sc_peer_u(_sc_table(peer_u[l]), eidx, xn2, n_tc, n_sc)
    w = _peer_u(eidx, xn2, gate, _expert_table(peer_u[l]), n_tc)
    peer_sc = _sc_peer_v(_sc_table(peer_v[l]), eidx, _sc_gate(part, gate, n_tc, w), n_tc, n_sc)
    peer_tc = _peer_v(eidx, w, _expert_table(peer_v[l]))
    y_p, y_s = _final(peer_tc, peer_sc, x1, final_g, n_p)

    def tail(a, row0, bsz, t, keep):
        return jnp.stack([a[row0 + (b + 1) * t - keep:row0 + (b + 1) * t] for b in range(bsz)])

    keep = min(WINDOW, sp)
    heads = lambda a: a.reshape(a.shape[0], a.shape[1], A_HEADS, A_DH)
    ctail = CONV_W - 1
    conv_tail = lambda buf, a, row0, bsz, t: jnp.concatenate([buf.astype(a.dtype), tail(a, row0, bsz, t, min(ctail, t))],
                                                             axis=1)[:, -ctail:]
    st = lambda a: a[None]
    return (y_p.reshape(bp, sp, d), y_s.reshape(bs, ts, d),
            st(heads(tail(ak, 0, bp, sp, keep))), st(heads(tail(av, 0, bp, sp, keep))),
            st(c_p), st(nn_p[:, :M_HEADS]), st(mm_p[:, :M_HEADS, 0]),
            st(conv_tail(zeros(bp, ctail, 2 * M_WIDTH), mqk, 0, bp, sp)),
            st(heads(ak[n_p:].reshape(bs, ts, A_WIDTH))), st(heads(av[n_p:].reshape(bs, ts, A_WIDTH))),
            st(c_s), st(nn_s[:, :M_HEADS]), st(mm_s[:, :M_HEADS, 0]),
            st(conv_tail(state_conv[l], mqk, n_p, bs, ts)))
```

```python
import functools
import math

import jax
import jax.numpy as jnp
from jax import lax
from jax.experimental import pallas as pl
from jax.experimental.pallas import tpu as pltpu
from jax.experimental.pallas import tpu_sc as plsc

f32 = jnp.float32
bf16 = jnp.bfloat16
i32 = jnp.int32

D_MODEL = 1024
CHUNK = 64
A_HEADS = 8
A_DH = 64
A_WIDTH = A_HEADS * A_DH
BAND_CHUNKS = 8
WINDOW = BAND_CHUNKS * CHUNK
MAX_REL = 128
ATT_SCALE = A_DH ** -0.5
M_HEADS = 4
M_DH = 128
M_WIDTH = M_HEADS * M_DH
CONV_W = 4
P_HEADS = 8
P_DKEY = 256
N_KEYS = 128
P_TOPK = 16
P_SLOTS = P_HEADS * P_TOPK
EPS = 1e-6
NEG = -1e30

LANES = 128
SUBLANES = 8
ROW_BLOCK = 256
ATT_TILE = 512
ATT_SUB = 128
ATT_KEYS = ATT_SUB + WINDOW
ROW_TILES = D_MODEL // LANES
G_ROWS = P_SLOTS * ROW_TILES
TOKEN_UNROLL = 8
VMEM_LIMIT = 56 * 1024 * 1024

SC_WORKERS = 32
SC_LANES = 16
SC_UNIT_ROWS = P_SLOTS // 2
SC_ROW_GROUP = 16
SC_WORDS = D_MODEL // 2
SC_CHUNKS = SC_WORDS // SC_LANES
SC_CHUNK_GROUP = 8
SC_PROMPT_SEQS = 1


def _rms(x, g):
    return x * lax.rsqrt(jnp.mean(x * x, axis=-1, keepdims=True) + EPS) * g


def _split_bf16(x):
    hi = x.astype(bf16)
    lo = (x - hi.astype(f32)).astype(bf16)
    return hi, lo


def _pair_specs(nbp, width, first=0):
    return (pl.BlockSpec((ROW_BLOCK, width), lambda i: (first + jnp.minimum(i, nbp - 1), 0)),
            pl.BlockSpec((ROW_BLOCK, width), lambda i: (jnp.maximum(i - nbp, 0), 0)))


def _pick(nbp, p_ref, s_ref):
    return jnp.where(pl.program_id(0) < nbp, p_ref[...], s_ref[...])


def _inproj_kernel(xp_ref, xs_ref, g_ref, w_ref, wgh_ref, wgl_ref,
                   aq_ref, ak_ref, av_ref, mqk_ref, mv_ref, mo_ref, gate_ref, *, nbp):
    xn = _rms(_pick(nbp, xp_ref, xs_ref), g_ref[...])
    xh, xl = _split_bf16(xn)

    def proj(lo, hi):
        return jnp.dot(xh, w_ref[:, lo:hi], preferred_element_type=f32)

    aq_ref[...] = proj(0, 512)
    ak_ref[...] = proj(512, 1024)
    av_ref[...] = proj(1024, 1536)
    mqk_ref[...] = proj(1536, 2560)
    mv_ref[...] = proj(2560, 3072)
    mo_ref[...] = proj(3072, 3584)
    gate_ref[...] = (jnp.dot(xh, wgh_ref[...], preferred_element_type=f32)
                     + jnp.dot(xl, wgh_ref[...], preferred_element_type=f32)
                     + jnp.dot(xh, wgl_ref[...], preferred_element_type=f32))


def _inproj(xp, xs, g1, w_in):
    n = xp.shape[0] + xs.shape[0]
    nbp = xp.shape[0] // ROW_BLOCK
    main = 3 * A_WIDTH + 4 * M_WIDTH
    w_main = w_in[:, :main].astype(bf16)
    wg = jnp.pad(w_in[:, main:], ((0, 0), (0, LANES - 2 * M_HEADS)))
    wgh, wgl = _split_bf16(wg)
    widths = (512, 512, 512, 1024, 512, 512, LANES)
    row = lambda w: pl.BlockSpec((ROW_BLOCK, w), lambda i: (i, 0))
    full = lambda a: pl.BlockSpec(a.shape, lambda i: (0,) * a.ndim)
    g = g1.reshape(1, D_MODEL)
    return pl.pallas_call(
        functools.partial(_inproj_kernel, nbp=nbp),
        grid=(n // ROW_BLOCK,),
        in_specs=[*_pair_specs(nbp, D_MODEL), full(g), full(w_main), full(wgh), full(wgl)],
        out_specs=[row(w) for w in widths],
        out_shape=[jax.ShapeDtypeStruct((n, w), f32) for w in widths],
        compiler_params=pltpu.CompilerParams(dimension_semantics=("parallel",), vmem_limit_bytes=VMEM_LIMIT),
        name="inproj",
    )(xp, xs, g, w_main, wgh, wgl)


def _attn_heads(q, k, v, bias_ref, key_ok):
    outs = []
    for h in range(A_HEADS):
        sl = slice(h * A_DH, (h + 1) * A_DH)
        s = lax.dot_general(q[:, sl], k[:, sl], (((1,), (1,)), ((), ())), preferred_element_type=f32)
        s = s * ATT_SCALE + bias_ref[h]
        if key_ok is not None:
            s = jnp.where(key_ok, s, NEG)
        m = jnp.max(s, axis=-1, keepdims=True)
        p = jnp.exp(s - m)
        l = jnp.sum(p, axis=-1, keepdims=True)
        o = jnp.dot(p.astype(bf16), v[:, sl], preferred_element_type=f32)
        outs.append(o / l)
    return jnp.concatenate(outs, axis=-1)


def _attn_prompt_kernel(q_ref, k0_ref, k1_ref, v0_ref, v1_ref, bias_ref, o_ref):
    t = pl.program_id(1)
    q = q_ref[...].astype(bf16)
    k = jnp.concatenate([k0_ref[...], k1_ref[...]], axis=0).astype(bf16)
    v = jnp.concatenate([v0_ref[...], v1_ref[...]], axis=0).astype(bf16)
    col = lax.broadcasted_iota(i32, (1, ATT_KEYS), 1)
    for s in range(ATT_TILE // ATT_SUB):
        lo = s * ATT_SUB
        key_ok = (t * ATT_TILE + lo + col) >= WINDOW
        o_ref[lo:lo + ATT_SUB, :] = _attn_heads(q[lo:lo + ATT_SUB], k[lo:lo + ATT_KEYS], v[lo:lo + ATT_KEYS],
                                                 bias_ref, key_ok)


def _attn_sample_kernel(q_ref, k_ref, v_ref, bias_ref, o_ref):
    o_ref[...] = _attn_heads(q_ref[...].astype(bf16), k_ref[0].astype(bf16), v_ref[0].astype(bf16), bias_ref, None)


def _rel_bias_table(rel_bias, rows, cols, offset, valid):
    span = rows + cols - 1
    rel = offset + rows - 1 - jnp.arange(span)
    diag = rel_bias[:, jnp.clip(rel, -MAX_REL, MAX_REL) + MAX_REL].astype(f32)
    diag = jnp.pad(diag, ((0, 0), (0, 1)))
    flat = jnp.tile(diag, (1, rows))[:, rows - 1:rows - 1 + rows * span]
    return jnp.where(valid[None], flat.reshape(-1, rows, span)[:, :, :cols], NEG)


def _attn_prompt(q, k, v, rel_bias, b0, bsz, s):
    assert s % ATT_TILE == 0 and WINDOW == ATT_TILE
    nt = s // ATT_TILE
    i = jnp.arange(ATT_SUB)[:, None]
    j = jnp.arange(ATT_KEYS)[None, :]
    off = j - (i // CHUNK) * CHUNK
    bias = _rel_bias_table(rel_bias, ATT_SUB, ATT_KEYS, WINDOW, (off >= 0) & (off < WINDOW + CHUNK))
    cur = pl.BlockSpec((ATT_TILE, A_WIDTH), lambda b, t: ((b0 + b) * nt + t, 0))
    prev = pl.BlockSpec((ATT_TILE, A_WIDTH), lambda b, t: ((b0 + b) * nt + jnp.maximum(t - 1, 0), 0))
    return pl.pallas_call(
        _attn_prompt_kernel,
        grid=(bsz, nt),
        in_specs=[cur, prev, cur, prev, cur, pl.BlockSpec(bias.shape, lambda b, t: (0, 0, 0))],
        out_specs=pl.BlockSpec((ATT_TILE, A_WIDTH), lambda b, t: (b * nt + t, 0)),
        out_shape=jax.ShapeDtypeStruct((bsz * s, A_WIDTH), f32),
        compiler_params=pltpu.CompilerParams(dimension_semantics=("parallel", "parallel"),
                                             vmem_limit_bytes=VMEM_LIMIT),
        name="attn_prompt",
    )(q, k, k, v, v, bias)


def _attn_sample(q, k, v, ck, cv, rel_bias, row0, bsz, t):
    l = ck.shape[1]
    assert row0 % t == 0
    keys = -(-(l + t) // LANES) * LANES
    padk = ((0, 0), (0, keys - l - t), (0, 0))
    kk = jnp.pad(jnp.concatenate([ck, k[row0:].reshape(bsz, t, A_WIDTH)], axis=1), padk)
    vv = jnp.pad(jnp.concatenate([cv, v[row0:].reshape(bsz, t, A_WIDTH)], axis=1), padk)
    j = jnp.arange(keys)[None, :]
    bias = _rel_bias_table(rel_bias, t, keys, l, jnp.broadcast_to(j < l + t, (t, keys)))
    return pl.pallas_call(
        _attn_sample_kernel,
        grid=(bsz,),
        in_specs=[pl.BlockSpec((t, A_WIDTH), lambda b: (row0 // t + b, 0)),
                  pl.BlockSpec((1, keys, A_WIDTH), lambda b: (b, 0, 0)),
                  pl.BlockSpec((1, keys, A_WIDTH), lambda b: (b, 0, 0)),
                  pl.BlockSpec(bias.shape, lambda b: (0, 0, 0))],
        out_specs=pl.BlockSpec((t, A_WIDTH), lambda b: (b, 0)),
        out_shape=jax.ShapeDtypeStruct((bsz * t, A_WIDTH), f32),
        compiler_params=pltpu.CompilerParams(dimension_semantics=("parallel",), vmem_limit_bytes=VMEM_LIMIT),
        name="attn_sample",
    )(q, kk, vv, bias)


def _mlstm_kernel(qk_ref, v_ref, o_ref, gate_ref, c0_ref, n0_ref, m0_ref, cbuf_ref,
                  cw_ref, cb_ref, bg_ref, mhg_ref,
                  h_ref, cout_ref, nout_ref, mout_ref,
                  c_s, n_s, m_s, prev_s, *, lc):
    c = pl.program_id(1)

    @pl.when(c == 0)
    def _():
        c_s[...] = c0_ref[0]
        n_s[...] = n0_ref[0]
        m_s[...] = m0_ref[0]
        prev_s[...] = cbuf_ref[0]

    a = qk_ref[...]
    ext = jnp.concatenate([prev_s[...], a], axis=0)
    conv = cb_ref[...]
    for j in range(CONV_W):
        lo = SUBLANES - (CONV_W - 1) + j
        conv = conv + cw_ref[j:j + 1, :] * ext[lo:lo + lc]
    prev_s[...] = a[lc - SUBLANES:lc]
    qk = conv * jax.nn.sigmoid(conv)

    z = gate_ref[...] + bg_ref[...]
    lane = lax.broadcasted_iota(i32, (lc, LANES), 1)
    row = lax.broadcasted_iota(i32, (lc, LANES), 0)
    logf = jnp.minimum(z, 0.0) - jnp.log1p(jnp.exp(-jnp.abs(z)))
    cum = jnp.where((lane >= M_HEADS) & (lane < 2 * M_HEADS), logf, 0.0)
    shift = 1
    while shift < lc:
        cum = cum + jnp.where(row >= shift, pltpu.roll(cum, shift, axis=0), 0.0)
        shift *= 2
    zc = jnp.where(lane < M_HEADS, z, cum)
    zt = jnp.concatenate([zc, jnp.zeros((LANES - lc, LANES), f32)], axis=0).T[:, :lc]

    ri = lax.broadcasted_iota(i32, (lc, lc), 0)
    ci = lax.broadcasted_iota(i32, (lc, lc), 1)
    causal = ri >= ci
    vall = v_ref[...]
    hs = []
    for h in range(M_HEADS):
        sl = slice(h * M_DH, (h + 1) * M_DH)
        q = qk[:, sl]
        k = qk[:, M_WIDTH + h * M_DH:M_WIDTH + (h + 1) * M_DH] * (M_DH ** -0.5)
        v = vall[:, sl]
        i_col = zc[:, h:h + 1]
        b_col = zc[:, M_HEADS + h:M_HEADS + h + 1]
        i_row = zt[h:h + 1, :]
        b_row = zt[M_HEADS + h:M_HEADS + h + 1, :]
        m_prev = m_s[h:h + 1, 0:1]
        c_prev = c_s[h]
        n_prev = n_s[h:h + 1, :]

        dmat = jnp.where(causal, b_col - b_row + i_row, NEG)
        inter = b_col + m_prev
        mt = jnp.maximum(jnp.max(dmat, axis=-1, keepdims=True), inter)
        w_intra = jnp.exp(dmat - mt)
        w_inter = jnp.exp(inter - mt)
        qb, kb, vb = q.astype(bf16), k.astype(bf16), v.astype(bf16)
        s = lax.dot_general(qb, kb, (((1,), (1,)), ((), ())), preferred_element_type=f32) * w_intra
        num = (w_inter * jnp.dot(qb, c_prev.astype(bf16), preferred_element_type=f32)
               + jnp.dot(s.astype(bf16), vb, preferred_element_type=f32))
        den = w_inter * jnp.sum(q * n_prev, axis=-1, keepdims=True) + jnp.sum(s, axis=-1, keepdims=True)
        hh = num / jnp.maximum(jnp.abs(den), jnp.exp(-mt))
        m_new = mt[lc - 1:lc, :]
        b_last = b_col[lc - 1:lc, :]
        w_s = jnp.exp(b_last - b_col + i_col - m_new)
        decay = jnp.exp(b_last + m_prev - m_new)
        kw = k * w_s
        c_s[h] = decay * c_prev + lax.dot_general(kw.astype(bf16), vb, (((0,), (0,)), ((), ())),
                                                   preferred_element_type=f32)
        n_s[h:h + 1, :] = decay * n_prev + jnp.sum(kw, axis=0, keepdims=True)
        m_s[h:h + 1, :] = jnp.broadcast_to(m_new, (1, LANES))
        hs.append(hh * lax.rsqrt(jnp.mean(hh * hh, axis=-1, keepdims=True) + EPS))

    h_all = jnp.concatenate(hs, axis=-1)
    h_ref[...] = h_all * mhg_ref[...] * jax.nn.sigmoid(o_ref[...])

    @pl.when(c == pl.num_programs(1) - 1)
    def _():
        cout_ref[0] = c_s[...]
        nout_ref[0] = n_s[...]
        mout_ref[0] = m_s[...]


def _mlstm(mqk, mv, mo, gates, row0, bsz, t, c0, n0, m0, cbuf, conv_w, conv_b, b_gates, mh_g):
    lc = min(CHUNK, t)
    nc = t // lc
    assert t % lc == 0 and lc % SUBLANES == 0 and row0 % lc == 0
    n0p = jnp.pad(n0.astype(f32), ((0, 0), (0, SUBLANES - M_HEADS), (0, 0)))
    m0p = jnp.pad(jnp.broadcast_to(m0.astype(f32)[:, :, None], (bsz, M_HEADS, LANES)),
                  ((0, 0), (0, SUBLANES - M_HEADS), (0, 0)))
    cbp = jnp.pad(cbuf.astype(f32), ((0, 0), (SUBLANES - (CONV_W - 1), 0), (0, 0)))
    bg = jnp.pad(b_gates.astype(f32), (0, LANES - 2 * M_HEADS)).reshape(1, LANES)
    seq = lambda w: pl.BlockSpec((lc, w), lambda b, c: (row0 // lc + b * nc + c, 0))
    out_seq = pl.BlockSpec((lc, M_WIDTH), lambda b, c: (b * nc + c, 0))
    per_b = lambda shp: pl.BlockSpec((1,) + shp, lambda b, c: (b,) + (0,) * len(shp))
    full = lambda a: pl.BlockSpec(a.shape, lambda b, c: (0,) * a.ndim)
    cb = conv_b.reshape(1, -1)
    mhg = mh_g.reshape(1, -1)
    return pl.pallas_call(
        functools.partial(_mlstm_kernel, lc=lc),
        grid=(bsz, nc),
        in_specs=[seq(2 * M_WIDTH), seq(M_WIDTH), seq(M_WIDTH), seq(LANES),
                  per_b((M_HEADS, M_DH, M_DH)), per_b((SUBLANES, M_DH)), per_b((SUBLANES, LANES)),
                  per_b((SUBLANES, 2 * M_WIDTH)),
                  full(conv_w), full(cb), full(bg), full(mhg)],
        out_specs=[out_seq, per_b((M_HEADS, M_DH, M_DH)), per_b((SUBLANES, M_DH)), per_b((SUBLANES, LANES))],
        out_shape=[jax.ShapeDtypeStruct((bsz * t, M_WIDTH), f32),
                   jax.ShapeDtypeStruct((bsz, M_HEADS, M_DH, M_DH), f32),
                   jax.ShapeDtypeStruct((bsz, SUBLANES, M_DH), f32),
                   jax.ShapeDtypeStruct((bsz, SUBLANES, LANES), f32)],
        scratch_shapes=[pltpu.VMEM((M_HEADS, M_DH, M_DH), f32), pltpu.VMEM((SUBLANES, M_DH), f32),
                        pltpu.VMEM((SUBLANES, LANES), f32), pltpu.VMEM((SUBLANES, 2 * M_WIDTH), f32)],
        compiler_params=pltpu.CompilerParams(dimension_semantics=("parallel", "arbitrary"),
                                             vmem_limit_bytes=VMEM_LIMIT),
        name="mlstm",
    )(mqk, mv, mo, gates, c0.astype(f32), n0p, m0p, cbp, conv_w, cb, bg, mhg)


def _topk_rows(s, k, payload=None):
    n = s.shape[0]
    rows = lax.broadcasted_iota(i32, s.shape, 0).astype(f32)
    vals, ids = [], []
    for _ in range(k):
        m = jnp.max(s, axis=0, keepdims=True)
        pos = jnp.min(jnp.where(s == m, rows, float(n)), axis=0, keepdims=True)
        sel = rows == pos
        vals.append(m)
        ids.append(pos if payload is None else jnp.max(jnp.where(sel, payload, -1.0), axis=0, keepdims=True))
        s = jnp.where(sel, -jnp.inf, s)
    return jnp.concatenate(vals, axis=0), jnp.concatenate(ids, axis=0)


def _mid_kernel(xp_ref, xs_ref, attp_ref, atts_ref, hp_ref, hs_ref, wo_ref, g2_ref, wq_ref, keys_ref,
                x1_ref, xn_ref, eidx_ref, gate_ref, *, nbp):
    cat = jnp.concatenate([_pick(nbp, attp_ref, atts_ref), _pick(nbp, hp_ref, hs_ref)], axis=-1).astype(bf16)
    x1 = _pick(nbp, xp_ref, xs_ref) + jnp.dot(cat, wo_ref[...], preferred_element_type=f32)
    x1_ref[...] = x1
    xn = _rms(x1, g2_ref[...])
    xn_ref[...] = xn
    xb = xn.astype(bf16)
    e_rows, g_rows = [], []
    for h in range(P_HEADS):
        q = jnp.dot(xb, wq_ref[:, h * P_DKEY:(h + 1) * P_DKEY], preferred_element_type=f32).astype(bf16)
        half = []
        for p in range(2):
            st = lax.dot_general(keys_ref[h, p], q[:, p * N_KEYS:(p + 1) * N_KEYS], (((1,), (1,)), ((), ())),
                                 preferred_element_type=f32)
            half.append(_topk_rows(st, P_TOPK))
        (v0, i0), (v1, i1) = half
        width = [P_TOPK // (a + 1) for a in range(P_TOPK)]
        fill = -sum(width) % SUBLANES
        cand = jnp.concatenate([v0[a:a + 1] + v1[:width[a]] for a in range(P_TOPK)]
                               + [jnp.full((fill, v0.shape[1]), -jnp.inf, f32)], axis=0)
        cidx = jnp.concatenate([i0[a:a + 1] * float(N_KEYS) + i1[:width[a]] for a in range(P_TOPK)]
                               + [jnp.zeros((fill, v0.shape[1]), f32)], axis=0)
        top_s, eid = _topk_rows(cand, P_TOPK, payload=cidx)
        ex = jnp.exp(top_s - top_s[0:1])
        e_rows.append(eid)
        g_rows.append(ex / jnp.sum(ex, axis=0, keepdims=True))
    eidx_ref[...] = jnp.concatenate(e_rows, axis=0).T.astype(i32)
    gate_ref[...] = jnp.concatenate(g_rows, axis=0).T


def _mid(xp, xp_row0, xs, attp, atts, hp, hs, n_a, n_b, w_out, g2, wq, keys):
    n = n_a + n_b
    nbp = n_a // ROW_BLOCK
    assert n_a % ROW_BLOCK == 0 and n_b % ROW_BLOCK == 0 and xp_row0 % ROW_BLOCK == 0
    row = lambda w: pl.BlockSpec((ROW_BLOCK, w), lambda i: (i, 0))
    full = lambda a: pl.BlockSpec(a.shape, lambda i: (0,) * a.ndim)
    wo = w_out.astype(bf16)
    wqb = wq.astype(bf16)
    kb = keys.astype(bf16)
    g = g2.reshape(1, D_MODEL)
    return pl.pallas_call(
        functools.partial(_mid_kernel, nbp=nbp),
        grid=(n // ROW_BLOCK,),
        in_specs=[*_pair_specs(nbp, D_MODEL, xp_row0 // ROW_BLOCK), *_pair_specs(nbp, A_WIDTH),
                  *_pair_specs(nbp, M_WIDTH), full(wo), full(g), full(wqb), full(kb)],
        out_specs=[row(D_MODEL), row(D_MODEL), row(P_SLOTS), row(P_SLOTS)],
        out_shape=[jax.ShapeDtypeStruct((n, D_MODEL), f32), jax.ShapeDtypeStruct((n, D_MODEL), f32),
                   jax.ShapeDtypeStruct((n, P_SLOTS), i32), jax.ShapeDtypeStruct((n, P_SLOTS), f32)],
        compiler_params=pltpu.CompilerParams(dimension_semantics=("parallel",), vmem_limit_bytes=VMEM_LIMIT),
        name="outproj_retrieve",
    )(xp, xs, attp, atts, hp, hs, wo, g, wqb, kb)


def _gather_rows(eidx_ref, tab_ref, g_ref, t):
    for r in range(P_SLOTS):
        g_ref[pl.ds(r * ROW_TILES, ROW_TILES), :] = tab_ref[eidx_ref[t, r]]


def _tile_rows(t, rows=SUBLANES):
    return pl.ds(pl.multiple_of(t * rows, rows), rows)


def _pipelined_tokens(nt, eidx_ref, tab_ref, g0_s, g1_s, compute):
    bufs = (g0_s, g1_s)
    _gather_rows(eidx_ref, tab_ref, g0_s, 0)

    def body(j, carry):
        for u in range(TOKEN_UNROLL):
            t = TOKEN_UNROLL * j + u
            _gather_rows(eidx_ref, tab_ref, bufs[(u + 1) % 2], jnp.minimum(t + 1, nt - 1))
            compute(t, bufs[u % 2])
        return carry

    lax.fori_loop(0, nt // TOKEN_UNROLL, body, 0)


def _gelu_tanh(x):
    return 0.5 * x * (1.0 + jnp.tanh(math.sqrt(2.0 / math.pi) * (x + 0.044715 * (x * x * x))))


def _peer_u_kernel(eidx_ref, xn_ref, g_ref, tab_ref, w_ref, xl_s, r_s, g0_s, g1_s):
    nt = xn_ref.shape[0]
    xn = xn_ref[...]
    xh = xn.astype(bf16).astype(f32)
    xl = xn - xh
    for k in range(ROW_TILES):
        xl_s[pl.ds(k, nt, stride=2 * SUBLANES), :] = xh[:, k * LANES:(k + 1) * LANES]
        xl_s[pl.ds(SUBLANES + k, nt, stride=2 * SUBLANES), :] = xl[:, k * LANES:(k + 1) * LANES]
    diag = (lax.broadcasted_iota(i32, (SUBLANES, G_ROWS), 1) % ROW_TILES
            == lax.broadcasted_iota(i32, (SUBLANES, G_ROWS), 0))

    def compute(t, g_s):
        lhs = xl_s[_tile_rows(t, 2 * SUBLANES), :].astype(bf16)
        out = lax.dot_general(lhs, g_s[...], (((1,), (1,)), ((), ())), preferred_element_type=f32)
        part = jnp.where(diag, out[:SUBLANES] + out[SUBLANES:], 0.0)
        for c in range(ROW_TILES):
            r_s[c, _tile_rows(t), :] = part[:, c * LANES:(c + 1) * LANES]

    _pipelined_tokens(nt, eidx_ref, tab_ref, g0_s, g1_s, compute)
    cols = []
    for c in range(ROW_TILES):
        acc = r_s[c, pl.ds(0, nt, stride=SUBLANES), :]
        for k in range(1, SUBLANES):
            acc = acc + r_s[c, pl.ds(k, nt, stride=SUBLANES), :]
        cols.append(acc)
    s = jnp.concatenate(cols, axis=-1)
    fold = (lax.broadcasted_iota(i32, (G_ROWS, P_SLOTS), 0) // ROW_TILES
            == lax.broadcasted_iota(i32, (G_ROWS, P_SLOTS), 1)).astype(bf16)
    sh, sl = _split_bf16(s)
    act = jnp.dot(sh, fold, preferred_element_type=f32) + jnp.dot(sl, fold, preferred_element_type=f32)
    w_ref[...] = g_ref[...] * _gelu_tanh(act)


def _peer_v_kernel(eidx_ref, w_ref, tab_ref, peer_ref, wl_s, o_s, g0_s, g1_s):
    nt = w_ref.shape[0]
    spread = (lax.broadcasted_iota(i32, (P_SLOTS, G_ROWS), 1) // ROW_TILES
              == lax.broadcasted_iota(i32, (P_SLOTS, G_ROWS), 0)).astype(bf16)
    wexp = jnp.dot(w_ref[...].astype(bf16), spread, preferred_element_type=f32)
    lane = lax.broadcasted_iota(i32, (nt, LANES), 1)
    for c in range(ROW_TILES):
        wc = wexp[:, c * LANES:(c + 1) * LANES]
        for k in range(SUBLANES):
            wl_s[c, pl.ds(k, nt, stride=SUBLANES), :] = jnp.where(lane % ROW_TILES == k, wc, 0.0)

    def compute(t, g_s):
        lhs = jnp.concatenate([wl_s[c, _tile_rows(t), :] for c in range(ROW_TILES)], axis=-1).astype(bf16)
        o_s[_tile_rows(t), :] = jnp.dot(lhs, g_s[...], preferred_element_type=f32)

    _pipelined_tokens(nt, eidx_ref, tab_ref, g0_s, g1_s, compute)
    for k in range(ROW_TILES):
        peer_ref[:, k * LANES:(k + 1) * LANES] = o_s[pl.ds(k, nt, stride=SUBLANES), :]


def _final_kernel(pa_ref, pb_ref, xa_ref, xb_ref, gf_ref, yp_ref, ys_ref, *, nba, nbp):
    y = _rms(_pick(nba, xa_ref, xb_ref) + _pick(nba, pa_ref, pb_ref), gf_ref[...])

    @pl.when(pl.program_id(0) < nbp)
    def _():
        yp_ref[...] = y

    @pl.when(pl.program_id(0) >= nbp)
    def _():
        ys_ref[...] = y


def _final(peer_a, peer_b, x1_a, x1_b, gf, n_p):
    n = x1_a.shape[0] + x1_b.shape[0]
    nba, nbp = x1_a.shape[0] // ROW_BLOCK, n_p // ROW_BLOCK
    g = gf.reshape(1, D_MODEL)
    return pl.pallas_call(
        functools.partial(_final_kernel, nba=nba, nbp=nbp),
        grid=(n // ROW_BLOCK,),
        in_specs=[*_pair_specs(nba, D_MODEL), *_pair_specs(nba, D_MODEL), pl.BlockSpec(g.shape, lambda i: (0, 0))],
        out_specs=list(_pair_specs(nbp, D_MODEL)),
        out_shape=[jax.ShapeDtypeStruct((n_p, D_MODEL), f32), jax.ShapeDtypeStruct((n - n_p, D_MODEL), f32)],
        compiler_params=pltpu.CompilerParams(dimension_semantics=("arbitrary",), vmem_limit_bytes=VMEM_LIMIT),
        name="final_norm",
    )(peer_a, peer_b, x1_a, x1_b, g)


def _peer_specs():
    row = lambda w: pl.BlockSpec((ROW_BLOCK, w), lambda i: (i, 0))
    idx = pl.BlockSpec((ROW_BLOCK, P_SLOTS), lambda i: (i, 0), memory_space=pltpu.SMEM)
    tab = pl.BlockSpec(memory_space=pltpu.VMEM)
    gscr = pltpu.VMEM((G_ROWS, LANES), bf16)
    params = pltpu.CompilerParams(dimension_semantics=("arbitrary",), vmem_limit_bytes=VMEM_LIMIT)
    return row, idx, tab, gscr, params


def _expert_table(tab):
    return tab.astype(bf16).reshape(tab.shape[0], ROW_TILES, LANES)


def _peer_u(eidx, xn, g, utab, n):
    row, idx, tab, gscr, params = _peer_specs()
    return pl.pallas_call(
        _peer_u_kernel,
        grid=(n // ROW_BLOCK,),
        in_specs=[idx, row(D_MODEL), row(P_SLOTS), tab],
        out_specs=row(P_SLOTS),
        out_shape=jax.ShapeDtypeStruct((n, P_SLOTS), f32),
        scratch_shapes=[pltpu.VMEM((ROW_BLOCK * 2 * SUBLANES, LANES), f32),
                        pltpu.VMEM((ROW_TILES, ROW_BLOCK * SUBLANES, LANES), f32), gscr, gscr],
        compiler_params=params,
        name="peer_u",
    )(eidx, xn, g, utab)


def _peer_v(eidx, w, vtab):
    n = w.shape[0]
    row, idx, tab, gscr, params = _peer_specs()
    return pl.pallas_call(
        _peer_v_kernel,
        grid=(n // ROW_BLOCK,),
        in_specs=[idx, row(P_SLOTS), tab],
        out_specs=row(D_MODEL),
        out_shape=jax.ShapeDtypeStruct((n, D_MODEL), f32),
        scratch_shapes=[pltpu.VMEM((ROW_TILES, ROW_BLOCK * SUBLANES, LANES), f32),
                        pltpu.VMEM((ROW_BLOCK * SUBLANES, LANES), f32), gscr, gscr],
        compiler_params=params,
        name="peer_v",
    )(eidx, w, vtab)


def _sc_table(tab):
    e = tab.shape[0]
    t = tab.astype(bf16).reshape(e, SC_CHUNKS, 2, SC_LANES).transpose(0, 1, 3, 2)
    return lax.bitcast_convert_type(t, i32).reshape(e, SC_WORDS)


def _sc_unpack(words):
    return plsc.unpack(plsc.bitcast(words, bf16), format=plsc.PackFormat.INTERLEAVED)


def _sc_gelu_tanh(x):
    z = math.sqrt(2.0 / math.pi) * (x + 0.044715 * (x * x * x))
    return 0.5 * x * (2.0 - 2.0 / (jnp.exp(2.0 * z) + 1.0))


def _sc_peer(utab32, vtab32, eidx, xn, gate):
    n_sc = eidx.shape[0]
    per = n_sc // SC_WORKERS
    units = 4 * per
    assert n_sc % (SC_WORKERS * SUBLANES) == 0

    def body(u_hbm, v_hbm, eidx_hbm, x_hbm, g_hbm, out_hbm, idx_v, x_v, g_v, rows_v, acc_v, w_v, out_v, sems):
        base = (lax.axis_index("s") * 2 + lax.axis_index("c")) * per
        lanes = lax.broadcasted_iota(i32, (SC_LANES,), 0)

        def load_token(tok, tslot):
            pltpu.sync_copy(eidx_hbm.at[base + tok], idx_v.at[tslot])
            pltpu.sync_copy(x_hbm.at[base + tok], x_v.at[tslot])
            pltpu.sync_copy(g_hbm.at[base + tok], g_v.at[tslot])

        def gather(tab_hbm, g):
            tok, k = g // 4, g % 4
            return pltpu.make_async_copy(tab_hbm.at[idx_v.at[tok % 2, pl.ds((k % 2) * SC_UNIT_ROWS, SC_UNIT_ROWS)]],
                                         rows_v.at[k % 2], sems.at[k % 2])

        def start(g):
            @pl.when(g % 4 < 2)
            def _():
                gather(u_hbm, g).start()

            @pl.when(g % 4 >= 2)
            def _():
                gather(v_hbm, g).start()

        def compute_u(tslot, half, rows):
            @pl.loop(0, SC_UNIT_ROWS // SC_ROW_GROUP)
            def _(rg):
                accs = [jnp.zeros((SC_LANES,), f32) for _ in range(SC_ROW_GROUP)]
                for c in range(SC_CHUNKS):
                    xa = x_v[tslot, pl.ds(2 * c * SC_LANES, SC_LANES)]
                    xb = x_v[tslot, pl.ds((2 * c + 1) * SC_LANES, SC_LANES)]
                    for r in range(SC_ROW_GROUP):
                        a, b = _sc_unpack(rows[rg * SC_ROW_GROUP + r, pl.ds(c * SC_LANES, SC_LANES)])
                        accs[r] = accs[r] + a * xa + b * xb
                for r in range(SC_ROW_GROUP):
                    acc_v[pl.ds((half * SC_UNIT_ROWS + rg * SC_ROW_GROUP + r) * SC_LANES, SC_LANES)] = accs[r]

        def gate_weights(tslot):
            @pl.loop(0, P_SLOTS // SC_LANES)
            def _(sg):
                first = sg * SC_LANES * SC_LANES
                act = jnp.zeros((SC_LANES,), f32)
                for lane in range(SC_LANES):
                    act = act + plsc.load_gather(acc_v, [first + lanes * SC_LANES + lane])
                w_v[pl.ds(sg * SC_LANES, SC_LANES)] = g_v[tslot, pl.ds(sg * SC_LANES, SC_LANES)] * _sc_gelu_tanh(act)

        def compute_v(half, rows):
            @pl.loop(0, SC_CHUNKS // SC_CHUNK_GROUP)
            def _(cg):
                keep = (half == 1).astype(f32)
                first = cg * SC_CHUNK_GROUP
                accs = [out_v[pl.ds((2 * first + j) * SC_LANES, SC_LANES)] * keep for j in range(2 * SC_CHUNK_GROUP)]
                for r in range(SC_UNIT_ROWS):
                    wv = plsc.load_gather(w_v, [jnp.zeros((SC_LANES,), i32) + (half * SC_UNIT_ROWS + r)])
                    for c in range(SC_CHUNK_GROUP):
                        a, b = _sc_unpack(rows[r, pl.ds((first + c) * SC_LANES, SC_LANES)])
                        accs[2 * c] = accs[2 * c] + a * wv
                        accs[2 * c + 1] = accs[2 * c + 1] + b * wv
                for j in range(2 * SC_CHUNK_GROUP):
                    out_v[pl.ds((2 * first + j) * SC_LANES, SC_LANES)] = accs[j]

        load_token(0, 0)
        start(0)

        @pl.loop(0, units)
        def _(g):
            tok, k = g // 4, g % 4

            @pl.when(jnp.logical_and(k == 3, g + 1 < units))
            def _():
                load_token(tok + 1, (tok + 1) % 2)

            @pl.when(g + 1 < units)
            def _():
                start(g + 1)

            gather(u_hbm, g).wait()
            rows = rows_v.at[k % 2]

            @pl.when(k < 2)
            def _():
                compute_u(tok % 2, k, rows)

            @pl.when(k == 1)
            def _():
                gate_weights(tok % 2)

            @pl.when(k >= 2)
            def _():
                compute_v(k - 2, rows)

            @pl.when(k == 3)
            def _():
                pltpu.sync_copy(out_v, out_hbm.at[base + tok])

    return pl.kernel(
        body, mesh=plsc.VectorSubcoreMesh(core_axis_name="c", subcore_axis_name="s"),
        out_type=jax.ShapeDtypeStruct((n_sc, D_MODEL), f32),
        scratch_types=[pltpu.VMEM((2, P_SLOTS), i32), pltpu.VMEM((2, D_MODEL), f32), pltpu.VMEM((2, P_SLOTS), f32),
                       pltpu.VMEM((2, SC_UNIT_ROWS, SC_WORDS), i32), pltpu.VMEM((P_SLOTS * SC_LANES,), f32),
                       pltpu.VMEM((P_SLOTS,), f32), pltpu.VMEM((D_MODEL,), f32), pltpu.SemaphoreType.DMA((2,))],
        compiler_params=pltpu.CompilerParams(needs_layout_passes=False),
        name="sc_peer",
    )(utab32, vtab32, eidx, xn, gate)


def kernel(x_prompt, x_sample, cache_k, cache_v, state_C, state_n, state_m, state_conv, norm1_g, w_in, b_gates, rel_bias, conv_w, conv_b, mh_norm_g, w_out, norm2_g, peer_wq, peer_keys, peer_u, peer_v, final_g):
    bp, sp, d = x_prompt.shape
    bs, ts, _ = x_sample.shape
    n_p, n_s = bp * sp, bs * ts
    n = n_p + n_s
    assert n_p % ROW_BLOCK == 0 and n_s % ROW_BLOCK == 0 and d == D_MODEL
    depth = w_in.shape[0]
    assert depth == 1, "the final norm is fused into the last layer's PEER pass"
    l = 0
    xp, xs = x_prompt.reshape(n_p, d), x_sample.reshape(n_s, d)

    aq, ak, av, mqk, mv, mo, gates = _inproj(xp, xs, norm1_g[l], w_in[l])
    zeros = lambda *shp: jnp.zeros(shp, f32)
    mparams = (conv_w[l], conv_b[l], b_gates[l], mh_norm_g[l])
    retrieval = (w_out[l], norm2_g[l], peer_wq[l], peer_keys[l])

    def prompt_mixers(b0, nb):
        att = _attn_prompt(aq, ak, av, rel_bias[l], b0, nb, sp)
        return (att,) + tuple(_mlstm(mqk, mv, mo, gates, b0 * sp, nb, sp, zeros(nb, M_HEADS, M_DH, M_DH),
                                     zeros(nb, M_HEADS, M_DH), zeros(nb, M_HEADS), zeros(nb, CONV_W - 1, 2 * M_WIDTH),
                                     *mparams))

    assert 0 < SC_PROMPT_SEQS < bp
    n_sc = SC_PROMPT_SEQS * sp
    att_a, h_a, c_a, nn_a, mm_a = prompt_mixers(0, SC_PROMPT_SEQS)
    x1_a, xn_a, eidx_a, gate_a = _mid(xp, 0, xs, att_a, att_a, h_a, h_a, n_sc, 0, *retrieval)
    peer_a = _sc_peer(_sc_table(peer_u[l]), _sc_table(peer_v[l]), eidx_a, xn_a, gate_a)

    att_b, h_b, c_b, nn_b, mm_b = prompt_mixers(SC_PROMPT_SEQS, bp - SC_PROMPT_SEQS)
    lcache = cache_k.shape[2]
    att_s = _attn_sample(aq, ak, av, cache_k[l].reshape(bs, lcache, A_WIDTH),
                         cache_v[l].reshape(bs, lcache, A_WIDTH), rel_bias[l], n_p, bs, ts)
    h_s, c_s, nn_s, mm_s = _mlstm(mqk, mv, mo, gates, n_p, bs, ts, state_C[l], state_n[l], state_m[l],
                                  state_conv[l], *mparams)
    x1_b, xn_b, eidx_b, gate_b = _mid(xp, n_sc, xs, att_b, att_s, h_b, h_s, n_p - n_sc, n_s, *retrieval)
    w_b = _peer_u(eidx_b, xn_b, gate_b, _expert_table(peer_u[l]), n - n_sc)
    peer_b = _peer_v(eidx_b, w_b, _expert_table(peer_v[l]))
    y_p, y_s = _final(peer_a, peer_b, x1_a, x1_b, final_g, n_p)
    c_p, nn_p, mm_p = (jnp.concatenate(ab, axis=0) for ab in ((c_a, c_b), (nn_a, nn_b), (mm_a, mm_b)))

    def tail(a, row0, bsz, t, keep):
        return jnp.stack([a[row0 + (b + 1) * t - keep:row0 + (b + 1) * t] for b in range(bsz)])

    keep = min(WINDOW, sp)
    heads = lambda a: a.reshape(a.shape[0], a.shape[1], A_HEADS, A_DH)
    ctail = CONV_W - 1
    conv_tail = lambda buf, a, row0, bsz, t: jnp.concatenate([buf.astype(a.dtype), tail(a, row0, bsz, t, min(ctail, t))],
                                                             axis=1)[:, -ctail:]
    st = lambda a: a[None]
    return (y_p.reshape(bp, sp, d), y_s.reshape(bs, ts, d),
            st(heads(tail(ak, 0, bp, sp, keep))), st(heads(tail(av, 0, bp, sp, keep))),
            st(c_p), st(nn_p[:, :M_HEADS]), st(mm_p[:, :M_HEADS, 0]),
            st(conv_tail(zeros(bp, ctail, 2 * M_WIDTH), mqk, 0, bp, sp)),
            st(heads(ak[n_p:].reshape(bs, ts, A_WIDTH))), st(heads(av[n_p:].reshape(bs, ts, A_WIDTH))),
            st(c_s), st(nn_s[:, :M_HEADS]), st(mm_s[:, :M_HEADS, 0]),
            st(conv_tail(state_conv[l], mqk, n_p, bs, ts)))
```

```python
import functools
import math

import jax
import jax.numpy as jnp
from jax import lax
from jax.experimental import pallas as pl
from jax.experimental.pallas import tpu as pltpu
from jax.experimental.pallas import tpu_sc as plsc

f32 = jnp.float32
bf16 = jnp.bfloat16
i32 = jnp.int32

D_MODEL = 1024
CHUNK = 64
A_HEADS = 8
A_DH = 64
A_WIDTH = A_HEADS * A_DH
BAND_CHUNKS = 8
WINDOW = BAND_CHUNKS * CHUNK
MAX_REL = 128
ATT_SCALE = A_DH ** -0.5
M_HEADS = 4
M_DH = 128
M_WIDTH = M_HEADS * M_DH
CONV_W = 4
P_HEADS = 8
P_DKEY = 256
N_KEYS = 128
P_TOPK = 16
P_SLOTS = P_HEADS * P_TOPK
EPS = 1e-6
NEG = -1e30

LANES = 128
SUBLANES = 8
ROW_BLOCK = 256
ATT_TILE = 512
ATT_SUB = 128
ATT_KEYS = ATT_SUB + WINDOW
ROW_TILES = D_MODEL // LANES
G_ROWS = P_SLOTS * ROW_TILES
TOKEN_UNROLL = 8
VMEM_LIMIT = 56 * 1024 * 1024

SC_WORKERS = 32
SC_LANES = 16
SC_UNIT_ROWS = P_SLOTS // 2
SC_ROW_GROUP = 16
SC_WORDS = D_MODEL // 2
SC_CHUNKS = SC_WORDS // SC_LANES
SC_CHUNK_GROUP = 8
SC_PROMPT_SEQS = 1


def _rms(x, g):
    return x * lax.rsqrt(jnp.mean(x * x, axis=-1, keepdims=True) + EPS) * g


def _split_bf16(x):
    hi = x.astype(bf16)
    lo = (x - hi.astype(f32)).astype(bf16)
    return hi, lo


def _pair_specs(nbp, width, first=0):
    return (pl.BlockSpec((ROW_BLOCK, width), lambda i: (first + jnp.minimum(i, nbp - 1), 0)),
            pl.BlockSpec((ROW_BLOCK, width), lambda i: (jnp.maximum(i - nbp, 0), 0)))


def _pick(nbp, p_ref, s_ref):
    return jnp.where(pl.program_id(0) < nbp, p_ref[...], s_ref[...])


def _inproj_kernel(xp_ref, xs_ref, g_ref, w_ref, wgh_ref, wgl_ref,
                   aq_ref, ak_ref, av_ref, mqk_ref, mv_ref, mo_ref, gate_ref, *, nbp):
    xn = _rms(_pick(nbp, xp_ref, xs_ref), g_ref[...])
    xh, xl = _split_bf16(xn)

    def proj(lo, hi):
        return jnp.dot(xh, w_ref[:, lo:hi], preferred_element_type=f32)

    aq_ref[...] = proj(0, 512)
    ak_ref[...] = proj(512, 1024)
    av_ref[...] = proj(1024, 1536)
    mqk_ref[...] = proj(1536, 2560)
    mv_ref[...] = proj(2560, 3072)
    mo_ref[...] = proj(3072, 3584)
    gate_ref[...] = (jnp.dot(xh, wgh_ref[...], preferred_element_type=f32)
                     + jnp.dot(xl, wgh_ref[...], preferred_element_type=f32)
                     + jnp.dot(xh, wgl_ref[...], preferred_element_type=f32))


def _inproj(xp, xs, g1, w_in):
    n = xp.shape[0] + xs.shape[0]
    nbp = xp.shape[0] // ROW_BLOCK
    main = 3 * A_WIDTH + 4 * M_WIDTH
    w_main = w_in[:, :main].astype(bf16)
    wg = jnp.pad(w_in[:, main:], ((0, 0), (0, LANES - 2 * M_HEADS)))
    wgh, wgl = _split_bf16(wg)
    widths = (512, 512, 512, 1024, 512, 512, LANES)
    row = lambda w: pl.BlockSpec((ROW_BLOCK, w), lambda i: (i, 0))
    full = lambda a: pl.BlockSpec(a.shape, lambda i: (0,) * a.ndim)
    g = g1.reshape(1, D_MODEL)
    return pl.pallas_call(
        functools.partial(_inproj_kernel, nbp=nbp),
        grid=(n // ROW_BLOCK,),
        in_specs=[*_pair_specs(nbp, D_MODEL), full(g), full(w_main), full(wgh), full(wgl)],
        out_specs=[row(w) for w in widths],
        out_shape=[jax.ShapeDtypeStruct((n, w), f32) for w in widths],
        compiler_params=pltpu.CompilerParams(dimension_semantics=("parallel",), vmem_limit_bytes=VMEM_LIMIT),
        name="inproj",
    )(xp, xs, g, w_main, wgh, wgl)


def _attn_heads(q, k, v, bias_ref, key_ok):
    outs = []
    for h in range(A_HEADS):
        sl = slice(h * A_DH, (h + 1) * A_DH)
        s = lax.dot_general(q[:, sl], k[:, sl], (((1,), (1,)), ((), ())), preferred_element_type=f32)
        s = s * ATT_SCALE + bias_ref[h]
        if key_ok is not None:
            s = jnp.where(key_ok, s, NEG)
        m = jnp.max(s, axis=-1, keepdims=True)
        p = jnp.exp(s - m)
        l = jnp.sum(p, axis=-1, keepdims=True)
        o = jnp.dot(p.astype(bf16), v[:, sl], preferred_element_type=f32)
        outs.append(o / l)
    return jnp.concatenate(outs, axis=-1)


def _attn_prompt_kernel(q_ref, k0_ref, k1_ref, v0_ref, v1_ref, bias_ref, o_ref):
    t = pl.program_id(1)
    q = q_ref[...].astype(bf16)
    k = jnp.concatenate([k0_ref[...], k1_ref[...]], axis=0).astype(bf16)
    v = jnp.concatenate([v0_ref[...], v1_ref[...]], axis=0).astype(bf16)
    col = lax.broadcasted_iota(i32, (1, ATT_KEYS), 1)
    for s in range(ATT_TILE // ATT_SUB):
        lo = s * ATT_SUB
        key_ok = (t * ATT_TILE + lo + col) >= WINDOW
        o_ref[lo:lo + ATT_SUB, :] = _attn_heads(q[lo:lo + ATT_SUB], k[lo:lo + ATT_KEYS], v[lo:lo + ATT_KEYS],
                                                 bias_ref, key_ok)


def _attn_sample_kernel(q_ref, k_ref, v_ref, bias_ref, o_ref):
    o_ref[...] = _attn_heads(q_ref[...].astype(bf16), k_ref[0].astype(bf16), v_ref[0].astype(bf16), bias_ref, None)


def _rel_bias_table(rel_bias, rows, cols, offset, valid):
    span = rows + cols - 1
    rel = offset + rows - 1 - jnp.arange(span)
    diag = rel_bias[:, jnp.clip(rel, -MAX_REL, MAX_REL) + MAX_REL].astype(f32)
    diag = jnp.pad(diag, ((0, 0), (0, 1)))
    flat = jnp.tile(diag, (1, rows))[:, rows - 1:rows - 1 + rows * span]
    return jnp.where(valid[None], flat.reshape(-1, rows, span)[:, :, :cols], NEG)


def _attn_prompt(q, k, v, rel_bias, b0, bsz, s):
    assert s % ATT_TILE == 0 and WINDOW == ATT_TILE
    nt = s // ATT_TILE
    i = jnp.arange(ATT_SUB)[:, None]
    j = jnp.arange(ATT_KEYS)[None, :]
    off = j - (i // CHUNK) * CHUNK
    bias = _rel_bias_table(rel_bias, ATT_SUB, ATT_KEYS, WINDOW, (off >= 0) & (off < WINDOW + CHUNK))
    cur = pl.BlockSpec((ATT_TILE, A_WIDTH), lambda b, t: ((b0 + b) * nt + t, 0))
    prev = pl.BlockSpec((ATT_TILE, A_WIDTH), lambda b, t: ((b0 + b) * nt + jnp.maximum(t - 1, 0), 0))
    return pl.pallas_call(
        _attn_prompt_kernel,
        grid=(bsz, nt),
        in_specs=[cur, prev, cur, prev, cur, pl.BlockSpec(bias.shape, lambda b, t: (0, 0, 0))],
        out_specs=pl.BlockSpec((ATT_TILE, A_WIDTH), lambda b, t: (b * nt + t, 0)),
        out_shape=jax.ShapeDtypeStruct((bsz * s, A_WIDTH), f32),
        compiler_params=pltpu.CompilerParams(dimension_semantics=("parallel", "parallel"),
                                             vmem_limit_bytes=VMEM_LIMIT),
        name="attn_prompt",
    )(q, k, k, v, v, bias)


def _attn_sample(q, k, v, ck, cv, rel_bias, row0, bsz, t):
    l = ck.shape[1]
    assert row0 % t == 0
    keys = -(-(l + t) // LANES) * LANES
    padk = ((0, 0), (0, keys - l - t), (0, 0))
    kk = jnp.pad(jnp.concatenate([ck, k[row0:].reshape(bsz, t, A_WIDTH)], axis=1), padk)
    vv = jnp.pad(jnp.concatenate([cv, v[row0:].reshape(bsz, t, A_WIDTH)], axis=1), padk)
    j = jnp.arange(keys)[None, :]
    bias = _rel_bias_table(rel_bias, t, keys, l, jnp.broadcast_to(j < l + t, (t, keys)))
    return pl.pallas_call(
        _attn_sample_kernel,
        grid=(bsz,),
        in_specs=[pl.BlockSpec((t, A_WIDTH), lambda b: (row0 // t + b, 0)),
                  pl.BlockSpec((1, keys, A_WIDTH), lambda b: (b, 0, 0)),
                  pl.BlockSpec((1, keys, A_WIDTH), lambda b: (b, 0, 0)),
                  pl.BlockSpec(bias.shape, lambda b: (0, 0, 0))],
        out_specs=pl.BlockSpec((t, A_WIDTH), lambda b: (b, 0)),
        out_shape=jax.ShapeDtypeStruct((bsz * t, A_WIDTH), f32),
        compiler_params=pltpu.CompilerParams(dimension_semantics=("parallel",), vmem_limit_bytes=VMEM_LIMIT),
        name="attn_sample",
    )(q, kk, vv, bias)


def _mlstm_kernel(qk_ref, v_ref, o_ref, gate_ref, c0_ref, n0_ref, m0_ref, cbuf_ref,
                  cw_ref, cb_ref, bg_ref, mhg_ref,
                  h_ref, cout_ref, nout_ref, mout_ref,
                  c_s, n_s, m_s, prev_s, *, lc):
    c = pl.program_id(1)

    @pl.when(c == 0)
    def _():
        c_s[...] = c0_ref[0]
        n_s[...] = n0_ref[0]
        m_s[...] = m0_ref[0]
        prev_s[...] = cbuf_ref[0]

    a = qk_ref[...]
    ext = jnp.concatenate([prev_s[...], a], axis=0)
    conv = cb_ref[...]
    for j in range(CONV_W):
        lo = SUBLANES - (CONV_W - 1) + j
        conv = conv + cw_ref[j:j + 1, :] * ext[lo:lo + lc]
    prev_s[...] = a[lc - SUBLANES:lc]
    qk = conv * jax.nn.sigmoid(conv)

    z = gate_ref[...] + bg_ref[...]
    lane = lax.broadcasted_iota(i32, (lc, LANES), 1)
    row = lax.broadcasted_iota(i32, (lc, LANES), 0)
    logf = jnp.minimum(z, 0.0) - jnp.log1p(jnp.exp(-jnp.abs(z)))
    cum = jnp.where((lane >= M_HEADS) & (lane < 2 * M_HEADS), logf, 0.0)
    shift = 1
    while shift < lc:
        cum = cum + jnp.where(row >= shift, pltpu.roll(cum, shift, axis=0), 0.0)
        shift *= 2
    zc = jnp.where(lane < M_HEADS, z, cum)
    zt = jnp.concatenate([zc, jnp.zeros((LANES - lc, LANES), f32)], axis=0).T[:, :lc]

    ri = lax.broadcasted_iota(i32, (lc, lc), 0)
    ci = lax.broadcasted_iota(i32, (lc, lc), 1)
    causal = ri >= ci
    vall = v_ref[...]
    hs = []
    for h in range(M_HEADS):
        sl = slice(h * M_DH, (h + 1) * M_DH)
        q = qk[:, sl]
        k = qk[:, M_WIDTH + h * M_DH:M_WIDTH + (h + 1) * M_DH] * (M_DH ** -0.5)
        v = vall[:, sl]
        i_col = zc[:, h:h + 1]
        b_col = zc[:, M_HEADS + h:M_HEADS + h + 1]
        i_row = zt[h:h + 1, :]
        b_row = zt[M_HEADS + h:M_HEADS + h + 1, :]
        m_prev = m_s[h:h + 1, 0:1]
        c_prev = c_s[h]
        n_prev = n_s[h:h + 1, :]

        dmat = jnp.where(causal, b_col - b_row + i_row, NEG)
        inter = b_col + m_prev
        mt = jnp.maximum(jnp.max(dmat, axis=-1, keepdims=True), inter)
        w_intra = jnp.exp(dmat - mt)
        w_inter = jnp.exp(inter - mt)
        qb, kb, vb = q.astype(bf16), k.astype(bf16), v.astype(bf16)
        s = lax.dot_general(qb, kb, (((1,), (1,)), ((), ())), preferred_element_type=f32) * w_intra
        num = (w_inter * jnp.dot(qb, c_prev.astype(bf16), preferred_element_type=f32)
               + jnp.dot(s.astype(bf16), vb, preferred_element_type=f32))
        den = w_inter * jnp.sum(q * n_prev, axis=-1, keepdims=True) + jnp.sum(s, axis=-1, keepdims=True)
        hh = num / jnp.maximum(jnp.abs(den), jnp.exp(-mt))
        m_new = mt[lc - 1:lc, :]
        b_last = b_col[lc - 1:lc, :]
        w_s = jnp.exp(b_last - b_col + i_col - m_new)
        decay = jnp.exp(b_last + m_prev - m_new)
        kw = k * w_s
        c_s[h] = decay * c_prev + lax.dot_general(kw.astype(bf16), vb, (((0,), (0,)), ((), ())),
                                                   preferred_element_type=f32)
        n_s[h:h + 1, :] = decay * n_prev + jnp.sum(kw, axis=0, keepdims=True)
        m_s[h:h + 1, :] = jnp.broadcast_to(m_new, (1, LANES))
        hs.append(hh * lax.rsqrt(jnp.mean(hh * hh, axis=-1, keepdims=True) + EPS))

    h_all = jnp.concatenate(hs, axis=-1)
    h_ref[...] = h_all * mhg_ref[...] * jax.nn.sigmoid(o_ref[...])

    @pl.when(c == pl.num_programs(1) - 1)
    def _():
        cout_ref[0] = c_s[...]
        nout_ref[0] = n_s[...]
        mout_ref[0] = m_s[...]


def _mlstm(mqk, mv, mo, gates, row0, bsz, t, c0, n0, m0, cbuf, conv_w, conv_b, b_gates, mh_g):
    lc = min(CHUNK, t)
    nc = t // lc
    assert t % lc == 0 and lc % SUBLANES == 0 and row0 % lc == 0
    n0p = jnp.pad(n0.astype(f32), ((0, 0), (0, SUBLANES - M_HEADS), (0, 0)))
    m0p = jnp.pad(jnp.broadcast_to(m0.astype(f32)[:, :, None], (bsz, M_HEADS, LANES)),
                  ((0, 0), (0, SUBLANES - M_HEADS), (0, 0)))
    cbp = jnp.pad(cbuf.astype(f32), ((0, 0), (SUBLANES - (CONV_W - 1), 0), (0, 0)))
    bg = jnp.pad(b_gates.astype(f32), (0, LANES - 2 * M_HEADS)).reshape(1, LANES)
    seq = lambda w: pl.BlockSpec((lc, w), lambda b, c: (row0 // lc + b * nc + c, 0))
    out_seq = pl.BlockSpec((lc, M_WIDTH), lambda b, c: (b * nc + c, 0))
    per_b = lambda shp: pl.BlockSpec((1,) + shp, lambda b, c: (b,) + (0,) * len(shp))
    full = lambda a: pl.BlockSpec(a.shape, lambda b, c: (0,) * a.ndim)
    cb = conv_b.reshape(1, -1)
    mhg = mh_g.reshape(1, -1)
    return pl.pallas_call(
        functools.partial(_mlstm_kernel, lc=lc),
        grid=(bsz, nc),
        in_specs=[seq(2 * M_WIDTH), seq(M_WIDTH), seq(M_WIDTH), seq(LANES),
                  per_b((M_HEADS, M_DH, M_DH)), per_b((SUBLANES, M_DH)), per_b((SUBLANES, LANES)),
                  per_b((SUBLANES, 2 * M_WIDTH)),
                  full(conv_w), full(cb), full(bg), full(mhg)],
        out_specs=[out_seq, per_b((M_HEADS, M_DH, M_DH)), per_b((SUBLANES, M_DH)), per_b((SUBLANES, LANES))],
        out_shape=[jax.ShapeDtypeStruct((bsz * t, M_WIDTH), f32),
                   jax.ShapeDtypeStruct((bsz, M_HEADS, M_DH, M_DH), f32),
                   jax.ShapeDtypeStruct((bsz, SUBLANES, M_DH), f32),
                   jax.ShapeDtypeStruct((bsz, SUBLANES, LANES), f32)],
        scratch_shapes=[pltpu.VMEM((M_HEADS, M_DH, M_DH), f32), pltpu.VMEM((SUBLANES, M_DH), f32),
                        pltpu.VMEM((SUBLANES, LANES), f32), pltpu.VMEM((SUBLANES, 2 * M_WIDTH), f32)],
        compiler_params=pltpu.CompilerParams(dimension_semantics=("parallel", "arbitrary"),
                                             vmem_limit_bytes=VMEM_LIMIT),
        name="mlstm",
    )(mqk, mv, mo, gates, c0.astype(f32), n0p, m0p, cbp, conv_w, cb, bg, mhg)


def _topk_rows(s, k, payload=None):
    n = s.shape[0]
    rows = lax.broadcasted_iota(i32, s.shape, 0).astype(f32)
    vals, ids = [], []
    for _ in range(k):
        m = jnp.max(s, axis=0, keepdims=True)
        pos = jnp.min(jnp.where(s == m, rows, float(n)), axis=0, keepdims=True)
        sel = rows == pos
        vals.append(m)
        ids.append(pos if payload is None else jnp.max(jnp.where(sel, payload, -1.0), axis=0, keepdims=True))
        s = jnp.where(sel, -jnp.inf, s)
    return jnp.concatenate(vals, axis=0), jnp.concatenate(ids, axis=0)


def _mid_kernel(xp_ref, xs_ref, attp_ref, atts_ref, hp_ref, hs_ref, wo_ref, g2_ref, wq_ref, keys_ref,
                x1_ref, xn_ref, eidx_ref, gate_ref, *, nbp):
    cat = jnp.concatenate([_pick(nbp, attp_ref, atts_ref), _pick(nbp, hp_ref, hs_ref)], axis=-1).astype(bf16)
    x1 = _pick(nbp, xp_ref, xs_ref) + jnp.dot(cat, wo_ref[...], preferred_element_type=f32)
    x1_ref[...] = x1
    xn = _rms(x1, g2_ref[...])
    xn_ref[...] = xn
    xb = xn.astype(bf16)
    e_rows, g_rows = [], []
    for h in range(P_HEADS):
        q = jnp.dot(xb, wq_ref[:, h * P_DKEY:(h + 1) * P_DKEY], preferred_element_type=f32).astype(bf16)
        half = []
        for p in range(2):
            st = lax.dot_general(keys_ref[h, p], q[:, p * N_KEYS:(p + 1) * N_KEYS], (((1,), (1,)), ((), ())),
                                 preferred_element_type=f32)
            half.append(_topk_rows(st, P_TOPK))
        (v0, i0), (v1, i1) = half
        width = [P_TOPK // (a + 1) for a in range(P_TOPK)]
        fill = -sum(width) % SUBLANES
        cand = jnp.concatenate([v0[a:a + 1] + v1[:width[a]] for a in range(P_TOPK)]
                               + [jnp.full((fill, v0.shape[1]), -jnp.inf, f32)], axis=0)
        cidx = jnp.concatenate([i0[a:a + 1] * float(N_KEYS) + i1[:width[a]] for a in range(P_TOPK)]
                               + [jnp.zeros((fill, v0.shape[1]), f32)], axis=0)
        top_s, eid = _topk_rows(cand, P_TOPK, payload=cidx)
        ex = jnp.exp(top_s - top_s[0:1])
        e_rows.append(eid)
        g_rows.append(ex / jnp.sum(ex, axis=0, keepdims=True))
    eidx_ref[...] = jnp.concatenate(e_rows, axis=0).T.astype(i32)
    gate_ref[...] = jnp.concatenate(g_rows, axis=0).T


def _mid(xp, xp_row0, xs, attp, atts, hp, hs, n_a, n_b, w_out, g2, wq, keys):
    n = n_a + n_b
    nbp = n_a // ROW_BLOCK
    assert n_a % ROW_BLOCK == 0 and n_b % ROW_BLOCK == 0 and xp_row0 % ROW_BLOCK == 0
    row = lambda w: pl.BlockSpec((ROW_BLOCK, w), lambda i: (i, 0))
    full = lambda a: pl.BlockSpec(a.shape, lambda i: (0,) * a.ndim)
    wo = w_out.astype(bf16)
    wqb = wq.astype(bf16)
    kb = keys.astype(bf16)
    g = g2.reshape(1, D_MODEL)
    return pl.pallas_call(
        functools.partial(_mid_kernel, nbp=nbp),
        grid=(n // ROW_BLOCK,),
        in_specs=[*_pair_specs(nbp, D_MODEL, xp_row0 // ROW_BLOCK), *_pair_specs(nbp, A_WIDTH),
                  *_pair_specs(nbp, M_WIDTH), full(wo), full(g), full(wqb), full(kb)],
        out_specs=[row(D_MODEL), row(D_MODEL), row(P_SLOTS), row(P_SLOTS)],
        out_shape=[jax.ShapeDtypeStruct((n, D_MODEL), f32), jax.ShapeDtypeStruct((n, D_MODEL), f32),
                   jax.ShapeDtypeStruct((n, P_SLOTS), i32), jax.ShapeDtypeStruct((n, P_SLOTS), f32)],
        compiler_params=pltpu.CompilerParams(dimension_semantics=("parallel",), vmem_limit_bytes=VMEM_LIMIT),
        name="outproj_retrieve",
    )(xp, xs, attp, atts, hp, hs, wo, g, wqb, kb)


def _gather_rows(eidx_ref, tab_ref, g_ref, t):
    for r in range(P_SLOTS):
        g_ref[pl.ds(r * ROW_TILES, ROW_TILES), :] = tab_ref[eidx_ref[t, r]]


def _tile_rows(t, rows=SUBLANES):
    return pl.ds(pl.multiple_of(t * rows, rows), rows)


def _pipelined_tokens(nt, eidx_ref, tab_ref, g0_s, g1_s, compute):
    bufs = (g0_s, g1_s)
    _gather_rows(eidx_ref, tab_ref, g0_s, 0)

    def body(j, carry):
        for u in range(TOKEN_UNROLL):
            t = TOKEN_UNROLL * j + u
            _gather_rows(eidx_ref, tab_ref, bufs[(u + 1) % 2], jnp.minimum(t + 1, nt - 1))
            compute(t, bufs[u % 2])
        return carry

    lax.fori_loop(0, nt // TOKEN_UNROLL, body, 0)


def _gelu_tanh(x):
    return 0.5 * x * (1.0 + jnp.tanh(math.sqrt(2.0 / math.pi) * (x + 0.044715 * (x * x * x))))


def _peer_u_kernel(eidx_ref, xn_ref, g_ref, tab_ref, w_ref, xl_s, r_s, g0_s, g1_s):
    nt = xn_ref.shape[0]
    xn = xn_ref[...]
    xh = xn.astype(bf16).astype(f32)
    xl = xn - xh
    for k in range(ROW_TILES):
        xl_s[pl.ds(k, nt, stride=2 * SUBLANES), :] = xh[:, k * LANES:(k + 1) * LANES]
        xl_s[pl.ds(SUBLANES + k, nt, stride=2 * SUBLANES), :] = xl[:, k * LANES:(k + 1) * LANES]
    diag = (lax.broadcasted_iota(i32, (SUBLANES, G_ROWS), 1) % ROW_TILES
            == lax.broadcasted_iota(i32, (SUBLANES, G_ROWS), 0))

    def compute(t, g_s):
        lhs = xl_s[_tile_rows(t, 2 * SUBLANES), :].astype(bf16)
        out = lax.dot_general(lhs, g_s[...], (((1,), (1,)), ((), ())), preferred_element_type=f32)
        part = jnp.where(diag, out[:SUBLANES] + out[SUBLANES:], 0.0)
        for c in range(ROW_TILES):
            r_s[c, _tile_rows(t), :] = part[:, c * LANES:(c + 1) * LANES]

    _pipelined_tokens(nt, eidx_ref, tab_ref, g0_s, g1_s, compute)
    cols = []
    for c in range(ROW_TILES):
        acc = r_s[c, pl.ds(0, nt, stride=SUBLANES), :]
        for k in range(1, SUBLANES):
            acc = acc + r_s[c, pl.ds(k, nt, stride=SUBLANES), :]
        cols.append(acc)
    s = jnp.concatenate(cols, axis=-1)
    fold = (lax.broadcasted_iota(i32, (G_ROWS, P_SLOTS), 0) // ROW_TILES
            == lax.broadcasted_iota(i32, (G_ROWS, P_SLOTS), 1)).astype(bf16)
    sh, sl = _split_bf16(s)
    act = jnp.dot(sh, fold, preferred_element_type=f32) + jnp.dot(sl, fold, preferred_element_type=f32)
    w_ref[...] = g_ref[...] * _gelu_tanh(act)


def _peer_v_kernel(eidx_ref, w_ref, tab_ref, peer_ref, wl_s, o_s, g0_s, g1_s):
    nt = w_ref.shape[0]
    spread = (lax.broadcasted_iota(i32, (P_SLOTS, G_ROWS), 1) // ROW_TILES
              == lax.broadcasted_iota(i32, (P_SLOTS, G_ROWS), 0)).astype(bf16)
    wexp = jnp.dot(w_ref[...].astype(bf16), spread, preferred_element_type=f32)
    lane = lax.broadcasted_iota(i32, (nt, LANES), 1)
    for c in range(ROW_TILES):
        wc = wexp[:, c * LANES:(c + 1) * LANES]
        for k in range(SUBLANES):
            wl_s[c, pl.ds(k, nt, stride=SUBLANES), :] = jnp.where(lane % ROW_TILES == k, wc, 0.0)

    def compute(t, g_s):
        lhs = jnp.concatenate([wl_s[c, _tile_rows(t), :] for c in range(ROW_TILES)], axis=-1).astype(bf16)
        o_s[_tile_rows(t), :] = jnp.dot(lhs, g_s[...], preferred_element_type=f32)

    _pipelined_tokens(nt, eidx_ref, tab_ref, g0_s, g1_s, compute)
    for k in range(ROW_TILES):
        peer_ref[:, k * LANES:(k + 1) * LANES] = o_s[pl.ds(k, nt, stride=SUBLANES), :]


def _final_kernel(pa_ref, pb_ref, xa_ref, xb_ref, gf_ref, yp_ref, ys_ref, *, nba, nbp):
    y = _rms(_pick(nba, xa_ref, xb_ref) + _pick(nba, pa_ref, pb_ref), gf_ref[...])

    @pl.when(pl.program_id(0) < nbp)
    def _():
        yp_ref[...] = y

    @pl.when(pl.program_id(0) >= nbp)
    def _():
        ys_ref[...] = y


def _final(peer_a, peer_b, x1_a, x1_b, gf, n_p):
    n = x1_a.shape[0] + x1_b.shape[0]
    nba, nbp = x1_a.shape[0] // ROW_BLOCK, n_p // ROW_BLOCK
    g = gf.reshape(1, D_MODEL)
    return pl.pallas_call(
        functools.partial(_final_kernel, nba=nba, nbp=nbp),
        grid=(n // ROW_BLOCK,),
        in_specs=[*_pair_specs(nba, D_MODEL), *_pair_specs(nba, D_MODEL), pl.BlockSpec(g.shape, lambda i: (0, 0))],
        out_specs=list(_pair_specs(nbp, D_MODEL)),
        out_shape=[jax.ShapeDtypeStruct((n_p, D_MODEL), f32), jax.ShapeDtypeStruct((n - n_p, D_MODEL), f32)],
        compiler_params=pltpu.CompilerParams(dimension_semantics=("arbitrary",), vmem_limit_bytes=VMEM_LIMIT),
        name="final_norm",
    )(peer_a, peer_b, x1_a, x1_b, g)


def _peer_specs():
    row = lambda w: pl.BlockSpec((ROW_BLOCK, w), lambda i: (i, 0))
    idx = pl.BlockSpec((ROW_BLOCK, P_SLOTS), lambda i: (i, 0), memory_space=pltpu.SMEM)
    tab = pl.BlockSpec(memory_space=pltpu.VMEM)
    gscr = pltpu.VMEM((G_ROWS, LANES), bf16)
    params = pltpu.CompilerParams(dimension_semantics=("arbitrary",), vmem_limit_bytes=VMEM_LIMIT)
    return row, idx, tab, gscr, params


def _expert_table(tab):
    return tab.astype(bf16).reshape(tab.shape[0], ROW_TILES, LANES)


def _peer_u(eidx, xn, g, utab, n):
    row, idx, tab, gscr, params = _peer_specs()
    return pl.pallas_call(
        _peer_u_kernel,
        grid=(n // ROW_BLOCK,),
        in_specs=[idx, row(D_MODEL), row(P_SLOTS), tab],
        out_specs=row(P_SLOTS),
        out_shape=jax.ShapeDtypeStruct((n, P_SLOTS), f32),
        scratch_shapes=[pltpu.VMEM((ROW_BLOCK * 2 * SUBLANES, LANES), f32),
                        pltpu.VMEM((ROW_TILES, ROW_BLOCK * SUBLANES, LANES), f32), gscr, gscr],
        compiler_params=params,
        name="peer_u",
    )(eidx, xn, g, utab)


def _peer_v(eidx, w, vtab):
    n = w.shape[0]
    row, idx, tab, gscr, params = _peer_specs()
    return pl.pallas_call(
        _peer_v_kernel,
        grid=(n // ROW_BLOCK,),
        in_specs=[idx, row(P_SLOTS), tab],
        out_specs=row(D_MODEL),
        out_shape=jax.ShapeDtypeStruct((n, D_MODEL), f32),
        scratch_shapes=[pltpu.VMEM((ROW_TILES, ROW_BLOCK * SUBLANES, LANES), f32),
                        pltpu.VMEM((ROW_BLOCK * SUBLANES, LANES), f32), gscr, gscr],
        compiler_params=params,
        name="peer_v",
    )(eidx, w, vtab)


def _sc_table(tab):
    e = tab.shape[0]
    t = tab.astype(bf16).reshape(e, SC_CHUNKS, 2, SC_LANES).transpose(0, 1, 3, 2)
    return lax.bitcast_convert_type(t, i32).reshape(e, SC_WORDS)


def _sc_unpack(words):
    return plsc.unpack(plsc.bitcast(words, bf16), format=plsc.PackFormat.INTERLEAVED)


def _sc_gelu_tanh(x):
    z = math.sqrt(2.0 / math.pi) * (x + 0.044715 * (x * x * x))
    return 0.5 * x * (2.0 - 2.0 / (jnp.exp(2.0 * z) + 1.0))


def _sc_peer(utab32, vtab32, eidx, xn, gate):
    n_sc = eidx.shape[0]
    per = n_sc // SC_WORKERS
    units = 4 * per
    assert n_sc % (SC_WORKERS * SUBLANES) == 0

    def body(u_hbm, v_hbm, eidx_hbm, x_hbm, g_hbm, out_hbm,
             idx_v, x_v, g_v, rows_v, acc_v, w_v, w16_v, out_v, row_sems, tok_sems, out_sem):
        base = (lax.axis_index("s") * 2 + lax.axis_index("c")) * per
        lanes = lax.broadcasted_iota(i32, (SC_LANES,), 0)

        def token_copies(tok):
            tslot = tok % 2
            return (pltpu.make_async_copy(eidx_hbm.at[base + tok], idx_v.at[tslot], tok_sems.at[0]),
                    pltpu.make_async_copy(x_hbm.at[base + tok], x_v.at[tslot], tok_sems.at[1]),
                    pltpu.make_async_copy(g_hbm.at[base + tok], g_v.at[tslot], tok_sems.at[2]))

        def store_out(tok):
            return pltpu.make_async_copy(out_v, out_hbm.at[base + tok], out_sem)

        def gather(tab_hbm, g):
            tok, k = g // 4, g % 4
            return pltpu.make_async_copy(tab_hbm.at[idx_v.at[tok % 2, pl.ds((k % 2) * SC_UNIT_ROWS, SC_UNIT_ROWS)]],
                                         rows_v.at[k % 2], row_sems.at[k % 2])

        def start(g):
            @pl.when(g % 4 < 2)
            def _():
                gather(u_hbm, g).start()

            @pl.when(g % 4 >= 2)
            def _():
                gather(v_hbm, g).start()

        def compute_u(tslot, half, rows):
            @pl.loop(0, SC_UNIT_ROWS // SC_ROW_GROUP)
            def _(rg):
                slot0 = half * SC_UNIT_ROWS + rg * SC_ROW_GROUP

                @pl.loop(0, SC_CHUNKS // SC_CHUNK_GROUP)
                def _(cg):
                    keep = (cg > 0).astype(f32)
                    accs = [acc_v[pl.ds((slot0 + r) * SC_LANES, SC_LANES)] * keep for r in range(SC_ROW_GROUP)]
                    for c in range(SC_CHUNK_GROUP):
                        ch = cg * SC_CHUNK_GROUP + c
                        xa = x_v[tslot, pl.ds(2 * ch * SC_LANES, SC_LANES)]
                        xb = x_v[tslot, pl.ds((2 * ch + 1) * SC_LANES, SC_LANES)]
                        for r in range(SC_ROW_GROUP):
                            a, b = _sc_unpack(rows[rg * SC_ROW_GROUP + r, pl.ds(ch * SC_LANES, SC_LANES)])
                            accs[r] = accs[r] + a * xa + b * xb
                    for r in range(SC_ROW_GROUP):
                        acc_v[pl.ds((slot0 + r) * SC_LANES, SC_LANES)] = accs[r]

        def gate_weights(tslot):
            @pl.loop(0, P_SLOTS // SC_LANES)
            def _(sg):
                first = sg * SC_LANES * SC_LANES
                act = jnp.zeros((SC_LANES,), f32)
                for lane in range(SC_LANES):
                    act = act + plsc.load_gather(acc_v, [first + lanes * SC_LANES + lane])
                w_v[pl.ds(sg * SC_LANES, SC_LANES)] = g_v[tslot, pl.ds(sg * SC_LANES, SC_LANES)] * _sc_gelu_tanh(act)

            @pl.loop(0, P_SLOTS // SC_LANES)
            def _(sg):
                for r in range(SC_LANES):
                    w16_v[pl.ds((sg * SC_LANES + r) * SC_LANES, SC_LANES)] = plsc.load_gather(
                        w_v, [jnp.zeros((SC_LANES,), i32) + (sg * SC_LANES + r)])

        def compute_v(half, rows):
            @pl.loop(0, SC_CHUNKS // SC_CHUNK_GROUP)
            def _(cg):
                first = cg * SC_CHUNK_GROUP

                @pl.loop(0, SC_UNIT_ROWS // SC_ROW_GROUP)
                def _(rg):
                    keep = jnp.logical_or(half == 1, rg > 0).astype(f32)
                    accs = [out_v[pl.ds((2 * first + j) * SC_LANES, SC_LANES)] * keep for j in range(2 * SC_CHUNK_GROUP)]
                    for r in range(SC_ROW_GROUP):
                        row = rg * SC_ROW_GROUP + r
                        wv = w16_v[pl.ds((half * SC_UNIT_ROWS + row) * SC_LANES, SC_LANES)]
                        for c in range(SC_CHUNK_GROUP):
                            a, b = _sc_unpack(rows[row, pl.ds((first + c) * SC_LANES, SC_LANES)])
                            accs[2 * c] = accs[2 * c] + a * wv
                            accs[2 * c + 1] = accs[2 * c + 1] + b * wv
                    for j in range(2 * SC_CHUNK_GROUP):
                        out_v[pl.ds((2 * first + j) * SC_LANES, SC_LANES)] = accs[j]

        for cp in token_copies(0):
            cp.start()
        for cp in token_copies(0):
            cp.wait()
        start(0)

        @pl.loop(0, units)
        def _(g):
            tok, k = g // 4, g % 4

            @pl.when(jnp.logical_and(k == 0, tok + 1 < per))
            def _():
                for cp in token_copies(tok + 1):
                    cp.start()

            @pl.when(jnp.logical_and(k == 3, tok + 1 < per))
            def _():
                for cp in token_copies(tok + 1):
                    cp.wait()

            @pl.when(g + 1 < units)
            def _():
                start(g + 1)

            gather(u_hbm, g).wait()
            rows = rows_v.at[k % 2]

            @pl.when(k < 2)
            def _():
                compute_u(tok % 2, k, rows)

            @pl.when(k == 1)
            def _():
                gate_weights(tok % 2)

            @pl.when(jnp.logical_and(k == 2, tok > 0))
            def _():
                store_out(tok - 1).wait()

            @pl.when(k >= 2)
            def _():
                compute_v(k - 2, rows)

            @pl.when(k == 3)
            def _():
                store_out(tok).start()

        store_out(per - 1).wait()

    return pl.kernel(
        body, mesh=plsc.VectorSubcoreMesh(core_axis_name="c", subcore_axis_name="s"),
        out_type=jax.ShapeDtypeStruct((n_sc, D_MODEL), f32),
        scratch_types=[pltpu.VMEM((2, P_SLOTS), i32), pltpu.VMEM((2, D_MODEL), f32), pltpu.VMEM((2, P_SLOTS), f32),
                       pltpu.VMEM((2, SC_UNIT_ROWS, SC_WORDS), i32), pltpu.VMEM((P_SLOTS * SC_LANES,), f32),
                       pltpu.VMEM((P_SLOTS,), f32), pltpu.VMEM((P_SLOTS * SC_LANES,), f32), pltpu.VMEM((D_MODEL,), f32),
                       pltpu.SemaphoreType.DMA((2,)), pltpu.SemaphoreType.DMA((3,)), pltpu.SemaphoreType.DMA],
        compiler_params=pltpu.CompilerParams(needs_layout_passes=False),
        name="sc_peer",
    )(utab32, vtab32, eidx, xn, gate)


def kernel(x_prompt, x_sample, cache_k, cache_v, state_C, state_n, state_m, state_conv, norm1_g, w_in, b_gates, rel_bias, conv_w, conv_b, mh_norm_g, w_out, norm2_g, peer_wq, peer_keys, peer_u, peer_v, final_g):
    bp, sp, d = x_prompt.shape
    bs, ts, _ = x_sample.shape
    n_p, n_s = bp * sp, bs * ts
    n = n_p + n_s
    assert n_p % ROW_BLOCK == 0 and n_s % ROW_BLOCK == 0 and d == D_MODEL
    depth = w_in.shape[0]
    assert depth == 1, "the final norm is fused into the last layer's PEER pass"
    l = 0
    xp, xs = x_prompt.reshape(n_p, d), x_sample.reshape(n_s, d)

    aq, ak, av, mqk, mv, mo, gates = _inproj(xp, xs, norm1_g[l], w_in[l])
    zeros = lambda *shp: jnp.zeros(shp, f32)
    mparams = (conv_w[l], conv_b[l], b_gates[l], mh_norm_g[l])
    retrieval = (w_out[l], norm2_g[l], peer_wq[l], peer_keys[l])

    def prompt_mixers(b0, nb):
        att = _attn_prompt(aq, ak, av, rel_bias[l], b0, nb, sp)
        return (att,) + tuple(_mlstm(mqk, mv, mo, gates, b0 * sp, nb, sp, zeros(nb, M_HEADS, M_DH, M_DH),
                                     zeros(nb, M_HEADS, M_DH), zeros(nb, M_HEADS), zeros(nb, CONV_W - 1, 2 * M_WIDTH),
                                     *mparams))

    assert 0 < SC_PROMPT_SEQS < bp
    n_sc = SC_PROMPT_SEQS * sp
    att_a, h_a, c_a, nn_a, mm_a = prompt_mixers(0, SC_PROMPT_SEQS)
    x1_a, xn_a, eidx_a, gate_a = _mid(xp, 0, xs, att_a, att_a, h_a, h_a, n_sc, 0, *retrieval)
    peer_a = _sc_peer(_sc_table(peer_u[l]), _sc_table(peer_v[l]), eidx_a, xn_a, gate_a)

    att_b, h_b, c_b, nn_b, mm_b = prompt_mixers(SC_PROMPT_SEQS, bp - SC_PROMPT_SEQS)
    lcache = cache_k.shape[2]
    att_s = _attn_sample(aq, ak, av, cache_k[l].reshape(bs, lcache, A_WIDTH),
                         cache_v[l].reshape(bs, lcache, A_WIDTH), rel_bias[l], n_p, bs, ts)
    h_s, c_s, nn_s, mm_s = _mlstm(mqk, mv, mo, gates, n_p, bs, ts, state_C[l], state_n[l], state_m[l],
                                  state_conv[l], *mparams)
    x1_b, xn_b, eidx_b, gate_b = _mid(xp, n_sc, xs, att_b, att_s, h_b, h_s, n_p - n_sc, n_s, *retrieval)
    w_b = _peer_u(eidx_b, xn_b, gate_b, _expert_table(peer_u[l]), n - n_sc)
    peer_b = _peer_v(eidx_b, w_b, _expert_table(peer_v[l]))
    y_p, y_s = _final(peer_a, peer_b, x1_a, x1_b, final_g, n_p)
    c_p, nn_p, mm_p = (jnp.concatenate(ab, axis=0) for ab in ((c_a, c_b), (nn_a, nn_b), (mm_a, mm_b)))

    def tail(a, row0, bsz, t, keep):
        return jnp.stack([a[row0 + (b + 1) * t - keep:row0 + (b + 1) * t] for b in range(bsz)])

    keep = min(WINDOW, sp)
    heads = lambda a: a.reshape(a.shape[0], a.shape[1], A_HEADS, A_DH)
    ctail = CONV_W - 1
    conv_tail = lambda buf, a, row0, bsz, t: jnp.concatenate([buf.astype(a.dtype), tail(a, row0, bsz, t, min(ctail, t))],
                                                             axis=1)[:, -ctail:]
    st = lambda a: a[None]
    return (y_p.reshape(bp, sp, d), y_s.reshape(bs, ts, d),
            st(heads(tail(ak, 0, bp, sp, keep))), st(heads(tail(av, 0, bp, sp, keep))),
            st(c_p), st(nn_p[:, :M_HEADS]), st(mm_p[:, :M_HEADS, 0]),
            st(conv_tail(zeros(bp, ctail, 2 * M_WIDTH), mqk, 0, bp, sp)),
            st(heads(ak[n_p:].reshape(bs, ts, A_WIDTH))), st(heads(av[n_p:].reshape(bs, ts, A_WIDTH))),
            st(c_s), st(nn_s[:, :M_HEADS]), st(mm_s[:, :M_HEADS, 0]),
            st(conv_tail(state_conv[l], mqk, n_p, bs, ts)))
```

```python
import functools
import math

import jax
import jax.numpy as jnp
from jax import lax
from jax.experimental import pallas as pl
from jax.experimental.pallas import tpu as pltpu
from jax.experimental.pallas import tpu_sc as plsc

f32 = jnp.float32
bf16 = jnp.bfloat16
i32 = jnp.int32

D_MODEL = 1024
CHUNK = 64
A_HEADS = 8
A_DH = 64
A_WIDTH = A_HEADS * A_DH
BAND_CHUNKS = 8
WINDOW = BAND_CHUNKS * CHUNK
MAX_REL = 128
ATT_SCALE = A_DH ** -0.5
M_HEADS = 4
M_DH = 128
M_WIDTH = M_HEADS * M_DH
CONV_W = 4
P_HEADS = 8
P_DKEY = 256
N_KEYS = 128
P_TOPK = 16
P_SLOTS = P_HEADS * P_TOPK
EPS = 1e-6
NEG = -1e30

LANES = 128
SUBLANES = 8
ROW_BLOCK = 256
ATT_TILE = 512
ATT_SUB = 128
ATT_KEYS = ATT_SUB + WINDOW
ROW_TILES = D_MODEL // LANES
G_ROWS = P_SLOTS * ROW_TILES
TOKEN_UNROLL = 8
VMEM_LIMIT = 56 * 1024 * 1024

SC_WORKERS = 32
SC_LANES = 16
SC_UNIT_ROWS = P_SLOTS // 2
SC_ROW_GROUP = 16
SC_WORDS = D_MODEL // 2
SC_CHUNKS = SC_WORDS // SC_LANES
SC_CHUNK_GROUP = 8
SC_PROMPT_SEQS = 1


def _rms(x, g):
    return x * lax.rsqrt(jnp.mean(x * x, axis=-1, keepdims=True) + EPS) * g


def _split_bf16(x):
    hi = x.astype(bf16)
    lo = (x - hi.astype(f32)).astype(bf16)
    return hi, lo


def _pair_specs(nbp, width, first=0):
    return (pl.BlockSpec((ROW_BLOCK, width), lambda i: (first + jnp.minimum(i, nbp - 1), 0)),
            pl.BlockSpec((ROW_BLOCK, width), lambda i: (jnp.maximum(i - nbp, 0), 0)))


def _pick(nbp, p_ref, s_ref):
    return jnp.where(pl.program_id(0) < nbp, p_ref[...], s_ref[...])


def _inproj_kernel(xp_ref, xs_ref, g_ref, w_ref, wgh_ref, wgl_ref,
                   aq_ref, ak_ref, av_ref, mqk_ref, mv_ref, mo_ref, gate_ref, *, nbp):
    xn = _rms(_pick(nbp, xp_ref, xs_ref), g_ref[...])
    xh, xl = _split_bf16(xn)

    def proj(lo, hi):
        return jnp.dot(xh, w_ref[:, lo:hi], preferred_element_type=f32)

    aq_ref[...] = proj(0, 512)
    ak_ref[...] = proj(512, 1024)
    av_ref[...] = proj(1024, 1536)
    mqk_ref[...] = proj(1536, 2560)
    mv_ref[...] = proj(2560, 3072)
    mo_ref[...] = proj(3072, 3584)
    gate_ref[...] = (jnp.dot(xh, wgh_ref[...], preferred_element_type=f32)
                     + jnp.dot(xl, wgh_ref[...], preferred_element_type=f32)
                     + jnp.dot(xh, wgl_ref[...], preferred_element_type=f32))


def _inproj(xp, xs, g1, w_in):
    n = xp.shape[0] + xs.shape[0]
    nbp = xp.shape[0] // ROW_BLOCK
    main = 3 * A_WIDTH + 4 * M_WIDTH
    w_main = w_in[:, :main].astype(bf16)
    wg = jnp.pad(w_in[:, main:], ((0, 0), (0, LANES - 2 * M_HEADS)))
    wgh, wgl = _split_bf16(wg)
    widths = (512, 512, 512, 1024, 512, 512, LANES)
    row = lambda w: pl.BlockSpec((ROW_BLOCK, w), lambda i: (i, 0))
    full = lambda a: pl.BlockSpec(a.shape, lambda i: (0,) * a.ndim)
    g = g1.reshape(1, D_MODEL)
    return pl.pallas_call(
        functools.partial(_inproj_kernel, nbp=nbp),
        grid=(n // ROW_BLOCK,),
        in_specs=[*_pair_specs(nbp, D_MODEL), full(g), full(w_main), full(wgh), full(wgl)],
        out_specs=[row(w) for w in widths],
        out_shape=[jax.ShapeDtypeStruct((n, w), f32) for w in widths],
        compiler_params=pltpu.CompilerParams(dimension_semantics=("parallel",), vmem_limit_bytes=VMEM_LIMIT),
        name="inproj",
    )(xp, xs, g, w_main, wgh, wgl)


def _attn_heads(q, k, v, bias_ref, key_ok):
    outs = []
    for h in range(A_HEADS):
        sl = slice(h * A_DH, (h + 1) * A_DH)
        s = lax.dot_general(q[:, sl], k[:, sl], (((1,), (1,)), ((), ())), preferred_element_type=f32)
        s = s * ATT_SCALE + bias_ref[h]
        if key_ok is not None:
            s = jnp.where(key_ok, s, NEG)
        m = jnp.max(s, axis=-1, keepdims=True)
        p = jnp.exp(s - m)
        l = jnp.sum(p, axis=-1, keepdims=True)
        o = jnp.dot(p.astype(bf16), v[:, sl], preferred_element_type=f32)
        outs.append(o / l)
    return jnp.concatenate(outs, axis=-1)


def _attn_prompt_kernel(q_ref, k0_ref, k1_ref, v0_ref, v1_ref, bias_ref, o_ref):
    t = pl.program_id(1)
    q = q_ref[...].astype(bf16)
    k = jnp.concatenate([k0_ref[...], k1_ref[...]], axis=0).astype(bf16)
    v = jnp.concatenate([v0_ref[...], v1_ref[...]], axis=0).astype(bf16)
    col = lax.broadcasted_iota(i32, (1, ATT_KEYS), 1)
    for s in range(ATT_TILE // ATT_SUB):
        lo = s * ATT_SUB
        key_ok = (t * ATT_TILE + lo + col) >= WINDOW
        o_ref[lo:lo + ATT_SUB, :] = _attn_heads(q[lo:lo + ATT_SUB], k[lo:lo + ATT_KEYS], v[lo:lo + ATT_KEYS],
                                                 bias_ref, key_ok)


def _attn_sample_kernel(q_ref, k_ref, v_ref, bias_ref, o_ref):
    o_ref[...] = _attn_heads(q_ref[...].astype(bf16), k_ref[0].astype(bf16), v_ref[0].astype(bf16), bias_ref, None)


def _rel_bias_table(rel_bias, rows, cols, offset, valid):
    span = rows + cols - 1
    rel = offset + rows - 1 - jnp.arange(span)
    diag = rel_bias[:, jnp.clip(rel, -MAX_REL, MAX_REL) + MAX_REL].astype(f32)
    diag = jnp.pad(diag, ((0, 0), (0, 1)))
    flat = jnp.tile(diag, (1, rows))[:, rows - 1:rows - 1 + rows * span]
    return jnp.where(valid[None], flat.reshape(-1, rows, span)[:, :, :cols], NEG)


def _attn_prompt(q, k, v, rel_bias, b0, bsz, s):
    assert s % ATT_TILE == 0 and WINDOW == ATT_TILE
    nt = s // ATT_TILE
    i = jnp.arange(ATT_SUB)[:, None]
    j = jnp.arange(ATT_KEYS)[None, :]
    off = j - (i // CHUNK) * CHUNK
    bias = _rel_bias_table(rel_bias, ATT_SUB, ATT_KEYS, WINDOW, (off >= 0) & (off < WINDOW + CHUNK))
    cur = pl.BlockSpec((ATT_TILE, A_WIDTH), lambda b, t: ((b0 + b) * nt + t, 0))
    prev = pl.BlockSpec((ATT_TILE, A_WIDTH), lambda b, t: ((b0 + b) * nt + jnp.maximum(t - 1, 0), 0))
    return pl.pallas_call(
        _attn_prompt_kernel,
        grid=(bsz, nt),
        in_specs=[cur, prev, cur, prev, cur, pl.BlockSpec(bias.shape, lambda b, t: (0, 0, 0))],
        out_specs=pl.BlockSpec((ATT_TILE, A_WIDTH), lambda b, t: (b * nt + t, 0)),
        out_shape=jax.ShapeDtypeStruct((bsz * s, A_WIDTH), f32),
        compiler_params=pltpu.CompilerParams(dimension_semantics=("parallel", "parallel"),
                                             vmem_limit_bytes=VMEM_LIMIT),
        name="attn_prompt",
    )(q, k, k, v, v, bias)


def _attn_sample(q, k, v, ck, cv, rel_bias, row0, bsz, t):
    l = ck.shape[1]
    assert row0 % t == 0
    keys = -(-(l + t) // LANES) * LANES
    padk = ((0, 0), (0, keys - l - t), (0, 0))
    kk = jnp.pad(jnp.concatenate([ck, k[row0:].reshape(bsz, t, A_WIDTH)], axis=1), padk)
    vv = jnp.pad(jnp.concatenate([cv, v[row0:].reshape(bsz, t, A_WIDTH)], axis=1), padk)
    j = jnp.arange(keys)[None, :]
    bias = _rel_bias_table(rel_bias, t, keys, l, jnp.broadcast_to(j < l + t, (t, keys)))
    return pl.pallas_call(
        _attn_sample_kernel,
        grid=(bsz,),
        in_specs=[pl.BlockSpec((t, A_WIDTH), lambda b: (row0 // t + b, 0)),
                  pl.BlockSpec((1, keys, A_WIDTH), lambda b: (b, 0, 0)),
                  pl.BlockSpec((1, keys, A_WIDTH), lambda b: (b, 0, 0)),
                  pl.BlockSpec(bias.shape, lambda b: (0, 0, 0))],
        out_specs=pl.BlockSpec((t, A_WIDTH), lambda b: (b, 0)),
        out_shape=jax.ShapeDtypeStruct((bsz * t, A_WIDTH), f32),
        compiler_params=pltpu.CompilerParams(dimension_semantics=("parallel",), vmem_limit_bytes=VMEM_LIMIT),
        name="attn_sample",
    )(q, kk, vv, bias)


def _mlstm_chunk(a, vall, o_in, gate_in, cw_ref, cb_ref, bg_ref, mhg_ref, c_s, n_s, m_s, prev_s):
    lc = a.shape[0]
    ext = jnp.concatenate([prev_s[...], a], axis=0)
    conv = cb_ref[...]
    for j in range(CONV_W):
        lo = SUBLANES - (CONV_W - 1) + j
        conv = conv + cw_ref[j:j + 1, :] * ext[lo:lo + lc]
    prev_s[...] = a[lc - SUBLANES:lc]
    qk = conv * jax.nn.sigmoid(conv)

    z = gate_in + bg_ref[...]
    lane = lax.broadcasted_iota(i32, (lc, LANES), 1)
    row = lax.broadcasted_iota(i32, (lc, LANES), 0)
    logf = jnp.minimum(z, 0.0) - jnp.log1p(jnp.exp(-jnp.abs(z)))
    cum = jnp.where((lane >= M_HEADS) & (lane < 2 * M_HEADS), logf, 0.0)
    shift = 1
    while shift < lc:
        cum = cum + jnp.where(row >= shift, pltpu.roll(cum, shift, axis=0), 0.0)
        shift *= 2
    zc = jnp.where(lane < M_HEADS, z, cum)
    zt = jnp.concatenate([zc, jnp.zeros((LANES - lc, LANES), f32)], axis=0).T[:, :lc]

    ri = lax.broadcasted_iota(i32, (lc, lc), 0)
    ci = lax.broadcasted_iota(i32, (lc, lc), 1)
    causal = ri >= ci
    state = [(m_s[h:h + 1, 0:1], c_s[h], n_s[h:h + 1, :]) for h in range(M_HEADS)]
    hs, new_state = [], []
    for h in range(M_HEADS):
        sl = slice(h * M_DH, (h + 1) * M_DH)
        q = qk[:, sl]
        k = qk[:, M_WIDTH + h * M_DH:M_WIDTH + (h + 1) * M_DH] * (M_DH ** -0.5)
        v = vall[:, sl]
        i_col = zc[:, h:h + 1]
        b_col = zc[:, M_HEADS + h:M_HEADS + h + 1]
        i_row = zt[h:h + 1, :]
        b_row = zt[M_HEADS + h:M_HEADS + h + 1, :]
        m_prev, c_prev, n_prev = state[h]

        dmat = jnp.where(causal, b_col - b_row + i_row, NEG)
        inter = b_col + m_prev
        mt = jnp.maximum(jnp.max(dmat, axis=-1, keepdims=True), inter)
        w_intra = jnp.exp(dmat - mt)
        w_inter = jnp.exp(inter - mt)
        qb, kb, vb = q.astype(bf16), k.astype(bf16), v.astype(bf16)
        s = lax.dot_general(qb, kb, (((1,), (1,)), ((), ())), preferred_element_type=f32) * w_intra
        num = (w_inter * jnp.dot(qb, c_prev.astype(bf16), preferred_element_type=f32)
               + jnp.dot(s.astype(bf16), vb, preferred_element_type=f32))
        den = w_inter * jnp.sum(q * n_prev, axis=-1, keepdims=True) + jnp.sum(s, axis=-1, keepdims=True)
        hh = num / jnp.maximum(jnp.abs(den), jnp.exp(-mt))
        m_new = mt[lc - 1:lc, :]
        b_last = b_col[lc - 1:lc, :]
        w_s = jnp.exp(b_last - b_col + i_col - m_new)
        decay = jnp.exp(b_last + m_prev - m_new)
        kw = k * w_s
        new_state.append((jnp.broadcast_to(m_new, (1, LANES)),
                          decay * c_prev + lax.dot_general(kw.astype(bf16), vb, (((0,), (0,)), ((), ())),
                                                           preferred_element_type=f32),
                          decay * n_prev + jnp.sum(kw, axis=0, keepdims=True)))
        hs.append(hh * lax.rsqrt(jnp.mean(hh * hh, axis=-1, keepdims=True) + EPS))

    for h, (m_new, c_new, n_new) in enumerate(new_state):
        m_s[h:h + 1, :] = m_new
        c_s[h] = c_new
        n_s[h:h + 1, :] = n_new
    return jnp.concatenate(hs, axis=-1) * mhg_ref[...] * jax.nn.sigmoid(o_in)


def _mlstm_kernel(qk_ref, v_ref, o_ref, gate_ref, c0_ref, n0_ref, m0_ref, cbuf_ref,
                  cw_ref, cb_ref, bg_ref, mhg_ref,
                  h_ref, cout_ref, nout_ref, mout_ref,
                  c_s, n_s, m_s, prev_s):
    c = pl.program_id(1)

    @pl.when(c == 0)
    def _():
        c_s[...] = c0_ref[0]
        n_s[...] = n0_ref[0]
        m_s[...] = m0_ref[0]
        prev_s[...] = cbuf_ref[0]

    h_ref[...] = _mlstm_chunk(qk_ref[...], v_ref[...], o_ref[...], gate_ref[...],
                              cw_ref, cb_ref, bg_ref, mhg_ref, c_s, n_s, m_s, prev_s)

    @pl.when(c == pl.num_programs(1) - 1)
    def _():
        cout_ref[0] = c_s[...]
        nout_ref[0] = n_s[...]
        mout_ref[0] = m_s[...]


def _mlstm(mqk, mv, mo, gates, row0, bsz, t, c0, n0, m0, cbuf, conv_w, conv_b, b_gates, mh_g):
    lc = min(CHUNK, t)
    step = lc
    nc = t // step
    assert t % step == 0 and lc % SUBLANES == 0 and row0 % step == 0
    n0p = jnp.pad(n0.astype(f32), ((0, 0), (0, SUBLANES - M_HEADS), (0, 0)))
    m0p = jnp.pad(jnp.broadcast_to(m0.astype(f32)[:, :, None], (bsz, M_HEADS, LANES)),
                  ((0, 0), (0, SUBLANES - M_HEADS), (0, 0)))
    cbp = jnp.pad(cbuf.astype(f32), ((0, 0), (SUBLANES - (CONV_W - 1), 0), (0, 0)))
    bg = jnp.pad(b_gates.astype(f32), (0, LANES - 2 * M_HEADS)).reshape(1, LANES)
    seq = lambda w: pl.BlockSpec((step, w), lambda b, c: (row0 // step + b * nc + c, 0))
    out_seq = pl.BlockSpec((step, M_WIDTH), lambda b, c: (b * nc + c, 0))
    per_b = lambda shp: pl.BlockSpec((1,) + shp, lambda b, c: (b,) + (0,) * len(shp))
    full = lambda a: pl.BlockSpec(a.shape, lambda b, c: (0,) * a.ndim)
    cb = conv_b.reshape(1, -1)
    mhg = mh_g.reshape(1, -1)
    return pl.pallas_call(
        _mlstm_kernel,
        grid=(bsz, nc),
        in_specs=[seq(2 * M_WIDTH), seq(M_WIDTH), seq(M_WIDTH), seq(LANES),
                  per_b((M_HEADS, M_DH, M_DH)), per_b((SUBLANES, M_DH)), per_b((SUBLANES, LANES)),
                  per_b((SUBLANES, 2 * M_WIDTH)),
                  full(conv_w), full(cb), full(bg), full(mhg)],
        out_specs=[out_seq, per_b((M_HEADS, M_DH, M_DH)), per_b((SUBLANES, M_DH)), per_b((SUBLANES, LANES))],
        out_shape=[jax.ShapeDtypeStruct((bsz * t, M_WIDTH), f32),
                   jax.ShapeDtypeStruct((bsz, M_HEADS, M_DH, M_DH), f32),
                   jax.ShapeDtypeStruct((bsz, SUBLANES, M_DH), f32),
                   jax.ShapeDtypeStruct((bsz, SUBLANES, LANES), f32)],
        scratch_shapes=[pltpu.VMEM((M_HEADS, M_DH, M_DH), f32), pltpu.VMEM((SUBLANES, M_DH), f32),
                        pltpu.VMEM((SUBLANES, LANES), f32), pltpu.VMEM((SUBLANES, 2 * M_WIDTH), f32)],
        compiler_params=pltpu.CompilerParams(dimension_semantics=("parallel", "arbitrary"),
                                             vmem_limit_bytes=VMEM_LIMIT),
        name="mlstm",
    )(mqk, mv, mo, gates, c0.astype(f32), n0p, m0p, cbp, conv_w, cb, bg, mhg)


def _topk_rows(s, k, payload=None):
    n = s.shape[0]
    rows = lax.broadcasted_iota(i32, s.shape, 0).astype(f32)
    vals, ids = [], []
    for _ in range(k):
        m = jnp.max(s, axis=0, keepdims=True)
        pos = jnp.min(jnp.where(s == m, rows, float(n)), axis=0, keepdims=True)
        sel = rows == pos
        vals.append(m)
        ids.append(pos if payload is None else jnp.max(jnp.where(sel, payload, -1.0), axis=0, keepdims=True))
        s = jnp.where(sel, -jnp.inf, s)
    return jnp.concatenate(vals, axis=0), jnp.concatenate(ids, axis=0)


def _mid_kernel(xp_ref, xs_ref, attp_ref, atts_ref, hp_ref, hs_ref, wo_ref, g2_ref, wq_ref, keys_ref,
                x1_ref, xn_ref, eidx_ref, gate_ref, *, nbp):
    cat = jnp.concatenate([_pick(nbp, attp_ref, atts_ref), _pick(nbp, hp_ref, hs_ref)], axis=-1).astype(bf16)
    x1 = _pick(nbp, xp_ref, xs_ref) + jnp.dot(cat, wo_ref[...], preferred_element_type=f32)
    x1_ref[...] = x1
    xn = _rms(x1, g2_ref[...])
    xn_ref[...] = xn
    xb = xn.astype(bf16)
    e_rows, g_rows = [], []
    for h in range(P_HEADS):
        q = jnp.dot(xb, wq_ref[:, h * P_DKEY:(h + 1) * P_DKEY], preferred_element_type=f32).astype(bf16)
        half = []
        for p in range(2):
            st = lax.dot_general(keys_ref[h, p], q[:, p * N_KEYS:(p + 1) * N_KEYS], (((1,), (1,)), ((), ())),
                                 preferred_element_type=f32)
            half.append(_topk_rows(st, P_TOPK))
        (v0, i0), (v1, i1) = half
        width = [P_TOPK // (a + 1) for a in range(P_TOPK)]
        fill = -sum(width) % SUBLANES
        cand = jnp.concatenate([v0[a:a + 1] + v1[:width[a]] for a in range(P_TOPK)]
                               + [jnp.full((fill, v0.shape[1]), -jnp.inf, f32)], axis=0)
        cidx = jnp.concatenate([i0[a:a + 1] * float(N_KEYS) + i1[:width[a]] for a in range(P_TOPK)]
                               + [jnp.zeros((fill, v0.shape[1]), f32)], axis=0)
        top_s, eid = _topk_rows(cand, P_TOPK, payload=cidx)
        ex = jnp.exp(top_s - top_s[0:1])
        e_rows.append(eid)
        g_rows.append(ex / jnp.sum(ex, axis=0, keepdims=True))
    eidx_ref[...] = jnp.concatenate(e_rows, axis=0).T.astype(i32)
    gate_ref[...] = jnp.concatenate(g_rows, axis=0).T


def _mid(xp, xp_row0, xs, attp, atts, hp, hs, n_a, n_b, w_out, g2, wq, keys):
    n = n_a + n_b
    nbp = n_a // ROW_BLOCK
    assert n_a % ROW_BLOCK == 0 and n_b % ROW_BLOCK == 0 and xp_row0 % ROW_BLOCK == 0
    row = lambda w: pl.BlockSpec((ROW_BLOCK, w), lambda i: (i, 0))
    full = lambda a: pl.BlockSpec(a.shape, lambda i: (0,) * a.ndim)
    wo = w_out.astype(bf16)
    wqb = wq.astype(bf16)
    kb = keys.astype(bf16)
    g = g2.reshape(1, D_MODEL)
    return pl.pallas_call(
        functools.partial(_mid_kernel, nbp=nbp),
        grid=(n // ROW_BLOCK,),
        in_specs=[*_pair_specs(nbp, D_MODEL, xp_row0 // ROW_BLOCK), *_pair_specs(nbp, A_WIDTH),
                  *_pair_specs(nbp, M_WIDTH), full(wo), full(g), full(wqb), full(kb)],
        out_specs=[row(D_MODEL), row(D_MODEL), row(P_SLOTS), row(P_SLOTS)],
        out_shape=[jax.ShapeDtypeStruct((n, D_MODEL), f32), jax.ShapeDtypeStruct((n, D_MODEL), f32),
                   jax.ShapeDtypeStruct((n, P_SLOTS), i32), jax.ShapeDtypeStruct((n, P_SLOTS), f32)],
        compiler_params=pltpu.CompilerParams(dimension_semantics=("parallel",), vmem_limit_bytes=VMEM_LIMIT),
        name="outproj_retrieve",
    )(xp, xs, attp, atts, hp, hs, wo, g, wqb, kb)


def _gather_rows(eidx_ref, tab_ref, g_ref, t):
    for r in range(P_SLOTS):
        g_ref[pl.ds(r * ROW_TILES, ROW_TILES), :] = tab_ref[eidx_ref[t, r]]


def _tile_rows(t, rows=SUBLANES):
    return pl.ds(pl.multiple_of(t * rows, rows), rows)


def _pipelined_tokens(nt, eidx_ref, tab_ref, g0_s, g1_s, compute):
    bufs = (g0_s, g1_s)
    _gather_rows(eidx_ref, tab_ref, g0_s, 0)

    def body(j, carry):
        for u in range(TOKEN_UNROLL):
            t = TOKEN_UNROLL * j + u
            _gather_rows(eidx_ref, tab_ref, bufs[(u + 1) % 2], jnp.minimum(t + 1, nt - 1))
            compute(t, bufs[u % 2])
        return carry

    lax.fori_loop(0, nt // TOKEN_UNROLL, body, 0)


def _gelu_tanh(x):
    return 0.5 * x * (1.0 + jnp.tanh(math.sqrt(2.0 / math.pi) * (x + 0.044715 * (x * x * x))))


def _peer_u_kernel(eidx_ref, xn_ref, g_ref, tab_ref, w_ref, xl_s, r_s, g0_s, g1_s):
    nt = xn_ref.shape[0]
    xn = xn_ref[...]
    xh = xn.astype(bf16).astype(f32)
    xl = xn - xh
    for k in range(ROW_TILES):
        xl_s[pl.ds(k, nt, stride=2 * SUBLANES), :] = xh[:, k * LANES:(k + 1) * LANES]
        xl_s[pl.ds(SUBLANES + k, nt, stride=2 * SUBLANES), :] = xl[:, k * LANES:(k + 1) * LANES]
    diag = (lax.broadcasted_iota(i32, (SUBLANES, G_ROWS), 1) % ROW_TILES
            == lax.broadcasted_iota(i32, (SUBLANES, G_ROWS), 0))

    def compute(t, g_s):
        lhs = xl_s[_tile_rows(t, 2 * SUBLANES), :].astype(bf16)
        out = lax.dot_general(lhs, g_s[...], (((1,), (1,)), ((), ())), preferred_element_type=f32)
        part = jnp.where(diag, out[:SUBLANES] + out[SUBLANES:], 0.0)
        for c in range(ROW_TILES):
            r_s[c, _tile_rows(t), :] = part[:, c * LANES:(c + 1) * LANES]

    _pipelined_tokens(nt, eidx_ref, tab_ref, g0_s, g1_s, compute)
    cols = []
    for c in range(ROW_TILES):
        acc = r_s[c, pl.ds(0, nt, stride=SUBLANES), :]
        for k in range(1, SUBLANES):
            acc = acc + r_s[c, pl.ds(k, nt, stride=SUBLANES), :]
        cols.append(acc)
    s = jnp.concatenate(cols, axis=-1)
    fold = (lax.broadcasted_iota(i32, (G_ROWS, P_SLOTS), 0) // ROW_TILES
            == lax.broadcasted_iota(i32, (G_ROWS, P_SLOTS), 1)).astype(bf16)
    sh, sl = _split_bf16(s)
    act = jnp.dot(sh, fold, preferred_element_type=f32) + jnp.dot(sl, fold, preferred_element_type=f32)
    w_ref[...] = g_ref[...] * _gelu_tanh(act)


def _peer_v_kernel(eidx_ref, w_ref, tab_ref, peer_ref, wl_s, o_s, g0_s, g1_s):
    nt = w_ref.shape[0]
    spread = (lax.broadcasted_iota(i32, (P_SLOTS, G_ROWS), 1) // ROW_TILES
              == lax.broadcasted_iota(i32, (P_SLOTS, G_ROWS), 0)).astype(bf16)
    wexp = jnp.dot(w_ref[...].astype(bf16), spread, preferred_element_type=f32)
    lane = lax.broadcasted_iota(i32, (nt, LANES), 1)
    for c in range(ROW_TILES):
        wc = wexp[:, c * LANES:(c + 1) * LANES]
        for k in range(SUBLANES):
            wl_s[c, pl.ds(k, nt, stride=SUBLANES), :] = jnp.where(lane % ROW_TILES == k, wc, 0.0)

    def compute(t, g_s):
        lhs = jnp.concatenate([wl_s[c, _tile_rows(t), :] for c in range(ROW_TILES)], axis=-1).astype(bf16)
        o_s[_tile_rows(t), :] = jnp.dot(lhs, g_s[...], preferred_element_type=f32)

    _pipelined_tokens(nt, eidx_ref, tab_ref, g0_s, g1_s, compute)
    for k in range(ROW_TILES):
        peer_ref[:, k * LANES:(k + 1) * LANES] = o_s[pl.ds(k, nt, stride=SUBLANES), :]


def _final_kernel(pa_ref, pb_ref, xa_ref, xb_ref, gf_ref, yp_ref, ys_ref, *, nba, nbp):
    y = _rms(_pick(nba, xa_ref, xb_ref) + _pick(nba, pa_ref, pb_ref), gf_ref[...])

    @pl.when(pl.program_id(0) < nbp)
    def _():
        yp_ref[...] = y

    @pl.when(pl.program_id(0) >= nbp)
    def _():
        ys_ref[...] = y


def _final(peer_a, peer_b, x1_a, x1_b, gf, n_p):
    n = x1_a.shape[0] + x1_b.shape[0]
    nba, nbp = x1_a.shape[0] // ROW_BLOCK, n_p // ROW_BLOCK
    g = gf.reshape(1, D_MODEL)
    return pl.pallas_call(
        functools.partial(_final_kernel, nba=nba, nbp=nbp),
        grid=(n // ROW_BLOCK,),
        in_specs=[*_pair_specs(nba, D_MODEL), *_pair_specs(nba, D_MODEL), pl.BlockSpec(g.shape, lambda i: (0, 0))],
        out_specs=list(_pair_specs(nbp, D_MODEL)),
        out_shape=[jax.ShapeDtypeStruct((n_p, D_MODEL), f32), jax.ShapeDtypeStruct((n - n_p, D_MODEL), f32)],
        compiler_params=pltpu.CompilerParams(dimension_semantics=("arbitrary",), vmem_limit_bytes=VMEM_LIMIT),
        name="final_norm",
    )(peer_a, peer_b, x1_a, x1_b, g)


def _peer_specs():
    row = lambda w: pl.BlockSpec((ROW_BLOCK, w), lambda i: (i, 0))
    idx = pl.BlockSpec((ROW_BLOCK, P_SLOTS), lambda i: (i, 0), memory_space=pltpu.SMEM)
    tab = pl.BlockSpec(memory_space=pltpu.VMEM)
    gscr = pltpu.VMEM((G_ROWS, LANES), bf16)
    params = pltpu.CompilerParams(dimension_semantics=("arbitrary",), vmem_limit_bytes=VMEM_LIMIT)
    return row, idx, tab, gscr, params


def _expert_table(tab_bf16):
    return tab_bf16.reshape(tab_bf16.shape[0], ROW_TILES, LANES)


def _peer_u(eidx, xn, g, utab, n):
    row, idx, tab, gscr, params = _peer_specs()
    return pl.pallas_call(
        _peer_u_kernel,
        grid=(n // ROW_BLOCK,),
        in_specs=[idx, row(D_MODEL), row(P_SLOTS), tab],
        out_specs=row(P_SLOTS),
        out_shape=jax.ShapeDtypeStruct((n, P_SLOTS), f32),
        scratch_shapes=[pltpu.VMEM((ROW_BLOCK * 2 * SUBLANES, LANES), f32),
                        pltpu.VMEM((ROW_TILES, ROW_BLOCK * SUBLANES, LANES), f32), gscr, gscr],
        compiler_params=params,
        name="peer_u",
    )(eidx, xn, g, utab)


def _peer_v(eidx, w, vtab):
    n = w.shape[0]
    row, idx, tab, gscr, params = _peer_specs()
    return pl.pallas_call(
        _peer_v_kernel,
        grid=(n // ROW_BLOCK,),
        in_specs=[idx, row(P_SLOTS), tab],
        out_specs=row(D_MODEL),
        out_shape=jax.ShapeDtypeStruct((n, D_MODEL), f32),
        scratch_shapes=[pltpu.VMEM((ROW_TILES, ROW_BLOCK * SUBLANES, LANES), f32),
                        pltpu.VMEM((ROW_BLOCK * SUBLANES, LANES), f32), gscr, gscr],
        compiler_params=params,
        name="peer_v",
    )(eidx, w, vtab)


def _sc_table(tab_bf16):
    e = tab_bf16.shape[0]
    t = tab_bf16.reshape(e, SC_CHUNKS, 2, SC_LANES).transpose(0, 1, 3, 2)
    return lax.bitcast_convert_type(t, i32).reshape(e, SC_WORDS)


def _sc_unpack(words):
    return plsc.unpack(plsc.bitcast(words, bf16), format=plsc.PackFormat.INTERLEAVED)


def _sc_gelu_tanh(x):
    z = math.sqrt(2.0 / math.pi) * (x + 0.044715 * (x * x * x))
    return 0.5 * x * (2.0 - 2.0 / (jnp.exp(2.0 * z) + 1.0))


def _sc_peer(utab32, vtab32, eidx, xn, gate):
    n_sc = eidx.shape[0]
    per = n_sc // SC_WORKERS
    units = 4 * per
    assert n_sc % (SC_WORKERS * SUBLANES) == 0

    def body(u_hbm, v_hbm, eidx_hbm, x_hbm, g_hbm, out_hbm,
             idx_v, x_v, g_v, rows_v, acc_v, w_v, w16_v, out_v, row_sems, tok_sems, out_sem):
        base = (lax.axis_index("s") * 2 + lax.axis_index("c")) * per
        lanes = lax.broadcasted_iota(i32, (SC_LANES,), 0)

        def token_copies(tok):
            tslot = tok % 2
            return (pltpu.make_async_copy(eidx_hbm.at[base + tok], idx_v.at[tslot], tok_sems.at[0]),
                    pltpu.make_async_copy(x_hbm.at[base + tok], x_v.at[tslot], tok_sems.at[1]),
                    pltpu.make_async_copy(g_hbm.at[base + tok], g_v.at[tslot], tok_sems.at[2]))

        def store_out(tok):
            return pltpu.make_async_copy(out_v, out_hbm.at[base + tok], out_sem)

        def gather(tab_hbm, g):
            tok, k = g // 4, g % 4
            return pltpu.make_async_copy(tab_hbm.at[idx_v.at[tok % 2, pl.ds((k % 2) * SC_UNIT_ROWS, SC_UNIT_ROWS)]],
                                         rows_v.at[k % 2], row_sems.at[k % 2])

        def start(g):
            @pl.when(g % 4 < 2)
            def _():
                gather(u_hbm, g).start()

            @pl.when(g % 4 >= 2)
            def _():
                gather(v_hbm, g).start()

        def compute_u(tslot, half, rows):
            @pl.loop(0, SC_UNIT_ROWS // SC_ROW_GROUP)
            def _(rg):
                slot0 = half * SC_UNIT_ROWS + rg * SC_ROW_GROUP

                @pl.loop(0, SC_CHUNKS // SC_CHUNK_GROUP)
                def _(cg):
                    keep = jnp.where(cg > 0, 1.0, 0.0).astype(f32)
                    accs = [acc_v[pl.ds((slot0 + r) * SC_LANES, SC_LANES)] * keep for r in range(SC_ROW_GROUP)]
                    for c in range(SC_CHUNK_GROUP):
                        ch = cg * SC_CHUNK_GROUP + c
                        xa = x_v[tslot, pl.ds(2 * ch * SC_LANES, SC_LANES)]
                        xb = x_v[tslot, pl.ds((2 * ch + 1) * SC_LANES, SC_LANES)]
                        for r in range(SC_ROW_GROUP):
                            a, b = _sc_unpack(rows[rg * SC_ROW_GROUP + r, pl.ds(ch * SC_LANES, SC_LANES)])
                            accs[r] = accs[r] + a * xa + b * xb
                    for r in range(SC_ROW_GROUP):
                        acc_v[pl.ds((slot0 + r) * SC_LANES, SC_LANES)] = accs[r]

        def gate_weights(tslot):
            @pl.loop(0, P_SLOTS // SC_LANES)
            def _(sg):
                first = sg * SC_LANES * SC_LANES
                act = jnp.zeros((SC_LANES,), f32)
                for lane in range(SC_LANES):
                    act = act + plsc.load_gather(acc_v, [first + lanes * SC_LANES + lane])
                w_v[pl.ds(sg * SC_LANES, SC_LANES)] = g_v[tslot, pl.ds(sg * SC_LANES, SC_LANES)] * _sc_gelu_tanh(act)

            @pl.loop(0, P_SLOTS // SC_LANES)
            def _(sg):
                for r in range(SC_LANES):
                    w16_v[pl.ds((sg * SC_LANES + r) * SC_LANES, SC_LANES)] = plsc.load_gather(
                        w_v, [jnp.zeros((SC_LANES,), i32) + (sg * SC_LANES + r)])

        def compute_v(half, rows):
            @pl.loop(0, SC_CHUNKS // SC_CHUNK_GROUP)
            def _(cg):
                first = cg * SC_CHUNK_GROUP

                @pl.loop(0, SC_UNIT_ROWS // SC_ROW_GROUP)
                def _(rg):
                    keep = jnp.where(jnp.logical_or(half == 1, rg > 0), 1.0, 0.0).astype(f32)
                    accs = [out_v[pl.ds((2 * first + j) * SC_LANES, SC_LANES)] * keep for j in range(2 * SC_CHUNK_GROUP)]
                    for r in range(SC_ROW_GROUP):
                        row = rg * SC_ROW_GROUP + r
                        wv = w16_v[pl.ds((half * SC_UNIT_ROWS + row) * SC_LANES, SC_LANES)]
                        for c in range(SC_CHUNK_GROUP):
                            a, b = _sc_unpack(rows[row, pl.ds((first + c) * SC_LANES, SC_LANES)])
                            accs[2 * c] = accs[2 * c] + a * wv
                            accs[2 * c + 1] = accs[2 * c + 1] + b * wv
                    for j in range(2 * SC_CHUNK_GROUP):
                        out_v[pl.ds((2 * first + j) * SC_LANES, SC_LANES)] = accs[j]

        for cp in token_copies(0):
            cp.start()
        for cp in token_copies(0):
            cp.wait()
        start(0)

        @pl.loop(0, units)
        def _(g):
            tok, k = g // 4, g % 4

            @pl.when(jnp.logical_and(k == 0, tok + 1 < per))
            def _():
                for cp in token_copies(tok + 1):
                    cp.start()

            @pl.when(jnp.logical_and(k == 3, tok + 1 < per))
            def _():
                for cp in token_copies(tok + 1):
                    cp.wait()

            @pl.when(g + 1 < units)
            def _():
                start(g + 1)

            gather(u_hbm, g).wait()
            rows = rows_v.at[k % 2]

            @pl.when(k < 2)
            def _():
                compute_u(tok % 2, k, rows)

            @pl.when(k == 1)
            def _():
                gate_weights(tok % 2)

            @pl.when(jnp.logical_and(k == 2, tok > 0))
            def _():
                store_out(tok - 1).wait()

            @pl.when(k >= 2)
            def _():
                compute_v(k - 2, rows)

            @pl.when(k == 3)
            def _():
                store_out(tok).start()

        store_out(per - 1).wait()

    return pl.kernel(
        body, mesh=plsc.VectorSubcoreMesh(core_axis_name="c", subcore_axis_name="s"),
        out_type=jax.ShapeDtypeStruct((n_sc, D_MODEL), f32),
        scratch_types=[pltpu.VMEM((2, P_SLOTS), i32), pltpu.VMEM((2, D_MODEL), f32), pltpu.VMEM((2, P_SLOTS), f32),
                       pltpu.VMEM((2, SC_UNIT_ROWS, SC_WORDS), i32), pltpu.VMEM((P_SLOTS * SC_LANES,), f32),
                       pltpu.VMEM((P_SLOTS,), f32), pltpu.VMEM((P_SLOTS * SC_LANES,), f32), pltpu.VMEM((D_MODEL,), f32),
                       pltpu.SemaphoreType.DMA((2,)), pltpu.SemaphoreType.DMA((3,)), pltpu.SemaphoreType.DMA],
        compiler_params=pltpu.CompilerParams(needs_layout_passes=False),
        name="sc_peer",
    )(utab32, vtab32, eidx, xn, gate)


def kernel(x_prompt, x_sample, cache_k, cache_v, state_C, state_n, state_m, state_conv, norm1_g, w_in, b_gates, rel_bias, conv_w, conv_b, mh_norm_g, w_out, norm2_g, peer_wq, peer_keys, peer_u, peer_v, final_g):
    bp, sp, d = x_prompt.shape
    bs, ts, _ = x_sample.shape
    n_p, n_s = bp * sp, bs * ts
    n = n_p + n_s
    assert n_p % ROW_BLOCK == 0 and n_s % ROW_BLOCK == 0 and d == D_MODEL
    depth = w_in.shape[0]
    assert depth == 1, "the final norm is fused into the last layer's PEER pass"
    l = 0
    xp, xs = x_prompt.reshape(n_p, d), x_sample.reshape(n_s, d)

    aq, ak, av, mqk, mv, mo, gates = _inproj(xp, xs, norm1_g[l], w_in[l])
    zeros = lambda *shp: jnp.zeros(shp, f32)
    mparams = (conv_w[l], conv_b[l], b_gates[l], mh_norm_g[l])
    retrieval = (w_out[l], norm2_g[l], peer_wq[l], peer_keys[l])

    def prompt_mixers(b0, nb):
        att = _attn_prompt(aq, ak, av, rel_bias[l], b0, nb, sp)
        return (att,) + tuple(_mlstm(mqk, mv, mo, gates, b0 * sp, nb, sp, zeros(nb, M_HEADS, M_DH, M_DH),
                                     zeros(nb, M_HEADS, M_DH), zeros(nb, M_HEADS), zeros(nb, CONV_W - 1, 2 * M_WIDTH),
                                     *mparams))

    assert 0 < SC_PROMPT_SEQS < bp
    n_sc = SC_PROMPT_SEQS * sp
    att_a, h_a, c_a, nn_a, mm_a = prompt_mixers(0, SC_PROMPT_SEQS)
    x1_a, xn_a, eidx_a, gate_a = _mid(xp, 0, xs, att_a, att_a, h_a, h_a, n_sc, 0, *retrieval)
    u_bf16, v_bf16 = peer_u[l].astype(bf16), peer_v[l].astype(bf16)
    peer_a = _sc_peer(_sc_table(u_bf16), _sc_table(v_bf16), eidx_a, xn_a, gate_a)

    att_b, h_b, c_b, nn_b, mm_b = prompt_mixers(SC_PROMPT_SEQS, bp - SC_PROMPT_SEQS)
    lcache = cache_k.shape[2]
    att_s = _attn_sample(aq, ak, av, cache_k[l].reshape(bs, lcache, A_WIDTH),
                         cache_v[l].reshape(bs, lcache, A_WIDTH), rel_bias[l], n_p, bs, ts)
    h_s, c_s, nn_s, mm_s = _mlstm(mqk, mv, mo, gates, n_p, bs, ts, state_C[l], state_n[l], state_m[l],
                                  state_conv[l], *mparams)
    x1_b, xn_b, eidx_b, gate_b = _mid(xp, n_sc, xs, att_b, att_s, h_b, h_s, n_p - n_sc, n_s, *retrieval)
    w_b = _peer_u(eidx_b, xn_b, gate_b, _expert_table(u_bf16), n - n_sc)
    peer_b = _peer_v(eidx_b, w_b, _expert_table(v_bf16))
    y_p, y_s = _final(peer_a, peer_b, x1_a, x1_b, final_g, n_p)
    c_p, nn_p, mm_p = (jnp.concatenate(ab, axis=0) for ab in ((c_a, c_b), (nn_a, nn_b), (mm_a, mm_b)))

    def tail(a, row0, bsz, t, keep):
        return jnp.stack([a[row0 + (b + 1) * t - keep:row0 + (b + 1) * t] for b in range(bsz)])

    keep = min(WINDOW, sp)
    heads = lambda a: a.reshape(a.shape[0], a.shape[1], A_HEADS, A_DH)
    ctail = CONV_W - 1
    conv_tail = lambda buf, a, row0, bsz, t: jnp.concatenate([buf.astype(a.dtype), tail(a, row0, bsz, t, min(ctail, t))],
                                                             axis=1)[:, -ctail:]
    st = lambda a: a[None]
    return (y_p.reshape(bp, sp, d), y_s.reshape(bs, ts, d),
            st(heads(tail(ak, 0, bp, sp, keep))), st(heads(tail(av, 0, bp, sp, keep))),
            st(c_p), st(nn_p[:, :M_HEADS]), st(mm_p[:, :M_HEADS, 0]),
            st(conv_tail(zeros(bp, ctail, 2 * M_WIDTH), mqk, 0, bp, sp)),
            st(heads(ak[n_p:].reshape(bs, ts, A_WIDTH))), st(heads(av[n_p:].reshape(bs, ts, A_WIDTH))),
            st(c_s), st(nn_s[:, :M_HEADS]), st(mm_s[:, :M_HEADS, 0]),
            st(conv_tail(state_conv[l], mqk, n_p, bs, ts)))
```

```python
import functools
import math

import jax
import jax.numpy as jnp
from jax import lax
from jax.experimental import pallas as pl
from jax.experimental.pallas import tpu as pltpu
from jax.experimental.pallas import tpu_sc as plsc

f32 = jnp.float32
bf16 = jnp.bfloat16
i32 = jnp.int32

D_MODEL = 1024
CHUNK = 64
A_HEADS = 8
A_DH = 64
A_WIDTH = A_HEADS * A_DH
BAND_CHUNKS = 8
WINDOW = BAND_CHUNKS * CHUNK
MAX_REL = 128
ATT_SCALE = A_DH ** -0.5
M_HEADS = 4
M_DH = 128
M_WIDTH = M_HEADS * M_DH
CONV_W = 4
P_HEADS = 8
P_DKEY = 256
N_KEYS = 128
P_TOPK = 16
P_SLOTS = P_HEADS * P_TOPK
EPS = 1e-6
NEG = -1e30

LANES = 128
SUBLANES = 8
ROW_BLOCK = 256
ATT_TILE = 512
ATT_SUB = 128
ATT_KEYS = ATT_SUB + WINDOW
ROW_TILES = D_MODEL // LANES
G_ROWS = P_SLOTS * ROW_TILES
TOKEN_UNROLL = 8
VMEM_LIMIT = 56 * 1024 * 1024

SC_WORKERS = 32
SC_LANES = 16
SC_UNIT_ROWS = P_SLOTS // 2
SC_ROW_GROUP = 16
SC_WORDS = D_MODEL // 2
SC_CHUNKS = SC_WORDS // SC_LANES
SC_CHUNK_GROUP = 8
SC_PROMPT_SEQS = 1


def _rms(x, g):
    return x * lax.rsqrt(jnp.mean(x * x, axis=-1, keepdims=True) + EPS) * g


def _split_bf16(x):
    hi = x.astype(bf16)
    lo = (x - hi.astype(f32)).astype(bf16)
    return hi, lo


def _pair_specs(nbp, width, first=0):
    return (pl.BlockSpec((ROW_BLOCK, width), lambda i: (first + jnp.minimum(i, nbp - 1), 0)),
            pl.BlockSpec((ROW_BLOCK, width), lambda i: (jnp.maximum(i - nbp, 0), 0)))


def _pick(nbp, p_ref, s_ref):
    return jnp.where(pl.program_id(0) < nbp, p_ref[...], s_ref[...])


def _inproj_kernel(xp_ref, xs_ref, g_ref, w_ref, wgh_ref, wgl_ref,
                   aq_ref, ak_ref, av_ref, mqk_ref, mv_ref, mo_ref, gate_ref, *, nbp):
    xn = _rms(_pick(nbp, xp_ref, xs_ref), g_ref[...])
    xh, xl = _split_bf16(xn)

    def proj(lo, hi):
        return jnp.dot(xh, w_ref[:, lo:hi], preferred_element_type=f32)

    aq_ref[...] = proj(0, 512)
    ak_ref[...] = proj(512, 1024)
    av_ref[...] = proj(1024, 1536)
    mqk_ref[...] = proj(1536, 2560)
    mv_ref[...] = proj(2560, 3072)
    mo_ref[...] = proj(3072, 3584)
    gate_ref[...] = (jnp.dot(xh, wgh_ref[...], preferred_element_type=f32)
                     + jnp.dot(xl, wgh_ref[...], preferred_element_type=f32)
                     + jnp.dot(xh, wgl_ref[...], preferred_element_type=f32))


def _inproj(xp, xs, g1, w_in):
    n = xp.shape[0] + xs.shape[0]
    nbp = xp.shape[0] // ROW_BLOCK
    main = 3 * A_WIDTH + 4 * M_WIDTH
    w_main = w_in[:, :main].astype(bf16)
    wg = jnp.pad(w_in[:, main:], ((0, 0), (0, LANES - 2 * M_HEADS)))
    wgh, wgl = _split_bf16(wg)
    widths = (512, 512, 512, 1024, 512, 512, LANES)
    row = lambda w: pl.BlockSpec((ROW_BLOCK, w), lambda i: (i, 0))
    full = lambda a: pl.BlockSpec(a.shape, lambda i: (0,) * a.ndim)
    g = g1.reshape(1, D_MODEL)
    return pl.pallas_call(
        functools.partial(_inproj_kernel, nbp=nbp),
        grid=(n // ROW_BLOCK,),
        in_specs=[*_pair_specs(nbp, D_MODEL), full(g), full(w_main), full(wgh), full(wgl)],
        out_specs=[row(w) for w in widths],
        out_shape=[jax.ShapeDtypeStruct((n, w), f32) for w in widths],
        compiler_params=pltpu.CompilerParams(dimension_semantics=("parallel",), vmem_limit_bytes=VMEM_LIMIT),
        name="inproj",
    )(xp, xs, g, w_main, wgh, wgl)


def _attn_heads(q, k, v, bias_ref, key_ok):
    outs = []
    for h in range(A_HEADS):
        sl = slice(h * A_DH, (h + 1) * A_DH)
        s = lax.dot_general(q[:, sl], k[:, sl], (((1,), (1,)), ((), ())), preferred_element_type=f32)
        s = s * ATT_SCALE + bias_ref[h]
        if key_ok is not None:
            s = jnp.where(key_ok, s, NEG)
        m = jnp.max(s, axis=-1, keepdims=True)
        p = jnp.exp(s - m)
        l = jnp.sum(p, axis=-1, keepdims=True)
        o = jnp.dot(p.astype(bf16), v[:, sl], preferred_element_type=f32)
        outs.append(o / l)
    return jnp.concatenate(outs, axis=-1)


def _attn_prompt_kernel(q_ref, k0_ref, k1_ref, v0_ref, v1_ref, bias_ref, o_ref):
    t = pl.program_id(1)
    q = q_ref[...].astype(bf16)
    k = jnp.concatenate([k0_ref[...], k1_ref[...]], axis=0).astype(bf16)
    v = jnp.concatenate([v0_ref[...], v1_ref[...]], axis=0).astype(bf16)
    col = lax.broadcasted_iota(i32, (1, ATT_KEYS), 1)
    for s in range(ATT_TILE // ATT_SUB):
        lo = s * ATT_SUB
        key_ok = (t * ATT_TILE + lo + col) >= WINDOW
        o_ref[lo:lo + ATT_SUB, :] = _attn_heads(q[lo:lo + ATT_SUB], k[lo:lo + ATT_KEYS], v[lo:lo + ATT_KEYS],
                                                 bias_ref, key_ok)


def _attn_sample_kernel(q_ref, k_ref, v_ref, bias_ref, o_ref):
    o_ref[...] = _attn_heads(q_ref[...].astype(bf16), k_ref[0].astype(bf16), v_ref[0].astype(bf16), bias_ref, None)


def _rel_bias_table(rel_bias, rows, cols, offset, valid):
    span = rows + cols - 1
    rel = offset + rows - 1 - jnp.arange(span)
    diag = rel_bias[:, jnp.clip(rel, -MAX_REL, MAX_REL) + MAX_REL].astype(f32)
    diag = jnp.pad(diag, ((0, 0), (0, 1)))
    flat = jnp.tile(diag, (1, rows))[:, rows - 1:rows - 1 + rows * span]
    return jnp.where(valid[None], flat.reshape(-1, rows, span)[:, :, :cols], NEG)


def _attn_prompt(q, k, v, rel_bias, b0, bsz, s):
    assert s % ATT_TILE == 0 and WINDOW == ATT_TILE
    nt = s // ATT_TILE
    i = jnp.arange(ATT_SUB)[:, None]
    j = jnp.arange(ATT_KEYS)[None, :]
    off = j - (i // CHUNK) * CHUNK
    bias = _rel_bias_table(rel_bias, ATT_SUB, ATT_KEYS, WINDOW, (off >= 0) & (off < WINDOW + CHUNK))
    cur = pl.BlockSpec((ATT_TILE, A_WIDTH), lambda b, t: ((b0 + b) * nt + t, 0))
    prev = pl.BlockSpec((ATT_TILE, A_WIDTH), lambda b, t: ((b0 + b) * nt + jnp.maximum(t - 1, 0), 0))
    return pl.pallas_call(
        _attn_prompt_kernel,
        grid=(bsz, nt),
        in_specs=[cur, prev, cur, prev, cur, pl.BlockSpec(bias.shape, lambda b, t: (0, 0, 0))],
        out_specs=pl.BlockSpec((ATT_TILE, A_WIDTH), lambda b, t: (b * nt + t, 0)),
        out_shape=jax.ShapeDtypeStruct((bsz * s, A_WIDTH), f32),
        compiler_params=pltpu.CompilerParams(dimension_semantics=("parallel", "parallel"),
                                             vmem_limit_bytes=VMEM_LIMIT),
        name="attn_prompt",
    )(q, k, k, v, v, bias)


def _attn_sample(q, k, v, ck, cv, rel_bias, row0, bsz, t):
    l = ck.shape[1]
    assert row0 % t == 0
    keys = -(-(l + t) // LANES) * LANES
    padk = ((0, 0), (0, keys - l - t), (0, 0))
    kk = jnp.pad(jnp.concatenate([ck, k[row0:].reshape(bsz, t, A_WIDTH)], axis=1), padk)
    vv = jnp.pad(jnp.concatenate([cv, v[row0:].reshape(bsz, t, A_WIDTH)], axis=1), padk)
    j = jnp.arange(keys)[None, :]
    bias = _rel_bias_table(rel_bias, t, keys, l, jnp.broadcast_to(j < l + t, (t, keys)))
    return pl.pallas_call(
        _attn_sample_kernel,
        grid=(bsz,),
        in_specs=[pl.BlockSpec((t, A_WIDTH), lambda b: (row0 // t + b, 0)),
                  pl.BlockSpec((1, keys, A_WIDTH), lambda b: (b, 0, 0)),
                  pl.BlockSpec((1, keys, A_WIDTH), lambda b: (b, 0, 0)),
                  pl.BlockSpec(bias.shape, lambda b: (0, 0, 0))],
        out_specs=pl.BlockSpec((t, A_WIDTH), lambda b: (b, 0)),
        out_shape=jax.ShapeDtypeStruct((bsz * t, A_WIDTH), f32),
        compiler_params=pltpu.CompilerParams(dimension_semantics=("parallel",), vmem_limit_bytes=VMEM_LIMIT),
        name="attn_sample",
    )(q, kk, vv, bias)


def _mlstm_chunk(a, vall, o_in, gate_in, cw_ref, cb_ref, bg_ref, mhg_ref, c_s, n_s, m_s, prev_s):
    lc = a.shape[0]
    ext = jnp.concatenate([prev_s[...], a], axis=0)
    conv = cb_ref[...]
    for j in range(CONV_W):
        lo = SUBLANES - (CONV_W - 1) + j
        conv = conv + cw_ref[j:j + 1, :] * ext[lo:lo + lc]
    prev_s[...] = a[lc - SUBLANES:lc]
    qk = conv * jax.nn.sigmoid(conv)

    z = gate_in + bg_ref[...]
    lane = lax.broadcasted_iota(i32, (lc, LANES), 1)
    row = lax.broadcasted_iota(i32, (lc, LANES), 0)
    logf = jnp.minimum(z, 0.0) - jnp.log1p(jnp.exp(-jnp.abs(z)))
    cum = jnp.where((lane >= M_HEADS) & (lane < 2 * M_HEADS), logf, 0.0)
    shift = 1
    while shift < lc:
        cum = cum + jnp.where(row >= shift, pltpu.roll(cum, shift, axis=0), 0.0)
        shift *= 2
    zc = jnp.where(lane < M_HEADS, z, cum)
    zt = jnp.concatenate([zc, jnp.zeros((LANES - lc, LANES), f32)], axis=0).T[:, :lc]

    ri = lax.broadcasted_iota(i32, (lc, lc), 0)
    ci = lax.broadcasted_iota(i32, (lc, lc), 1)
    causal = ri >= ci
    state = [(m_s[h:h + 1, 0:1], c_s[h], n_s[h:h + 1, :]) for h in range(M_HEADS)]
    hs, new_state = [], []
    for h in range(M_HEADS):
        sl = slice(h * M_DH, (h + 1) * M_DH)
        q = qk[:, sl]
        k = qk[:, M_WIDTH + h * M_DH:M_WIDTH + (h + 1) * M_DH] * (M_DH ** -0.5)
        v = vall[:, sl]
        i_col = zc[:, h:h + 1]
        b_col = zc[:, M_HEADS + h:M_HEADS + h + 1]
        i_row = zt[h:h + 1, :]
        b_row = zt[M_HEADS + h:M_HEADS + h + 1, :]
        m_prev, c_prev, n_prev = state[h]

        dmat = jnp.where(causal, b_col - b_row + i_row, NEG)
        inter = b_col + m_prev
        mt = jnp.maximum(jnp.max(dmat, axis=-1, keepdims=True), inter)
        w_intra = jnp.exp(dmat - mt)
        w_inter = jnp.exp(inter - mt)
        qb, kb, vb = q.astype(bf16), k.astype(bf16), v.astype(bf16)
        s = lax.dot_general(qb, kb, (((1,), (1,)), ((), ())), preferred_element_type=f32) * w_intra
        num = (w_inter * jnp.dot(qb, c_prev.astype(bf16), preferred_element_type=f32)
               + jnp.dot(s.astype(bf16), vb, preferred_element_type=f32))
        den = w_inter * jnp.sum(q * n_prev, axis=-1, keepdims=True) + jnp.sum(s, axis=-1, keepdims=True)
        hh = num / jnp.maximum(jnp.abs(den), jnp.exp(-mt))
        m_new = mt[lc - 1:lc, :]
        b_last = b_col[lc - 1:lc, :]
        w_s = jnp.exp(b_last - b_col + i_col - m_new)
        decay = jnp.exp(b_last + m_prev - m_new)
        kw = k * w_s
        new_state.append((jnp.broadcast_to(m_new, (1, LANES)),
                          decay * c_prev + lax.dot_general(kw.astype(bf16), vb, (((0,), (0,)), ((), ())),
                                                           preferred_element_type=f32),
                          decay * n_prev + jnp.sum(kw, axis=0, keepdims=True)))
        hs.append(hh * lax.rsqrt(jnp.mean(hh * hh, axis=-1, keepdims=True) + EPS))

    for h, (m_new, c_new, n_new) in enumerate(new_state):
        m_s[h:h + 1, :] = m_new
        c_s[h] = c_new
        n_s[h:h + 1, :] = n_new
    return jnp.concatenate(hs, axis=-1) * mhg_ref[...] * jax.nn.sigmoid(o_in)


def _mlstm_kernel(qk_ref, v_ref, o_ref, gate_ref, c0_ref, n0_ref, m0_ref, cbuf_ref,
                  cw_ref, cb_ref, bg_ref, mhg_ref,
                  h_ref, cout_ref, nout_ref, mout_ref,
                  c_s, n_s, m_s, prev_s):
    c = pl.program_id(1)

    @pl.when(c == 0)
    def _():
        c_s[...] = c0_ref[0]
        n_s[...] = n0_ref[0]
        m_s[...] = m0_ref[0]
        prev_s[...] = cbuf_ref[0]

    h_ref[...] = _mlstm_chunk(qk_ref[...], v_ref[...], o_ref[...], gate_ref[...],
                              cw_ref, cb_ref, bg_ref, mhg_ref, c_s, n_s, m_s, prev_s)

    @pl.when(c == pl.num_programs(1) - 1)
    def _():
        cout_ref[0] = c_s[...]
        nout_ref[0] = n_s[...]
        mout_ref[0] = m_s[...]


def _mlstm(mqk, mv, mo, gates, row0, bsz, t, c0, n0, m0, cbuf, conv_w, conv_b, b_gates, mh_g):
    lc = min(CHUNK, t)
    step = lc
    nc = t // step
    assert t % step == 0 and lc % SUBLANES == 0 and row0 % step == 0
    n0p = jnp.pad(n0.astype(f32), ((0, 0), (0, SUBLANES - M_HEADS), (0, 0)))
    m0p = jnp.pad(jnp.broadcast_to(m0.astype(f32)[:, :, None], (bsz, M_HEADS, LANES)),
                  ((0, 0), (0, SUBLANES - M_HEADS), (0, 0)))
    cbp = jnp.pad(cbuf.astype(f32), ((0, 0), (SUBLANES - (CONV_W - 1), 0), (0, 0)))
    bg = jnp.pad(b_gates.astype(f32), (0, LANES - 2 * M_HEADS)).reshape(1, LANES)
    seq = lambda w: pl.BlockSpec((step, w), lambda b, c: (row0 // step + b * nc + c, 0))
    out_seq = pl.BlockSpec((step, M_WIDTH), lambda b, c: (b * nc + c, 0))
    per_b = lambda shp: pl.BlockSpec((1,) + shp, lambda b, c: (b,) + (0,) * len(shp))
    full = lambda a: pl.BlockSpec(a.shape, lambda b, c: (0,) * a.ndim)
    cb = conv_b.reshape(1, -1)
    mhg = mh_g.reshape(1, -1)
    return pl.pallas_call(
        _mlstm_kernel,
        grid=(bsz, nc),
        in_specs=[seq(2 * M_WIDTH), seq(M_WIDTH), seq(M_WIDTH), seq(LANES),
                  per_b((M_HEADS, M_DH, M_DH)), per_b((SUBLANES, M_DH)), per_b((SUBLANES, LANES)),
                  per_b((SUBLANES, 2 * M_WIDTH)),
                  full(conv_w), full(cb), full(bg), full(mhg)],
        out_specs=[out_seq, per_b((M_HEADS, M_DH, M_DH)), per_b((SUBLANES, M_DH)), per_b((SUBLANES, LANES))],
        out_shape=[jax.ShapeDtypeStruct((bsz * t, M_WIDTH), f32),
                   jax.ShapeDtypeStruct((bsz, M_HEADS, M_DH, M_DH), f32),
                   jax.ShapeDtypeStruct((bsz, SUBLANES, M_DH), f32),
                   jax.ShapeDtypeStruct((bsz, SUBLANES, LANES), f32)],
        scratch_shapes=[pltpu.VMEM((M_HEADS, M_DH, M_DH), f32), pltpu.VMEM((SUBLANES, M_DH), f32),
                        pltpu.VMEM((SUBLANES, LANES), f32), pltpu.VMEM((SUBLANES, 2 * M_WIDTH), f32)],
        compiler_params=pltpu.CompilerParams(dimension_semantics=("parallel", "arbitrary"),
                                             vmem_limit_bytes=VMEM_LIMIT),
        name="mlstm",
    )(mqk, mv, mo, gates, c0.astype(f32), n0p, m0p, cbp, conv_w, cb, bg, mhg)


def _topk_rows(s, k, payload=None):
    n = s.shape[0]
    rows = lax.broadcasted_iota(i32, s.shape, 0).astype(f32)
    vals, ids = [], []
    for _ in range(k):
        m = jnp.max(s, axis=0, keepdims=True)
        pos = jnp.min(jnp.where(s == m, rows, float(n)), axis=0, keepdims=True)
        sel = rows == pos
        vals.append(m)
        ids.append(pos if payload is None else jnp.max(jnp.where(sel, payload, -1.0), axis=0, keepdims=True))
        s = jnp.where(sel, -jnp.inf, s)
    return jnp.concatenate(vals, axis=0), jnp.concatenate(ids, axis=0)


def _mid_kernel(xp_ref, xs_ref, attp_ref, atts_ref, hp_ref, hs_ref, wo_ref, g2_ref, wq_ref, keys_ref,
                x1_ref, xn_ref, eidx_ref, gate_ref, *, nbp):
    cat = jnp.concatenate([_pick(nbp, attp_ref, atts_ref), _pick(nbp, hp_ref, hs_ref)], axis=-1).astype(bf16)
    x1 = _pick(nbp, xp_ref, xs_ref) + jnp.dot(cat, wo_ref[...], preferred_element_type=f32)
    x1_ref[...] = x1
    xn = _rms(x1, g2_ref[...])
    xn_ref[...] = xn
    xb = xn.astype(bf16)
    e_rows, g_rows = [], []
    for h in range(P_HEADS):
        q = jnp.dot(xb, wq_ref[:, h * P_DKEY:(h + 1) * P_DKEY], preferred_element_type=f32).astype(bf16)
        half = []
        for p in range(2):
            st = lax.dot_general(keys_ref[h, p], q[:, p * N_KEYS:(p + 1) * N_KEYS], (((1,), (1,)), ((), ())),
                                 preferred_element_type=f32)
            half.append(_topk_rows(st, P_TOPK))
        (v0, i0), (v1, i1) = half
        width = [P_TOPK // (a + 1) for a in range(P_TOPK)]
        fill = -sum(width) % SUBLANES
        cand = jnp.concatenate([v0[a:a + 1] + v1[:width[a]] for a in range(P_TOPK)]
                               + [jnp.full((fill, v0.shape[1]), -jnp.inf, f32)], axis=0)
        cidx = jnp.concatenate([i0[a:a + 1] * float(N_KEYS) + i1[:width[a]] for a in range(P_TOPK)]
                               + [jnp.zeros((fill, v0.shape[1]), f32)], axis=0)
        top_s, eid = _topk_rows(cand, P_TOPK, payload=cidx)
        ex = jnp.exp(top_s - top_s[0:1])
        e_rows.append(eid)
        g_rows.append(ex / jnp.sum(ex, axis=0, keepdims=True))
    eidx_ref[...] = jnp.concatenate(e_rows, axis=0).T.astype(i32)
    gate_ref[...] = jnp.concatenate(g_rows, axis=0).T


def _mid(xp, xp_row0, xs, attp, atts, hp, hs, n_a, n_b, w_out, g2, wq, keys):
    n = n_a + n_b
    nbp = n_a // ROW_BLOCK
    assert n_a % ROW_BLOCK == 0 and n_b % ROW_BLOCK == 0 and xp_row0 % ROW_BLOCK == 0
    row = lambda w: pl.BlockSpec((ROW_BLOCK, w), lambda i: (i, 0))
    full = lambda a: pl.BlockSpec(a.shape, lambda i: (0,) * a.ndim)
    wo = w_out.astype(bf16)
    wqb = wq.astype(bf16)
    kb = keys.astype(bf16)
    g = g2.reshape(1, D_MODEL)
    return pl.pallas_call(
        functools.partial(_mid_kernel, nbp=nbp),
        grid=(n // ROW_BLOCK,),
        in_specs=[*_pair_specs(nbp, D_MODEL, xp_row0 // ROW_BLOCK), *_pair_specs(nbp, A_WIDTH),
                  *_pair_specs(nbp, M_WIDTH), full(wo), full(g), full(wqb), full(kb)],
        out_specs=[row(D_MODEL), row(D_MODEL), row(P_SLOTS), row(P_SLOTS)],
        out_shape=[jax.ShapeDtypeStruct((n, D_MODEL), f32), jax.ShapeDtypeStruct((n, D_MODEL), f32),
                   jax.ShapeDtypeStruct((n, P_SLOTS), i32), jax.ShapeDtypeStruct((n, P_SLOTS), f32)],
        compiler_params=pltpu.CompilerParams(dimension_semantics=("parallel",), vmem_limit_bytes=VMEM_LIMIT),
        name="outproj_retrieve",
    )(xp, xs, attp, atts, hp, hs, wo, g, wqb, kb)


def _gather_rows(eidx_ref, tab_ref, g_ref, t):
    for r in range(P_SLOTS):
        g_ref[pl.ds(r * ROW_TILES, ROW_TILES), :] = tab_ref[eidx_ref[t, r]]


def _tile_rows(t, rows=SUBLANES):
    return pl.ds(pl.multiple_of(t * rows, rows), rows)


def _pipelined_tokens(nt, eidx_ref, tab_ref, g0_s, g1_s, compute):
    bufs = (g0_s, g1_s)
    _gather_rows(eidx_ref, tab_ref, g0_s, 0)

    def body(j, carry):
        for u in range(TOKEN_UNROLL):
            t = TOKEN_UNROLL * j + u
            _gather_rows(eidx_ref, tab_ref, bufs[(u + 1) % 2], jnp.minimum(t + 1, nt - 1))
            compute(t, bufs[u % 2])
        return carry

    lax.fori_loop(0, nt // TOKEN_UNROLL, body, 0)


def _gelu_tanh(x):
    return 0.5 * x * (1.0 + jnp.tanh(math.sqrt(2.0 / math.pi) * (x + 0.044715 * (x * x * x))))


def _peer_u_kernel(eidx_ref, xn_ref, g_ref, tab_ref, w_ref, xl_s, r_s, g0_s, g1_s):
    nt = xn_ref.shape[0]
    xn = xn_ref[...]
    xh = xn.astype(bf16).astype(f32)
    xl = xn - xh
    for k in range(ROW_TILES):
        xl_s[pl.ds(k, nt, stride=2 * SUBLANES), :] = xh[:, k * LANES:(k + 1) * LANES]
        xl_s[pl.ds(SUBLANES + k, nt, stride=2 * SUBLANES), :] = xl[:, k * LANES:(k + 1) * LANES]
    diag = (lax.broadcasted_iota(i32, (SUBLANES, G_ROWS), 1) % ROW_TILES
            == lax.broadcasted_iota(i32, (SUBLANES, G_ROWS), 0))

    def compute(t, g_s):
        lhs = xl_s[_tile_rows(t, 2 * SUBLANES), :].astype(bf16)
        out = lax.dot_general(lhs, g_s[...], (((1,), (1,)), ((), ())), preferred_element_type=f32)
        part = jnp.where(diag, out[:SUBLANES] + out[SUBLANES:], 0.0)
        for c in range(ROW_TILES):
            r_s[c, _tile_rows(t), :] = part[:, c * LANES:(c + 1) * LANES]

    _pipelined_tokens(nt, eidx_ref, tab_ref, g0_s, g1_s, compute)
    cols = []
    for c in range(ROW_TILES):
        acc = r_s[c, pl.ds(0, nt, stride=SUBLANES), :]
        for k in range(1, SUBLANES):
            acc = acc + r_s[c, pl.ds(k, nt, stride=SUBLANES), :]
        cols.append(acc)
    s = jnp.concatenate(cols, axis=-1)
    fold = (lax.broadcasted_iota(i32, (G_ROWS, P_SLOTS), 0) // ROW_TILES
            == lax.broadcasted_iota(i32, (G_ROWS, P_SLOTS), 1)).astype(bf16)
    sh, sl = _split_bf16(s)
    act = jnp.dot(sh, fold, preferred_element_type=f32) + jnp.dot(sl, fold, preferred_element_type=f32)
    w_ref[...] = g_ref[...] * _gelu_tanh(act)


def _peer_v_kernel(eidx_ref, w_ref, tab_ref, peer_ref, wl_s, o_s, g0_s, g1_s):
    nt = w_ref.shape[0]
    spread = (lax.broadcasted_iota(i32, (P_SLOTS, G_ROWS), 1) // ROW_TILES
              == lax.broadcasted_iota(i32, (P_SLOTS, G_ROWS), 0)).astype(bf16)
    wexp = jnp.dot(w_ref[...].astype(bf16), spread, preferred_element_type=f32)
    lane = lax.broadcasted_iota(i32, (nt, LANES), 1)
    for c in range(ROW_TILES):
        wc = wexp[:, c * LANES:(c + 1) * LANES]
        for k in range(SUBLANES):
            wl_s[c, pl.ds(k, nt, stride=SUBLANES), :] = jnp.where(lane % ROW_TILES == k, wc, 0.0)

    def compute(t, g_s):
        lhs = jnp.concatenate([wl_s[c, _tile_rows(t), :] for c in range(ROW_TILES)], axis=-1).astype(bf16)
        o_s[_tile_rows(t), :] = jnp.dot(lhs, g_s[...], preferred_element_type=f32)

    _pipelined_tokens(nt, eidx_ref, tab_ref, g0_s, g1_s, compute)
    for k in range(ROW_TILES):
        peer_ref[:, k * LANES:(k + 1) * LANES] = o_s[pl.ds(k, nt, stride=SUBLANES), :]


def _final_kernel(pa_ref, pb_ref, xa_ref, xb_ref, gf_ref, yp_ref, ys_ref, *, nba, nbp):
    y = _rms(_pick(nba, xa_ref, xb_ref) + _pick(nba, pa_ref, pb_ref), gf_ref[...])

    @pl.when(pl.program_id(0) < nbp)
    def _():
        yp_ref[...] = y

    @pl.when(pl.program_id(0) >= nbp)
    def _():
        ys_ref[...] = y


def _final(peer_a, peer_b, x1_a, x1_b, gf, n_p):
    n = x1_a.shape[0] + x1_b.shape[0]
    nba, nbp = x1_a.shape[0] // ROW_BLOCK, n_p // ROW_BLOCK
    g = gf.reshape(1, D_MODEL)
    return pl.pallas_call(
        functools.partial(_final_kernel, nba=nba, nbp=nbp),
        grid=(n // ROW_BLOCK,),
        in_specs=[*_pair_specs(nba, D_MODEL), *_pair_specs(nba, D_MODEL), pl.BlockSpec(g.shape, lambda i: (0, 0))],
        out_specs=list(_pair_specs(nbp, D_MODEL)),
        out_shape=[jax.ShapeDtypeStruct((n_p, D_MODEL), f32), jax.ShapeDtypeStruct((n - n_p, D_MODEL), f32)],
        compiler_params=pltpu.CompilerParams(dimension_semantics=("arbitrary",), vmem_limit_bytes=VMEM_LIMIT),
        name="final_norm",
    )(peer_a, peer_b, x1_a, x1_b, g)


def _peer_specs():
    row = lambda w: pl.BlockSpec((ROW_BLOCK, w), lambda i: (i, 0))
    idx = pl.BlockSpec((ROW_BLOCK, P_SLOTS), lambda i: (i, 0), memory_space=pltpu.SMEM)
    tab = pl.BlockSpec(memory_space=pltpu.VMEM)
    gscr = pltpu.VMEM((G_ROWS, LANES), bf16)
    params = pltpu.CompilerParams(dimension_semantics=("arbitrary",), vmem_limit_bytes=VMEM_LIMIT)
    return row, idx, tab, gscr, params


def _expert_table(tab):
    return tab.astype(bf16).reshape(tab.shape[0], ROW_TILES, LANES)


def _peer_u(eidx, xn, g, utab, n):
    row, idx, tab, gscr, params = _peer_specs()
    return pl.pallas_call(
        _peer_u_kernel,
        grid=(n // ROW_BLOCK,),
        in_specs=[idx, row(D_MODEL), row(P_SLOTS), tab],
        out_specs=row(P_SLOTS),
        out_shape=jax.ShapeDtypeStruct((n, P_SLOTS), f32),
        scratch_shapes=[pltpu.VMEM((ROW_BLOCK * 2 * SUBLANES, LANES), f32),
                        pltpu.VMEM((ROW_TILES, ROW_BLOCK * SUBLANES, LANES), f32), gscr, gscr],
        compiler_params=params,
        name="peer_u",
    )(eidx, xn, g, utab)


def _peer_v(eidx, w, vtab):
    n = w.shape[0]
    row, idx, tab, gscr, params = _peer_specs()
    return pl.pallas_call(
        _peer_v_kernel,
        grid=(n // ROW_BLOCK,),
        in_specs=[idx, row(P_SLOTS), tab],
        out_specs=row(D_MODEL),
        out_shape=jax.ShapeDtypeStruct((n, D_MODEL), f32),
        scratch_shapes=[pltpu.VMEM((ROW_TILES, ROW_BLOCK * SUBLANES, LANES), f32),
                        pltpu.VMEM((ROW_BLOCK * SUBLANES, LANES), f32), gscr, gscr],
        compiler_params=params,
        name="peer_v",
    )(eidx, w, vtab)


def _sc_table_kernel(t_ref, o_ref):
    t = t_ref[...]
    o_ref[...] = pltpu.pack_elementwise([t[:, :SC_WORDS], t[:, SC_WORDS:]], packed_dtype=bf16)


def _sc_table(tab):
    e = tab.shape[0]
    return pl.pallas_call(
        _sc_table_kernel,
        grid=(e // ROW_BLOCK,),
        in_specs=[pl.BlockSpec((ROW_BLOCK, D_MODEL), lambda i: (i, 0))],
        out_specs=pl.BlockSpec((ROW_BLOCK, SC_WORDS), lambda i: (i, 0)),
        out_shape=jax.ShapeDtypeStruct((e, SC_WORDS), jnp.uint32),
        compiler_params=pltpu.CompilerParams(dimension_semantics=("parallel",), vmem_limit_bytes=VMEM_LIMIT),
        name="sc_table",
    )(tab)


def _sc_unpack(words):
    return plsc.unpack(plsc.bitcast(words, bf16), format=plsc.PackFormat.INTERLEAVED)


def _sc_gelu_tanh(x):
    z = math.sqrt(2.0 / math.pi) * (x + 0.044715 * (x * x * x))
    return 0.5 * x * (2.0 - 2.0 / (jnp.exp(2.0 * z) + 1.0))


def _sc_peer(utab32, vtab32, eidx, xn, gate):
    n_sc = eidx.shape[0]
    per = n_sc // SC_WORKERS
    units = 4 * per
    assert n_sc % (SC_WORKERS * SUBLANES) == 0

    def body(u_hbm, v_hbm, eidx_hbm, x_hbm, g_hbm, out_hbm,
             idx_v, x_v, g_v, rows_v, acc_v, w_v, w16_v, out_v, row_sems, tok_sems, out_sem):
        base = (lax.axis_index("s") * 2 + lax.axis_index("c")) * per
        lanes = lax.broadcasted_iota(i32, (SC_LANES,), 0)

        def token_copies(tok):
            tslot = tok % 2
            return (pltpu.make_async_copy(eidx_hbm.at[base + tok], idx_v.at[tslot], tok_sems.at[0]),
                    pltpu.make_async_copy(x_hbm.at[base + tok], x_v.at[tslot], tok_sems.at[1]),
                    pltpu.make_async_copy(g_hbm.at[base + tok], g_v.at[tslot], tok_sems.at[2]))

        def store_out(tok):
            return pltpu.make_async_copy(out_v, out_hbm.at[base + tok], out_sem)

        def gather(tab_hbm, g):
            tok, k = g // 4, g % 4
            return pltpu.make_async_copy(tab_hbm.at[idx_v.at[tok % 2, pl.ds((k % 2) * SC_UNIT_ROWS, SC_UNIT_ROWS)]],
                                         rows_v.at[k % 2], row_sems.at[k % 2])

        def start(g):
            @pl.when(g % 4 < 2)
            def _():
                gather(u_hbm, g).start()

            @pl.when(g % 4 >= 2)
            def _():
                gather(v_hbm, g).start()

        def compute_u(tslot, half, rows):
            @pl.loop(0, SC_UNIT_ROWS // SC_ROW_GROUP)
            def _(rg):
                slot0 = half * SC_UNIT_ROWS + rg * SC_ROW_GROUP

                @pl.loop(0, SC_CHUNKS // SC_CHUNK_GROUP)
                def _(cg):
                    keep = jnp.where(cg > 0, 1.0, 0.0).astype(f32)
                    accs = [acc_v[pl.ds((slot0 + r) * SC_LANES, SC_LANES)] * keep for r in range(SC_ROW_GROUP)]
                    for c in range(SC_CHUNK_GROUP):
                        ch = cg * SC_CHUNK_GROUP + c
                        xa = x_v[tslot, pl.ds(ch * SC_LANES, SC_LANES)]
                        xb = x_v[tslot, pl.ds(SC_WORDS + ch * SC_LANES, SC_LANES)]
                        for r in range(SC_ROW_GROUP):
                            a, b = _sc_unpack(rows[rg * SC_ROW_GROUP + r, pl.ds(ch * SC_LANES, SC_LANES)])
                            accs[r] = accs[r] + a * xa + b * xb
                    for r in range(SC_ROW_GROUP):
                        acc_v[pl.ds((slot0 + r) * SC_LANES, SC_LANES)] = accs[r]

        def gate_weights(tslot):
            @pl.loop(0, P_SLOTS // SC_LANES)
            def _(sg):
                first = sg * SC_LANES * SC_LANES
                act = jnp.zeros((SC_LANES,), f32)
                for lane in range(SC_LANES):
                    act = act + plsc.load_gather(acc_v, [first + lanes * SC_LANES + lane])
                w_v[pl.ds(sg * SC_LANES, SC_LANES)] = g_v[tslot, pl.ds(sg * SC_LANES, SC_LANES)] * _sc_gelu_tanh(act)

            @pl.loop(0, P_SLOTS // SC_LANES)
            def _(sg):
                for r in range(SC_LANES):
                    w16_v[pl.ds((sg * SC_LANES + r) * SC_LANES, SC_LANES)] = plsc.load_gather(
                        w_v, [jnp.zeros((SC_LANES,), i32) + (sg * SC_LANES + r)])

        def compute_v(half, rows):
            @pl.loop(0, SC_CHUNKS // SC_CHUNK_GROUP)
            def _(cg):
                first = cg * SC_CHUNK_GROUP

                @pl.loop(0, SC_UNIT_ROWS // SC_ROW_GROUP)
                def _(rg):
                    keep = jnp.where(jnp.logical_or(half == 1, rg > 0), 1.0, 0.0).astype(f32)
                    spots = [pl.ds((j % 2) * SC_WORDS + (first + j // 2) * SC_LANES, SC_LANES)
                             for j in range(2 * SC_CHUNK_GROUP)]
                    accs = [out_v[spot] * keep for spot in spots]
                    for r in range(SC_ROW_GROUP):
                        row = rg * SC_ROW_GROUP + r
                        wv = w16_v[pl.ds((half * SC_UNIT_ROWS + row) * SC_LANES, SC_LANES)]
                        for c in range(SC_CHUNK_GROUP):
                            a, b = _sc_unpack(rows[row, pl.ds((first + c) * SC_LANES, SC_LANES)])
                            accs[2 * c] = accs[2 * c] + a * wv
                            accs[2 * c + 1] = accs[2 * c + 1] + b * wv
                    for spot, acc in zip(spots, accs):
                        out_v[spot] = acc

        for cp in token_copies(0):
            cp.start()
        for cp in token_copies(0):
            cp.wait()
        start(0)

        @pl.loop(0, units)
        def _(g):
            tok, k = g // 4, g % 4

            @pl.when(jnp.logical_and(k == 0, tok + 1 < per))
            def _():
                for cp in token_copies(tok + 1):
                    cp.start()

            @pl.when(jnp.logical_and(k == 3, tok + 1 < per))
            def _():
                for cp in token_copies(tok + 1):
                    cp.wait()

            @pl.when(g + 1 < units)
            def _():
                start(g + 1)

            gather(u_hbm, g).wait()
            rows = rows_v.at[k % 2]

            @pl.when(k < 2)
            def _():
                compute_u(tok % 2, k, rows)

            @pl.when(k == 1)
            def _():
                gate_weights(tok % 2)

            @pl.when(jnp.logical_and(k == 2, tok > 0))
            def _():
                store_out(tok - 1).wait()

            @pl.when(k >= 2)
            def _():
                compute_v(k - 2, rows)

            @pl.when(k == 3)
            def _():
                store_out(tok).start()

        store_out(per - 1).wait()

    return pl.kernel(
        body, mesh=plsc.VectorSubcoreMesh(core_axis_name="c", subcore_axis_name="s"),
        out_type=jax.ShapeDtypeStruct((n_sc, D_MODEL), f32),
        scratch_types=[pltpu.VMEM((2, P_SLOTS), i32), pltpu.VMEM((2, D_MODEL), f32), pltpu.VMEM((2, P_SLOTS), f32),
                       pltpu.VMEM((2, SC_UNIT_ROWS, SC_WORDS), jnp.uint32), pltpu.VMEM((P_SLOTS * SC_LANES,), f32),
                       pltpu.VMEM((P_SLOTS,), f32), pltpu.VMEM((P_SLOTS * SC_LANES,), f32), pltpu.VMEM((D_MODEL,), f32),
                       pltpu.SemaphoreType.DMA((2,)), pltpu.SemaphoreType.DMA((3,)), pltpu.SemaphoreType.DMA],
        compiler_params=pltpu.CompilerParams(needs_layout_passes=False),
        name="sc_peer",
    )(utab32, vtab32, eidx, xn, gate)


def kernel(x_prompt, x_sample, cache_k, cache_v, state_C, state_n, state_m, state_conv, norm1_g, w_in, b_gates, rel_bias, conv_w, conv_b, mh_norm_g, w_out, norm2_g, peer_wq, peer_keys, peer_u, peer_v, final_g):
    bp, sp, d = x_prompt.shape
    bs, ts, _ = x_sample.shape
    n_p, n_s = bp * sp, bs * ts
    n = n_p + n_s
    assert n_p % ROW_BLOCK == 0 and n_s % ROW_BLOCK == 0 and d == D_MODEL
    depth = w_in.shape[0]
    assert depth == 1, "the final norm is fused into the last layer's PEER pass"
    l = 0
    xp, xs = x_prompt.reshape(n_p, d), x_sample.reshape(n_s, d)

    aq, ak, av, mqk, mv, mo, gates = _inproj(xp, xs, norm1_g[l], w_in[l])
    zeros = lambda *shp: jnp.zeros(shp, f32)
    mparams = (conv_w[l], conv_b[l], b_gates[l], mh_norm_g[l])
    retrieval = (w_out[l], norm2_g[l], peer_wq[l], peer_keys[l])

    def prompt_mixers(b0, nb):
        att = _attn_prompt(aq, ak, av, rel_bias[l], b0, nb, sp)
        return (att,) + tuple(_mlstm(mqk, mv, mo, gates, b0 * sp, nb, sp, zeros(nb, M_HEADS, M_DH, M_DH),
                                     zeros(nb, M_HEADS, M_DH), zeros(nb, M_HEADS), zeros(nb, CONV_W - 1, 2 * M_WIDTH),
                                     *mparams))

    assert 0 < SC_PROMPT_SEQS < bp
    n_sc = SC_PROMPT_SEQS * sp
    att_a, h_a, c_a, nn_a, mm_a = prompt_mixers(0, SC_PROMPT_SEQS)
    x1_a, xn_a, eidx_a, gate_a = _mid(xp, 0, xs, att_a, att_a, h_a, h_a, n_sc, 0, *retrieval)
    peer_a = _sc_peer(_sc_table(peer_u[l]), _sc_table(peer_v[l]), eidx_a, xn_a, gate_a)

    att_b, h_b, c_b, nn_b, mm_b = prompt_mixers(SC_PROMPT_SEQS, bp - SC_PROMPT_SEQS)
    lcache = cache_k.shape[2]
    att_s = _attn_sample(aq, ak, av, cache_k[l].reshape(bs, lcache, A_WIDTH),
                         cache_v[l].reshape(bs, lcache, A_WIDTH), rel_bias[l], n_p, bs, ts)
    h_s, c_s, nn_s, mm_s = _mlstm(mqk, mv, mo, gates, n_p, bs, ts, state_C[l], state_n[l], state_m[l],
                                  state_conv[l], *mparams)
    x1_b, xn_b, eidx_b, gate_b = _mid(xp, n_sc, xs, att_b, att_s, h_b, h_s, n_p - n_sc, n_s, *retrieval)
    w_b = _peer_u(eidx_b, xn_b, gate_b, _expert_table(peer_u[l]), n - n_sc)
    peer_b = _peer_v(eidx_b, w_b, _expert_table(peer_v[l]))
    y_p, y_s = _final(peer_a, peer_b, x1_a, x1_b, final_g, n_p)
    c_p, nn_p, mm_p = (jnp.concatenate(ab, axis=0) for ab in ((c_a, c_b), (nn_a, nn_b), (mm_a, mm_b)))

    def tail(a, row0, bsz, t, keep):
        return jnp.stack([a[row0 + (b + 1) * t - keep:row0 + (b + 1) * t] for b in range(bsz)])

    keep = min(WINDOW, sp)
    heads = lambda a: a.reshape(a.shape[0], a.shape[1], A_HEADS, A_DH)
    ctail = CONV_W - 1
    conv_tail = lambda buf, a, row0, bsz, t: jnp.concatenate([buf.astype(a.dtype), tail(a, row0, bsz, t, min(ctail, t))],
                                                             axis=1)[:, -ctail:]
    st = lambda a: a[None]
    return (y_p.reshape(bp, sp, d), y_s.reshape(bs, ts, d),
            st(heads(tail(ak, 0, bp, sp, keep))), st(heads(tail(av, 0, bp, sp, keep))),
            st(c_p), st(nn_p[:, :M_HEADS]), st(mm_p[:, :M_HEADS, 0]),
            st(conv_tail(zeros(bp, ctail, 2 * M_WIDTH), mqk, 0, bp, sp)),
            st(heads(ak[n_p:].reshape(bs, ts, A_WIDTH))), st(heads(av[n_p:].reshape(bs, ts, A_WIDTH))),
            st(c_s), st(nn_s[:, :M_HEADS]), st(mm_s[:, :M_HEADS, 0]),
            st(conv_tail(state_conv[l], mqk, n_p, bs, ts)))
```

```python
import functools
import math

import jax
import jax.numpy as jnp
from jax import lax
from jax.experimental import pallas as pl
from jax.experimental.pallas import tpu as pltpu
from jax.experimental.pallas import tpu_sc as plsc

f32 = jnp.float32
bf16 = jnp.bfloat16
i32 = jnp.int32

D_MODEL = 1024
CHUNK = 64
A_HEADS = 8
A_DH = 64
A_WIDTH = A_HEADS * A_DH
BAND_CHUNKS = 8
WINDOW = BAND_CHUNKS * CHUNK
MAX_REL = 128
ATT_SCALE = A_DH ** -0.5
M_HEADS = 4
M_DH = 128
M_WIDTH = M_HEADS * M_DH
CONV_W = 4
P_HEADS = 8
P_DKEY = 256
N_KEYS = 128
P_TOPK = 16
P_SLOTS = P_HEADS * P_TOPK
EPS = 1e-6
NEG = -1e30

LANES = 128
SUBLANES = 8
ROW_BLOCK = 256
ATT_TILE = 512
ATT_SUB = 128
ATT_KEYS = ATT_SUB + WINDOW
ROW_TILES = D_MODEL // LANES
G_ROWS = P_SLOTS * ROW_TILES
TOKEN_UNROLL = 8
VMEM_LIMIT = 56 * 1024 * 1024

SC_WORKERS = 32
SC_LANES = 16
SC_UNIT_ROWS = P_SLOTS // 2
SC_ROW_GROUP = 16
SC_WORDS = D_MODEL // 2
SC_CHUNKS = SC_WORDS // SC_LANES
SC_CHUNK_GROUP = 8
SC_EXTRA_ROWS = 1024


def _rms(x, g):
    return x * lax.rsqrt(jnp.mean(x * x, axis=-1, keepdims=True) + EPS) * g


def _split_bf16(x):
    hi = x.astype(bf16)
    lo = (x - hi.astype(f32)).astype(bf16)
    return hi, lo


def _pair_specs(nbp, width, first=0, second=0):
    return (pl.BlockSpec((ROW_BLOCK, width), lambda i: (first + jnp.minimum(i, nbp - 1), 0)),
            pl.BlockSpec((ROW_BLOCK, width), lambda i: (second + jnp.maximum(i - nbp, 0), 0)))


def _pick(nbp, p_ref, s_ref):
    return jnp.where(pl.program_id(0) < nbp, p_ref[...], s_ref[...])


def _inproj_kernel(xp_ref, xs_ref, g_ref, w_ref, wgh_ref, wgl_ref,
                   aq_ref, ak_ref, av_ref, mqk_ref, mv_ref, mo_ref, gate_ref, *, nbp):
    xn = _rms(_pick(nbp, xp_ref, xs_ref), g_ref[...])
    xh, xl = _split_bf16(xn)

    def proj(lo, hi):
        return jnp.dot(xh, w_ref[:, lo:hi], preferred_element_type=f32)

    aq_ref[...] = proj(0, 512)
    ak_ref[...] = proj(512, 1024)
    av_ref[...] = proj(1024, 1536)
    mqk_ref[...] = proj(1536, 2560)
    mv_ref[...] = proj(2560, 3072)
    mo_ref[...] = proj(3072, 3584)
    gate_ref[...] = (jnp.dot(xh, wgh_ref[...], preferred_element_type=f32)
                     + jnp.dot(xl, wgh_ref[...], preferred_element_type=f32)
                     + jnp.dot(xh, wgl_ref[...], preferred_element_type=f32))


def _inproj(xp, xs, g1, w_in):
    n = xp.shape[0] + xs.shape[0]
    nbp = xp.shape[0] // ROW_BLOCK
    main = 3 * A_WIDTH + 4 * M_WIDTH
    w_main = w_in[:, :main].astype(bf16)
    wg = jnp.pad(w_in[:, main:], ((0, 0), (0, LANES - 2 * M_HEADS)))
    wgh, wgl = _split_bf16(wg)
    widths = (512, 512, 512, 1024, 512, 512, LANES)
    row = lambda w: pl.BlockSpec((ROW_BLOCK, w), lambda i: (i, 0))
    full = lambda a: pl.BlockSpec(a.shape, lambda i: (0,) * a.ndim)
    g = g1.reshape(1, D_MODEL)
    return pl.pallas_call(
        functools.partial(_inproj_kernel, nbp=nbp),
        grid=(n // ROW_BLOCK,),
        in_specs=[*_pair_specs(nbp, D_MODEL), full(g), full(w_main), full(wgh), full(wgl)],
        out_specs=[row(w) for w in widths],
        out_shape=[jax.ShapeDtypeStruct((n, w), f32) for w in widths],
        compiler_params=pltpu.CompilerParams(dimension_semantics=("parallel",), vmem_limit_bytes=VMEM_LIMIT),
        name="inproj",
    )(xp, xs, g, w_main, wgh, wgl)


def _attn_heads(q, k, v, bias_ref, key_ok):
    outs = []
    for h in range(A_HEADS):
        sl = slice(h * A_DH, (h + 1) * A_DH)
        s = lax.dot_general(q[:, sl], k[:, sl], (((1,), (1,)), ((), ())), preferred_element_type=f32)
        s = s * ATT_SCALE + bias_ref[h]
        if key_ok is not None:
            s = jnp.where(key_ok, s, NEG)
        m = jnp.max(s, axis=-1, keepdims=True)
        p = jnp.exp(s - m)
        l = jnp.sum(p, axis=-1, keepdims=True)
        o = jnp.dot(p.astype(bf16), v[:, sl], preferred_element_type=f32)
        outs.append(o / l)
    return jnp.concatenate(outs, axis=-1)


def _attn_prompt_kernel(q_ref, k0_ref, k1_ref, v0_ref, v1_ref, bias_ref, o_ref, *, t0):
    t = t0 + pl.program_id(1)
    q = q_ref[...].astype(bf16)
    k = jnp.concatenate([k0_ref[...], k1_ref[...]], axis=0).astype(bf16)
    v = jnp.concatenate([v0_ref[...], v1_ref[...]], axis=0).astype(bf16)
    col = lax.broadcasted_iota(i32, (1, ATT_KEYS), 1)
    for s in range(ATT_TILE // ATT_SUB):
        lo = s * ATT_SUB
        key_ok = (t * ATT_TILE + lo + col) >= WINDOW
        o_ref[lo:lo + ATT_SUB, :] = _attn_heads(q[lo:lo + ATT_SUB], k[lo:lo + ATT_KEYS], v[lo:lo + ATT_KEYS],
                                                 bias_ref, key_ok)


def _attn_sample_kernel(q_ref, k_ref, v_ref, bias_ref, o_ref):
    o_ref[...] = _attn_heads(q_ref[...].astype(bf16), k_ref[0].astype(bf16), v_ref[0].astype(bf16), bias_ref, None)


def _rel_bias_table(rel_bias, rows, cols, offset, valid):
    span = rows + cols - 1
    rel = offset + rows - 1 - jnp.arange(span)
    diag = rel_bias[:, jnp.clip(rel, -MAX_REL, MAX_REL) + MAX_REL].astype(f32)
    diag = jnp.pad(diag, ((0, 0), (0, 1)))
    flat = jnp.tile(diag, (1, rows))[:, rows - 1:rows - 1 + rows * span]
    return jnp.where(valid[None], flat.reshape(-1, rows, span)[:, :, :cols], NEG)


def _attn_prompt(q, k, v, rel_bias, seq, s, lo, hi):
    assert s % ATT_TILE == 0 and WINDOW == ATT_TILE and lo % ATT_TILE == 0 and hi % ATT_TILE == 0
    nt, t0, cnt = s // ATT_TILE, lo // ATT_TILE, (hi - lo) // ATT_TILE
    i = jnp.arange(ATT_SUB)[:, None]
    j = jnp.arange(ATT_KEYS)[None, :]
    off = j - (i // CHUNK) * CHUNK
    bias = _rel_bias_table(rel_bias, ATT_SUB, ATT_KEYS, WINDOW, (off >= 0) & (off < WINDOW + CHUNK))
    cur = pl.BlockSpec((ATT_TILE, A_WIDTH), lambda b, t: (seq * nt + t0 + t, 0))
    prev = pl.BlockSpec((ATT_TILE, A_WIDTH), lambda b, t: (seq * nt + jnp.maximum(t0 + t - 1, 0), 0))
    return pl.pallas_call(
        functools.partial(_attn_prompt_kernel, t0=t0),
        grid=(1, cnt),
        in_specs=[cur, prev, cur, prev, cur, pl.BlockSpec(bias.shape, lambda b, t: (0, 0, 0))],
        out_specs=pl.BlockSpec((ATT_TILE, A_WIDTH), lambda b, t: (t, 0)),
        out_shape=jax.ShapeDtypeStruct((hi - lo, A_WIDTH), f32),
        compiler_params=pltpu.CompilerParams(dimension_semantics=("parallel", "parallel"),
                                             vmem_limit_bytes=VMEM_LIMIT),
        name="attn_prompt",
    )(q, k, k, v, v, bias)


def _attn_sample(q, k, v, ck, cv, rel_bias, row0, bsz, t):
    l = ck.shape[1]
    assert row0 % t == 0
    keys = -(-(l + t) // LANES) * LANES
    padk = ((0, 0), (0, keys - l - t), (0, 0))
    kk = jnp.pad(jnp.concatenate([ck, k[row0:].reshape(bsz, t, A_WIDTH)], axis=1), padk)
    vv = jnp.pad(jnp.concatenate([cv, v[row0:].reshape(bsz, t, A_WIDTH)], axis=1), padk)
    j = jnp.arange(keys)[None, :]
    bias = _rel_bias_table(rel_bias, t, keys, l, jnp.broadcast_to(j < l + t, (t, keys)))
    return pl.pallas_call(
        _attn_sample_kernel,
        grid=(bsz,),
        in_specs=[pl.BlockSpec((t, A_WIDTH), lambda b: (row0 // t + b, 0)),
                  pl.BlockSpec((1, keys, A_WIDTH), lambda b: (b, 0, 0)),
                  pl.BlockSpec((1, keys, A_WIDTH), lambda b: (b, 0, 0)),
                  pl.BlockSpec(bias.shape, lambda b: (0, 0, 0))],
        out_specs=pl.BlockSpec((t, A_WIDTH), lambda b: (b, 0)),
        out_shape=jax.ShapeDtypeStruct((bsz * t, A_WIDTH), f32),
        compiler_params=pltpu.CompilerParams(dimension_semantics=("parallel",), vmem_limit_bytes=VMEM_LIMIT),
        name="attn_sample",
    )(q, kk, vv, bias)


def _mlstm_chunk(a, vall, o_in, gate_in, cw_ref, cb_ref, bg_ref, mhg_ref, c_s, n_s, m_s, prev_s):
    lc = a.shape[0]
    ext = jnp.concatenate([prev_s[...], a], axis=0)
    conv = cb_ref[...]
    for j in range(CONV_W):
        lo = SUBLANES - (CONV_W - 1) + j
        conv = conv + cw_ref[j:j + 1, :] * ext[lo:lo + lc]
    prev_s[...] = a[lc - SUBLANES:lc]
    qk = conv * jax.nn.sigmoid(conv)

    z = gate_in + bg_ref[...]
    lane = lax.broadcasted_iota(i32, (lc, LANES), 1)
    row = lax.broadcasted_iota(i32, (lc, LANES), 0)
    logf = jnp.minimum(z, 0.0) - jnp.log1p(jnp.exp(-jnp.abs(z)))
    cum = jnp.where((lane >= M_HEADS) & (lane < 2 * M_HEADS), logf, 0.0)
    shift = 1
    while shift < lc:
        cum = cum + jnp.where(row >= shift, pltpu.roll(cum, shift, axis=0), 0.0)
        shift *= 2
    zc = jnp.where(lane < M_HEADS, z, cum)
    zt = jnp.concatenate([zc, jnp.zeros((LANES - lc, LANES), f32)], axis=0).T[:, :lc]

    ri = lax.broadcasted_iota(i32, (lc, lc), 0)
    ci = lax.broadcasted_iota(i32, (lc, lc), 1)
    causal = ri >= ci
    state = [(m_s[h:h + 1, 0:1], c_s[h], n_s[h:h + 1, :]) for h in range(M_HEADS)]
    hs, new_state = [], []
    for h in range(M_HEADS):
        sl = slice(h * M_DH, (h + 1) * M_DH)
        q = qk[:, sl]
        k = qk[:, M_WIDTH + h * M_DH:M_WIDTH + (h + 1) * M_DH] * (M_DH ** -0.5)
        v = vall[:, sl]
        i_col = zc[:, h:h + 1]
        b_col = zc[:, M_HEADS + h:M_HEADS + h + 1]
        i_row = zt[h:h + 1, :]
        b_row = zt[M_HEADS + h:M_HEADS + h + 1, :]
        m_prev, c_prev, n_prev = state[h]

        dmat = jnp.where(causal, b_col - b_row + i_row, NEG)
        inter = b_col + m_prev
        mt = jnp.maximum(jnp.max(dmat, axis=-1, keepdims=True), inter)
        w_intra = jnp.exp(dmat - mt)
        w_inter = jnp.exp(inter - mt)
        qb, kb, vb = q.astype(bf16), k.astype(bf16), v.astype(bf16)
        s = lax.dot_general(qb, kb, (((1,), (1,)), ((), ())), preferred_element_type=f32) * w_intra
        num = (w_inter * jnp.dot(qb, c_prev.astype(bf16), preferred_element_type=f32)
               + jnp.dot(s.astype(bf16), vb, preferred_element_type=f32))
        den = w_inter * jnp.sum(q * n_prev, axis=-1, keepdims=True) + jnp.sum(s, axis=-1, keepdims=True)
        hh = num / jnp.maximum(jnp.abs(den), jnp.exp(-mt))
        m_new = mt[lc - 1:lc, :]
        b_last = b_col[lc - 1:lc, :]
        w_s = jnp.exp(b_last - b_col + i_col - m_new)
        decay = jnp.exp(b_last + m_prev - m_new)
        kw = k * w_s
        new_state.append((jnp.broadcast_to(m_new, (1, LANES)),
                          decay * c_prev + lax.dot_general(kw.astype(bf16), vb, (((0,), (0,)), ((), ())),
                                                           preferred_element_type=f32),
                          decay * n_prev + jnp.sum(kw, axis=0, keepdims=True)))
        hs.append(hh * lax.rsqrt(jnp.mean(hh * hh, axis=-1, keepdims=True) + EPS))

    for h, (m_new, c_new, n_new) in enumerate(new_state):
        m_s[h:h + 1, :] = m_new
        c_s[h] = c_new
        n_s[h:h + 1, :] = n_new
    return jnp.concatenate(hs, axis=-1) * mhg_ref[...] * jax.nn.sigmoid(o_in)


def _mlstm_kernel(qk_ref, v_ref, o_ref, gate_ref, c0_ref, n0_ref, m0_ref, cbuf_ref,
                  cw_ref, cb_ref, bg_ref, mhg_ref,
                  h_ref, cout_ref, nout_ref, mout_ref,
                  c_s, n_s, m_s, prev_s):
    c = pl.program_id(1)

    @pl.when(c == 0)
    def _():
        c_s[...] = c0_ref[0]
        n_s[...] = n0_ref[0]
        m_s[...] = m0_ref[0]
        prev_s[...] = cbuf_ref[0]

    h_ref[...] = _mlstm_chunk(qk_ref[...], v_ref[...], o_ref[...], gate_ref[...],
                              cw_ref, cb_ref, bg_ref, mhg_ref, c_s, n_s, m_s, prev_s)

    @pl.when(c == pl.num_programs(1) - 1)
    def _():
        cout_ref[0] = c_s[...]
        nout_ref[0] = n_s[...]
        mout_ref[0] = m_s[...]


def _mlstm(mqk, mv, mo, gates, row0, bsz, t, c0, n0, m0, cbuf, conv_w, conv_b, b_gates, mh_g):
    lc = min(CHUNK, t)
    step = lc
    nc = t // step
    assert t % step == 0 and lc % SUBLANES == 0 and row0 % step == 0
    n0p = jnp.pad(n0.astype(f32), ((0, 0), (0, SUBLANES - M_HEADS), (0, 0)))
    m0p = jnp.pad(jnp.broadcast_to(m0.astype(f32)[:, :, None], (bsz, M_HEADS, LANES)),
                  ((0, 0), (0, SUBLANES - M_HEADS), (0, 0)))
    cbp = jnp.pad(cbuf.astype(f32), ((0, 0), (SUBLANES - (CONV_W - 1), 0), (0, 0)))
    bg = jnp.pad(b_gates.astype(f32), (0, LANES - 2 * M_HEADS)).reshape(1, LANES)
    seq = lambda w: pl.BlockSpec((step, w), lambda b, c: (row0 // step + b * nc + c, 0))
    out_seq = pl.BlockSpec((step, M_WIDTH), lambda b, c: (b * nc + c, 0))
    per_b = lambda shp: pl.BlockSpec((1,) + shp, lambda b, c: (b,) + (0,) * len(shp))
    full = lambda a: pl.BlockSpec(a.shape, lambda b, c: (0,) * a.ndim)
    cb = conv_b.reshape(1, -1)
    mhg = mh_g.reshape(1, -1)
    return pl.pallas_call(
        _mlstm_kernel,
        grid=(bsz, nc),
        in_specs=[seq(2 * M_WIDTH), seq(M_WIDTH), seq(M_WIDTH), seq(LANES),
                  per_b((M_HEADS, M_DH, M_DH)), per_b((SUBLANES, M_DH)), per_b((SUBLANES, LANES)),
                  per_b((SUBLANES, 2 * M_WIDTH)),
                  full(conv_w), full(cb), full(bg), full(mhg)],
        out_specs=[out_seq, per_b((M_HEADS, M_DH, M_DH)), per_b((SUBLANES, M_DH)), per_b((SUBLANES, LANES))],
        out_shape=[jax.ShapeDtypeStruct((bsz * t, M_WIDTH), f32),
                   jax.ShapeDtypeStruct((bsz, M_HEADS, M_DH, M_DH), f32),
                   jax.ShapeDtypeStruct((bsz, SUBLANES, M_DH), f32),
                   jax.ShapeDtypeStruct((bsz, SUBLANES, LANES), f32)],
        scratch_shapes=[pltpu.VMEM((M_HEADS, M_DH, M_DH), f32), pltpu.VMEM((SUBLANES, M_DH), f32),
                        pltpu.VMEM((SUBLANES, LANES), f32), pltpu.VMEM((SUBLANES, 2 * M_WIDTH), f32)],
        compiler_params=pltpu.CompilerParams(dimension_semantics=("parallel", "arbitrary"),
                                             vmem_limit_bytes=VMEM_LIMIT),
        name="mlstm",
    )(mqk, mv, mo, gates, c0.astype(f32), n0p, m0p, cbp, conv_w, cb, bg, mhg)


def _topk_rows(s, k, payload=None):
    n = s.shape[0]
    rows = lax.broadcasted_iota(i32, s.shape, 0).astype(f32)
    vals, ids = [], []
    for _ in range(k):
        m = jnp.max(s, axis=0, keepdims=True)
        pos = jnp.min(jnp.where(s == m, rows, float(n)), axis=0, keepdims=True)
        sel = rows == pos
        vals.append(m)
        ids.append(pos if payload is None else jnp.max(jnp.where(sel, payload, -1.0), axis=0, keepdims=True))
        s = jnp.where(sel, -jnp.inf, s)
    return jnp.concatenate(vals, axis=0), jnp.concatenate(ids, axis=0)


def _mid_kernel(xp_ref, xs_ref, attp_ref, atts_ref, hp_ref, hs_ref, wo_ref, g2_ref, wq_ref, keys_ref,
                x1_ref, xn_ref, eidx_ref, gate_ref, *, nbp):
    cat = jnp.concatenate([_pick(nbp, attp_ref, atts_ref), _pick(nbp, hp_ref, hs_ref)], axis=-1).astype(bf16)
    x1 = _pick(nbp, xp_ref, xs_ref) + jnp.dot(cat, wo_ref[...], preferred_element_type=f32)
    x1_ref[...] = x1
    xn = _rms(x1, g2_ref[...])
    xn_ref[...] = xn
    xb = xn.astype(bf16)
    e_rows, g_rows = [], []
    for h in range(P_HEADS):
        q = jnp.dot(xb, wq_ref[:, h * P_DKEY:(h + 1) * P_DKEY], preferred_element_type=f32).astype(bf16)
        half = []
        for p in range(2):
            st = lax.dot_general(keys_ref[h, p], q[:, p * N_KEYS:(p + 1) * N_KEYS], (((1,), (1,)), ((), ())),
                                 preferred_element_type=f32)
            half.append(_topk_rows(st, P_TOPK))
        (v0, i0), (v1, i1) = half
        width = [P_TOPK // (a + 1) for a in range(P_TOPK)]
        fill = -sum(width) % SUBLANES
        cand = jnp.concatenate([v0[a:a + 1] + v1[:width[a]] for a in range(P_TOPK)]
                               + [jnp.full((fill, v0.shape[1]), -jnp.inf, f32)], axis=0)
        cidx = jnp.concatenate([i0[a:a + 1] * float(N_KEYS) + i1[:width[a]] for a in range(P_TOPK)]
                               + [jnp.zeros((fill, v0.shape[1]), f32)], axis=0)
        top_s, eid = _topk_rows(cand, P_TOPK, payload=cidx)
        ex = jnp.exp(top_s - top_s[0:1])
        e_rows.append(eid)
        g_rows.append(ex / jnp.sum(ex, axis=0, keepdims=True))
    eidx_ref[...] = jnp.concatenate(e_rows, axis=0).T.astype(i32)
    gate_ref[...] = jnp.concatenate(g_rows, axis=0).T


def _mid(xp, xp_row0, xs, xs_row0, attp, atts, hp, hs, n_a, n_b, w_out, g2, wq, keys):
    n = n_a + n_b
    nbp = n_a // ROW_BLOCK
    assert n_a % ROW_BLOCK == 0 and n_b % ROW_BLOCK == 0 and xp_row0 % ROW_BLOCK == 0 and xs_row0 % ROW_BLOCK == 0
    row = lambda w: pl.BlockSpec((ROW_BLOCK, w), lambda i: (i, 0))
    full = lambda a: pl.BlockSpec(a.shape, lambda i: (0,) * a.ndim)
    wo = w_out.astype(bf16)
    wqb = wq.astype(bf16)
    kb = keys.astype(bf16)
    g = g2.reshape(1, D_MODEL)
    return pl.pallas_call(
        functools.partial(_mid_kernel, nbp=nbp),
        grid=(n // ROW_BLOCK,),
        in_specs=[*_pair_specs(nbp, D_MODEL, xp_row0 // ROW_BLOCK, xs_row0 // ROW_BLOCK), *_pair_specs(nbp, A_WIDTH),
                  *_pair_specs(nbp, M_WIDTH), full(wo), full(g), full(wqb), full(kb)],
        out_specs=[row(D_MODEL), row(D_MODEL), row(P_SLOTS), row(P_SLOTS)],
        out_shape=[jax.ShapeDtypeStruct((n, D_MODEL), f32), jax.ShapeDtypeStruct((n, D_MODEL), f32),
                   jax.ShapeDtypeStruct((n, P_SLOTS), i32), jax.ShapeDtypeStruct((n, P_SLOTS), f32)],
        compiler_params=pltpu.CompilerParams(dimension_semantics=("parallel",), vmem_limit_bytes=VMEM_LIMIT),
        name="outproj_retrieve",
    )(xp, xs, attp, atts, hp, hs, wo, g, wqb, kb)


def _gather_rows(eidx_ref, tab_ref, g_ref, t):
    for r in range(P_SLOTS):
        g_ref[pl.ds(r * ROW_TILES, ROW_TILES), :] = tab_ref[eidx_ref[t, r]]


def _tile_rows(t, rows=SUBLANES):
    return pl.ds(pl.multiple_of(t * rows, rows), rows)


def _pipelined_tokens(nt, eidx_ref, tab_ref, g0_s, g1_s, compute):
    bufs = (g0_s, g1_s)
    _gather_rows(eidx_ref, tab_ref, g0_s, 0)

    def body(j, carry):
        for u in range(TOKEN_UNROLL):
            t = TOKEN_UNROLL * j + u
            _gather_rows(eidx_ref, tab_ref, bufs[(u + 1) % 2], jnp.minimum(t + 1, nt - 1))
            compute(t, bufs[u % 2])
        return carry

    lax.fori_loop(0, nt // TOKEN_UNROLL, body, 0)


def _gelu_tanh(x):
    return 0.5 * x * (1.0 + jnp.tanh(math.sqrt(2.0 / math.pi) * (x + 0.044715 * (x * x * x))))


def _peer_u_kernel(eidx_ref, xn_ref, g_ref, tab_ref, w_ref, xl_s, r_s, g0_s, g1_s):
    nt = xn_ref.shape[0]
    xn = xn_ref[...]
    xh = xn.astype(bf16).astype(f32)
    xl = xn - xh
    for k in range(ROW_TILES):
        xl_s[pl.ds(k, nt, stride=2 * SUBLANES), :] = xh[:, k * LANES:(k + 1) * LANES]
        xl_s[pl.ds(SUBLANES + k, nt, stride=2 * SUBLANES), :] = xl[:, k * LANES:(k + 1) * LANES]
    diag = (lax.broadcasted_iota(i32, (SUBLANES, G_ROWS), 1) % ROW_TILES
            == lax.broadcasted_iota(i32, (SUBLANES, G_ROWS), 0))

    def compute(t, g_s):
        lhs = xl_s[_tile_rows(t, 2 * SUBLANES), :].astype(bf16)
        out = lax.dot_general(lhs, g_s[...], (((1,), (1,)), ((), ())), preferred_element_type=f32)
        part = jnp.where(diag, out[:SUBLANES] + out[SUBLANES:], 0.0)
        for c in range(ROW_TILES):
            r_s[c, _tile_rows(t), :] = part[:, c * LANES:(c + 1) * LANES]

    _pipelined_tokens(nt, eidx_ref, tab_ref, g0_s, g1_s, compute)
    cols = []
    for c in range(ROW_TILES):
        acc = r_s[c, pl.ds(0, nt, stride=SUBLANES), :]
        for k in range(1, SUBLANES):
            acc = acc + r_s[c, pl.ds(k, nt, stride=SUBLANES), :]
        cols.append(acc)
    s = jnp.concatenate(cols, axis=-1)
    fold = (lax.broadcasted_iota(i32, (G_ROWS, P_SLOTS), 0) // ROW_TILES
            == lax.broadcasted_iota(i32, (G_ROWS, P_SLOTS), 1)).astype(bf16)
    sh, sl = _split_bf16(s)
    act = jnp.dot(sh, fold, preferred_element_type=f32) + jnp.dot(sl, fold, preferred_element_type=f32)
    w_ref[...] = g_ref[...] * _gelu_tanh(act)


def _peer_v_kernel(eidx_ref, w_ref, tab_ref, peer_ref, wl_s, o_s, g0_s, g1_s):
    nt = w_ref.shape[0]
    spread = (lax.broadcasted_iota(i32, (P_SLOTS, G_ROWS), 1) // ROW_TILES
              == lax.broadcasted_iota(i32, (P_SLOTS, G_ROWS), 0)).astype(bf16)
    wexp = jnp.dot(w_ref[...].astype(bf16), spread, preferred_element_type=f32)
    lane = lax.broadcasted_iota(i32, (nt, LANES), 1)
    for c in range(ROW_TILES):
        wc = wexp[:, c * LANES:(c + 1) * LANES]
        for k in range(SUBLANES):
            wl_s[c, pl.ds(k, nt, stride=SUBLANES), :] = jnp.where(lane % ROW_TILES == k, wc, 0.0)

    def compute(t, g_s):
        lhs = jnp.concatenate([wl_s[c, _tile_rows(t), :] for c in range(ROW_TILES)], axis=-1).astype(bf16)
        o_s[_tile_rows(t), :] = jnp.dot(lhs, g_s[...], preferred_element_type=f32)

    _pipelined_tokens(nt, eidx_ref, tab_ref, g0_s, g1_s, compute)
    for k in range(ROW_TILES):
        peer_ref[:, k * LANES:(k + 1) * LANES] = o_s[pl.ds(k, nt, stride=SUBLANES), :]


def _final_kernel(pa_ref, pb_ref, xa_ref, xb_ref, gf_ref, yp_ref, ys_ref, *, nba, nbp):
    y = _rms(_pick(nba, xa_ref, xb_ref) + _pick(nba, pa_ref, pb_ref), gf_ref[...])

    @pl.when(pl.program_id(0) < nbp)
    def _():
        yp_ref[...] = y

    @pl.when(pl.program_id(0) >= nbp)
    def _():
        ys_ref[...] = y


def _final(peer_a, peer_b, x1_a, x1_b, gf, n_p):
    n = x1_a.shape[0] + x1_b.shape[0]
    nba, nbp = x1_a.shape[0] // ROW_BLOCK, n_p // ROW_BLOCK
    g = gf.reshape(1, D_MODEL)
    return pl.pallas_call(
        functools.partial(_final_kernel, nba=nba, nbp=nbp),
        grid=(n // ROW_BLOCK,),
        in_specs=[*_pair_specs(nba, D_MODEL), *_pair_specs(nba, D_MODEL), pl.BlockSpec(g.shape, lambda i: (0, 0))],
        out_specs=list(_pair_specs(nbp, D_MODEL)),
        out_shape=[jax.ShapeDtypeStruct((n_p, D_MODEL), f32), jax.ShapeDtypeStruct((n - n_p, D_MODEL), f32)],
        compiler_params=pltpu.CompilerParams(dimension_semantics=("arbitrary",), vmem_limit_bytes=VMEM_LIMIT),
        name="final_norm",
    )(peer_a, peer_b, x1_a, x1_b, g)


def _peer_specs():
    row = lambda w: pl.BlockSpec((ROW_BLOCK, w), lambda i: (i, 0))
    idx = pl.BlockSpec((ROW_BLOCK, P_SLOTS), lambda i: (i, 0), memory_space=pltpu.SMEM)
    tab = pl.BlockSpec(memory_space=pltpu.VMEM)
    gscr = pltpu.VMEM((G_ROWS, LANES), bf16)
    params = pltpu.CompilerParams(dimension_semantics=("arbitrary",), vmem_limit_bytes=VMEM_LIMIT)
    return row, idx, tab, gscr, params


def _expert_table(tab):
    return tab.astype(bf16).reshape(tab.shape[0], ROW_TILES, LANES)


def _peer_u(eidx, xn, g, utab, n):
    row, idx, tab, gscr, params = _peer_specs()
    return pl.pallas_call(
        _peer_u_kernel,
        grid=(n // ROW_BLOCK,),
        in_specs=[idx, row(D_MODEL), row(P_SLOTS), tab],
        out_specs=row(P_SLOTS),
        out_shape=jax.ShapeDtypeStruct((n, P_SLOTS), f32),
        scratch_shapes=[pltpu.VMEM((ROW_BLOCK * 2 * SUBLANES, LANES), f32),
                        pltpu.VMEM((ROW_TILES, ROW_BLOCK * SUBLANES, LANES), f32), gscr, gscr],
        compiler_params=params,
        name="peer_u",
    )(eidx, xn, g, utab)


def _peer_v(eidx, w, vtab):
    n = w.shape[0]
    row, idx, tab, gscr, params = _peer_specs()
    return pl.pallas_call(
        _peer_v_kernel,
        grid=(n // ROW_BLOCK,),
        in_specs=[idx, row(P_SLOTS), tab],
        out_specs=row(D_MODEL),
        out_shape=jax.ShapeDtypeStruct((n, D_MODEL), f32),
        scratch_shapes=[pltpu.VMEM((ROW_TILES, ROW_BLOCK * SUBLANES, LANES), f32),
                        pltpu.VMEM((ROW_BLOCK * SUBLANES, LANES), f32), gscr, gscr],
        compiler_params=params,
        name="peer_v",
    )(eidx, w, vtab)


def _sc_table_kernel(t_ref, o_ref):
    t = t_ref[...]
    o_ref[...] = pltpu.pack_elementwise([t[:, :SC_WORDS], t[:, SC_WORDS:]], packed_dtype=bf16)


def _sc_table(tab):
    e = tab.shape[0]
    return pl.pallas_call(
        _sc_table_kernel,
        grid=(e // ROW_BLOCK,),
        in_specs=[pl.BlockSpec((ROW_BLOCK, D_MODEL), lambda i: (i, 0))],
        out_specs=pl.BlockSpec((ROW_BLOCK, SC_WORDS), lambda i: (i, 0)),
        out_shape=jax.ShapeDtypeStruct((e, SC_WORDS), jnp.uint32),
        compiler_params=pltpu.CompilerParams(dimension_semantics=("parallel",), vmem_limit_bytes=VMEM_LIMIT),
        name="sc_table",
    )(tab)


def _sc_unpack(words):
    return plsc.unpack(plsc.bitcast(words, bf16), format=plsc.PackFormat.INTERLEAVED)


def _sc_gelu_tanh(x):
    z = math.sqrt(2.0 / math.pi) * (x + 0.044715 * (x * x * x))
    return 0.5 * x * (2.0 - 2.0 / (jnp.exp(2.0 * z) + 1.0))


def _sc_peer(utab32, vtab32, eidx, xn, gate):
    n_sc = eidx.shape[0]
    per = n_sc // SC_WORKERS
    units = 4 * per
    assert n_sc % (SC_WORKERS * SUBLANES) == 0

    def body(u_hbm, v_hbm, eidx_hbm, x_hbm, g_hbm, out_hbm,
             idx_v, x_v, g_v, rows_v, acc_v, w_v, w16_v, out_v, row_sems, tok_sems, out_sem):
        base = (lax.axis_index("s") * 2 + lax.axis_index("c")) * per
        lanes = lax.broadcasted_iota(i32, (SC_LANES,), 0)

        def token_copies(tok):
            tslot = tok % 2
            return (pltpu.make_async_copy(eidx_hbm.at[base + tok], idx_v.at[tslot], tok_sems.at[0]),
                    pltpu.make_async_copy(x_hbm.at[base + tok], x_v.at[tslot], tok_sems.at[1]),
                    pltpu.make_async_copy(g_hbm.at[base + tok], g_v.at[tslot], tok_sems.at[2]))

        def store_out(tok):
            return pltpu.make_async_copy(out_v, out_hbm.at[base + tok], out_sem)

        def gather(tab_hbm, g):
            tok, k = g // 4, g % 4
            return pltpu.make_async_copy(tab_hbm.at[idx_v.at[tok % 2, pl.ds((k % 2) * SC_UNIT_ROWS, SC_UNIT_ROWS)]],
                                         rows_v.at[k % 2], row_sems.at[k % 2])

        def start(g):
            @pl.when(g % 4 < 2)
            def _():
                gather(u_hbm, g).start()

            @pl.when(g % 4 >= 2)
            def _():
                gather(v_hbm, g).start()

        def compute_u(tslot, half, rows):
            @pl.loop(0, SC_UNIT_ROWS // SC_ROW_GROUP)
            def _(rg):
                slot0 = half * SC_UNIT_ROWS + rg * SC_ROW_GROUP

                @pl.loop(0, SC_CHUNKS // SC_CHUNK_GROUP)
                def _(cg):
                    keep = jnp.where(cg > 0, 1.0, 0.0).astype(f32)
                    accs = [acc_v[pl.ds((slot0 + r) * SC_LANES, SC_LANES)] * keep for r in range(SC_ROW_GROUP)]
                    for c in range(SC_CHUNK_GROUP):
                        ch = cg * SC_CHUNK_GROUP + c
                        xa = x_v[tslot, pl.ds(ch * SC_LANES, SC_LANES)]
                        xb = x_v[tslot, pl.ds(SC_WORDS + ch * SC_LANES, SC_LANES)]
                        for r in range(SC_ROW_GROUP):
                            a, b = _sc_unpack(rows[rg * SC_ROW_GROUP + r, pl.ds(ch * SC_LANES, SC_LANES)])
                            accs[r] = accs[r] + a * xa + b * xb
                    for r in range(SC_ROW_GROUP):
                        acc_v[pl.ds((slot0 + r) * SC_LANES, SC_LANES)] = accs[r]

        def gate_weights(tslot):
            @pl.loop(0, P_SLOTS // SC_LANES)
            def _(sg):
                first = sg * SC_LANES * SC_LANES
                act = jnp.zeros((SC_LANES,), f32)
                for lane in range(SC_LANES):
                    act = act + plsc.load_gather(acc_v, [first + lanes * SC_LANES + lane])
                w_v[pl.ds(sg * SC_LANES, SC_LANES)] = g_v[tslot, pl.ds(sg * SC_LANES, SC_LANES)] * _sc_gelu_tanh(act)

            @pl.loop(0, P_SLOTS // SC_LANES)
            def _(sg):
                for r in range(SC_LANES):
                    w16_v[pl.ds((sg * SC_LANES + r) * SC_LANES, SC_LANES)] = plsc.load_gather(
                        w_v, [jnp.zeros((SC_LANES,), i32) + (sg * SC_LANES + r)])

        def compute_v(half, rows):
            @pl.loop(0, SC_CHUNKS // SC_CHUNK_GROUP)
            def _(cg):
                first = cg * SC_CHUNK_GROUP

                @pl.loop(0, SC_UNIT_ROWS // SC_ROW_GROUP)
                def _(rg):
                    keep = jnp.where(jnp.logical_or(half == 1, rg > 0), 1.0, 0.0).astype(f32)
                    spots = [pl.ds((j % 2) * SC_WORDS + (first + j // 2) * SC_LANES, SC_LANES)
                             for j in range(2 * SC_CHUNK_GROUP)]
                    accs = [out_v[spot] * keep for spot in spots]
                    for r in range(SC_ROW_GROUP):
                        row = rg * SC_ROW_GROUP + r
                        wv = w16_v[pl.ds((half * SC_UNIT_ROWS + row) * SC_LANES, SC_LANES)]
                        for c in range(SC_CHUNK_GROUP):
                            a, b = _sc_unpack(rows[row, pl.ds((first + c) * SC_LANES, SC_LANES)])
                            accs[2 * c] = accs[2 * c] + a * wv
                            accs[2 * c + 1] = accs[2 * c + 1] + b * wv
                    for spot, acc in zip(spots, accs):
                        out_v[spot] = acc

        for cp in token_copies(0):
            cp.start()
        for cp in token_copies(0):
            cp.wait()
        start(0)

        @pl.loop(0, units)
        def _(g):
            tok, k = g // 4, g % 4

            @pl.when(jnp.logical_and(k == 0, tok + 1 < per))
            def _():
                for cp in token_copies(tok + 1):
                    cp.start()

            @pl.when(jnp.logical_and(k == 3, tok + 1 < per))
            def _():
                for cp in token_copies(tok + 1):
                    cp.wait()

            @pl.when(g + 1 < units)
            def _():
                start(g + 1)

            gather(u_hbm, g).wait()
            rows = rows_v.at[k % 2]

            @pl.when(k < 2)
            def _():
                compute_u(tok % 2, k, rows)

            @pl.when(k == 1)
            def _():
                gate_weights(tok % 2)

            @pl.when(jnp.logical_and(k == 2, tok > 0))
            def _():
                store_out(tok - 1).wait()

            @pl.when(k >= 2)
            def _():
                compute_v(k - 2, rows)

            @pl.when(k == 3)
            def _():
                store_out(tok).start()

        store_out(per - 1).wait()

    return pl.kernel(
        body, mesh=plsc.VectorSubcoreMesh(core_axis_name="c", subcore_axis_name="s"),
        out_type=jax.ShapeDtypeStruct((n_sc, D_MODEL), f32),
        scratch_types=[pltpu.VMEM((2, P_SLOTS), i32), pltpu.VMEM((2, D_MODEL), f32), pltpu.VMEM((2, P_SLOTS), f32),
                       pltpu.VMEM((2, SC_UNIT_ROWS, SC_WORDS), jnp.uint32), pltpu.VMEM((P_SLOTS * SC_LANES,), f32),
                       pltpu.VMEM((P_SLOTS,), f32), pltpu.VMEM((P_SLOTS * SC_LANES,), f32), pltpu.VMEM((D_MODEL,), f32),
                       pltpu.SemaphoreType.DMA((2,)), pltpu.SemaphoreType.DMA((3,)), pltpu.SemaphoreType.DMA],
        compiler_params=pltpu.CompilerParams(needs_layout_passes=False),
        name="sc_peer",
    )(utab32, vtab32, eidx, xn, gate)


def kernel(x_prompt, x_sample, cache_k, cache_v, state_C, state_n, state_m, state_conv, norm1_g, w_in, b_gates, rel_bias, conv_w, conv_b, mh_norm_g, w_out, norm2_g, peer_wq, peer_keys, peer_u, peer_v, final_g):
    bp, sp, d = x_prompt.shape
    bs, ts, _ = x_sample.shape
    n_p, n_s = bp * sp, bs * ts
    n = n_p + n_s
    assert n_p % ROW_BLOCK == 0 and n_s % ROW_BLOCK == 0 and d == D_MODEL
    depth = w_in.shape[0]
    assert depth == 1, "the final norm is fused into the last layer's PEER pass"
    l = 0
    xp, xs = x_prompt.reshape(n_p, d), x_sample.reshape(n_s, d)

    aq, ak, av, mqk, mv, mo, gates = _inproj(xp, xs, norm1_g[l], w_in[l])
    zeros = lambda *shp: jnp.zeros(shp, f32)
    mparams = (conv_w[l], conv_b[l], b_gates[l], mh_norm_g[l])
    retrieval = (w_out[l], norm2_g[l], peer_wq[l], peer_keys[l])

    def mixers(seq, lo, hi, state):
        att = _attn_prompt(aq, ak, av, rel_bias[l], seq, sp, lo, hi)
        h, c, nn, mm = _mlstm(mqk, mv, mo, gates, seq * sp + lo, 1, hi - lo, *state, *mparams)
        conv_rows = mqk[seq * sp + hi - (CONV_W - 1):seq * sp + hi][None]
        return att, h, (c, nn[:, :M_HEADS], mm[:, :M_HEADS, 0], conv_rows)

    fresh = (zeros(1, M_HEADS, M_DH, M_DH), zeros(1, M_HEADS, M_DH), zeros(1, M_HEADS), zeros(1, CONV_W - 1, 2 * M_WIDTH))

    assert bp == 2 and 0 < SC_EXTRA_ROWS < sp
    n_sc = sp + SC_EXTRA_ROWS
    att_0, h_0, state_0 = mixers(0, 0, sp, fresh)
    att_1a, h_1a, state_1a = mixers(1, 0, SC_EXTRA_ROWS, fresh)
    x1_a, xn_a, eidx_a, gate_a = _mid(xp, 0, xp, sp, att_0, att_1a, h_0, h_1a, sp, SC_EXTRA_ROWS, *retrieval)
    peer_a = _sc_peer(_sc_table(peer_u[l]), _sc_table(peer_v[l]), eidx_a, xn_a, gate_a)

    att_1b, h_1b, state_1 = mixers(1, SC_EXTRA_ROWS, sp, state_1a)
    lcache = cache_k.shape[2]
    att_s = _attn_sample(aq, ak, av, cache_k[l].reshape(bs, lcache, A_WIDTH),
                         cache_v[l].reshape(bs, lcache, A_WIDTH), rel_bias[l], n_p, bs, ts)
    h_s, c_s, nn_s, mm_s = _mlstm(mqk, mv, mo, gates, n_p, bs, ts, state_C[l], state_n[l], state_m[l],
                                  state_conv[l], *mparams)
    x1_b, xn_b, eidx_b, gate_b = _mid(xp, n_sc, xs, 0, att_1b, att_s, h_1b, h_s, n_p - n_sc, n_s, *retrieval)
    w_b = _peer_u(eidx_b, xn_b, gate_b, _expert_table(peer_u[l]), n - n_sc)
    peer_b = _peer_v(eidx_b, w_b, _expert_table(peer_v[l]))
    y_p, y_s = _final(peer_a, peer_b, x1_a, x1_b, final_g, n_p)
    c_p, nn_p, mm_p = (jnp.concatenate(ab, axis=0) for ab in zip(state_0[:3], state_1[:3]))

    def tail(a, row0, bsz, t, keep):
        return jnp.stack([a[row0 + (b + 1) * t - keep:row0 + (b + 1) * t] for b in range(bsz)])

    keep = min(WINDOW, sp)
    heads = lambda a: a.reshape(a.shape[0], a.shape[1], A_HEADS, A_DH)
    ctail = CONV_W - 1
    conv_tail = lambda buf, a, row0, bsz, t: jnp.concatenate([buf.astype(a.dtype), tail(a, row0, bsz, t, min(ctail, t))],
                                                             axis=1)[:, -ctail:]
    st = lambda a: a[None]
    return (y_p.reshape(bp, sp, d), y_s.reshape(bs, ts, d),
            st(heads(tail(ak, 0, bp, sp, keep))), st(heads(tail(av, 0, bp, sp, keep))),
            st(c_p), st(nn_p), st(mm_p),
            st(conv_tail(zeros(bp, ctail, 2 * M_WIDTH), mqk, 0, bp, sp)),
            st(heads(ak[n_p:].reshape(bs, ts, A_WIDTH))), st(heads(av[n_p:].reshape(bs, ts, A_WIDTH))),
            st(c_s), st(nn_s[:, :M_HEADS]), st(mm_s[:, :M_HEADS, 0]),
            st(conv_tail(state_conv[l], mqk, n_p, bs, ts)))
```

```python
import functools
import math

import jax
import jax.numpy as jnp
from jax import lax
from jax.experimental import pallas as pl
from jax.experimental.pallas import tpu as pltpu
from jax.experimental.pallas import tpu_sc as plsc

f32 = jnp.float32
bf16 = jnp.bfloat16
i32 = jnp.int32

D_MODEL = 1024
CHUNK = 64
A_HEADS = 8
A_DH = 64
A_WIDTH = A_HEADS * A_DH
BAND_CHUNKS = 8
WINDOW = BAND_CHUNKS * CHUNK
MAX_REL = 128
ATT_SCALE = A_DH ** -0.5
M_HEADS = 4
M_DH = 128
M_WIDTH = M_HEADS * M_DH
CONV_W = 4
P_HEADS = 8
P_DKEY = 256
N_KEYS = 128
P_TOPK = 16
P_SLOTS = P_HEADS * P_TOPK
EPS = 1e-6
NEG = -1e30

LANES = 128
SUBLANES = 8
ROW_BLOCK = 256
ATT_TILE = 512
ATT_SUB = 128
ATT_KEYS = ATT_SUB + WINDOW
ROW_TILES = D_MODEL // LANES
G_ROWS = P_SLOTS * ROW_TILES
TOKEN_UNROLL = 8
VMEM_LIMIT = 56 * 1024 * 1024

SC_WORKERS = 32
SC_LANES = 16
SC_UNIT_ROWS = P_SLOTS // 2
SC_ROW_GROUP = 16
SC_WORDS = D_MODEL // 2
SC_CHUNKS = SC_WORDS // SC_LANES
SC_CHUNK_GROUP = 8
SC_FIRST_ROWS = 8192
SC_EXTRA_ROWS = 2048


def _rms(x, g):
    return x * lax.rsqrt(jnp.mean(x * x, axis=-1, keepdims=True) + EPS) * g


def _split_bf16(x):
    hi = x.astype(bf16)
    lo = (x - hi.astype(f32)).astype(bf16)
    return hi, lo


def _pair_specs(nbp, width, first=0, second=0):
    return (pl.BlockSpec((ROW_BLOCK, width), lambda i: (first + jnp.minimum(i, nbp - 1), 0)),
            pl.BlockSpec((ROW_BLOCK, width), lambda i: (second + jnp.maximum(i - nbp, 0), 0)))


def _pick(nbp, p_ref, s_ref):
    return jnp.where(pl.program_id(0) < nbp, p_ref[...], s_ref[...])


def _inproj_kernel(xp_ref, xs_ref, g_ref, w_ref, wgh_ref, wgl_ref,
                   aq_ref, ak_ref, av_ref, mqk_ref, mv_ref, mo_ref, gate_ref, *, nbp):
    xn = _rms(_pick(nbp, xp_ref, xs_ref), g_ref[...])
    xh, xl = _split_bf16(xn)

    def proj(lo, hi):
        return jnp.dot(xh, w_ref[:, lo:hi], preferred_element_type=f32)

    aq_ref[...] = proj(0, 512)
    ak_ref[...] = proj(512, 1024)
    av_ref[...] = proj(1024, 1536)
    mqk_ref[...] = proj(1536, 2560)
    mv_ref[...] = proj(2560, 3072)
    mo_ref[...] = proj(3072, 3584)
    gate_ref[...] = (jnp.dot(xh, wgh_ref[...], preferred_element_type=f32)
                     + jnp.dot(xl, wgh_ref[...], preferred_element_type=f32)
                     + jnp.dot(xh, wgl_ref[...], preferred_element_type=f32))


def _inproj(xp, xs, g1, w_in):
    n = xp.shape[0] + xs.shape[0]
    nbp = xp.shape[0] // ROW_BLOCK
    main = 3 * A_WIDTH + 4 * M_WIDTH
    w_main = w_in[:, :main].astype(bf16)
    wg = jnp.pad(w_in[:, main:], ((0, 0), (0, LANES - 2 * M_HEADS)))
    wgh, wgl = _split_bf16(wg)
    widths = (512, 512, 512, 1024, 512, 512, LANES)
    row = lambda w: pl.BlockSpec((ROW_BLOCK, w), lambda i: (i, 0))
    full = lambda a: pl.BlockSpec(a.shape, lambda i: (0,) * a.ndim)
    g = g1.reshape(1, D_MODEL)
    return pl.pallas_call(
        functools.partial(_inproj_kernel, nbp=nbp),
        grid=(n // ROW_BLOCK,),
        in_specs=[*_pair_specs(nbp, D_MODEL), full(g), full(w_main), full(wgh), full(wgl)],
        out_specs=[row(w) for w in widths],
        out_shape=[jax.ShapeDtypeStruct((n, w), f32) for w in widths],
        compiler_params=pltpu.CompilerParams(dimension_semantics=("parallel",), vmem_limit_bytes=VMEM_LIMIT),
        name="inproj",
    )(xp, xs, g, w_main, wgh, wgl)


def _attn_heads(q, k, v, bias_ref, key_ok):
    outs = []
    for h in range(A_HEADS):
        sl = slice(h * A_DH, (h + 1) * A_DH)
        s = lax.dot_general(q[:, sl], k[:, sl], (((1,), (1,)), ((), ())), preferred_element_type=f32)
        s = s * ATT_SCALE + bias_ref[h]
        if key_ok is not None:
            s = jnp.where(key_ok, s, NEG)
        m = jnp.max(s, axis=-1, keepdims=True)
        p = jnp.exp(s - m)
        l = jnp.sum(p, axis=-1, keepdims=True)
        o = jnp.dot(p.astype(bf16), v[:, sl], preferred_element_type=f32)
        outs.append(o / l)
    return jnp.concatenate(outs, axis=-1)


def _attn_prompt_kernel(q_ref, k0_ref, k1_ref, v0_ref, v1_ref, bias_ref, o_ref, *, t0):
    t = t0 + pl.program_id(1)
    q = q_ref[...].astype(bf16)
    k = jnp.concatenate([k0_ref[...], k1_ref[...]], axis=0).astype(bf16)
    v = jnp.concatenate([v0_ref[...], v1_ref[...]], axis=0).astype(bf16)
    col = lax.broadcasted_iota(i32, (1, ATT_KEYS), 1)
    for s in range(ATT_TILE // ATT_SUB):
        lo = s * ATT_SUB
        key_ok = (t * ATT_TILE + lo + col) >= WINDOW
        o_ref[lo:lo + ATT_SUB, :] = _attn_heads(q[lo:lo + ATT_SUB], k[lo:lo + ATT_KEYS], v[lo:lo + ATT_KEYS],
                                                 bias_ref, key_ok)


def _attn_sample_kernel(q_ref, k_ref, v_ref, bias_ref, o_ref):
    o_ref[...] = _attn_heads(q_ref[...].astype(bf16), k_ref[0].astype(bf16), v_ref[0].astype(bf16), bias_ref, None)


def _rel_bias_table(rel_bias, rows, cols, offset, valid):
    span = rows + cols - 1
    rel = offset + rows - 1 - jnp.arange(span)
    diag = rel_bias[:, jnp.clip(rel, -MAX_REL, MAX_REL) + MAX_REL].astype(f32)
    diag = jnp.pad(diag, ((0, 0), (0, 1)))
    flat = jnp.tile(diag, (1, rows))[:, rows - 1:rows - 1 + rows * span]
    return jnp.where(valid[None], flat.reshape(-1, rows, span)[:, :, :cols], NEG)


def _attn_prompt(q, k, v, rel_bias, seq, s, lo, hi):
    assert s % ATT_TILE == 0 and WINDOW == ATT_TILE and lo % ATT_TILE == 0 and hi % ATT_TILE == 0
    nt, t0, cnt = s // ATT_TILE, lo // ATT_TILE, (hi - lo) // ATT_TILE
    i = jnp.arange(ATT_SUB)[:, None]
    j = jnp.arange(ATT_KEYS)[None, :]
    off = j - (i // CHUNK) * CHUNK
    bias = _rel_bias_table(rel_bias, ATT_SUB, ATT_KEYS, WINDOW, (off >= 0) & (off < WINDOW + CHUNK))
    cur = pl.BlockSpec((ATT_TILE, A_WIDTH), lambda b, t: (seq * nt + t0 + t, 0))
    prev = pl.BlockSpec((ATT_TILE, A_WIDTH), lambda b, t: (seq * nt + jnp.maximum(t0 + t - 1, 0), 0))
    return pl.pallas_call(
        functools.partial(_attn_prompt_kernel, t0=t0),
        grid=(1, cnt),
        in_specs=[cur, prev, cur, prev, cur, pl.BlockSpec(bias.shape, lambda b, t: (0, 0, 0))],
        out_specs=pl.BlockSpec((ATT_TILE, A_WIDTH), lambda b, t: (t, 0)),
        out_shape=jax.ShapeDtypeStruct((hi - lo, A_WIDTH), f32),
        compiler_params=pltpu.CompilerParams(dimension_semantics=("parallel", "parallel"),
                                             vmem_limit_bytes=VMEM_LIMIT),
        name="attn_prompt",
    )(q, k, k, v, v, bias)


def _attn_sample(q, k, v, ck, cv, rel_bias, row0, bsz, t):
    l = ck.shape[1]
    assert row0 % t == 0
    keys = -(-(l + t) // LANES) * LANES
    padk = ((0, 0), (0, keys - l - t), (0, 0))
    kk = jnp.pad(jnp.concatenate([ck, k[row0:].reshape(bsz, t, A_WIDTH)], axis=1), padk)
    vv = jnp.pad(jnp.concatenate([cv, v[row0:].reshape(bsz, t, A_WIDTH)], axis=1), padk)
    j = jnp.arange(keys)[None, :]
    bias = _rel_bias_table(rel_bias, t, keys, l, jnp.broadcast_to(j < l + t, (t, keys)))
    return pl.pallas_call(
        _attn_sample_kernel,
        grid=(bsz,),
        in_specs=[pl.BlockSpec((t, A_WIDTH), lambda b: (row0 // t + b, 0)),
                  pl.BlockSpec((1, keys, A_WIDTH), lambda b: (b, 0, 0)),
                  pl.BlockSpec((1, keys, A_WIDTH), lambda b: (b, 0, 0)),
                  pl.BlockSpec(bias.shape, lambda b: (0, 0, 0))],
        out_specs=pl.BlockSpec((t, A_WIDTH), lambda b: (b, 0)),
        out_shape=jax.ShapeDtypeStruct((bsz * t, A_WIDTH), f32),
        compiler_params=pltpu.CompilerParams(dimension_semantics=("parallel",), vmem_limit_bytes=VMEM_LIMIT),
        name="attn_sample",
    )(q, kk, vv, bias)


def _mlstm_chunk(a, vall, o_in, gate_in, cw_ref, cb_ref, bg_ref, mhg_ref, c_s, n_s, m_s, prev_s):
    lc = a.shape[0]
    ext = jnp.concatenate([prev_s[...], a], axis=0)
    conv = cb_ref[...]
    for j in range(CONV_W):
        lo = SUBLANES - (CONV_W - 1) + j
        conv = conv + cw_ref[j:j + 1, :] * ext[lo:lo + lc]
    prev_s[...] = a[lc - SUBLANES:lc]
    qk = conv * jax.nn.sigmoid(conv)

    z = gate_in + bg_ref[...]
    lane = lax.broadcasted_iota(i32, (lc, LANES), 1)
    row = lax.broadcasted_iota(i32, (lc, LANES), 0)
    logf = jnp.minimum(z, 0.0) - jnp.log1p(jnp.exp(-jnp.abs(z)))
    cum = jnp.where((lane >= M_HEADS) & (lane < 2 * M_HEADS), logf, 0.0)
    shift = 1
    while shift < lc:
        cum = cum + jnp.where(row >= shift, pltpu.roll(cum, shift, axis=0), 0.0)
        shift *= 2
    zc = jnp.where(lane < M_HEADS, z, cum)
    zt = jnp.concatenate([zc, jnp.zeros((LANES - lc, LANES), f32)], axis=0).T[:, :lc]

    ri = lax.broadcasted_iota(i32, (lc, lc), 0)
    ci = lax.broadcasted_iota(i32, (lc, lc), 1)
    causal = ri >= ci
    state = [(m_s[h:h + 1, 0:1], c_s[h], n_s[h:h + 1, :]) for h in range(M_HEADS)]
    hs, new_state = [], []
    for h in range(M_HEADS):
        sl = slice(h * M_DH, (h + 1) * M_DH)
        q = qk[:, sl]
        k = qk[:, M_WIDTH + h * M_DH:M_WIDTH + (h + 1) * M_DH] * (M_DH ** -0.5)
        v = vall[:, sl]
        i_col = zc[:, h:h + 1]
        b_col = zc[:, M_HEADS + h:M_HEADS + h + 1]
        i_row = zt[h:h + 1, :]
        b_row = zt[M_HEADS + h:M_HEADS + h + 1, :]
        m_prev, c_prev, n_prev = state[h]

        dmat = jnp.where(causal, b_col - b_row + i_row, NEG)
        inter = b_col + m_prev
        mt = jnp.maximum(jnp.max(dmat, axis=-1, keepdims=True), inter)
        w_intra = jnp.exp(dmat - mt)
        w_inter = jnp.exp(inter - mt)
        qb, kb, vb = q.astype(bf16), k.astype(bf16), v.astype(bf16)
        s = lax.dot_general(qb, kb, (((1,), (1,)), ((), ())), preferred_element_type=f32) * w_intra
        num = (w_inter * jnp.dot(qb, c_prev.astype(bf16), preferred_element_type=f32)
               + jnp.dot(s.astype(bf16), vb, preferred_element_type=f32))
        den = w_inter * jnp.sum(q * n_prev, axis=-1, keepdims=True) + jnp.sum(s, axis=-1, keepdims=True)
        hh = num / jnp.maximum(jnp.abs(den), jnp.exp(-mt))
        m_new = mt[lc - 1:lc, :]
        b_last = b_col[lc - 1:lc, :]
        w_s = jnp.exp(b_last - b_col + i_col - m_new)
        decay = jnp.exp(b_last + m_prev - m_new)
        kw = k * w_s
        new_state.append((jnp.broadcast_to(m_new, (1, LANES)),
                          decay * c_prev + lax.dot_general(kw.astype(bf16), vb, (((0,), (0,)), ((), ())),
                                                           preferred_element_type=f32),
                          decay * n_prev + jnp.sum(kw, axis=0, keepdims=True)))
        hs.append(hh * lax.rsqrt(jnp.mean(hh * hh, axis=-1, keepdims=True) + EPS))

    for h, (m_new, c_new, n_new) in enumerate(new_state):
        m_s[h:h + 1, :] = m_new
        c_s[h] = c_new
        n_s[h:h + 1, :] = n_new
    return jnp.concatenate(hs, axis=-1) * mhg_ref[...] * jax.nn.sigmoid(o_in)


def _mlstm_kernel(qk_ref, v_ref, o_ref, gate_ref, c0_ref, n0_ref, m0_ref, cbuf_ref,
                  cw_ref, cb_ref, bg_ref, mhg_ref,
                  h_ref, cout_ref, nout_ref, mout_ref,
                  c_s, n_s, m_s, prev_s):
    c = pl.program_id(1)

    @pl.when(c == 0)
    def _():
        c_s[...] = c0_ref[0]
        n_s[...] = n0_ref[0]
        m_s[...] = m0_ref[0]
        prev_s[...] = cbuf_ref[0]

    h_ref[...] = _mlstm_chunk(qk_ref[...], v_ref[...], o_ref[...], gate_ref[...],
                              cw_ref, cb_ref, bg_ref, mhg_ref, c_s, n_s, m_s, prev_s)

    @pl.when(c == pl.num_programs(1) - 1)
    def _():
        cout_ref[0] = c_s[...]
        nout_ref[0] = n_s[...]
        mout_ref[0] = m_s[...]


def _mlstm(mqk, mv, mo, gates, row0, bsz, t, c0, n0, m0, cbuf, conv_w, conv_b, b_gates, mh_g):
    lc = min(CHUNK, t)
    step = lc
    nc = t // step
    assert t % step == 0 and lc % SUBLANES == 0 and row0 % step == 0
    n0p = jnp.pad(n0.astype(f32), ((0, 0), (0, SUBLANES - M_HEADS), (0, 0)))
    m0p = jnp.pad(jnp.broadcast_to(m0.astype(f32)[:, :, None], (bsz, M_HEADS, LANES)),
                  ((0, 0), (0, SUBLANES - M_HEADS), (0, 0)))
    cbp = jnp.pad(cbuf.astype(f32), ((0, 0), (SUBLANES - (CONV_W - 1), 0), (0, 0)))
    bg = jnp.pad(b_gates.astype(f32), (0, LANES - 2 * M_HEADS)).reshape(1, LANES)
    seq = lambda w: pl.BlockSpec((step, w), lambda b, c: (row0 // step + b * nc + c, 0))
    out_seq = pl.BlockSpec((step, M_WIDTH), lambda b, c: (b * nc + c, 0))
    per_b = lambda shp: pl.BlockSpec((1,) + shp, lambda b, c: (b,) + (0,) * len(shp))
    full = lambda a: pl.BlockSpec(a.shape, lambda b, c: (0,) * a.ndim)
    cb = conv_b.reshape(1, -1)
    mhg = mh_g.reshape(1, -1)
    return pl.pallas_call(
        _mlstm_kernel,
        grid=(bsz, nc),
        in_specs=[seq(2 * M_WIDTH), seq(M_WIDTH), seq(M_WIDTH), seq(LANES),
                  per_b((M_HEADS, M_DH, M_DH)), per_b((SUBLANES, M_DH)), per_b((SUBLANES, LANES)),
                  per_b((SUBLANES, 2 * M_WIDTH)),
                  full(conv_w), full(cb), full(bg), full(mhg)],
        out_specs=[out_seq, per_b((M_HEADS, M_DH, M_DH)), per_b((SUBLANES, M_DH)), per_b((SUBLANES, LANES))],
        out_shape=[jax.ShapeDtypeStruct((bsz * t, M_WIDTH), f32),
                   jax.ShapeDtypeStruct((bsz, M_HEADS, M_DH, M_DH), f32),
                   jax.ShapeDtypeStruct((bsz, SUBLANES, M_DH), f32),
                   jax.ShapeDtypeStruct((bsz, SUBLANES, LANES), f32)],
        scratch_shapes=[pltpu.VMEM((M_HEADS, M_DH, M_DH), f32), pltpu.VMEM((SUBLANES, M_DH), f32),
                        pltpu.VMEM((SUBLANES, LANES), f32), pltpu.VMEM((SUBLANES, 2 * M_WIDTH), f32)],
        compiler_params=pltpu.CompilerParams(dimension_semantics=("parallel", "arbitrary"),
                                             vmem_limit_bytes=VMEM_LIMIT),
        name="mlstm",
    )(mqk, mv, mo, gates, c0.astype(f32), n0p, m0p, cbp, conv_w, cb, bg, mhg)


def _topk_rows(s, k, payload=None):
    n = s.shape[0]
    rows = lax.broadcasted_iota(i32, s.shape, 0).astype(f32)
    vals, ids = [], []
    for _ in range(k):
        m = jnp.max(s, axis=0, keepdims=True)
        pos = jnp.min(jnp.where(s == m, rows, float(n)), axis=0, keepdims=True)
        sel = rows == pos
        vals.append(m)
        ids.append(pos if payload is None else jnp.max(jnp.where(sel, payload, -1.0), axis=0, keepdims=True))
        s = jnp.where(sel, -jnp.inf, s)
    return jnp.concatenate(vals, axis=0), jnp.concatenate(ids, axis=0)


def _mid_kernel(xp_ref, xs_ref, attp_ref, atts_ref, hp_ref, hs_ref, wo_ref, g2_ref, wq_ref, keys_ref,
                x1_ref, xn_ref, eidx_ref, gate_ref, *, nbp):
    cat = jnp.concatenate([_pick(nbp, attp_ref, atts_ref), _pick(nbp, hp_ref, hs_ref)], axis=-1).astype(bf16)
    x1 = _pick(nbp, xp_ref, xs_ref) + jnp.dot(cat, wo_ref[...], preferred_element_type=f32)
    x1_ref[...] = x1
    xn = _rms(x1, g2_ref[...])
    xn_ref[...] = xn
    xb = xn.astype(bf16)
    e_rows, g_rows = [], []
    for h in range(P_HEADS):
        q = jnp.dot(xb, wq_ref[:, h * P_DKEY:(h + 1) * P_DKEY], preferred_element_type=f32).astype(bf16)
        half = []
        for p in range(2):
            st = lax.dot_general(keys_ref[h, p], q[:, p * N_KEYS:(p + 1) * N_KEYS], (((1,), (1,)), ((), ())),
                                 preferred_element_type=f32)
            half.append(_topk_rows(st, P_TOPK))
        (v0, i0), (v1, i1) = half
        width = [P_TOPK // (a + 1) for a in range(P_TOPK)]
        fill = -sum(width) % SUBLANES
        cand = jnp.concatenate([v0[a:a + 1] + v1[:width[a]] for a in range(P_TOPK)]
                               + [jnp.full((fill, v0.shape[1]), -jnp.inf, f32)], axis=0)
        cidx = jnp.concatenate([i0[a:a + 1] * float(N_KEYS) + i1[:width[a]] for a in range(P_TOPK)]
                               + [jnp.zeros((fill, v0.shape[1]), f32)], axis=0)
        top_s, eid = _topk_rows(cand, P_TOPK, payload=cidx)
        ex = jnp.exp(top_s - top_s[0:1])
        e_rows.append(eid)
        g_rows.append(ex / jnp.sum(ex, axis=0, keepdims=True))
    eidx_ref[...] = jnp.concatenate(e_rows, axis=0).T.astype(i32)
    gate_ref[...] = jnp.concatenate(g_rows, axis=0).T


def _mid(xp, xp_row0, xs, xs_row0, attp, atts, hp, hs, n_a, n_b, w_out, g2, wq, keys):
    n = n_a + n_b
    nbp = n_a // ROW_BLOCK
    assert n_a % ROW_BLOCK == 0 and n_b % ROW_BLOCK == 0 and xp_row0 % ROW_BLOCK == 0 and xs_row0 % ROW_BLOCK == 0
    row = lambda w: pl.BlockSpec((ROW_BLOCK, w), lambda i: (i, 0))
    full = lambda a: pl.BlockSpec(a.shape, lambda i: (0,) * a.ndim)
    wo = w_out.astype(bf16)
    wqb = wq.astype(bf16)
    kb = keys.astype(bf16)
    g = g2.reshape(1, D_MODEL)
    return pl.pallas_call(
        functools.partial(_mid_kernel, nbp=nbp),
        grid=(n // ROW_BLOCK,),
        in_specs=[*_pair_specs(nbp, D_MODEL, xp_row0 // ROW_BLOCK, xs_row0 // ROW_BLOCK), *_pair_specs(nbp, A_WIDTH),
                  *_pair_specs(nbp, M_WIDTH), full(wo), full(g), full(wqb), full(kb)],
        out_specs=[row(D_MODEL), row(D_MODEL), row(P_SLOTS), row(P_SLOTS)],
        out_shape=[jax.ShapeDtypeStruct((n, D_MODEL), f32), jax.ShapeDtypeStruct((n, D_MODEL), f32),
                   jax.ShapeDtypeStruct((n, P_SLOTS), i32), jax.ShapeDtypeStruct((n, P_SLOTS), f32)],
        compiler_params=pltpu.CompilerParams(dimension_semantics=("parallel",), vmem_limit_bytes=VMEM_LIMIT),
        name="outproj_retrieve",
    )(xp, xs, attp, atts, hp, hs, wo, g, wqb, kb)


def _gather_rows(eidx_ref, tab_ref, g_ref, t):
    for r in range(P_SLOTS):
        g_ref[pl.ds(r * ROW_TILES, ROW_TILES), :] = tab_ref[eidx_ref[t, r]]


def _tile_rows(t, rows=SUBLANES):
    return pl.ds(pl.multiple_of(t * rows, rows), rows)


def _pipelined_tokens(nt, eidx_ref, tab_ref, g0_s, g1_s, compute):
    bufs = (g0_s, g1_s)
    _gather_rows(eidx_ref, tab_ref, g0_s, 0)

    def body(j, carry):
        for u in range(TOKEN_UNROLL):
            t = TOKEN_UNROLL * j + u
            _gather_rows(eidx_ref, tab_ref, bufs[(u + 1) % 2], jnp.minimum(t + 1, nt - 1))
            compute(t, bufs[u % 2])
        return carry

    lax.fori_loop(0, nt // TOKEN_UNROLL, body, 0)


def _gelu_tanh(x):
    return 0.5 * x * (1.0 + jnp.tanh(math.sqrt(2.0 / math.pi) * (x + 0.044715 * (x * x * x))))


def _peer_u_kernel(eidx_ref, xn_ref, g_ref, tab_ref, w_ref, xl_s, r_s, g0_s, g1_s):
    nt = xn_ref.shape[0]
    xn = xn_ref[...]
    xh = xn.astype(bf16).astype(f32)
    xl = xn - xh
    for k in range(ROW_TILES):
        xl_s[pl.ds(k, nt, stride=2 * SUBLANES), :] = xh[:, k * LANES:(k + 1) * LANES]
        xl_s[pl.ds(SUBLANES + k, nt, stride=2 * SUBLANES), :] = xl[:, k * LANES:(k + 1) * LANES]
    diag = (lax.broadcasted_iota(i32, (SUBLANES, G_ROWS), 1) % ROW_TILES
            == lax.broadcasted_iota(i32, (SUBLANES, G_ROWS), 0))

    def compute(t, g_s):
        lhs = xl_s[_tile_rows(t, 2 * SUBLANES), :].astype(bf16)
        out = lax.dot_general(lhs, g_s[...], (((1,), (1,)), ((), ())), preferred_element_type=f32)
        part = jnp.where(diag, out[:SUBLANES] + out[SUBLANES:], 0.0)
        for c in range(ROW_TILES):
            r_s[c, _tile_rows(t), :] = part[:, c * LANES:(c + 1) * LANES]

    _pipelined_tokens(nt, eidx_ref, tab_ref, g0_s, g1_s, compute)
    cols = []
    for c in range(ROW_TILES):
        acc = r_s[c, pl.ds(0, nt, stride=SUBLANES), :]
        for k in range(1, SUBLANES):
            acc = acc + r_s[c, pl.ds(k, nt, stride=SUBLANES), :]
        cols.append(acc)
    s = jnp.concatenate(cols, axis=-1)
    fold = (lax.broadcasted_iota(i32, (G_ROWS, P_SLOTS), 0) // ROW_TILES
            == lax.broadcasted_iota(i32, (G_ROWS, P_SLOTS), 1)).astype(bf16)
    sh, sl = _split_bf16(s)
    act = jnp.dot(sh, fold, preferred_element_type=f32) + jnp.dot(sl, fold, preferred_element_type=f32)
    w_ref[...] = g_ref[...] * _gelu_tanh(act)


def _peer_v_kernel(eidx_ref, w_ref, tab_ref, peer_ref, wl_s, o_s, g0_s, g1_s):
    nt = w_ref.shape[0]
    spread = (lax.broadcasted_iota(i32, (P_SLOTS, G_ROWS), 1) // ROW_TILES
              == lax.broadcasted_iota(i32, (P_SLOTS, G_ROWS), 0)).astype(bf16)
    wexp = jnp.dot(w_ref[...].astype(bf16), spread, preferred_element_type=f32)
    lane = lax.broadcasted_iota(i32, (nt, LANES), 1)
    for c in range(ROW_TILES):
        wc = wexp[:, c * LANES:(c + 1) * LANES]
        for k in range(SUBLANES):
            wl_s[c, pl.ds(k, nt, stride=SUBLANES), :] = jnp.where(lane % ROW_TILES == k, wc, 0.0)

    def compute(t, g_s):
        lhs = jnp.concatenate([wl_s[c, _tile_rows(t), :] for c in range(ROW_TILES)], axis=-1).astype(bf16)
        o_s[_tile_rows(t), :] = jnp.dot(lhs, g_s[...], preferred_element_type=f32)

    _pipelined_tokens(nt, eidx_ref, tab_ref, g0_s, g1_s, compute)
    for k in range(ROW_TILES):
        peer_ref[:, k * LANES:(k + 1) * LANES] = o_s[pl.ds(k, nt, stride=SUBLANES), :]


def _final_kernel(*refs, starts, nbp):
    k = len(starts)
    peer_refs, x_refs, (gf_ref, yp_ref, ys_ref) = refs[:k], refs[k:2 * k], refs[2 * k:]
    resid = x_refs[0][...] + peer_refs[0][...]
    for start, p_ref, x_ref in zip(starts[1:], peer_refs[1:], x_refs[1:]):
        resid = jnp.where(pl.program_id(0) >= start, x_ref[...] + p_ref[...], resid)
    y = _rms(resid, gf_ref[...])

    @pl.when(pl.program_id(0) < nbp)
    def _():
        yp_ref[...] = y

    @pl.when(pl.program_id(0) >= nbp)
    def _():
        ys_ref[...] = y


def _final(peers, x1s, gf, n_p):
    counts = [x.shape[0] // ROW_BLOCK for x in x1s]
    starts = [sum(counts[:k]) for k in range(len(counts))]
    n, nbp = sum(counts) * ROW_BLOCK, n_p // ROW_BLOCK
    g = gf.reshape(1, D_MODEL)
    seg = [pl.BlockSpec((ROW_BLOCK, D_MODEL), lambda i, s=s, c=c: (jnp.clip(i - s, 0, c - 1), 0))
           for s, c in zip(starts, counts)]
    return pl.pallas_call(
        functools.partial(_final_kernel, starts=tuple(starts), nbp=nbp),
        grid=(n // ROW_BLOCK,),
        in_specs=[*seg, *seg, pl.BlockSpec(g.shape, lambda i: (0, 0))],
        out_specs=list(_pair_specs(nbp, D_MODEL)),
        out_shape=[jax.ShapeDtypeStruct((n_p, D_MODEL), f32), jax.ShapeDtypeStruct((n - n_p, D_MODEL), f32)],
        compiler_params=pltpu.CompilerParams(dimension_semantics=("arbitrary",), vmem_limit_bytes=VMEM_LIMIT),
        name="final_norm",
    )(*peers, *x1s, g)


def _peer_specs():
    row = lambda w: pl.BlockSpec((ROW_BLOCK, w), lambda i: (i, 0))
    idx = pl.BlockSpec((ROW_BLOCK, P_SLOTS), lambda i: (i, 0), memory_space=pltpu.SMEM)
    tab = pl.BlockSpec(memory_space=pltpu.VMEM)
    gscr = pltpu.VMEM((G_ROWS, LANES), bf16)
    params = pltpu.CompilerParams(dimension_semantics=("arbitrary",), vmem_limit_bytes=VMEM_LIMIT)
    return row, idx, tab, gscr, params


def _expert_table(tab):
    return tab.astype(bf16).reshape(tab.shape[0], ROW_TILES, LANES)


def _peer_u(eidx, xn, g, utab, n):
    row, idx, tab, gscr, params = _peer_specs()
    return pl.pallas_call(
        _peer_u_kernel,
        grid=(n // ROW_BLOCK,),
        in_specs=[idx, row(D_MODEL), row(P_SLOTS), tab],
        out_specs=row(P_SLOTS),
        out_shape=jax.ShapeDtypeStruct((n, P_SLOTS), f32),
        scratch_shapes=[pltpu.VMEM((ROW_BLOCK * 2 * SUBLANES, LANES), f32),
                        pltpu.VMEM((ROW_TILES, ROW_BLOCK * SUBLANES, LANES), f32), gscr, gscr],
        compiler_params=params,
        name="peer_u",
    )(eidx, xn, g, utab)


def _peer_v(eidx, w, vtab):
    n = w.shape[0]
    row, idx, tab, gscr, params = _peer_specs()
    return pl.pallas_call(
        _peer_v_kernel,
        grid=(n // ROW_BLOCK,),
        in_specs=[idx, row(P_SLOTS), tab],
        out_specs=row(D_MODEL),
        out_shape=jax.ShapeDtypeStruct((n, D_MODEL), f32),
        scratch_shapes=[pltpu.VMEM((ROW_TILES, ROW_BLOCK * SUBLANES, LANES), f32),
                        pltpu.VMEM((ROW_BLOCK * SUBLANES, LANES), f32), gscr, gscr],
        compiler_params=params,
        name="peer_v",
    )(eidx, w, vtab)


def _sc_table_kernel(t_ref, o_ref):
    t = t_ref[...]
    o_ref[...] = pltpu.pack_elementwise([t[:, :SC_WORDS], t[:, SC_WORDS:]], packed_dtype=bf16)


def _sc_table(tab):
    e = tab.shape[0]
    return pl.pallas_call(
        _sc_table_kernel,
        grid=(e // ROW_BLOCK,),
        in_specs=[pl.BlockSpec((ROW_BLOCK, D_MODEL), lambda i: (i, 0))],
        out_specs=pl.BlockSpec((ROW_BLOCK, SC_WORDS), lambda i: (i, 0)),
        out_shape=jax.ShapeDtypeStruct((e, SC_WORDS), jnp.uint32),
        compiler_params=pltpu.CompilerParams(dimension_semantics=("parallel",), vmem_limit_bytes=VMEM_LIMIT),
        name="sc_table",
    )(tab)


def _sc_unpack(words):
    return plsc.unpack(plsc.bitcast(words, bf16), format=plsc.PackFormat.INTERLEAVED)


def _sc_gelu_tanh(x):
    z = math.sqrt(2.0 / math.pi) * (x + 0.044715 * (x * x * x))
    return 0.5 * x * (2.0 - 2.0 / (jnp.exp(2.0 * z) + 1.0))


def _sc_peer(utab32, vtab32, eidx, xn, gate):
    n_sc = eidx.shape[0]
    per = n_sc // SC_WORKERS
    units = 4 * per
    assert n_sc % (SC_WORKERS * SUBLANES) == 0

    def body(u_hbm, v_hbm, eidx_hbm, x_hbm, g_hbm, out_hbm,
             idx_v, x_v, g_v, rows_v, acc_v, w_v, w16_v, out_v, row_sems, tok_sems, out_sem):
        base = (lax.axis_index("s") * 2 + lax.axis_index("c")) * per
        lanes = lax.broadcasted_iota(i32, (SC_LANES,), 0)

        def token_copies(tok):
            tslot = tok % 2
            return (pltpu.make_async_copy(eidx_hbm.at[base + tok], idx_v.at[tslot], tok_sems.at[0]),
                    pltpu.make_async_copy(x_hbm.at[base + tok], x_v.at[tslot], tok_sems.at[1]),
                    pltpu.make_async_copy(g_hbm.at[base + tok], g_v.at[tslot], tok_sems.at[2]))

        def store_out(tok):
            return pltpu.make_async_copy(out_v, out_hbm.at[base + tok], out_sem)

        def gather(tab_hbm, g):
            tok, k = g // 4, g % 4
            return pltpu.make_async_copy(tab_hbm.at[idx_v.at[tok % 2, pl.ds((k % 2) * SC_UNIT_ROWS, SC_UNIT_ROWS)]],
                                         rows_v.at[k % 2], row_sems.at[k % 2])

        def start(g):
            @pl.when(g % 4 < 2)
            def _():
                gather(u_hbm, g).start()

            @pl.when(g % 4 >= 2)
            def _():
                gather(v_hbm, g).start()

        def compute_u(tslot, half, rows):
            @pl.loop(0, SC_UNIT_ROWS // SC_ROW_GROUP)
            def _(rg):
                slot0 = half * SC_UNIT_ROWS + rg * SC_ROW_GROUP

                @pl.loop(0, SC_CHUNKS // SC_CHUNK_GROUP)
                def _(cg):
                    keep = jnp.where(cg > 0, 1.0, 0.0).astype(f32)
                    accs = [acc_v[pl.ds((slot0 + r) * SC_LANES, SC_LANES)] * keep for r in range(SC_ROW_GROUP)]
                    for c in range(SC_CHUNK_GROUP):
                        ch = cg * SC_CHUNK_GROUP + c
                        xa = x_v[tslot, pl.ds(ch * SC_LANES, SC_LANES)]
                        xb = x_v[tslot, pl.ds(SC_WORDS + ch * SC_LANES, SC_LANES)]
                        for r in range(SC_ROW_GROUP):
                            a, b = _sc_unpack(rows[rg * SC_ROW_GROUP + r, pl.ds(ch * SC_LANES, SC_LANES)])
                            accs[r] = accs[r] + a * xa + b * xb
                    for r in range(SC_ROW_GROUP):
                        acc_v[pl.ds((slot0 + r) * SC_LANES, SC_LANES)] = accs[r]

        def gate_weights(tslot):
            @pl.loop(0, P_SLOTS // SC_LANES)
            def _(sg):
                first = sg * SC_LANES * SC_LANES
                act = jnp.zeros((SC_LANES,), f32)
                for lane in range(SC_LANES):
                    act = act + plsc.load_gather(acc_v, [first + lanes * SC_LANES + lane])
                w_v[pl.ds(sg * SC_LANES, SC_LANES)] = g_v[tslot, pl.ds(sg * SC_LANES, SC_LANES)] * _sc_gelu_tanh(act)

            @pl.loop(0, P_SLOTS // SC_LANES)
            def _(sg):
                for r in range(SC_LANES):
                    w16_v[pl.ds((sg * SC_LANES + r) * SC_LANES, SC_LANES)] = plsc.load_gather(
                        w_v, [jnp.zeros((SC_LANES,), i32) + (sg * SC_LANES + r)])

        def compute_v(half, rows):
            @pl.loop(0, SC_CHUNKS // SC_CHUNK_GROUP)
            def _(cg):
                first = cg * SC_CHUNK_GROUP

                @pl.loop(0, SC_UNIT_ROWS // SC_ROW_GROUP)
                def _(rg):
                    keep = jnp.where(jnp.logical_or(half == 1, rg > 0), 1.0, 0.0).astype(f32)
                    spots = [pl.ds((j % 2) * SC_WORDS + (first + j // 2) * SC_LANES, SC_LANES)
                             for j in range(2 * SC_CHUNK_GROUP)]
                    accs = [out_v[spot] * keep for spot in spots]
                    for r in range(SC_ROW_GROUP):
                        row = rg * SC_ROW_GROUP + r
                        wv = w16_v[pl.ds((half * SC_UNIT_ROWS + row) * SC_LANES, SC_LANES)]
                        for c in range(SC_CHUNK_GROUP):
                            a, b = _sc_unpack(rows[row, pl.ds((first + c) * SC_LANES, SC_LANES)])
                            accs[2 * c] = accs[2 * c] + a * wv
                            accs[2 * c + 1] = accs[2 * c + 1] + b * wv
                    for spot, acc in zip(spots, accs):
                        out_v[spot] = acc

        for cp in token_copies(0):
            cp.start()
        for cp in token_copies(0):
            cp.wait()
        start(0)

        @pl.loop(0, units)
        def _(g):
            tok, k = g // 4, g % 4

            @pl.when(jnp.logical_and(k == 0, tok + 1 < per))
            def _():
                for cp in token_copies(tok + 1):
                    cp.start()

            @pl.when(jnp.logical_and(k == 3, tok + 1 < per))
            def _():
                for cp in token_copies(tok + 1):
                    cp.wait()

            @pl.when(g + 1 < units)
            def _():
                start(g + 1)

            gather(u_hbm, g).wait()
            rows = rows_v.at[k % 2]

            @pl.when(k < 2)
            def _():
                compute_u(tok % 2, k, rows)

            @pl.when(k == 1)
            def _():
                gate_weights(tok % 2)

            @pl.when(jnp.logical_and(k == 2, tok > 0))
            def _():
                store_out(tok - 1).wait()

            @pl.when(k >= 2)
            def _():
                compute_v(k - 2, rows)

            @pl.when(k == 3)
            def _():
                store_out(tok).start()

        store_out(per - 1).wait()

    return pl.kernel(
        body, mesh=plsc.VectorSubcoreMesh(core_axis_name="c", subcore_axis_name="s"),
        out_type=jax.ShapeDtypeStruct((n_sc, D_MODEL), f32),
        scratch_types=[pltpu.VMEM((2, P_SLOTS), i32), pltpu.VMEM((2, D_MODEL), f32), pltpu.VMEM((2, P_SLOTS), f32),
                       pltpu.VMEM((2, SC_UNIT_ROWS, SC_WORDS), jnp.uint32), pltpu.VMEM((P_SLOTS * SC_LANES,), f32),
                       pltpu.VMEM((P_SLOTS,), f32), pltpu.VMEM((P_SLOTS * SC_LANES,), f32), pltpu.VMEM((D_MODEL,), f32),
                       pltpu.SemaphoreType.DMA((2,)), pltpu.SemaphoreType.DMA((3,)), pltpu.SemaphoreType.DMA],
        compiler_params=pltpu.CompilerParams(needs_layout_passes=False),
        name="sc_peer",
    )(utab32, vtab32, eidx, xn, gate)


def kernel(x_prompt, x_sample, cache_k, cache_v, state_C, state_n, state_m, state_conv, norm1_g, w_in, b_gates, rel_bias, conv_w, conv_b, mh_norm_g, w_out, norm2_g, peer_wq, peer_keys, peer_u, peer_v, final_g):
    bp, sp, d = x_prompt.shape
    bs, ts, _ = x_sample.shape
    n_p, n_s = bp * sp, bs * ts
    n = n_p + n_s
    assert n_p % ROW_BLOCK == 0 and n_s % ROW_BLOCK == 0 and d == D_MODEL
    depth = w_in.shape[0]
    assert depth == 1, "the final norm is fused into the last layer's PEER pass"
    l = 0
    xp, xs = x_prompt.reshape(n_p, d), x_sample.reshape(n_s, d)

    aq, ak, av, mqk, mv, mo, gates = _inproj(xp, xs, norm1_g[l], w_in[l])
    zeros = lambda *shp: jnp.zeros(shp, f32)
    mparams = (conv_w[l], conv_b[l], b_gates[l], mh_norm_g[l])
    retrieval = (w_out[l], norm2_g[l], peer_wq[l], peer_keys[l])

    def mixers(seq, lo, hi, state):
        att = _attn_prompt(aq, ak, av, rel_bias[l], seq, sp, lo, hi)
        h, c, nn, mm = _mlstm(mqk, mv, mo, gates, seq * sp + lo, 1, hi - lo, *state, *mparams)
        conv_rows = mqk[seq * sp + hi - (CONV_W - 1):seq * sp + hi][None]
        return att, h, (c, nn[:, :M_HEADS], mm[:, :M_HEADS, 0], conv_rows)

    fresh = (zeros(1, M_HEADS, M_DH, M_DH), zeros(1, M_HEADS, M_DH), zeros(1, M_HEADS), zeros(1, CONV_W - 1, 2 * M_WIDTH))

    assert bp == 2 and 0 < SC_FIRST_ROWS < sp and 0 < SC_EXTRA_ROWS < sp
    n_sc = sp + SC_EXTRA_ROWS
    sc_tables = _sc_table(peer_u[l]), _sc_table(peer_v[l])
    att_0a, h_0a, state_0a = mixers(0, 0, SC_FIRST_ROWS, fresh)
    x1_a, xn_a, eidx_a, gate_a = _mid(xp, 0, xp, 0, att_0a, att_0a, h_0a, h_0a, SC_FIRST_ROWS, 0, *retrieval)
    peer_a = _sc_peer(*sc_tables, eidx_a, xn_a, gate_a)

    att_0b, h_0b, state_0 = mixers(0, SC_FIRST_ROWS, sp, state_0a)
    att_1a, h_1a, state_1a = mixers(1, 0, SC_EXTRA_ROWS, fresh)
    x1_c, xn_c, eidx_c, gate_c = _mid(xp, SC_FIRST_ROWS, xp, sp, att_0b, att_1a, h_0b, h_1a,
                                      sp - SC_FIRST_ROWS, SC_EXTRA_ROWS, *retrieval)
    peer_c = _sc_peer(*sc_tables, eidx_c, xn_c, gate_c)

    att_1b, h_1b, state_1 = mixers(1, SC_EXTRA_ROWS, sp, state_1a)
    lcache = cache_k.shape[2]
    att_s = _attn_sample(aq, ak, av, cache_k[l].reshape(bs, lcache, A_WIDTH),
                         cache_v[l].reshape(bs, lcache, A_WIDTH), rel_bias[l], n_p, bs, ts)
    h_s, c_s, nn_s, mm_s = _mlstm(mqk, mv, mo, gates, n_p, bs, ts, state_C[l], state_n[l], state_m[l],
                                  state_conv[l], *mparams)
    x1_b, xn_b, eidx_b, gate_b = _mid(xp, n_sc, xs, 0, att_1b, att_s, h_1b, h_s, n_p - n_sc, n_s, *retrieval)
    w_b = _peer_u(eidx_b, xn_b, gate_b, _expert_table(peer_u[l]), n - n_sc)
    peer_b = _peer_v(eidx_b, w_b, _expert_table(peer_v[l]))
    y_p, y_s = _final((peer_a, peer_c, peer_b), (x1_a, x1_c, x1_b), final_g, n_p)
    c_p, nn_p, mm_p = (jnp.concatenate(ab, axis=0) for ab in zip(state_0[:3], state_1[:3]))

    def tail(a, row0, bsz, t, keep):
        return jnp.stack([a[row0 + (b + 1) * t - keep:row0 + (b + 1) * t] for b in range(bsz)])

    keep = min(WINDOW, sp)
    heads = lambda a: a.reshape(a.shape[0], a.shape[1], A_HEADS, A_DH)
    ctail = CONV_W - 1
    conv_tail = lambda buf, a, row0, bsz, t: jnp.concatenate([buf.astype(a.dtype), tail(a, row0, bsz, t, min(ctail, t))],
                                                             axis=1)[:, -ctail:]
    st = lambda a: a[None]
    return (y_p.reshape(bp, sp, d), y_s.reshape(bs, ts, d),
            st(heads(tail(ak, 0, bp, sp, keep))), st(heads(tail(av, 0, bp, sp, keep))),
            st(c_p), st(nn_p), st(mm_p),
            st(conv_tail(zeros(bp, ctail, 2 * M_WIDTH), mqk, 0, bp, sp)),
            st(heads(ak[n_p:].reshape(bs, ts, A_WIDTH))), st(heads(av[n_p:].reshape(bs, ts, A_WIDTH))),
            st(c_s), st(nn_s[:, :M_HEADS]), st(mm_s[:, :M_HEADS, 0]),
            st(conv_tail(state_conv[l], mqk, n_p, bs, ts)))
```

```python
import functools
import math

import jax
import jax.numpy as jnp
from jax import lax
from jax.experimental import pallas as pl
from jax.experimental.pallas import tpu as pltpu
from jax.experimental.pallas import tpu_sc as plsc

f32 = jnp.float32
bf16 = jnp.bfloat16
i32 = jnp.int32

D_MODEL = 1024
CHUNK = 64
A_HEADS = 8
A_DH = 64
A_WIDTH = A_HEADS * A_DH
BAND_CHUNKS = 8
WINDOW = BAND_CHUNKS * CHUNK
MAX_REL = 128
ATT_SCALE = A_DH ** -0.5
M_HEADS = 4
M_DH = 128
M_WIDTH = M_HEADS * M_DH
CONV_W = 4
P_HEADS = 8
P_DKEY = 256
N_KEYS = 128
P_TOPK = 16
P_SLOTS = P_HEADS * P_TOPK
EPS = 1e-6
NEG = -1e30

LANES = 128
SUBLANES = 8
ROW_BLOCK = 256
ATT_TILE = 512
ATT_SUB = 128
ATT_KEYS = ATT_SUB + WINDOW
ROW_TILES = D_MODEL // LANES
G_ROWS = P_SLOTS * ROW_TILES
TOKEN_UNROLL = 8
VMEM_LIMIT = 56 * 1024 * 1024

SC_WORKERS = 32
SC_LANES = 16
SC_UNIT_ROWS = P_SLOTS // 2
SC_ROW_GROUP = 16
SC_WORDS = D_MODEL // 2
SC_CHUNKS = SC_WORDS // SC_LANES
SC_CHUNK_GROUP = 8
SC_FIRST_ROWS = 4096
SC_EXTRA_ROWS = 2560


def _rms(x, g):
    return x * lax.rsqrt(jnp.mean(x * x, axis=-1, keepdims=True) + EPS) * g


def _split_bf16(x):
    hi = x.astype(bf16)
    lo = (x - hi.astype(f32)).astype(bf16)
    return hi, lo


def _pair_specs(nbp, width, first=0, second=0):
    return (pl.BlockSpec((ROW_BLOCK, width), lambda i: (first + jnp.minimum(i, nbp - 1), 0)),
            pl.BlockSpec((ROW_BLOCK, width), lambda i: (second + jnp.maximum(i - nbp, 0), 0)))


def _pick(nbp, p_ref, s_ref):
    return jnp.where(pl.program_id(0) < nbp, p_ref[...], s_ref[...])


def _inproj_kernel(xp_ref, xs_ref, g_ref, w_ref, wgh_ref, wgl_ref,
                   aq_ref, ak_ref, av_ref, mqk_ref, mv_ref, mo_ref, gate_ref, *, nbp):
    xn = _rms(_pick(nbp, xp_ref, xs_ref), g_ref[...])
    xh, xl = _split_bf16(xn)

    def proj(lo, hi):
        return jnp.dot(xh, w_ref[:, lo:hi], preferred_element_type=f32)

    aq_ref[...] = proj(0, 512)
    ak_ref[...] = proj(512, 1024)
    av_ref[...] = proj(1024, 1536)
    mqk_ref[...] = proj(1536, 2560)
    mv_ref[...] = proj(2560, 3072)
    mo_ref[...] = proj(3072, 3584)
    gate_ref[...] = (jnp.dot(xh, wgh_ref[...], preferred_element_type=f32)
                     + jnp.dot(xl, wgh_ref[...], preferred_element_type=f32)
                     + jnp.dot(xh, wgl_ref[...], preferred_element_type=f32))


def _inproj(xp, xs, g1, w_in):
    n = xp.shape[0] + xs.shape[0]
    nbp = xp.shape[0] // ROW_BLOCK
    main = 3 * A_WIDTH + 4 * M_WIDTH
    w_main = w_in[:, :main].astype(bf16)
    wg = jnp.pad(w_in[:, main:], ((0, 0), (0, LANES - 2 * M_HEADS)))
    wgh, wgl = _split_bf16(wg)
    widths = (512, 512, 512, 1024, 512, 512, LANES)
    row = lambda w: pl.BlockSpec((ROW_BLOCK, w), lambda i: (i, 0))
    full = lambda a: pl.BlockSpec(a.shape, lambda i: (0,) * a.ndim)
    g = g1.reshape(1, D_MODEL)
    return pl.pallas_call(
        functools.partial(_inproj_kernel, nbp=nbp),
        grid=(n // ROW_BLOCK,),
        in_specs=[*_pair_specs(nbp, D_MODEL), full(g), full(w_main), full(wgh), full(wgl)],
        out_specs=[row(w) for w in widths],
        out_shape=[jax.ShapeDtypeStruct((n, w), f32) for w in widths],
        compiler_params=pltpu.CompilerParams(dimension_semantics=("parallel",), vmem_limit_bytes=VMEM_LIMIT),
        name="inproj",
    )(xp, xs, g, w_main, wgh, wgl)


def _attn_heads(q, k, v, bias_ref, key_ok):
    outs = []
    for h in range(A_HEADS):
        sl = slice(h * A_DH, (h + 1) * A_DH)
        s = lax.dot_general(q[:, sl], k[:, sl], (((1,), (1,)), ((), ())), preferred_element_type=f32)
        s = s * ATT_SCALE + bias_ref[h]
        if key_ok is not None:
            s = jnp.where(key_ok, s, NEG)
        m = jnp.max(s, axis=-1, keepdims=True)
        p = jnp.exp(s - m)
        l = jnp.sum(p, axis=-1, keepdims=True)
        o = jnp.dot(p.astype(bf16), v[:, sl], preferred_element_type=f32)
        outs.append(o / l)
    return jnp.concatenate(outs, axis=-1)


def _attn_prompt_kernel(q_ref, k0_ref, k1_ref, v0_ref, v1_ref, bias_ref, o_ref, *, t0):
    t = t0 + pl.program_id(1)
    q = q_ref[...].astype(bf16)
    k = jnp.concatenate([k0_ref[...], k1_ref[...]], axis=0).astype(bf16)
    v = jnp.concatenate([v0_ref[...], v1_ref[...]], axis=0).astype(bf16)
    col = lax.broadcasted_iota(i32, (1, ATT_KEYS), 1)
    for s in range(ATT_TILE // ATT_SUB):
        lo = s * ATT_SUB
        key_ok = (t * ATT_TILE + lo + col) >= WINDOW
        o_ref[lo:lo + ATT_SUB, :] = _attn_heads(q[lo:lo + ATT_SUB], k[lo:lo + ATT_KEYS], v[lo:lo + ATT_KEYS],
                                                 bias_ref, key_ok)


def _attn_sample_kernel(q_ref, k_ref, v_ref, bias_ref, o_ref):
    o_ref[...] = _attn_heads(q_ref[...].astype(bf16), k_ref[0].astype(bf16), v_ref[0].astype(bf16), bias_ref, None)


def _rel_bias_table(rel_bias, rows, cols, offset, valid):
    span = rows + cols - 1
    rel = offset + rows - 1 - jnp.arange(span)
    diag = rel_bias[:, jnp.clip(rel, -MAX_REL, MAX_REL) + MAX_REL].astype(f32)
    diag = jnp.pad(diag, ((0, 0), (0, 1)))
    flat = jnp.tile(diag, (1, rows))[:, rows - 1:rows - 1 + rows * span]
    return jnp.where(valid[None], flat.reshape(-1, rows, span)[:, :, :cols], NEG)


def _attn_prompt(q, k, v, rel_bias, seq, s, lo, hi):
    assert s % ATT_TILE == 0 and WINDOW == ATT_TILE and lo % ATT_TILE == 0 and hi % ATT_TILE == 0
    nt, t0, cnt = s // ATT_TILE, lo // ATT_TILE, (hi - lo) // ATT_TILE
    i = jnp.arange(ATT_SUB)[:, None]
    j = jnp.arange(ATT_KEYS)[None, :]
    off = j - (i // CHUNK) * CHUNK
    bias = _rel_bias_table(rel_bias, ATT_SUB, ATT_KEYS, WINDOW, (off >= 0) & (off < WINDOW + CHUNK))
    cur = pl.BlockSpec((ATT_TILE, A_WIDTH), lambda b, t: (seq * nt + t0 + t, 0))
    prev = pl.BlockSpec((ATT_TILE, A_WIDTH), lambda b, t: (seq * nt + jnp.maximum(t0 + t - 1, 0), 0))
    return pl.pallas_call(
        functools.partial(_attn_prompt_kernel, t0=t0),
        grid=(1, cnt),
        in_specs=[cur, prev, cur, prev, cur, pl.BlockSpec(bias.shape, lambda b, t: (0, 0, 0))],
        out_specs=pl.BlockSpec((ATT_TILE, A_WIDTH), lambda b, t: (t, 0)),
        out_shape=jax.ShapeDtypeStruct((hi - lo, A_WIDTH), f32),
        compiler_params=pltpu.CompilerParams(dimension_semantics=("parallel", "parallel"),
                                             vmem_limit_bytes=VMEM_LIMIT),
        name="attn_prompt",
    )(q, k, k, v, v, bias)


def _attn_sample(q, k, v, ck, cv, rel_bias, row0, bsz, t):
    l = ck.shape[1]
    assert row0 % t == 0
    keys = -(-(l + t) // LANES) * LANES
    padk = ((0, 0), (0, keys - l - t), (0, 0))
    kk = jnp.pad(jnp.concatenate([ck, k[row0:].reshape(bsz, t, A_WIDTH)], axis=1), padk)
    vv = jnp.pad(jnp.concatenate([cv, v[row0:].reshape(bsz, t, A_WIDTH)], axis=1), padk)
    j = jnp.arange(keys)[None, :]
    bias = _rel_bias_table(rel_bias, t, keys, l, jnp.broadcast_to(j < l + t, (t, keys)))
    return pl.pallas_call(
        _attn_sample_kernel,
        grid=(bsz,),
        in_specs=[pl.BlockSpec((t, A_WIDTH), lambda b: (row0 // t + b, 0)),
                  pl.BlockSpec((1, keys, A_WIDTH), lambda b: (b, 0, 0)),
                  pl.BlockSpec((1, keys, A_WIDTH), lambda b: (b, 0, 0)),
                  pl.BlockSpec(bias.shape, lambda b: (0, 0, 0))],
        out_specs=pl.BlockSpec((t, A_WIDTH), lambda b: (b, 0)),
        out_shape=jax.ShapeDtypeStruct((bsz * t, A_WIDTH), f32),
        compiler_params=pltpu.CompilerParams(dimension_semantics=("parallel",), vmem_limit_bytes=VMEM_LIMIT),
        name="attn_sample",
    )(q, kk, vv, bias)


def _mlstm_chunk(a, vall, o_in, gate_in, cw_ref, cb_ref, bg_ref, mhg_ref, c_s, n_s, m_s, prev_s):
    lc = a.shape[0]
    ext = jnp.concatenate([prev_s[...], a], axis=0)
    conv = cb_ref[...]
    for j in range(CONV_W):
        lo = SUBLANES - (CONV_W - 1) + j
        conv = conv + cw_ref[j:j + 1, :] * ext[lo:lo + lc]
    prev_s[...] = a[lc - SUBLANES:lc]
    qk = conv * jax.nn.sigmoid(conv)

    z = gate_in + bg_ref[...]
    lane = lax.broadcasted_iota(i32, (lc, LANES), 1)
    row = lax.broadcasted_iota(i32, (lc, LANES), 0)
    logf = jnp.minimum(z, 0.0) - jnp.log1p(jnp.exp(-jnp.abs(z)))
    cum = jnp.where((lane >= M_HEADS) & (lane < 2 * M_HEADS), logf, 0.0)
    shift = 1
    while shift < lc:
        cum = cum + jnp.where(row >= shift, pltpu.roll(cum, shift, axis=0), 0.0)
        shift *= 2
    zc = jnp.where(lane < M_HEADS, z, cum)
    zt = jnp.concatenate([zc, jnp.zeros((LANES - lc, LANES), f32)], axis=0).T[:, :lc]

    ri = lax.broadcasted_iota(i32, (lc, lc), 0)
    ci = lax.broadcasted_iota(i32, (lc, lc), 1)
    causal = ri >= ci
    state = [(m_s[h:h + 1, 0:1], c_s[h], n_s[h:h + 1, :]) for h in range(M_HEADS)]
    hs, new_state = [], []
    for h in range(M_HEADS):
        sl = slice(h * M_DH, (h + 1) * M_DH)
        q = qk[:, sl]
        k = qk[:, M_WIDTH + h * M_DH:M_WIDTH + (h + 1) * M_DH] * (M_DH ** -0.5)
        v = vall[:, sl]
        i_col = zc[:, h:h + 1]
        b_col = zc[:, M_HEADS + h:M_HEADS + h + 1]
        i_row = zt[h:h + 1, :]
        b_row = zt[M_HEADS + h:M_HEADS + h + 1, :]
        m_prev, c_prev, n_prev = state[h]

        dmat = jnp.where(causal, b_col - b_row + i_row, NEG)
        inter = b_col + m_prev
        mt = jnp.maximum(jnp.max(dmat, axis=-1, keepdims=True), inter)
        w_intra = jnp.exp(dmat - mt)
        w_inter = jnp.exp(inter - mt)
        qb, kb, vb = q.astype(bf16), k.astype(bf16), v.astype(bf16)
        s = lax.dot_general(qb, kb, (((1,), (1,)), ((), ())), preferred_element_type=f32) * w_intra
        num = (w_inter * jnp.dot(qb, c_prev.astype(bf16), preferred_element_type=f32)
               + jnp.dot(s.astype(bf16), vb, preferred_element_type=f32))
        den = w_inter * jnp.sum(q * n_prev, axis=-1, keepdims=True) + jnp.sum(s, axis=-1, keepdims=True)
        hh = num / jnp.maximum(jnp.abs(den), jnp.exp(-mt))
        m_new = mt[lc - 1:lc, :]
        b_last = b_col[lc - 1:lc, :]
        w_s = jnp.exp(b_last - b_col + i_col - m_new)
        decay = jnp.exp(b_last + m_prev - m_new)
        kw = k * w_s
        new_state.append((jnp.broadcast_to(m_new, (1, LANES)),
                          decay * c_prev + lax.dot_general(kw.astype(bf16), vb, (((0,), (0,)), ((), ())),
                                                           preferred_element_type=f32),
                          decay * n_prev + jnp.sum(kw, axis=0, keepdims=True)))
        hs.append(hh * lax.rsqrt(jnp.mean(hh * hh, axis=-1, keepdims=True) + EPS))

    for h, (m_new, c_new, n_new) in enumerate(new_state):
        m_s[h:h + 1, :] = m_new
        c_s[h] = c_new
        n_s[h:h + 1, :] = n_new
    return jnp.concatenate(hs, axis=-1) * mhg_ref[...] * jax.nn.sigmoid(o_in)


def _mlstm_kernel(qk_ref, v_ref, o_ref, gate_ref, c0_ref, n0_ref, m0_ref, cbuf_ref,
                  cw_ref, cb_ref, bg_ref, mhg_ref,
                  h_ref, cout_ref, nout_ref, mout_ref,
                  c_s, n_s, m_s, prev_s):
    c = pl.program_id(1)

    @pl.when(c == 0)
    def _():
        c_s[...] = c0_ref[0]
        n_s[...] = n0_ref[0]
        m_s[...] = m0_ref[0]
        prev_s[...] = cbuf_ref[0]

    h_ref[...] = _mlstm_chunk(qk_ref[...], v_ref[...], o_ref[...], gate_ref[...],
                              cw_ref, cb_ref, bg_ref, mhg_ref, c_s, n_s, m_s, prev_s)

    @pl.when(c == pl.num_programs(1) - 1)
    def _():
        cout_ref[0] = c_s[...]
        nout_ref[0] = n_s[...]
        mout_ref[0] = m_s[...]


def _mlstm(mqk, mv, mo, gates, row0, bsz, t, c0, n0, m0, cbuf, conv_w, conv_b, b_gates, mh_g):
    lc = min(CHUNK, t)
    step = lc
    nc = t // step
    assert t % step == 0 and lc % SUBLANES == 0 and row0 % step == 0
    n0p = jnp.pad(n0.astype(f32), ((0, 0), (0, SUBLANES - M_HEADS), (0, 0)))
    m0p = jnp.pad(jnp.broadcast_to(m0.astype(f32)[:, :, None], (bsz, M_HEADS, LANES)),
                  ((0, 0), (0, SUBLANES - M_HEADS), (0, 0)))
    cbp = jnp.pad(cbuf.astype(f32), ((0, 0), (SUBLANES - (CONV_W - 1), 0), (0, 0)))
    bg = jnp.pad(b_gates.astype(f32), (0, LANES - 2 * M_HEADS)).reshape(1, LANES)
    seq = lambda w: pl.BlockSpec((step, w), lambda b, c: (row0 // step + b * nc + c, 0))
    out_seq = pl.BlockSpec((step, M_WIDTH), lambda b, c: (b * nc + c, 0))
    per_b = lambda shp: pl.BlockSpec((1,) + shp, lambda b, c: (b,) + (0,) * len(shp))
    full = lambda a: pl.BlockSpec(a.shape, lambda b, c: (0,) * a.ndim)
    cb = conv_b.reshape(1, -1)
    mhg = mh_g.reshape(1, -1)
    return pl.pallas_call(
        _mlstm_kernel,
        grid=(bsz, nc),
        in_specs=[seq(2 * M_WIDTH), seq(M_WIDTH), seq(M_WIDTH), seq(LANES),
                  per_b((M_HEADS, M_DH, M_DH)), per_b((SUBLANES, M_DH)), per_b((SUBLANES, LANES)),
                  per_b((SUBLANES, 2 * M_WIDTH)),
                  full(conv_w), full(cb), full(bg), full(mhg)],
        out_specs=[out_seq, per_b((M_HEADS, M_DH, M_DH)), per_b((SUBLANES, M_DH)), per_b((SUBLANES, LANES))],
        out_shape=[jax.ShapeDtypeStruct((bsz * t, M_WIDTH), f32),
                   jax.ShapeDtypeStruct((bsz, M_HEADS, M_DH, M_DH), f32),
                   jax.ShapeDtypeStruct((bsz, SUBLANES, M_DH), f32),
                   jax.ShapeDtypeStruct((bsz, SUBLANES, LANES), f32)],
        scratch_shapes=[pltpu.VMEM((M_HEADS, M_DH, M_DH), f32), pltpu.VMEM((SUBLANES, M_DH), f32),
                        pltpu.VMEM((SUBLANES, LANES), f32), pltpu.VMEM((SUBLANES, 2 * M_WIDTH), f32)],
        compiler_params=pltpu.CompilerParams(dimension_semantics=("parallel", "arbitrary"),
                                             vmem_limit_bytes=VMEM_LIMIT),
        name="mlstm",
    )(mqk, mv, mo, gates, c0.astype(f32), n0p, m0p, cbp, conv_w, cb, bg, mhg)


def _topk_rows(s, k, payload=None):
    n = s.shape[0]
    rows = lax.broadcasted_iota(i32, s.shape, 0).astype(f32)
    vals, ids = [], []
    for _ in range(k):
        m = jnp.max(s, axis=0, keepdims=True)
        pos = jnp.min(jnp.where(s == m, rows, float(n)), axis=0, keepdims=True)
        sel = rows == pos
        vals.append(m)
        ids.append(pos if payload is None else jnp.max(jnp.where(sel, payload, -1.0), axis=0, keepdims=True))
        s = jnp.where(sel, -jnp.inf, s)
    return jnp.concatenate(vals, axis=0), jnp.concatenate(ids, axis=0)


def _mid_kernel(xp_ref, xs_ref, attp_ref, atts_ref, hp_ref, hs_ref, wo_ref, g2_ref, wq_ref, keys_ref,
                x1_ref, xn_ref, eidx_ref, gate_ref, *, nbp):
    cat = jnp.concatenate([_pick(nbp, attp_ref, atts_ref), _pick(nbp, hp_ref, hs_ref)], axis=-1).astype(bf16)
    x1 = _pick(nbp, xp_ref, xs_ref) + jnp.dot(cat, wo_ref[...], preferred_element_type=f32)
    x1_ref[...] = x1
    xn = _rms(x1, g2_ref[...])
    xn_ref[...] = xn
    xb = xn.astype(bf16)
    e_rows, g_rows = [], []
    for h in range(P_HEADS):
        q = jnp.dot(xb, wq_ref[:, h * P_DKEY:(h + 1) * P_DKEY], preferred_element_type=f32).astype(bf16)
        half = []
        for p in range(2):
            st = lax.dot_general(keys_ref[h, p], q[:, p * N_KEYS:(p + 1) * N_KEYS], (((1,), (1,)), ((), ())),
                                 preferred_element_type=f32)
            half.append(_topk_rows(st, P_TOPK))
        (v0, i0), (v1, i1) = half
        width = [P_TOPK // (a + 1) for a in range(P_TOPK)]
        fill = -sum(width) % SUBLANES
        cand = jnp.concatenate([v0[a:a + 1] + v1[:width[a]] for a in range(P_TOPK)]
                               + [jnp.full((fill, v0.shape[1]), -jnp.inf, f32)], axis=0)
        cidx = jnp.concatenate([i0[a:a + 1] * float(N_KEYS) + i1[:width[a]] for a in range(P_TOPK)]
                               + [jnp.zeros((fill, v0.shape[1]), f32)], axis=0)
        top_s, eid = _topk_rows(cand, P_TOPK, payload=cidx)
        ex = jnp.exp(top_s - top_s[0:1])
        e_rows.append(eid)
        g_rows.append(ex / jnp.sum(ex, axis=0, keepdims=True))
    eidx_ref[...] = jnp.concatenate(e_rows, axis=0).T.astype(i32)
    gate_ref[...] = jnp.concatenate(g_rows, axis=0).T


def _mid(xp, xp_row0, xs, xs_row0, attp, atts, hp, hs, n_a, n_b, w_out, g2, wq, keys):
    n = n_a + n_b
    nbp = n_a // ROW_BLOCK
    assert n_a % ROW_BLOCK == 0 and n_b % ROW_BLOCK == 0 and xp_row0 % ROW_BLOCK == 0 and xs_row0 % ROW_BLOCK == 0
    row = lambda w: pl.BlockSpec((ROW_BLOCK, w), lambda i: (i, 0))
    full = lambda a: pl.BlockSpec(a.shape, lambda i: (0,) * a.ndim)
    wo = w_out.astype(bf16)
    wqb = wq.astype(bf16)
    kb = keys.astype(bf16)
    g = g2.reshape(1, D_MODEL)
    return pl.pallas_call(
        functools.partial(_mid_kernel, nbp=nbp),
        grid=(n // ROW_BLOCK,),
        in_specs=[*_pair_specs(nbp, D_MODEL, xp_row0 // ROW_BLOCK, xs_row0 // ROW_BLOCK), *_pair_specs(nbp, A_WIDTH),
                  *_pair_specs(nbp, M_WIDTH), full(wo), full(g), full(wqb), full(kb)],
        out_specs=[row(D_MODEL), row(D_MODEL), row(P_SLOTS), row(P_SLOTS)],
        out_shape=[jax.ShapeDtypeStruct((n, D_MODEL), f32), jax.ShapeDtypeStruct((n, D_MODEL), f32),
                   jax.ShapeDtypeStruct((n, P_SLOTS), i32), jax.ShapeDtypeStruct((n, P_SLOTS), f32)],
        compiler_params=pltpu.CompilerParams(dimension_semantics=("parallel",), vmem_limit_bytes=VMEM_LIMIT),
        name="outproj_retrieve",
    )(xp, xs, attp, atts, hp, hs, wo, g, wqb, kb)


def _gather_rows(eidx_ref, tab_ref, g_ref, t):
    for r in range(P_SLOTS):
        g_ref[pl.ds(r * ROW_TILES, ROW_TILES), :] = tab_ref[eidx_ref[t, r]]


def _tile_rows(t, rows=SUBLANES):
    return pl.ds(pl.multiple_of(t * rows, rows), rows)


def _pipelined_tokens(nt, eidx_ref, tab_ref, g0_s, g1_s, compute):
    bufs = (g0_s, g1_s)
    _gather_rows(eidx_ref, tab_ref, g0_s, 0)

    def body(j, carry):
        for u in range(TOKEN_UNROLL):
            t = TOKEN_UNROLL * j + u
            _gather_rows(eidx_ref, tab_ref, bufs[(u + 1) % 2], jnp.minimum(t + 1, nt - 1))
            compute(t, bufs[u % 2])
        return carry

    lax.fori_loop(0, nt // TOKEN_UNROLL, body, 0)


def _gelu_tanh(x):
    return 0.5 * x * (1.0 + jnp.tanh(math.sqrt(2.0 / math.pi) * (x + 0.044715 * (x * x * x))))


def _peer_u_kernel(eidx_ref, xn_ref, g_ref, tab_ref, w_ref, xl_s, r_s, g0_s, g1_s):
    nt = xn_ref.shape[0]
    xn = xn_ref[...]
    xh = xn.astype(bf16).astype(f32)
    xl = xn - xh
    for k in range(ROW_TILES):
        xl_s[pl.ds(k, nt, stride=2 * SUBLANES), :] = xh[:, k * LANES:(k + 1) * LANES]
        xl_s[pl.ds(SUBLANES + k, nt, stride=2 * SUBLANES), :] = xl[:, k * LANES:(k + 1) * LANES]
    diag = (lax.broadcasted_iota(i32, (SUBLANES, G_ROWS), 1) % ROW_TILES
            == lax.broadcasted_iota(i32, (SUBLANES, G_ROWS), 0))

    def compute(t, g_s):
        lhs = xl_s[_tile_rows(t, 2 * SUBLANES), :].astype(bf16)
        out = lax.dot_general(lhs, g_s[...], (((1,), (1,)), ((), ())), preferred_element_type=f32)
        part = jnp.where(diag, out[:SUBLANES] + out[SUBLANES:], 0.0)
        for c in range(ROW_TILES):
            r_s[c, _tile_rows(t), :] = part[:, c * LANES:(c + 1) * LANES]

    _pipelined_tokens(nt, eidx_ref, tab_ref, g0_s, g1_s, compute)
    cols = []
    for c in range(ROW_TILES):
        acc = r_s[c, pl.ds(0, nt, stride=SUBLANES), :]
        for k in range(1, SUBLANES):
            acc = acc + r_s[c, pl.ds(k, nt, stride=SUBLANES), :]
        cols.append(acc)
    s = jnp.concatenate(cols, axis=-1)
    fold = (lax.broadcasted_iota(i32, (G_ROWS, P_SLOTS), 0) // ROW_TILES
            == lax.broadcasted_iota(i32, (G_ROWS, P_SLOTS), 1)).astype(bf16)
    sh, sl = _split_bf16(s)
    act = jnp.dot(sh, fold, preferred_element_type=f32) + jnp.dot(sl, fold, preferred_element_type=f32)
    w_ref[...] = g_ref[...] * _gelu_tanh(act)


def _peer_v_kernel(eidx_ref, w_ref, tab_ref, peer_ref, wl_s, o_s, g0_s, g1_s):
    nt = w_ref.shape[0]
    spread = (lax.broadcasted_iota(i32, (P_SLOTS, G_ROWS), 1) // ROW_TILES
              == lax.broadcasted_iota(i32, (P_SLOTS, G_ROWS), 0)).astype(bf16)
    wexp = jnp.dot(w_ref[...].astype(bf16), spread, preferred_element_type=f32)
    lane = lax.broadcasted_iota(i32, (nt, LANES), 1)
    for c in range(ROW_TILES):
        wc = wexp[:, c * LANES:(c + 1) * LANES]
        for k in range(SUBLANES):
            wl_s[c, pl.ds(k, nt, stride=SUBLANES), :] = jnp.where(lane % ROW_TILES == k, wc, 0.0)

    def compute(t, g_s):
        lhs = jnp.concatenate([wl_s[c, _tile_rows(t), :] for c in range(ROW_TILES)], axis=-1).astype(bf16)
        o_s[_tile_rows(t), :] = jnp.dot(lhs, g_s[...], preferred_element_type=f32)

    _pipelined_tokens(nt, eidx_ref, tab_ref, g0_s, g1_s, compute)
    for k in range(ROW_TILES):
        peer_ref[:, k * LANES:(k + 1) * LANES] = o_s[pl.ds(k, nt, stride=SUBLANES), :]


def _final_kernel(*refs, starts, nbp):
    k = len(starts)
    peer_refs, x_refs, (gf_ref, yp_ref, ys_ref) = refs[:k], refs[k:2 * k], refs[2 * k:]
    resid = x_refs[0][...] + peer_refs[0][...]
    for start, p_ref, x_ref in zip(starts[1:], peer_refs[1:], x_refs[1:]):
        resid = jnp.where(pl.program_id(0) >= start, x_ref[...] + p_ref[...], resid)
    y = _rms(resid, gf_ref[...])

    @pl.when(pl.program_id(0) < nbp)
    def _():
        yp_ref[...] = y

    @pl.when(pl.program_id(0) >= nbp)
    def _():
        ys_ref[...] = y


def _final(peers, x1s, gf, n_p):
    counts = [x.shape[0] // ROW_BLOCK for x in x1s]
    starts = [sum(counts[:k]) for k in range(len(counts))]
    n, nbp = sum(counts) * ROW_BLOCK, n_p // ROW_BLOCK
    g = gf.reshape(1, D_MODEL)
    seg = [pl.BlockSpec((ROW_BLOCK, D_MODEL), lambda i, s=s, c=c: (jnp.clip(i - s, 0, c - 1), 0))
           for s, c in zip(starts, counts)]
    return pl.pallas_call(
        functools.partial(_final_kernel, starts=tuple(starts), nbp=nbp),
        grid=(n // ROW_BLOCK,),
        in_specs=[*seg, *seg, pl.BlockSpec(g.shape, lambda i: (0, 0))],
        out_specs=list(_pair_specs(nbp, D_MODEL)),
        out_shape=[jax.ShapeDtypeStruct((n_p, D_MODEL), f32), jax.ShapeDtypeStruct((n - n_p, D_MODEL), f32)],
        compiler_params=pltpu.CompilerParams(dimension_semantics=("arbitrary",), vmem_limit_bytes=VMEM_LIMIT),
        name="final_norm",
    )(*peers, *x1s, g)


def _peer_specs():
    row = lambda w: pl.BlockSpec((ROW_BLOCK, w), lambda i: (i, 0))
    idx = pl.BlockSpec((ROW_BLOCK, P_SLOTS), lambda i: (i, 0), memory_space=pltpu.SMEM)
    tab = pl.BlockSpec(memory_space=pltpu.VMEM)
    gscr = pltpu.VMEM((G_ROWS, LANES), bf16)
    params = pltpu.CompilerParams(dimension_semantics=("arbitrary",), vmem_limit_bytes=VMEM_LIMIT)
    return row, idx, tab, gscr, params


def _expert_table(tab):
    return tab.astype(bf16).reshape(tab.shape[0], ROW_TILES, LANES)


def _peer_u(eidx, xn, g, utab, n):
    row, idx, tab, gscr, params = _peer_specs()
    return pl.pallas_call(
        _peer_u_kernel,
        grid=(n // ROW_BLOCK,),
        in_specs=[idx, row(D_MODEL), row(P_SLOTS), tab],
        out_specs=row(P_SLOTS),
        out_shape=jax.ShapeDtypeStruct((n, P_SLOTS), f32),
        scratch_shapes=[pltpu.VMEM((ROW_BLOCK * 2 * SUBLANES, LANES), f32),
                        pltpu.VMEM((ROW_TILES, ROW_BLOCK * SUBLANES, LANES), f32), gscr, gscr],
        compiler_params=params,
        name="peer_u",
    )(eidx, xn, g, utab)


def _peer_v(eidx, w, vtab):
    n = w.shape[0]
    row, idx, tab, gscr, params = _peer_specs()
    return pl.pallas_call(
        _peer_v_kernel,
        grid=(n // ROW_BLOCK,),
        in_specs=[idx, row(P_SLOTS), tab],
        out_specs=row(D_MODEL),
        out_shape=jax.ShapeDtypeStruct((n, D_MODEL), f32),
        scratch_shapes=[pltpu.VMEM((ROW_TILES, ROW_BLOCK * SUBLANES, LANES), f32),
                        pltpu.VMEM((ROW_BLOCK * SUBLANES, LANES), f32), gscr, gscr],
        compiler_params=params,
        name="peer_v",
    )(eidx, w, vtab)


def _sc_table_kernel(t_ref, o_ref):
    t = t_ref[...]
    o_ref[...] = pltpu.pack_elementwise([t[:, :SC_WORDS], t[:, SC_WORDS:]], packed_dtype=bf16)


def _sc_table(tab):
    e = tab.shape[0]
    return pl.pallas_call(
        _sc_table_kernel,
        grid=(e // ROW_BLOCK,),
        in_specs=[pl.BlockSpec((ROW_BLOCK, D_MODEL), lambda i: (i, 0))],
        out_specs=pl.BlockSpec((ROW_BLOCK, SC_WORDS), lambda i: (i, 0)),
        out_shape=jax.ShapeDtypeStruct((e, SC_WORDS), jnp.uint32),
        compiler_params=pltpu.CompilerParams(dimension_semantics=("parallel",), vmem_limit_bytes=VMEM_LIMIT),
        name="sc_table",
    )(tab)


def _sc_unpack(words):
    return plsc.unpack(plsc.bitcast(words, bf16), format=plsc.PackFormat.INTERLEAVED)


def _sc_gelu_tanh(x):
    z = math.sqrt(2.0 / math.pi) * (x + 0.044715 * (x * x * x))
    return 0.5 * x * (2.0 - 2.0 / (jnp.exp(2.0 * z) + 1.0))


def _sc_peer(utab32, vtab32, eidx, xn, gate):
    n_sc = eidx.shape[0]
    per = n_sc // SC_WORKERS
    units = 4 * per
    assert n_sc % (SC_WORKERS * SUBLANES) == 0

    def body(u_hbm, v_hbm, eidx_hbm, x_hbm, g_hbm, out_hbm,
             idx_v, x_v, g_v, rows_v, acc_v, w_v, w16_v, out_v, row_sems, tok_sems, out_sem):
        base = (lax.axis_index("s") * 2 + lax.axis_index("c")) * per
        lanes = lax.broadcasted_iota(i32, (SC_LANES,), 0)

        def token_copies(tok):
            tslot = tok % 2
            return (pltpu.make_async_copy(eidx_hbm.at[base + tok], idx_v.at[tslot], tok_sems.at[0]),
                    pltpu.make_async_copy(x_hbm.at[base + tok], x_v.at[tslot], tok_sems.at[1]),
                    pltpu.make_async_copy(g_hbm.at[base + tok], g_v.at[tslot], tok_sems.at[2]))

        def store_out(tok):
            return pltpu.make_async_copy(out_v, out_hbm.at[base + tok], out_sem)

        def gather(tab_hbm, g):
            tok, k = g // 4, g % 4
            return pltpu.make_async_copy(tab_hbm.at[idx_v.at[tok % 2, pl.ds((k % 2) * SC_UNIT_ROWS, SC_UNIT_ROWS)]],
                                         rows_v.at[k % 2], row_sems.at[k % 2])

        def start(g):
            @pl.when(g % 4 < 2)
            def _():
                gather(u_hbm, g).start()

            @pl.when(g % 4 >= 2)
            def _():
                gather(v_hbm, g).start()

        def compute_u(tslot, half, rows):
            @pl.loop(0, SC_UNIT_ROWS // SC_ROW_GROUP)
            def _(rg):
                slot0 = half * SC_UNIT_ROWS + rg * SC_ROW_GROUP

                @pl.loop(0, SC_CHUNKS // SC_CHUNK_GROUP)
                def _(cg):
                    keep = jnp.where(cg > 0, 1.0, 0.0).astype(f32)
                    accs = [acc_v[pl.ds((slot0 + r) * SC_LANES, SC_LANES)] * keep for r in range(SC_ROW_GROUP)]
                    for c in range(SC_CHUNK_GROUP):
                        ch = cg * SC_CHUNK_GROUP + c
                        xa = x_v[tslot, pl.ds(ch * SC_LANES, SC_LANES)]
                        xb = x_v[tslot, pl.ds(SC_WORDS + ch * SC_LANES, SC_LANES)]
                        for r in range(SC_ROW_GROUP):
                            a, b = _sc_unpack(rows[rg * SC_ROW_GROUP + r, pl.ds(ch * SC_LANES, SC_LANES)])
                            accs[r] = accs[r] + a * xa + b * xb
                    for r in range(SC_ROW_GROUP):
                        acc_v[pl.ds((slot0 + r) * SC_LANES, SC_LANES)] = accs[r]

        def gate_weights(tslot):
            @pl.loop(0, P_SLOTS // SC_LANES)
            def _(sg):
                first = sg * SC_LANES * SC_LANES
                act = jnp.zeros((SC_LANES,), f32)
                for lane in range(SC_LANES):
                    act = act + plsc.load_gather(acc_v, [first + lanes * SC_LANES + lane])
                w_v[pl.ds(sg * SC_LANES, SC_LANES)] = g_v[tslot, pl.ds(sg * SC_LANES, SC_LANES)] * _sc_gelu_tanh(act)

            @pl.loop(0, P_SLOTS // SC_LANES)
            def _(sg):
                for r in range(SC_LANES):
                    w16_v[pl.ds((sg * SC_LANES + r) * SC_LANES, SC_LANES)] = plsc.load_gather(
                        w_v, [jnp.zeros((SC_LANES,), i32) + (sg * SC_LANES + r)])

        def compute_v(half, rows):
            @pl.loop(0, SC_CHUNKS // SC_CHUNK_GROUP)
            def _(cg):
                first = cg * SC_CHUNK_GROUP

                @pl.loop(0, SC_UNIT_ROWS // SC_ROW_GROUP)
                def _(rg):
                    keep = jnp.where(jnp.logical_or(half == 1, rg > 0), 1.0, 0.0).astype(f32)
                    spots = [pl.ds((j % 2) * SC_WORDS + (first + j // 2) * SC_LANES, SC_LANES)
                             for j in range(2 * SC_CHUNK_GROUP)]
                    accs = [out_v[spot] * keep for spot in spots]
                    for r in range(SC_ROW_GROUP):
                        row = rg * SC_ROW_GROUP + r
                        wv = w16_v[pl.ds((half * SC_UNIT_ROWS + row) * SC_LANES, SC_LANES)]
                        for c in range(SC_CHUNK_GROUP):
                            a, b = _sc_unpack(rows[row, pl.ds((first + c) * SC_LANES, SC_LANES)])
                            accs[2 * c] = accs[2 * c] + a * wv
                            accs[2 * c + 1] = accs[2 * c + 1] + b * wv
                    for spot, acc in zip(spots, accs):
                        out_v[spot] = acc

        for cp in token_copies(0):
            cp.start()
        for cp in token_copies(0):
            cp.wait()
        start(0)

        @pl.loop(0, units)
        def _(g):
            tok, k = g // 4, g % 4

            @pl.when(jnp.logical_and(k == 0, tok + 1 < per))
            def _():
                for cp in token_copies(tok + 1):
                    cp.start()

            @pl.when(jnp.logical_and(k == 3, tok + 1 < per))
            def _():
                for cp in token_copies(tok + 1):
                    cp.wait()

            @pl.when(g + 1 < units)
            def _():
                start(g + 1)

            gather(u_hbm, g).wait()
            rows = rows_v.at[k % 2]

            @pl.when(k < 2)
            def _():
                compute_u(tok % 2, k, rows)

            @pl.when(k == 1)
            def _():
                gate_weights(tok % 2)

            @pl.when(jnp.logical_and(k == 2, tok > 0))
            def _():
                store_out(tok - 1).wait()

            @pl.when(k >= 2)
            def _():
                compute_v(k - 2, rows)

            @pl.when(k == 3)
            def _():
                store_out(tok).start()

        store_out(per - 1).wait()

    return pl.kernel(
        body, mesh=plsc.VectorSubcoreMesh(core_axis_name="c", subcore_axis_name="s"),
        out_type=jax.ShapeDtypeStruct((n_sc, D_MODEL), f32),
        scratch_types=[pltpu.VMEM((2, P_SLOTS), i32), pltpu.VMEM((2, D_MODEL), f32), pltpu.VMEM((2, P_SLOTS), f32),
                       pltpu.VMEM((2, SC_UNIT_ROWS, SC_WORDS), jnp.uint32), pltpu.VMEM((P_SLOTS * SC_LANES,), f32),
                       pltpu.VMEM((P_SLOTS,), f32), pltpu.VMEM((P_SLOTS * SC_LANES,), f32), pltpu.VMEM((D_MODEL,), f32),
                       pltpu.SemaphoreType.DMA((2,)), pltpu.SemaphoreType.DMA((3,)), pltpu.SemaphoreType.DMA],
        compiler_params=pltpu.CompilerParams(needs_layout_passes=False),
        name="sc_peer",
    )(utab32, vtab32, eidx, xn, gate)


def kernel(x_prompt, x_sample, cache_k, cache_v, state_C, state_n, state_m, state_conv, norm1_g, w_in, b_gates, rel_bias, conv_w, conv_b, mh_norm_g, w_out, norm2_g, peer_wq, peer_keys, peer_u, peer_v, final_g):
    bp, sp, d = x_prompt.shape
    bs, ts, _ = x_sample.shape
    n_p, n_s = bp * sp, bs * ts
    n = n_p + n_s
    assert n_p % ROW_BLOCK == 0 and n_s % ROW_BLOCK == 0 and d == D_MODEL
    depth = w_in.shape[0]
    assert depth == 1, "the final norm is fused into the last layer's PEER pass"
    l = 0
    xp, xs = x_prompt.reshape(n_p, d), x_sample.reshape(n_s, d)

    aq, ak, av, mqk, mv, mo, gates = _inproj(xp, xs, norm1_g[l], w_in[l])
    zeros = lambda *shp: jnp.zeros(shp, f32)
    mparams = (conv_w[l], conv_b[l], b_gates[l], mh_norm_g[l])
    retrieval = (w_out[l], norm2_g[l], peer_wq[l], peer_keys[l])

    def mixers(seq, lo, hi, state):
        att = _attn_prompt(aq, ak, av, rel_bias[l], seq, sp, lo, hi)
        h, c, nn, mm = _mlstm(mqk, mv, mo, gates, seq * sp + lo, 1, hi - lo, *state, *mparams)
        conv_rows = mqk[seq * sp + hi - (CONV_W - 1):seq * sp + hi][None]
        return att, h, (c, nn[:, :M_HEADS], mm[:, :M_HEADS, 0], conv_rows)

    fresh = (zeros(1, M_HEADS, M_DH, M_DH), zeros(1, M_HEADS, M_DH), zeros(1, M_HEADS), zeros(1, CONV_W - 1, 2 * M_WIDTH))

    assert bp == 2 and 0 < SC_FIRST_ROWS < sp and 0 < SC_EXTRA_ROWS < sp
    n_sc = sp + SC_EXTRA_ROWS
    sc_tables = _sc_table(peer_u[l]), _sc_table(peer_v[l])
    att_0a, h_0a, state_0a = mixers(0, 0, SC_FIRST_ROWS, fresh)
    x1_a, xn_a, eidx_a, gate_a = _mid(xp, 0, xp, 0, att_0a, att_0a, h_0a, h_0a, SC_FIRST_ROWS, 0, *retrieval)
    peer_a = _sc_peer(*sc_tables, eidx_a, xn_a, gate_a)

    att_0b, h_0b, state_0 = mixers(0, SC_FIRST_ROWS, sp, state_0a)
    att_1a, h_1a, state_1a = mixers(1, 0, SC_EXTRA_ROWS, fresh)
    x1_c, xn_c, eidx_c, gate_c = _mid(xp, SC_FIRST_ROWS, xp, sp, att_0b, att_1a, h_0b, h_1a,
                                      sp - SC_FIRST_ROWS, SC_EXTRA_ROWS, *retrieval)
    peer_c = _sc_peer(*sc_tables, eidx_c, xn_c, gate_c)

    att_1b, h_1b, state_1 = mixers(1, SC_EXTRA_ROWS, sp, state_1a)
    lcache = cache_k.shape[2]
    att_s = _attn_sample(aq, ak, av, cache_k[l].reshape(bs, lcache, A_WIDTH),
                         cache_v[l].reshape(bs, lcache, A_WIDTH), rel_bias[l], n_p, bs, ts)
    h_s, c_s, nn_s, mm_s = _mlstm(mqk, mv, mo, gates, n_p, bs, ts, state_C[l], state_n[l], state_m[l],
                                  state_conv[l], *mparams)
    x1_b, xn_b, eidx_b, gate_b = _mid(xp, n_sc, xs, 0, att_1b, att_s, h_1b, h_s, n_p - n_sc, n_s, *retrieval)
    w_b = _peer_u(eidx_b, xn_b, gate_b, _expert_table(peer_u[l]), n - n_sc)
    peer_b = _peer_v(eidx_b, w_b, _expert_table(peer_v[l]))
    y_p, y_s = _final((peer_a, peer_c, peer_b), (x1_a, x1_c, x1_b), final_g, n_p)
    c_p, nn_p, mm_p = (jnp.concatenate(ab, axis=0) for ab in zip(state_0[:3], state_1[:3]))

    def tail(a, row0, bsz, t, keep):
        return jnp.stack([a[row0 + (b + 1) * t - keep:row0 + (b + 1) * t] for b in range(bsz)])

    keep = min(WINDOW, sp)
    heads = lambda a: a.reshape(a.shape[0], a.shape[1], A_HEADS, A_DH)
    ctail = CONV_W - 1
    conv_tail = lambda buf, a, row0, bsz, t: jnp.concatenate([buf.astype(a.dtype), tail(a, row0, bsz, t, min(ctail, t))],
                                                             axis=1)[:, -ctail:]
    st = lambda a: a[None]
    return (y_p.reshape(bp, sp, d), y_s.reshape(bs, ts, d),
            st(heads(tail(ak, 0, bp, sp, keep))), st(heads(tail(av, 0, bp, sp, keep))),
            st(c_p), st(nn_p), st(mm_p),
            st(conv_tail(zeros(bp, ctail, 2 * M_WIDTH), mqk, 0, bp, sp)),
            st(heads(ak[n_p:].reshape(bs, ts, A_WIDTH))), st(heads(av[n_p:].reshape(bs, ts, A_WIDTH))),
            st(c_s), st(nn_s[:, :M_HEADS]), st(mm_s[:, :M_HEADS, 0]),
            st(conv_tail(state_conv[l], mqk, n_p, bs, ts)))
```

```python
import functools
import math

import jax
import jax.numpy as jnp
from jax import lax
from jax.experimental import pallas as pl
from jax.experimental.pallas import tpu as pltpu
from jax.experimental.pallas import tpu_sc as plsc

f32 = jnp.float32
bf16 = jnp.bfloat16
i32 = jnp.int32

D_MODEL = 1024
CHUNK = 64
A_HEADS = 8
A_DH = 64
A_WIDTH = A_HEADS * A_DH
BAND_CHUNKS = 8
WINDOW = BAND_CHUNKS * CHUNK
MAX_REL = 128
ATT_SCALE = A_DH ** -0.5
M_HEADS = 4
M_DH = 128
M_WIDTH = M_HEADS * M_DH
CONV_W = 4
P_HEADS = 8
P_DKEY = 256
N_KEYS = 128
P_TOPK = 16
P_SLOTS = P_HEADS * P_TOPK
EPS = 1e-6
NEG = -1e30

LANES = 128
SUBLANES = 8
ROW_BLOCK = 256
ATT_TILE = 512
ATT_SUB = 128
ATT_KEYS = ATT_SUB + WINDOW
ROW_TILES = D_MODEL // LANES
G_ROWS = P_SLOTS * ROW_TILES
TOKEN_UNROLL = 8
INDEX_GROUPS = P_SLOTS // SUBLANES
VMEM_LIMIT = 56 * 1024 * 1024

SC_WORKERS = 32
SC_LANES = 16
SC_UNIT_ROWS = P_SLOTS // 2
SC_ROW_GROUP = 16
SC_WORDS = D_MODEL // 2
SC_CHUNKS = SC_WORDS // SC_LANES
SC_CHUNK_GROUP = 8
SC_FIRST_ROWS = 8192
SC_EXTRA_ROWS = 2048


def _rms(x, g):
    return x * lax.rsqrt(jnp.mean(x * x, axis=-1, keepdims=True) + EPS) * g


def _split_bf16(x):
    hi = x.astype(bf16)
    lo = (x - hi.astype(f32)).astype(bf16)
    return hi, lo


def _pair_specs(nbp, width, first=0, second=0):
    return (pl.BlockSpec((ROW_BLOCK, width), lambda i: (first + jnp.minimum(i, nbp - 1), 0)),
            pl.BlockSpec((ROW_BLOCK, width), lambda i: (second + jnp.maximum(i - nbp, 0), 0)))


def _pick(nbp, p_ref, s_ref):
    return jnp.where(pl.program_id(0) < nbp, p_ref[...], s_ref[...])


def _inproj_kernel(xp_ref, xs_ref, g_ref, w_ref, wgh_ref, wgl_ref,
                   aq_ref, ak_ref, av_ref, mqk_ref, mv_ref, mo_ref, gate_ref, *, nbp):
    xn = _rms(_pick(nbp, xp_ref, xs_ref), g_ref[...])
    xh, xl = _split_bf16(xn)

    def proj(lo, hi):
        return jnp.dot(xh, w_ref[:, lo:hi], preferred_element_type=f32)

    aq_ref[...] = proj(0, 512)
    ak_ref[...] = proj(512, 1024)
    av_ref[...] = proj(1024, 1536)
    mqk_ref[...] = proj(1536, 2560)
    mv_ref[...] = proj(2560, 3072)
    mo_ref[...] = proj(3072, 3584)
    gate_ref[...] = (jnp.dot(xh, wgh_ref[...], preferred_element_type=f32)
                     + jnp.dot(xl, wgh_ref[...], preferred_element_type=f32)
                     + jnp.dot(xh, wgl_ref[...], preferred_element_type=f32))


def _inproj(xp, xs, g1, w_in):
    n = xp.shape[0] + xs.shape[0]
    nbp = xp.shape[0] // ROW_BLOCK
    main = 3 * A_WIDTH + 4 * M_WIDTH
    w_main = w_in[:, :main].astype(bf16)
    wg = jnp.pad(w_in[:, main:], ((0, 0), (0, LANES - 2 * M_HEADS)))
    wgh, wgl = _split_bf16(wg)
    widths = (512, 512, 512, 1024, 512, 512, LANES)
    row = lambda w: pl.BlockSpec((ROW_BLOCK, w), lambda i: (i, 0))
    full = lambda a: pl.BlockSpec(a.shape, lambda i: (0,) * a.ndim)
    g = g1.reshape(1, D_MODEL)
    return pl.pallas_call(
        functools.partial(_inproj_kernel, nbp=nbp),
        grid=(n // ROW_BLOCK,),
        in_specs=[*_pair_specs(nbp, D_MODEL), full(g), full(w_main), full(wgh), full(wgl)],
        out_specs=[row(w) for w in widths],
        out_shape=[jax.ShapeDtypeStruct((n, w), f32) for w in widths],
        compiler_params=pltpu.CompilerParams(dimension_semantics=("parallel",), vmem_limit_bytes=VMEM_LIMIT),
        name="inproj",
    )(xp, xs, g, w_main, wgh, wgl)


def _attn_heads(q, k, v, bias_ref, key_ok):
    outs = []
    for h in range(A_HEADS):
        sl = slice(h * A_DH, (h + 1) * A_DH)
        s = lax.dot_general(q[:, sl], k[:, sl], (((1,), (1,)), ((), ())), preferred_element_type=f32)
        s = s * ATT_SCALE + bias_ref[h]
        if key_ok is not None:
            s = jnp.where(key_ok, s, NEG)
        m = jnp.max(s, axis=-1, keepdims=True)
        p = jnp.exp(s - m)
        l = jnp.sum(p, axis=-1, keepdims=True)
        o = jnp.dot(p.astype(bf16), v[:, sl], preferred_element_type=f32)
        outs.append(o / l)
    return jnp.concatenate(outs, axis=-1)


def _attn_prompt_kernel(q_ref, k0_ref, k1_ref, v0_ref, v1_ref, bias_ref, o_ref, *, t0):
    t = t0 + pl.program_id(1)
    q = q_ref[...].astype(bf16)
    k = jnp.concatenate([k0_ref[...], k1_ref[...]], axis=0).astype(bf16)
    v = jnp.concatenate([v0_ref[...], v1_ref[...]], axis=0).astype(bf16)
    col = lax.broadcasted_iota(i32, (1, ATT_KEYS), 1)
    for s in range(ATT_TILE // ATT_SUB):
        lo = s * ATT_SUB
        key_ok = (t * ATT_TILE + lo + col) >= WINDOW
        o_ref[lo:lo + ATT_SUB, :] = _attn_heads(q[lo:lo + ATT_SUB], k[lo:lo + ATT_KEYS], v[lo:lo + ATT_KEYS],
                                                 bias_ref, key_ok)


def _attn_sample_kernel(q_ref, k_ref, v_ref, bias_ref, o_ref):
    o_ref[...] = _attn_heads(q_ref[...].astype(bf16), k_ref[0].astype(bf16), v_ref[0].astype(bf16), bias_ref, None)


def _rel_bias_table(rel_bias, rows, cols, offset, valid):
    span = rows + cols - 1
    rel = offset + rows - 1 - jnp.arange(span)
    diag = rel_bias[:, jnp.clip(rel, -MAX_REL, MAX_REL) + MAX_REL].astype(f32)
    diag = jnp.pad(diag, ((0, 0), (0, 1)))
    flat = jnp.tile(diag, (1, rows))[:, rows - 1:rows - 1 + rows * span]
    return jnp.where(valid[None], flat.reshape(-1, rows, span)[:, :, :cols], NEG)


def _attn_prompt(q, k, v, rel_bias, seq, s, lo, hi):
    assert s % ATT_TILE == 0 and WINDOW == ATT_TILE and lo % ATT_TILE == 0 and hi % ATT_TILE == 0
    nt, t0, cnt = s // ATT_TILE, lo // ATT_TILE, (hi - lo) // ATT_TILE
    i = jnp.arange(ATT_SUB)[:, None]
    j = jnp.arange(ATT_KEYS)[None, :]
    off = j - (i // CHUNK) * CHUNK
    bias = _rel_bias_table(rel_bias, ATT_SUB, ATT_KEYS, WINDOW, (off >= 0) & (off < WINDOW + CHUNK))
    cur = pl.BlockSpec((ATT_TILE, A_WIDTH), lambda b, t: (seq * nt + t0 + t, 0))
    prev = pl.BlockSpec((ATT_TILE, A_WIDTH), lambda b, t: (seq * nt + jnp.maximum(t0 + t - 1, 0), 0))
    return pl.pallas_call(
        functools.partial(_attn_prompt_kernel, t0=t0),
        grid=(1, cnt),
        in_specs=[cur, prev, cur, prev, cur, pl.BlockSpec(bias.shape, lambda b, t: (0, 0, 0))],
        out_specs=pl.BlockSpec((ATT_TILE, A_WIDTH), lambda b, t: (t, 0)),
        out_shape=jax.ShapeDtypeStruct((hi - lo, A_WIDTH), f32),
        compiler_params=pltpu.CompilerParams(dimension_semantics=("parallel", "parallel"),
                                             vmem_limit_bytes=VMEM_LIMIT),
        name="attn_prompt",
    )(q, k, k, v, v, bias)


def _attn_sample(q, k, v, ck, cv, rel_bias, row0, bsz, t):
    l = ck.shape[1]
    assert row0 % t == 0
    keys = -(-(l + t) // LANES) * LANES
    padk = ((0, 0), (0, keys - l - t), (0, 0))
    kk = jnp.pad(jnp.concatenate([ck, k[row0:].reshape(bsz, t, A_WIDTH)], axis=1), padk)
    vv = jnp.pad(jnp.concatenate([cv, v[row0:].reshape(bsz, t, A_WIDTH)], axis=1), padk)
    j = jnp.arange(keys)[None, :]
    bias = _rel_bias_table(rel_bias, t, keys, l, jnp.broadcast_to(j < l + t, (t, keys)))
    return pl.pallas_call(
        _attn_sample_kernel,
        grid=(bsz,),
        in_specs=[pl.BlockSpec((t, A_WIDTH), lambda b: (row0 // t + b, 0)),
                  pl.BlockSpec((1, keys, A_WIDTH), lambda b: (b, 0, 0)),
                  pl.BlockSpec((1, keys, A_WIDTH), lambda b: (b, 0, 0)),
                  pl.BlockSpec(bias.shape, lambda b: (0, 0, 0))],
        out_specs=pl.BlockSpec((t, A_WIDTH), lambda b: (b, 0)),
        out_shape=jax.ShapeDtypeStruct((bsz * t, A_WIDTH), f32),
        compiler_params=pltpu.CompilerParams(dimension_semantics=("parallel",), vmem_limit_bytes=VMEM_LIMIT),
        name="attn_sample",
    )(q, kk, vv, bias)


def _mlstm_chunk(a, vall, o_in, gate_in, cw_ref, cb_ref, bg_ref, mhg_ref, c_s, n_s, m_s, prev_s):
    lc = a.shape[0]
    ext = jnp.concatenate([prev_s[...], a], axis=0)
    conv = cb_ref[...]
    for j in range(CONV_W):
        lo = SUBLANES - (CONV_W - 1) + j
        conv = conv + cw_ref[j:j + 1, :] * ext[lo:lo + lc]
    prev_s[...] = a[lc - SUBLANES:lc]
    qk = conv * jax.nn.sigmoid(conv)

    z = gate_in + bg_ref[...]
    lane = lax.broadcasted_iota(i32, (lc, LANES), 1)
    row = lax.broadcasted_iota(i32, (lc, LANES), 0)
    logf = jnp.minimum(z, 0.0) - jnp.log1p(jnp.exp(-jnp.abs(z)))
    cum = jnp.where((lane >= M_HEADS) & (lane < 2 * M_HEADS), logf, 0.0)
    shift = 1
    while shift < lc:
        cum = cum + jnp.where(row >= shift, pltpu.roll(cum, shift, axis=0), 0.0)
        shift *= 2
    zc = jnp.where(lane < M_HEADS, z, cum)
    zt = jnp.concatenate([zc, jnp.zeros((LANES - lc, LANES), f32)], axis=0).T[:, :lc]

    ri = lax.broadcasted_iota(i32, (lc, lc), 0)
    ci = lax.broadcasted_iota(i32, (lc, lc), 1)
    causal = ri >= ci
    state = [(m_s[h:h + 1, 0:1], c_s[h], n_s[h:h + 1, :]) for h in range(M_HEADS)]
    hs, new_state = [], []
    for h in range(M_HEADS):
        sl = slice(h * M_DH, (h + 1) * M_DH)
        q = qk[:, sl]
        k = qk[:, M_WIDTH + h * M_DH:M_WIDTH + (h + 1) * M_DH] * (M_DH ** -0.5)
        v = vall[:, sl]
        i_col = zc[:, h:h + 1]
        b_col = zc[:, M_HEADS + h:M_HEADS + h + 1]
        i_row = zt[h:h + 1, :]
        b_row = zt[M_HEADS + h:M_HEADS + h + 1, :]
        m_prev, c_prev, n_prev = state[h]

        dmat = jnp.where(causal, b_col - b_row + i_row, NEG)
        inter = b_col + m_prev
        mt = jnp.maximum(jnp.max(dmat, axis=-1, keepdims=True), inter)
        w_intra = jnp.exp(dmat - mt)
        w_inter = jnp.exp(inter - mt)
        qb, kb, vb = q.astype(bf16), k.astype(bf16), v.astype(bf16)
        s = lax.dot_general(qb, kb, (((1,), (1,)), ((), ())), preferred_element_type=f32) * w_intra
        num = (w_inter * jnp.dot(qb, c_prev.astype(bf16), preferred_element_type=f32)
               + jnp.dot(s.astype(bf16), vb, preferred_element_type=f32))
        den = w_inter * jnp.sum(q * n_prev, axis=-1, keepdims=True) + jnp.sum(s, axis=-1, keepdims=True)
        hh = num / jnp.maximum(jnp.abs(den), jnp.exp(-mt))
        m_new = mt[lc - 1:lc, :]
        b_last = b_col[lc - 1:lc, :]
        w_s = jnp.exp(b_last - b_col + i_col - m_new)
        decay = jnp.exp(b_last + m_prev - m_new)
        kw = k * w_s
        new_state.append((jnp.broadcast_to(m_new, (1, LANES)),
                          decay * c_prev + lax.dot_general(kw.astype(bf16), vb, (((0,), (0,)), ((), ())),
                                                           preferred_element_type=f32),
                          decay * n_prev + jnp.sum(kw, axis=0, keepdims=True)))
        hs.append(hh * lax.rsqrt(jnp.mean(hh * hh, axis=-1, keepdims=True) + EPS))

    for h, (m_new, c_new, n_new) in enumerate(new_state):
        m_s[h:h + 1, :] = m_new
        c_s[h] = c_new
        n_s[h:h + 1, :] = n_new
    return jnp.concatenate(hs, axis=-1) * mhg_ref[...] * jax.nn.sigmoid(o_in)


def _mlstm_kernel(qk_ref, v_ref, o_ref, gate_ref, c0_ref, n0_ref, m0_ref, cbuf_ref,
                  cw_ref, cb_ref, bg_ref, mhg_ref,
                  h_ref, cout_ref, nout_ref, mout_ref,
                  c_s, n_s, m_s, prev_s):
    c = pl.program_id(1)

    @pl.when(c == 0)
    def _():
        c_s[...] = c0_ref[0]
        n_s[...] = n0_ref[0]
        m_s[...] = m0_ref[0]
        prev_s[...] = cbuf_ref[0]

    h_ref[...] = _mlstm_chunk(qk_ref[...], v_ref[...], o_ref[...], gate_ref[...],
                              cw_ref, cb_ref, bg_ref, mhg_ref, c_s, n_s, m_s, prev_s)

    @pl.when(c == pl.num_programs(1) - 1)
    def _():
        cout_ref[0] = c_s[...]
        nout_ref[0] = n_s[...]
        mout_ref[0] = m_s[...]


def _mlstm(mqk, mv, mo, gates, row0, bsz, t, c0, n0, m0, cbuf, conv_w, conv_b, b_gates, mh_g):
    lc = min(CHUNK, t)
    step = lc
    nc = t // step
    assert t % step == 0 and lc % SUBLANES == 0 and row0 % step == 0
    n0p = jnp.pad(n0.astype(f32), ((0, 0), (0, SUBLANES - M_HEADS), (0, 0)))
    m0p = jnp.pad(jnp.broadcast_to(m0.astype(f32)[:, :, None], (bsz, M_HEADS, LANES)),
                  ((0, 0), (0, SUBLANES - M_HEADS), (0, 0)))
    cbp = jnp.pad(cbuf.astype(f32), ((0, 0), (SUBLANES - (CONV_W - 1), 0), (0, 0)))
    bg = jnp.pad(b_gates.astype(f32), (0, LANES - 2 * M_HEADS)).reshape(1, LANES)
    seq = lambda w: pl.BlockSpec((step, w), lambda b, c: (row0 // step + b * nc + c, 0))
    out_seq = pl.BlockSpec((step, M_WIDTH), lambda b, c: (b * nc + c, 0))
    per_b = lambda shp: pl.BlockSpec((1,) + shp, lambda b, c: (b,) + (0,) * len(shp))
    full = lambda a: pl.BlockSpec(a.shape, lambda b, c: (0,) * a.ndim)
    cb = conv_b.reshape(1, -1)
    mhg = mh_g.reshape(1, -1)
    return pl.pallas_call(
        _mlstm_kernel,
        grid=(bsz, nc),
        in_specs=[seq(2 * M_WIDTH), seq(M_WIDTH), seq(M_WIDTH), seq(LANES),
                  per_b((M_HEADS, M_DH, M_DH)), per_b((SUBLANES, M_DH)), per_b((SUBLANES, LANES)),
                  per_b((SUBLANES, 2 * M_WIDTH)),
                  full(conv_w), full(cb), full(bg), full(mhg)],
        out_specs=[out_seq, per_b((M_HEADS, M_DH, M_DH)), per_b((SUBLANES, M_DH)), per_b((SUBLANES, LANES))],
        out_shape=[jax.ShapeDtypeStruct((bsz * t, M_WIDTH), f32),
                   jax.ShapeDtypeStruct((bsz, M_HEADS, M_DH, M_DH), f32),
                   jax.ShapeDtypeStruct((bsz, SUBLANES, M_DH), f32),
                   jax.ShapeDtypeStruct((bsz, SUBLANES, LANES), f32)],
        scratch_shapes=[pltpu.VMEM((M_HEADS, M_DH, M_DH), f32), pltpu.VMEM((SUBLANES, M_DH), f32),
                        pltpu.VMEM((SUBLANES, LANES), f32), pltpu.VMEM((SUBLANES, 2 * M_WIDTH), f32)],
        compiler_params=pltpu.CompilerParams(dimension_semantics=("parallel", "arbitrary"),
                                             vmem_limit_bytes=VMEM_LIMIT),
        name="mlstm",
    )(mqk, mv, mo, gates, c0.astype(f32), n0p, m0p, cbp, conv_w, cb, bg, mhg)


def _topk_rows(s, k, payload=None):
    n = s.shape[0]
    rows = lax.broadcasted_iota(i32, s.shape, 0).astype(f32)
    vals, ids = [], []
    for _ in range(k):
        m = jnp.max(s, axis=0, keepdims=True)
        pos = jnp.min(jnp.where(s == m, rows, float(n)), axis=0, keepdims=True)
        sel = rows == pos
        vals.append(m)
        ids.append(pos if payload is None else jnp.max(jnp.where(sel, payload, -1.0), axis=0, keepdims=True))
        s = jnp.where(sel, -jnp.inf, s)
    return jnp.concatenate(vals, axis=0), jnp.concatenate(ids, axis=0)


def _mid_kernel(xp_ref, xs_ref, attp_ref, atts_ref, hp_ref, hs_ref, wo_ref, g2_ref, wq_ref, keys_ref,
                x1_ref, xn_ref, eidx_ref, eidx_t_ref, gate_ref, *, nbp):
    cat = jnp.concatenate([_pick(nbp, attp_ref, atts_ref), _pick(nbp, hp_ref, hs_ref)], axis=-1).astype(bf16)
    x1 = _pick(nbp, xp_ref, xs_ref) + jnp.dot(cat, wo_ref[...], preferred_element_type=f32)
    x1_ref[...] = x1
    xn = _rms(x1, g2_ref[...])
    xn_ref[...] = xn
    xb = xn.astype(bf16)
    e_rows, g_rows = [], []
    for h in range(P_HEADS):
        q = jnp.dot(xb, wq_ref[:, h * P_DKEY:(h + 1) * P_DKEY], preferred_element_type=f32).astype(bf16)
        half = []
        for p in range(2):
            st = lax.dot_general(keys_ref[h, p], q[:, p * N_KEYS:(p + 1) * N_KEYS], (((1,), (1,)), ((), ())),
                                 preferred_element_type=f32)
            half.append(_topk_rows(st, P_TOPK))
        (v0, i0), (v1, i1) = half
        width = [P_TOPK // (a + 1) for a in range(P_TOPK)]
        fill = -sum(width) % SUBLANES
        cand = jnp.concatenate([v0[a:a + 1] + v1[:width[a]] for a in range(P_TOPK)]
                               + [jnp.full((fill, v0.shape[1]), -jnp.inf, f32)], axis=0)
        cidx = jnp.concatenate([i0[a:a + 1] * float(N_KEYS) + i1[:width[a]] for a in range(P_TOPK)]
                               + [jnp.zeros((fill, v0.shape[1]), f32)], axis=0)
        top_s, eid = _topk_rows(cand, P_TOPK, payload=cidx)
        ex = jnp.exp(top_s - top_s[0:1])
        e_rows.append(eid)
        g_rows.append(ex / jnp.sum(ex, axis=0, keepdims=True))
    eidx_t = jnp.concatenate(e_rows, axis=0)
    eidx_t_ref[...] = eidx_t.astype(i32)
    eidx_ref[...] = eidx_t.T.astype(i32)
    gate_ref[...] = jnp.concatenate(g_rows, axis=0).T


def _mid(xp, xp_row0, xs, xs_row0, attp, atts, hp, hs, n_a, n_b, w_out, g2, wq, keys):
    n = n_a + n_b
    nbp = n_a // ROW_BLOCK
    assert n_a % ROW_BLOCK == 0 and n_b % ROW_BLOCK == 0 and xp_row0 % ROW_BLOCK == 0 and xs_row0 % ROW_BLOCK == 0
    row = lambda w: pl.BlockSpec((ROW_BLOCK, w), lambda i: (i, 0))
    full = lambda a: pl.BlockSpec(a.shape, lambda i: (0,) * a.ndim)
    wo = w_out.astype(bf16)
    wqb = wq.astype(bf16)
    kb = keys.astype(bf16)
    g = g2.reshape(1, D_MODEL)
    return pl.pallas_call(
        functools.partial(_mid_kernel, nbp=nbp),
        grid=(n // ROW_BLOCK,),
        in_specs=[*_pair_specs(nbp, D_MODEL, xp_row0 // ROW_BLOCK, xs_row0 // ROW_BLOCK), *_pair_specs(nbp, A_WIDTH),
                  *_pair_specs(nbp, M_WIDTH), full(wo), full(g), full(wqb), full(kb)],
        out_specs=[row(D_MODEL), row(D_MODEL), row(P_SLOTS), pl.BlockSpec((P_SLOTS, ROW_BLOCK), lambda i: (0, i)),
                   row(P_SLOTS)],
        out_shape=[jax.ShapeDtypeStruct((n, D_MODEL), f32), jax.ShapeDtypeStruct((n, D_MODEL), f32),
                   jax.ShapeDtypeStruct((n, P_SLOTS), i32), jax.ShapeDtypeStruct((P_SLOTS, n), i32),
                   jax.ShapeDtypeStruct((n, P_SLOTS), f32)],
        compiler_params=pltpu.CompilerParams(dimension_semantics=("parallel",), vmem_limit_bytes=VMEM_LIMIT),
        name="outproj_retrieve",
    )(xp, xs, attp, atts, hp, hs, wo, g, wqb, kb)


def _gather_rows(eidx_refs, tab_ref, g_ref, t):
    for i in range(SUBLANES):
        for k, eidx_ref in enumerate(eidx_refs):
            r = k * SUBLANES + i
            g_ref[pl.ds(r * ROW_TILES, ROW_TILES), :] = tab_ref[eidx_ref[i, t]]


def _tile_rows(t, rows=SUBLANES):
    return pl.ds(pl.multiple_of(t * rows, rows), rows)


def _pipelined_tokens(nt, eidx_refs, tab_ref, g0_s, g1_s, compute):
    bufs = (g0_s, g1_s)
    _gather_rows(eidx_refs, tab_ref, g0_s, 0)

    def body(j, carry):
        for u in range(TOKEN_UNROLL):
            t = TOKEN_UNROLL * j + u
            _gather_rows(eidx_refs, tab_ref, bufs[(u + 1) % 2], jnp.minimum(t + 1, nt - 1))
            compute(t, bufs[u % 2])
        return carry

    lax.fori_loop(0, nt // TOKEN_UNROLL, body, 0)


def _gelu_tanh(x):
    return 0.5 * x * (1.0 + jnp.tanh(math.sqrt(2.0 / math.pi) * (x + 0.044715 * (x * x * x))))


def _peer_u_kernel(*refs):
    eidx_refs, (xn_ref, g_ref, tab_ref, w_ref, xl_s, r_s, g0_s, g1_s) = refs[:INDEX_GROUPS], refs[INDEX_GROUPS:]
    nt = xn_ref.shape[0]
    xn = xn_ref[...]
    xh = xn.astype(bf16).astype(f32)
    xl = xn - xh
    for k in range(ROW_TILES):
        xl_s[pl.ds(k, nt, stride=2 * SUBLANES), :] = xh[:, k * LANES:(k + 1) * LANES]
        xl_s[pl.ds(SUBLANES + k, nt, stride=2 * SUBLANES), :] = xl[:, k * LANES:(k + 1) * LANES]
    diag = (lax.broadcasted_iota(i32, (SUBLANES, G_ROWS), 1) % ROW_TILES
            == lax.broadcasted_iota(i32, (SUBLANES, G_ROWS), 0))

    def compute(t, g_s):
        lhs = xl_s[_tile_rows(t, 2 * SUBLANES), :].astype(bf16)
        out = lax.dot_general(lhs, g_s[...], (((1,), (1,)), ((), ())), preferred_element_type=f32)
        part = jnp.where(diag, out[:SUBLANES] + out[SUBLANES:], 0.0)
        for c in range(ROW_TILES):
            r_s[c, _tile_rows(t), :] = part[:, c * LANES:(c + 1) * LANES]

    _pipelined_tokens(nt, eidx_refs, tab_ref, g0_s, g1_s, compute)
    cols = []
    for c in range(ROW_TILES):
        acc = r_s[c, pl.ds(0, nt, stride=SUBLANES), :]
        for k in range(1, SUBLANES):
            acc = acc + r_s[c, pl.ds(k, nt, stride=SUBLANES), :]
        cols.append(acc)
    s = jnp.concatenate(cols, axis=-1)
    fold = (lax.broadcasted_iota(i32, (G_ROWS, P_SLOTS), 0) // ROW_TILES
            == lax.broadcasted_iota(i32, (G_ROWS, P_SLOTS), 1)).astype(bf16)
    sh, sl = _split_bf16(s)
    act = jnp.dot(sh, fold, preferred_element_type=f32) + jnp.dot(sl, fold, preferred_element_type=f32)
    w_ref[...] = g_ref[...] * _gelu_tanh(act)


def _peer_v_kernel(*refs):
    eidx_refs, (w_ref, tab_ref, peer_ref, wl_s, o_s, g0_s, g1_s) = refs[:INDEX_GROUPS], refs[INDEX_GROUPS:]
    nt = w_ref.shape[0]
    spread = (lax.broadcasted_iota(i32, (P_SLOTS, G_ROWS), 1) // ROW_TILES
              == lax.broadcasted_iota(i32, (P_SLOTS, G_ROWS), 0)).astype(bf16)
    wexp = jnp.dot(w_ref[...].astype(bf16), spread, preferred_element_type=f32)
    lane = lax.broadcasted_iota(i32, (nt, LANES), 1)
    for c in range(ROW_TILES):
        wc = wexp[:, c * LANES:(c + 1) * LANES]
        for k in range(SUBLANES):
            wl_s[c, pl.ds(k, nt, stride=SUBLANES), :] = jnp.where(lane % ROW_TILES == k, wc, 0.0)

    def compute(t, g_s):
        lhs = jnp.concatenate([wl_s[c, _tile_rows(t), :] for c in range(ROW_TILES)], axis=-1).astype(bf16)
        o_s[_tile_rows(t), :] = jnp.dot(lhs, g_s[...], preferred_element_type=f32)

    _pipelined_tokens(nt, eidx_refs, tab_ref, g0_s, g1_s, compute)
    for k in range(ROW_TILES):
        peer_ref[:, k * LANES:(k + 1) * LANES] = o_s[pl.ds(k, nt, stride=SUBLANES), :]


def _final_kernel(*refs, starts, nbp):
    k = len(starts)
    peer_refs, x_refs, (gf_ref, yp_ref, ys_ref) = refs[:k], refs[k:2 * k], refs[2 * k:]
    resid = x_refs[0][...] + peer_refs[0][...]
    for start, p_ref, x_ref in zip(starts[1:], peer_refs[1:], x_refs[1:]):
        resid = jnp.where(pl.program_id(0) >= start, x_ref[...] + p_ref[...], resid)
    y = _rms(resid, gf_ref[...])

    @pl.when(pl.program_id(0) < nbp)
    def _():
        yp_ref[...] = y

    @pl.when(pl.program_id(0) >= nbp)
    def _():
        ys_ref[...] = y


def _final(peers, x1s, gf, n_p):
    counts = [x.shape[0] // ROW_BLOCK for x in x1s]
    starts = [sum(counts[:k]) for k in range(len(counts))]
    n, nbp = sum(counts) * ROW_BLOCK, n_p // ROW_BLOCK
    g = gf.reshape(1, D_MODEL)
    seg = [pl.BlockSpec((ROW_BLOCK, D_MODEL), lambda i, s=s, c=c: (jnp.clip(i - s, 0, c - 1), 0))
           for s, c in zip(starts, counts)]
    return pl.pallas_call(
        functools.partial(_final_kernel, starts=tuple(starts), nbp=nbp),
        grid=(n // ROW_BLOCK,),
        in_specs=[*seg, *seg, pl.BlockSpec(g.shape, lambda i: (0, 0))],
        out_specs=list(_pair_specs(nbp, D_MODEL)),
        out_shape=[jax.ShapeDtypeStruct((n_p, D_MODEL), f32), jax.ShapeDtypeStruct((n - n_p, D_MODEL), f32)],
        compiler_params=pltpu.CompilerParams(dimension_semantics=("arbitrary",), vmem_limit_bytes=VMEM_LIMIT),
        name="final_norm",
    )(*peers, *x1s, g)


def _peer_specs():
    row = lambda w: pl.BlockSpec((ROW_BLOCK, w), lambda i: (i, 0))
    idx = [pl.BlockSpec((SUBLANES, ROW_BLOCK), lambda i, k=k: (k, i), memory_space=pltpu.SMEM)
           for k in range(INDEX_GROUPS)]
    tab = pl.BlockSpec(memory_space=pltpu.VMEM)
    gscr = pltpu.VMEM((G_ROWS, LANES), bf16)
    params = pltpu.CompilerParams(dimension_semantics=("arbitrary",), vmem_limit_bytes=VMEM_LIMIT)
    return row, idx, tab, gscr, params


def _expert_table(tab):
    return tab.astype(bf16).reshape(tab.shape[0], ROW_TILES, LANES)


def _peer_u(eidx_t, xn, g, utab, n):
    row, idx, tab, gscr, params = _peer_specs()
    return pl.pallas_call(
        _peer_u_kernel,
        grid=(n // ROW_BLOCK,),
        in_specs=[*idx, row(D_MODEL), row(P_SLOTS), tab],
        out_specs=row(P_SLOTS),
        out_shape=jax.ShapeDtypeStruct((n, P_SLOTS), f32),
        scratch_shapes=[pltpu.VMEM((ROW_BLOCK * 2 * SUBLANES, LANES), f32),
                        pltpu.VMEM((ROW_TILES, ROW_BLOCK * SUBLANES, LANES), f32), gscr, gscr],
        compiler_params=params,
        name="peer_u",
    )(*[eidx_t] * INDEX_GROUPS, xn, g, utab)


def _peer_v(eidx_t, w, vtab):
    n = w.shape[0]
    row, idx, tab, gscr, params = _peer_specs()
    return pl.pallas_call(
        _peer_v_kernel,
        grid=(n // ROW_BLOCK,),
        in_specs=[*idx, row(P_SLOTS), tab],
        out_specs=row(D_MODEL),
        out_shape=jax.ShapeDtypeStruct((n, D_MODEL), f32),
        scratch_shapes=[pltpu.VMEM((ROW_TILES, ROW_BLOCK * SUBLANES, LANES), f32),
                        pltpu.VMEM((ROW_BLOCK * SUBLANES, LANES), f32), gscr, gscr],
        compiler_params=params,
        name="peer_v",
    )(*[eidx_t] * INDEX_GROUPS, w, vtab)


def _sc_table_kernel(t_ref, o_ref):
    t = t_ref[...]
    o_ref[...] = pltpu.pack_elementwise([t[:, :SC_WORDS], t[:, SC_WORDS:]], packed_dtype=bf16)


def _sc_table(tab):
    e = tab.shape[0]
    return pl.pallas_call(
        _sc_table_kernel,
        grid=(e // ROW_BLOCK,),
        in_specs=[pl.BlockSpec((ROW_BLOCK, D_MODEL), lambda i: (i, 0))],
        out_specs=pl.BlockSpec((ROW_BLOCK, SC_WORDS), lambda i: (i, 0)),
        out_shape=jax.ShapeDtypeStruct((e, SC_WORDS), jnp.uint32),
        compiler_params=pltpu.CompilerParams(dimension_semantics=("parallel",), vmem_limit_bytes=VMEM_LIMIT),
        name="sc_table",
    )(tab)


def _sc_unpack(words):
    return plsc.unpack(plsc.bitcast(words, bf16), format=plsc.PackFormat.INTERLEAVED)


def _sc_gelu_tanh(x):
    z = math.sqrt(2.0 / math.pi) * (x + 0.044715 * (x * x * x))
    return 0.5 * x * (2.0 - 2.0 / (jnp.exp(2.0 * z) + 1.0))


def _sc_peer(utab32, vtab32, eidx, xn, gate):
    n_sc = eidx.shape[0]
    per = n_sc // SC_WORKERS
    units = 4 * per
    assert n_sc % (SC_WORKERS * SUBLANES) == 0

    def body(u_hbm, v_hbm, eidx_hbm, x_hbm, g_hbm, out_hbm,
             idx_v, x_v, g_v, rows_v, acc_v, w_v, w16_v, out_v, row_sems, tok_sems, out_sem):
        base = (lax.axis_index("s") * 2 + lax.axis_index("c")) * per
        lanes = lax.broadcasted_iota(i32, (SC_LANES,), 0)

        def token_copies(tok):
            tslot = tok % 2
            return (pltpu.make_async_copy(eidx_hbm.at[base + tok], idx_v.at[tslot], tok_sems.at[0]),
                    pltpu.make_async_copy(x_hbm.at[base + tok], x_v.at[tslot], tok_sems.at[1]),
                    pltpu.make_async_copy(g_hbm.at[base + tok], g_v.at[tslot], tok_sems.at[2]))

        def store_out(tok):
            return pltpu.make_async_copy(out_v, out_hbm.at[base + tok], out_sem)

        def gather(tab_hbm, g):
            tok, k = g // 4, g % 4
            return pltpu.make_async_copy(tab_hbm.at[idx_v.at[tok % 2, pl.ds((k % 2) * SC_UNIT_ROWS, SC_UNIT_ROWS)]],
                                         rows_v.at[k % 2], row_sems.at[k % 2])

        def start(g):
            @pl.when(g % 4 < 2)
            def _():
                gather(u_hbm, g).start()

            @pl.when(g % 4 >= 2)
            def _():
                gather(v_hbm, g).start()

        def compute_u(tslot, half, rows):
            @pl.loop(0, SC_UNIT_ROWS // SC_ROW_GROUP)
            def _(rg):
                slot0 = half * SC_UNIT_ROWS + rg * SC_ROW_GROUP

                @pl.loop(0, SC_CHUNKS // SC_CHUNK_GROUP)
                def _(cg):
                    keep = jnp.where(cg > 0, 1.0, 0.0).astype(f32)
                    accs = [acc_v[pl.ds((slot0 + r) * SC_LANES, SC_LANES)] * keep for r in range(SC_ROW_GROUP)]
                    for c in range(SC_CHUNK_GROUP):
                        ch = cg * SC_CHUNK_GROUP + c
                        xa = x_v[tslot, pl.ds(ch * SC_LANES, SC_LANES)]
                        xb = x_v[tslot, pl.ds(SC_WORDS + ch * SC_LANES, SC_LANES)]
                        for r in range(SC_ROW_GROUP):
                            a, b = _sc_unpack(rows[rg * SC_ROW_GROUP + r, pl.ds(ch * SC_LANES, SC_LANES)])
                            accs[r] = accs[r] + a * xa + b * xb
                    for r in range(SC_ROW_GROUP):
                        acc_v[pl.ds((slot0 + r) * SC_LANES, SC_LANES)] = accs[r]

        def gate_weights(tslot):
            @pl.loop(0, P_SLOTS // SC_LANES)
            def _(sg):
                first = sg * SC_LANES * SC_LANES
                act = jnp.zeros((SC_LANES,), f32)
                for lane in range(SC_LANES):
                    act = act + plsc.load_gather(acc_v, [first + lanes * SC_LANES + lane])
                w_v[pl.ds(sg * SC_LANES, SC_LANES)] = g_v[tslot, pl.ds(sg * SC_LANES, SC_LANES)] * _sc_gelu_tanh(act)

            @pl.loop(0, P_SLOTS // SC_LANES)
            def _(sg):
                for r in range(SC_LANES):
                    w16_v[pl.ds((sg * SC_LANES + r) * SC_LANES, SC_LANES)] = plsc.load_gather(
                        w_v, [jnp.zeros((SC_LANES,), i32) + (sg * SC_LANES + r)])

        def compute_v(half, rows):
            @pl.loop(0, SC_CHUNKS // SC_CHUNK_GROUP)
            def _(cg):
                first = cg * SC_CHUNK_GROUP

                @pl.loop(0, SC_UNIT_ROWS // SC_ROW_GROUP)
                def _(rg):
                    keep = jnp.where(jnp.logical_or(half == 1, rg > 0), 1.0, 0.0).astype(f32)
                    spots = [pl.ds((j % 2) * SC_WORDS + (first + j // 2) * SC_LANES, SC_LANES)
                             for j in range(2 * SC_CHUNK_GROUP)]
                    accs = [out_v[spot] * keep for spot in spots]
                    for r in range(SC_ROW_GROUP):
                        row = rg * SC_ROW_GROUP + r
                        wv = w16_v[pl.ds((half * SC_UNIT_ROWS + row) * SC_LANES, SC_LANES)]
                        for c in range(SC_CHUNK_GROUP):
                            a, b = _sc_unpack(rows[row, pl.ds((first + c) * SC_LANES, SC_LANES)])
                            accs[2 * c] = accs[2 * c] + a * wv
                            accs[2 * c + 1] = accs[2 * c + 1] + b * wv
                    for spot, acc in zip(spots, accs):
                        out_v[spot] = acc

        for cp in token_copies(0):
            cp.start()
        for cp in token_copies(0):
            cp.wait()
        start(0)

        @pl.loop(0, units)
        def _(g):
            tok, k = g // 4, g % 4

            @pl.when(jnp.logical_and(k == 0, tok + 1 < per))
            def _():
                for cp in token_copies(tok + 1):
                    cp.start()

            @pl.when(jnp.logical_and(k == 3, tok + 1 < per))
            def _():
                for cp in token_copies(tok + 1):
                    cp.wait()

            @pl.when(g + 1 < units)
            def _():
                start(g + 1)

            gather(u_hbm, g).wait()
            rows = rows_v.at[k % 2]

            @pl.when(k < 2)
            def _():
                compute_u(tok % 2, k, rows)

            @pl.when(k == 1)
            def _():
                gate_weights(tok % 2)

            @pl.when(jnp.logical_and(k == 2, tok > 0))
            def _():
                store_out(tok - 1).wait()

            @pl.when(k >= 2)
            def _():
                compute_v(k - 2, rows)

            @pl.when(k == 3)
            def _():
                store_out(tok).start()

        store_out(per - 1).wait()

    return pl.kernel(
        body, mesh=plsc.VectorSubcoreMesh(core_axis_name="c", subcore_axis_name="s"),
        out_type=jax.ShapeDtypeStruct((n_sc, D_MODEL), f32),
        scratch_types=[pltpu.VMEM((2, P_SLOTS), i32), pltpu.VMEM((2, D_MODEL), f32), pltpu.VMEM((2, P_SLOTS), f32),
                       pltpu.VMEM((2, SC_UNIT_ROWS, SC_WORDS), jnp.uint32), pltpu.VMEM((P_SLOTS * SC_LANES,), f32),
                       pltpu.VMEM((P_SLOTS,), f32), pltpu.VMEM((P_SLOTS * SC_LANES,), f32), pltpu.VMEM((D_MODEL,), f32),
                       pltpu.SemaphoreType.DMA((2,)), pltpu.SemaphoreType.DMA((3,)), pltpu.SemaphoreType.DMA],
        compiler_params=pltpu.CompilerParams(needs_layout_passes=False),
        name="sc_peer",
    )(utab32, vtab32, eidx, xn, gate)


def kernel(x_prompt, x_sample, cache_k, cache_v, state_C, state_n, state_m, state_conv, norm1_g, w_in, b_gates, rel_bias, conv_w, conv_b, mh_norm_g, w_out, norm2_g, peer_wq, peer_keys, peer_u, peer_v, final_g):
    bp, sp, d = x_prompt.shape
    bs, ts, _ = x_sample.shape
    n_p, n_s = bp * sp, bs * ts
    n = n_p + n_s
    assert n_p % ROW_BLOCK == 0 and n_s % ROW_BLOCK == 0 and d == D_MODEL
    depth = w_in.shape[0]
    assert depth == 1, "the final norm is fused into the last layer's PEER pass"
    l = 0
    xp, xs = x_prompt.reshape(n_p, d), x_sample.reshape(n_s, d)

    aq, ak, av, mqk, mv, mo, gates = _inproj(xp, xs, norm1_g[l], w_in[l])
    zeros = lambda *shp: jnp.zeros(shp, f32)
    mparams = (conv_w[l], conv_b[l], b_gates[l], mh_norm_g[l])
    retrieval = (w_out[l], norm2_g[l], peer_wq[l], peer_keys[l])

    def mixers(seq, lo, hi, state):
        att = _attn_prompt(aq, ak, av, rel_bias[l], seq, sp, lo, hi)
        h, c, nn, mm = _mlstm(mqk, mv, mo, gates, seq * sp + lo, 1, hi - lo, *state, *mparams)
        conv_rows = mqk[seq * sp + hi - (CONV_W - 1):seq * sp + hi][None]
        return att, h, (c, nn[:, :M_HEADS], mm[:, :M_HEADS, 0], conv_rows)

    fresh = (zeros(1, M_HEADS, M_DH, M_DH), zeros(1, M_HEADS, M_DH), zeros(1, M_HEADS), zeros(1, CONV_W - 1, 2 * M_WIDTH))

    assert bp == 2 and 0 < SC_FIRST_ROWS < sp and 0 < SC_EXTRA_ROWS < sp
    n_sc = sp + SC_EXTRA_ROWS
    sc_tables = _sc_table(peer_u[l]), _sc_table(peer_v[l])
    att_0a, h_0a, state_0a = mixers(0, 0, SC_FIRST_ROWS, fresh)
    x1_a, xn_a, eidx_a, _, gate_a = _mid(xp, 0, xp, 0, att_0a, att_0a, h_0a, h_0a, SC_FIRST_ROWS, 0, *retrieval)
    peer_a = _sc_peer(*sc_tables, eidx_a, xn_a, gate_a)

    att_0b, h_0b, state_0 = mixers(0, SC_FIRST_ROWS, sp, state_0a)
    att_1a, h_1a, state_1a = mixers(1, 0, SC_EXTRA_ROWS, fresh)
    x1_c, xn_c, eidx_c, _, gate_c = _mid(xp, SC_FIRST_ROWS, xp, sp, att_0b, att_1a, h_0b, h_1a,
                                         sp - SC_FIRST_ROWS, SC_EXTRA_ROWS, *retrieval)
    peer_c = _sc_peer(*sc_tables, eidx_c, xn_c, gate_c)

    att_1b, h_1b, state_1 = mixers(1, SC_EXTRA_ROWS, sp, state_1a)
    lcache = cache_k.shape[2]
    att_s = _attn_sample(aq, ak, av, cache_k[l].reshape(bs, lcache, A_WIDTH),
                         cache_v[l].reshape(bs, lcache, A_WIDTH), rel_bias[l], n_p, bs, ts)
    h_s, c_s, nn_s, mm_s = _mlstm(mqk, mv, mo, gates, n_p, bs, ts, state_C[l], state_n[l], state_m[l],
                                  state_conv[l], *mparams)
    x1_b, xn_b, _, eidx_tb, gate_b = _mid(xp, n_sc, xs, 0, att_1b, att_s, h_1b, h_s, n_p - n_sc, n_s, *retrieval)
    w_b = _peer_u(eidx_tb, xn_b, gate_b, _expert_table(peer_u[l]), n - n_sc)
    peer_b = _peer_v(eidx_tb, w_b, _expert_table(peer_v[l]))
    y_p, y_s = _final((peer_a, peer_c, peer_b), (x1_a, x1_c, x1_b), final_g, n_p)
    c_p, nn_p, mm_p = (jnp.concatenate(ab, axis=0) for ab in zip(state_0[:3], state_1[:3]))

    def tail(a, row0, bsz, t, keep):
        return jnp.stack([a[row0 + (b + 1) * t - keep:row0 + (b + 1) * t] for b in range(bsz)])

    keep = min(WINDOW, sp)
    heads = lambda a: a.reshape(a.shape[0], a.shape[1], A_HEADS, A_DH)
    ctail = CONV_W - 1
    conv_tail = lambda buf, a, row0, bsz, t: jnp.concatenate([buf.astype(a.dtype), tail(a, row0, bsz, t, min(ctail, t))],
                                                             axis=1)[:, -ctail:]
    st = lambda a: a[None]
    return (y_p.reshape(bp, sp, d), y_s.reshape(bs, ts, d),
            st(heads(tail(ak, 0, bp, sp, keep))), st(heads(tail(av, 0, bp, sp, keep))),
            st(c_p), st(nn_p), st(mm_p),
            st(conv_tail(zeros(bp, ctail, 2 * M_WIDTH), mqk, 0, bp, sp)),
            st(heads(ak[n_p:].reshape(bs, ts, A_WIDTH))), st(heads(av[n_p:].reshape(bs, ts, A_WIDTH))),
            st(c_s), st(nn_s[:, :M_HEADS]), st(mm_s[:, :M_HEADS, 0]),
            st(conv_tail(state_conv[l], mqk, n_p, bs, ts)))
```

```python
import functools
import math

import jax
import jax.numpy as jnp
from jax import lax
from jax.experimental import pallas as pl
from jax.experimental.pallas import tpu as pltpu
from jax.experimental.pallas import tpu_sc as plsc

f32 = jnp.float32
bf16 = jnp.bfloat16
i32 = jnp.int32

D_MODEL = 1024
CHUNK = 64
A_HEADS = 8
A_DH = 64
A_WIDTH = A_HEADS * A_DH
BAND_CHUNKS = 8
WINDOW = BAND_CHUNKS * CHUNK
MAX_REL = 128
ATT_SCALE = A_DH ** -0.5
M_HEADS = 4
M_DH = 128
M_WIDTH = M_HEADS * M_DH
CONV_W = 4
P_HEADS = 8
P_DKEY = 256
N_KEYS = 128
P_TOPK = 16
P_SLOTS = P_HEADS * P_TOPK
EPS = 1e-6
NEG = -1e30

LANES = 128
SUBLANES = 8
ROW_BLOCK = 256
ATT_TILE = 512
ATT_SUB = 128
ATT_KEYS = ATT_SUB + WINDOW
ROW_TILES = D_MODEL // LANES
G_ROWS = P_SLOTS * ROW_TILES
TOKEN_UNROLL = 8
INDEX_GROUPS = P_SLOTS // SUBLANES
VMEM_LIMIT = 56 * 1024 * 1024

SC_WORKERS = 32
SC_LANES = 16
SC_UNIT_ROWS = P_SLOTS // 2
SC_ROW_GROUP = 16
SC_WORDS = D_MODEL // 2
SC_CHUNKS = SC_WORDS // SC_LANES
SC_CHUNK_GROUP = 8
SC_FIRST_ROWS = 4096
SC_EXTRA_ROWS = 2560


def _rms(x, g):
    return x * lax.rsqrt(jnp.mean(x * x, axis=-1, keepdims=True) + EPS) * g


def _split_bf16(x):
    hi = x.astype(bf16)
    lo = (x - hi.astype(f32)).astype(bf16)
    return hi, lo


def _pair_specs(nbp, width, first=0, second=0):
    return (pl.BlockSpec((ROW_BLOCK, width), lambda i: (first + jnp.minimum(i, nbp - 1), 0)),
            pl.BlockSpec((ROW_BLOCK, width), lambda i: (second + jnp.maximum(i - nbp, 0), 0)))


def _pick(nbp, p_ref, s_ref):
    return jnp.where(pl.program_id(0) < nbp, p_ref[...], s_ref[...])


def _inproj_kernel(xp_ref, xs_ref, g_ref, w_ref, wgh_ref, wgl_ref,
                   aq_ref, ak_ref, av_ref, mqk_ref, mv_ref, mo_ref, gate_ref, *, nbp):
    xn = _rms(_pick(nbp, xp_ref, xs_ref), g_ref[...])
    xh, xl = _split_bf16(xn)

    def proj(lo, hi):
        return jnp.dot(xh, w_ref[:, lo:hi], preferred_element_type=f32)

    aq_ref[...] = proj(0, 512)
    ak_ref[...] = proj(512, 1024)
    av_ref[...] = proj(1024, 1536)
    mqk_ref[...] = proj(1536, 2560)
    mv_ref[...] = proj(2560, 3072)
    mo_ref[...] = proj(3072, 3584)
    gate_ref[...] = (jnp.dot(xh, wgh_ref[...], preferred_element_type=f32)
                     + jnp.dot(xl, wgh_ref[...], preferred_element_type=f32)
                     + jnp.dot(xh, wgl_ref[...], preferred_element_type=f32))


def _inproj(xp, xs, g1, w_in):
    n = xp.shape[0] + xs.shape[0]
    nbp = xp.shape[0] // ROW_BLOCK
    main = 3 * A_WIDTH + 4 * M_WIDTH
    w_main = w_in[:, :main].astype(bf16)
    wg = jnp.pad(w_in[:, main:], ((0, 0), (0, LANES - 2 * M_HEADS)))
    wgh, wgl = _split_bf16(wg)
    widths = (512, 512, 512, 1024, 512, 512, LANES)
    row = lambda w: pl.BlockSpec((ROW_BLOCK, w), lambda i: (i, 0))
    full = lambda a: pl.BlockSpec(a.shape, lambda i: (0,) * a.ndim)
    g = g1.reshape(1, D_MODEL)
    return pl.pallas_call(
        functools.partial(_inproj_kernel, nbp=nbp),
        grid=(n // ROW_BLOCK,),
        in_specs=[*_pair_specs(nbp, D_MODEL), full(g), full(w_main), full(wgh), full(wgl)],
        out_specs=[row(w) for w in widths],
        out_shape=[jax.ShapeDtypeStruct((n, w), f32) for w in widths],
        compiler_params=pltpu.CompilerParams(dimension_semantics=("parallel",), vmem_limit_bytes=VMEM_LIMIT),
        name="inproj",
    )(xp, xs, g, w_main, wgh, wgl)


def _attn_heads(q, k, v, bias_ref, key_ok):
    outs = []
    for h in range(A_HEADS):
        sl = slice(h * A_DH, (h + 1) * A_DH)
        s = lax.dot_general(q[:, sl], k[:, sl], (((1,), (1,)), ((), ())), preferred_element_type=f32)
        s = s * ATT_SCALE + bias_ref[h]
        if key_ok is not None:
            s = jnp.where(key_ok, s, NEG)
        m = jnp.max(s, axis=-1, keepdims=True)
        p = jnp.exp(s - m)
        l = jnp.sum(p, axis=-1, keepdims=True)
        o = jnp.dot(p.astype(bf16), v[:, sl], preferred_element_type=f32)
        outs.append(o / l)
    return jnp.concatenate(outs, axis=-1)


def _attn_prompt_kernel(q_ref, k0_ref, k1_ref, v0_ref, v1_ref, bias_ref, o_ref, *, t0):
    t = t0 + pl.program_id(1)
    q = q_ref[...].astype(bf16)
    k = jnp.concatenate([k0_ref[...], k1_ref[...]], axis=0).astype(bf16)
    v = jnp.concatenate([v0_ref[...], v1_ref[...]], axis=0).astype(bf16)
    col = lax.broadcasted_iota(i32, (1, ATT_KEYS), 1)
    for s in range(ATT_TILE // ATT_SUB):
        lo = s * ATT_SUB
        key_ok = (t * ATT_TILE + lo + col) >= WINDOW
        o_ref[lo:lo + ATT_SUB, :] = _attn_heads(q[lo:lo + ATT_SUB], k[lo:lo + ATT_KEYS], v[lo:lo + ATT_KEYS],
                                                 bias_ref, key_ok)


def _attn_sample_kernel(q_ref, k_ref, v_ref, bias_ref, o_ref):
    o_ref[...] = _attn_heads(q_ref[...].astype(bf16), k_ref[0].astype(bf16), v_ref[0].astype(bf16), bias_ref, None)


def _rel_bias_table(rel_bias, rows, cols, offset, valid):
    span = rows + cols - 1
    rel = offset + rows - 1 - jnp.arange(span)
    diag = rel_bias[:, jnp.clip(rel, -MAX_REL, MAX_REL) + MAX_REL].astype(f32)
    diag = jnp.pad(diag, ((0, 0), (0, 1)))
    flat = jnp.tile(diag, (1, rows))[:, rows - 1:rows - 1 + rows * span]
    return jnp.where(valid[None], flat.reshape(-1, rows, span)[:, :, :cols], NEG)


def _attn_prompt(q, k, v, rel_bias, seq, s, lo, hi):
    assert s % ATT_TILE == 0 and WINDOW == ATT_TILE and lo % ATT_TILE == 0 and hi % ATT_TILE == 0
    nt, t0, cnt = s // ATT_TILE, lo // ATT_TILE, (hi - lo) // ATT_TILE
    i = jnp.arange(ATT_SUB)[:, None]
    j = jnp.arange(ATT_KEYS)[None, :]
    off = j - (i // CHUNK) * CHUNK
    bias = _rel_bias_table(rel_bias, ATT_SUB, ATT_KEYS, WINDOW, (off >= 0) & (off < WINDOW + CHUNK))
    cur = pl.BlockSpec((ATT_TILE, A_WIDTH), lambda b, t: (seq * nt + t0 + t, 0))
    prev = pl.BlockSpec((ATT_TILE, A_WIDTH), lambda b, t: (seq * nt + jnp.maximum(t0 + t - 1, 0), 0))
    return pl.pallas_call(
        functools.partial(_attn_prompt_kernel, t0=t0),
        grid=(1, cnt),
        in_specs=[cur, prev, cur, prev, cur, pl.BlockSpec(bias.shape, lambda b, t: (0, 0, 0))],
        out_specs=pl.BlockSpec((ATT_TILE, A_WIDTH), lambda b, t: (t, 0)),
        out_shape=jax.ShapeDtypeStruct((hi - lo, A_WIDTH), f32),
        compiler_params=pltpu.CompilerParams(dimension_semantics=("parallel", "parallel"),
                                             vmem_limit_bytes=VMEM_LIMIT),
        name="attn_prompt",
    )(q, k, k, v, v, bias)


def _attn_sample(q, k, v, ck, cv, rel_bias, row0, bsz, t):
    l = ck.shape[1]
    assert row0 % t == 0
    keys = -(-(l + t) // LANES) * LANES
    padk = ((0, 0), (0, keys - l - t), (0, 0))
    kk = jnp.pad(jnp.concatenate([ck, k[row0:].reshape(bsz, t, A_WIDTH)], axis=1), padk)
    vv = jnp.pad(jnp.concatenate([cv, v[row0:].reshape(bsz, t, A_WIDTH)], axis=1), padk)
    j = jnp.arange(keys)[None, :]
    bias = _rel_bias_table(rel_bias, t, keys, l, jnp.broadcast_to(j < l + t, (t, keys)))
    return pl.pallas_call(
        _attn_sample_kernel,
        grid=(bsz,),
        in_specs=[pl.BlockSpec((t, A_WIDTH), lambda b: (row0 // t + b, 0)),
                  pl.BlockSpec((1, keys, A_WIDTH), lambda b: (b, 0, 0)),
                  pl.BlockSpec((1, keys, A_WIDTH), lambda b: (b, 0, 0)),
                  pl.BlockSpec(bias.shape, lambda b: (0, 0, 0))],
        out_specs=pl.BlockSpec((t, A_WIDTH), lambda b: (b, 0)),
        out_shape=jax.ShapeDtypeStruct((bsz * t, A_WIDTH), f32),
        compiler_params=pltpu.CompilerParams(dimension_semantics=("parallel",), vmem_limit_bytes=VMEM_LIMIT),
        name="attn_sample",
    )(q, kk, vv, bias)


def _mlstm_chunk(a, vall, o_in, gate_in, cw_ref, cb_ref, bg_ref, mhg_ref, c_s, n_s, m_s, prev_s):
    lc = a.shape[0]
    ext = jnp.concatenate([prev_s[...], a], axis=0)
    conv = cb_ref[...]
    for j in range(CONV_W):
        lo = SUBLANES - (CONV_W - 1) + j
        conv = conv + cw_ref[j:j + 1, :] * ext[lo:lo + lc]
    prev_s[...] = a[lc - SUBLANES:lc]
    qk = conv * jax.nn.sigmoid(conv)

    z = gate_in + bg_ref[...]
    lane = lax.broadcasted_iota(i32, (lc, LANES), 1)
    row = lax.broadcasted_iota(i32, (lc, LANES), 0)
    logf = jnp.minimum(z, 0.0) - jnp.log1p(jnp.exp(-jnp.abs(z)))
    cum = jnp.where((lane >= M_HEADS) & (lane < 2 * M_HEADS), logf, 0.0)
    shift = 1
    while shift < lc:
        cum = cum + jnp.where(row >= shift, pltpu.roll(cum, shift, axis=0), 0.0)
        shift *= 2
    zc = jnp.where(lane < M_HEADS, z, cum)
    zt = jnp.concatenate([zc, jnp.zeros((LANES - lc, LANES), f32)], axis=0).T[:, :lc]

    ri = lax.broadcasted_iota(i32, (lc, lc), 0)
    ci = lax.broadcasted_iota(i32, (lc, lc), 1)
    causal = ri >= ci
    state = [(m_s[h:h + 1, 0:1], c_s[h], n_s[h:h + 1, :]) for h in range(M_HEADS)]
    hs, new_state = [], []
    for h in range(M_HEADS):
        sl = slice(h * M_DH, (h + 1) * M_DH)
        q = qk[:, sl]
        k = qk[:, M_WIDTH + h * M_DH:M_WIDTH + (h + 1) * M_DH] * (M_DH ** -0.5)
        v = vall[:, sl]
        i_col = zc[:, h:h + 1]
        b_col = zc[:, M_HEADS + h:M_HEADS + h + 1]
        i_row = zt[h:h + 1, :]
        b_row = zt[M_HEADS + h:M_HEADS + h + 1, :]
        m_prev, c_prev, n_prev = state[h]

        dmat = jnp.where(causal, b_col - b_row + i_row, NEG)
        inter = b_col + m_prev
        mt = jnp.maximum(jnp.max(dmat, axis=-1, keepdims=True), inter)
        w_intra = jnp.exp(dmat - mt)
        w_inter = jnp.exp(inter - mt)
        qb, kb, vb = q.astype(bf16), k.astype(bf16), v.astype(bf16)
        s = lax.dot_general(qb, kb, (((1,), (1,)), ((), ())), preferred_element_type=f32) * w_intra
        num = (w_inter * jnp.dot(qb, c_prev.astype(bf16), preferred_element_type=f32)
               + jnp.dot(s.astype(bf16), vb, preferred_element_type=f32))
        den = w_inter * jnp.sum(q * n_prev, axis=-1, keepdims=True) + jnp.sum(s, axis=-1, keepdims=True)
        hh = num / jnp.maximum(jnp.abs(den), jnp.exp(-mt))
        m_new = mt[lc - 1:lc, :]
        b_last = b_col[lc - 1:lc, :]
        w_s = jnp.exp(b_last - b_col + i_col - m_new)
        decay = jnp.exp(b_last + m_prev - m_new)
        kw = k * w_s
        new_state.append((jnp.broadcast_to(m_new, (1, LANES)),
                          decay * c_prev + lax.dot_general(kw.astype(bf16), vb, (((0,), (0,)), ((), ())),
                                                           preferred_element_type=f32),
                          decay * n_prev + jnp.sum(kw, axis=0, keepdims=True)))
        hs.append(hh * lax.rsqrt(jnp.mean(hh * hh, axis=-1, keepdims=True) + EPS))

    for h, (m_new, c_new, n_new) in enumerate(new_state):
        m_s[h:h + 1, :] = m_new
        c_s[h] = c_new
        n_s[h:h + 1, :] = n_new
    return jnp.concatenate(hs, axis=-1) * mhg_ref[...] * jax.nn.sigmoid(o_in)


def _mlstm_kernel(qk_ref, v_ref, o_ref, gate_ref, c0_ref, n0_ref, m0_ref, cbuf_ref,
                  cw_ref, cb_ref, bg_ref, mhg_ref,
                  h_ref, cout_ref, nout_ref, mout_ref,
                  c_s, n_s, m_s, prev_s):
    c = pl.program_id(1)

    @pl.when(c == 0)
    def _():
        c_s[...] = c0_ref[0]
        n_s[...] = n0_ref[0]
        m_s[...] = m0_ref[0]
        prev_s[...] = cbuf_ref[0]

    h_ref[...] = _mlstm_chunk(qk_ref[...], v_ref[...], o_ref[...], gate_ref[...],
                              cw_ref, cb_ref, bg_ref, mhg_ref, c_s, n_s, m_s, prev_s)

    @pl.when(c == pl.num_programs(1) - 1)
    def _():
        cout_ref[0] = c_s[...]
        nout_ref[0] = n_s[...]
        mout_ref[0] = m_s[...]


def _mlstm(mqk, mv, mo, gates, row0, bsz, t, c0, n0, m0, cbuf, conv_w, conv_b, b_gates, mh_g):
    lc = min(CHUNK, t)
    step = lc
    nc = t // step
    assert t % step == 0 and lc % SUBLANES == 0 and row0 % step == 0
    n0p = jnp.pad(n0.astype(f32), ((0, 0), (0, SUBLANES - M_HEADS), (0, 0)))
    m0p = jnp.pad(jnp.broadcast_to(m0.astype(f32)[:, :, None], (bsz, M_HEADS, LANES)),
                  ((0, 0), (0, SUBLANES - M_HEADS), (0, 0)))
    cbp = jnp.pad(cbuf.astype(f32), ((0, 0), (SUBLANES - (CONV_W - 1), 0), (0, 0)))
    bg = jnp.pad(b_gates.astype(f32), (0, LANES - 2 * M_HEADS)).reshape(1, LANES)
    seq = lambda w: pl.BlockSpec((step, w), lambda b, c: (row0 // step + b * nc + c, 0))
    out_seq = pl.BlockSpec((step, M_WIDTH), lambda b, c: (b * nc + c, 0))
    per_b = lambda shp: pl.BlockSpec((1,) + shp, lambda b, c: (b,) + (0,) * len(shp))
    full = lambda a: pl.BlockSpec(a.shape, lambda b, c: (0,) * a.ndim)
    cb = conv_b.reshape(1, -1)
    mhg = mh_g.reshape(1, -1)
    return pl.pallas_call(
        _mlstm_kernel,
        grid=(bsz, nc),
        in_specs=[seq(2 * M_WIDTH), seq(M_WIDTH), seq(M_WIDTH), seq(LANES),
                  per_b((M_HEADS, M_DH, M_DH)), per_b((SUBLANES, M_DH)), per_b((SUBLANES, LANES)),
                  per_b((SUBLANES, 2 * M_WIDTH)),
                  full(conv_w), full(cb), full(bg), full(mhg)],
        out_specs=[out_seq, per_b((M_HEADS, M_DH, M_DH)), per_b((SUBLANES, M_DH)), per_b((SUBLANES, LANES))],
        out_shape=[jax.ShapeDtypeStruct((bsz * t, M_WIDTH), f32),
                   jax.ShapeDtypeStruct((bsz, M_HEADS, M_DH, M_DH), f32),
                   jax.ShapeDtypeStruct((bsz, SUBLANES, M_DH), f32),
                   jax.ShapeDtypeStruct((bsz, SUBLANES, LANES), f32)],
        scratch_shapes=[pltpu.VMEM((M_HEADS, M_DH, M_DH), f32), pltpu.VMEM((SUBLANES, M_DH), f32),
                        pltpu.VMEM((SUBLANES, LANES), f32), pltpu.VMEM((SUBLANES, 2 * M_WIDTH), f32)],
        compiler_params=pltpu.CompilerParams(dimension_semantics=("parallel", "arbitrary"),
                                             vmem_limit_bytes=VMEM_LIMIT),
        name="mlstm",
    )(mqk, mv, mo, gates, c0.astype(f32), n0p, m0p, cbp, conv_w, cb, bg, mhg)


def _topk_rows(s, k, payload=None):
    n = s.shape[0]
    rows = lax.broadcasted_iota(i32, s.shape, 0).astype(f32)
    vals, ids = [], []
    for _ in range(k):
        m = jnp.max(s, axis=0, keepdims=True)
        pos = jnp.min(jnp.where(s == m, rows, float(n)), axis=0, keepdims=True)
        sel = rows == pos
        vals.append(m)
        ids.append(pos if payload is None else jnp.max(jnp.where(sel, payload, -1.0), axis=0, keepdims=True))
        s = jnp.where(sel, -jnp.inf, s)
    return jnp.concatenate(vals, axis=0), jnp.concatenate(ids, axis=0)


def _mid_kernel(xp_ref, xs_ref, attp_ref, atts_ref, hp_ref, hs_ref, wo_ref, g2_ref, wq_ref, keys_ref,
                x1_ref, xn_ref, eidx_ref, eidx_t_ref, gate_ref, *, nbp):
    cat = jnp.concatenate([_pick(nbp, attp_ref, atts_ref), _pick(nbp, hp_ref, hs_ref)], axis=-1).astype(bf16)
    x1 = _pick(nbp, xp_ref, xs_ref) + jnp.dot(cat, wo_ref[...], preferred_element_type=f32)
    x1_ref[...] = x1
    xn = _rms(x1, g2_ref[...])
    xn_ref[...] = xn
    xb = xn.astype(bf16)
    e_rows, g_rows = [], []
    for h in range(P_HEADS):
        q = jnp.dot(xb, wq_ref[:, h * P_DKEY:(h + 1) * P_DKEY], preferred_element_type=f32).astype(bf16)
        half = []
        for p in range(2):
            st = lax.dot_general(keys_ref[h, p], q[:, p * N_KEYS:(p + 1) * N_KEYS], (((1,), (1,)), ((), ())),
                                 preferred_element_type=f32)
            half.append(_topk_rows(st, P_TOPK))
        (v0, i0), (v1, i1) = half
        width = [P_TOPK // (a + 1) for a in range(P_TOPK)]
        fill = -sum(width) % SUBLANES
        cand = jnp.concatenate([v0[a:a + 1] + v1[:width[a]] for a in range(P_TOPK)]
                               + [jnp.full((fill, v0.shape[1]), -jnp.inf, f32)], axis=0)
        cidx = jnp.concatenate([i0[a:a + 1] * float(N_KEYS) + i1[:width[a]] for a in range(P_TOPK)]
                               + [jnp.zeros((fill, v0.shape[1]), f32)], axis=0)
        top_s, eid = _topk_rows(cand, P_TOPK, payload=cidx)
        ex = jnp.exp(top_s - top_s[0:1])
        e_rows.append(eid)
        g_rows.append(ex / jnp.sum(ex, axis=0, keepdims=True))
    eidx_t = jnp.concatenate(e_rows, axis=0)
    eidx_t_ref[...] = eidx_t.astype(i32)
    eidx_ref[...] = eidx_t.T.astype(i32)
    gate_ref[...] = jnp.concatenate(g_rows, axis=0).T


def _mid(xp, xp_row0, xs, xs_row0, attp, atts, hp, hs, n_a, n_b, w_out, g2, wq, keys):
    n = n_a + n_b
    nbp = n_a // ROW_BLOCK
    assert n_a % ROW_BLOCK == 0 and n_b % ROW_BLOCK == 0 and xp_row0 % ROW_BLOCK == 0 and xs_row0 % ROW_BLOCK == 0
    row = lambda w: pl.BlockSpec((ROW_BLOCK, w), lambda i: (i, 0))
    full = lambda a: pl.BlockSpec(a.shape, lambda i: (0,) * a.ndim)
    wo = w_out.astype(bf16)
    wqb = wq.astype(bf16)
    kb = keys.astype(bf16)
    g = g2.reshape(1, D_MODEL)
    return pl.pallas_call(
        functools.partial(_mid_kernel, nbp=nbp),
        grid=(n // ROW_BLOCK,),
        in_specs=[*_pair_specs(nbp, D_MODEL, xp_row0 // ROW_BLOCK, xs_row0 // ROW_BLOCK), *_pair_specs(nbp, A_WIDTH),
                  *_pair_specs(nbp, M_WIDTH), full(wo), full(g), full(wqb), full(kb)],
        out_specs=[row(D_MODEL), row(D_MODEL), row(P_SLOTS), pl.BlockSpec((P_SLOTS, ROW_BLOCK), lambda i: (0, i)),
                   row(P_SLOTS)],
        out_shape=[jax.ShapeDtypeStruct((n, D_MODEL), f32), jax.ShapeDtypeStruct((n, D_MODEL), f32),
                   jax.ShapeDtypeStruct((n, P_SLOTS), i32), jax.ShapeDtypeStruct((P_SLOTS, n), i32),
                   jax.ShapeDtypeStruct((n, P_SLOTS), f32)],
        compiler_params=pltpu.CompilerParams(dimension_semantics=("parallel",), vmem_limit_bytes=VMEM_LIMIT),
        name="outproj_retrieve",
    )(xp, xs, attp, atts, hp, hs, wo, g, wqb, kb)


def _gather_rows(eidx_refs, tab_ref, g_ref, t):
    for i in range(SUBLANES):
        for k, eidx_ref in enumerate(eidx_refs):
            r = k * SUBLANES + i
            g_ref[pl.ds(r * ROW_TILES, ROW_TILES), :] = tab_ref[eidx_ref[i, t]]


def _tile_rows(t, rows=SUBLANES):
    return pl.ds(pl.multiple_of(t * rows, rows), rows)


def _pipelined_tokens(nt, eidx_refs, tab_ref, g0_s, g1_s, compute):
    bufs = (g0_s, g1_s)
    _gather_rows(eidx_refs, tab_ref, g0_s, 0)

    def body(j, carry):
        for u in range(TOKEN_UNROLL):
            t = TOKEN_UNROLL * j + u
            _gather_rows(eidx_refs, tab_ref, bufs[(u + 1) % 2], jnp.minimum(t + 1, nt - 1))
            compute(t, bufs[u % 2])
        return carry

    lax.fori_loop(0, nt // TOKEN_UNROLL, body, 0)


def _gelu_tanh(x):
    return 0.5 * x * (1.0 + jnp.tanh(math.sqrt(2.0 / math.pi) * (x + 0.044715 * (x * x * x))))


def _peer_u_kernel(*refs):
    eidx_refs, (xn_ref, g_ref, tab_ref, w_ref, xl_s, r_s, g0_s, g1_s) = refs[:INDEX_GROUPS], refs[INDEX_GROUPS:]
    nt = xn_ref.shape[0]
    xn = xn_ref[...]
    xh = xn.astype(bf16).astype(f32)
    xl = xn - xh
    for k in range(ROW_TILES):
        xl_s[pl.ds(k, nt, stride=2 * SUBLANES), :] = xh[:, k * LANES:(k + 1) * LANES]
        xl_s[pl.ds(SUBLANES + k, nt, stride=2 * SUBLANES), :] = xl[:, k * LANES:(k + 1) * LANES]
    diag = (lax.broadcasted_iota(i32, (SUBLANES, G_ROWS), 1) % ROW_TILES
            == lax.broadcasted_iota(i32, (SUBLANES, G_ROWS), 0))

    def compute(t, g_s):
        lhs = xl_s[_tile_rows(t, 2 * SUBLANES), :].astype(bf16)
        out = lax.dot_general(lhs, g_s[...], (((1,), (1,)), ((), ())), preferred_element_type=f32)
        part = jnp.where(diag, out[:SUBLANES] + out[SUBLANES:], 0.0)
        for c in range(ROW_TILES):
            r_s[c, _tile_rows(t), :] = part[:, c * LANES:(c + 1) * LANES]

    _pipelined_tokens(nt, eidx_refs, tab_ref, g0_s, g1_s, compute)
    cols = []
    for c in range(ROW_TILES):
        acc = r_s[c, pl.ds(0, nt, stride=SUBLANES), :]
        for k in range(1, SUBLANES):
            acc = acc + r_s[c, pl.ds(k, nt, stride=SUBLANES), :]
        cols.append(acc)
    s = jnp.concatenate(cols, axis=-1)
    fold = (lax.broadcasted_iota(i32, (G_ROWS, P_SLOTS), 0) // ROW_TILES
            == lax.broadcasted_iota(i32, (G_ROWS, P_SLOTS), 1)).astype(bf16)
    sh, sl = _split_bf16(s)
    act = jnp.dot(sh, fold, preferred_element_type=f32) + jnp.dot(sl, fold, preferred_element_type=f32)
    w_ref[...] = g_ref[...] * _gelu_tanh(act)


def _peer_v_kernel(*refs):
    eidx_refs, (w_ref, tab_ref, peer_ref, wl_s, o_s, g0_s, g1_s) = refs[:INDEX_GROUPS], refs[INDEX_GROUPS:]
    nt = w_ref.shape[0]
    spread = (lax.broadcasted_iota(i32, (P_SLOTS, G_ROWS), 1) // ROW_TILES
              == lax.broadcasted_iota(i32, (P_SLOTS, G_ROWS), 0)).astype(bf16)
    wexp = jnp.dot(w_ref[...].astype(bf16), spread, preferred_element_type=f32)
    lane = lax.broadcasted_iota(i32, (nt, LANES), 1)
    for c in range(ROW_TILES):
        wc = wexp[:, c * LANES:(c + 1) * LANES]
        for k in range(SUBLANES):
            wl_s[c, pl.ds(k, nt, stride=SUBLANES), :] = jnp.where(lane % ROW_TILES == k, wc, 0.0)

    def compute(t, g_s):
        lhs = jnp.concatenate([wl_s[c, _tile_rows(t), :] for c in range(ROW_TILES)], axis=-1).astype(bf16)
        o_s[_tile_rows(t), :] = jnp.dot(lhs, g_s[...], preferred_element_type=f32)

    _pipelined_tokens(nt, eidx_refs, tab_ref, g0_s, g1_s, compute)
    for k in range(ROW_TILES):
        peer_ref[:, k * LANES:(k + 1) * LANES] = o_s[pl.ds(k, nt, stride=SUBLANES), :]


def _final_kernel(*refs, starts, nbp):
    k = len(starts)
    peer_refs, x_refs, (gf_ref, yp_ref, ys_ref) = refs[:k], refs[k:2 * k], refs[2 * k:]
    resid = x_refs[0][...] + peer_refs[0][...]
    for start, p_ref, x_ref in zip(starts[1:], peer_refs[1:], x_refs[1:]):
        resid = jnp.where(pl.program_id(0) >= start, x_ref[...] + p_ref[...], resid)
    y = _rms(resid, gf_ref[...])

    @pl.when(pl.program_id(0) < nbp)
    def _():
        yp_ref[...] = y

    @pl.when(pl.program_id(0) >= nbp)
    def _():
        ys_ref[...] = y


def _final(peers, x1s, gf, n_p):
    counts = [x.shape[0] // ROW_BLOCK for x in x1s]
    starts = [sum(counts[:k]) for k in range(len(counts))]
    n, nbp = sum(counts) * ROW_BLOCK, n_p // ROW_BLOCK
    g = gf.reshape(1, D_MODEL)
    seg = [pl.BlockSpec((ROW_BLOCK, D_MODEL), lambda i, s=s, c=c: (jnp.clip(i - s, 0, c - 1), 0))
           for s, c in zip(starts, counts)]
    return pl.pallas_call(
        functools.partial(_final_kernel, starts=tuple(starts), nbp=nbp),
        grid=(n // ROW_BLOCK,),
        in_specs=[*seg, *seg, pl.BlockSpec(g.shape, lambda i: (0, 0))],
        out_specs=list(_pair_specs(nbp, D_MODEL)),
        out_shape=[jax.ShapeDtypeStruct((n_p, D_MODEL), f32), jax.ShapeDtypeStruct((n - n_p, D_MODEL), f32)],
        compiler_params=pltpu.CompilerParams(dimension_semantics=("arbitrary",), vmem_limit_bytes=VMEM_LIMIT),
        name="final_norm",
    )(*peers, *x1s, g)


def _peer_specs():
    row = lambda w: pl.BlockSpec((ROW_BLOCK, w), lambda i: (i, 0))
    idx = [pl.BlockSpec((SUBLANES, ROW_BLOCK), lambda i, k=k: (k, i), memory_space=pltpu.SMEM)
           for k in range(INDEX_GROUPS)]
    tab = pl.BlockSpec(memory_space=pltpu.VMEM)
    gscr = pltpu.VMEM((G_ROWS, LANES), bf16)
    params = pltpu.CompilerParams(dimension_semantics=("arbitrary",), vmem_limit_bytes=VMEM_LIMIT)
    return row, idx, tab, gscr, params


def _expert_table(tab):
    return tab.astype(bf16).reshape(tab.shape[0], ROW_TILES, LANES)


def _peer_u(eidx_t, xn, g, utab, n):
    row, idx, tab, gscr, params = _peer_specs()
    return pl.pallas_call(
        _peer_u_kernel,
        grid=(n // ROW_BLOCK,),
        in_specs=[*idx, row(D_MODEL), row(P_SLOTS), tab],
        out_specs=row(P_SLOTS),
        out_shape=jax.ShapeDtypeStruct((n, P_SLOTS), f32),
        scratch_shapes=[pltpu.VMEM((ROW_BLOCK * 2 * SUBLANES, LANES), f32),
                        pltpu.VMEM((ROW_TILES, ROW_BLOCK * SUBLANES, LANES), f32), gscr, gscr],
        compiler_params=params,
        name="peer_u",
    )(*[eidx_t] * INDEX_GROUPS, xn, g, utab)


def _peer_v(eidx_t, w, vtab):
    n = w.shape[0]
    row, idx, tab, gscr, params = _peer_specs()
    return pl.pallas_call(
        _peer_v_kernel,
        grid=(n // ROW_BLOCK,),
        in_specs=[*idx, row(P_SLOTS), tab],
        out_specs=row(D_MODEL),
        out_shape=jax.ShapeDtypeStruct((n, D_MODEL), f32),
        scratch_shapes=[pltpu.VMEM((ROW_TILES, ROW_BLOCK * SUBLANES, LANES), f32),
                        pltpu.VMEM((ROW_BLOCK * SUBLANES, LANES), f32), gscr, gscr],
        compiler_params=params,
        name="peer_v",
    )(*[eidx_t] * INDEX_GROUPS, w, vtab)


def _sc_table_kernel(t_ref, o_ref):
    t = t_ref[...]
    o_ref[...] = pltpu.pack_elementwise([t[:, :SC_WORDS], t[:, SC_WORDS:]], packed_dtype=bf16)


def _sc_table(tab):
    e = tab.shape[0]
    return pl.pallas_call(
        _sc_table_kernel,
        grid=(e // ROW_BLOCK,),
        in_specs=[pl.BlockSpec((ROW_BLOCK, D_MODEL), lambda i: (i, 0))],
        out_specs=pl.BlockSpec((ROW_BLOCK, SC_WORDS), lambda i: (i, 0)),
        out_shape=jax.ShapeDtypeStruct((e, SC_WORDS), jnp.uint32),
        compiler_params=pltpu.CompilerParams(dimension_semantics=("parallel",), vmem_limit_bytes=VMEM_LIMIT),
        name="sc_table",
    )(tab)


def _sc_unpack(words):
    return plsc.unpack(plsc.bitcast(words, bf16), format=plsc.PackFormat.INTERLEAVED)


def _sc_gelu_tanh(x):
    z = math.sqrt(2.0 / math.pi) * (x + 0.044715 * (x * x * x))
    return 0.5 * x * (2.0 - 2.0 / (jnp.exp(2.0 * z) + 1.0))


def _sc_peer(utab32, vtab32, eidx, xn, gate):
    n_sc = eidx.shape[0]
    per = n_sc // SC_WORKERS
    units = 4 * per
    assert n_sc % (SC_WORKERS * SUBLANES) == 0

    def body(u_hbm, v_hbm, eidx_hbm, x_hbm, g_hbm, out_hbm,
             idx_v, x_v, g_v, rows_v, acc_v, w_v, w16_v, out_v, row_sems, tok_sems, out_sem):
        base = (lax.axis_index("s") * 2 + lax.axis_index("c")) * per
        lanes = lax.broadcasted_iota(i32, (SC_LANES,), 0)

        def token_copies(tok):
            tslot = tok % 2
            return (pltpu.make_async_copy(eidx_hbm.at[base + tok], idx_v.at[tslot], tok_sems.at[0]),
                    pltpu.make_async_copy(x_hbm.at[base + tok], x_v.at[tslot], tok_sems.at[1]),
                    pltpu.make_async_copy(g_hbm.at[base + tok], g_v.at[tslot], tok_sems.at[2]))

        def store_out(tok):
            return pltpu.make_async_copy(out_v, out_hbm.at[base + tok], out_sem)

        def gather(tab_hbm, g):
            tok, k = g // 4, g % 4
            return pltpu.make_async_copy(tab_hbm.at[idx_v.at[tok % 2, pl.ds((k % 2) * SC_UNIT_ROWS, SC_UNIT_ROWS)]],
                                         rows_v.at[k % 2], row_sems.at[k % 2])

        def start(g):
            @pl.when(g % 4 < 2)
            def _():
                gather(u_hbm, g).start()

            @pl.when(g % 4 >= 2)
            def _():
                gather(v_hbm, g).start()

        def compute_u(tslot, half, rows):
            @pl.loop(0, SC_UNIT_ROWS // SC_ROW_GROUP)
            def _(rg):
                slot0 = half * SC_UNIT_ROWS + rg * SC_ROW_GROUP

                @pl.loop(0, SC_CHUNKS // SC_CHUNK_GROUP)
                def _(cg):
                    keep = jnp.where(cg > 0, 1.0, 0.0).astype(f32)
                    accs = [acc_v[pl.ds((slot0 + r) * SC_LANES, SC_LANES)] * keep for r in range(SC_ROW_GROUP)]
                    for c in range(SC_CHUNK_GROUP):
                        ch = cg * SC_CHUNK_GROUP + c
                        xa = x_v[tslot, pl.ds(ch * SC_LANES, SC_LANES)]
                        xb = x_v[tslot, pl.ds(SC_WORDS + ch * SC_LANES, SC_LANES)]
                        for r in range(SC_ROW_GROUP):
                            a, b = _sc_unpack(rows[rg * SC_ROW_GROUP + r, pl.ds(ch * SC_LANES, SC_LANES)])
                            accs[r] = accs[r] + a * xa + b * xb
                    for r in range(SC_ROW_GROUP):
                        acc_v[pl.ds((slot0 + r) * SC_LANES, SC_LANES)] = accs[r]

        def gate_weights(tslot):
            @pl.loop(0, P_SLOTS // SC_LANES)
            def _(sg):
                first = sg * SC_LANES * SC_LANES
                act = jnp.zeros((SC_LANES,), f32)
                for lane in range(SC_LANES):
                    act = act + plsc.load_gather(acc_v, [first + lanes * SC_LANES + lane])
                w_v[pl.ds(sg * SC_LANES, SC_LANES)] = g_v[tslot, pl.ds(sg * SC_LANES, SC_LANES)] * _sc_gelu_tanh(act)

            @pl.loop(0, P_SLOTS // SC_LANES)
            def _(sg):
                for r in range(SC_LANES):
                    w16_v[pl.ds((sg * SC_LANES + r) * SC_LANES, SC_LANES)] = plsc.load_gather(
                        w_v, [jnp.zeros((SC_LANES,), i32) + (sg * SC_LANES + r)])

        def compute_v(half, rows):
            @pl.loop(0, SC_CHUNKS // SC_CHUNK_GROUP)
            def _(cg):
                first = cg * SC_CHUNK_GROUP

                @pl.loop(0, SC_UNIT_ROWS // SC_ROW_GROUP)
                def _(rg):
                    keep = jnp.where(jnp.logical_or(half == 1, rg > 0), 1.0, 0.0).astype(f32)
                    spots = [pl.ds((j % 2) * SC_WORDS + (first + j // 2) * SC_LANES, SC_LANES)
                             for j in range(2 * SC_CHUNK_GROUP)]
                    accs = [out_v[spot] * keep for spot in spots]
                    for r in range(SC_ROW_GROUP):
                        row = rg * SC_ROW_GROUP + r
                        wv = w16_v[pl.ds((half * SC_UNIT_ROWS + row) * SC_LANES, SC_LANES)]
                        for c in range(SC_CHUNK_GROUP):
                            a, b = _sc_unpack(rows[row, pl.ds((first + c) * SC_LANES, SC_LANES)])
                            accs[2 * c] = accs[2 * c] + a * wv
                            accs[2 * c + 1] = accs[2 * c + 1] + b * wv
                    for spot, acc in zip(spots, accs):
                        out_v[spot] = acc

        for cp in token_copies(0):
            cp.start()
        for cp in token_copies(0):
            cp.wait()
        start(0)

        @pl.loop(0, units)
        def _(g):
            tok, k = g // 4, g % 4

            @pl.when(jnp.logical_and(k == 0, tok + 1 < per))
            def _():
                for cp in token_copies(tok + 1):
                    cp.start()

            @pl.when(jnp.logical_and(k == 3, tok + 1 < per))
            def _():
                for cp in token_copies(tok + 1):
                    cp.wait()

            @pl.when(g + 1 < units)
            def _():
                start(g + 1)

            gather(u_hbm, g).wait()
            rows = rows_v.at[k % 2]

            @pl.when(k < 2)
            def _():
                compute_u(tok % 2, k, rows)

            @pl.when(k == 1)
            def _():
                gate_weights(tok % 2)

            @pl.when(jnp.logical_and(k == 2, tok > 0))
            def _():
                store_out(tok - 1).wait()

            @pl.when(k >= 2)
            def _():
                compute_v(k - 2, rows)

            @pl.when(k == 3)
            def _():
                store_out(tok).start()

        store_out(per - 1).wait()

    return pl.kernel(
        body, mesh=plsc.VectorSubcoreMesh(core_axis_name="c", subcore_axis_name="s"),
        out_type=jax.ShapeDtypeStruct((n_sc, D_MODEL), f32),
        scratch_types=[pltpu.VMEM((2, P_SLOTS), i32), pltpu.VMEM((2, D_MODEL), f32), pltpu.VMEM((2, P_SLOTS), f32),
                       pltpu.VMEM((2, SC_UNIT_ROWS, SC_WORDS), jnp.uint32), pltpu.VMEM((P_SLOTS * SC_LANES,), f32),
                       pltpu.VMEM((P_SLOTS,), f32), pltpu.VMEM((P_SLOTS * SC_LANES,), f32), pltpu.VMEM((D_MODEL,), f32),
                       pltpu.SemaphoreType.DMA((2,)), pltpu.SemaphoreType.DMA((3,)), pltpu.SemaphoreType.DMA],
        compiler_params=pltpu.CompilerParams(needs_layout_passes=False),
        name="sc_peer",
    )(utab32, vtab32, eidx, xn, gate)


def kernel(x_prompt, x_sample, cache_k, cache_v, state_C, state_n, state_m, state_conv, norm1_g, w_in, b_gates, rel_bias, conv_w, conv_b, mh_norm_g, w_out, norm2_g, peer_wq, peer_keys, peer_u, peer_v, final_g):
    bp, sp, d = x_prompt.shape
    bs, ts, _ = x_sample.shape
    n_p, n_s = bp * sp, bs * ts
    n = n_p + n_s
    assert n_p % ROW_BLOCK == 0 and n_s % ROW_BLOCK == 0 and d == D_MODEL
    depth = w_in.shape[0]
    assert depth == 1, "the final norm is fused into the last layer's PEER pass"
    l = 0
    xp, xs = x_prompt.reshape(n_p, d), x_sample.reshape(n_s, d)

    aq, ak, av, mqk, mv, mo, gates = _inproj(xp, xs, norm1_g[l], w_in[l])
    zeros = lambda *shp: jnp.zeros(shp, f32)
    mparams = (conv_w[l], conv_b[l], b_gates[l], mh_norm_g[l])
    retrieval = (w_out[l], norm2_g[l], peer_wq[l], peer_keys[l])

    def mixers(seq, lo, hi, state):
        att = _attn_prompt(aq, ak, av, rel_bias[l], seq, sp, lo, hi)
        h, c, nn, mm = _mlstm(mqk, mv, mo, gates, seq * sp + lo, 1, hi - lo, *state, *mparams)
        conv_rows = mqk[seq * sp + hi - (CONV_W - 1):seq * sp + hi][None]
        return att, h, (c, nn[:, :M_HEADS], mm[:, :M_HEADS, 0], conv_rows)

    fresh = (zeros(1, M_HEADS, M_DH, M_DH), zeros(1, M_HEADS, M_DH), zeros(1, M_HEADS), zeros(1, CONV_W - 1, 2 * M_WIDTH))

    assert bp == 2 and 0 < SC_FIRST_ROWS < sp and 0 < SC_EXTRA_ROWS < sp
    n_sc = sp + SC_EXTRA_ROWS
    sc_tables = _sc_table(peer_u[l]), _sc_table(peer_v[l])
    att_0a, h_0a, state_0a = mixers(0, 0, SC_FIRST_ROWS, fresh)
    x1_a, xn_a, eidx_a, _, gate_a = _mid(xp, 0, xp, 0, att_0a, att_0a, h_0a, h_0a, SC_FIRST_ROWS, 0, *retrieval)
    peer_a = _sc_peer(*sc_tables, eidx_a, xn_a, gate_a)

    att_0b, h_0b, state_0 = mixers(0, SC_FIRST_ROWS, sp, state_0a)
    att_1a, h_1a, state_1a = mixers(1, 0, SC_EXTRA_ROWS, fresh)
    x1_c, xn_c, eidx_c, _, gate_c = _mid(xp, SC_FIRST_ROWS, xp, sp, att_0b, att_1a, h_0b, h_1a,
                                         sp - SC_FIRST_ROWS, SC_EXTRA_ROWS, *retrieval)
    peer_c = _sc_peer(*sc_tables, eidx_c, xn_c, gate_c)

    att_1b, h_1b, state_1 = mixers(1, SC_EXTRA_ROWS, sp, state_1a)
    lcache = cache_k.shape[2]
    att_s = _attn_sample(aq, ak, av, cache_k[l].reshape(bs, lcache, A_WIDTH),
                         cache_v[l].reshape(bs, lcache, A_WIDTH), rel_bias[l], n_p, bs, ts)
    h_s, c_s, nn_s, mm_s = _mlstm(mqk, mv, mo, gates, n_p, bs, ts, state_C[l], state_n[l], state_m[l],
                                  state_conv[l], *mparams)
    x1_b, xn_b, _, eidx_tb, gate_b = _mid(xp, n_sc, xs, 0, att_1b, att_s, h_1b, h_s, n_p - n_sc, n_s, *retrieval)
    w_b = _peer_u(eidx_tb, xn_b, gate_b, _expert_table(peer_u[l]), n - n_sc)
    peer_b = _peer_v(eidx_tb, w_b, _expert_table(peer_v[l]))
    y_p, y_s = _final((peer_a, peer_c, peer_b), (x1_a, x1_c, x1_b), final_g, n_p)
    c_p, nn_p, mm_p = (jnp.concatenate(ab, axis=0) for ab in zip(state_0[:3], state_1[:3]))

    def tail(a, row0, bsz, t, keep):
        return jnp.stack([a[row0 + (b + 1) * t - keep:row0 + (b + 1) * t] for b in range(bsz)])

    keep = min(WINDOW, sp)
    heads = lambda a: a.reshape(a.shape[0], a.shape[1], A_HEADS, A_DH)
    ctail = CONV_W - 1
    conv_tail = lambda buf, a, row0, bsz, t: jnp.concatenate([buf.astype(a.dtype), tail(a, row0, bsz, t, min(ctail, t))],
                                                             axis=1)[:, -ctail:]
    st = lambda a: a[None]
    return (y_p.reshape(bp, sp, d), y_s.reshape(bs, ts, d),
            st(heads(tail(ak, 0, bp, sp, keep))), st(heads(tail(av, 0, bp, sp, keep))),
            st(c_p), st(nn_p), st(mm_p),
            st(conv_tail(zeros(bp, ctail, 2 * M_WIDTH), mqk, 0, bp, sp)),
            st(heads(ak[n_p:].reshape(bs, ts, A_WIDTH))), st(heads(av[n_p:].reshape(bs, ts, A_WIDTH))),
            st(c_s), st(nn_s[:, :M_HEADS]), st(mm_s[:, :M_HEADS, 0]),
            st(conv_tail(state_conv[l], mqk, n_p, bs, ts)))
```

```python
import functools
import math

import jax
import jax.numpy as jnp
from jax import lax
from jax.experimental import pallas as pl
from jax.experimental.pallas import tpu as pltpu
from jax.experimental.pallas import tpu_sc as plsc

f32 = jnp.float32
bf16 = jnp.bfloat16
i32 = jnp.int32

D_MODEL = 1024
CHUNK = 64
A_HEADS = 8
A_DH = 64
A_WIDTH = A_HEADS * A_DH
BAND_CHUNKS = 8
WINDOW = BAND_CHUNKS * CHUNK
MAX_REL = 128
ATT_SCALE = A_DH ** -0.5
M_HEADS = 4
M_DH = 128
M_WIDTH = M_HEADS * M_DH
CONV_W = 4
P_HEADS = 8
P_DKEY = 256
N_KEYS = 128
P_TOPK = 16
P_SLOTS = P_HEADS * P_TOPK
EPS = 1e-6
NEG = -1e30

LANES = 128
SUBLANES = 8
ROW_BLOCK = 256
ATT_TILE = 512
ATT_SUB = 128
ATT_KEYS = ATT_SUB + WINDOW
ROW_TILES = D_MODEL // LANES
G_ROWS = P_SLOTS * ROW_TILES
TOKEN_UNROLL = 8
INDEX_GROUPS = P_SLOTS // SUBLANES
VMEM_LIMIT = 56 * 1024 * 1024

SC_WORKERS = 32
SC_LANES = 16
SC_UNIT_ROWS = P_SLOTS // 2
SC_ROW_GROUP = 16
SC_WORDS = D_MODEL // 2
SC_CHUNKS = SC_WORDS // SC_LANES
SC_CHUNK_GROUP = 8
SC_FIRST_ROWS = 4096
SC_EXTRA_ROWS = 3072


def _rms(x, g):
    return x * lax.rsqrt(jnp.mean(x * x, axis=-1, keepdims=True) + EPS) * g


def _split_bf16(x):
    hi = x.astype(bf16)
    lo = (x - hi.astype(f32)).astype(bf16)
    return hi, lo


def _pair_specs(nbp, width, first=0, second=0):
    return (pl.BlockSpec((ROW_BLOCK, width), lambda i: (first + jnp.minimum(i, nbp - 1), 0)),
            pl.BlockSpec((ROW_BLOCK, width), lambda i: (second + jnp.maximum(i - nbp, 0), 0)))


def _pick(nbp, p_ref, s_ref):
    return jnp.where(pl.program_id(0) < nbp, p_ref[...], s_ref[...])


def _inproj_kernel(xp_ref, xs_ref, g_ref, w_ref, wgh_ref, wgl_ref,
                   aq_ref, ak_ref, av_ref, mqk_ref, mv_ref, mo_ref, gate_ref, *, nbp):
    xn = _rms(_pick(nbp, xp_ref, xs_ref), g_ref[...])
    xh, xl = _split_bf16(xn)

    def proj(lo, hi):
        return jnp.dot(xh, w_ref[:, lo:hi], preferred_element_type=f32)

    aq_ref[...] = proj(0, 512)
    ak_ref[...] = proj(512, 1024)
    av_ref[...] = proj(1024, 1536)
    mqk_ref[...] = proj(1536, 2560)
    mv_ref[...] = proj(2560, 3072)
    mo_ref[...] = proj(3072, 3584)
    gate_ref[...] = (jnp.dot(xh, wgh_ref[...], preferred_element_type=f32)
                     + jnp.dot(xl, wgh_ref[...], preferred_element_type=f32)
                     + jnp.dot(xh, wgl_ref[...], preferred_element_type=f32))


def _inproj(xp, xs, g1, w_in):
    n = xp.shape[0] + xs.shape[0]
    nbp = xp.shape[0] // ROW_BLOCK
    main = 3 * A_WIDTH + 4 * M_WIDTH
    w_main = w_in[:, :main].astype(bf16)
    wg = jnp.pad(w_in[:, main:], ((0, 0), (0, LANES - 2 * M_HEADS)))
    wgh, wgl = _split_bf16(wg)
    widths = (512, 512, 512, 1024, 512, 512, LANES)
    row = lambda w: pl.BlockSpec((ROW_BLOCK, w), lambda i: (i, 0))
    full = lambda a: pl.BlockSpec(a.shape, lambda i: (0,) * a.ndim)
    g = g1.reshape(1, D_MODEL)
    return pl.pallas_call(
        functools.partial(_inproj_kernel, nbp=nbp),
        grid=(n // ROW_BLOCK,),
        in_specs=[*_pair_specs(nbp, D_MODEL), full(g), full(w_main), full(wgh), full(wgl)],
        out_specs=[row(w) for w in widths],
        out_shape=[jax.ShapeDtypeStruct((n, w), f32) for w in widths],
        compiler_params=pltpu.CompilerParams(dimension_semantics=("parallel",), vmem_limit_bytes=VMEM_LIMIT),
        name="inproj",
    )(xp, xs, g, w_main, wgh, wgl)


def _attn_heads(q, k, v, bias_ref, key_ok):
    outs = []
    for h in range(A_HEADS):
        sl = slice(h * A_DH, (h + 1) * A_DH)
        s = lax.dot_general(q[:, sl], k[:, sl], (((1,), (1,)), ((), ())), preferred_element_type=f32)
        s = s * ATT_SCALE + bias_ref[h]
        if key_ok is not None:
            s = jnp.where(key_ok, s, NEG)
        m = jnp.max(s, axis=-1, keepdims=True)
        p = jnp.exp(s - m)
        l = jnp.sum(p, axis=-1, keepdims=True)
        o = jnp.dot(p.astype(bf16), v[:, sl], preferred_element_type=f32)
        outs.append(o / l)
    return jnp.concatenate(outs, axis=-1)


def _attn_prompt_kernel(q_ref, k0_ref, k1_ref, v0_ref, v1_ref, bias_ref, o_ref, *, t0):
    t = t0 + pl.program_id(1)
    q = q_ref[...].astype(bf16)
    k = jnp.concatenate([k0_ref[...], k1_ref[...]], axis=0).astype(bf16)
    v = jnp.concatenate([v0_ref[...], v1_ref[...]], axis=0).astype(bf16)
    col = lax.broadcasted_iota(i32, (1, ATT_KEYS), 1)
    for s in range(ATT_TILE // ATT_SUB):
        lo = s * ATT_SUB
        key_ok = (t * ATT_TILE + lo + col) >= WINDOW
        o_ref[lo:lo + ATT_SUB, :] = _attn_heads(q[lo:lo + ATT_SUB], k[lo:lo + ATT_KEYS], v[lo:lo + ATT_KEYS],
                                                 bias_ref, key_ok)


def _attn_sample_kernel(q_ref, k_ref, v_ref, bias_ref, o_ref):
    o_ref[...] = _attn_heads(q_ref[...].astype(bf16), k_ref[0].astype(bf16), v_ref[0].astype(bf16), bias_ref, None)


def _rel_bias_table(rel_bias, rows, cols, offset, valid):
    span = rows + cols - 1
    rel = offset + rows - 1 - jnp.arange(span)
    diag = rel_bias[:, jnp.clip(rel, -MAX_REL, MAX_REL) + MAX_REL].astype(f32)
    diag = jnp.pad(diag, ((0, 0), (0, 1)))
    flat = jnp.tile(diag, (1, rows))[:, rows - 1:rows - 1 + rows * span]
    return jnp.where(valid[None], flat.reshape(-1, rows, span)[:, :, :cols], NEG)


def _attn_prompt(q, k, v, rel_bias, seq, s, lo, hi):
    assert s % ATT_TILE == 0 and WINDOW == ATT_TILE and lo % ATT_TILE == 0 and hi % ATT_TILE == 0
    nt, t0, cnt = s // ATT_TILE, lo // ATT_TILE, (hi - lo) // ATT_TILE
    i = jnp.arange(ATT_SUB)[:, None]
    j = jnp.arange(ATT_KEYS)[None, :]
    off = j - (i // CHUNK) * CHUNK
    bias = _rel_bias_table(rel_bias, ATT_SUB, ATT_KEYS, WINDOW, (off >= 0) & (off < WINDOW + CHUNK))
    cur = pl.BlockSpec((ATT_TILE, A_WIDTH), lambda b, t: (seq * nt + t0 + t, 0))
    prev = pl.BlockSpec((ATT_TILE, A_WIDTH), lambda b, t: (seq * nt + jnp.maximum(t0 + t - 1, 0), 0))
    return pl.pallas_call(
        functools.partial(_attn_prompt_kernel, t0=t0),
        grid=(1, cnt),
        in_specs=[cur, prev, cur, prev, cur, pl.BlockSpec(bias.shape, lambda b, t: (0, 0, 0))],
        out_specs=pl.BlockSpec((ATT_TILE, A_WIDTH), lambda b, t: (t, 0)),
        out_shape=jax.ShapeDtypeStruct((hi - lo, A_WIDTH), f32),
        compiler_params=pltpu.CompilerParams(dimension_semantics=("parallel", "parallel"),
                                             vmem_limit_bytes=VMEM_LIMIT),
        name="attn_prompt",
    )(q, k, k, v, v, bias)


def _attn_sample(q, k, v, ck, cv, rel_bias, row0, bsz, t):
    l = ck.shape[1]
    assert row0 % t == 0
    keys = -(-(l + t) // LANES) * LANES
    padk = ((0, 0), (0, keys - l - t), (0, 0))
    kk = jnp.pad(jnp.concatenate([ck, k[row0:].reshape(bsz, t, A_WIDTH)], axis=1), padk)
    vv = jnp.pad(jnp.concatenate([cv, v[row0:].reshape(bsz, t, A_WIDTH)], axis=1), padk)
    j = jnp.arange(keys)[None, :]
    bias = _rel_bias_table(rel_bias, t, keys, l, jnp.broadcast_to(j < l + t, (t, keys)))
    return pl.pallas_call(
        _attn_sample_kernel,
        grid=(bsz,),
        in_specs=[pl.BlockSpec((t, A_WIDTH), lambda b: (row0 // t + b, 0)),
                  pl.BlockSpec((1, keys, A_WIDTH), lambda b: (b, 0, 0)),
                  pl.BlockSpec((1, keys, A_WIDTH), lambda b: (b, 0, 0)),
                  pl.BlockSpec(bias.shape, lambda b: (0, 0, 0))],
        out_specs=pl.BlockSpec((t, A_WIDTH), lambda b: (b, 0)),
        out_shape=jax.ShapeDtypeStruct((bsz * t, A_WIDTH), f32),
        compiler_params=pltpu.CompilerParams(dimension_semantics=("parallel",), vmem_limit_bytes=VMEM_LIMIT),
        name="attn_sample",
    )(q, kk, vv, bias)


def _mlstm_chunk(a, vall, o_in, gate_in, cw_ref, cb_ref, bg_ref, mhg_ref, c_s, n_s, m_s, prev_s):
    lc = a.shape[0]
    ext = jnp.concatenate([prev_s[...], a], axis=0)
    conv = cb_ref[...]
    for j in range(CONV_W):
        lo = SUBLANES - (CONV_W - 1) + j
        conv = conv + cw_ref[j:j + 1, :] * ext[lo:lo + lc]
    prev_s[...] = a[lc - SUBLANES:lc]
    qk = conv * jax.nn.sigmoid(conv)

    z = gate_in + bg_ref[...]
    lane = lax.broadcasted_iota(i32, (lc, LANES), 1)
    row = lax.broadcasted_iota(i32, (lc, LANES), 0)
    logf = jnp.minimum(z, 0.0) - jnp.log1p(jnp.exp(-jnp.abs(z)))
    cum = jnp.where((lane >= M_HEADS) & (lane < 2 * M_HEADS), logf, 0.0)
    shift = 1
    while shift < lc:
        cum = cum + jnp.where(row >= shift, pltpu.roll(cum, shift, axis=0), 0.0)
        shift *= 2
    zc = jnp.where(lane < M_HEADS, z, cum)
    zt = jnp.concatenate([zc, jnp.zeros((LANES - lc, LANES), f32)], axis=0).T[:, :lc]

    ri = lax.broadcasted_iota(i32, (lc, lc), 0)
    ci = lax.broadcasted_iota(i32, (lc, lc), 1)
    causal = ri >= ci
    state = [(m_s[h:h + 1, 0:1], c_s[h], n_s[h:h + 1, :]) for h in range(M_HEADS)]
    hs, new_state = [], []
    for h in range(M_HEADS):
        sl = slice(h * M_DH, (h + 1) * M_DH)
        q = qk[:, sl]
        k = qk[:, M_WIDTH + h * M_DH:M_WIDTH + (h + 1) * M_DH] * (M_DH ** -0.5)
        v = vall[:, sl]
        i_col = zc[:, h:h + 1]
        b_col = zc[:, M_HEADS + h:M_HEADS + h + 1]
        i_row = zt[h:h + 1, :]
        b_row = zt[M_HEADS + h:M_HEADS + h + 1, :]
        m_prev, c_prev, n_prev = state[h]

        dmat = jnp.where(causal, b_col - b_row + i_row, NEG)
        inter = b_col + m_prev
        mt = jnp.maximum(jnp.max(dmat, axis=-1, keepdims=True), inter)
        w_intra = jnp.exp(dmat - mt)
        w_inter = jnp.exp(inter - mt)
        qb, kb, vb = q.astype(bf16), k.astype(bf16), v.astype(bf16)
        s = lax.dot_general(qb, kb, (((1,), (1,)), ((), ())), preferred_element_type=f32) * w_intra
        num = (w_inter * jnp.dot(qb, c_prev.astype(bf16), preferred_element_type=f32)
               + jnp.dot(s.astype(bf16), vb, preferred_element_type=f32))
        den = w_inter * jnp.sum(q * n_prev, axis=-1, keepdims=True) + jnp.sum(s, axis=-1, keepdims=True)
        hh = num / jnp.maximum(jnp.abs(den), jnp.exp(-mt))
        m_new = mt[lc - 1:lc, :]
        b_last = b_col[lc - 1:lc, :]
        w_s = jnp.exp(b_last - b_col + i_col - m_new)
        decay = jnp.exp(b_last + m_prev - m_new)
        kw = k * w_s
        new_state.append((jnp.broadcast_to(m_new, (1, LANES)),
                          decay * c_prev + lax.dot_general(kw.astype(bf16), vb, (((0,), (0,)), ((), ())),
                                                           preferred_element_type=f32),
                          decay * n_prev + jnp.sum(kw, axis=0, keepdims=True)))
        hs.append(hh * lax.rsqrt(jnp.mean(hh * hh, axis=-1, keepdims=True) + EPS))

    for h, (m_new, c_new, n_new) in enumerate(new_state):
        m_s[h:h + 1, :] = m_new
        c_s[h] = c_new
        n_s[h:h + 1, :] = n_new
    return jnp.concatenate(hs, axis=-1) * mhg_ref[...] * jax.nn.sigmoid(o_in)


def _mlstm_kernel(qk_ref, v_ref, o_ref, gate_ref, c0_ref, n0_ref, m0_ref, cbuf_ref,
                  cw_ref, cb_ref, bg_ref, mhg_ref,
                  h_ref, cout_ref, nout_ref, mout_ref,
                  c_s, n_s, m_s, prev_s):
    c = pl.program_id(1)

    @pl.when(c == 0)
    def _():
        c_s[...] = c0_ref[0]
        n_s[...] = n0_ref[0]
        m_s[...] = m0_ref[0]
        prev_s[...] = cbuf_ref[0]

    h_ref[...] = _mlstm_chunk(qk_ref[...], v_ref[...], o_ref[...], gate_ref[...],
                              cw_ref, cb_ref, bg_ref, mhg_ref, c_s, n_s, m_s, prev_s)

    @pl.when(c == pl.num_programs(1) - 1)
    def _():
        cout_ref[0] = c_s[...]
        nout_ref[0] = n_s[...]
        mout_ref[0] = m_s[...]


def _mlstm(mqk, mv, mo, gates, row0, bsz, t, c0, n0, m0, cbuf, conv_w, conv_b, b_gates, mh_g):
    lc = min(CHUNK, t)
    step = lc
    nc = t // step
    assert t % step == 0 and lc % SUBLANES == 0 and row0 % step == 0
    n0p = jnp.pad(n0.astype(f32), ((0, 0), (0, SUBLANES - M_HEADS), (0, 0)))
    m0p = jnp.pad(jnp.broadcast_to(m0.astype(f32)[:, :, None], (bsz, M_HEADS, LANES)),
                  ((0, 0), (0, SUBLANES - M_HEADS), (0, 0)))
    cbp = jnp.pad(cbuf.astype(f32), ((0, 0), (SUBLANES - (CONV_W - 1), 0), (0, 0)))
    bg = jnp.pad(b_gates.astype(f32), (0, LANES - 2 * M_HEADS)).reshape(1, LANES)
    seq = lambda w: pl.BlockSpec((step, w), lambda b, c: (row0 // step + b * nc + c, 0))
    out_seq = pl.BlockSpec((step, M_WIDTH), lambda b, c: (b * nc + c, 0))
    per_b = lambda shp: pl.BlockSpec((1,) + shp, lambda b, c: (b,) + (0,) * len(shp))
    full = lambda a: pl.BlockSpec(a.shape, lambda b, c: (0,) * a.ndim)
    cb = conv_b.reshape(1, -1)
    mhg = mh_g.reshape(1, -1)
    return pl.pallas_call(
        _mlstm_kernel,
        grid=(bsz, nc),
        in_specs=[seq(2 * M_WIDTH), seq(M_WIDTH), seq(M_WIDTH), seq(LANES),
                  per_b((M_HEADS, M_DH, M_DH)), per_b((SUBLANES, M_DH)), per_b((SUBLANES, LANES)),
                  per_b((SUBLANES, 2 * M_WIDTH)),
                  full(conv_w), full(cb), full(bg), full(mhg)],
        out_specs=[out_seq, per_b((M_HEADS, M_DH, M_DH)), per_b((SUBLANES, M_DH)), per_b((SUBLANES, LANES))],
        out_shape=[jax.ShapeDtypeStruct((bsz * t, M_WIDTH), f32),
                   jax.ShapeDtypeStruct((bsz, M_HEADS, M_DH, M_DH), f32),
                   jax.ShapeDtypeStruct((bsz, SUBLANES, M_DH), f32),
                   jax.ShapeDtypeStruct((bsz, SUBLANES, LANES), f32)],
        scratch_shapes=[pltpu.VMEM((M_HEADS, M_DH, M_DH), f32), pltpu.VMEM((SUBLANES, M_DH), f32),
                        pltpu.VMEM((SUBLANES, LANES), f32), pltpu.VMEM((SUBLANES, 2 * M_WIDTH), f32)],
        compiler_params=pltpu.CompilerParams(dimension_semantics=("parallel", "arbitrary"),
                                             vmem_limit_bytes=VMEM_LIMIT),
        name="mlstm",
    )(mqk, mv, mo, gates, c0.astype(f32), n0p, m0p, cbp, conv_w, cb, bg, mhg)


def _topk_rows(s, k, payload=None):
    n = s.shape[0]
    rows = lax.broadcasted_iota(i32, s.shape, 0).astype(f32)
    vals, ids = [], []
    for _ in range(k):
        m = jnp.max(s, axis=0, keepdims=True)
        pos = jnp.min(jnp.where(s == m, rows, float(n)), axis=0, keepdims=True)
        sel = rows == pos
        vals.append(m)
        ids.append(pos if payload is None else jnp.max(jnp.where(sel, payload, -1.0), axis=0, keepdims=True))
        s = jnp.where(sel, -jnp.inf, s)
    return jnp.concatenate(vals, axis=0), jnp.concatenate(ids, axis=0)


def _mid_kernel(xp_ref, xs_ref, attp_ref, atts_ref, hp_ref, hs_ref, wo_ref, g2_ref, wq_ref, keys_ref, after_ref,
                x1_ref, xn_ref, eidx_ref, eidx_t_ref, gate_ref, *, nbp):
    del after_ref
    cat = jnp.concatenate([_pick(nbp, attp_ref, atts_ref), _pick(nbp, hp_ref, hs_ref)], axis=-1).astype(bf16)
    x1 = _pick(nbp, xp_ref, xs_ref) + jnp.dot(cat, wo_ref[...], preferred_element_type=f32)
    x1_ref[...] = x1
    xn = _rms(x1, g2_ref[...])
    xn_ref[...] = xn
    xb = xn.astype(bf16)
    e_rows, g_rows = [], []
    for h in range(P_HEADS):
        q = jnp.dot(xb, wq_ref[:, h * P_DKEY:(h + 1) * P_DKEY], preferred_element_type=f32).astype(bf16)
        half = []
        for p in range(2):
            st = lax.dot_general(keys_ref[h, p], q[:, p * N_KEYS:(p + 1) * N_KEYS], (((1,), (1,)), ((), ())),
                                 preferred_element_type=f32)
            half.append(_topk_rows(st, P_TOPK))
        (v0, i0), (v1, i1) = half
        width = [P_TOPK // (a + 1) for a in range(P_TOPK)]
        fill = -sum(width) % SUBLANES
        cand = jnp.concatenate([v0[a:a + 1] + v1[:width[a]] for a in range(P_TOPK)]
                               + [jnp.full((fill, v0.shape[1]), -jnp.inf, f32)], axis=0)
        cidx = jnp.concatenate([i0[a:a + 1] * float(N_KEYS) + i1[:width[a]] for a in range(P_TOPK)]
                               + [jnp.zeros((fill, v0.shape[1]), f32)], axis=0)
        top_s, eid = _topk_rows(cand, P_TOPK, payload=cidx)
        ex = jnp.exp(top_s - top_s[0:1])
        e_rows.append(eid)
        g_rows.append(ex / jnp.sum(ex, axis=0, keepdims=True))
    eidx_t = jnp.concatenate(e_rows, axis=0)
    eidx_t_ref[...] = eidx_t.astype(i32)
    eidx_ref[...] = eidx_t.T.astype(i32)
    gate_ref[...] = jnp.concatenate(g_rows, axis=0).T


def _mid(xp, xp_row0, xs, xs_row0, attp, atts, hp, hs, n_a, n_b, after, w_out, g2, wq, keys):
    n = n_a + n_b
    nbp = n_a // ROW_BLOCK
    assert n_a % ROW_BLOCK == 0 and n_b % ROW_BLOCK == 0 and xp_row0 % ROW_BLOCK == 0 and xs_row0 % ROW_BLOCK == 0
    row = lambda w: pl.BlockSpec((ROW_BLOCK, w), lambda i: (i, 0))
    full = lambda a: pl.BlockSpec(a.shape, lambda i: (0,) * a.ndim)
    wo = w_out.astype(bf16)
    wqb = wq.astype(bf16)
    kb = keys.astype(bf16)
    g = g2.reshape(1, D_MODEL)
    return pl.pallas_call(
        functools.partial(_mid_kernel, nbp=nbp),
        grid=(n // ROW_BLOCK,),
        in_specs=[*_pair_specs(nbp, D_MODEL, xp_row0 // ROW_BLOCK, xs_row0 // ROW_BLOCK), *_pair_specs(nbp, A_WIDTH),
                  *_pair_specs(nbp, M_WIDTH), full(wo), full(g), full(wqb), full(kb),
                  pl.BlockSpec((SUBLANES, P_SLOTS), lambda i: (0, 0))],
        out_specs=[row(D_MODEL), row(D_MODEL), row(P_SLOTS), pl.BlockSpec((P_SLOTS, ROW_BLOCK), lambda i: (0, i)),
                   row(P_SLOTS)],
        out_shape=[jax.ShapeDtypeStruct((n, D_MODEL), f32), jax.ShapeDtypeStruct((n, D_MODEL), f32),
                   jax.ShapeDtypeStruct((n, P_SLOTS), i32), jax.ShapeDtypeStruct((P_SLOTS, n), i32),
                   jax.ShapeDtypeStruct((n, P_SLOTS), f32)],
        compiler_params=pltpu.CompilerParams(dimension_semantics=("parallel",), vmem_limit_bytes=VMEM_LIMIT),
        name="outproj_retrieve",
    )(xp, xs, attp, atts, hp, hs, wo, g, wqb, kb, after)


def _gather_rows(eidx_refs, tab_ref, g_ref, t):
    for i in range(SUBLANES):
        for k, eidx_ref in enumerate(eidx_refs):
            r = k * SUBLANES + i
            g_ref[pl.ds(r * ROW_TILES, ROW_TILES), :] = tab_ref[eidx_ref[i, t]]


def _tile_rows(t, rows=SUBLANES):
    return pl.ds(pl.multiple_of(t * rows, rows), rows)


def _pipelined_tokens(nt, eidx_refs, tab_ref, g0_s, g1_s, compute):
    bufs = (g0_s, g1_s)
    _gather_rows(eidx_refs, tab_ref, g0_s, 0)

    def body(j, carry):
        for u in range(TOKEN_UNROLL):
            t = TOKEN_UNROLL * j + u
            _gather_rows(eidx_refs, tab_ref, bufs[(u + 1) % 2], jnp.minimum(t + 1, nt - 1))
            compute(t, bufs[u % 2])
        return carry

    lax.fori_loop(0, nt // TOKEN_UNROLL, body, 0)


def _gelu_tanh(x):
    return 0.5 * x * (1.0 + jnp.tanh(math.sqrt(2.0 / math.pi) * (x + 0.044715 * (x * x * x))))


def _peer_u_kernel(*refs):
    eidx_refs, (xn_ref, g_ref, tab_ref, w_ref, xl_s, r_s, g0_s, g1_s) = refs[:INDEX_GROUPS], refs[INDEX_GROUPS:]
    nt = xn_ref.shape[0]
    xn = xn_ref[...]
    xh = xn.astype(bf16).astype(f32)
    xl = xn - xh
    for k in range(ROW_TILES):
        xl_s[pl.ds(k, nt, stride=2 * SUBLANES), :] = xh[:, k * LANES:(k + 1) * LANES]
        xl_s[pl.ds(SUBLANES + k, nt, stride=2 * SUBLANES), :] = xl[:, k * LANES:(k + 1) * LANES]
    diag = (lax.broadcasted_iota(i32, (SUBLANES, G_ROWS), 1) % ROW_TILES
            == lax.broadcasted_iota(i32, (SUBLANES, G_ROWS), 0))

    def compute(t, g_s):
        lhs = xl_s[_tile_rows(t, 2 * SUBLANES), :].astype(bf16)
        out = lax.dot_general(lhs, g_s[...], (((1,), (1,)), ((), ())), preferred_element_type=f32)
        part = jnp.where(diag, out[:SUBLANES] + out[SUBLANES:], 0.0)
        for c in range(ROW_TILES):
            r_s[c, _tile_rows(t), :] = part[:, c * LANES:(c + 1) * LANES]

    _pipelined_tokens(nt, eidx_refs, tab_ref, g0_s, g1_s, compute)
    cols = []
    for c in range(ROW_TILES):
        acc = r_s[c, pl.ds(0, nt, stride=SUBLANES), :]
        for k in range(1, SUBLANES):
            acc = acc + r_s[c, pl.ds(k, nt, stride=SUBLANES), :]
        cols.append(acc)
    s = jnp.concatenate(cols, axis=-1)
    fold = (lax.broadcasted_iota(i32, (G_ROWS, P_SLOTS), 0) // ROW_TILES
            == lax.broadcasted_iota(i32, (G_ROWS, P_SLOTS), 1)).astype(bf16)
    sh, sl = _split_bf16(s)
    act = jnp.dot(sh, fold, preferred_element_type=f32) + jnp.dot(sl, fold, preferred_element_type=f32)
    w_ref[...] = g_ref[...] * _gelu_tanh(act)


def _peer_v_kernel(*refs):
    eidx_refs, (w_ref, tab_ref, peer_ref, wl_s, o_s, g0_s, g1_s) = refs[:INDEX_GROUPS], refs[INDEX_GROUPS:]
    nt = w_ref.shape[0]
    spread = (lax.broadcasted_iota(i32, (P_SLOTS, G_ROWS), 1) // ROW_TILES
              == lax.broadcasted_iota(i32, (P_SLOTS, G_ROWS), 0)).astype(bf16)
    wexp = jnp.dot(w_ref[...].astype(bf16), spread, preferred_element_type=f32)
    lane = lax.broadcasted_iota(i32, (nt, LANES), 1)
    for c in range(ROW_TILES):
        wc = wexp[:, c * LANES:(c + 1) * LANES]
        for k in range(SUBLANES):
            wl_s[c, pl.ds(k, nt, stride=SUBLANES), :] = jnp.where(lane % ROW_TILES == k, wc, 0.0)

    def compute(t, g_s):
        lhs = jnp.concatenate([wl_s[c, _tile_rows(t), :] for c in range(ROW_TILES)], axis=-1).astype(bf16)
        o_s[_tile_rows(t), :] = jnp.dot(lhs, g_s[...], preferred_element_type=f32)

    _pipelined_tokens(nt, eidx_refs, tab_ref, g0_s, g1_s, compute)
    for k in range(ROW_TILES):
        peer_ref[:, k * LANES:(k + 1) * LANES] = o_s[pl.ds(k, nt, stride=SUBLANES), :]


def _final_kernel(*refs, starts, nbp):
    k = len(starts)
    peer_refs, x_refs, (gf_ref, yp_ref, ys_ref) = refs[:k], refs[k:2 * k], refs[2 * k:]
    resid = x_refs[0][...] + peer_refs[0][...]
    for start, p_ref, x_ref in zip(starts[1:], peer_refs[1:], x_refs[1:]):
        resid = jnp.where(pl.program_id(0) >= start, x_ref[...] + p_ref[...], resid)
    y = _rms(resid, gf_ref[...])

    @pl.when(pl.program_id(0) < nbp)
    def _():
        yp_ref[...] = y

    @pl.when(pl.program_id(0) >= nbp)
    def _():
        ys_ref[...] = y


def _final(peers, x1s, gf, n_p):
    counts = [x.shape[0] // ROW_BLOCK for x in x1s]
    starts = [sum(counts[:k]) for k in range(len(counts))]
    n, nbp = sum(counts) * ROW_BLOCK, n_p // ROW_BLOCK
    g = gf.reshape(1, D_MODEL)
    seg = [pl.BlockSpec((ROW_BLOCK, D_MODEL), lambda i, s=s, c=c: (jnp.clip(i - s, 0, c - 1), 0))
           for s, c in zip(starts, counts)]
    return pl.pallas_call(
        functools.partial(_final_kernel, starts=tuple(starts), nbp=nbp),
        grid=(n // ROW_BLOCK,),
        in_specs=[*seg, *seg, pl.BlockSpec(g.shape, lambda i: (0, 0))],
        out_specs=list(_pair_specs(nbp, D_MODEL)),
        out_shape=[jax.ShapeDtypeStruct((n_p, D_MODEL), f32), jax.ShapeDtypeStruct((n - n_p, D_MODEL), f32)],
        compiler_params=pltpu.CompilerParams(dimension_semantics=("arbitrary",), vmem_limit_bytes=VMEM_LIMIT),
        name="final_norm",
    )(*peers, *x1s, g)


def _peer_specs():
    row = lambda w: pl.BlockSpec((ROW_BLOCK, w), lambda i: (i, 0))
    idx = [pl.BlockSpec((SUBLANES, ROW_BLOCK), lambda i, k=k: (k, i), memory_space=pltpu.SMEM)
           for k in range(INDEX_GROUPS)]
    tab = pl.BlockSpec(memory_space=pltpu.VMEM)
    gscr = pltpu.VMEM((G_ROWS, LANES), bf16)
    params = pltpu.CompilerParams(dimension_semantics=("arbitrary",), vmem_limit_bytes=VMEM_LIMIT)
    return row, idx, tab, gscr, params


def _expert_table(tab):
    return tab.astype(bf16).reshape(tab.shape[0], ROW_TILES, LANES)


def _peer_u(eidx_t, xn, g, utab, n):
    row, idx, tab, gscr, params = _peer_specs()
    return pl.pallas_call(
        _peer_u_kernel,
        grid=(n // ROW_BLOCK,),
        in_specs=[*idx, row(D_MODEL), row(P_SLOTS), tab],
        out_specs=row(P_SLOTS),
        out_shape=jax.ShapeDtypeStruct((n, P_SLOTS), f32),
        scratch_shapes=[pltpu.VMEM((ROW_BLOCK * 2 * SUBLANES, LANES), f32),
                        pltpu.VMEM((ROW_TILES, ROW_BLOCK * SUBLANES, LANES), f32), gscr, gscr],
        compiler_params=params,
        name="peer_u",
    )(*[eidx_t] * INDEX_GROUPS, xn, g, utab)


def _peer_v(eidx_t, w, vtab):
    n = w.shape[0]
    row, idx, tab, gscr, params = _peer_specs()
    return pl.pallas_call(
        _peer_v_kernel,
        grid=(n // ROW_BLOCK,),
        in_specs=[*idx, row(P_SLOTS), tab],
        out_specs=row(D_MODEL),
        out_shape=jax.ShapeDtypeStruct((n, D_MODEL), f32),
        scratch_shapes=[pltpu.VMEM((ROW_TILES, ROW_BLOCK * SUBLANES, LANES), f32),
                        pltpu.VMEM((ROW_BLOCK * SUBLANES, LANES), f32), gscr, gscr],
        compiler_params=params,
        name="peer_v",
    )(*[eidx_t] * INDEX_GROUPS, w, vtab)


def _sc_table_kernel(t_ref, o_ref):
    t = t_ref[...]
    o_ref[...] = pltpu.pack_elementwise([t[:, :SC_WORDS], t[:, SC_WORDS:]], packed_dtype=bf16)


def _sc_table(tab):
    e = tab.shape[0]
    return pl.pallas_call(
        _sc_table_kernel,
        grid=(e // ROW_BLOCK,),
        in_specs=[pl.BlockSpec((ROW_BLOCK, D_MODEL), lambda i: (i, 0))],
        out_specs=pl.BlockSpec((ROW_BLOCK, SC_WORDS), lambda i: (i, 0)),
        out_shape=jax.ShapeDtypeStruct((e, SC_WORDS), jnp.uint32),
        compiler_params=pltpu.CompilerParams(dimension_semantics=("parallel",), vmem_limit_bytes=VMEM_LIMIT),
        name="sc_table",
    )(tab)


def _sc_unpack(words):
    return plsc.unpack(plsc.bitcast(words, bf16), format=plsc.PackFormat.INTERLEAVED)


def _sc_gelu_tanh(x):
    z = math.sqrt(2.0 / math.pi) * (x + 0.044715 * (x * x * x))
    return 0.5 * x * (2.0 - 2.0 / (jnp.exp(2.0 * z) + 1.0))


def _sc_peer(utab32, vtab32, eidx, xn, gate):
    n_sc = eidx.shape[0]
    per = n_sc // SC_WORKERS
    units = 4 * per
    assert n_sc % (SC_WORKERS * SUBLANES) == 0

    def body(u_hbm, v_hbm, eidx_hbm, x_hbm, g_hbm, out_hbm,
             idx_v, x_v, g_v, rows_v, acc_v, w_v, w16_v, out_v, row_sems, tok_sems, out_sem):
        base = (lax.axis_index("s") * 2 + lax.axis_index("c")) * per
        lanes = lax.broadcasted_iota(i32, (SC_LANES,), 0)

        def token_copies(tok):
            tslot = tok % 2
            return (pltpu.make_async_copy(eidx_hbm.at[base + tok], idx_v.at[tslot], tok_sems.at[0]),
                    pltpu.make_async_copy(x_hbm.at[base + tok], x_v.at[tslot], tok_sems.at[1]),
                    pltpu.make_async_copy(g_hbm.at[base + tok], g_v.at[tslot], tok_sems.at[2]))

        def store_out(tok):
            return pltpu.make_async_copy(out_v, out_hbm.at[base + tok], out_sem)

        def gather(tab_hbm, g):
            tok, k = g // 4, g % 4
            return pltpu.make_async_copy(tab_hbm.at[idx_v.at[tok % 2, pl.ds((k % 2) * SC_UNIT_ROWS, SC_UNIT_ROWS)]],
                                         rows_v.at[k % 2], row_sems.at[k % 2])

        def start(g):
            @pl.when(g % 4 < 2)
            def _():
                gather(u_hbm, g).start()

            @pl.when(g % 4 >= 2)
            def _():
                gather(v_hbm, g).start()

        def compute_u(tslot, half, rows):
            @pl.loop(0, SC_UNIT_ROWS // SC_ROW_GROUP)
            def _(rg):
                slot0 = half * SC_UNIT_ROWS + rg * SC_ROW_GROUP

                @pl.loop(0, SC_CHUNKS // SC_CHUNK_GROUP)
                def _(cg):
                    keep = jnp.where(cg > 0, 1.0, 0.0).astype(f32)
                    accs = [acc_v[pl.ds((slot0 + r) * SC_LANES, SC_LANES)] * keep for r in range(SC_ROW_GROUP)]
                    for c in range(SC_CHUNK_GROUP):
                        ch = cg * SC_CHUNK_GROUP + c
                        xa = x_v[tslot, pl.ds(ch * SC_LANES, SC_LANES)]
                        xb = x_v[tslot, pl.ds(SC_WORDS + ch * SC_LANES, SC_LANES)]
                        for r in range(SC_ROW_GROUP):
                            a, b = _sc_unpack(rows[rg * SC_ROW_GROUP + r, pl.ds(ch * SC_LANES, SC_LANES)])
                            accs[r] = accs[r] + a * xa + b * xb
                    for r in range(SC_ROW_GROUP):
                        acc_v[pl.ds((slot0 + r) * SC_LANES, SC_LANES)] = accs[r]

        def gate_weights(tslot):
            @pl.loop(0, P_SLOTS // SC_LANES)
            def _(sg):
                first = sg * SC_LANES * SC_LANES
                act = jnp.zeros((SC_LANES,), f32)
                for lane in range(SC_LANES):
                    act = act + plsc.load_gather(acc_v, [first + lanes * SC_LANES + lane])
                w_v[pl.ds(sg * SC_LANES, SC_LANES)] = g_v[tslot, pl.ds(sg * SC_LANES, SC_LANES)] * _sc_gelu_tanh(act)

            @pl.loop(0, P_SLOTS // SC_LANES)
            def _(sg):
                for r in range(SC_LANES):
                    w16_v[pl.ds((sg * SC_LANES + r) * SC_LANES, SC_LANES)] = plsc.load_gather(
                        w_v, [jnp.zeros((SC_LANES,), i32) + (sg * SC_LANES + r)])

        def compute_v(half, rows):
            @pl.loop(0, SC_CHUNKS // SC_CHUNK_GROUP)
            def _(cg):
                first = cg * SC_CHUNK_GROUP

                @pl.loop(0, SC_UNIT_ROWS // SC_ROW_GROUP)
                def _(rg):
                    keep = jnp.where(jnp.logical_or(half == 1, rg > 0), 1.0, 0.0).astype(f32)
                    spots = [pl.ds((j % 2) * SC_WORDS + (first + j // 2) * SC_LANES, SC_LANES)
                             for j in range(2 * SC_CHUNK_GROUP)]
                    accs = [out_v[spot] * keep for spot in spots]
                    for r in range(SC_ROW_GROUP):
                        row = rg * SC_ROW_GROUP + r
                        wv = w16_v[pl.ds((half * SC_UNIT_ROWS + row) * SC_LANES, SC_LANES)]
                        for c in range(SC_CHUNK_GROUP):
                            a, b = _sc_unpack(rows[row, pl.ds((first + c) * SC_LANES, SC_LANES)])
                            accs[2 * c] = accs[2 * c] + a * wv
                            accs[2 * c + 1] = accs[2 * c + 1] + b * wv
                    for spot, acc in zip(spots, accs):
                        out_v[spot] = acc

        for cp in token_copies(0):
            cp.start()
        for cp in token_copies(0):
            cp.wait()
        start(0)

        @pl.loop(0, units)
        def _(g):
            tok, k = g // 4, g % 4

            @pl.when(jnp.logical_and(k == 0, tok + 1 < per))
            def _():
                for cp in token_copies(tok + 1):
                    cp.start()

            @pl.when(jnp.logical_and(k == 3, tok + 1 < per))
            def _():
                for cp in token_copies(tok + 1):
                    cp.wait()

            @pl.when(g + 1 < units)
            def _():
                start(g + 1)

            gather(u_hbm, g).wait()
            rows = rows_v.at[k % 2]

            @pl.when(k < 2)
            def _():
                compute_u(tok % 2, k, rows)

            @pl.when(k == 1)
            def _():
                gate_weights(tok % 2)

            @pl.when(jnp.logical_and(k == 2, tok > 0))
            def _():
                store_out(tok - 1).wait()

            @pl.when(k >= 2)
            def _():
                compute_v(k - 2, rows)

            @pl.when(k == 3)
            def _():
                store_out(tok).start()

        store_out(per - 1).wait()

    return pl.kernel(
        body, mesh=plsc.VectorSubcoreMesh(core_axis_name="c", subcore_axis_name="s"),
        out_type=jax.ShapeDtypeStruct((n_sc, D_MODEL), f32),
        scratch_types=[pltpu.VMEM((2, P_SLOTS), i32), pltpu.VMEM((2, D_MODEL), f32), pltpu.VMEM((2, P_SLOTS), f32),
                       pltpu.VMEM((2, SC_UNIT_ROWS, SC_WORDS), jnp.uint32), pltpu.VMEM((P_SLOTS * SC_LANES,), f32),
                       pltpu.VMEM((P_SLOTS,), f32), pltpu.VMEM((P_SLOTS * SC_LANES,), f32), pltpu.VMEM((D_MODEL,), f32),
                       pltpu.SemaphoreType.DMA((2,)), pltpu.SemaphoreType.DMA((3,)), pltpu.SemaphoreType.DMA],
        compiler_params=pltpu.CompilerParams(needs_layout_passes=False),
        name="sc_peer",
    )(utab32, vtab32, eidx, xn, gate)


def kernel(x_prompt, x_sample, cache_k, cache_v, state_C, state_n, state_m, state_conv, norm1_g, w_in, b_gates, rel_bias, conv_w, conv_b, mh_norm_g, w_out, norm2_g, peer_wq, peer_keys, peer_u, peer_v, final_g):
    bp, sp, d = x_prompt.shape
    bs, ts, _ = x_sample.shape
    n_p, n_s = bp * sp, bs * ts
    n = n_p + n_s
    assert n_p % ROW_BLOCK == 0 and n_s % ROW_BLOCK == 0 and d == D_MODEL
    depth = w_in.shape[0]
    assert depth == 1, "the final norm is fused into the last layer's PEER pass"
    l = 0
    xp, xs = x_prompt.reshape(n_p, d), x_sample.reshape(n_s, d)

    aq, ak, av, mqk, mv, mo, gates = _inproj(xp, xs, norm1_g[l], w_in[l])
    zeros = lambda *shp: jnp.zeros(shp, f32)
    mparams = (conv_w[l], conv_b[l], b_gates[l], mh_norm_g[l])
    retrieval = (w_out[l], norm2_g[l], peer_wq[l], peer_keys[l])

    def mixers(seq, lo, hi, state):
        att = _attn_prompt(aq, ak, av, rel_bias[l], seq, sp, lo, hi)
        h, c, nn, mm = _mlstm(mqk, mv, mo, gates, seq * sp + lo, 1, hi - lo, *state, *mparams)
        conv_rows = mqk[seq * sp + hi - (CONV_W - 1):seq * sp + hi][None]
        return att, h, (c, nn[:, :M_HEADS], mm[:, :M_HEADS, 0], conv_rows)

    fresh = (zeros(1, M_HEADS, M_DH, M_DH), zeros(1, M_HEADS, M_DH), zeros(1, M_HEADS), zeros(1, CONV_W - 1, 2 * M_WIDTH))

    assert bp == 2 and 0 < SC_FIRST_ROWS < sp and 0 < SC_EXTRA_ROWS < sp
    n_sc = sp + SC_EXTRA_ROWS
    sc_tables = _sc_table(peer_u[l]), _sc_table(peer_v[l])
    att_0a, h_0a, state_0a = mixers(0, 0, SC_FIRST_ROWS, fresh)
    x1_a, xn_a, eidx_a, _, gate_a = _mid(xp, 0, xp, 0, att_0a, att_0a, h_0a, h_0a, SC_FIRST_ROWS, 0,
                                         jnp.zeros((SUBLANES, P_SLOTS), i32), *retrieval)
    peer_a = _sc_peer(*sc_tables, eidx_a, xn_a, gate_a)

    att_0b, h_0b, state_0 = mixers(0, SC_FIRST_ROWS, sp, state_0a)
    att_1a, h_1a, state_1a = mixers(1, 0, SC_EXTRA_ROWS, fresh)
    x1_c, xn_c, eidx_c, _, gate_c = _mid(xp, SC_FIRST_ROWS, xp, sp, att_0b, att_1a, h_0b, h_1a,
                                         sp - SC_FIRST_ROWS, SC_EXTRA_ROWS, eidx_a, *retrieval)
    peer_c = _sc_peer(*sc_tables, eidx_c, xn_c, gate_c)

    att_1b, h_1b, state_1 = mixers(1, SC_EXTRA_ROWS, sp, state_1a)
    lcache = cache_k.shape[2]
    att_s = _attn_sample(aq, ak, av, cache_k[l].reshape(bs, lcache, A_WIDTH),
                         cache_v[l].reshape(bs, lcache, A_WIDTH), rel_bias[l], n_p, bs, ts)
    h_s, c_s, nn_s, mm_s = _mlstm(mqk, mv, mo, gates, n_p, bs, ts, state_C[l], state_n[l], state_m[l],
                                  state_conv[l], *mparams)
    x1_b, xn_b, _, eidx_tb, gate_b = _mid(xp, n_sc, xs, 0, att_1b, att_s, h_1b, h_s, n_p - n_sc, n_s, eidx_c,
                                          *retrieval)
    w_b = _peer_u(eidx_tb, xn_b, gate_b, _expert_table(peer_u[l]), n - n_sc)
    peer_b = _peer_v(eidx_tb, w_b, _expert_table(peer_v[l]))
    y_p, y_s = _final((peer_a, peer_c, peer_b), (x1_a, x1_c, x1_b), final_g, n_p)
    c_p, nn_p, mm_p = (jnp.concatenate(ab, axis=0) for ab in zip(state_0[:3], state_1[:3]))

    def tail(a, row0, bsz, t, keep):
        return jnp.stack([a[row0 + (b + 1) * t - keep:row0 + (b + 1) * t] for b in range(bsz)])

    keep = min(WINDOW, sp)
    heads = lambda a: a.reshape(a.shape[0], a.shape[1], A_HEADS, A_DH)
    ctail = CONV_W - 1
    conv_tail = lambda buf, a, row0, bsz, t: jnp.concatenate([buf.astype(a.dtype), tail(a, row0, bsz, t, min(ctail, t))],
                                                             axis=1)[:, -ctail:]
    st = lambda a: a[None]
    return (y_p.reshape(bp, sp, d), y_s.reshape(bs, ts, d),
            st(heads(tail(ak, 0, bp, sp, keep))), st(heads(tail(av, 0, bp, sp, keep))),
            st(c_p), st(nn_p), st(mm_p),
            st(conv_tail(zeros(bp, ctail, 2 * M_WIDTH), mqk, 0, bp, sp)),
            st(heads(ak[n_p:].reshape(bs, ts, A_WIDTH))), st(heads(av[n_p:].reshape(bs, ts, A_WIDTH))),
            st(c_s), st(nn_s[:, :M_HEADS]), st(mm_s[:, :M_HEADS, 0]),
            st(conv_tail(state_conv[l], mqk, n_p, bs, ts)))
```

```python
import functools
import math

import jax
import jax.numpy as jnp
from jax import lax
from jax.experimental import pallas as pl
from jax.experimental.pallas import tpu as pltpu
from jax.experimental.pallas import tpu_sc as plsc

f32 = jnp.float32
bf16 = jnp.bfloat16
i32 = jnp.int32

D_MODEL = 1024
CHUNK = 64
A_HEADS = 8
A_DH = 64
A_WIDTH = A_HEADS * A_DH
BAND_CHUNKS = 8
WINDOW = BAND_CHUNKS * CHUNK
MAX_REL = 128
ATT_SCALE = A_DH ** -0.5
M_HEADS = 4
M_DH = 128
M_WIDTH = M_HEADS * M_DH
CONV_W = 4
P_HEADS = 8
P_DKEY = 256
N_KEYS = 128
P_TOPK = 16
P_SLOTS = P_HEADS * P_TOPK
EPS = 1e-6
NEG = -1e30

LANES = 128
SUBLANES = 8
ROW_BLOCK = 256
ATT_TILE = 512
ATT_SUB = 128
ATT_KEYS = ATT_SUB + WINDOW
ROW_TILES = D_MODEL // LANES
G_ROWS = P_SLOTS * ROW_TILES
TOKEN_UNROLL = 8
INDEX_GROUPS = P_SLOTS // SUBLANES
VMEM_LIMIT = 56 * 1024 * 1024

SC_WORKERS = 32
SC_LANES = 16
SC_UNIT_ROWS = P_SLOTS // 2
SC_ROW_GROUP = 16
SC_WORDS = D_MODEL // 2
SC_CHUNKS = SC_WORDS // SC_LANES
SC_CHUNK_GROUP = 8
SC_STAGE_ENDS = (1536, 6656, 19456)


def _rms(x, g):
    return x * lax.rsqrt(jnp.mean(x * x, axis=-1, keepdims=True) + EPS) * g


def _split_bf16(x):
    hi = x.astype(bf16)
    lo = (x - hi.astype(f32)).astype(bf16)
    return hi, lo


def _pair_specs(nbp, width, first=0, second=0):
    return (pl.BlockSpec((ROW_BLOCK, width), lambda i: (first + jnp.minimum(i, nbp - 1), 0)),
            pl.BlockSpec((ROW_BLOCK, width), lambda i: (second + jnp.maximum(i - nbp, 0), 0)))


def _pick(nbp, p_ref, s_ref):
    return jnp.where(pl.program_id(0) < nbp, p_ref[...], s_ref[...])


def _inproj_kernel(xp_ref, xs_ref, g_ref, w_ref, wgh_ref, wgl_ref,
                   aq_ref, ak_ref, av_ref, mqk_ref, mv_ref, mo_ref, gate_ref, *, nbp):
    xn = _rms(_pick(nbp, xp_ref, xs_ref), g_ref[...])
    xh, xl = _split_bf16(xn)

    def proj(lo, hi):
        return jnp.dot(xh, w_ref[:, lo:hi], preferred_element_type=f32)

    aq_ref[...] = proj(0, 512)
    ak_ref[...] = proj(512, 1024)
    av_ref[...] = proj(1024, 1536)
    mqk_ref[...] = proj(1536, 2560)
    mv_ref[...] = proj(2560, 3072)
    mo_ref[...] = proj(3072, 3584)
    gate_ref[...] = (jnp.dot(xh, wgh_ref[...], preferred_element_type=f32)
                     + jnp.dot(xl, wgh_ref[...], preferred_element_type=f32)
                     + jnp.dot(xh, wgl_ref[...], preferred_element_type=f32))


def _inproj(xp, xs, g1, w_in):
    n = xp.shape[0] + xs.shape[0]
    nbp = xp.shape[0] // ROW_BLOCK
    main = 3 * A_WIDTH + 4 * M_WIDTH
    w_main = w_in[:, :main].astype(bf16)
    wg = jnp.pad(w_in[:, main:], ((0, 0), (0, LANES - 2 * M_HEADS)))
    wgh, wgl = _split_bf16(wg)
    widths = (512, 512, 512, 1024, 512, 512, LANES)
    row = lambda w: pl.BlockSpec((ROW_BLOCK, w), lambda i: (i, 0))
    full = lambda a: pl.BlockSpec(a.shape, lambda i: (0,) * a.ndim)
    g = g1.reshape(1, D_MODEL)
    return pl.pallas_call(
        functools.partial(_inproj_kernel, nbp=nbp),
        grid=(n // ROW_BLOCK,),
        in_specs=[*_pair_specs(nbp, D_MODEL), full(g), full(w_main), full(wgh), full(wgl)],
        out_specs=[row(w) for w in widths],
        out_shape=[jax.ShapeDtypeStruct((n, w), f32) for w in widths],
        compiler_params=pltpu.CompilerParams(dimension_semantics=("parallel",), vmem_limit_bytes=VMEM_LIMIT),
        name="inproj",
    )(xp, xs, g, w_main, wgh, wgl)


def _attn_heads(q, k, v, bias_ref, key_ok):
    outs = []
    for h in range(A_HEADS):
        sl = slice(h * A_DH, (h + 1) * A_DH)
        s = lax.dot_general(q[:, sl], k[:, sl], (((1,), (1,)), ((), ())), preferred_element_type=f32)
        s = s * ATT_SCALE + bias_ref[h]
        if key_ok is not None:
            s = jnp.where(key_ok, s, NEG)
        m = jnp.max(s, axis=-1, keepdims=True)
        p = jnp.exp(s - m)
        l = jnp.sum(p, axis=-1, keepdims=True)
        o = jnp.dot(p.astype(bf16), v[:, sl], preferred_element_type=f32)
        outs.append(o / l)
    return jnp.concatenate(outs, axis=-1)


def _attn_prompt_kernel(q_ref, k0_ref, k1_ref, v0_ref, v1_ref, bias_ref, o_ref, *, t0):
    t = t0 + pl.program_id(1)
    q = q_ref[...].astype(bf16)
    k = jnp.concatenate([k0_ref[...], k1_ref[...]], axis=0).astype(bf16)
    v = jnp.concatenate([v0_ref[...], v1_ref[...]], axis=0).astype(bf16)
    col = lax.broadcasted_iota(i32, (1, ATT_KEYS), 1)
    for s in range(ATT_TILE // ATT_SUB):
        lo = s * ATT_SUB
        key_ok = (t * ATT_TILE + lo + col) >= WINDOW
        o_ref[lo:lo + ATT_SUB, :] = _attn_heads(q[lo:lo + ATT_SUB], k[lo:lo + ATT_KEYS], v[lo:lo + ATT_KEYS],
                                                 bias_ref, key_ok)


def _attn_sample_kernel(q_ref, k_ref, v_ref, bias_ref, o_ref):
    o_ref[...] = _attn_heads(q_ref[...].astype(bf16), k_ref[0].astype(bf16), v_ref[0].astype(bf16), bias_ref, None)


def _rel_bias_table(rel_bias, rows, cols, offset, valid):
    span = rows + cols - 1
    rel = offset + rows - 1 - jnp.arange(span)
    diag = rel_bias[:, jnp.clip(rel, -MAX_REL, MAX_REL) + MAX_REL].astype(f32)
    diag = jnp.pad(diag, ((0, 0), (0, 1)))
    flat = jnp.tile(diag, (1, rows))[:, rows - 1:rows - 1 + rows * span]
    return jnp.where(valid[None], flat.reshape(-1, rows, span)[:, :, :cols], NEG)


def _attn_prompt(q, k, v, rel_bias, seq, s, lo, hi):
    assert s % ATT_TILE == 0 and WINDOW == ATT_TILE and lo % ATT_TILE == 0 and hi % ATT_TILE == 0
    nt, t0, cnt = s // ATT_TILE, lo // ATT_TILE, (hi - lo) // ATT_TILE
    i = jnp.arange(ATT_SUB)[:, None]
    j = jnp.arange(ATT_KEYS)[None, :]
    off = j - (i // CHUNK) * CHUNK
    bias = _rel_bias_table(rel_bias, ATT_SUB, ATT_KEYS, WINDOW, (off >= 0) & (off < WINDOW + CHUNK))
    cur = pl.BlockSpec((ATT_TILE, A_WIDTH), lambda b, t: (seq * nt + t0 + t, 0))
    prev = pl.BlockSpec((ATT_TILE, A_WIDTH), lambda b, t: (seq * nt + jnp.maximum(t0 + t - 1, 0), 0))
    return pl.pallas_call(
        functools.partial(_attn_prompt_kernel, t0=t0),
        grid=(1, cnt),
        in_specs=[cur, prev, cur, prev, cur, pl.BlockSpec(bias.shape, lambda b, t: (0, 0, 0))],
        out_specs=pl.BlockSpec((ATT_TILE, A_WIDTH), lambda b, t: (t, 0)),
        out_shape=jax.ShapeDtypeStruct((hi - lo, A_WIDTH), f32),
        compiler_params=pltpu.CompilerParams(dimension_semantics=("parallel", "parallel"),
                                             vmem_limit_bytes=VMEM_LIMIT),
        name="attn_prompt",
    )(q, k, k, v, v, bias)


def _attn_sample(q, k, v, ck, cv, rel_bias, row0, bsz, t):
    l = ck.shape[1]
    assert row0 % t == 0
    keys = -(-(l + t) // LANES) * LANES
    padk = ((0, 0), (0, keys - l - t), (0, 0))
    kk = jnp.pad(jnp.concatenate([ck, k[row0:].reshape(bsz, t, A_WIDTH)], axis=1), padk)
    vv = jnp.pad(jnp.concatenate([cv, v[row0:].reshape(bsz, t, A_WIDTH)], axis=1), padk)
    j = jnp.arange(keys)[None, :]
    bias = _rel_bias_table(rel_bias, t, keys, l, jnp.broadcast_to(j < l + t, (t, keys)))
    return pl.pallas_call(
        _attn_sample_kernel,
        grid=(bsz,),
        in_specs=[pl.BlockSpec((t, A_WIDTH), lambda b: (row0 // t + b, 0)),
                  pl.BlockSpec((1, keys, A_WIDTH), lambda b: (b, 0, 0)),
                  pl.BlockSpec((1, keys, A_WIDTH), lambda b: (b, 0, 0)),
                  pl.BlockSpec(bias.shape, lambda b: (0, 0, 0))],
        out_specs=pl.BlockSpec((t, A_WIDTH), lambda b: (b, 0)),
        out_shape=jax.ShapeDtypeStruct((bsz * t, A_WIDTH), f32),
        compiler_params=pltpu.CompilerParams(dimension_semantics=("parallel",), vmem_limit_bytes=VMEM_LIMIT),
        name="attn_sample",
    )(q, kk, vv, bias)


def _mlstm_chunk(a, vall, o_in, gate_in, cw_ref, cb_ref, bg_ref, mhg_ref, c_s, n_s, m_s, prev_s):
    lc = a.shape[0]
    ext = jnp.concatenate([prev_s[...], a], axis=0)
    conv = cb_ref[...]
    for j in range(CONV_W):
        lo = SUBLANES - (CONV_W - 1) + j
        conv = conv + cw_ref[j:j + 1, :] * ext[lo:lo + lc]
    prev_s[...] = a[lc - SUBLANES:lc]
    qk = conv * jax.nn.sigmoid(conv)

    z = gate_in + bg_ref[...]
    lane = lax.broadcasted_iota(i32, (lc, LANES), 1)
    row = lax.broadcasted_iota(i32, (lc, LANES), 0)
    logf = jnp.minimum(z, 0.0) - jnp.log1p(jnp.exp(-jnp.abs(z)))
    cum = jnp.where((lane >= M_HEADS) & (lane < 2 * M_HEADS), logf, 0.0)
    shift = 1
    while shift < lc:
        cum = cum + jnp.where(row >= shift, pltpu.roll(cum, shift, axis=0), 0.0)
        shift *= 2
    zc = jnp.where(lane < M_HEADS, z, cum)
    zt = jnp.concatenate([zc, jnp.zeros((LANES - lc, LANES), f32)], axis=0).T[:, :lc]

    ri = lax.broadcasted_iota(i32, (lc, lc), 0)
    ci = lax.broadcasted_iota(i32, (lc, lc), 1)
    causal = ri >= ci
    state = [(m_s[h:h + 1, 0:1], c_s[h], n_s[h:h + 1, :]) for h in range(M_HEADS)]
    hs, new_state = [], []
    for h in range(M_HEADS):
        sl = slice(h * M_DH, (h + 1) * M_DH)
        q = qk[:, sl]
        k = qk[:, M_WIDTH + h * M_DH:M_WIDTH + (h + 1) * M_DH] * (M_DH ** -0.5)
        v = vall[:, sl]
        i_col = zc[:, h:h + 1]
        b_col = zc[:, M_HEADS + h:M_HEADS + h + 1]
        i_row = zt[h:h + 1, :]
        b_row = zt[M_HEADS + h:M_HEADS + h + 1, :]
        m_prev, c_prev, n_prev = state[h]

        dmat = jnp.where(causal, b_col - b_row + i_row, NEG)
        inter = b_col + m_prev
        mt = jnp.maximum(jnp.max(dmat, axis=-1, keepdims=True), inter)
        w_intra = jnp.exp(dmat - mt)
        w_inter = jnp.exp(inter - mt)
        qb, kb, vb = q.astype(bf16), k.astype(bf16), v.astype(bf16)
        s = lax.dot_general(qb, kb, (((1,), (1,)), ((), ())), preferred_element_type=f32) * w_intra
        num = (w_inter * jnp.dot(qb, c_prev.astype(bf16), preferred_element_type=f32)
               + jnp.dot(s.astype(bf16), vb, preferred_element_type=f32))
        den = w_inter * jnp.sum(q * n_prev, axis=-1, keepdims=True) + jnp.sum(s, axis=-1, keepdims=True)
        hh = num / jnp.maximum(jnp.abs(den), jnp.exp(-mt))
        m_new = mt[lc - 1:lc, :]
        b_last = b_col[lc - 1:lc, :]
        w_s = jnp.exp(b_last - b_col + i_col - m_new)
        decay = jnp.exp(b_last + m_prev - m_new)
        kw = k * w_s
        new_state.append((jnp.broadcast_to(m_new, (1, LANES)),
                          decay * c_prev + lax.dot_general(kw.astype(bf16), vb, (((0,), (0,)), ((), ())),
                                                           preferred_element_type=f32),
                          decay * n_prev + jnp.sum(kw, axis=0, keepdims=True)))
        hs.append(hh * lax.rsqrt(jnp.mean(hh * hh, axis=-1, keepdims=True) + EPS))

    for h, (m_new, c_new, n_new) in enumerate(new_state):
        m_s[h:h + 1, :] = m_new
        c_s[h] = c_new
        n_s[h:h + 1, :] = n_new
    return jnp.concatenate(hs, axis=-1) * mhg_ref[...] * jax.nn.sigmoid(o_in)


def _mlstm_kernel(qk_ref, v_ref, o_ref, gate_ref, c0_ref, n0_ref, m0_ref, cbuf_ref,
                  cw_ref, cb_ref, bg_ref, mhg_ref,
                  h_ref, cout_ref, nout_ref, mout_ref,
                  c_s, n_s, m_s, prev_s):
    c = pl.program_id(1)

    @pl.when(c == 0)
    def _():
        c_s[...] = c0_ref[0]
        n_s[...] = n0_ref[0]
        m_s[...] = m0_ref[0]
        prev_s[...] = cbuf_ref[0]

    h_ref[...] = _mlstm_chunk(qk_ref[...], v_ref[...], o_ref[...], gate_ref[...],
                              cw_ref, cb_ref, bg_ref, mhg_ref, c_s, n_s, m_s, prev_s)

    @pl.when(c == pl.num_programs(1) - 1)
    def _():
        cout_ref[0] = c_s[...]
        nout_ref[0] = n_s[...]
        mout_ref[0] = m_s[...]


def _mlstm(mqk, mv, mo, gates, row0, bsz, t, c0, n0, m0, cbuf, conv_w, conv_b, b_gates, mh_g):
    lc = min(CHUNK, t)
    step = lc
    nc = t // step
    assert t % step == 0 and lc % SUBLANES == 0 and row0 % step == 0
    n0p = jnp.pad(n0.astype(f32), ((0, 0), (0, SUBLANES - M_HEADS), (0, 0)))
    m0p = jnp.pad(jnp.broadcast_to(m0.astype(f32)[:, :, None], (bsz, M_HEADS, LANES)),
                  ((0, 0), (0, SUBLANES - M_HEADS), (0, 0)))
    cbp = jnp.pad(cbuf.astype(f32), ((0, 0), (SUBLANES - (CONV_W - 1), 0), (0, 0)))
    bg = jnp.pad(b_gates.astype(f32), (0, LANES - 2 * M_HEADS)).reshape(1, LANES)
    seq = lambda w: pl.BlockSpec((step, w), lambda b, c: (row0 // step + b * nc + c, 0))
    out_seq = pl.BlockSpec((step, M_WIDTH), lambda b, c: (b * nc + c, 0))
    per_b = lambda shp: pl.BlockSpec((1,) + shp, lambda b, c: (b,) + (0,) * len(shp))
    full = lambda a: pl.BlockSpec(a.shape, lambda b, c: (0,) * a.ndim)
    cb = conv_b.reshape(1, -1)
    mhg = mh_g.reshape(1, -1)
    return pl.pallas_call(
        _mlstm_kernel,
        grid=(bsz, nc),
        in_specs=[seq(2 * M_WIDTH), seq(M_WIDTH), seq(M_WIDTH), seq(LANES),
                  per_b((M_HEADS, M_DH, M_DH)), per_b((SUBLANES, M_DH)), per_b((SUBLANES, LANES)),
                  per_b((SUBLANES, 2 * M_WIDTH)),
                  full(conv_w), full(cb), full(bg), full(mhg)],
        out_specs=[out_seq, per_b((M_HEADS, M_DH, M_DH)), per_b((SUBLANES, M_DH)), per_b((SUBLANES, LANES))],
        out_shape=[jax.ShapeDtypeStruct((bsz * t, M_WIDTH), f32),
                   jax.ShapeDtypeStruct((bsz, M_HEADS, M_DH, M_DH), f32),
                   jax.ShapeDtypeStruct((bsz, SUBLANES, M_DH), f32),
                   jax.ShapeDtypeStruct((bsz, SUBLANES, LANES), f32)],
        scratch_shapes=[pltpu.VMEM((M_HEADS, M_DH, M_DH), f32), pltpu.VMEM((SUBLANES, M_DH), f32),
                        pltpu.VMEM((SUBLANES, LANES), f32), pltpu.VMEM((SUBLANES, 2 * M_WIDTH), f32)],
        compiler_params=pltpu.CompilerParams(dimension_semantics=("parallel", "arbitrary"),
                                             vmem_limit_bytes=VMEM_LIMIT),
        name="mlstm",
    )(mqk, mv, mo, gates, c0.astype(f32), n0p, m0p, cbp, conv_w, cb, bg, mhg)


def _topk_rows(s, k, payload=None):
    n = s.shape[0]
    rows = lax.broadcasted_iota(i32, s.shape, 0).astype(f32)
    vals, ids = [], []
    for _ in range(k):
        m = jnp.max(s, axis=0, keepdims=True)
        pos = jnp.min(jnp.where(s == m, rows, float(n)), axis=0, keepdims=True)
        sel = rows == pos
        vals.append(m)
        ids.append(pos if payload is None else jnp.max(jnp.where(sel, payload, -1.0), axis=0, keepdims=True))
        s = jnp.where(sel, -jnp.inf, s)
    return jnp.concatenate(vals, axis=0), jnp.concatenate(ids, axis=0)


def _mid_kernel(xp_ref, xs_ref, attp_ref, atts_ref, hp_ref, hs_ref, wo_ref, g2_ref, wq_ref, keys_ref, after_ref,
                x1_ref, xn_ref, eidx_ref, eidx_t_ref, gate_ref, *, nbp):
    del after_ref
    cat = jnp.concatenate([_pick(nbp, attp_ref, atts_ref), _pick(nbp, hp_ref, hs_ref)], axis=-1).astype(bf16)
    x1 = _pick(nbp, xp_ref, xs_ref) + jnp.dot(cat, wo_ref[...], preferred_element_type=f32)
    x1_ref[...] = x1
    xn = _rms(x1, g2_ref[...])
    xn_ref[...] = xn
    xb = xn.astype(bf16)
    e_rows, g_rows = [], []
    for h in range(P_HEADS):
        q = jnp.dot(xb, wq_ref[:, h * P_DKEY:(h + 1) * P_DKEY], preferred_element_type=f32).astype(bf16)
        half = []
        for p in range(2):
            st = lax.dot_general(keys_ref[h, p], q[:, p * N_KEYS:(p + 1) * N_KEYS], (((1,), (1,)), ((), ())),
                                 preferred_element_type=f32)
            half.append(_topk_rows(st, P_TOPK))
        (v0, i0), (v1, i1) = half
        width = [P_TOPK // (a + 1) for a in range(P_TOPK)]
        fill = -sum(width) % SUBLANES
        cand = jnp.concatenate([v0[a:a + 1] + v1[:width[a]] for a in range(P_TOPK)]
                               + [jnp.full((fill, v0.shape[1]), -jnp.inf, f32)], axis=0)
        cidx = jnp.concatenate([i0[a:a + 1] * float(N_KEYS) + i1[:width[a]] for a in range(P_TOPK)]
                               + [jnp.zeros((fill, v0.shape[1]), f32)], axis=0)
        top_s, eid = _topk_rows(cand, P_TOPK, payload=cidx)
        ex = jnp.exp(top_s - top_s[0:1])
        e_rows.append(eid)
        g_rows.append(ex / jnp.sum(ex, axis=0, keepdims=True))
    eidx_t = jnp.concatenate(e_rows, axis=0)
    eidx_t_ref[...] = eidx_t.astype(i32)
    eidx_ref[...] = eidx_t.T.astype(i32)
    gate_ref[...] = jnp.concatenate(g_rows, axis=0).T


def _mid(xp, xp_row0, xs, xs_row0, attp, atts, hp, hs, n_a, n_b, after, w_out, g2, wq, keys):
    n = n_a + n_b
    nbp = n_a // ROW_BLOCK
    assert n_a % ROW_BLOCK == 0 and n_b % ROW_BLOCK == 0 and xp_row0 % ROW_BLOCK == 0 and xs_row0 % ROW_BLOCK == 0
    row = lambda w: pl.BlockSpec((ROW_BLOCK, w), lambda i: (i, 0))
    full = lambda a: pl.BlockSpec(a.shape, lambda i: (0,) * a.ndim)
    wo = w_out.astype(bf16)
    wqb = wq.astype(bf16)
    kb = keys.astype(bf16)
    g = g2.reshape(1, D_MODEL)
    return pl.pallas_call(
        functools.partial(_mid_kernel, nbp=nbp),
        grid=(n // ROW_BLOCK,),
        in_specs=[*_pair_specs(nbp, D_MODEL, xp_row0 // ROW_BLOCK, xs_row0 // ROW_BLOCK), *_pair_specs(nbp, A_WIDTH),
                  *_pair_specs(nbp, M_WIDTH), full(wo), full(g), full(wqb), full(kb),
                  pl.BlockSpec((SUBLANES, P_SLOTS), lambda i: (0, 0))],
        out_specs=[row(D_MODEL), row(D_MODEL), row(P_SLOTS), pl.BlockSpec((P_SLOTS, ROW_BLOCK), lambda i: (0, i)),
                   row(P_SLOTS)],
        out_shape=[jax.ShapeDtypeStruct((n, D_MODEL), f32), jax.ShapeDtypeStruct((n, D_MODEL), f32),
                   jax.ShapeDtypeStruct((n, P_SLOTS), i32), jax.ShapeDtypeStruct((P_SLOTS, n), i32),
                   jax.ShapeDtypeStruct((n, P_SLOTS), f32)],
        compiler_params=pltpu.CompilerParams(dimension_semantics=("parallel",), vmem_limit_bytes=VMEM_LIMIT),
        name="outproj_retrieve",
    )(xp, xs, attp, atts, hp, hs, wo, g, wqb, kb, after)


def _gather_rows(eidx_refs, tab_ref, g_ref, t):
    for i in range(SUBLANES):
        for k, eidx_ref in enumerate(eidx_refs):
            r = k * SUBLANES + i
            g_ref[pl.ds(r * ROW_TILES, ROW_TILES), :] = tab_ref[eidx_ref[i, t]]


def _tile_rows(t, rows=SUBLANES):
    return pl.ds(pl.multiple_of(t * rows, rows), rows)


def _pipelined_tokens(nt, eidx_refs, tab_ref, g0_s, g1_s, compute):
    bufs = (g0_s, g1_s)
    _gather_rows(eidx_refs, tab_ref, g0_s, 0)

    def body(j, carry):
        for u in range(TOKEN_UNROLL):
            t = TOKEN_UNROLL * j + u
            _gather_rows(eidx_refs, tab_ref, bufs[(u + 1) % 2], jnp.minimum(t + 1, nt - 1))
            compute(t, bufs[u % 2])
        return carry

    lax.fori_loop(0, nt // TOKEN_UNROLL, body, 0)


def _gelu_tanh(x):
    return 0.5 * x * (1.0 + jnp.tanh(math.sqrt(2.0 / math.pi) * (x + 0.044715 * (x * x * x))))


def _peer_u_kernel(*refs):
    eidx_refs, (xn_ref, g_ref, tab_ref, w_ref, xl_s, r_s, g0_s, g1_s) = refs[:INDEX_GROUPS], refs[INDEX_GROUPS:]
    nt = xn_ref.shape[0]
    xn = xn_ref[...]
    xh = xn.astype(bf16).astype(f32)
    xl = xn - xh
    for k in range(ROW_TILES):
        xl_s[pl.ds(k, nt, stride=2 * SUBLANES), :] = xh[:, k * LANES:(k + 1) * LANES]
        xl_s[pl.ds(SUBLANES + k, nt, stride=2 * SUBLANES), :] = xl[:, k * LANES:(k + 1) * LANES]
    diag = (lax.broadcasted_iota(i32, (SUBLANES, G_ROWS), 1) % ROW_TILES
            == lax.broadcasted_iota(i32, (SUBLANES, G_ROWS), 0))

    def compute(t, g_s):
        lhs = xl_s[_tile_rows(t, 2 * SUBLANES), :].astype(bf16)
        out = lax.dot_general(lhs, g_s[...], (((1,), (1,)), ((), ())), preferred_element_type=f32)
        part = jnp.where(diag, out[:SUBLANES] + out[SUBLANES:], 0.0)
        for c in range(ROW_TILES):
            r_s[c, _tile_rows(t), :] = part[:, c * LANES:(c + 1) * LANES]

    _pipelined_tokens(nt, eidx_refs, tab_ref, g0_s, g1_s, compute)
    cols = []
    for c in range(ROW_TILES):
        acc = r_s[c, pl.ds(0, nt, stride=SUBLANES), :]
        for k in range(1, SUBLANES):
            acc = acc + r_s[c, pl.ds(k, nt, stride=SUBLANES), :]
        cols.append(acc)
    s = jnp.concatenate(cols, axis=-1)
    fold = (lax.broadcasted_iota(i32, (G_ROWS, P_SLOTS), 0) // ROW_TILES
            == lax.broadcasted_iota(i32, (G_ROWS, P_SLOTS), 1)).astype(bf16)
    sh, sl = _split_bf16(s)
    act = jnp.dot(sh, fold, preferred_element_type=f32) + jnp.dot(sl, fold, preferred_element_type=f32)
    w_ref[...] = g_ref[...] * _gelu_tanh(act)


def _peer_v_kernel(*refs):
    eidx_refs, (w_ref, tab_ref, peer_ref, wl_s, o_s, g0_s, g1_s) = refs[:INDEX_GROUPS], refs[INDEX_GROUPS:]
    nt = w_ref.shape[0]
    spread = (lax.broadcasted_iota(i32, (P_SLOTS, G_ROWS), 1) // ROW_TILES
              == lax.broadcasted_iota(i32, (P_SLOTS, G_ROWS), 0)).astype(bf16)
    wexp = jnp.dot(w_ref[...].astype(bf16), spread, preferred_element_type=f32)
    lane = lax.broadcasted_iota(i32, (nt, LANES), 1)
    for c in range(ROW_TILES):
        wc = wexp[:, c * LANES:(c + 1) * LANES]
        for k in range(SUBLANES):
            wl_s[c, pl.ds(k, nt, stride=SUBLANES), :] = jnp.where(lane % ROW_TILES == k, wc, 0.0)

    def compute(t, g_s):
        lhs = jnp.concatenate([wl_s[c, _tile_rows(t), :] for c in range(ROW_TILES)], axis=-1).astype(bf16)
        o_s[_tile_rows(t), :] = jnp.dot(lhs, g_s[...], preferred_element_type=f32)

    _pipelined_tokens(nt, eidx_refs, tab_ref, g0_s, g1_s, compute)
    for k in range(ROW_TILES):
        peer_ref[:, k * LANES:(k + 1) * LANES] = o_s[pl.ds(k, nt, stride=SUBLANES), :]


def _final_kernel(*refs, starts, nbp):
    k = len(starts)
    peer_refs, x_refs, (gf_ref, yp_ref, ys_ref) = refs[:k], refs[k:2 * k], refs[2 * k:]
    resid = x_refs[0][...] + peer_refs[0][...]
    for start, p_ref, x_ref in zip(starts[1:], peer_refs[1:], x_refs[1:]):
        resid = jnp.where(pl.program_id(0) >= start, x_ref[...] + p_ref[...], resid)
    y = _rms(resid, gf_ref[...])

    @pl.when(pl.program_id(0) < nbp)
    def _():
        yp_ref[...] = y

    @pl.when(pl.program_id(0) >= nbp)
    def _():
        ys_ref[...] = y


def _final(peers, x1s, gf, n_p):
    counts = [x.shape[0] // ROW_BLOCK for x in x1s]
    starts = [sum(counts[:k]) for k in range(len(counts))]
    n, nbp = sum(counts) * ROW_BLOCK, n_p // ROW_BLOCK
    g = gf.reshape(1, D_MODEL)
    seg = [pl.BlockSpec((ROW_BLOCK, D_MODEL), lambda i, s=s, c=c: (jnp.clip(i - s, 0, c - 1), 0))
           for s, c in zip(starts, counts)]
    return pl.pallas_call(
        functools.partial(_final_kernel, starts=tuple(starts), nbp=nbp),
        grid=(n // ROW_BLOCK,),
        in_specs=[*seg, *seg, pl.BlockSpec(g.shape, lambda i: (0, 0))],
        out_specs=list(_pair_specs(nbp, D_MODEL)),
        out_shape=[jax.ShapeDtypeStruct((n_p, D_MODEL), f32), jax.ShapeDtypeStruct((n - n_p, D_MODEL), f32)],
        compiler_params=pltpu.CompilerParams(dimension_semantics=("arbitrary",), vmem_limit_bytes=VMEM_LIMIT),
        name="final_norm",
    )(*peers, *x1s, g)


def _peer_specs():
    row = lambda w: pl.BlockSpec((ROW_BLOCK, w), lambda i: (i, 0))
    idx = [pl.BlockSpec((SUBLANES, ROW_BLOCK), lambda i, k=k: (k, i), memory_space=pltpu.SMEM)
           for k in range(INDEX_GROUPS)]
    tab = pl.BlockSpec(memory_space=pltpu.VMEM)
    gscr = pltpu.VMEM((G_ROWS, LANES), bf16)
    params = pltpu.CompilerParams(dimension_semantics=("arbitrary",), vmem_limit_bytes=VMEM_LIMIT)
    return row, idx, tab, gscr, params


def _expert_table(tab):
    return tab.astype(bf16).reshape(tab.shape[0], ROW_TILES, LANES)


def _peer_u(eidx_t, xn, g, utab, n):
    row, idx, tab, gscr, params = _peer_specs()
    return pl.pallas_call(
        _peer_u_kernel,
        grid=(n // ROW_BLOCK,),
        in_specs=[*idx, row(D_MODEL), row(P_SLOTS), tab],
        out_specs=row(P_SLOTS),
        out_shape=jax.ShapeDtypeStruct((n, P_SLOTS), f32),
        scratch_shapes=[pltpu.VMEM((ROW_BLOCK * 2 * SUBLANES, LANES), f32),
                        pltpu.VMEM((ROW_TILES, ROW_BLOCK * SUBLANES, LANES), f32), gscr, gscr],
        compiler_params=params,
        name="peer_u",
    )(*[eidx_t] * INDEX_GROUPS, xn, g, utab)


def _peer_v(eidx_t, w, vtab):
    n = w.shape[0]
    row, idx, tab, gscr, params = _peer_specs()
    return pl.pallas_call(
        _peer_v_kernel,
        grid=(n // ROW_BLOCK,),
        in_specs=[*idx, row(P_SLOTS), tab],
        out_specs=row(D_MODEL),
        out_shape=jax.ShapeDtypeStruct((n, D_MODEL), f32),
        scratch_shapes=[pltpu.VMEM((ROW_TILES, ROW_BLOCK * SUBLANES, LANES), f32),
                        pltpu.VMEM((ROW_BLOCK * SUBLANES, LANES), f32), gscr, gscr],
        compiler_params=params,
        name="peer_v",
    )(*[eidx_t] * INDEX_GROUPS, w, vtab)


def _sc_table_kernel(t_ref, o_ref):
    t = t_ref[...]
    o_ref[...] = pltpu.pack_elementwise([t[:, :SC_WORDS], t[:, SC_WORDS:]], packed_dtype=bf16)


def _sc_table(tab):
    e = tab.shape[0]
    return pl.pallas_call(
        _sc_table_kernel,
        grid=(e // ROW_BLOCK,),
        in_specs=[pl.BlockSpec((ROW_BLOCK, D_MODEL), lambda i: (i, 0))],
        out_specs=pl.BlockSpec((ROW_BLOCK, SC_WORDS), lambda i: (i, 0)),
        out_shape=jax.ShapeDtypeStruct((e, SC_WORDS), jnp.uint32),
        compiler_params=pltpu.CompilerParams(dimension_semantics=("parallel",), vmem_limit_bytes=VMEM_LIMIT),
        name="sc_table",
    )(tab)


def _sc_unpack(words):
    return plsc.unpack(plsc.bitcast(words, bf16), format=plsc.PackFormat.INTERLEAVED)


def _sc_gelu_tanh(x):
    z = math.sqrt(2.0 / math.pi) * (x + 0.044715 * (x * x * x))
    return 0.5 * x * (2.0 - 2.0 / (jnp.exp(2.0 * z) + 1.0))


def _sc_peer(utab32, vtab32, eidx, xn, gate):
    n_sc = eidx.shape[0]
    per = n_sc // SC_WORKERS
    units = 4 * per
    assert n_sc % (SC_WORKERS * SUBLANES) == 0

    def body(u_hbm, v_hbm, eidx_hbm, x_hbm, g_hbm, out_hbm,
             idx_v, x_v, g_v, rows_v, acc_v, w_v, w16_v, out_v, row_sems, tok_sems, out_sem):
        base = (lax.axis_index("s") * 2 + lax.axis_index("c")) * per
        lanes = lax.broadcasted_iota(i32, (SC_LANES,), 0)

        def token_copies(tok):
            tslot = tok % 2
            return (pltpu.make_async_copy(eidx_hbm.at[base + tok], idx_v.at[tslot], tok_sems.at[0]),
                    pltpu.make_async_copy(x_hbm.at[base + tok], x_v.at[tslot], tok_sems.at[1]),
                    pltpu.make_async_copy(g_hbm.at[base + tok], g_v.at[tslot], tok_sems.at[2]))

        def store_out(tok):
            return pltpu.make_async_copy(out_v, out_hbm.at[base + tok], out_sem)

        def gather(tab_hbm, g):
            tok, k = g // 4, g % 4
            return pltpu.make_async_copy(tab_hbm.at[idx_v.at[tok % 2, pl.ds((k % 2) * SC_UNIT_ROWS, SC_UNIT_ROWS)]],
                                         rows_v.at[k % 2], row_sems.at[k % 2])

        def start(g):
            @pl.when(g % 4 < 2)
            def _():
                gather(u_hbm, g).start()

            @pl.when(g % 4 >= 2)
            def _():
                gather(v_hbm, g).start()

        def compute_u(tslot, half, rows):
            @pl.loop(0, SC_UNIT_ROWS // SC_ROW_GROUP)
            def _(rg):
                slot0 = half * SC_UNIT_ROWS + rg * SC_ROW_GROUP

                @pl.loop(0, SC_CHUNKS // SC_CHUNK_GROUP)
                def _(cg):
                    keep = jnp.where(cg > 0, 1.0, 0.0).astype(f32)
                    accs = [acc_v[pl.ds((slot0 + r) * SC_LANES, SC_LANES)] * keep for r in range(SC_ROW_GROUP)]
                    for c in range(SC_CHUNK_GROUP):
                        ch = cg * SC_CHUNK_GROUP + c
                        xa = x_v[tslot, pl.ds(ch * SC_LANES, SC_LANES)]
                        xb = x_v[tslot, pl.ds(SC_WORDS + ch * SC_LANES, SC_LANES)]
                        for r in range(SC_ROW_GROUP):
                            a, b = _sc_unpack(rows[rg * SC_ROW_GROUP + r, pl.ds(ch * SC_LANES, SC_LANES)])
                            accs[r] = accs[r] + a * xa + b * xb
                    for r in range(SC_ROW_GROUP):
                        acc_v[pl.ds((slot0 + r) * SC_LANES, SC_LANES)] = accs[r]

        def gate_weights(tslot):
            @pl.loop(0, P_SLOTS // SC_LANES)
            def _(sg):
                first = sg * SC_LANES * SC_LANES
                act = jnp.zeros((SC_LANES,), f32)
                for lane in range(SC_LANES):
                    act = act + plsc.load_gather(acc_v, [first + lanes * SC_LANES + lane])
                w_v[pl.ds(sg * SC_LANES, SC_LANES)] = g_v[tslot, pl.ds(sg * SC_LANES, SC_LANES)] * _sc_gelu_tanh(act)

            @pl.loop(0, P_SLOTS // SC_LANES)
            def _(sg):
                for r in range(SC_LANES):
                    w16_v[pl.ds((sg * SC_LANES + r) * SC_LANES, SC_LANES)] = plsc.load_gather(
                        w_v, [jnp.zeros((SC_LANES,), i32) + (sg * SC_LANES + r)])

        def compute_v(half, rows):
            @pl.loop(0, SC_CHUNKS // SC_CHUNK_GROUP)
            def _(cg):
                first = cg * SC_CHUNK_GROUP

                @pl.loop(0, SC_UNIT_ROWS // SC_ROW_GROUP)
                def _(rg):
                    keep = jnp.where(jnp.logical_or(half == 1, rg > 0), 1.0, 0.0).astype(f32)
                    spots = [pl.ds((j % 2) * SC_WORDS + (first + j // 2) * SC_LANES, SC_LANES)
                             for j in range(2 * SC_CHUNK_GROUP)]
                    accs = [out_v[spot] * keep for spot in spots]
                    for r in range(SC_ROW_GROUP):
                        row = rg * SC_ROW_GROUP + r
                        wv = w16_v[pl.ds((half * SC_UNIT_ROWS + row) * SC_LANES, SC_LANES)]
                        for c in range(SC_CHUNK_GROUP):
                            a, b = _sc_unpack(rows[row, pl.ds((first + c) * SC_LANES, SC_LANES)])
                            accs[2 * c] = accs[2 * c] + a * wv
                            accs[2 * c + 1] = accs[2 * c + 1] + b * wv
                    for spot, acc in zip(spots, accs):
                        out_v[spot] = acc

        for cp in token_copies(0):
            cp.start()
        for cp in token_copies(0):
            cp.wait()
        start(0)

        @pl.loop(0, units)
        def _(g):
            tok, k = g // 4, g % 4

            @pl.when(jnp.logical_and(k == 0, tok + 1 < per))
            def _():
                for cp in token_copies(tok + 1):
                    cp.start()

            @pl.when(jnp.logical_and(k == 3, tok + 1 < per))
            def _():
                for cp in token_copies(tok + 1):
                    cp.wait()

            @pl.when(g + 1 < units)
            def _():
                start(g + 1)

            gather(u_hbm, g).wait()
            rows = rows_v.at[k % 2]

            @pl.when(k < 2)
            def _():
                compute_u(tok % 2, k, rows)

            @pl.when(k == 1)
            def _():
                gate_weights(tok % 2)

            @pl.when(jnp.logical_and(k == 2, tok > 0))
            def _():
                store_out(tok - 1).wait()

            @pl.when(k >= 2)
            def _():
                compute_v(k - 2, rows)

            @pl.when(k == 3)
            def _():
                store_out(tok).start()

        store_out(per - 1).wait()

    return pl.kernel(
        body, mesh=plsc.VectorSubcoreMesh(core_axis_name="c", subcore_axis_name="s"),
        out_type=jax.ShapeDtypeStruct((n_sc, D_MODEL), f32),
        scratch_types=[pltpu.VMEM((2, P_SLOTS), i32), pltpu.VMEM((2, D_MODEL), f32), pltpu.VMEM((2, P_SLOTS), f32),
                       pltpu.VMEM((2, SC_UNIT_ROWS, SC_WORDS), jnp.uint32), pltpu.VMEM((P_SLOTS * SC_LANES,), f32),
                       pltpu.VMEM((P_SLOTS,), f32), pltpu.VMEM((P_SLOTS * SC_LANES,), f32), pltpu.VMEM((D_MODEL,), f32),
                       pltpu.SemaphoreType.DMA((2,)), pltpu.SemaphoreType.DMA((3,)), pltpu.SemaphoreType.DMA],
        compiler_params=pltpu.CompilerParams(needs_layout_passes=False),
        name="sc_peer",
    )(utab32, vtab32, eidx, xn, gate)


def kernel(x_prompt, x_sample, cache_k, cache_v, state_C, state_n, state_m, state_conv, norm1_g, w_in, b_gates, rel_bias, conv_w, conv_b, mh_norm_g, w_out, norm2_g, peer_wq, peer_keys, peer_u, peer_v, final_g):
    bp, sp, d = x_prompt.shape
    bs, ts, _ = x_sample.shape
    n_p, n_s = bp * sp, bs * ts
    n = n_p + n_s
    assert n_p % ROW_BLOCK == 0 and n_s % ROW_BLOCK == 0 and d == D_MODEL
    depth = w_in.shape[0]
    assert depth == 1, "the final norm is fused into the last layer's PEER pass"
    l = 0
    xp, xs = x_prompt.reshape(n_p, d), x_sample.reshape(n_s, d)

    aq, ak, av, mqk, mv, mo, gates = _inproj(xp, xs, norm1_g[l], w_in[l])
    zeros = lambda *shp: jnp.zeros(shp, f32)
    mparams = (conv_w[l], conv_b[l], b_gates[l], mh_norm_g[l])
    retrieval = (w_out[l], norm2_g[l], peer_wq[l], peer_keys[l])

    def mixers(seq, lo, hi, state):
        att = _attn_prompt(aq, ak, av, rel_bias[l], seq, sp, lo, hi)
        h, c, nn, mm = _mlstm(mqk, mv, mo, gates, seq * sp + lo, 1, hi - lo, *state, *mparams)
        conv_rows = mqk[seq * sp + hi - (CONV_W - 1):seq * sp + hi][None]
        return att, h, (c, nn[:, :M_HEADS], mm[:, :M_HEADS, 0], conv_rows)

    fresh = (zeros(1, M_HEADS, M_DH, M_DH), zeros(1, M_HEADS, M_DH), zeros(1, M_HEADS), zeros(1, CONV_W - 1, 2 * M_WIDTH))

    states = [fresh] * bp

    def stage(lo, hi, after, tail=None):
        parts = []
        for seq in range(bp):
            a, b = max(lo, seq * sp), min(hi, (seq + 1) * sp)
            if a < b:
                att, h, states[seq] = mixers(seq, a - seq * sp, b - seq * sp, states[seq])
                parts.append((att, h, b - a, xp, a))
        parts += [tail] if tail is not None else []
        assert 1 <= len(parts) <= 2, "the retrieval kernel reads at most two row sources"
        (att0, h0, n0, x0, r0), (att1, h1, n1, x1, r1) = parts[0], parts[-1]
        n1, r1 = (n1, r1) if len(parts) == 2 else (0, 0)
        return _mid(x0, r0, x1, r1, att0, att1, h0, h1, n0, n1, after, *retrieval)

    sc_tables = _sc_table(peer_u[l]), _sc_table(peer_v[l])
    peers, x1s, lo, after = [], [], 0, jnp.zeros((SUBLANES, P_SLOTS), i32)
    for hi in SC_STAGE_ENDS:
        x1, xn, eidx, _, gate = stage(lo, hi, after)
        peers.append(_sc_peer(*sc_tables, eidx, xn, gate))
        x1s.append(x1)
        lo, after = hi, eidx

    lcache = cache_k.shape[2]
    att_s = _attn_sample(aq, ak, av, cache_k[l].reshape(bs, lcache, A_WIDTH),
                         cache_v[l].reshape(bs, lcache, A_WIDTH), rel_bias[l], n_p, bs, ts)
    h_s, c_s, nn_s, mm_s = _mlstm(mqk, mv, mo, gates, n_p, bs, ts, state_C[l], state_n[l], state_m[l],
                                  state_conv[l], *mparams)
    x1, xn, _, eidx_t, gate = stage(lo, n_p, after, tail=(att_s, h_s, n_s, xs, 0))
    w = _peer_u(eidx_t, xn, gate, _expert_table(peer_u[l]), n - lo)
    peers.append(_peer_v(eidx_t, w, _expert_table(peer_v[l])))
    x1s.append(x1)
    y_p, y_s = _final(peers, x1s, final_g, n_p)
    c_p, nn_p, mm_p = (jnp.concatenate(per_seq, axis=0) for per_seq in zip(*(st[:3] for st in states)))

    def tail(a, row0, bsz, t, keep):
        return jnp.stack([a[row0 + (b + 1) * t - keep:row0 + (b + 1) * t] for b in range(bsz)])

    keep = min(WINDOW, sp)
    heads = lambda a: a.reshape(a.shape[0], a.shape[1], A_HEADS, A_DH)
    ctail = CONV_W - 1
    conv_tail = lambda buf, a, row0, bsz, t: jnp.concatenate([buf.astype(a.dtype), tail(a, row0, bsz, t, min(ctail, t))],
                                                             axis=1)[:, -ctail:]
    st = lambda a: a[None]
    return (y_p.reshape(bp, sp, d), y_s.reshape(bs, ts, d),
            st(heads(tail(ak, 0, bp, sp, keep))), st(heads(tail(av, 0, bp, sp, keep))),
            st(c_p), st(nn_p), st(mm_p),
            st(conv_tail(zeros(bp, ctail, 2 * M_WIDTH), mqk, 0, bp, sp)),
            st(heads(ak[n_p:].reshape(bs, ts, A_WIDTH))), st(heads(av[n_p:].reshape(bs, ts, A_WIDTH))),
            st(c_s), st(nn_s[:, :M_HEADS]), st(mm_s[:, :M_HEADS, 0]),
            st(conv_tail(state_conv[l], mqk, n_p, bs, ts)))
```

```python
import functools
import math

import jax
import jax.numpy as jnp
from jax import lax
from jax.experimental import pallas as pl
from jax.experimental.pallas import tpu as pltpu
from jax.experimental.pallas import tpu_sc as plsc

f32 = jnp.float32
bf16 = jnp.bfloat16
i32 = jnp.int32

D_MODEL = 1024
CHUNK = 64
A_HEADS = 8
A_DH = 64
A_WIDTH = A_HEADS * A_DH
BAND_CHUNKS = 8
WINDOW = BAND_CHUNKS * CHUNK
MAX_REL = 128
ATT_SCALE = A_DH ** -0.5
M_HEADS = 4
M_DH = 128
M_WIDTH = M_HEADS * M_DH
CONV_W = 4
P_HEADS = 8
P_DKEY = 256
N_KEYS = 128
P_TOPK = 16
P_SLOTS = P_HEADS * P_TOPK
EPS = 1e-6
NEG = -1e30

LANES = 128
SUBLANES = 8
ROW_BLOCK = 256
ATT_TILE = 512
ATT_SUB = 128
ATT_KEYS = ATT_SUB + WINDOW
ROW_TILES = D_MODEL // LANES
G_ROWS = P_SLOTS * ROW_TILES
TOKEN_UNROLL = 8
INDEX_GROUPS = P_SLOTS // SUBLANES
VMEM_LIMIT = 56 * 1024 * 1024

SC_WORKERS = 32
SC_LANES = 16
SC_UNIT_ROWS = P_SLOTS // 2
SC_ROW_GROUP = 16
SC_WORDS = D_MODEL // 2
SC_CHUNKS = SC_WORDS // SC_LANES
SC_CHUNK_GROUP = 8
SC_STAGE_ENDS = (1536, 6656, 19968)


def _rms(x, g):
    return x * lax.rsqrt(jnp.mean(x * x, axis=-1, keepdims=True) + EPS) * g


def _split_bf16(x):
    hi = x.astype(bf16)
    lo = (x - hi.astype(f32)).astype(bf16)
    return hi, lo


def _pair_specs(nbp, width, first=0, second=0):
    return (pl.BlockSpec((ROW_BLOCK, width), lambda i: (first + jnp.minimum(i, nbp - 1), 0)),
            pl.BlockSpec((ROW_BLOCK, width), lambda i: (second + jnp.maximum(i - nbp, 0), 0)))


def _pick(nbp, p_ref, s_ref):
    return jnp.where(pl.program_id(0) < nbp, p_ref[...], s_ref[...])


def _inproj_kernel(xp_ref, xs_ref, g_ref, w_ref, wgh_ref, wgl_ref,
                   aq_ref, ak_ref, av_ref, mqk_ref, mv_ref, mo_ref, gate_ref, *, nbp):
    xn = _rms(_pick(nbp, xp_ref, xs_ref), g_ref[...])
    xh, xl = _split_bf16(xn)

    def proj(lo, hi):
        return jnp.dot(xh, w_ref[:, lo:hi], preferred_element_type=f32)

    aq_ref[...] = proj(0, 512)
    ak_ref[...] = proj(512, 1024)
    av_ref[...] = proj(1024, 1536)
    mqk_ref[...] = proj(1536, 2560)
    mv_ref[...] = proj(2560, 3072)
    mo_ref[...] = proj(3072, 3584)
    gate_ref[...] = (jnp.dot(xh, wgh_ref[...], preferred_element_type=f32)
                     + jnp.dot(xl, wgh_ref[...], preferred_element_type=f32)
                     + jnp.dot(xh, wgl_ref[...], preferred_element_type=f32))


def _inproj(xp, xs, g1, w_in):
    n = xp.shape[0] + xs.shape[0]
    nbp = xp.shape[0] // ROW_BLOCK
    main = 3 * A_WIDTH + 4 * M_WIDTH
    w_main = w_in[:, :main].astype(bf16)
    wg = jnp.pad(w_in[:, main:], ((0, 0), (0, LANES - 2 * M_HEADS)))
    wgh, wgl = _split_bf16(wg)
    widths = (512, 512, 512, 1024, 512, 512, LANES)
    row = lambda w: pl.BlockSpec((ROW_BLOCK, w), lambda i: (i, 0))
    full = lambda a: pl.BlockSpec(a.shape, lambda i: (0,) * a.ndim)
    g = g1.reshape(1, D_MODEL)
    return pl.pallas_call(
        functools.partial(_inproj_kernel, nbp=nbp),
        grid=(n // ROW_BLOCK,),
        in_specs=[*_pair_specs(nbp, D_MODEL), full(g), full(w_main), full(wgh), full(wgl)],
        out_specs=[row(w) for w in widths],
        out_shape=[jax.ShapeDtypeStruct((n, w), f32) for w in widths],
        compiler_params=pltpu.CompilerParams(dimension_semantics=("parallel",), vmem_limit_bytes=VMEM_LIMIT),
        name="inproj",
    )(xp, xs, g, w_main, wgh, wgl)


def _attn_heads(q, k, v, bias_ref, key_ok):
    outs = []
    for h in range(A_HEADS):
        sl = slice(h * A_DH, (h + 1) * A_DH)
        s = lax.dot_general(q[:, sl], k[:, sl], (((1,), (1,)), ((), ())), preferred_element_type=f32)
        s = s * ATT_SCALE + bias_ref[h]
        if key_ok is not None:
            s = jnp.where(key_ok, s, NEG)
        m = jnp.max(s, axis=-1, keepdims=True)
        p = jnp.exp(s - m)
        l = jnp.sum(p, axis=-1, keepdims=True)
        o = jnp.dot(p.astype(bf16), v[:, sl], preferred_element_type=f32)
        outs.append(o / l)
    return jnp.concatenate(outs, axis=-1)


def _attn_prompt_kernel(q_ref, k0_ref, k1_ref, v0_ref, v1_ref, bias_ref, o_ref, *, t0):
    t = t0 + pl.program_id(1)
    q = q_ref[...].astype(bf16)
    k = jnp.concatenate([k0_ref[...], k1_ref[...]], axis=0).astype(bf16)
    v = jnp.concatenate([v0_ref[...], v1_ref[...]], axis=0).astype(bf16)
    col = lax.broadcasted_iota(i32, (1, ATT_KEYS), 1)
    for s in range(ATT_TILE // ATT_SUB):
        lo = s * ATT_SUB
        key_ok = (t * ATT_TILE + lo + col) >= WINDOW
        o_ref[lo:lo + ATT_SUB, :] = _attn_heads(q[lo:lo + ATT_SUB], k[lo:lo + ATT_KEYS], v[lo:lo + ATT_KEYS],
                                                 bias_ref, key_ok)


def _attn_sample_kernel(q_ref, k_ref, v_ref, bias_ref, o_ref):
    o_ref[...] = _attn_heads(q_ref[...].astype(bf16), k_ref[0].astype(bf16), v_ref[0].astype(bf16), bias_ref, None)


def _rel_bias_table(rel_bias, rows, cols, offset, valid):
    span = rows + cols - 1
    rel = offset + rows - 1 - jnp.arange(span)
    diag = rel_bias[:, jnp.clip(rel, -MAX_REL, MAX_REL) + MAX_REL].astype(f32)
    diag = jnp.pad(diag, ((0, 0), (0, 1)))
    flat = jnp.tile(diag, (1, rows))[:, rows - 1:rows - 1 + rows * span]
    return jnp.where(valid[None], flat.reshape(-1, rows, span)[:, :, :cols], NEG)


def _attn_prompt(q, k, v, rel_bias, seq, s, lo, hi):
    assert s % ATT_TILE == 0 and WINDOW == ATT_TILE and lo % ATT_TILE == 0 and hi % ATT_TILE == 0
    nt, t0, cnt = s // ATT_TILE, lo // ATT_TILE, (hi - lo) // ATT_TILE
    i = jnp.arange(ATT_SUB)[:, None]
    j = jnp.arange(ATT_KEYS)[None, :]
    off = j - (i // CHUNK) * CHUNK
    bias = _rel_bias_table(rel_bias, ATT_SUB, ATT_KEYS, WINDOW, (off >= 0) & (off < WINDOW + CHUNK))
    cur = pl.BlockSpec((ATT_TILE, A_WIDTH), lambda b, t: (seq * nt + t0 + t, 0))
    prev = pl.BlockSpec((ATT_TILE, A_WIDTH), lambda b, t: (seq * nt + jnp.maximum(t0 + t - 1, 0), 0))
    return pl.pallas_call(
        functools.partial(_attn_prompt_kernel, t0=t0),
        grid=(1, cnt),
        in_specs=[cur, prev, cur, prev, cur, pl.BlockSpec(bias.shape, lambda b, t: (0, 0, 0))],
        out_specs=pl.BlockSpec((ATT_TILE, A_WIDTH), lambda b, t: (t, 0)),
        out_shape=jax.ShapeDtypeStruct((hi - lo, A_WIDTH), f32),
        compiler_params=pltpu.CompilerParams(dimension_semantics=("parallel", "parallel"),
                                             vmem_limit_bytes=VMEM_LIMIT),
        name="attn_prompt",
    )(q, k, k, v, v, bias)


def _attn_sample(q, k, v, ck, cv, rel_bias, row0, bsz, t):
    l = ck.shape[1]
    assert row0 % t == 0
    keys = -(-(l + t) // LANES) * LANES
    padk = ((0, 0), (0, keys - l - t), (0, 0))
    kk = jnp.pad(jnp.concatenate([ck, k[row0:].reshape(bsz, t, A_WIDTH)], axis=1), padk)
    vv = jnp.pad(jnp.concatenate([cv, v[row0:].reshape(bsz, t, A_WIDTH)], axis=1), padk)
    j = jnp.arange(keys)[None, :]
    bias = _rel_bias_table(rel_bias, t, keys, l, jnp.broadcast_to(j < l + t, (t, keys)))
    return pl.pallas_call(
        _attn_sample_kernel,
        grid=(bsz,),
        in_specs=[pl.BlockSpec((t, A_WIDTH), lambda b: (row0 // t + b, 0)),
                  pl.BlockSpec((1, keys, A_WIDTH), lambda b: (b, 0, 0)),
                  pl.BlockSpec((1, keys, A_WIDTH), lambda b: (b, 0, 0)),
                  pl.BlockSpec(bias.shape, lambda b: (0, 0, 0))],
        out_specs=pl.BlockSpec((t, A_WIDTH), lambda b: (b, 0)),
        out_shape=jax.ShapeDtypeStruct((bsz * t, A_WIDTH), f32),
        compiler_params=pltpu.CompilerParams(dimension_semantics=("parallel",), vmem_limit_bytes=VMEM_LIMIT),
        name="attn_sample",
    )(q, kk, vv, bias)


def _mlstm_chunk(a, vall, o_in, gate_in, cw_ref, cb_ref, bg_ref, mhg_ref, c_s, n_s, m_s, prev_s):
    lc = a.shape[0]
    ext = jnp.concatenate([prev_s[...], a], axis=0)
    conv = cb_ref[...]
    for j in range(CONV_W):
        lo = SUBLANES - (CONV_W - 1) + j
        conv = conv + cw_ref[j:j + 1, :] * ext[lo:lo + lc]
    prev_s[...] = a[lc - SUBLANES:lc]
    qk = conv * jax.nn.sigmoid(conv)

    z = gate_in + bg_ref[...]
    lane = lax.broadcasted_iota(i32, (lc, LANES), 1)
    row = lax.broadcasted_iota(i32, (lc, LANES), 0)
    logf = jnp.minimum(z, 0.0) - jnp.log1p(jnp.exp(-jnp.abs(z)))
    cum = jnp.where((lane >= M_HEADS) & (lane < 2 * M_HEADS), logf, 0.0)
    shift = 1
    while shift < lc:
        cum = cum + jnp.where(row >= shift, pltpu.roll(cum, shift, axis=0), 0.0)
        shift *= 2
    zc = jnp.where(lane < M_HEADS, z, cum)
    zt = jnp.concatenate([zc, jnp.zeros((LANES - lc, LANES), f32)], axis=0).T[:, :lc]

    ri = lax.broadcasted_iota(i32, (lc, lc), 0)
    ci = lax.broadcasted_iota(i32, (lc, lc), 1)
    causal = ri >= ci
    state = [(m_s[h:h + 1, 0:1], c_s[h], n_s[h:h + 1, :]) for h in range(M_HEADS)]
    hs, new_state = [], []
    for h in range(M_HEADS):
        sl = slice(h * M_DH, (h + 1) * M_DH)
        q = qk[:, sl]
        k = qk[:, M_WIDTH + h * M_DH:M_WIDTH + (h + 1) * M_DH] * (M_DH ** -0.5)
        v = vall[:, sl]
        i_col = zc[:, h:h + 1]
        b_col = zc[:, M_HEADS + h:M_HEADS + h + 1]
        i_row = zt[h:h + 1, :]
        b_row = zt[M_HEADS + h:M_HEADS + h + 1, :]
        m_prev, c_prev, n_prev = state[h]

        dmat = jnp.where(causal, b_col - b_row + i_row, NEG)
        inter = b_col + m_prev
        mt = jnp.maximum(jnp.max(dmat, axis=-1, keepdims=True), inter)
        w_intra = jnp.exp(dmat - mt)
        w_inter = jnp.exp(inter - mt)
        qb, kb, vb = q.astype(bf16), k.astype(bf16), v.astype(bf16)
        s = lax.dot_general(qb, kb, (((1,), (1,)), ((), ())), preferred_element_type=f32) * w_intra
        num = (w_inter * jnp.dot(qb, c_prev.astype(bf16), preferred_element_type=f32)
               + jnp.dot(s.astype(bf16), vb, preferred_element_type=f32))
        den = w_inter * jnp.sum(q * n_prev, axis=-1, keepdims=True) + jnp.sum(s, axis=-1, keepdims=True)
        hh = num / jnp.maximum(jnp.abs(den), jnp.exp(-mt))
        m_new = mt[lc - 1:lc, :]
        b_last = b_col[lc - 1:lc, :]
        w_s = jnp.exp(b_last - b_col + i_col - m_new)
        decay = jnp.exp(b_last + m_prev - m_new)
        kw = k * w_s
        new_state.append((jnp.broadcast_to(m_new, (1, LANES)),
                          decay * c_prev + lax.dot_general(kw.astype(bf16), vb, (((0,), (0,)), ((), ())),
                                                           preferred_element_type=f32),
                          decay * n_prev + jnp.sum(kw, axis=0, keepdims=True)))
        hs.append(hh * lax.rsqrt(jnp.mean(hh * hh, axis=-1, keepdims=True) + EPS))

    for h, (m_new, c_new, n_new) in enumerate(new_state):
        m_s[h:h + 1, :] = m_new
        c_s[h] = c_new
        n_s[h:h + 1, :] = n_new
    return jnp.concatenate(hs, axis=-1) * mhg_ref[...] * jax.nn.sigmoid(o_in)


def _mlstm_kernel(qk_ref, v_ref, o_ref, gate_ref, c0_ref, n0_ref, m0_ref, cbuf_ref,
                  cw_ref, cb_ref, bg_ref, mhg_ref,
                  h_ref, cout_ref, nout_ref, mout_ref,
                  c_s, n_s, m_s, prev_s):
    c = pl.program_id(1)

    @pl.when(c == 0)
    def _():
        c_s[...] = c0_ref[0]
        n_s[...] = n0_ref[0]
        m_s[...] = m0_ref[0]
        prev_s[...] = cbuf_ref[0]

    h_ref[...] = _mlstm_chunk(qk_ref[...], v_ref[...], o_ref[...], gate_ref[...],
                              cw_ref, cb_ref, bg_ref, mhg_ref, c_s, n_s, m_s, prev_s)

    @pl.when(c == pl.num_programs(1) - 1)
    def _():
        cout_ref[0] = c_s[...]
        nout_ref[0] = n_s[...]
        mout_ref[0] = m_s[...]


def _mlstm(mqk, mv, mo, gates, row0, bsz, t, c0, n0, m0, cbuf, conv_w, conv_b, b_gates, mh_g):
    lc = min(CHUNK, t)
    step = lc
    nc = t // step
    assert t % step == 0 and lc % SUBLANES == 0 and row0 % step == 0
    n0p = jnp.pad(n0.astype(f32), ((0, 0), (0, SUBLANES - M_HEADS), (0, 0)))
    m0p = jnp.pad(jnp.broadcast_to(m0.astype(f32)[:, :, None], (bsz, M_HEADS, LANES)),
                  ((0, 0), (0, SUBLANES - M_HEADS), (0, 0)))
    cbp = jnp.pad(cbuf.astype(f32), ((0, 0), (SUBLANES - (CONV_W - 1), 0), (0, 0)))
    bg = jnp.pad(b_gates.astype(f32), (0, LANES - 2 * M_HEADS)).reshape(1, LANES)
    seq = lambda w: pl.BlockSpec((step, w), lambda b, c: (row0 // step + b * nc + c, 0))
    out_seq = pl.BlockSpec((step, M_WIDTH), lambda b, c: (b * nc + c, 0))
    per_b = lambda shp: pl.BlockSpec((1,) + shp, lambda b, c: (b,) + (0,) * len(shp))
    full = lambda a: pl.BlockSpec(a.shape, lambda b, c: (0,) * a.ndim)
    cb = conv_b.reshape(1, -1)
    mhg = mh_g.reshape(1, -1)
    return pl.pallas_call(
        _mlstm_kernel,
        grid=(bsz, nc),
        in_specs=[seq(2 * M_WIDTH), seq(M_WIDTH), seq(M_WIDTH), seq(LANES),
                  per_b((M_HEADS, M_DH, M_DH)), per_b((SUBLANES, M_DH)), per_b((SUBLANES, LANES)),
                  per_b((SUBLANES, 2 * M_WIDTH)),
                  full(conv_w), full(cb), full(bg), full(mhg)],
        out_specs=[out_seq, per_b((M_HEADS, M_DH, M_DH)), per_b((SUBLANES, M_DH)), per_b((SUBLANES, LANES))],
        out_shape=[jax.ShapeDtypeStruct((bsz * t, M_WIDTH), f32),
                   jax.ShapeDtypeStruct((bsz, M_HEADS, M_DH, M_DH), f32),
                   jax.ShapeDtypeStruct((bsz, SUBLANES, M_DH), f32),
                   jax.ShapeDtypeStruct((bsz, SUBLANES, LANES), f32)],
        scratch_shapes=[pltpu.VMEM((M_HEADS, M_DH, M_DH), f32), pltpu.VMEM((SUBLANES, M_DH), f32),
                        pltpu.VMEM((SUBLANES, LANES), f32), pltpu.VMEM((SUBLANES, 2 * M_WIDTH), f32)],
        compiler_params=pltpu.CompilerParams(dimension_semantics=("parallel", "arbitrary"),
                                             vmem_limit_bytes=VMEM_LIMIT),
        name="mlstm",
    )(mqk, mv, mo, gates, c0.astype(f32), n0p, m0p, cbp, conv_w, cb, bg, mhg)


def _topk_rows(s, k, payload=None):
    n = s.shape[0]
    rows = lax.broadcasted_iota(i32, s.shape, 0).astype(f32)
    vals, ids = [], []
    for _ in range(k):
        m = jnp.max(s, axis=0, keepdims=True)
        pos = jnp.min(jnp.where(s == m, rows, float(n)), axis=0, keepdims=True)
        sel = rows == pos
        vals.append(m)
        ids.append(pos if payload is None else jnp.max(jnp.where(sel, payload, -1.0), axis=0, keepdims=True))
        s = jnp.where(sel, -jnp.inf, s)
    return jnp.concatenate(vals, axis=0), jnp.concatenate(ids, axis=0)


def _mid_kernel(xp_ref, xs_ref, attp_ref, atts_ref, hp_ref, hs_ref, wo_ref, g2_ref, wq_ref, keys_ref, after_ref,
                x1_ref, xn_ref, eidx_ref, eidx_t_ref, gate_ref, *, nbp):
    del after_ref
    cat = jnp.concatenate([_pick(nbp, attp_ref, atts_ref), _pick(nbp, hp_ref, hs_ref)], axis=-1).astype(bf16)
    x1 = _pick(nbp, xp_ref, xs_ref) + jnp.dot(cat, wo_ref[...], preferred_element_type=f32)
    x1_ref[...] = x1
    xn = _rms(x1, g2_ref[...])
    xn_ref[...] = xn
    xb = xn.astype(bf16)
    e_rows, g_rows = [], []
    for h in range(P_HEADS):
        q = jnp.dot(xb, wq_ref[:, h * P_DKEY:(h + 1) * P_DKEY], preferred_element_type=f32).astype(bf16)
        half = []
        for p in range(2):
            st = lax.dot_general(keys_ref[h, p], q[:, p * N_KEYS:(p + 1) * N_KEYS], (((1,), (1,)), ((), ())),
                                 preferred_element_type=f32)
            half.append(_topk_rows(st, P_TOPK))
        (v0, i0), (v1, i1) = half
        width = [P_TOPK // (a + 1) for a in range(P_TOPK)]
        fill = -sum(width) % SUBLANES
        cand = jnp.concatenate([v0[a:a + 1] + v1[:width[a]] for a in range(P_TOPK)]
                               + [jnp.full((fill, v0.shape[1]), -jnp.inf, f32)], axis=0)
        cidx = jnp.concatenate([i0[a:a + 1] * float(N_KEYS) + i1[:width[a]] for a in range(P_TOPK)]
                               + [jnp.zeros((fill, v0.shape[1]), f32)], axis=0)
        top_s, eid = _topk_rows(cand, P_TOPK, payload=cidx)
        ex = jnp.exp(top_s - top_s[0:1])
        e_rows.append(eid)
        g_rows.append(ex / jnp.sum(ex, axis=0, keepdims=True))
    eidx_t = jnp.concatenate(e_rows, axis=0)
    eidx_t_ref[...] = eidx_t.astype(i32)
    eidx_ref[...] = eidx_t.T.astype(i32)
    gate_ref[...] = jnp.concatenate(g_rows, axis=0).T


def _mid(xp, xp_row0, xs, xs_row0, attp, atts, hp, hs, n_a, n_b, after, w_out, g2, wq, keys):
    n = n_a + n_b
    nbp = n_a // ROW_BLOCK
    assert n_a % ROW_BLOCK == 0 and n_b % ROW_BLOCK == 0 and xp_row0 % ROW_BLOCK == 0 and xs_row0 % ROW_BLOCK == 0
    row = lambda w: pl.BlockSpec((ROW_BLOCK, w), lambda i: (i, 0))
    full = lambda a: pl.BlockSpec(a.shape, lambda i: (0,) * a.ndim)
    wo = w_out.astype(bf16)
    wqb = wq.astype(bf16)
    kb = keys.astype(bf16)
    g = g2.reshape(1, D_MODEL)
    return pl.pallas_call(
        functools.partial(_mid_kernel, nbp=nbp),
        grid=(n // ROW_BLOCK,),
        in_specs=[*_pair_specs(nbp, D_MODEL, xp_row0 // ROW_BLOCK, xs_row0 // ROW_BLOCK), *_pair_specs(nbp, A_WIDTH),
                  *_pair_specs(nbp, M_WIDTH), full(wo), full(g), full(wqb), full(kb),
                  pl.BlockSpec((SUBLANES, P_SLOTS), lambda i: (0, 0))],
        out_specs=[row(D_MODEL), row(D_MODEL), row(P_SLOTS), pl.BlockSpec((P_SLOTS, ROW_BLOCK), lambda i: (0, i)),
                   row(P_SLOTS)],
        out_shape=[jax.ShapeDtypeStruct((n, D_MODEL), f32), jax.ShapeDtypeStruct((n, D_MODEL), f32),
                   jax.ShapeDtypeStruct((n, P_SLOTS), i32), jax.ShapeDtypeStruct((P_SLOTS, n), i32),
                   jax.ShapeDtypeStruct((n, P_SLOTS), f32)],
        compiler_params=pltpu.CompilerParams(dimension_semantics=("parallel",), vmem_limit_bytes=VMEM_LIMIT),
        name="outproj_retrieve",
    )(xp, xs, attp, atts, hp, hs, wo, g, wqb, kb, after)


def _gather_rows(eidx_refs, tab_ref, g_ref, t):
    for i in range(SUBLANES):
        for k, eidx_ref in enumerate(eidx_refs):
            r = k * SUBLANES + i
            g_ref[pl.ds(r * ROW_TILES, ROW_TILES), :] = tab_ref[eidx_ref[i, t]]


def _tile_rows(t, rows=SUBLANES):
    return pl.ds(pl.multiple_of(t * rows, rows), rows)


def _pipelined_tokens(nt, eidx_refs, tab_ref, g0_s, g1_s, compute):
    bufs = (g0_s, g1_s)
    _gather_rows(eidx_refs, tab_ref, g0_s, 0)

    def body(j, carry):
        for u in range(TOKEN_UNROLL):
            t = TOKEN_UNROLL * j + u
            _gather_rows(eidx_refs, tab_ref, bufs[(u + 1) % 2], jnp.minimum(t + 1, nt - 1))
            compute(t, bufs[u % 2])
        return carry

    lax.fori_loop(0, nt // TOKEN_UNROLL, body, 0)


def _gelu_tanh(x):
    return 0.5 * x * (1.0 + jnp.tanh(math.sqrt(2.0 / math.pi) * (x + 0.044715 * (x * x * x))))


def _peer_u_kernel(*refs):
    eidx_refs, (xn_ref, g_ref, tab_ref, w_ref, xl_s, r_s, g0_s, g1_s) = refs[:INDEX_GROUPS], refs[INDEX_GROUPS:]
    nt = xn_ref.shape[0]
    xn = xn_ref[...]
    xh = xn.astype(bf16).astype(f32)
    xl = xn - xh
    for k in range(ROW_TILES):
        xl_s[pl.ds(k, nt, stride=2 * SUBLANES), :] = xh[:, k * LANES:(k + 1) * LANES]
        xl_s[pl.ds(SUBLANES + k, nt, stride=2 * SUBLANES), :] = xl[:, k * LANES:(k + 1) * LANES]
    diag = (lax.broadcasted_iota(i32, (SUBLANES, G_ROWS), 1) % ROW_TILES
            == lax.broadcasted_iota(i32, (SUBLANES, G_ROWS), 0))

    def compute(t, g_s):
        lhs = xl_s[_tile_rows(t, 2 * SUBLANES), :].astype(bf16)
        out = lax.dot_general(lhs, g_s[...], (((1,), (1,)), ((), ())), preferred_element_type=f32)
        part = jnp.where(diag, out[:SUBLANES] + out[SUBLANES:], 0.0)
        for c in range(ROW_TILES):
            r_s[c, _tile_rows(t), :] = part[:, c * LANES:(c + 1) * LANES]

    _pipelined_tokens(nt, eidx_refs, tab_ref, g0_s, g1_s, compute)
    cols = []
    for c in range(ROW_TILES):
        acc = r_s[c, pl.ds(0, nt, stride=SUBLANES), :]
        for k in range(1, SUBLANES):
            acc = acc + r_s[c, pl.ds(k, nt, stride=SUBLANES), :]
        cols.append(acc)
    s = jnp.concatenate(cols, axis=-1)
    fold = (lax.broadcasted_iota(i32, (G_ROWS, P_SLOTS), 0) // ROW_TILES
            == lax.broadcasted_iota(i32, (G_ROWS, P_SLOTS), 1)).astype(bf16)
    sh, sl = _split_bf16(s)
    act = jnp.dot(sh, fold, preferred_element_type=f32) + jnp.dot(sl, fold, preferred_element_type=f32)
    w_ref[...] = g_ref[...] * _gelu_tanh(act)


def _peer_v_kernel(*refs):
    eidx_refs, (w_ref, tab_ref, peer_ref, wl_s, o_s, g0_s, g1_s) = refs[:INDEX_GROUPS], refs[INDEX_GROUPS:]
    nt = w_ref.shape[0]
    spread = (lax.broadcasted_iota(i32, (P_SLOTS, G_ROWS), 1) // ROW_TILES
              == lax.broadcasted_iota(i32, (P_SLOTS, G_ROWS), 0)).astype(bf16)
    wexp = jnp.dot(w_ref[...].astype(bf16), spread, preferred_element_type=f32)
    lane = lax.broadcasted_iota(i32, (nt, LANES), 1)
    for c in range(ROW_TILES):
        wc = wexp[:, c * LANES:(c + 1) * LANES]
        for k in range(SUBLANES):
            wl_s[c, pl.ds(k, nt, stride=SUBLANES), :] = jnp.where(lane % ROW_TILES == k, wc, 0.0)

    def compute(t, g_s):
        lhs = jnp.concatenate([wl_s[c, _tile_rows(t), :] for c in range(ROW_TILES)], axis=-1).astype(bf16)
        o_s[_tile_rows(t), :] = jnp.dot(lhs, g_s[...], preferred_element_type=f32)

    _pipelined_tokens(nt, eidx_refs, tab_ref, g0_s, g1_s, compute)
    for k in range(ROW_TILES):
        peer_ref[:, k * LANES:(k + 1) * LANES] = o_s[pl.ds(k, nt, stride=SUBLANES), :]


def _final_kernel(*refs, starts, nbp):
    k = len(starts)
    peer_refs, x_refs, (gf_ref, yp_ref, ys_ref) = refs[:k], refs[k:2 * k], refs[2 * k:]
    resid = x_refs[0][...] + peer_refs[0][...]
    for start, p_ref, x_ref in zip(starts[1:], peer_refs[1:], x_refs[1:]):
        resid = jnp.where(pl.program_id(0) >= start, x_ref[...] + p_ref[...], resid)
    y = _rms(resid, gf_ref[...])

    @pl.when(pl.program_id(0) < nbp)
    def _():
        yp_ref[...] = y

    @pl.when(pl.program_id(0) >= nbp)
    def _():
        ys_ref[...] = y


def _final(peers, x1s, gf, n_p):
    counts = [x.shape[0] // ROW_BLOCK for x in x1s]
    starts = [sum(counts[:k]) for k in range(len(counts))]
    n, nbp = sum(counts) * ROW_BLOCK, n_p // ROW_BLOCK
    g = gf.reshape(1, D_MODEL)
    seg = [pl.BlockSpec((ROW_BLOCK, D_MODEL), lambda i, s=s, c=c: (jnp.clip(i - s, 0, c - 1), 0))
           for s, c in zip(starts, counts)]
    return pl.pallas_call(
        functools.partial(_final_kernel, starts=tuple(starts), nbp=nbp),
        grid=(n // ROW_BLOCK,),
        in_specs=[*seg, *seg, pl.BlockSpec(g.shape, lambda i: (0, 0))],
        out_specs=list(_pair_specs(nbp, D_MODEL)),
        out_shape=[jax.ShapeDtypeStruct((n_p, D_MODEL), f32), jax.ShapeDtypeStruct((n - n_p, D_MODEL), f32)],
        compiler_params=pltpu.CompilerParams(dimension_semantics=("arbitrary",), vmem_limit_bytes=VMEM_LIMIT),
        name="final_norm",
    )(*peers, *x1s, g)


def _peer_specs():
    row = lambda w: pl.BlockSpec((ROW_BLOCK, w), lambda i: (i, 0))
    idx = [pl.BlockSpec((SUBLANES, ROW_BLOCK), lambda i, k=k: (k, i), memory_space=pltpu.SMEM)
           for k in range(INDEX_GROUPS)]
    tab = pl.BlockSpec(memory_space=pltpu.VMEM)
    gscr = pltpu.VMEM((G_ROWS, LANES), bf16)
    params = pltpu.CompilerParams(dimension_semantics=("arbitrary",), vmem_limit_bytes=VMEM_LIMIT)
    return row, idx, tab, gscr, params


def _expert_table(tab):
    return tab.astype(bf16).reshape(tab.shape[0], ROW_TILES, LANES)


def _peer_u(eidx_t, xn, g, utab, n):
    row, idx, tab, gscr, params = _peer_specs()
    return pl.pallas_call(
        _peer_u_kernel,
        grid=(n // ROW_BLOCK,),
        in_specs=[*idx, row(D_MODEL), row(P_SLOTS), tab],
        out_specs=row(P_SLOTS),
        out_shape=jax.ShapeDtypeStruct((n, P_SLOTS), f32),
        scratch_shapes=[pltpu.VMEM((ROW_BLOCK * 2 * SUBLANES, LANES), f32),
                        pltpu.VMEM((ROW_TILES, ROW_BLOCK * SUBLANES, LANES), f32), gscr, gscr],
        compiler_params=params,
        name="peer_u",
    )(*[eidx_t] * INDEX_GROUPS, xn, g, utab)


def _peer_v(eidx_t, w, vtab):
    n = w.shape[0]
    row, idx, tab, gscr, params = _peer_specs()
    return pl.pallas_call(
        _peer_v_kernel,
        grid=(n // ROW_BLOCK,),
        in_specs=[*idx, row(P_SLOTS), tab],
        out_specs=row(D_MODEL),
        out_shape=jax.ShapeDtypeStruct((n, D_MODEL), f32),
        scratch_shapes=[pltpu.VMEM((ROW_TILES, ROW_BLOCK * SUBLANES, LANES), f32),
                        pltpu.VMEM((ROW_BLOCK * SUBLANES, LANES), f32), gscr, gscr],
        compiler_params=params,
        name="peer_v",
    )(*[eidx_t] * INDEX_GROUPS, w, vtab)


def _sc_table_kernel(t_ref, o_ref):
    t = t_ref[...]
    o_ref[...] = pltpu.pack_elementwise([t[:, :SC_WORDS], t[:, SC_WORDS:]], packed_dtype=bf16)


def _sc_table(tab):
    e = tab.shape[0]
    return pl.pallas_call(
        _sc_table_kernel,
        grid=(e // ROW_BLOCK,),
        in_specs=[pl.BlockSpec((ROW_BLOCK, D_MODEL), lambda i: (i, 0))],
        out_specs=pl.BlockSpec((ROW_BLOCK, SC_WORDS), lambda i: (i, 0)),
        out_shape=jax.ShapeDtypeStruct((e, SC_WORDS), jnp.uint32),
        compiler_params=pltpu.CompilerParams(dimension_semantics=("parallel",), vmem_limit_bytes=VMEM_LIMIT),
        name="sc_table",
    )(tab)


def _sc_unpack(words):
    return plsc.unpack(plsc.bitcast(words, bf16), format=plsc.PackFormat.INTERLEAVED)


def _sc_gelu_tanh(x):
    z = math.sqrt(2.0 / math.pi) * (x + 0.044715 * (x * x * x))
    return 0.5 * x * (2.0 - 2.0 / (jnp.exp(2.0 * z) + 1.0))


def _sc_peer(utab32, vtab32, eidx, xn, gate):
    n_sc = eidx.shape[0]
    per = n_sc // SC_WORKERS
    units = 4 * per
    assert n_sc % (SC_WORKERS * SUBLANES) == 0

    def body(u_hbm, v_hbm, eidx_hbm, x_hbm, g_hbm, out_hbm,
             idx_v, x_v, g_v, rows_v, acc_v, w_v, w16_v, out_v, row_sems, tok_sems, out_sem):
        base = (lax.axis_index("s") * 2 + lax.axis_index("c")) * per
        lanes = lax.broadcasted_iota(i32, (SC_LANES,), 0)

        def token_copies(tok):
            tslot = tok % 2
            return (pltpu.make_async_copy(eidx_hbm.at[base + tok], idx_v.at[tslot], tok_sems.at[0]),
                    pltpu.make_async_copy(x_hbm.at[base + tok], x_v.at[tslot], tok_sems.at[1]),
                    pltpu.make_async_copy(g_hbm.at[base + tok], g_v.at[tslot], tok_sems.at[2]))

        def store_out(tok):
            return pltpu.make_async_copy(out_v, out_hbm.at[base + tok], out_sem)

        def gather(tab_hbm, g):
            tok, k = g // 4, g % 4
            return pltpu.make_async_copy(tab_hbm.at[idx_v.at[tok % 2, pl.ds((k % 2) * SC_UNIT_ROWS, SC_UNIT_ROWS)]],
                                         rows_v.at[k % 2], row_sems.at[k % 2])

        def start(g):
            @pl.when(g % 4 < 2)
            def _():
                gather(u_hbm, g).start()

            @pl.when(g % 4 >= 2)
            def _():
                gather(v_hbm, g).start()

        def compute_u(tslot, half, rows):
            @pl.loop(0, SC_UNIT_ROWS // SC_ROW_GROUP)
            def _(rg):
                slot0 = half * SC_UNIT_ROWS + rg * SC_ROW_GROUP

                @pl.loop(0, SC_CHUNKS // SC_CHUNK_GROUP)
                def _(cg):
                    keep = jnp.where(cg > 0, 1.0, 0.0).astype(f32)
                    accs = [acc_v[pl.ds((slot0 + r) * SC_LANES, SC_LANES)] * keep for r in range(SC_ROW_GROUP)]
                    for c in range(SC_CHUNK_GROUP):
                        ch = cg * SC_CHUNK_GROUP + c
                        xa = x_v[tslot, pl.ds(ch * SC_LANES, SC_LANES)]
                        xb = x_v[tslot, pl.ds(SC_WORDS + ch * SC_LANES, SC_LANES)]
                        for r in range(SC_ROW_GROUP):
                            a, b = _sc_unpack(rows[rg * SC_ROW_GROUP + r, pl.ds(ch * SC_LANES, SC_LANES)])
                            accs[r] = accs[r] + a * xa + b * xb
                    for r in range(SC_ROW_GROUP):
                        acc_v[pl.ds((slot0 + r) * SC_LANES, SC_LANES)] = accs[r]

        def gate_weights(tslot):
            @pl.loop(0, P_SLOTS // SC_LANES)
            def _(sg):
                first = sg * SC_LANES * SC_LANES
                act = jnp.zeros((SC_LANES,), f32)
                for lane in range(SC_LANES):
                    act = act + plsc.load_gather(acc_v, [first + lanes * SC_LANES + lane])
                w_v[pl.ds(sg * SC_LANES, SC_LANES)] = g_v[tslot, pl.ds(sg * SC_LANES, SC_LANES)] * _sc_gelu_tanh(act)

            @pl.loop(0, P_SLOTS // SC_LANES)
            def _(sg):
                for r in range(SC_LANES):
                    w16_v[pl.ds((sg * SC_LANES + r) * SC_LANES, SC_LANES)] = plsc.load_gather(
                        w_v, [jnp.zeros((SC_LANES,), i32) + (sg * SC_LANES + r)])

        def compute_v(half, rows):
            @pl.loop(0, SC_CHUNKS // SC_CHUNK_GROUP)
            def _(cg):
                first = cg * SC_CHUNK_GROUP

                @pl.loop(0, SC_UNIT_ROWS // SC_ROW_GROUP)
                def _(rg):
                    keep = jnp.where(jnp.logical_or(half == 1, rg > 0), 1.0, 0.0).astype(f32)
                    spots = [pl.ds((j % 2) * SC_WORDS + (first + j // 2) * SC_LANES, SC_LANES)
                             for j in range(2 * SC_CHUNK_GROUP)]
                    accs = [out_v[spot] * keep for spot in spots]
                    for r in range(SC_ROW_GROUP):
                        row = rg * SC_ROW_GROUP + r
                        wv = w16_v[pl.ds((half * SC_UNIT_ROWS + row) * SC_LANES, SC_LANES)]
                        for c in range(SC_CHUNK_GROUP):
                            a, b = _sc_unpack(rows[row, pl.ds((first + c) * SC_LANES, SC_LANES)])
                            accs[2 * c] = accs[2 * c] + a * wv
                            accs[2 * c + 1] = accs[2 * c + 1] + b * wv
                    for spot, acc in zip(spots, accs):
                        out_v[spot] = acc

        for cp in token_copies(0):
            cp.start()
        for cp in token_copies(0):
            cp.wait()
        start(0)

        @pl.loop(0, units)
        def _(g):
            tok, k = g // 4, g % 4

            @pl.when(jnp.logical_and(k == 0, tok + 1 < per))
            def _():
                for cp in token_copies(tok + 1):
                    cp.start()

            @pl.when(jnp.logical_and(k == 3, tok + 1 < per))
            def _():
                for cp in token_copies(tok + 1):
                    cp.wait()

            @pl.when(g + 1 < units)
            def _():
                start(g + 1)

            gather(u_hbm, g).wait()
            rows = rows_v.at[k % 2]

            @pl.when(k < 2)
            def _():
                compute_u(tok % 2, k, rows)

            @pl.when(k == 1)
            def _():
                gate_weights(tok % 2)

            @pl.when(jnp.logical_and(k == 2, tok > 0))
            def _():
                store_out(tok - 1).wait()

            @pl.when(k >= 2)
            def _():
                compute_v(k - 2, rows)

            @pl.when(k == 3)
            def _():
                store_out(tok).start()

        store_out(per - 1).wait()

    return pl.kernel(
        body, mesh=plsc.VectorSubcoreMesh(core_axis_name="c", subcore_axis_name="s"),
        out_type=jax.ShapeDtypeStruct((n_sc, D_MODEL), f32),
        scratch_types=[pltpu.VMEM((2, P_SLOTS), i32), pltpu.VMEM((2, D_MODEL), f32), pltpu.VMEM((2, P_SLOTS), f32),
                       pltpu.VMEM((2, SC_UNIT_ROWS, SC_WORDS), jnp.uint32), pltpu.VMEM((P_SLOTS * SC_LANES,), f32),
                       pltpu.VMEM((P_SLOTS,), f32), pltpu.VMEM((P_SLOTS * SC_LANES,), f32), pltpu.VMEM((D_MODEL,), f32),
                       pltpu.SemaphoreType.DMA((2,)), pltpu.SemaphoreType.DMA((3,)), pltpu.SemaphoreType.DMA],
        compiler_params=pltpu.CompilerParams(needs_layout_passes=False),
        name="sc_peer",
    )(utab32, vtab32, eidx, xn, gate)


def kernel(x_prompt, x_sample, cache_k, cache_v, state_C, state_n, state_m, state_conv, norm1_g, w_in, b_gates, rel_bias, conv_w, conv_b, mh_norm_g, w_out, norm2_g, peer_wq, peer_keys, peer_u, peer_v, final_g):
    bp, sp, d = x_prompt.shape
    bs, ts, _ = x_sample.shape
    n_p, n_s = bp * sp, bs * ts
    n = n_p + n_s
    assert n_p % ROW_BLOCK == 0 and n_s % ROW_BLOCK == 0 and d == D_MODEL
    depth = w_in.shape[0]
    assert depth == 1, "the final norm is fused into the last layer's PEER pass"
    l = 0
    xp, xs = x_prompt.reshape(n_p, d), x_sample.reshape(n_s, d)

    aq, ak, av, mqk, mv, mo, gates = _inproj(xp, xs, norm1_g[l], w_in[l])
    zeros = lambda *shp: jnp.zeros(shp, f32)
    mparams = (conv_w[l], conv_b[l], b_gates[l], mh_norm_g[l])
    retrieval = (w_out[l], norm2_g[l], peer_wq[l], peer_keys[l])

    def mixers(seq, lo, hi, state):
        att = _attn_prompt(aq, ak, av, rel_bias[l], seq, sp, lo, hi)
        h, c, nn, mm = _mlstm(mqk, mv, mo, gates, seq * sp + lo, 1, hi - lo, *state, *mparams)
        conv_rows = mqk[seq * sp + hi - (CONV_W - 1):seq * sp + hi][None]
        return att, h, (c, nn[:, :M_HEADS], mm[:, :M_HEADS, 0], conv_rows)

    fresh = (zeros(1, M_HEADS, M_DH, M_DH), zeros(1, M_HEADS, M_DH), zeros(1, M_HEADS), zeros(1, CONV_W - 1, 2 * M_WIDTH))

    states = [fresh] * bp

    def stage(lo, hi, after, tail=None):
        parts = []
        for seq in range(bp):
            a, b = max(lo, seq * sp), min(hi, (seq + 1) * sp)
            if a < b:
                att, h, states[seq] = mixers(seq, a - seq * sp, b - seq * sp, states[seq])
                parts.append((att, h, b - a, xp, a))
        parts += [tail] if tail is not None else []
        assert 1 <= len(parts) <= 2, "the retrieval kernel reads at most two row sources"
        (att0, h0, n0, x0, r0), (att1, h1, n1, x1, r1) = parts[0], parts[-1]
        n1, r1 = (n1, r1) if len(parts) == 2 else (0, 0)
        return _mid(x0, r0, x1, r1, att0, att1, h0, h1, n0, n1, after, *retrieval)

    sc_tables = _sc_table(peer_u[l]), _sc_table(peer_v[l])
    peers, x1s, lo, after = [], [], 0, jnp.zeros((SUBLANES, P_SLOTS), i32)
    for hi in SC_STAGE_ENDS:
        x1, xn, eidx, _, gate = stage(lo, hi, after)
        peers.append(_sc_peer(*sc_tables, eidx, xn, gate))
        x1s.append(x1)
        lo, after = hi, eidx

    lcache = cache_k.shape[2]
    att_s = _attn_sample(aq, ak, av, cache_k[l].reshape(bs, lcache, A_WIDTH),
                         cache_v[l].reshape(bs, lcache, A_WIDTH), rel_bias[l], n_p, bs, ts)
    h_s, c_s, nn_s, mm_s = _mlstm(mqk, mv, mo, gates, n_p, bs, ts, state_C[l], state_n[l], state_m[l],
                                  state_conv[l], *mparams)
    x1, xn, _, eidx_t, gate = stage(lo, n_p, after, tail=(att_s, h_s, n_s, xs, 0))
    w = _peer_u(eidx_t, xn, gate, _expert_table(peer_u[l]), n - lo)
    peers.append(_peer_v(eidx_t, w, _expert_table(peer_v[l])))
    x1s.append(x1)
    y_p, y_s = _final(peers, x1s, final_g, n_p)
    c_p, nn_p, mm_p = (jnp.concatenate(per_seq, axis=0) for per_seq in zip(*(st[:3] for st in states)))

    def tail(a, row0, bsz, t, keep):
        return jnp.stack([a[row0 + (b + 1) * t - keep:row0 + (b + 1) * t] for b in range(bsz)])

    keep = min(WINDOW, sp)
    heads = lambda a: a.reshape(a.shape[0], a.shape[1], A_HEADS, A_DH)
    ctail = CONV_W - 1
    conv_tail = lambda buf, a, row0, bsz, t: jnp.concatenate([buf.astype(a.dtype), tail(a, row0, bsz, t, min(ctail, t))],
                                                             axis=1)[:, -ctail:]
    st = lambda a: a[None]
    return (y_p.reshape(bp, sp, d), y_s.reshape(bs, ts, d),
            st(heads(tail(ak, 0, bp, sp, keep))), st(heads(tail(av, 0, bp, sp, keep))),
            st(c_p), st(nn_p), st(mm_p),
            st(conv_tail(zeros(bp, ctail, 2 * M_WIDTH), mqk, 0, bp, sp)),
            st(heads(ak[n_p:].reshape(bs, ts, A_WIDTH))), st(heads(av[n_p:].reshape(bs, ts, A_WIDTH))),
            st(c_s), st(nn_s[:, :M_HEADS]), st(mm_s[:, :M_HEADS, 0]),
            st(conv_tail(state_conv[l], mqk, n_p, bs, ts)))
```

```python
import functools
import math

import jax
import jax.numpy as jnp
from jax import lax
from jax.experimental import pallas as pl
from jax.experimental.pallas import tpu as pltpu
from jax.experimental.pallas import tpu_sc as plsc

f32 = jnp.float32
bf16 = jnp.bfloat16
i32 = jnp.int32

D_MODEL = 1024
CHUNK = 64
A_HEADS = 8
A_DH = 64
A_WIDTH = A_HEADS * A_DH
BAND_CHUNKS = 8
WINDOW = BAND_CHUNKS * CHUNK
MAX_REL = 128
ATT_SCALE = A_DH ** -0.5
M_HEADS = 4
M_DH = 128
M_WIDTH = M_HEADS * M_DH
CONV_W = 4
P_HEADS = 8
P_DKEY = 256
N_KEYS = 128
P_TOPK = 16
P_SLOTS = P_HEADS * P_TOPK
EPS = 1e-6
NEG = -1e30

LANES = 128
SUBLANES = 8
ROW_BLOCK = 256
ATT_TILE = 512
ATT_SUB = 128
ATT_KEYS = ATT_SUB + WINDOW
ROW_TILES = D_MODEL // LANES
G_ROWS = P_SLOTS * ROW_TILES
TOKEN_UNROLL = 8
INDEX_GROUPS = P_SLOTS // SUBLANES
VMEM_LIMIT = 56 * 1024 * 1024

SC_WORKERS = 32
SC_LANES = 16
SC_UNIT_ROWS = P_SLOTS // 2
SC_ROW_GROUP = 16
SC_WORDS = D_MODEL // 2
SC_CHUNKS = SC_WORDS // SC_LANES
SC_CHUNK_GROUP = 8
SC_STAGE_ENDS = (2560, 6656, 20480)


def _rms(x, g):
    return x * lax.rsqrt(jnp.mean(x * x, axis=-1, keepdims=True) + EPS) * g


def _split_bf16(x):
    hi = x.astype(bf16)
    lo = (x - hi.astype(f32)).astype(bf16)
    return hi, lo


def _pair_specs(nbp, width, first=0, second=0):
    return (pl.BlockSpec((ROW_BLOCK, width), lambda i: (first + jnp.minimum(i, nbp - 1), 0)),
            pl.BlockSpec((ROW_BLOCK, width), lambda i: (second + jnp.maximum(i - nbp, 0), 0)))


def _pick(nbp, p_ref, s_ref):
    return jnp.where(pl.program_id(0) < nbp, p_ref[...], s_ref[...])


def _inproj_kernel(xp_ref, xs_ref, g_ref, w_ref, wgh_ref, wgl_ref, after_ref,
                   aq_ref, ak_ref, av_ref, mqk_ref, mv_ref, mo_ref, gate_ref, *, nbp):
    del after_ref
    xn = _rms(_pick(nbp, xp_ref, xs_ref), g_ref[...])
    xh, xl = _split_bf16(xn)

    def proj(lo, hi):
        return jnp.dot(xh, w_ref[:, lo:hi], preferred_element_type=f32)

    aq_ref[...] = proj(0, 512)
    ak_ref[...] = proj(512, 1024)
    av_ref[...] = proj(1024, 1536)
    mqk_ref[...] = proj(1536, 2560)
    mv_ref[...] = proj(2560, 3072)
    mo_ref[...] = proj(3072, 3584)
    gate_ref[...] = (jnp.dot(xh, wgh_ref[...], preferred_element_type=f32)
                     + jnp.dot(xl, wgh_ref[...], preferred_element_type=f32)
                     + jnp.dot(xh, wgl_ref[...], preferred_element_type=f32))


def _inproj(xp, n_p, xs, g1, w_in, after):
    n = n_p + xs.shape[0]
    nbp = n_p // ROW_BLOCK
    main = 3 * A_WIDTH + 4 * M_WIDTH
    w_main = w_in[:, :main].astype(bf16)
    wg = jnp.pad(w_in[:, main:], ((0, 0), (0, LANES - 2 * M_HEADS)))
    wgh, wgl = _split_bf16(wg)
    widths = (512, 512, 512, 1024, 512, 512, LANES)
    row = lambda w: pl.BlockSpec((ROW_BLOCK, w), lambda i: (i, 0))
    full = lambda a: pl.BlockSpec(a.shape, lambda i: (0,) * a.ndim)
    g = g1.reshape(1, D_MODEL)
    return pl.pallas_call(
        functools.partial(_inproj_kernel, nbp=nbp),
        grid=(n // ROW_BLOCK,),
        in_specs=[*_pair_specs(nbp, D_MODEL), full(g), full(w_main), full(wgh), full(wgl),
                  pl.BlockSpec((SUBLANES, P_SLOTS), lambda i: (0, 0))],
        out_specs=[row(w) for w in widths],
        out_shape=[jax.ShapeDtypeStruct((n, w), f32) for w in widths],
        compiler_params=pltpu.CompilerParams(dimension_semantics=("parallel",), vmem_limit_bytes=VMEM_LIMIT),
        name="inproj",
    )(xp, xs, g, w_main, wgh, wgl, after)


def _attn_heads(q, k, v, bias_ref, key_ok):
    outs = []
    for h in range(A_HEADS):
        sl = slice(h * A_DH, (h + 1) * A_DH)
        s = lax.dot_general(q[:, sl], k[:, sl], (((1,), (1,)), ((), ())), preferred_element_type=f32)
        s = s * ATT_SCALE + bias_ref[h]
        if key_ok is not None:
            s = jnp.where(key_ok, s, NEG)
        m = jnp.max(s, axis=-1, keepdims=True)
        p = jnp.exp(s - m)
        l = jnp.sum(p, axis=-1, keepdims=True)
        o = jnp.dot(p.astype(bf16), v[:, sl], preferred_element_type=f32)
        outs.append(o / l)
    return jnp.concatenate(outs, axis=-1)


def _attn_prompt_kernel(q_ref, k0_ref, k1_ref, v0_ref, v1_ref, bias_ref, o_ref, *, t0):
    t = t0 + pl.program_id(1)
    q = q_ref[...].astype(bf16)
    k = jnp.concatenate([k0_ref[...], k1_ref[...]], axis=0).astype(bf16)
    v = jnp.concatenate([v0_ref[...], v1_ref[...]], axis=0).astype(bf16)
    col = lax.broadcasted_iota(i32, (1, ATT_KEYS), 1)
    for s in range(ATT_TILE // ATT_SUB):
        lo = s * ATT_SUB
        key_ok = (t * ATT_TILE + lo + col) >= WINDOW
        o_ref[lo:lo + ATT_SUB, :] = _attn_heads(q[lo:lo + ATT_SUB], k[lo:lo + ATT_KEYS], v[lo:lo + ATT_KEYS],
                                                 bias_ref, key_ok)


def _attn_sample_kernel(q_ref, k_ref, v_ref, bias_ref, o_ref):
    o_ref[...] = _attn_heads(q_ref[...].astype(bf16), k_ref[0].astype(bf16), v_ref[0].astype(bf16), bias_ref, None)


def _rel_bias_table(rel_bias, rows, cols, offset, valid):
    span = rows + cols - 1
    rel = offset + rows - 1 - jnp.arange(span)
    diag = rel_bias[:, jnp.clip(rel, -MAX_REL, MAX_REL) + MAX_REL].astype(f32)
    diag = jnp.pad(diag, ((0, 0), (0, 1)))
    flat = jnp.tile(diag, (1, rows))[:, rows - 1:rows - 1 + rows * span]
    return jnp.where(valid[None], flat.reshape(-1, rows, span)[:, :, :cols], NEG)


def _attn_prompt(q, k, v, rel_bias, seq, s, lo, hi):
    assert s % ATT_TILE == 0 and WINDOW == ATT_TILE and lo % ATT_TILE == 0 and hi % ATT_TILE == 0
    nt, t0, cnt = s // ATT_TILE, lo // ATT_TILE, (hi - lo) // ATT_TILE
    i = jnp.arange(ATT_SUB)[:, None]
    j = jnp.arange(ATT_KEYS)[None, :]
    off = j - (i // CHUNK) * CHUNK
    bias = _rel_bias_table(rel_bias, ATT_SUB, ATT_KEYS, WINDOW, (off >= 0) & (off < WINDOW + CHUNK))
    cur = pl.BlockSpec((ATT_TILE, A_WIDTH), lambda b, t: (seq * nt + t0 + t, 0))
    prev = pl.BlockSpec((ATT_TILE, A_WIDTH), lambda b, t: (seq * nt + jnp.maximum(t0 + t - 1, 0), 0))
    return pl.pallas_call(
        functools.partial(_attn_prompt_kernel, t0=t0),
        grid=(1, cnt),
        in_specs=[cur, prev, cur, prev, cur, pl.BlockSpec(bias.shape, lambda b, t: (0, 0, 0))],
        out_specs=pl.BlockSpec((ATT_TILE, A_WIDTH), lambda b, t: (t, 0)),
        out_shape=jax.ShapeDtypeStruct((hi - lo, A_WIDTH), f32),
        compiler_params=pltpu.CompilerParams(dimension_semantics=("parallel", "parallel"),
                                             vmem_limit_bytes=VMEM_LIMIT),
        name="attn_prompt",
    )(q, k, k, v, v, bias)


def _attn_sample(q, k, v, ck, cv, rel_bias, row0, bsz, t):
    l = ck.shape[1]
    assert row0 % t == 0
    keys = -(-(l + t) // LANES) * LANES
    padk = ((0, 0), (0, keys - l - t), (0, 0))
    kk = jnp.pad(jnp.concatenate([ck, k[row0:].reshape(bsz, t, A_WIDTH)], axis=1), padk)
    vv = jnp.pad(jnp.concatenate([cv, v[row0:].reshape(bsz, t, A_WIDTH)], axis=1), padk)
    j = jnp.arange(keys)[None, :]
    bias = _rel_bias_table(rel_bias, t, keys, l, jnp.broadcast_to(j < l + t, (t, keys)))
    return pl.pallas_call(
        _attn_sample_kernel,
        grid=(bsz,),
        in_specs=[pl.BlockSpec((t, A_WIDTH), lambda b: (row0 // t + b, 0)),
                  pl.BlockSpec((1, keys, A_WIDTH), lambda b: (b, 0, 0)),
                  pl.BlockSpec((1, keys, A_WIDTH), lambda b: (b, 0, 0)),
                  pl.BlockSpec(bias.shape, lambda b: (0, 0, 0))],
        out_specs=pl.BlockSpec((t, A_WIDTH), lambda b: (b, 0)),
        out_shape=jax.ShapeDtypeStruct((bsz * t, A_WIDTH), f32),
        compiler_params=pltpu.CompilerParams(dimension_semantics=("parallel",), vmem_limit_bytes=VMEM_LIMIT),
        name="attn_sample",
    )(q, kk, vv, bias)


def _mlstm_chunk(a, vall, o_in, gate_in, cw_ref, cb_ref, bg_ref, mhg_ref, c_s, n_s, m_s, prev_s):
    lc = a.shape[0]
    ext = jnp.concatenate([prev_s[...], a], axis=0)
    conv = cb_ref[...]
    for j in range(CONV_W):
        lo = SUBLANES - (CONV_W - 1) + j
        conv = conv + cw_ref[j:j + 1, :] * ext[lo:lo + lc]
    prev_s[...] = a[lc - SUBLANES:lc]
    qk = conv * jax.nn.sigmoid(conv)

    z = gate_in + bg_ref[...]
    lane = lax.broadcasted_iota(i32, (lc, LANES), 1)
    row = lax.broadcasted_iota(i32, (lc, LANES), 0)
    logf = jnp.minimum(z, 0.0) - jnp.log1p(jnp.exp(-jnp.abs(z)))
    cum = jnp.where((lane >= M_HEADS) & (lane < 2 * M_HEADS), logf, 0.0)
    shift = 1
    while shift < lc:
        cum = cum + jnp.where(row >= shift, pltpu.roll(cum, shift, axis=0), 0.0)
        shift *= 2
    zc = jnp.where(lane < M_HEADS, z, cum)
    zt = jnp.concatenate([zc, jnp.zeros((LANES - lc, LANES), f32)], axis=0).T[:, :lc]

    ri = lax.broadcasted_iota(i32, (lc, lc), 0)
    ci = lax.broadcasted_iota(i32, (lc, lc), 1)
    causal = ri >= ci
    state = [(m_s[h:h + 1, 0:1], c_s[h], n_s[h:h + 1, :]) for h in range(M_HEADS)]
    hs, new_state = [], []
    for h in range(M_HEADS):
        sl = slice(h * M_DH, (h + 1) * M_DH)
        q = qk[:, sl]
        k = qk[:, M_WIDTH + h * M_DH:M_WIDTH + (h + 1) * M_DH] * (M_DH ** -0.5)
        v = vall[:, sl]
        i_col = zc[:, h:h + 1]
        b_col = zc[:, M_HEADS + h:M_HEADS + h + 1]
        i_row = zt[h:h + 1, :]
        b_row = zt[M_HEADS + h:M_HEADS + h + 1, :]
        m_prev, c_prev, n_prev = state[h]

        dmat = jnp.where(causal, b_col - b_row + i_row, NEG)
        inter = b_col + m_prev
        mt = jnp.maximum(jnp.max(dmat, axis=-1, keepdims=True), inter)
        w_intra = jnp.exp(dmat - mt)
        w_inter = jnp.exp(inter - mt)
        qb, kb, vb = q.astype(bf16), k.astype(bf16), v.astype(bf16)
        s = lax.dot_general(qb, kb, (((1,), (1,)), ((), ())), preferred_element_type=f32) * w_intra
        num = (w_inter * jnp.dot(qb, c_prev.astype(bf16), preferred_element_type=f32)
               + jnp.dot(s.astype(bf16), vb, preferred_element_type=f32))
        den = w_inter * jnp.sum(q * n_prev, axis=-1, keepdims=True) + jnp.sum(s, axis=-1, keepdims=True)
        hh = num / jnp.maximum(jnp.abs(den), jnp.exp(-mt))
        m_new = mt[lc - 1:lc, :]
        b_last = b_col[lc - 1:lc, :]
        w_s = jnp.exp(b_last - b_col + i_col - m_new)
        decay = jnp.exp(b_last + m_prev - m_new)
        kw = k * w_s
        new_state.append((jnp.broadcast_to(m_new, (1, LANES)),
                          decay * c_prev + lax.dot_general(kw.astype(bf16), vb, (((0,), (0,)), ((), ())),
                                                           preferred_element_type=f32),
                          decay * n_prev + jnp.sum(kw, axis=0, keepdims=True)))
        hs.append(hh * lax.rsqrt(jnp.mean(hh * hh, axis=-1, keepdims=True) + EPS))

    for h, (m_new, c_new, n_new) in enumerate(new_state):
        m_s[h:h + 1, :] = m_new
        c_s[h] = c_new
        n_s[h:h + 1, :] = n_new
    return jnp.concatenate(hs, axis=-1) * mhg_ref[...] * jax.nn.sigmoid(o_in)


def _mlstm_kernel(qk_ref, v_ref, o_ref, gate_ref, c0_ref, n0_ref, m0_ref, cbuf_ref,
                  cw_ref, cb_ref, bg_ref, mhg_ref,
                  h_ref, cout_ref, nout_ref, mout_ref,
                  c_s, n_s, m_s, prev_s):
    c = pl.program_id(1)

    @pl.when(c == 0)
    def _():
        c_s[...] = c0_ref[0]
        n_s[...] = n0_ref[0]
        m_s[...] = m0_ref[0]
        prev_s[...] = cbuf_ref[0]

    h_ref[...] = _mlstm_chunk(qk_ref[...], v_ref[...], o_ref[...], gate_ref[...],
                              cw_ref, cb_ref, bg_ref, mhg_ref, c_s, n_s, m_s, prev_s)

    @pl.when(c == pl.num_programs(1) - 1)
    def _():
        cout_ref[0] = c_s[...]
        nout_ref[0] = n_s[...]
        mout_ref[0] = m_s[...]


def _mlstm(mqk, mv, mo, gates, row0, bsz, t, c0, n0, m0, cbuf, conv_w, conv_b, b_gates, mh_g):
    lc = min(CHUNK, t)
    step = lc
    nc = t // step
    assert t % step == 0 and lc % SUBLANES == 0 and row0 % step == 0
    n0p = jnp.pad(n0.astype(f32), ((0, 0), (0, SUBLANES - M_HEADS), (0, 0)))
    m0p = jnp.pad(jnp.broadcast_to(m0.astype(f32)[:, :, None], (bsz, M_HEADS, LANES)),
                  ((0, 0), (0, SUBLANES - M_HEADS), (0, 0)))
    cbp = jnp.pad(cbuf.astype(f32), ((0, 0), (SUBLANES - (CONV_W - 1), 0), (0, 0)))
    bg = jnp.pad(b_gates.astype(f32), (0, LANES - 2 * M_HEADS)).reshape(1, LANES)
    seq = lambda w: pl.BlockSpec((step, w), lambda b, c: (row0 // step + b * nc + c, 0))
    out_seq = pl.BlockSpec((step, M_WIDTH), lambda b, c: (b * nc + c, 0))
    per_b = lambda shp: pl.BlockSpec((1,) + shp, lambda b, c: (b,) + (0,) * len(shp))
    full = lambda a: pl.BlockSpec(a.shape, lambda b, c: (0,) * a.ndim)
    cb = conv_b.reshape(1, -1)
    mhg = mh_g.reshape(1, -1)
    return pl.pallas_call(
        _mlstm_kernel,
        grid=(bsz, nc),
        in_specs=[seq(2 * M_WIDTH), seq(M_WIDTH), seq(M_WIDTH), seq(LANES),
                  per_b((M_HEADS, M_DH, M_DH)), per_b((SUBLANES, M_DH)), per_b((SUBLANES, LANES)),
                  per_b((SUBLANES, 2 * M_WIDTH)),
                  full(conv_w), full(cb), full(bg), full(mhg)],
        out_specs=[out_seq, per_b((M_HEADS, M_DH, M_DH)), per_b((SUBLANES, M_DH)), per_b((SUBLANES, LANES))],
        out_shape=[jax.ShapeDtypeStruct((bsz * t, M_WIDTH), f32),
                   jax.ShapeDtypeStruct((bsz, M_HEADS, M_DH, M_DH), f32),
                   jax.ShapeDtypeStruct((bsz, SUBLANES, M_DH), f32),
                   jax.ShapeDtypeStruct((bsz, SUBLANES, LANES), f32)],
        scratch_shapes=[pltpu.VMEM((M_HEADS, M_DH, M_DH), f32), pltpu.VMEM((SUBLANES, M_DH), f32),
                        pltpu.VMEM((SUBLANES, LANES), f32), pltpu.VMEM((SUBLANES, 2 * M_WIDTH), f32)],
        compiler_params=pltpu.CompilerParams(dimension_semantics=("parallel", "arbitrary"),
                                             vmem_limit_bytes=VMEM_LIMIT),
        name="mlstm",
    )(mqk, mv, mo, gates, c0.astype(f32), n0p, m0p, cbp, conv_w, cb, bg, mhg)


def _topk_rows(s, k, payload=None):
    n = s.shape[0]
    rows = lax.broadcasted_iota(i32, s.shape, 0).astype(f32)
    vals, ids = [], []
    for _ in range(k):
        m = jnp.max(s, axis=0, keepdims=True)
        pos = jnp.min(jnp.where(s == m, rows, float(n)), axis=0, keepdims=True)
        sel = rows == pos
        vals.append(m)
        ids.append(pos if payload is None else jnp.max(jnp.where(sel, payload, -1.0), axis=0, keepdims=True))
        s = jnp.where(sel, -jnp.inf, s)
    return jnp.concatenate(vals, axis=0), jnp.concatenate(ids, axis=0)


def _mid_kernel(xp_ref, xs_ref, attp_ref, atts_ref, hp_ref, hs_ref, wo_ref, g2_ref, wq_ref, keys_ref, after_ref,
                x1_ref, xn_ref, eidx_ref, eidx_t_ref, gate_ref, *, nbp):
    del after_ref
    cat = jnp.concatenate([_pick(nbp, attp_ref, atts_ref), _pick(nbp, hp_ref, hs_ref)], axis=-1).astype(bf16)
    x1 = _pick(nbp, xp_ref, xs_ref) + jnp.dot(cat, wo_ref[...], preferred_element_type=f32)
    x1_ref[...] = x1
    xn = _rms(x1, g2_ref[...])
    xn_ref[...] = xn
    xb = xn.astype(bf16)
    e_rows, g_rows = [], []
    for h in range(P_HEADS):
        q = jnp.dot(xb, wq_ref[:, h * P_DKEY:(h + 1) * P_DKEY], preferred_element_type=f32).astype(bf16)
        half = []
        for p in range(2):
            st = lax.dot_general(keys_ref[h, p], q[:, p * N_KEYS:(p + 1) * N_KEYS], (((1,), (1,)), ((), ())),
                                 preferred_element_type=f32)
            half.append(_topk_rows(st, P_TOPK))
        (v0, i0), (v1, i1) = half
        width = [P_TOPK // (a + 1) for a in range(P_TOPK)]
        fill = -sum(width) % SUBLANES
        cand = jnp.concatenate([v0[a:a + 1] + v1[:width[a]] for a in range(P_TOPK)]
                               + [jnp.full((fill, v0.shape[1]), -jnp.inf, f32)], axis=0)
        cidx = jnp.concatenate([i0[a:a + 1] * float(N_KEYS) + i1[:width[a]] for a in range(P_TOPK)]
                               + [jnp.zeros((fill, v0.shape[1]), f32)], axis=0)
        top_s, eid = _topk_rows(cand, P_TOPK, payload=cidx)
        ex = jnp.exp(top_s - top_s[0:1])
        e_rows.append(eid)
        g_rows.append(ex / jnp.sum(ex, axis=0, keepdims=True))
    eidx_t = jnp.concatenate(e_rows, axis=0)
    eidx_t_ref[...] = eidx_t.astype(i32)
    eidx_ref[...] = eidx_t.T.astype(i32)
    gate_ref[...] = jnp.concatenate(g_rows, axis=0).T


def _mid(xp, xp_row0, xs, xs_row0, attp, atts, hp, hs, n_a, n_b, after, w_out, g2, wq, keys):
    n = n_a + n_b
    nbp = n_a // ROW_BLOCK
    assert n_a % ROW_BLOCK == 0 and n_b % ROW_BLOCK == 0 and xp_row0 % ROW_BLOCK == 0 and xs_row0 % ROW_BLOCK == 0
    row = lambda w: pl.BlockSpec((ROW_BLOCK, w), lambda i: (i, 0))
    full = lambda a: pl.BlockSpec(a.shape, lambda i: (0,) * a.ndim)
    wo = w_out.astype(bf16)
    wqb = wq.astype(bf16)
    kb = keys.astype(bf16)
    g = g2.reshape(1, D_MODEL)
    return pl.pallas_call(
        functools.partial(_mid_kernel, nbp=nbp),
        grid=(n // ROW_BLOCK,),
        in_specs=[*_pair_specs(nbp, D_MODEL, xp_row0 // ROW_BLOCK, xs_row0 // ROW_BLOCK), *_pair_specs(nbp, A_WIDTH),
                  *_pair_specs(nbp, M_WIDTH), full(wo), full(g), full(wqb), full(kb),
                  pl.BlockSpec((SUBLANES, P_SLOTS), lambda i: (0, 0))],
        out_specs=[row(D_MODEL), row(D_MODEL), row(P_SLOTS), pl.BlockSpec((P_SLOTS, ROW_BLOCK), lambda i: (0, i)),
                   row(P_SLOTS)],
        out_shape=[jax.ShapeDtypeStruct((n, D_MODEL), f32), jax.ShapeDtypeStruct((n, D_MODEL), f32),
                   jax.ShapeDtypeStruct((n, P_SLOTS), i32), jax.ShapeDtypeStruct((P_SLOTS, n), i32),
                   jax.ShapeDtypeStruct((n, P_SLOTS), f32)],
        compiler_params=pltpu.CompilerParams(dimension_semantics=("parallel",), vmem_limit_bytes=VMEM_LIMIT),
        name="outproj_retrieve",
    )(xp, xs, attp, atts, hp, hs, wo, g, wqb, kb, after)


def _gather_rows(eidx_refs, tab_ref, g_ref, t):
    for i in range(SUBLANES):
        for k, eidx_ref in enumerate(eidx_refs):
            r = k * SUBLANES + i
            g_ref[pl.ds(r * ROW_TILES, ROW_TILES), :] = tab_ref[eidx_ref[i, t]]


def _tile_rows(t, rows=SUBLANES):
    return pl.ds(pl.multiple_of(t * rows, rows), rows)


def _pipelined_tokens(nt, eidx_refs, tab_ref, g0_s, g1_s, compute):
    bufs = (g0_s, g1_s)
    _gather_rows(eidx_refs, tab_ref, g0_s, 0)

    def body(j, carry):
        for u in range(TOKEN_UNROLL):
            t = TOKEN_UNROLL * j + u
            _gather_rows(eidx_refs, tab_ref, bufs[(u + 1) % 2], jnp.minimum(t + 1, nt - 1))
            compute(t, bufs[u % 2])
        return carry

    lax.fori_loop(0, nt // TOKEN_UNROLL, body, 0)


def _gelu_tanh(x):
    return 0.5 * x * (1.0 + jnp.tanh(math.sqrt(2.0 / math.pi) * (x + 0.044715 * (x * x * x))))


def _peer_u_kernel(*refs):
    eidx_refs, (xn_ref, g_ref, tab_ref, w_ref, xl_s, r_s, g0_s, g1_s) = refs[:INDEX_GROUPS], refs[INDEX_GROUPS:]
    nt = xn_ref.shape[0]
    xn = xn_ref[...]
    xh = xn.astype(bf16).astype(f32)
    xl = xn - xh
    for k in range(ROW_TILES):
        xl_s[pl.ds(k, nt, stride=2 * SUBLANES), :] = xh[:, k * LANES:(k + 1) * LANES]
        xl_s[pl.ds(SUBLANES + k, nt, stride=2 * SUBLANES), :] = xl[:, k * LANES:(k + 1) * LANES]
    diag = (lax.broadcasted_iota(i32, (SUBLANES, G_ROWS), 1) % ROW_TILES
            == lax.broadcasted_iota(i32, (SUBLANES, G_ROWS), 0))

    def compute(t, g_s):
        lhs = xl_s[_tile_rows(t, 2 * SUBLANES), :].astype(bf16)
        out = lax.dot_general(lhs, g_s[...], (((1,), (1,)), ((), ())), preferred_element_type=f32)
        part = jnp.where(diag, out[:SUBLANES] + out[SUBLANES:], 0.0)
        for c in range(ROW_TILES):
            r_s[c, _tile_rows(t), :] = part[:, c * LANES:(c + 1) * LANES]

    _pipelined_tokens(nt, eidx_refs, tab_ref, g0_s, g1_s, compute)
    cols = []
    for c in range(ROW_TILES):
        acc = r_s[c, pl.ds(0, nt, stride=SUBLANES), :]
        for k in range(1, SUBLANES):
            acc = acc + r_s[c, pl.ds(k, nt, stride=SUBLANES), :]
        cols.append(acc)
    s = jnp.concatenate(cols, axis=-1)
    fold = (lax.broadcasted_iota(i32, (G_ROWS, P_SLOTS), 0) // ROW_TILES
            == lax.broadcasted_iota(i32, (G_ROWS, P_SLOTS), 1)).astype(bf16)
    sh, sl = _split_bf16(s)
    act = jnp.dot(sh, fold, preferred_element_type=f32) + jnp.dot(sl, fold, preferred_element_type=f32)
    w_ref[...] = g_ref[...] * _gelu_tanh(act)


def _peer_v_kernel(*refs):
    eidx_refs, (w_ref, tab_ref, peer_ref, wl_s, o_s, g0_s, g1_s) = refs[:INDEX_GROUPS], refs[INDEX_GROUPS:]
    nt = w_ref.shape[0]
    spread = (lax.broadcasted_iota(i32, (P_SLOTS, G_ROWS), 1) // ROW_TILES
              == lax.broadcasted_iota(i32, (P_SLOTS, G_ROWS), 0)).astype(bf16)
    wexp = jnp.dot(w_ref[...].astype(bf16), spread, preferred_element_type=f32)
    lane = lax.broadcasted_iota(i32, (nt, LANES), 1)
    for c in range(ROW_TILES):
        wc = wexp[:, c * LANES:(c + 1) * LANES]
        for k in range(SUBLANES):
            wl_s[c, pl.ds(k, nt, stride=SUBLANES), :] = jnp.where(lane % ROW_TILES == k, wc, 0.0)

    def compute(t, g_s):
        lhs = jnp.concatenate([wl_s[c, _tile_rows(t), :] for c in range(ROW_TILES)], axis=-1).astype(bf16)
        o_s[_tile_rows(t), :] = jnp.dot(lhs, g_s[...], preferred_element_type=f32)

    _pipelined_tokens(nt, eidx_refs, tab_ref, g0_s, g1_s, compute)
    for k in range(ROW_TILES):
        peer_ref[:, k * LANES:(k + 1) * LANES] = o_s[pl.ds(k, nt, stride=SUBLANES), :]


def _final_kernel(*refs, starts, nbp):
    k = len(starts)
    peer_refs, x_refs, (gf_ref, yp_ref, ys_ref) = refs[:k], refs[k:2 * k], refs[2 * k:]
    resid = x_refs[0][...] + peer_refs[0][...]
    for start, p_ref, x_ref in zip(starts[1:], peer_refs[1:], x_refs[1:]):
        resid = jnp.where(pl.program_id(0) >= start, x_ref[...] + p_ref[...], resid)
    y = _rms(resid, gf_ref[...])

    @pl.when(pl.program_id(0) < nbp)
    def _():
        yp_ref[...] = y

    @pl.when(pl.program_id(0) >= nbp)
    def _():
        ys_ref[...] = y


def _final(peers, x1s, gf, n_p):
    counts = [x.shape[0] // ROW_BLOCK for x in x1s]
    starts = [sum(counts[:k]) for k in range(len(counts))]
    n, nbp = sum(counts) * ROW_BLOCK, n_p // ROW_BLOCK
    g = gf.reshape(1, D_MODEL)
    seg = [pl.BlockSpec((ROW_BLOCK, D_MODEL), lambda i, s=s, c=c: (jnp.clip(i - s, 0, c - 1), 0))
           for s, c in zip(starts, counts)]
    return pl.pallas_call(
        functools.partial(_final_kernel, starts=tuple(starts), nbp=nbp),
        grid=(n // ROW_BLOCK,),
        in_specs=[*seg, *seg, pl.BlockSpec(g.shape, lambda i: (0, 0))],
        out_specs=list(_pair_specs(nbp, D_MODEL)),
        out_shape=[jax.ShapeDtypeStruct((n_p, D_MODEL), f32), jax.ShapeDtypeStruct((n - n_p, D_MODEL), f32)],
        compiler_params=pltpu.CompilerParams(dimension_semantics=("arbitrary",), vmem_limit_bytes=VMEM_LIMIT),
        name="final_norm",
    )(*peers, *x1s, g)


def _peer_specs():
    row = lambda w: pl.BlockSpec((ROW_BLOCK, w), lambda i: (i, 0))
    idx = [pl.BlockSpec((SUBLANES, ROW_BLOCK), lambda i, k=k: (k, i), memory_space=pltpu.SMEM)
           for k in range(INDEX_GROUPS)]
    tab = pl.BlockSpec(memory_space=pltpu.VMEM)
    gscr = pltpu.VMEM((G_ROWS, LANES), bf16)
    params = pltpu.CompilerParams(dimension_semantics=("arbitrary",), vmem_limit_bytes=VMEM_LIMIT)
    return row, idx, tab, gscr, params


def _expert_table(tab):
    return tab.astype(bf16).reshape(tab.shape[0], ROW_TILES, LANES)


def _peer_u(eidx_t, xn, g, utab, n):
    row, idx, tab, gscr, params = _peer_specs()
    return pl.pallas_call(
        _peer_u_kernel,
        grid=(n // ROW_BLOCK,),
        in_specs=[*idx, row(D_MODEL), row(P_SLOTS), tab],
        out_specs=row(P_SLOTS),
        out_shape=jax.ShapeDtypeStruct((n, P_SLOTS), f32),
        scratch_shapes=[pltpu.VMEM((ROW_BLOCK * 2 * SUBLANES, LANES), f32),
                        pltpu.VMEM((ROW_TILES, ROW_BLOCK * SUBLANES, LANES), f32), gscr, gscr],
        compiler_params=params,
        name="peer_u",
    )(*[eidx_t] * INDEX_GROUPS, xn, g, utab)


def _peer_v(eidx_t, w, vtab):
    n = w.shape[0]
    row, idx, tab, gscr, params = _peer_specs()
    return pl.pallas_call(
        _peer_v_kernel,
        grid=(n // ROW_BLOCK,),
        in_specs=[*idx, row(P_SLOTS), tab],
        out_specs=row(D_MODEL),
        out_shape=jax.ShapeDtypeStruct((n, D_MODEL), f32),
        scratch_shapes=[pltpu.VMEM((ROW_TILES, ROW_BLOCK * SUBLANES, LANES), f32),
                        pltpu.VMEM((ROW_BLOCK * SUBLANES, LANES), f32), gscr, gscr],
        compiler_params=params,
        name="peer_v",
    )(*[eidx_t] * INDEX_GROUPS, w, vtab)


def _sc_table_kernel(t_ref, o_ref):
    t = t_ref[...]
    o_ref[...] = pltpu.pack_elementwise([t[:, :SC_WORDS], t[:, SC_WORDS:]], packed_dtype=bf16)


def _sc_table(tab):
    e = tab.shape[0]
    return pl.pallas_call(
        _sc_table_kernel,
        grid=(e // ROW_BLOCK,),
        in_specs=[pl.BlockSpec((ROW_BLOCK, D_MODEL), lambda i: (i, 0))],
        out_specs=pl.BlockSpec((ROW_BLOCK, SC_WORDS), lambda i: (i, 0)),
        out_shape=jax.ShapeDtypeStruct((e, SC_WORDS), jnp.uint32),
        compiler_params=pltpu.CompilerParams(dimension_semantics=("parallel",), vmem_limit_bytes=VMEM_LIMIT),
        name="sc_table",
    )(tab)


def _sc_unpack(words):
    return plsc.unpack(plsc.bitcast(words, bf16), format=plsc.PackFormat.INTERLEAVED)


def _sc_gelu_tanh(x):
    z = math.sqrt(2.0 / math.pi) * (x + 0.044715 * (x * x * x))
    return 0.5 * x * (2.0 - 2.0 / (jnp.exp(2.0 * z) + 1.0))


def _sc_peer(utab32, vtab32, eidx, xn, gate):
    n_sc = eidx.shape[0]
    per = n_sc // SC_WORKERS
    units = 4 * per
    assert n_sc % (SC_WORKERS * SUBLANES) == 0

    def body(u_hbm, v_hbm, eidx_hbm, x_hbm, g_hbm, out_hbm,
             idx_v, x_v, g_v, rows_v, acc_v, w_v, w16_v, out_v, row_sems, tok_sems, out_sem):
        base = (lax.axis_index("s") * 2 + lax.axis_index("c")) * per
        lanes = lax.broadcasted_iota(i32, (SC_LANES,), 0)

        def token_copies(tok):
            tslot = tok % 2
            return (pltpu.make_async_copy(eidx_hbm.at[base + tok], idx_v.at[tslot], tok_sems.at[0]),
                    pltpu.make_async_copy(x_hbm.at[base + tok], x_v.at[tslot], tok_sems.at[1]),
                    pltpu.make_async_copy(g_hbm.at[base + tok], g_v.at[tslot], tok_sems.at[2]))

        def store_out(tok):
            return pltpu.make_async_copy(out_v, out_hbm.at[base + tok], out_sem)

        def gather(tab_hbm, g):
            tok, k = g // 4, g % 4
            return pltpu.make_async_copy(tab_hbm.at[idx_v.at[tok % 2, pl.ds((k % 2) * SC_UNIT_ROWS, SC_UNIT_ROWS)]],
                                         rows_v.at[k % 2], row_sems.at[k % 2])

        def start(g):
            @pl.when(g % 4 < 2)
            def _():
                gather(u_hbm, g).start()

            @pl.when(g % 4 >= 2)
            def _():
                gather(v_hbm, g).start()

        def compute_u(tslot, half, rows):
            @pl.loop(0, SC_UNIT_ROWS // SC_ROW_GROUP)
            def _(rg):
                slot0 = half * SC_UNIT_ROWS + rg * SC_ROW_GROUP

                @pl.loop(0, SC_CHUNKS // SC_CHUNK_GROUP)
                def _(cg):
                    keep = jnp.where(cg > 0, 1.0, 0.0).astype(f32)
                    accs = [acc_v[pl.ds((slot0 + r) * SC_LANES, SC_LANES)] * keep for r in range(SC_ROW_GROUP)]
                    for c in range(SC_CHUNK_GROUP):
                        ch = cg * SC_CHUNK_GROUP + c
                        xa = x_v[tslot, pl.ds(ch * SC_LANES, SC_LANES)]
                        xb = x_v[tslot, pl.ds(SC_WORDS + ch * SC_LANES, SC_LANES)]
                        for r in range(SC_ROW_GROUP):
                            a, b = _sc_unpack(rows[rg * SC_ROW_GROUP + r, pl.ds(ch * SC_LANES, SC_LANES)])
                            accs[r] = accs[r] + a * xa + b * xb
                    for r in range(SC_ROW_GROUP):
                        acc_v[pl.ds((slot0 + r) * SC_LANES, SC_LANES)] = accs[r]

        def gate_weights(tslot):
            @pl.loop(0, P_SLOTS // SC_LANES)
            def _(sg):
                first = sg * SC_LANES * SC_LANES
                act = jnp.zeros((SC_LANES,), f32)
                for lane in range(SC_LANES):
                    act = act + plsc.load_gather(acc_v, [first + lanes * SC_LANES + lane])
                w_v[pl.ds(sg * SC_LANES, SC_LANES)] = g_v[tslot, pl.ds(sg * SC_LANES, SC_LANES)] * _sc_gelu_tanh(act)

            @pl.loop(0, P_SLOTS // SC_LANES)
            def _(sg):
                for r in range(SC_LANES):
                    w16_v[pl.ds((sg * SC_LANES + r) * SC_LANES, SC_LANES)] = plsc.load_gather(
                        w_v, [jnp.zeros((SC_LANES,), i32) + (sg * SC_LANES + r)])

        def compute_v(half, rows):
            @pl.loop(0, SC_CHUNKS // SC_CHUNK_GROUP)
            def _(cg):
                first = cg * SC_CHUNK_GROUP

                @pl.loop(0, SC_UNIT_ROWS // SC_ROW_GROUP)
                def _(rg):
                    keep = jnp.where(jnp.logical_or(half == 1, rg > 0), 1.0, 0.0).astype(f32)
                    spots = [pl.ds((j % 2) * SC_WORDS + (first + j // 2) * SC_LANES, SC_LANES)
                             for j in range(2 * SC_CHUNK_GROUP)]
                    accs = [out_v[spot] * keep for spot in spots]
                    for r in range(SC_ROW_GROUP):
                        row = rg * SC_ROW_GROUP + r
                        wv = w16_v[pl.ds((half * SC_UNIT_ROWS + row) * SC_LANES, SC_LANES)]
                        for c in range(SC_CHUNK_GROUP):
                            a, b = _sc_unpack(rows[row, pl.ds((first + c) * SC_LANES, SC_LANES)])
                            accs[2 * c] = accs[2 * c] + a * wv
                            accs[2 * c + 1] = accs[2 * c + 1] + b * wv
                    for spot, acc in zip(spots, accs):
                        out_v[spot] = acc

        for cp in token_copies(0):
            cp.start()
        for cp in token_copies(0):
            cp.wait()
        start(0)

        @pl.loop(0, units)
        def _(g):
            tok, k = g // 4, g % 4

            @pl.when(jnp.logical_and(k == 0, tok + 1 < per))
            def _():
                for cp in token_copies(tok + 1):
                    cp.start()

            @pl.when(jnp.logical_and(k == 3, tok + 1 < per))
            def _():
                for cp in token_copies(tok + 1):
                    cp.wait()

            @pl.when(g + 1 < units)
            def _():
                start(g + 1)

            gather(u_hbm, g).wait()
            rows = rows_v.at[k % 2]

            @pl.when(k < 2)
            def _():
                compute_u(tok % 2, k, rows)

            @pl.when(k == 1)
            def _():
                gate_weights(tok % 2)

            @pl.when(jnp.logical_and(k == 2, tok > 0))
            def _():
                store_out(tok - 1).wait()

            @pl.when(k >= 2)
            def _():
                compute_v(k - 2, rows)

            @pl.when(k == 3)
            def _():
                store_out(tok).start()

        store_out(per - 1).wait()

    return pl.kernel(
        body, mesh=plsc.VectorSubcoreMesh(core_axis_name="c", subcore_axis_name="s"),
        out_type=jax.ShapeDtypeStruct((n_sc, D_MODEL), f32),
        scratch_types=[pltpu.VMEM((2, P_SLOTS), i32), pltpu.VMEM((2, D_MODEL), f32), pltpu.VMEM((2, P_SLOTS), f32),
                       pltpu.VMEM((2, SC_UNIT_ROWS, SC_WORDS), jnp.uint32), pltpu.VMEM((P_SLOTS * SC_LANES,), f32),
                       pltpu.VMEM((P_SLOTS,), f32), pltpu.VMEM((P_SLOTS * SC_LANES,), f32), pltpu.VMEM((D_MODEL,), f32),
                       pltpu.SemaphoreType.DMA((2,)), pltpu.SemaphoreType.DMA((3,)), pltpu.SemaphoreType.DMA],
        compiler_params=pltpu.CompilerParams(needs_layout_passes=False),
        name="sc_peer",
    )(utab32, vtab32, eidx, xn, gate)


def kernel(x_prompt, x_sample, cache_k, cache_v, state_C, state_n, state_m, state_conv, norm1_g, w_in, b_gates, rel_bias, conv_w, conv_b, mh_norm_g, w_out, norm2_g, peer_wq, peer_keys, peer_u, peer_v, final_g):
    bp, sp, d = x_prompt.shape
    bs, ts, _ = x_sample.shape
    n_p, n_s = bp * sp, bs * ts
    n = n_p + n_s
    assert n_p % ROW_BLOCK == 0 and n_s % ROW_BLOCK == 0 and d == D_MODEL
    depth = w_in.shape[0]
    assert depth == 1, "the final norm is fused into the last layer's PEER pass"
    l = 0
    xp, xs = x_prompt.reshape(n_p, d), x_sample.reshape(n_s, d)

    zeros = lambda *shp: jnp.zeros(shp, f32)
    mparams = (conv_w[l], conv_b[l], b_gates[l], mh_norm_g[l])
    retrieval = (w_out[l], norm2_g[l], peer_wq[l], peer_keys[l])

    def mixers(proj, seq, lo, hi, state):
        aq, ak, av, mqk, mv, mo, gates = proj
        att = _attn_prompt(aq, ak, av, rel_bias[l], seq, sp, lo, hi)
        h, c, nn, mm = _mlstm(mqk, mv, mo, gates, seq * sp + lo, 1, hi - lo, *state, *mparams)
        conv_rows = mqk[seq * sp + hi - (CONV_W - 1):seq * sp + hi][None]
        return att, h, (c, nn[:, :M_HEADS], mm[:, :M_HEADS, 0], conv_rows)

    fresh = (zeros(1, M_HEADS, M_DH, M_DH), zeros(1, M_HEADS, M_DH), zeros(1, M_HEADS), zeros(1, CONV_W - 1, 2 * M_WIDTH))

    states = [fresh] * bp

    def stage(proj, lo, hi, after, tail=None):
        parts = []
        for seq in range(bp):
            a, b = max(lo, seq * sp), min(hi, (seq + 1) * sp)
            if a < b:
                att, h, states[seq] = mixers(proj, seq, a - seq * sp, b - seq * sp, states[seq])
                parts.append((att, h, b - a, xp, a))
        parts += [tail] if tail is not None else []
        assert 1 <= len(parts) <= 2, "the retrieval kernel reads at most two row sources"
        (att0, h0, n0, x0, r0), (att1, h1, n1, x1, r1) = parts[0], parts[-1]
        n1, r1 = (n1, r1) if len(parts) == 2 else (0, 0)
        return _mid(x0, r0, x1, r1, att0, att1, h0, h1, n0, n1, after, *retrieval)

    sc_tables = _sc_table(peer_u[l]), _sc_table(peer_v[l])
    peers, x1s, lo, after = [], [], 0, jnp.zeros((SUBLANES, P_SLOTS), i32)
    proj = _inproj(xp, SC_STAGE_ENDS[0], xs, norm1_g[l], w_in[l], after)
    for hi in SC_STAGE_ENDS:
        x1, xn, eidx, _, gate = stage(proj, lo, hi, after)
        peers.append(_sc_peer(*sc_tables, eidx, xn, gate))
        x1s.append(x1)
        if lo == 0:
            proj = _inproj(xp, n_p, xs, norm1_g[l], w_in[l], eidx)
        lo, after = hi, eidx
    aq, ak, av, mqk, mv, mo, gates = proj

    lcache = cache_k.shape[2]
    att_s = _attn_sample(aq, ak, av, cache_k[l].reshape(bs, lcache, A_WIDTH),
                         cache_v[l].reshape(bs, lcache, A_WIDTH), rel_bias[l], n_p, bs, ts)
    h_s, c_s, nn_s, mm_s = _mlstm(mqk, mv, mo, gates, n_p, bs, ts, state_C[l], state_n[l], state_m[l],
                                  state_conv[l], *mparams)
    x1, xn, _, eidx_t, gate = stage(proj, lo, n_p, after, tail=(att_s, h_s, n_s, xs, 0))
    w = _peer_u(eidx_t, xn, gate, _expert_table(peer_u[l]), n - lo)
    peers.append(_peer_v(eidx_t, w, _expert_table(peer_v[l])))
    x1s.append(x1)
    y_p, y_s = _final(peers, x1s, final_g, n_p)
    c_p, nn_p, mm_p = (jnp.concatenate(per_seq, axis=0) for per_seq in zip(*(st[:3] for st in states)))

    def tail(a, row0, bsz, t, keep):
        return jnp.stack([a[row0 + (b + 1) * t - keep:row0 + (b + 1) * t] for b in range(bsz)])

    keep = min(WINDOW, sp)
    heads = lambda a: a.reshape(a.shape[0], a.shape[1], A_HEADS, A_DH)
    ctail = CONV_W - 1
    conv_tail = lambda buf, a, row0, bsz, t: jnp.concatenate([buf.astype(a.dtype), tail(a, row0, bsz, t, min(ctail, t))],
                                                             axis=1)[:, -ctail:]
    st = lambda a: a[None]
    return (y_p.reshape(bp, sp, d), y_s.reshape(bs, ts, d),
            st(heads(tail(ak, 0, bp, sp, keep))), st(heads(tail(av, 0, bp, sp, keep))),
            st(c_p), st(nn_p), st(mm_p),
            st(conv_tail(zeros(bp, ctail, 2 * M_WIDTH), mqk, 0, bp, sp)),
            st(heads(ak[n_p:].reshape(bs, ts, A_WIDTH))), st(heads(av[n_p:].reshape(bs, ts, A_WIDTH))),
            st(c_s), st(nn_s[:, :M_HEADS]), st(mm_s[:, :M_HEADS, 0]),
            st(conv_tail(state_conv[l], mqk, n_p, bs, ts)))
```

```python
import functools
import math

import jax
import jax.numpy as jnp
from jax import lax
from jax.experimental import pallas as pl
from jax.experimental.pallas import tpu as pltpu
from jax.experimental.pallas import tpu_sc as plsc

f32 = jnp.float32
bf16 = jnp.bfloat16
i32 = jnp.int32

D_MODEL = 1024
CHUNK = 64
A_HEADS = 8
A_DH = 64
A_WIDTH = A_HEADS * A_DH
BAND_CHUNKS = 8
WINDOW = BAND_CHUNKS * CHUNK
MAX_REL = 128
ATT_SCALE = A_DH ** -0.5
M_HEADS = 4
M_DH = 128
M_WIDTH = M_HEADS * M_DH
CONV_W = 4
P_HEADS = 8
P_DKEY = 256
N_KEYS = 128
P_TOPK = 16
P_SLOTS = P_HEADS * P_TOPK
EPS = 1e-6
NEG = -1e30

LANES = 128
SUBLANES = 8
ROW_BLOCK = 256
ATT_TILE = 512
ATT_SUB = 128
ATT_KEYS = ATT_SUB + WINDOW
ROW_TILES = D_MODEL // LANES
G_ROWS = P_SLOTS * ROW_TILES
TOKEN_UNROLL = 8
INDEX_GROUPS = P_SLOTS // SUBLANES
VMEM_LIMIT = 56 * 1024 * 1024

SC_WORKERS = 32
SC_LANES = 16
SC_UNIT_ROWS = P_SLOTS // 2
SC_ROW_GROUP = 16
SC_WORDS = D_MODEL // 2
SC_CHUNKS = SC_WORDS // SC_LANES
SC_CHUNK_GROUP = 8
SC_STAGE_ENDS = (2560, 6656, 20480)


def _rms(x, g):
    return x * lax.rsqrt(jnp.mean(x * x, axis=-1, keepdims=True) + EPS) * g


def _split_bf16(x):
    hi = x.astype(bf16)
    lo = (x - hi.astype(f32)).astype(bf16)
    return hi, lo


def _pair_specs(nbp, width, first=0, second=0):
    return (pl.BlockSpec((ROW_BLOCK, width), lambda i: (first + jnp.minimum(i, nbp - 1), 0)),
            pl.BlockSpec((ROW_BLOCK, width), lambda i: (second + jnp.maximum(i - nbp, 0), 0)))


def _pick(nbp, p_ref, s_ref):
    return jnp.where(pl.program_id(0) < nbp, p_ref[...], s_ref[...])


def _inproj_kernel(xp_ref, xs_ref, g_ref, w_ref, wgh_ref, wgl_ref, after_ref,
                   aq_ref, ak_ref, av_ref, mqk_ref, mv_ref, mo_ref, gate_ref, *, nbp):
    del after_ref
    xn = _rms(_pick(nbp, xp_ref, xs_ref), g_ref[...])
    xh, xl = _split_bf16(xn)

    def proj(lo, hi):
        return jnp.dot(xh, w_ref[:, lo:hi], preferred_element_type=f32)

    aq_ref[...] = proj(0, 512)
    ak_ref[...] = proj(512, 1024)
    av_ref[...] = proj(1024, 1536)
    mqk_ref[...] = proj(1536, 2560)
    mv_ref[...] = proj(2560, 3072)
    mo_ref[...] = proj(3072, 3584)
    gate_ref[...] = (jnp.dot(xh, wgh_ref[...], preferred_element_type=f32)
                     + jnp.dot(xl, wgh_ref[...], preferred_element_type=f32)
                     + jnp.dot(xh, wgl_ref[...], preferred_element_type=f32))


def _inproj(xp, n_p, xs, g1, w_in, after):
    n = n_p + xs.shape[0]
    nbp = n_p // ROW_BLOCK
    main = 3 * A_WIDTH + 4 * M_WIDTH
    w_main = w_in[:, :main].astype(bf16)
    wg = jnp.pad(w_in[:, main:], ((0, 0), (0, LANES - 2 * M_HEADS)))
    wgh, wgl = _split_bf16(wg)
    widths = (512, 512, 512, 1024, 512, 512, LANES)
    row = lambda w: pl.BlockSpec((ROW_BLOCK, w), lambda i: (i, 0))
    full = lambda a: pl.BlockSpec(a.shape, lambda i: (0,) * a.ndim)
    g = g1.reshape(1, D_MODEL)
    return pl.pallas_call(
        functools.partial(_inproj_kernel, nbp=nbp),
        grid=(n // ROW_BLOCK,),
        in_specs=[*_pair_specs(nbp, D_MODEL), full(g), full(w_main), full(wgh), full(wgl),
                  pl.BlockSpec((SUBLANES, P_SLOTS), lambda i: (0, 0))],
        out_specs=[row(w) for w in widths],
        out_shape=[jax.ShapeDtypeStruct((n, w), f32) for w in widths],
        compiler_params=pltpu.CompilerParams(dimension_semantics=("parallel",), vmem_limit_bytes=VMEM_LIMIT),
        name="inproj",
    )(xp, xs, g, w_main, wgh, wgl, after)


def _attn_heads(q, k, v, bias_ref, key_ok):
    outs = []
    for h in range(A_HEADS):
        sl = slice(h * A_DH, (h + 1) * A_DH)
        s = lax.dot_general(q[:, sl], k[:, sl], (((1,), (1,)), ((), ())), preferred_element_type=f32)
        s = s * ATT_SCALE + bias_ref[h]
        if key_ok is not None:
            s = jnp.where(key_ok, s, NEG)
        m = jnp.max(s, axis=-1, keepdims=True)
        p = jnp.exp(s - m)
        l = jnp.sum(p, axis=-1, keepdims=True)
        o = jnp.dot(p.astype(bf16), v[:, sl], preferred_element_type=f32)
        outs.append(o / l)
    return jnp.concatenate(outs, axis=-1)


def _attn_prompt_kernel(q_ref, k0_ref, k1_ref, v0_ref, v1_ref, bias_ref, o_ref, *, t0):
    t = t0 + pl.program_id(1)
    q = q_ref[...].astype(bf16)
    k = jnp.concatenate([k0_ref[...], k1_ref[...]], axis=0).astype(bf16)
    v = jnp.concatenate([v0_ref[...], v1_ref[...]], axis=0).astype(bf16)
    col = lax.broadcasted_iota(i32, (1, ATT_KEYS), 1)
    for s in range(ATT_TILE // ATT_SUB):
        lo = s * ATT_SUB
        key_ok = (t * ATT_TILE + lo + col) >= WINDOW
        o_ref[lo:lo + ATT_SUB, :] = _attn_heads(q[lo:lo + ATT_SUB], k[lo:lo + ATT_KEYS], v[lo:lo + ATT_KEYS],
                                                 bias_ref, key_ok)


def _attn_sample_kernel(q_ref, k_ref, v_ref, bias_ref, o_ref):
    o_ref[...] = _attn_heads(q_ref[...].astype(bf16), k_ref[0].astype(bf16), v_ref[0].astype(bf16), bias_ref, None)


def _rel_bias_table(rel_bias, rows, cols, offset, valid):
    span = rows + cols - 1
    rel = offset + rows - 1 - jnp.arange(span)
    diag = rel_bias[:, jnp.clip(rel, -MAX_REL, MAX_REL) + MAX_REL].astype(f32)
    diag = jnp.pad(diag, ((0, 0), (0, 1)))
    flat = jnp.tile(diag, (1, rows))[:, rows - 1:rows - 1 + rows * span]
    return jnp.where(valid[None], flat.reshape(-1, rows, span)[:, :, :cols], NEG)


def _attn_prompt(q, k, v, rel_bias, seq, s, lo, hi):
    assert s % ATT_TILE == 0 and WINDOW == ATT_TILE and lo % ATT_TILE == 0 and hi % ATT_TILE == 0
    nt, t0, cnt = s // ATT_TILE, lo // ATT_TILE, (hi - lo) // ATT_TILE
    i = jnp.arange(ATT_SUB)[:, None]
    j = jnp.arange(ATT_KEYS)[None, :]
    off = j - (i // CHUNK) * CHUNK
    bias = _rel_bias_table(rel_bias, ATT_SUB, ATT_KEYS, WINDOW, (off >= 0) & (off < WINDOW + CHUNK))
    cur = pl.BlockSpec((ATT_TILE, A_WIDTH), lambda b, t: (seq * nt + t0 + t, 0))
    prev = pl.BlockSpec((ATT_TILE, A_WIDTH), lambda b, t: (seq * nt + jnp.maximum(t0 + t - 1, 0), 0))
    return pl.pallas_call(
        functools.partial(_attn_prompt_kernel, t0=t0),
        grid=(1, cnt),
        in_specs=[cur, prev, cur, prev, cur, pl.BlockSpec(bias.shape, lambda b, t: (0, 0, 0))],
        out_specs=pl.BlockSpec((ATT_TILE, A_WIDTH), lambda b, t: (t, 0)),
        out_shape=jax.ShapeDtypeStruct((hi - lo, A_WIDTH), f32),
        compiler_params=pltpu.CompilerParams(dimension_semantics=("parallel", "parallel"),
                                             vmem_limit_bytes=VMEM_LIMIT),
        name="attn_prompt",
    )(q, k, k, v, v, bias)


def _attn_sample(q, k, v, ck, cv, rel_bias, row0, bsz, t):
    l = ck.shape[1]
    assert row0 % t == 0
    keys = -(-(l + t) // LANES) * LANES
    padk = ((0, 0), (0, keys - l - t), (0, 0))
    kk = jnp.pad(jnp.concatenate([ck, k[row0:].reshape(bsz, t, A_WIDTH)], axis=1), padk)
    vv = jnp.pad(jnp.concatenate([cv, v[row0:].reshape(bsz, t, A_WIDTH)], axis=1), padk)
    j = jnp.arange(keys)[None, :]
    bias = _rel_bias_table(rel_bias, t, keys, l, jnp.broadcast_to(j < l + t, (t, keys)))
    return pl.pallas_call(
        _attn_sample_kernel,
        grid=(bsz,),
        in_specs=[pl.BlockSpec((t, A_WIDTH), lambda b: (row0 // t + b, 0)),
                  pl.BlockSpec((1, keys, A_WIDTH), lambda b: (b, 0, 0)),
                  pl.BlockSpec((1, keys, A_WIDTH), lambda b: (b, 0, 0)),
                  pl.BlockSpec(bias.shape, lambda b: (0, 0, 0))],
        out_specs=pl.BlockSpec((t, A_WIDTH), lambda b: (b, 0)),
        out_shape=jax.ShapeDtypeStruct((bsz * t, A_WIDTH), f32),
        compiler_params=pltpu.CompilerParams(dimension_semantics=("parallel",), vmem_limit_bytes=VMEM_LIMIT),
        name="attn_sample",
    )(q, kk, vv, bias)


def _mlstm_chunk(a, vall, o_in, gate_in, cw_ref, cb_ref, bg_ref, mhg_ref, c_s, n_s, m_s, prev_s):
    lc = a.shape[0]
    ext = jnp.concatenate([prev_s[...], a], axis=0)
    conv = cb_ref[...]
    for j in range(CONV_W):
        lo = SUBLANES - (CONV_W - 1) + j
        conv = conv + cw_ref[j:j + 1, :] * ext[lo:lo + lc]
    prev_s[...] = a[lc - SUBLANES:lc]
    qk = conv * jax.nn.sigmoid(conv)

    z = gate_in + bg_ref[...]
    lane = lax.broadcasted_iota(i32, (lc, LANES), 1)
    row = lax.broadcasted_iota(i32, (lc, LANES), 0)
    logf = jnp.minimum(z, 0.0) - jnp.log1p(jnp.exp(-jnp.abs(z)))
    cum = jnp.where((lane >= M_HEADS) & (lane < 2 * M_HEADS), logf, 0.0)
    shift = 1
    while shift < lc:
        cum = cum + jnp.where(row >= shift, pltpu.roll(cum, shift, axis=0), 0.0)
        shift *= 2
    zc = jnp.where(lane < M_HEADS, z, cum)
    zt = jnp.concatenate([zc, jnp.zeros((LANES - lc, LANES), f32)], axis=0).T[:, :lc]

    ri = lax.broadcasted_iota(i32, (lc, lc), 0)
    ci = lax.broadcasted_iota(i32, (lc, lc), 1)
    causal = ri >= ci
    state = [(m_s[h:h + 1, 0:1], c_s[h], n_s[h:h + 1, :]) for h in range(M_HEADS)]
    hs, new_state = [], []
    for h in range(M_HEADS):
        sl = slice(h * M_DH, (h + 1) * M_DH)
        q = qk[:, sl]
        k = qk[:, M_WIDTH + h * M_DH:M_WIDTH + (h + 1) * M_DH] * (M_DH ** -0.5)
        v = vall[:, sl]
        i_col = zc[:, h:h + 1]
        b_col = zc[:, M_HEADS + h:M_HEADS + h + 1]
        i_row = zt[h:h + 1, :]
        b_row = zt[M_HEADS + h:M_HEADS + h + 1, :]
        m_prev, c_prev, n_prev = state[h]

        dmat = jnp.where(causal, b_col - b_row + i_row, NEG)
        inter = b_col + m_prev
        mt = jnp.maximum(jnp.max(dmat, axis=-1, keepdims=True), inter)
        w_intra = jnp.exp(dmat - mt)
        w_inter = jnp.exp(inter - mt)
        qb, kb, vb = q.astype(bf16), k.astype(bf16), v.astype(bf16)
        s = lax.dot_general(qb, kb, (((1,), (1,)), ((), ())), preferred_element_type=f32) * w_intra
        num = (w_inter * jnp.dot(qb, c_prev.astype(bf16), preferred_element_type=f32)
               + jnp.dot(s.astype(bf16), vb, preferred_element_type=f32))
        den = w_inter * jnp.sum(q * n_prev, axis=-1, keepdims=True) + jnp.sum(s, axis=-1, keepdims=True)
        hh = num / jnp.maximum(jnp.abs(den), jnp.exp(-mt))
        m_new = mt[lc - 1:lc, :]
        b_last = b_col[lc - 1:lc, :]
        w_s = jnp.exp(b_last - b_col + i_col - m_new)
        decay = jnp.exp(b_last + m_prev - m_new)
        kw = k * w_s
        new_state.append((jnp.broadcast_to(m_new, (1, LANES)),
                          decay * c_prev + lax.dot_general(kw.astype(bf16), vb, (((0,), (0,)), ((), ())),
                                                           preferred_element_type=f32),
                          decay * n_prev + jnp.sum(kw, axis=0, keepdims=True)))
        hs.append(hh * lax.rsqrt(jnp.mean(hh * hh, axis=-1, keepdims=True) + EPS))

    for h, (m_new, c_new, n_new) in enumerate(new_state):
        m_s[h:h + 1, :] = m_new
        c_s[h] = c_new
        n_s[h:h + 1, :] = n_new
    return jnp.concatenate(hs, axis=-1) * mhg_ref[...] * jax.nn.sigmoid(o_in)


def _mlstm_kernel(qk_ref, v_ref, o_ref, gate_ref, c0_ref, n0_ref, m0_ref, cbuf_ref,
                  cw_ref, cb_ref, bg_ref, mhg_ref,
                  h_ref, cout_ref, nout_ref, mout_ref,
                  c_s, n_s, m_s, prev_s):
    c = pl.program_id(1)

    @pl.when(c == 0)
    def _():
        c_s[...] = c0_ref[0]
        n_s[...] = n0_ref[0]
        m_s[...] = m0_ref[0]
        prev_s[...] = cbuf_ref[0]

    h_ref[...] = _mlstm_chunk(qk_ref[...], v_ref[...], o_ref[...], gate_ref[...],
                              cw_ref, cb_ref, bg_ref, mhg_ref, c_s, n_s, m_s, prev_s)

    @pl.when(c == pl.num_programs(1) - 1)
    def _():
        cout_ref[0] = c_s[...]
        nout_ref[0] = n_s[...]
        mout_ref[0] = m_s[...]


def _mlstm(mqk, mv, mo, gates, row0, bsz, t, c0, n0, m0, cbuf, conv_w, conv_b, b_gates, mh_g):
    lc = min(CHUNK, t)
    step = lc
    nc = t // step
    assert t % step == 0 and lc % SUBLANES == 0 and row0 % step == 0
    n0p = jnp.pad(n0.astype(f32), ((0, 0), (0, SUBLANES - M_HEADS), (0, 0)))
    m0p = jnp.pad(jnp.broadcast_to(m0.astype(f32)[:, :, None], (bsz, M_HEADS, LANES)),
                  ((0, 0), (0, SUBLANES - M_HEADS), (0, 0)))
    cbp = jnp.pad(cbuf.astype(f32), ((0, 0), (SUBLANES - (CONV_W - 1), 0), (0, 0)))
    bg = jnp.pad(b_gates.astype(f32), (0, LANES - 2 * M_HEADS)).reshape(1, LANES)
    seq = lambda w: pl.BlockSpec((step, w), lambda b, c: (row0 // step + b * nc + c, 0))
    out_seq = pl.BlockSpec((step, M_WIDTH), lambda b, c: (b * nc + c, 0))
    per_b = lambda shp: pl.BlockSpec((1,) + shp, lambda b, c: (b,) + (0,) * len(shp))
    full = lambda a: pl.BlockSpec(a.shape, lambda b, c: (0,) * a.ndim)
    cb = conv_b.reshape(1, -1)
    mhg = mh_g.reshape(1, -1)
    return pl.pallas_call(
        _mlstm_kernel,
        grid=(bsz, nc),
        in_specs=[seq(2 * M_WIDTH), seq(M_WIDTH), seq(M_WIDTH), seq(LANES),
                  per_b((M_HEADS, M_DH, M_DH)), per_b((SUBLANES, M_DH)), per_b((SUBLANES, LANES)),
                  per_b((SUBLANES, 2 * M_WIDTH)),
                  full(conv_w), full(cb), full(bg), full(mhg)],
        out_specs=[out_seq, per_b((M_HEADS, M_DH, M_DH)), per_b((SUBLANES, M_DH)), per_b((SUBLANES, LANES))],
        out_shape=[jax.ShapeDtypeStruct((bsz * t, M_WIDTH), f32),
                   jax.ShapeDtypeStruct((bsz, M_HEADS, M_DH, M_DH), f32),
                   jax.ShapeDtypeStruct((bsz, SUBLANES, M_DH), f32),
                   jax.ShapeDtypeStruct((bsz, SUBLANES, LANES), f32)],
        scratch_shapes=[pltpu.VMEM((M_HEADS, M_DH, M_DH), f32), pltpu.VMEM((SUBLANES, M_DH), f32),
                        pltpu.VMEM((SUBLANES, LANES), f32), pltpu.VMEM((SUBLANES, 2 * M_WIDTH), f32)],
        compiler_params=pltpu.CompilerParams(dimension_semantics=("parallel", "arbitrary"),
                                             vmem_limit_bytes=VMEM_LIMIT),
        name="mlstm",
    )(mqk, mv, mo, gates, c0.astype(f32), n0p, m0p, cbp, conv_w, cb, bg, mhg)


def _topk_rows(s, k, payload=None):
    n = s.shape[0]
    rows = lax.broadcasted_iota(i32, s.shape, 0).astype(f32)
    vals, ids = [], []
    for _ in range(k):
        m = jnp.max(s, axis=0, keepdims=True)
        pos = jnp.min(jnp.where(s == m, rows, float(n)), axis=0, keepdims=True)
        sel = rows == pos
        vals.append(m)
        ids.append(pos if payload is None else jnp.max(jnp.where(sel, payload, -1.0), axis=0, keepdims=True))
        s = jnp.where(sel, -jnp.inf, s)
    return jnp.concatenate(vals, axis=0), jnp.concatenate(ids, axis=0)


def _mid_kernel(xp_ref, xs_ref, attp_ref, atts_ref, hp_ref, hs_ref, wo_ref, g2_ref, wq_ref, keys_ref, after_ref,
                x1_ref, xn_ref, eidx_ref, eidx_t_ref, gate_ref, *, nbp):
    del after_ref
    cat = jnp.concatenate([_pick(nbp, attp_ref, atts_ref), _pick(nbp, hp_ref, hs_ref)], axis=-1).astype(bf16)
    x1 = _pick(nbp, xp_ref, xs_ref) + jnp.dot(cat, wo_ref[...], preferred_element_type=f32)
    x1_ref[...] = x1
    xn = _rms(x1, g2_ref[...])
    xn_ref[...] = xn
    xb = xn.astype(bf16)
    e_rows, g_rows = [], []
    for h in range(P_HEADS):
        q = jnp.dot(xb, wq_ref[:, h * P_DKEY:(h + 1) * P_DKEY], preferred_element_type=f32).astype(bf16)
        half = []
        for p in range(2):
            st = lax.dot_general(keys_ref[h, p], q[:, p * N_KEYS:(p + 1) * N_KEYS], (((1,), (1,)), ((), ())),
                                 preferred_element_type=f32)
            half.append(_topk_rows(st, P_TOPK))
        (v0, i0), (v1, i1) = half
        width = [P_TOPK // (a + 1) for a in range(P_TOPK)]
        fill = -sum(width) % SUBLANES
        cand = jnp.concatenate([v0[a:a + 1] + v1[:width[a]] for a in range(P_TOPK)]
                               + [jnp.full((fill, v0.shape[1]), -jnp.inf, f32)], axis=0)
        cidx = jnp.concatenate([i0[a:a + 1] * float(N_KEYS) + i1[:width[a]] for a in range(P_TOPK)]
                               + [jnp.zeros((fill, v0.shape[1]), f32)], axis=0)
        top_s, eid = _topk_rows(cand, P_TOPK, payload=cidx)
        ex = jnp.exp(top_s - top_s[0:1])
        e_rows.append(eid)
        g_rows.append(ex / jnp.sum(ex, axis=0, keepdims=True))
    eidx_t = jnp.concatenate(e_rows, axis=0)
    eidx_t_ref[...] = eidx_t.astype(i32)
    eidx_ref[...] = eidx_t.T.astype(i32)
    gate_ref[...] = jnp.concatenate(g_rows, axis=0).T


def _mid(xp, xp_row0, xs, xs_row0, attp, atts, hp, hs, n_a, n_b, after, w_out, g2, wq, keys):
    n = n_a + n_b
    nbp = n_a // ROW_BLOCK
    assert n_a % ROW_BLOCK == 0 and n_b % ROW_BLOCK == 0 and xp_row0 % ROW_BLOCK == 0 and xs_row0 % ROW_BLOCK == 0
    row = lambda w: pl.BlockSpec((ROW_BLOCK, w), lambda i: (i, 0))
    full = lambda a: pl.BlockSpec(a.shape, lambda i: (0,) * a.ndim)
    wo = w_out.astype(bf16)
    wqb = wq.astype(bf16)
    kb = keys.astype(bf16)
    g = g2.reshape(1, D_MODEL)
    return pl.pallas_call(
        functools.partial(_mid_kernel, nbp=nbp),
        grid=(n // ROW_BLOCK,),
        in_specs=[*_pair_specs(nbp, D_MODEL, xp_row0 // ROW_BLOCK, xs_row0 // ROW_BLOCK), *_pair_specs(nbp, A_WIDTH),
                  *_pair_specs(nbp, M_WIDTH), full(wo), full(g), full(wqb), full(kb),
                  pl.BlockSpec((SUBLANES, P_SLOTS), lambda i: (0, 0))],
        out_specs=[row(D_MODEL), row(D_MODEL), row(P_SLOTS), pl.BlockSpec((P_SLOTS, ROW_BLOCK), lambda i: (0, i)),
                   row(P_SLOTS)],
        out_shape=[jax.ShapeDtypeStruct((n, D_MODEL), f32), jax.ShapeDtypeStruct((n, D_MODEL), f32),
                   jax.ShapeDtypeStruct((n, P_SLOTS), i32), jax.ShapeDtypeStruct((P_SLOTS, n), i32),
                   jax.ShapeDtypeStruct((n, P_SLOTS), f32)],
        compiler_params=pltpu.CompilerParams(dimension_semantics=("parallel",), vmem_limit_bytes=VMEM_LIMIT),
        name="outproj_retrieve",
    )(xp, xs, attp, atts, hp, hs, wo, g, wqb, kb, after)


def _gather_rows(eidx_refs, tab_ref, g_ref, t):
    for i in range(SUBLANES):
        for k, eidx_ref in enumerate(eidx_refs):
            r = k * SUBLANES + i
            g_ref[pl.ds(r * ROW_TILES, ROW_TILES), :] = tab_ref[eidx_ref[i, t]]


def _tile_rows(t, rows=SUBLANES):
    return pl.ds(pl.multiple_of(t * rows, rows), rows)


def _pipelined_tokens(nt, eidx_refs, tab_ref, g0_s, g1_s, compute):
    bufs = (g0_s, g1_s)
    _gather_rows(eidx_refs, tab_ref, g0_s, 0)

    def body(j, carry):
        for u in range(TOKEN_UNROLL):
            t = TOKEN_UNROLL * j + u
            _gather_rows(eidx_refs, tab_ref, bufs[(u + 1) % 2], jnp.minimum(t + 1, nt - 1))
            compute(t, bufs[u % 2])
        return carry

    lax.fori_loop(0, nt // TOKEN_UNROLL, body, 0)


def _gelu_tanh(x):
    return 0.5 * x * (1.0 + jnp.tanh(math.sqrt(2.0 / math.pi) * (x + 0.044715 * (x * x * x))))


def _peer_u_kernel(*refs):
    eidx_refs, (xn_ref, g_ref, tab_ref, w_ref, xl_s, r_s, g0_s, g1_s) = refs[:INDEX_GROUPS], refs[INDEX_GROUPS:]
    nt = xn_ref.shape[0]
    xn = xn_ref[...]
    xh = xn.astype(bf16).astype(f32)
    xl = xn - xh
    for k in range(ROW_TILES):
        xl_s[pl.ds(k, nt, stride=2 * SUBLANES), :] = xh[:, k * LANES:(k + 1) * LANES]
        xl_s[pl.ds(SUBLANES + k, nt, stride=2 * SUBLANES), :] = xl[:, k * LANES:(k + 1) * LANES]
    diag = (lax.broadcasted_iota(i32, (SUBLANES, G_ROWS), 1) % ROW_TILES
            == lax.broadcasted_iota(i32, (SUBLANES, G_ROWS), 0))

    def compute(t, g_s):
        lhs = xl_s[_tile_rows(t, 2 * SUBLANES), :].astype(bf16)
        out = lax.dot_general(lhs, g_s[...], (((1,), (1,)), ((), ())), preferred_element_type=f32)
        part = jnp.where(diag, out[:SUBLANES] + out[SUBLANES:], 0.0)
        for c in range(ROW_TILES):
            r_s[c, _tile_rows(t), :] = part[:, c * LANES:(c + 1) * LANES]

    _pipelined_tokens(nt, eidx_refs, tab_ref, g0_s, g1_s, compute)
    cols = []
    for c in range(ROW_TILES):
        acc = r_s[c, pl.ds(0, nt, stride=SUBLANES), :]
        for k in range(1, SUBLANES):
            acc = acc + r_s[c, pl.ds(k, nt, stride=SUBLANES), :]
        cols.append(acc)
    s = jnp.concatenate(cols, axis=-1)
    fold = (lax.broadcasted_iota(i32, (G_ROWS, P_SLOTS), 0) // ROW_TILES
            == lax.broadcasted_iota(i32, (G_ROWS, P_SLOTS), 1)).astype(bf16)
    sh, sl = _split_bf16(s)
    act = jnp.dot(sh, fold, preferred_element_type=f32) + jnp.dot(sl, fold, preferred_element_type=f32)
    w_ref[...] = g_ref[...] * _gelu_tanh(act)


def _peer_v_kernel(*refs):
    eidx_refs, (w_ref, tab_ref, peer_ref, wl_s, o_s, g0_s, g1_s) = refs[:INDEX_GROUPS], refs[INDEX_GROUPS:]
    nt = w_ref.shape[0]
    spread = (lax.broadcasted_iota(i32, (P_SLOTS, G_ROWS), 1) // ROW_TILES
              == lax.broadcasted_iota(i32, (P_SLOTS, G_ROWS), 0)).astype(bf16)
    wexp = jnp.dot(w_ref[...].astype(bf16), spread, preferred_element_type=f32)
    lane = lax.broadcasted_iota(i32, (nt, LANES), 1)
    for c in range(ROW_TILES):
        wc = wexp[:, c * LANES:(c + 1) * LANES]
        for k in range(SUBLANES):
            wl_s[c, pl.ds(k, nt, stride=SUBLANES), :] = jnp.where(lane % ROW_TILES == k, wc, 0.0)

    def compute(t, g_s):
        lhs = jnp.concatenate([wl_s[c, _tile_rows(t), :] for c in range(ROW_TILES)], axis=-1).astype(bf16)
        o_s[_tile_rows(t), :] = jnp.dot(lhs, g_s[...], preferred_element_type=f32)

    _pipelined_tokens(nt, eidx_refs, tab_ref, g0_s, g1_s, compute)
    for k in range(ROW_TILES):
        peer_ref[:, k * LANES:(k + 1) * LANES] = o_s[pl.ds(k, nt, stride=SUBLANES), :]


def _final_kernel(*refs, starts, nbp):
    k = len(starts)
    peer_refs, x_refs, (gf_ref, yp_ref, ys_ref) = refs[:k], refs[k:2 * k], refs[2 * k:]
    resid = x_refs[0][...] + peer_refs[0][...]
    for start, p_ref, x_ref in zip(starts[1:], peer_refs[1:], x_refs[1:]):
        resid = jnp.where(pl.program_id(0) >= start, x_ref[...] + p_ref[...], resid)
    y = _rms(resid, gf_ref[...])

    @pl.when(pl.program_id(0) < nbp)
    def _():
        yp_ref[...] = y

    @pl.when(pl.program_id(0) >= nbp)
    def _():
        ys_ref[...] = y


def _final(peers, x1s, gf, n_p):
    counts = [x.shape[0] // ROW_BLOCK for x in x1s]
    starts = [sum(counts[:k]) for k in range(len(counts))]
    n, nbp = sum(counts) * ROW_BLOCK, n_p // ROW_BLOCK
    g = gf.reshape(1, D_MODEL)
    seg = [pl.BlockSpec((ROW_BLOCK, D_MODEL), lambda i, s=s, c=c: (jnp.clip(i - s, 0, c - 1), 0))
           for s, c in zip(starts, counts)]
    return pl.pallas_call(
        functools.partial(_final_kernel, starts=tuple(starts), nbp=nbp),
        grid=(n // ROW_BLOCK,),
        in_specs=[*seg, *seg, pl.BlockSpec(g.shape, lambda i: (0, 0))],
        out_specs=list(_pair_specs(nbp, D_MODEL)),
        out_shape=[jax.ShapeDtypeStruct((n_p, D_MODEL), f32), jax.ShapeDtypeStruct((n - n_p, D_MODEL), f32)],
        compiler_params=pltpu.CompilerParams(dimension_semantics=("arbitrary",), vmem_limit_bytes=VMEM_LIMIT),
        name="final_norm",
    )(*peers, *x1s, g)


def _peer_specs():
    row = lambda w: pl.BlockSpec((ROW_BLOCK, w), lambda i: (i, 0))
    idx = [pl.BlockSpec((SUBLANES, ROW_BLOCK), lambda i, k=k: (k, i), memory_space=pltpu.SMEM)
           for k in range(INDEX_GROUPS)]
    tab = pl.BlockSpec(memory_space=pltpu.VMEM)
    gscr = pltpu.VMEM((G_ROWS, LANES), bf16)
    params = pltpu.CompilerParams(dimension_semantics=("arbitrary",), vmem_limit_bytes=VMEM_LIMIT)
    return row, idx, tab, gscr, params


def _expert_table(tab):
    return tab.astype(bf16).reshape(tab.shape[0], ROW_TILES, LANES)


def _peer_u(eidx_t, xn, g, utab, n):
    row, idx, tab, gscr, params = _peer_specs()
    return pl.pallas_call(
        _peer_u_kernel,
        grid=(n // ROW_BLOCK,),
        in_specs=[*idx, row(D_MODEL), row(P_SLOTS), tab],
        out_specs=row(P_SLOTS),
        out_shape=jax.ShapeDtypeStruct((n, P_SLOTS), f32),
        scratch_shapes=[pltpu.VMEM((ROW_BLOCK * 2 * SUBLANES, LANES), f32),
                        pltpu.VMEM((ROW_TILES, ROW_BLOCK * SUBLANES, LANES), f32), gscr, gscr],
        compiler_params=params,
        name="peer_u",
    )(*[eidx_t] * INDEX_GROUPS, xn, g, utab)


def _peer_v(eidx_t, w, vtab):
    n = w.shape[0]
    row, idx, tab, gscr, params = _peer_specs()
    return pl.pallas_call(
        _peer_v_kernel,
        grid=(n // ROW_BLOCK,),
        in_specs=[*idx, row(P_SLOTS), tab],
        out_specs=row(D_MODEL),
        out_shape=jax.ShapeDtypeStruct((n, D_MODEL), f32),
        scratch_shapes=[pltpu.VMEM((ROW_TILES, ROW_BLOCK * SUBLANES, LANES), f32),
                        pltpu.VMEM((ROW_BLOCK * SUBLANES, LANES), f32), gscr, gscr],
        compiler_params=params,
        name="peer_v",
    )(*[eidx_t] * INDEX_GROUPS, w, vtab)


def _sc_table_kernel(t_ref, o_ref):
    t = t_ref[...]
    o_ref[...] = pltpu.pack_elementwise([t[:, :SC_WORDS], t[:, SC_WORDS:]], packed_dtype=bf16)


def _sc_table(tab):
    e = tab.shape[0]
    return pl.pallas_call(
        _sc_table_kernel,
        grid=(e // ROW_BLOCK,),
        in_specs=[pl.BlockSpec((ROW_BLOCK, D_MODEL), lambda i: (i, 0))],
        out_specs=pl.BlockSpec((ROW_BLOCK, SC_WORDS), lambda i: (i, 0)),
        out_shape=jax.ShapeDtypeStruct((e, SC_WORDS), jnp.uint32),
        compiler_params=pltpu.CompilerParams(dimension_semantics=("parallel",), vmem_limit_bytes=VMEM_LIMIT),
        name="sc_table",
    )(tab)


def _sc_unpack(words):
    return plsc.unpack(plsc.bitcast(words, bf16), format=plsc.PackFormat.INTERLEAVED)


def _sc_gelu_tanh(x):
    z = math.sqrt(2.0 / math.pi) * (x + 0.044715 * (x * x * x))
    return 0.5 * x * (2.0 - 2.0 / (jnp.exp(2.0 * z) + 1.0))


def _sc_peer(utab32, vtab32, eidx, xn, gate, after):
    n_sc = eidx.shape[0]
    per = n_sc // SC_WORKERS
    units = 4 * per
    assert n_sc % (SC_WORKERS * SUBLANES) == 0

    def body(u_hbm, v_hbm, eidx_hbm, x_hbm, g_hbm, after_hbm, out_hbm,
             idx_v, x_v, g_v, rows_v, acc_v, w_v, w16_v, out_v, row_sems, tok_sems, out_sem):
        del after_hbm
        base = (lax.axis_index("s") * 2 + lax.axis_index("c")) * per
        lanes = lax.broadcasted_iota(i32, (SC_LANES,), 0)

        def token_copies(tok):
            tslot = tok % 2
            return (pltpu.make_async_copy(eidx_hbm.at[base + tok], idx_v.at[tslot], tok_sems.at[0]),
                    pltpu.make_async_copy(x_hbm.at[base + tok], x_v.at[tslot], tok_sems.at[1]),
                    pltpu.make_async_copy(g_hbm.at[base + tok], g_v.at[tslot], tok_sems.at[2]))

        def store_out(tok):
            return pltpu.make_async_copy(out_v, out_hbm.at[base + tok], out_sem)

        def gather(tab_hbm, g):
            tok, k = g // 4, g % 4
            return pltpu.make_async_copy(tab_hbm.at[idx_v.at[tok % 2, pl.ds((k % 2) * SC_UNIT_ROWS, SC_UNIT_ROWS)]],
                                         rows_v.at[k % 2], row_sems.at[k % 2])

        def start(g):
            @pl.when(g % 4 < 2)
            def _():
                gather(u_hbm, g).start()

            @pl.when(g % 4 >= 2)
            def _():
                gather(v_hbm, g).start()

        def compute_u(tslot, half, rows):
            @pl.loop(0, SC_UNIT_ROWS // SC_ROW_GROUP)
            def _(rg):
                slot0 = half * SC_UNIT_ROWS + rg * SC_ROW_GROUP

                @pl.loop(0, SC_CHUNKS // SC_CHUNK_GROUP)
                def _(cg):
                    keep = jnp.where(cg > 0, 1.0, 0.0).astype(f32)
                    accs = [acc_v[pl.ds((slot0 + r) * SC_LANES, SC_LANES)] * keep for r in range(SC_ROW_GROUP)]
                    for c in range(SC_CHUNK_GROUP):
                        ch = cg * SC_CHUNK_GROUP + c
                        xa = x_v[tslot, pl.ds(ch * SC_LANES, SC_LANES)]
                        xb = x_v[tslot, pl.ds(SC_WORDS + ch * SC_LANES, SC_LANES)]
                        for r in range(SC_ROW_GROUP):
                            a, b = _sc_unpack(rows[rg * SC_ROW_GROUP + r, pl.ds(ch * SC_LANES, SC_LANES)])
                            accs[r] = accs[r] + a * xa + b * xb
                    for r in range(SC_ROW_GROUP):
                        acc_v[pl.ds((slot0 + r) * SC_LANES, SC_LANES)] = accs[r]

        def gate_weights(tslot):
            @pl.loop(0, P_SLOTS // SC_LANES)
            def _(sg):
                first = sg * SC_LANES * SC_LANES
                act = jnp.zeros((SC_LANES,), f32)
                for lane in range(SC_LANES):
                    act = act + plsc.load_gather(acc_v, [first + lanes * SC_LANES + lane])
                w_v[pl.ds(sg * SC_LANES, SC_LANES)] = g_v[tslot, pl.ds(sg * SC_LANES, SC_LANES)] * _sc_gelu_tanh(act)

            @pl.loop(0, P_SLOTS // SC_LANES)
            def _(sg):
                for r in range(SC_LANES):
                    w16_v[pl.ds((sg * SC_LANES + r) * SC_LANES, SC_LANES)] = plsc.load_gather(
                        w_v, [jnp.zeros((SC_LANES,), i32) + (sg * SC_LANES + r)])

        def compute_v(half, rows):
            @pl.loop(0, SC_CHUNKS // SC_CHUNK_GROUP)
            def _(cg):
                first = cg * SC_CHUNK_GROUP

                @pl.loop(0, SC_UNIT_ROWS // SC_ROW_GROUP)
                def _(rg):
                    keep = jnp.where(jnp.logical_or(half == 1, rg > 0), 1.0, 0.0).astype(f32)
                    spots = [pl.ds((j % 2) * SC_WORDS + (first + j // 2) * SC_LANES, SC_LANES)
                             for j in range(2 * SC_CHUNK_GROUP)]
                    accs = [out_v[spot] * keep for spot in spots]
                    for r in range(SC_ROW_GROUP):
                        row = rg * SC_ROW_GROUP + r
                        wv = w16_v[pl.ds((half * SC_UNIT_ROWS + row) * SC_LANES, SC_LANES)]
                        for c in range(SC_CHUNK_GROUP):
                            a, b = _sc_unpack(rows[row, pl.ds((first + c) * SC_LANES, SC_LANES)])
                            accs[2 * c] = accs[2 * c] + a * wv
                            accs[2 * c + 1] = accs[2 * c + 1] + b * wv
                    for spot, acc in zip(spots, accs):
                        out_v[spot] = acc

        for cp in token_copies(0):
            cp.start()
        for cp in token_copies(0):
            cp.wait()
        start(0)

        @pl.loop(0, units)
        def _(g):
            tok, k = g // 4, g % 4

            @pl.when(jnp.logical_and(k == 0, tok + 1 < per))
            def _():
                for cp in token_copies(tok + 1):
                    cp.start()

            @pl.when(jnp.logical_and(k == 3, tok + 1 < per))
            def _():
                for cp in token_copies(tok + 1):
                    cp.wait()

            @pl.when(g + 1 < units)
            def _():
                start(g + 1)

            gather(u_hbm, g).wait()
            rows = rows_v.at[k % 2]

            @pl.when(k < 2)
            def _():
                compute_u(tok % 2, k, rows)

            @pl.when(k == 1)
            def _():
                gate_weights(tok % 2)

            @pl.when(jnp.logical_and(k == 2, tok > 0))
            def _():
                store_out(tok - 1).wait()

            @pl.when(k >= 2)
            def _():
                compute_v(k - 2, rows)

            @pl.when(k == 3)
            def _():
                store_out(tok).start()

        store_out(per - 1).wait()

    return pl.kernel(
        body, mesh=plsc.VectorSubcoreMesh(core_axis_name="c", subcore_axis_name="s"),
        out_type=jax.ShapeDtypeStruct((n_sc, D_MODEL), f32),
        scratch_types=[pltpu.VMEM((2, P_SLOTS), i32), pltpu.VMEM((2, D_MODEL), f32), pltpu.VMEM((2, P_SLOTS), f32),
                       pltpu.VMEM((2, SC_UNIT_ROWS, SC_WORDS), jnp.uint32), pltpu.VMEM((P_SLOTS * SC_LANES,), f32),
                       pltpu.VMEM((P_SLOTS,), f32), pltpu.VMEM((P_SLOTS * SC_LANES,), f32), pltpu.VMEM((D_MODEL,), f32),
                       pltpu.SemaphoreType.DMA((2,)), pltpu.SemaphoreType.DMA((3,)), pltpu.SemaphoreType.DMA],
        compiler_params=pltpu.CompilerParams(needs_layout_passes=False),
        name="sc_peer",
    )(utab32, vtab32, eidx, xn, gate, after)


def kernel(x_prompt, x_sample, cache_k, cache_v, state_C, state_n, state_m, state_conv, norm1_g, w_in, b_gates, rel_bias, conv_w, conv_b, mh_norm_g, w_out, norm2_g, peer_wq, peer_keys, peer_u, peer_v, final_g):
    bp, sp, d = x_prompt.shape
    bs, ts, _ = x_sample.shape
    n_p, n_s = bp * sp, bs * ts
    n = n_p + n_s
    assert n_p % ROW_BLOCK == 0 and n_s % ROW_BLOCK == 0 and d == D_MODEL
    depth = w_in.shape[0]
    assert depth == 1, "the final norm is fused into the last layer's PEER pass"
    l = 0
    xp, xs = x_prompt.reshape(n_p, d), x_sample.reshape(n_s, d)

    zeros = lambda *shp: jnp.zeros(shp, f32)
    mparams = (conv_w[l], conv_b[l], b_gates[l], mh_norm_g[l])
    retrieval = (w_out[l], norm2_g[l], peer_wq[l], peer_keys[l])

    def mixers(proj, seq, lo, hi, state):
        aq, ak, av, mqk, mv, mo, gates = proj
        att = _attn_prompt(aq, ak, av, rel_bias[l], seq, sp, lo, hi)
        h, c, nn, mm = _mlstm(mqk, mv, mo, gates, seq * sp + lo, 1, hi - lo, *state, *mparams)
        conv_rows = mqk[seq * sp + hi - (CONV_W - 1):seq * sp + hi][None]
        return att, h, (c, nn[:, :M_HEADS], mm[:, :M_HEADS, 0], conv_rows)

    fresh = (zeros(1, M_HEADS, M_DH, M_DH), zeros(1, M_HEADS, M_DH), zeros(1, M_HEADS), zeros(1, CONV_W - 1, 2 * M_WIDTH))

    states = [fresh] * bp

    def stage(proj, lo, hi, after, tail=None):
        parts = []
        for seq in range(bp):
            a, b = max(lo, seq * sp), min(hi, (seq + 1) * sp)
            if a < b:
                att, h, states[seq] = mixers(proj, seq, a - seq * sp, b - seq * sp, states[seq])
                parts.append((att, h, b - a, xp, a))
        parts += [tail] if tail is not None else []
        assert 1 <= len(parts) <= 2, "the retrieval kernel reads at most two row sources"
        (att0, h0, n0, x0, r0), (att1, h1, n1, x1, r1) = parts[0], parts[-1]
        n1, r1 = (n1, r1) if len(parts) == 2 else (0, 0)
        return _mid(x0, r0, x1, r1, att0, att1, h0, h1, n0, n1, after, *retrieval)

    sc_tables = _sc_table(peer_u[l]), _sc_table(peer_v[l])
    peers, x1s, lo, after = [], [], 0, jnp.zeros((SUBLANES, P_SLOTS), i32)
    proj = _inproj(xp, SC_STAGE_ENDS[0], xs, norm1_g[l], w_in[l], after)
    for hi in SC_STAGE_ENDS:
        x1, xn, eidx, _, gate = stage(proj, lo, hi, after)
        peers.append(_sc_peer(*sc_tables, eidx, xn, gate, peers[-1] if peers else zeros(SUBLANES, D_MODEL)))
        x1s.append(x1)
        if lo == 0:
            proj = _inproj(xp, n_p, xs, norm1_g[l], w_in[l], eidx)
        lo, after = hi, eidx
    aq, ak, av, mqk, mv, mo, gates = proj

    lcache = cache_k.shape[2]
    att_s = _attn_sample(aq, ak, av, cache_k[l].reshape(bs, lcache, A_WIDTH),
                         cache_v[l].reshape(bs, lcache, A_WIDTH), rel_bias[l], n_p, bs, ts)
    h_s, c_s, nn_s, mm_s = _mlstm(mqk, mv, mo, gates, n_p, bs, ts, state_C[l], state_n[l], state_m[l],
                                  state_conv[l], *mparams)
    x1, xn, _, eidx_t, gate = stage(proj, lo, n_p, after, tail=(att_s, h_s, n_s, xs, 0))
    w = _peer_u(eidx_t, xn, gate, _expert_table(peer_u[l]), n - lo)
    peers.append(_peer_v(eidx_t, w, _expert_table(peer_v[l])))
    x1s.append(x1)
    y_p, y_s = _final(peers, x1s, final_g, n_p)
    c_p, nn_p, mm_p = (jnp.concatenate(per_seq, axis=0) for per_seq in zip(*(st[:3] for st in states)))

    def tail(a, row0, bsz, t, keep):
        return jnp.stack([a[row0 + (b + 1) * t - keep:row0 + (b + 1) * t] for b in range(bsz)])

    keep = min(WINDOW, sp)
    heads = lambda a: a.reshape(a.shape[0], a.shape[1], A_HEADS, A_DH)
    ctail = CONV_W - 1
    conv_tail = lambda buf, a, row0, bsz, t: jnp.concatenate([buf.astype(a.dtype), tail(a, row0, bsz, t, min(ctail, t))],
                                                             axis=1)[:, -ctail:]
    st = lambda a: a[None]
    return (y_p.reshape(bp, sp, d), y_s.reshape(bs, ts, d),
            st(heads(tail(ak, 0, bp, sp, keep))), st(heads(tail(av, 0, bp, sp, keep))),
            st(c_p), st(nn_p), st(mm_p),
            st(conv_tail(zeros(bp, ctail, 2 * M_WIDTH), mqk, 0, bp, sp)),
            st(heads(ak[n_p:].reshape(bs, ts, A_WIDTH))), st(heads(av[n_p:].reshape(bs, ts, A_WIDTH))),
            st(c_s), st(nn_s[:, :M_HEADS]), st(mm_s[:, :M_HEADS, 0]),
            st(conv_tail(state_conv[l], mqk, n_p, bs, ts)))
```

```python
import functools
import math

import jax
import jax.numpy as jnp
from jax import lax
from jax.experimental import pallas as pl
from jax.experimental.pallas import tpu as pltpu
from jax.experimental.pallas import tpu_sc as plsc

f32 = jnp.float32
bf16 = jnp.bfloat16
i32 = jnp.int32

D_MODEL = 1024
CHUNK = 64
A_HEADS = 8
A_DH = 64
A_WIDTH = A_HEADS * A_DH
BAND_CHUNKS = 8
WINDOW = BAND_CHUNKS * CHUNK
MAX_REL = 128
ATT_SCALE = A_DH ** -0.5
M_HEADS = 4
M_DH = 128
M_WIDTH = M_HEADS * M_DH
CONV_W = 4
P_HEADS = 8
P_DKEY = 256
N_KEYS = 128
P_TOPK = 16
P_SLOTS = P_HEADS * P_TOPK
EPS = 1e-6
NEG = -1e30

LANES = 128
SUBLANES = 8
ROW_BLOCK = 256
ATT_TILE = 512
ATT_SUB = 128
ATT_KEYS = ATT_SUB + WINDOW
ROW_TILES = D_MODEL // LANES
G_ROWS = P_SLOTS * ROW_TILES
TOKEN_UNROLL = 8
INDEX_GROUPS = P_SLOTS // SUBLANES
VMEM_LIMIT = 56 * 1024 * 1024

SC_WORKERS = 32
SC_LANES = 16
SC_UNIT_ROWS = P_SLOTS // 2
SC_ROW_GROUP = 16
SC_WORDS = D_MODEL // 2
SC_CHUNKS = SC_WORDS // SC_LANES
SC_CHUNK_GROUP = 8
SC_STAGE_ENDS = (2048, 6144, 20480)


def _rms(x, g):
    return x * lax.rsqrt(jnp.mean(x * x, axis=-1, keepdims=True) + EPS) * g


def _split_bf16(x):
    hi = x.astype(bf16)
    lo = (x - hi.astype(f32)).astype(bf16)
    return hi, lo


def _pair_specs(nbp, width, first=0, second=0):
    return (pl.BlockSpec((ROW_BLOCK, width), lambda i: (first + jnp.minimum(i, nbp - 1), 0)),
            pl.BlockSpec((ROW_BLOCK, width), lambda i: (second + jnp.maximum(i - nbp, 0), 0)))


def _pick(nbp, p_ref, s_ref):
    return jnp.where(pl.program_id(0) < nbp, p_ref[...], s_ref[...])


def _inproj_kernel(xp_ref, xs_ref, g_ref, w_ref, wgh_ref, wgl_ref, after_ref,
                   aq_ref, ak_ref, av_ref, mqk_ref, mv_ref, mo_ref, gate_ref, *, nbp):
    del after_ref
    xn = _rms(_pick(nbp, xp_ref, xs_ref), g_ref[...])
    xh, xl = _split_bf16(xn)

    def proj(lo, hi):
        return jnp.dot(xh, w_ref[:, lo:hi], preferred_element_type=f32)

    aq_ref[...] = proj(0, 512)
    ak_ref[...] = proj(512, 1024)
    av_ref[...] = proj(1024, 1536)
    mqk_ref[...] = proj(1536, 2560)
    mv_ref[...] = proj(2560, 3072)
    mo_ref[...] = proj(3072, 3584)
    gate_ref[...] = (jnp.dot(xh, wgh_ref[...], preferred_element_type=f32)
                     + jnp.dot(xl, wgh_ref[...], preferred_element_type=f32)
                     + jnp.dot(xh, wgl_ref[...], preferred_element_type=f32))


def _inproj(xp, n_p, xs, g1, w_in, after):
    n = n_p + xs.shape[0]
    nbp = n_p // ROW_BLOCK
    main = 3 * A_WIDTH + 4 * M_WIDTH
    w_main = w_in[:, :main].astype(bf16)
    wg = jnp.pad(w_in[:, main:], ((0, 0), (0, LANES - 2 * M_HEADS)))
    wgh, wgl = _split_bf16(wg)
    widths = (512, 512, 512, 1024, 512, 512, LANES)
    row = lambda w: pl.BlockSpec((ROW_BLOCK, w), lambda i: (i, 0))
    full = lambda a: pl.BlockSpec(a.shape, lambda i: (0,) * a.ndim)
    g = g1.reshape(1, D_MODEL)
    return pl.pallas_call(
        functools.partial(_inproj_kernel, nbp=nbp),
        grid=(n // ROW_BLOCK,),
        in_specs=[*_pair_specs(nbp, D_MODEL), full(g), full(w_main), full(wgh), full(wgl),
                  pl.BlockSpec((SUBLANES, P_SLOTS), lambda i: (0, 0))],
        out_specs=[row(w) for w in widths],
        out_shape=[jax.ShapeDtypeStruct((n, w), f32) for w in widths],
        compiler_params=pltpu.CompilerParams(dimension_semantics=("parallel",), vmem_limit_bytes=VMEM_LIMIT),
        name="inproj",
    )(xp, xs, g, w_main, wgh, wgl, after)


def _attn_heads(q, k, v, bias_ref, key_ok):
    outs = []
    for h in range(A_HEADS):
        sl = slice(h * A_DH, (h + 1) * A_DH)
        s = lax.dot_general(q[:, sl], k[:, sl], (((1,), (1,)), ((), ())), preferred_element_type=f32)
        s = s * ATT_SCALE + bias_ref[h]
        if key_ok is not None:
            s = jnp.where(key_ok, s, NEG)
        m = jnp.max(s, axis=-1, keepdims=True)
        p = jnp.exp(s - m)
        l = jnp.sum(p, axis=-1, keepdims=True)
        o = jnp.dot(p.astype(bf16), v[:, sl], preferred_element_type=f32)
        outs.append(o / l)
    return jnp.concatenate(outs, axis=-1)


def _attn_prompt_kernel(q_ref, k0_ref, k1_ref, v0_ref, v1_ref, bias_ref, o_ref, *, t0):
    t = t0 + pl.program_id(1)
    q = q_ref[...].astype(bf16)
    k = jnp.concatenate([k0_ref[...], k1_ref[...]], axis=0).astype(bf16)
    v = jnp.concatenate([v0_ref[...], v1_ref[...]], axis=0).astype(bf16)
    col = lax.broadcasted_iota(i32, (1, ATT_KEYS), 1)
    for s in range(ATT_TILE // ATT_SUB):
        lo = s * ATT_SUB
        key_ok = (t * ATT_TILE + lo + col) >= WINDOW
        o_ref[lo:lo + ATT_SUB, :] = _attn_heads(q[lo:lo + ATT_SUB], k[lo:lo + ATT_KEYS], v[lo:lo + ATT_KEYS],
                                                 bias_ref, key_ok)


def _attn_sample_kernel(q_ref, k_ref, v_ref, bias_ref, o_ref):
    o_ref[...] = _attn_heads(q_ref[...].astype(bf16), k_ref[0].astype(bf16), v_ref[0].astype(bf16), bias_ref, None)


def _rel_bias_table(rel_bias, rows, cols, offset, valid):
    span = rows + cols - 1
    rel = offset + rows - 1 - jnp.arange(span)
    diag = rel_bias[:, jnp.clip(rel, -MAX_REL, MAX_REL) + MAX_REL].astype(f32)
    diag = jnp.pad(diag, ((0, 0), (0, 1)))
    flat = jnp.tile(diag, (1, rows))[:, rows - 1:rows - 1 + rows * span]
    return jnp.where(valid[None], flat.reshape(-1, rows, span)[:, :, :cols], NEG)


def _attn_prompt(q, k, v, rel_bias, seq, s, lo, hi):
    assert s % ATT_TILE == 0 and WINDOW == ATT_TILE and lo % ATT_TILE == 0 and hi % ATT_TILE == 0
    nt, t0, cnt = s // ATT_TILE, lo // ATT_TILE, (hi - lo) // ATT_TILE
    i = jnp.arange(ATT_SUB)[:, None]
    j = jnp.arange(ATT_KEYS)[None, :]
    off = j - (i // CHUNK) * CHUNK
    bias = _rel_bias_table(rel_bias, ATT_SUB, ATT_KEYS, WINDOW, (off >= 0) & (off < WINDOW + CHUNK))
    cur = pl.BlockSpec((ATT_TILE, A_WIDTH), lambda b, t: (seq * nt + t0 + t, 0))
    prev = pl.BlockSpec((ATT_TILE, A_WIDTH), lambda b, t: (seq * nt + jnp.maximum(t0 + t - 1, 0), 0))
    return pl.pallas_call(
        functools.partial(_attn_prompt_kernel, t0=t0),
        grid=(1, cnt),
        in_specs=[cur, prev, cur, prev, cur, pl.BlockSpec(bias.shape, lambda b, t: (0, 0, 0))],
        out_specs=pl.BlockSpec((ATT_TILE, A_WIDTH), lambda b, t: (t, 0)),
        out_shape=jax.ShapeDtypeStruct((hi - lo, A_WIDTH), f32),
        compiler_params=pltpu.CompilerParams(dimension_semantics=("parallel", "parallel"),
                                             vmem_limit_bytes=VMEM_LIMIT),
        name="attn_prompt",
    )(q, k, k, v, v, bias)


def _attn_sample(q, k, v, ck, cv, rel_bias, row0, bsz, t):
    l = ck.shape[1]
    assert row0 % t == 0
    keys = -(-(l + t) // LANES) * LANES
    padk = ((0, 0), (0, keys - l - t), (0, 0))
    kk = jnp.pad(jnp.concatenate([ck, k[row0:].reshape(bsz, t, A_WIDTH)], axis=1), padk)
    vv = jnp.pad(jnp.concatenate([cv, v[row0:].reshape(bsz, t, A_WIDTH)], axis=1), padk)
    j = jnp.arange(keys)[None, :]
    bias = _rel_bias_table(rel_bias, t, keys, l, jnp.broadcast_to(j < l + t, (t, keys)))
    return pl.pallas_call(
        _attn_sample_kernel,
        grid=(bsz,),
        in_specs=[pl.BlockSpec((t, A_WIDTH), lambda b: (row0 // t + b, 0)),
                  pl.BlockSpec((1, keys, A_WIDTH), lambda b: (b, 0, 0)),
                  pl.BlockSpec((1, keys, A_WIDTH), lambda b: (b, 0, 0)),
                  pl.BlockSpec(bias.shape, lambda b: (0, 0, 0))],
        out_specs=pl.BlockSpec((t, A_WIDTH), lambda b: (b, 0)),
        out_shape=jax.ShapeDtypeStruct((bsz * t, A_WIDTH), f32),
        compiler_params=pltpu.CompilerParams(dimension_semantics=("parallel",), vmem_limit_bytes=VMEM_LIMIT),
        name="attn_sample",
    )(q, kk, vv, bias)


def _mlstm_chunk(a, vall, o_in, gate_in, cw_ref, cb_ref, bg_ref, mhg_ref, c_s, n_s, m_s, prev_s):
    lc = a.shape[0]
    ext = jnp.concatenate([prev_s[...], a], axis=0)
    conv = cb_ref[...]
    for j in range(CONV_W):
        lo = SUBLANES - (CONV_W - 1) + j
        conv = conv + cw_ref[j:j + 1, :] * ext[lo:lo + lc]
    prev_s[...] = a[lc - SUBLANES:lc]
    qk = conv * jax.nn.sigmoid(conv)

    z = gate_in + bg_ref[...]
    lane = lax.broadcasted_iota(i32, (lc, LANES), 1)
    row = lax.broadcasted_iota(i32, (lc, LANES), 0)
    logf = jnp.minimum(z, 0.0) - jnp.log1p(jnp.exp(-jnp.abs(z)))
    cum = jnp.where((lane >= M_HEADS) & (lane < 2 * M_HEADS), logf, 0.0)
    shift = 1
    while shift < lc:
        cum = cum + jnp.where(row >= shift, pltpu.roll(cum, shift, axis=0), 0.0)
        shift *= 2
    zc = jnp.where(lane < M_HEADS, z, cum)
    zt = jnp.concatenate([zc, jnp.zeros((LANES - lc, LANES), f32)], axis=0).T[:, :lc]

    ri = lax.broadcasted_iota(i32, (lc, lc), 0)
    ci = lax.broadcasted_iota(i32, (lc, lc), 1)
    causal = ri >= ci
    state = [(m_s[h:h + 1, 0:1], c_s[h], n_s[h:h + 1, :]) for h in range(M_HEADS)]
    hs, new_state = [], []
    for h in range(M_HEADS):
        sl = slice(h * M_DH, (h + 1) * M_DH)
        q = qk[:, sl]
        k = qk[:, M_WIDTH + h * M_DH:M_WIDTH + (h + 1) * M_DH] * (M_DH ** -0.5)
        v = vall[:, sl]
        i_col = zc[:, h:h + 1]
        b_col = zc[:, M_HEADS + h:M_HEADS + h + 1]
        i_row = zt[h:h + 1, :]
        b_row = zt[M_HEADS + h:M_HEADS + h + 1, :]
        m_prev, c_prev, n_prev = state[h]

        dmat = jnp.where(causal, b_col - b_row + i_row, NEG)
        inter = b_col + m_prev
        mt = jnp.maximum(jnp.max(dmat, axis=-1, keepdims=True), inter)
        w_intra = jnp.exp(dmat - mt)
        w_inter = jnp.exp(inter - mt)
        qb, kb, vb = q.astype(bf16), k.astype(bf16), v.astype(bf16)
        s = lax.dot_general(qb, kb, (((1,), (1,)), ((), ())), preferred_element_type=f32) * w_intra
        num = (w_inter * jnp.dot(qb, c_prev.astype(bf16), preferred_element_type=f32)
               + jnp.dot(s.astype(bf16), vb, preferred_element_type=f32))
        den = w_inter * jnp.sum(q * n_prev, axis=-1, keepdims=True) + jnp.sum(s, axis=-1, keepdims=True)
        hh = num / jnp.maximum(jnp.abs(den), jnp.exp(-mt))
        m_new = mt[lc - 1:lc, :]
        b_last = b_col[lc - 1:lc, :]
        w_s = jnp.exp(b_last - b_col + i_col - m_new)
        decay = jnp.exp(b_last + m_prev - m_new)
        kw = k * w_s
        new_state.append((jnp.broadcast_to(m_new, (1, LANES)),
                          decay * c_prev + lax.dot_general(kw.astype(bf16), vb, (((0,), (0,)), ((), ())),
                                                           preferred_element_type=f32),
                          decay * n_prev + jnp.sum(kw, axis=0, keepdims=True)))
        hs.append(hh * lax.rsqrt(jnp.mean(hh * hh, axis=-1, keepdims=True) + EPS))

    for h, (m_new, c_new, n_new) in enumerate(new_state):
        m_s[h:h + 1, :] = m_new
        c_s[h] = c_new
        n_s[h:h + 1, :] = n_new
    return jnp.concatenate(hs, axis=-1) * mhg_ref[...] * jax.nn.sigmoid(o_in)


def _mlstm_kernel(qk_ref, v_ref, o_ref, gate_ref, c0_ref, n0_ref, m0_ref, cbuf_ref,
                  cw_ref, cb_ref, bg_ref, mhg_ref,
                  h_ref, cout_ref, nout_ref, mout_ref,
                  c_s, n_s, m_s, prev_s):
    c = pl.program_id(1)

    @pl.when(c == 0)
    def _():
        c_s[...] = c0_ref[0]
        n_s[...] = n0_ref[0]
        m_s[...] = m0_ref[0]
        prev_s[...] = cbuf_ref[0]

    h_ref[...] = _mlstm_chunk(qk_ref[...], v_ref[...], o_ref[...], gate_ref[...],
                              cw_ref, cb_ref, bg_ref, mhg_ref, c_s, n_s, m_s, prev_s)

    @pl.when(c == pl.num_programs(1) - 1)
    def _():
        cout_ref[0] = c_s[...]
        nout_ref[0] = n_s[...]
        mout_ref[0] = m_s[...]


def _mlstm(mqk, mv, mo, gates, row0, bsz, t, c0, n0, m0, cbuf, conv_w, conv_b, b_gates, mh_g):
    lc = min(CHUNK, t)
    step = lc
    nc = t // step
    assert t % step == 0 and lc % SUBLANES == 0 and row0 % step == 0
    n0p = jnp.pad(n0.astype(f32), ((0, 0), (0, SUBLANES - M_HEADS), (0, 0)))
    m0p = jnp.pad(jnp.broadcast_to(m0.astype(f32)[:, :, None], (bsz, M_HEADS, LANES)),
                  ((0, 0), (0, SUBLANES - M_HEADS), (0, 0)))
    cbp = jnp.pad(cbuf.astype(f32), ((0, 0), (SUBLANES - (CONV_W - 1), 0), (0, 0)))
    bg = jnp.pad(b_gates.astype(f32), (0, LANES - 2 * M_HEADS)).reshape(1, LANES)
    seq = lambda w: pl.BlockSpec((step, w), lambda b, c: (row0 // step + b * nc + c, 0))
    out_seq = pl.BlockSpec((step, M_WIDTH), lambda b, c: (b * nc + c, 0))
    per_b = lambda shp: pl.BlockSpec((1,) + shp, lambda b, c: (b,) + (0,) * len(shp))
    full = lambda a: pl.BlockSpec(a.shape, lambda b, c: (0,) * a.ndim)
    cb = conv_b.reshape(1, -1)
    mhg = mh_g.reshape(1, -1)
    return pl.pallas_call(
        _mlstm_kernel,
        grid=(bsz, nc),
        in_specs=[seq(2 * M_WIDTH), seq(M_WIDTH), seq(M_WIDTH), seq(LANES),
                  per_b((M_HEADS, M_DH, M_DH)), per_b((SUBLANES, M_DH)), per_b((SUBLANES, LANES)),
                  per_b((SUBLANES, 2 * M_WIDTH)),
                  full(conv_w), full(cb), full(bg), full(mhg)],
        out_specs=[out_seq, per_b((M_HEADS, M_DH, M_DH)), per_b((SUBLANES, M_DH)), per_b((SUBLANES, LANES))],
        out_shape=[jax.ShapeDtypeStruct((bsz * t, M_WIDTH), f32),
                   jax.ShapeDtypeStruct((bsz, M_HEADS, M_DH, M_DH), f32),
                   jax.ShapeDtypeStruct((bsz, SUBLANES, M_DH), f32),
                   jax.ShapeDtypeStruct((bsz, SUBLANES, LANES), f32)],
        scratch_shapes=[pltpu.VMEM((M_HEADS, M_DH, M_DH), f32), pltpu.VMEM((SUBLANES, M_DH), f32),
                        pltpu.VMEM((SUBLANES, LANES), f32), pltpu.VMEM((SUBLANES, 2 * M_WIDTH), f32)],
        compiler_params=pltpu.CompilerParams(dimension_semantics=("parallel", "arbitrary"),
                                             vmem_limit_bytes=VMEM_LIMIT),
        name="mlstm",
    )(mqk, mv, mo, gates, c0.astype(f32), n0p, m0p, cbp, conv_w, cb, bg, mhg)


def _topk_rows(s, k, payload=None):
    n = s.shape[0]
    rows = lax.broadcasted_iota(i32, s.shape, 0).astype(f32)
    vals, ids = [], []
    for _ in range(k):
        m = jnp.max(s, axis=0, keepdims=True)
        pos = jnp.min(jnp.where(s == m, rows, float(n)), axis=0, keepdims=True)
        sel = rows == pos
        vals.append(m)
        ids.append(pos if payload is None else jnp.max(jnp.where(sel, payload, -1.0), axis=0, keepdims=True))
        s = jnp.where(sel, -jnp.inf, s)
    return jnp.concatenate(vals, axis=0), jnp.concatenate(ids, axis=0)


def _mid_kernel(xp_ref, xs_ref, attp_ref, atts_ref, hp_ref, hs_ref, wo_ref, g2_ref, wq_ref, keys_ref, after_ref,
                x1_ref, xn_ref, eidx_ref, eidx_t_ref, gate_ref, *, nbp):
    del after_ref
    cat = jnp.concatenate([_pick(nbp, attp_ref, atts_ref), _pick(nbp, hp_ref, hs_ref)], axis=-1).astype(bf16)
    x1 = _pick(nbp, xp_ref, xs_ref) + jnp.dot(cat, wo_ref[...], preferred_element_type=f32)
    x1_ref[...] = x1
    xn = _rms(x1, g2_ref[...])
    xn_ref[...] = xn
    xb = xn.astype(bf16)
    e_rows, g_rows = [], []
    for h in range(P_HEADS):
        q = jnp.dot(xb, wq_ref[:, h * P_DKEY:(h + 1) * P_DKEY], preferred_element_type=f32).astype(bf16)
        half = []
        for p in range(2):
            st = lax.dot_general(keys_ref[h, p], q[:, p * N_KEYS:(p + 1) * N_KEYS], (((1,), (1,)), ((), ())),
                                 preferred_element_type=f32)
            half.append(_topk_rows(st, P_TOPK))
        (v0, i0), (v1, i1) = half
        width = [P_TOPK // (a + 1) for a in range(P_TOPK)]
        fill = -sum(width) % SUBLANES
        cand = jnp.concatenate([v0[a:a + 1] + v1[:width[a]] for a in range(P_TOPK)]
                               + [jnp.full((fill, v0.shape[1]), -jnp.inf, f32)], axis=0)
        cidx = jnp.concatenate([i0[a:a + 1] * float(N_KEYS) + i1[:width[a]] for a in range(P_TOPK)]
                               + [jnp.zeros((fill, v0.shape[1]), f32)], axis=0)
        top_s, eid = _topk_rows(cand, P_TOPK, payload=cidx)
        ex = jnp.exp(top_s - top_s[0:1])
        e_rows.append(eid)
        g_rows.append(ex / jnp.sum(ex, axis=0, keepdims=True))
    eidx_t = jnp.concatenate(e_rows, axis=0)
    eidx_t_ref[...] = eidx_t.astype(i32)
    eidx_ref[...] = eidx_t.T.astype(i32)
    gate_ref[...] = jnp.concatenate(g_rows, axis=0).T


def _mid(xp, xp_row0, xs, xs_row0, attp, atts, hp, hs, n_a, n_b, after, w_out, g2, wq, keys):
    n = n_a + n_b
    nbp = n_a // ROW_BLOCK
    assert n_a % ROW_BLOCK == 0 and n_b % ROW_BLOCK == 0 and xp_row0 % ROW_BLOCK == 0 and xs_row0 % ROW_BLOCK == 0
    row = lambda w: pl.BlockSpec((ROW_BLOCK, w), lambda i: (i, 0))
    full = lambda a: pl.BlockSpec(a.shape, lambda i: (0,) * a.ndim)
    wo = w_out.astype(bf16)
    wqb = wq.astype(bf16)
    kb = keys.astype(bf16)
    g = g2.reshape(1, D_MODEL)
    return pl.pallas_call(
        functools.partial(_mid_kernel, nbp=nbp),
        grid=(n // ROW_BLOCK,),
        in_specs=[*_pair_specs(nbp, D_MODEL, xp_row0 // ROW_BLOCK, xs_row0 // ROW_BLOCK), *_pair_specs(nbp, A_WIDTH),
                  *_pair_specs(nbp, M_WIDTH), full(wo), full(g), full(wqb), full(kb),
                  pl.BlockSpec((SUBLANES, P_SLOTS), lambda i: (0, 0))],
        out_specs=[row(D_MODEL), row(D_MODEL), row(P_SLOTS), pl.BlockSpec((P_SLOTS, ROW_BLOCK), lambda i: (0, i)),
                   row(P_SLOTS)],
        out_shape=[jax.ShapeDtypeStruct((n, D_MODEL), f32), jax.ShapeDtypeStruct((n, D_MODEL), f32),
                   jax.ShapeDtypeStruct((n, P_SLOTS), i32), jax.ShapeDtypeStruct((P_SLOTS, n), i32),
                   jax.ShapeDtypeStruct((n, P_SLOTS), f32)],
        compiler_params=pltpu.CompilerParams(dimension_semantics=("parallel",), vmem_limit_bytes=VMEM_LIMIT),
        name="outproj_retrieve",
    )(xp, xs, attp, atts, hp, hs, wo, g, wqb, kb, after)


def _gather_rows(eidx_refs, tab_ref, g_ref, t):
    for i in range(SUBLANES):
        for k, eidx_ref in enumerate(eidx_refs):
            r = k * SUBLANES + i
            g_ref[pl.ds(r * ROW_TILES, ROW_TILES), :] = tab_ref[eidx_ref[i, t]]


def _tile_rows(t, rows=SUBLANES):
    return pl.ds(pl.multiple_of(t * rows, rows), rows)


def _pipelined_tokens(nt, eidx_refs, tab_ref, g0_s, g1_s, compute):
    bufs = (g0_s, g1_s)
    _gather_rows(eidx_refs, tab_ref, g0_s, 0)

    def body(j, carry):
        for u in range(TOKEN_UNROLL):
            t = TOKEN_UNROLL * j + u
            _gather_rows(eidx_refs, tab_ref, bufs[(u + 1) % 2], jnp.minimum(t + 1, nt - 1))
            compute(t, bufs[u % 2])
        return carry

    lax.fori_loop(0, nt // TOKEN_UNROLL, body, 0)


def _gelu_tanh(x):
    return 0.5 * x * (1.0 + jnp.tanh(math.sqrt(2.0 / math.pi) * (x + 0.044715 * (x * x * x))))


def _peer_u_kernel(*refs):
    eidx_refs, (xn_ref, g_ref, tab_ref, w_ref, xl_s, r_s, g0_s, g1_s) = refs[:INDEX_GROUPS], refs[INDEX_GROUPS:]
    nt = xn_ref.shape[0]
    xn = xn_ref[...]
    xh = xn.astype(bf16).astype(f32)
    xl = xn - xh
    for k in range(ROW_TILES):
        xl_s[pl.ds(k, nt, stride=2 * SUBLANES), :] = xh[:, k * LANES:(k + 1) * LANES]
        xl_s[pl.ds(SUBLANES + k, nt, stride=2 * SUBLANES), :] = xl[:, k * LANES:(k + 1) * LANES]
    diag = (lax.broadcasted_iota(i32, (SUBLANES, G_ROWS), 1) % ROW_TILES
            == lax.broadcasted_iota(i32, (SUBLANES, G_ROWS), 0))

    def compute(t, g_s):
        lhs = xl_s[_tile_rows(t, 2 * SUBLANES), :].astype(bf16)
        out = lax.dot_general(lhs, g_s[...], (((1,), (1,)), ((), ())), preferred_element_type=f32)
        part = jnp.where(diag, out[:SUBLANES] + out[SUBLANES:], 0.0)
        for c in range(ROW_TILES):
            r_s[c, _tile_rows(t), :] = part[:, c * LANES:(c + 1) * LANES]

    _pipelined_tokens(nt, eidx_refs, tab_ref, g0_s, g1_s, compute)
    cols = []
    for c in range(ROW_TILES):
        acc = r_s[c, pl.ds(0, nt, stride=SUBLANES), :]
        for k in range(1, SUBLANES):
            acc = acc + r_s[c, pl.ds(k, nt, stride=SUBLANES), :]
        cols.append(acc)
    s = jnp.concatenate(cols, axis=-1)
    fold = (lax.broadcasted_iota(i32, (G_ROWS, P_SLOTS), 0) // ROW_TILES
            == lax.broadcasted_iota(i32, (G_ROWS, P_SLOTS), 1)).astype(bf16)
    sh, sl = _split_bf16(s)
    act = jnp.dot(sh, fold, preferred_element_type=f32) + jnp.dot(sl, fold, preferred_element_type=f32)
    w_ref[...] = g_ref[...] * _gelu_tanh(act)


def _peer_v_kernel(*refs):
    eidx_refs, (w_ref, tab_ref, peer_ref, wl_s, o_s, g0_s, g1_s) = refs[:INDEX_GROUPS], refs[INDEX_GROUPS:]
    nt = w_ref.shape[0]
    spread = (lax.broadcasted_iota(i32, (P_SLOTS, G_ROWS), 1) // ROW_TILES
              == lax.broadcasted_iota(i32, (P_SLOTS, G_ROWS), 0)).astype(bf16)
    wexp = jnp.dot(w_ref[...].astype(bf16), spread, preferred_element_type=f32)
    lane = lax.broadcasted_iota(i32, (nt, LANES), 1)
    for c in range(ROW_TILES):
        wc = wexp[:, c * LANES:(c + 1) * LANES]
        for k in range(SUBLANES):
            wl_s[c, pl.ds(k, nt, stride=SUBLANES), :] = jnp.where(lane % ROW_TILES == k, wc, 0.0)

    def compute(t, g_s):
        lhs = jnp.concatenate([wl_s[c, _tile_rows(t), :] for c in range(ROW_TILES)], axis=-1).astype(bf16)
        o_s[_tile_rows(t), :] = jnp.dot(lhs, g_s[...], preferred_element_type=f32)

    _pipelined_tokens(nt, eidx_refs, tab_ref, g0_s, g1_s, compute)
    for k in range(ROW_TILES):
        peer_ref[:, k * LANES:(k + 1) * LANES] = o_s[pl.ds(k, nt, stride=SUBLANES), :]


def _final_kernel(*refs, starts, nbp):
    k = len(starts)
    peer_refs, x_refs, (gf_ref, yp_ref, ys_ref) = refs[:k], refs[k:2 * k], refs[2 * k:]
    resid = x_refs[0][...] + peer_refs[0][...]
    for start, p_ref, x_ref in zip(starts[1:], peer_refs[1:], x_refs[1:]):
        resid = jnp.where(pl.program_id(0) >= start, x_ref[...] + p_ref[...], resid)
    y = _rms(resid, gf_ref[...])

    @pl.when(pl.program_id(0) < nbp)
    def _():
        yp_ref[...] = y

    @pl.when(pl.program_id(0) >= nbp)
    def _():
        ys_ref[...] = y


def _final(peers, x1s, gf, n_p):
    counts = [x.shape[0] // ROW_BLOCK for x in x1s]
    starts = [sum(counts[:k]) for k in range(len(counts))]
    n, nbp = sum(counts) * ROW_BLOCK, n_p // ROW_BLOCK
    g = gf.reshape(1, D_MODEL)
    seg = [pl.BlockSpec((ROW_BLOCK, D_MODEL), lambda i, s=s, c=c: (jnp.clip(i - s, 0, c - 1), 0))
           for s, c in zip(starts, counts)]
    return pl.pallas_call(
        functools.partial(_final_kernel, starts=tuple(starts), nbp=nbp),
        grid=(n // ROW_BLOCK,),
        in_specs=[*seg, *seg, pl.BlockSpec(g.shape, lambda i: (0, 0))],
        out_specs=list(_pair_specs(nbp, D_MODEL)),
        out_shape=[jax.ShapeDtypeStruct((n_p, D_MODEL), f32), jax.ShapeDtypeStruct((n - n_p, D_MODEL), f32)],
        compiler_params=pltpu.CompilerParams(dimension_semantics=("arbitrary",), vmem_limit_bytes=VMEM_LIMIT),
        name="final_norm",
    )(*peers, *x1s, g)


def _peer_specs():
    row = lambda w: pl.BlockSpec((ROW_BLOCK, w), lambda i: (i, 0))
    idx = [pl.BlockSpec((SUBLANES, ROW_BLOCK), lambda i, k=k: (k, i), memory_space=pltpu.SMEM)
           for k in range(INDEX_GROUPS)]
    tab = pl.BlockSpec(memory_space=pltpu.VMEM)
    gscr = pltpu.VMEM((G_ROWS, LANES), bf16)
    params = pltpu.CompilerParams(dimension_semantics=("arbitrary",), vmem_limit_bytes=VMEM_LIMIT)
    return row, idx, tab, gscr, params


def _expert_table(tab):
    return tab.astype(bf16).reshape(tab.shape[0], ROW_TILES, LANES)


def _peer_u(eidx_t, xn, g, utab, n):
    row, idx, tab, gscr, params = _peer_specs()
    return pl.pallas_call(
        _peer_u_kernel,
        grid=(n // ROW_BLOCK,),
        in_specs=[*idx, row(D_MODEL), row(P_SLOTS), tab],
        out_specs=row(P_SLOTS),
        out_shape=jax.ShapeDtypeStruct((n, P_SLOTS), f32),
        scratch_shapes=[pltpu.VMEM((ROW_BLOCK * 2 * SUBLANES, LANES), f32),
                        pltpu.VMEM((ROW_TILES, ROW_BLOCK * SUBLANES, LANES), f32), gscr, gscr],
        compiler_params=params,
        name="peer_u",
    )(*[eidx_t] * INDEX_GROUPS, xn, g, utab)


def _peer_v(eidx_t, w, vtab):
    n = w.shape[0]
    row, idx, tab, gscr, params = _peer_specs()
    return pl.pallas_call(
        _peer_v_kernel,
        grid=(n // ROW_BLOCK,),
        in_specs=[*idx, row(P_SLOTS), tab],
        out_specs=row(D_MODEL),
        out_shape=jax.ShapeDtypeStruct((n, D_MODEL), f32),
        scratch_shapes=[pltpu.VMEM((ROW_TILES, ROW_BLOCK * SUBLANES, LANES), f32),
                        pltpu.VMEM((ROW_BLOCK * SUBLANES, LANES), f32), gscr, gscr],
        compiler_params=params,
        name="peer_v",
    )(*[eidx_t] * INDEX_GROUPS, w, vtab)


def _sc_table_kernel(t_ref, o_ref):
    t = t_ref[...]
    o_ref[...] = pltpu.pack_elementwise([t[:, :SC_WORDS], t[:, SC_WORDS:]], packed_dtype=bf16)


def _sc_table(tab):
    e = tab.shape[0]
    return pl.pallas_call(
        _sc_table_kernel,
        grid=(e // ROW_BLOCK,),
        in_specs=[pl.BlockSpec((ROW_BLOCK, D_MODEL), lambda i: (i, 0))],
        out_specs=pl.BlockSpec((ROW_BLOCK, SC_WORDS), lambda i: (i, 0)),
        out_shape=jax.ShapeDtypeStruct((e, SC_WORDS), jnp.uint32),
        compiler_params=pltpu.CompilerParams(dimension_semantics=("parallel",), vmem_limit_bytes=VMEM_LIMIT),
        name="sc_table",
    )(tab)


def _sc_unpack(words):
    return plsc.unpack(plsc.bitcast(words, bf16), format=plsc.PackFormat.INTERLEAVED)


def _sc_gelu_tanh(x):
    z = math.sqrt(2.0 / math.pi) * (x + 0.044715 * (x * x * x))
    return 0.5 * x * (2.0 - 2.0 / (jnp.exp(2.0 * z) + 1.0))


def _sc_peer(utab32, vtab32, eidx, xn, gate, after):
    n_sc = eidx.shape[0]
    per = n_sc // SC_WORKERS
    units = 4 * per
    assert n_sc % (SC_WORKERS * SUBLANES) == 0

    def body(u_hbm, v_hbm, eidx_hbm, x_hbm, g_hbm, after_hbm, out_hbm,
             idx_v, x_v, g_v, rows_v, acc_v, w_v, w16_v, out_v, row_sems, tok_sems, out_sem):
        del after_hbm
        base = (lax.axis_index("s") * 2 + lax.axis_index("c")) * per
        lanes = lax.broadcasted_iota(i32, (SC_LANES,), 0)

        def token_copies(tok):
            tslot = tok % 2
            return (pltpu.make_async_copy(eidx_hbm.at[base + tok], idx_v.at[tslot], tok_sems.at[0]),
                    pltpu.make_async_copy(x_hbm.at[base + tok], x_v.at[tslot], tok_sems.at[1]),
                    pltpu.make_async_copy(g_hbm.at[base + tok], g_v.at[tslot], tok_sems.at[2]))

        def store_out(tok):
            return pltpu.make_async_copy(out_v, out_hbm.at[base + tok], out_sem)

        def gather(tab_hbm, g):
            tok, k = g // 4, g % 4
            return pltpu.make_async_copy(tab_hbm.at[idx_v.at[tok % 2, pl.ds((k % 2) * SC_UNIT_ROWS, SC_UNIT_ROWS)]],
                                         rows_v.at[k % 2], row_sems.at[k % 2])

        def start(g):
            @pl.when(g % 4 < 2)
            def _():
                gather(u_hbm, g).start()

            @pl.when(g % 4 >= 2)
            def _():
                gather(v_hbm, g).start()

        def compute_u(tslot, half, rows):
            @pl.loop(0, SC_UNIT_ROWS // SC_ROW_GROUP)
            def _(rg):
                slot0 = half * SC_UNIT_ROWS + rg * SC_ROW_GROUP

                @pl.loop(0, SC_CHUNKS // SC_CHUNK_GROUP)
                def _(cg):
                    keep = jnp.where(cg > 0, 1.0, 0.0).astype(f32)
                    accs = [acc_v[pl.ds((slot0 + r) * SC_LANES, SC_LANES)] * keep for r in range(SC_ROW_GROUP)]
                    for c in range(SC_CHUNK_GROUP):
                        ch = cg * SC_CHUNK_GROUP + c
                        xa = x_v[tslot, pl.ds(ch * SC_LANES, SC_LANES)]
                        xb = x_v[tslot, pl.ds(SC_WORDS + ch * SC_LANES, SC_LANES)]
                        for r in range(SC_ROW_GROUP):
                            a, b = _sc_unpack(rows[rg * SC_ROW_GROUP + r, pl.ds(ch * SC_LANES, SC_LANES)])
                            accs[r] = accs[r] + a * xa + b * xb
                    for r in range(SC_ROW_GROUP):
                        acc_v[pl.ds((slot0 + r) * SC_LANES, SC_LANES)] = accs[r]

        def gate_weights(tslot):
            @pl.loop(0, P_SLOTS // SC_LANES)
            def _(sg):
                first = sg * SC_LANES * SC_LANES
                act = jnp.zeros((SC_LANES,), f32)
                for lane in range(SC_LANES):
                    act = act + plsc.load_gather(acc_v, [first + lanes * SC_LANES + lane])
                w_v[pl.ds(sg * SC_LANES, SC_LANES)] = g_v[tslot, pl.ds(sg * SC_LANES, SC_LANES)] * _sc_gelu_tanh(act)

            @pl.loop(0, P_SLOTS // SC_LANES)
            def _(sg):
                for r in range(SC_LANES):
                    w16_v[pl.ds((sg * SC_LANES + r) * SC_LANES, SC_LANES)] = plsc.load_gather(
                        w_v, [jnp.zeros((SC_LANES,), i32) + (sg * SC_LANES + r)])

        def compute_v(half, rows):
            @pl.loop(0, SC_CHUNKS // SC_CHUNK_GROUP)
            def _(cg):
                first = cg * SC_CHUNK_GROUP

                @pl.loop(0, SC_UNIT_ROWS // SC_ROW_GROUP)
                def _(rg):
                    keep = jnp.where(jnp.logical_or(half == 1, rg > 0), 1.0, 0.0).astype(f32)
                    spots = [pl.ds((j % 2) * SC_WORDS + (first + j // 2) * SC_LANES, SC_LANES)
                             for j in range(2 * SC_CHUNK_GROUP)]
                    accs = [out_v[spot] * keep for spot in spots]
                    for r in range(SC_ROW_GROUP):
                        row = rg * SC_ROW_GROUP + r
                        wv = w16_v[pl.ds((half * SC_UNIT_ROWS + row) * SC_LANES, SC_LANES)]
                        for c in range(SC_CHUNK_GROUP):
                            a, b = _sc_unpack(rows[row, pl.ds((first + c) * SC_LANES, SC_LANES)])
                            accs[2 * c] = accs[2 * c] + a * wv
                            accs[2 * c + 1] = accs[2 * c + 1] + b * wv
                    for spot, acc in zip(spots, accs):
                        out_v[spot] = acc

        for cp in token_copies(0):
            cp.start()
        for cp in token_copies(0):
            cp.wait()
        start(0)

        @pl.loop(0, units)
        def _(g):
            tok, k = g // 4, g % 4

            @pl.when(jnp.logical_and(k == 0, tok + 1 < per))
            def _():
                for cp in token_copies(tok + 1):
                    cp.start()

            @pl.when(jnp.logical_and(k == 3, tok + 1 < per))
            def _():
                for cp in token_copies(tok + 1):
                    cp.wait()

            @pl.when(g + 1 < units)
            def _():
                start(g + 1)

            gather(u_hbm, g).wait()
            rows = rows_v.at[k % 2]

            @pl.when(k < 2)
            def _():
                compute_u(tok % 2, k, rows)

            @pl.when(k == 1)
            def _():
                gate_weights(tok % 2)

            @pl.when(jnp.logical_and(k == 2, tok > 0))
            def _():
                store_out(tok - 1).wait()

            @pl.when(k >= 2)
            def _():
                compute_v(k - 2, rows)

            @pl.when(k == 3)
            def _():
                store_out(tok).start()

        store_out(per - 1).wait()

    return pl.kernel(
        body, mesh=plsc.VectorSubcoreMesh(core_axis_name="c", subcore_axis_name="s"),
        out_type=jax.ShapeDtypeStruct((n_sc, D_MODEL), f32),
        scratch_types=[pltpu.VMEM((2, P_SLOTS), i32), pltpu.VMEM((2, D_MODEL), f32), pltpu.VMEM((2, P_SLOTS), f32),
                       pltpu.VMEM((2, SC_UNIT_ROWS, SC_WORDS), jnp.uint32), pltpu.VMEM((P_SLOTS * SC_LANES,), f32),
                       pltpu.VMEM((P_SLOTS,), f32), pltpu.VMEM((P_SLOTS * SC_LANES,), f32), pltpu.VMEM((D_MODEL,), f32),
                       pltpu.SemaphoreType.DMA((2,)), pltpu.SemaphoreType.DMA((3,)), pltpu.SemaphoreType.DMA],
        compiler_params=pltpu.CompilerParams(needs_layout_passes=False),
        name="sc_peer",
    )(utab32, vtab32, eidx, xn, gate, after)


def kernel(x_prompt, x_sample, cache_k, cache_v, state_C, state_n, state_m, state_conv, norm1_g, w_in, b_gates, rel_bias, conv_w, conv_b, mh_norm_g, w_out, norm2_g, peer_wq, peer_keys, peer_u, peer_v, final_g):
    bp, sp, d = x_prompt.shape
    bs, ts, _ = x_sample.shape
    n_p, n_s = bp * sp, bs * ts
    n = n_p + n_s
    assert n_p % ROW_BLOCK == 0 and n_s % ROW_BLOCK == 0 and d == D_MODEL
    depth = w_in.shape[0]
    assert depth == 1, "the final norm is fused into the last layer's PEER pass"
    l = 0
    xp, xs = x_prompt.reshape(n_p, d), x_sample.reshape(n_s, d)

    zeros = lambda *shp: jnp.zeros(shp, f32)
    mparams = (conv_w[l], conv_b[l], b_gates[l], mh_norm_g[l])
    retrieval = (w_out[l], norm2_g[l], peer_wq[l], peer_keys[l])

    def mixers(proj, seq, lo, hi, state):
        aq, ak, av, mqk, mv, mo, gates = proj
        att = _attn_prompt(aq, ak, av, rel_bias[l], seq, sp, lo, hi)
        h, c, nn, mm = _mlstm(mqk, mv, mo, gates, seq * sp + lo, 1, hi - lo, *state, *mparams)
        conv_rows = mqk[seq * sp + hi - (CONV_W - 1):seq * sp + hi][None]
        return att, h, (c, nn[:, :M_HEADS], mm[:, :M_HEADS, 0], conv_rows)

    fresh = (zeros(1, M_HEADS, M_DH, M_DH), zeros(1, M_HEADS, M_DH), zeros(1, M_HEADS), zeros(1, CONV_W - 1, 2 * M_WIDTH))

    states = [fresh] * bp

    def stage(proj, lo, hi, after, tail=None):
        parts = []
        for seq in range(bp):
            a, b = max(lo, seq * sp), min(hi, (seq + 1) * sp)
            if a < b:
                att, h, states[seq] = mixers(proj, seq, a - seq * sp, b - seq * sp, states[seq])
                parts.append((att, h, b - a, xp, a))
        parts += [tail] if tail is not None else []
        assert 1 <= len(parts) <= 2, "the retrieval kernel reads at most two row sources"
        (att0, h0, n0, x0, r0), (att1, h1, n1, x1, r1) = parts[0], parts[-1]
        n1, r1 = (n1, r1) if len(parts) == 2 else (0, 0)
        return _mid(x0, r0, x1, r1, att0, att1, h0, h1, n0, n1, after, *retrieval)

    sc_tables = _sc_table(peer_u[l]), _sc_table(peer_v[l])
    peers, x1s, lo, after = [], [], 0, jnp.zeros((SUBLANES, P_SLOTS), i32)
    proj = _inproj(xp, SC_STAGE_ENDS[0], xs, norm1_g[l], w_in[l], after)
    for hi in SC_STAGE_ENDS:
        x1, xn, eidx, _, gate = stage(proj, lo, hi, after)
        peers.append(_sc_peer(*sc_tables, eidx, xn, gate, peers[-1] if peers else zeros(SUBLANES, D_MODEL)))
        x1s.append(x1)
        if lo == 0:
            proj = _inproj(xp, n_p, xs, norm1_g[l], w_in[l], eidx)
        lo, after = hi, eidx
    aq, ak, av, mqk, mv, mo, gates = proj

    lcache = cache_k.shape[2]
    att_s = _attn_sample(aq, ak, av, cache_k[l].reshape(bs, lcache, A_WIDTH),
                         cache_v[l].reshape(bs, lcache, A_WIDTH), rel_bias[l], n_p, bs, ts)
    h_s, c_s, nn_s, mm_s = _mlstm(mqk, mv, mo, gates, n_p, bs, ts, state_C[l], state_n[l], state_m[l],
                                  state_conv[l], *mparams)
    x1, xn, _, eidx_t, gate = stage(proj, lo, n_p, after, tail=(att_s, h_s, n_s, xs, 0))
    w = _peer_u(eidx_t, xn, gate, _expert_table(peer_u[l]), n - lo)
    peers.append(_peer_v(eidx_t, w, _expert_table(peer_v[l])))
    x1s.append(x1)
    y_p, y_s = _final(peers, x1s, final_g, n_p)
    c_p, nn_p, mm_p = (jnp.concatenate(per_seq, axis=0) for per_seq in zip(*(st[:3] for st in states)))

    def tail(a, row0, bsz, t, keep):
        return jnp.stack([a[row0 + (b + 1) * t - keep:row0 + (b + 1) * t] for b in range(bsz)])

    keep = min(WINDOW, sp)
    heads = lambda a: a.reshape(a.shape[0], a.shape[1], A_HEADS, A_DH)
    ctail = CONV_W - 1
    conv_tail = lambda buf, a, row0, bsz, t: jnp.concatenate([buf.astype(a.dtype), tail(a, row0, bsz, t, min(ctail, t))],
                                                             axis=1)[:, -ctail:]
    st = lambda a: a[None]
    return (y_p.reshape(bp, sp, d), y_s.reshape(bs, ts, d),
            st(heads(tail(ak, 0, bp, sp, keep))), st(heads(tail(av, 0, bp, sp, keep))),
            st(c_p), st(nn_p), st(mm_p),
            st(conv_tail(zeros(bp, ctail, 2 * M_WIDTH), mqk, 0, bp, sp)),
            st(heads(ak[n_p:].reshape(bs, ts, A_WIDTH))), st(heads(av[n_p:].reshape(bs, ts, A_WIDTH))),
            st(c_s), st(nn_s[:, :M_HEADS]), st(mm_s[:, :M_HEADS, 0]),
            st(conv_tail(state_conv[l], mqk, n_p, bs, ts)))
```

```python
import functools
import math

import jax
import jax.numpy as jnp
from jax import lax
from jax.experimental import pallas as pl
from jax.experimental.pallas import tpu as pltpu
from jax.experimental.pallas import tpu_sc as plsc

f32 = jnp.float32
bf16 = jnp.bfloat16
i32 = jnp.int32

D_MODEL = 1024
CHUNK = 64
A_HEADS = 8
A_DH = 64
A_WIDTH = A_HEADS * A_DH
BAND_CHUNKS = 8
WINDOW = BAND_CHUNKS * CHUNK
MAX_REL = 128
ATT_SCALE = A_DH ** -0.5
M_HEADS = 4
M_DH = 128
M_WIDTH = M_HEADS * M_DH
CONV_W = 4
IN_WIDTHS = (A_WIDTH, A_WIDTH, A_WIDTH, 2 * M_WIDTH, M_WIDTH, M_WIDTH)
P_HEADS = 8
P_DKEY = 256
N_KEYS = 128
P_TOPK = 16
P_SLOTS = P_HEADS * P_TOPK
EPS = 1e-6
NEG = -1e30

LANES = 128
SUBLANES = 8
ROW_BLOCK = 256
ATT_TILE = 512
ATT_SUB = 128
ATT_KEYS = ATT_SUB + WINDOW
ROW_TILES = D_MODEL // LANES
G_ROWS = P_SLOTS * ROW_TILES
TOKEN_UNROLL = 8
INDEX_GROUPS = P_SLOTS // SUBLANES
VMEM_LIMIT = 56 * 1024 * 1024

SC_WORKERS = 32
SC_LANES = 16
SC_UNIT_ROWS = P_SLOTS // 2
SC_ROW_GROUP = 16
SC_WORDS = D_MODEL // 2
SC_CHUNKS = SC_WORDS // SC_LANES
SC_CHUNK_GROUP = 8
SC_STAGE_ENDS = (2048, 6144, 20480)


def _rms(x, g):
    return x * lax.rsqrt(jnp.mean(x * x, axis=-1, keepdims=True) + EPS) * g


def _split_bf16(x):
    hi = x.astype(bf16)
    lo = (x - hi.astype(f32)).astype(bf16)
    return hi, lo


def _pair_specs(nbp, width, first=0, second=0):
    return (pl.BlockSpec((ROW_BLOCK, width), lambda i: (first + jnp.minimum(i, nbp - 1), 0)),
            pl.BlockSpec((ROW_BLOCK, width), lambda i: (second + jnp.maximum(i - nbp, 0), 0)))


def _pick(nbp, p_ref, s_ref):
    return jnp.where(pl.program_id(0) < nbp, p_ref[...], s_ref[...])


def _inproj_kernel(xp_ref, xs_ref, g_ref, w_ref, wgh_ref, wgl_ref, after_ref,
                   aq_ref, ak_ref, av_ref, mqk_ref, mv_ref, mo_ref, gate_ref, *, nbp):
    del after_ref
    xn = _rms(_pick(nbp, xp_ref, xs_ref), g_ref[...])
    xh, xl = _split_bf16(xn)

    lo = 0
    for out_ref, width in zip((aq_ref, ak_ref, av_ref, mqk_ref, mv_ref, mo_ref), IN_WIDTHS):
        out_ref[...] = jnp.dot(xh, w_ref[:, lo:lo + width], preferred_element_type=f32)
        lo += width
    gate_ref[...] = (jnp.dot(xh, wgh_ref[...], preferred_element_type=f32)
                     + jnp.dot(xl, wgh_ref[...], preferred_element_type=f32)
                     + jnp.dot(xh, wgl_ref[...], preferred_element_type=f32))


def _inproj(xp, n_p, xs, g1, w_in, after):
    n = n_p + xs.shape[0]
    nbp = n_p // ROW_BLOCK
    main = sum(IN_WIDTHS)
    w_main = w_in[:, :main].astype(bf16)
    wg = jnp.pad(w_in[:, main:], ((0, 0), (0, LANES - 2 * M_HEADS)))
    wgh, wgl = _split_bf16(wg)
    widths = IN_WIDTHS + (LANES,)
    row = lambda w: pl.BlockSpec((ROW_BLOCK, w), lambda i: (i, 0))
    full = lambda a: pl.BlockSpec(a.shape, lambda i: (0,) * a.ndim)
    g = g1.reshape(1, D_MODEL)
    return pl.pallas_call(
        functools.partial(_inproj_kernel, nbp=nbp),
        grid=(n // ROW_BLOCK,),
        in_specs=[*_pair_specs(nbp, D_MODEL), full(g), full(w_main), full(wgh), full(wgl),
                  pl.BlockSpec((SUBLANES, P_SLOTS), lambda i: (0, 0))],
        out_specs=[row(w) for w in widths],
        out_shape=[jax.ShapeDtypeStruct((n, w), f32) for w in widths],
        compiler_params=pltpu.CompilerParams(dimension_semantics=("parallel",), vmem_limit_bytes=VMEM_LIMIT),
        name="inproj",
    )(xp, xs, g, w_main, wgh, wgl, after)


def _attn_heads(q, k, v, bias_ref, key_ok):
    outs = []
    for h in range(A_HEADS):
        sl = slice(h * A_DH, (h + 1) * A_DH)
        s = lax.dot_general(q[:, sl], k[:, sl], (((1,), (1,)), ((), ())), preferred_element_type=f32)
        s = s * ATT_SCALE + bias_ref[h]
        if key_ok is not None:
            s = jnp.where(key_ok, s, NEG)
        m = jnp.max(s, axis=-1, keepdims=True)
        p = jnp.exp(s - m)
        l = jnp.sum(p, axis=-1, keepdims=True)
        o = jnp.dot(p.astype(bf16), v[:, sl], preferred_element_type=f32)
        outs.append(o / l)
    return jnp.concatenate(outs, axis=-1)


def _attn_prompt_kernel(q_ref, k0_ref, k1_ref, v0_ref, v1_ref, bias_ref, o_ref, *, t0):
    t = t0 + pl.program_id(1)
    q = q_ref[...].astype(bf16)
    k = jnp.concatenate([k0_ref[...], k1_ref[...]], axis=0).astype(bf16)
    v = jnp.concatenate([v0_ref[...], v1_ref[...]], axis=0).astype(bf16)
    col = lax.broadcasted_iota(i32, (1, ATT_KEYS), 1)
    for s in range(ATT_TILE // ATT_SUB):
        lo = s * ATT_SUB
        key_ok = (t * ATT_TILE + lo + col) >= WINDOW
        o_ref[lo:lo + ATT_SUB, :] = _attn_heads(q[lo:lo + ATT_SUB], k[lo:lo + ATT_KEYS], v[lo:lo + ATT_KEYS],
                                                 bias_ref, key_ok)


def _attn_sample_kernel(q_ref, k_ref, v_ref, bias_ref, o_ref):
    o_ref[...] = _attn_heads(q_ref[...].astype(bf16), k_ref[0].astype(bf16), v_ref[0].astype(bf16), bias_ref, None)


def _rel_bias_table(rel_bias, rows, cols, offset, valid):
    span = rows + cols - 1
    rel = offset + rows - 1 - jnp.arange(span)
    diag = rel_bias[:, jnp.clip(rel, -MAX_REL, MAX_REL) + MAX_REL].astype(f32)
    diag = jnp.pad(diag, ((0, 0), (0, 1)))
    flat = jnp.tile(diag, (1, rows))[:, rows - 1:rows - 1 + rows * span]
    return jnp.where(valid[None], flat.reshape(-1, rows, span)[:, :, :cols], NEG)


def _attn_prompt(q, k, v, rel_bias, seq, s, lo, hi):
    assert s % ATT_TILE == 0 and WINDOW == ATT_TILE and lo % ATT_TILE == 0 and hi % ATT_TILE == 0
    nt, t0, cnt = s // ATT_TILE, lo // ATT_TILE, (hi - lo) // ATT_TILE
    i = jnp.arange(ATT_SUB)[:, None]
    j = jnp.arange(ATT_KEYS)[None, :]
    off = j - (i // CHUNK) * CHUNK
    bias = _rel_bias_table(rel_bias, ATT_SUB, ATT_KEYS, WINDOW, (off >= 0) & (off < WINDOW + CHUNK))
    cur = pl.BlockSpec((ATT_TILE, A_WIDTH), lambda b, t: (seq * nt + t0 + t, 0))
    prev = pl.BlockSpec((ATT_TILE, A_WIDTH), lambda b, t: (seq * nt + jnp.maximum(t0 + t - 1, 0), 0))
    return pl.pallas_call(
        functools.partial(_attn_prompt_kernel, t0=t0),
        grid=(1, cnt),
        in_specs=[cur, prev, cur, prev, cur, pl.BlockSpec(bias.shape, lambda b, t: (0, 0, 0))],
        out_specs=pl.BlockSpec((ATT_TILE, A_WIDTH), lambda b, t: (t, 0)),
        out_shape=jax.ShapeDtypeStruct((hi - lo, A_WIDTH), f32),
        compiler_params=pltpu.CompilerParams(dimension_semantics=("parallel", "parallel"),
                                             vmem_limit_bytes=VMEM_LIMIT),
        name="attn_prompt",
    )(q, k, k, v, v, bias)


def _attn_sample(q, k, v, ck, cv, rel_bias, row0, bsz, t):
    l = ck.shape[1]
    assert row0 % t == 0
    keys = -(-(l + t) // LANES) * LANES
    padk = ((0, 0), (0, keys - l - t), (0, 0))
    kk = jnp.pad(jnp.concatenate([ck, k[row0:].reshape(bsz, t, A_WIDTH)], axis=1), padk)
    vv = jnp.pad(jnp.concatenate([cv, v[row0:].reshape(bsz, t, A_WIDTH)], axis=1), padk)
    j = jnp.arange(keys)[None, :]
    bias = _rel_bias_table(rel_bias, t, keys, l, jnp.broadcast_to(j < l + t, (t, keys)))
    return pl.pallas_call(
        _attn_sample_kernel,
        grid=(bsz,),
        in_specs=[pl.BlockSpec((t, A_WIDTH), lambda b: (row0 // t + b, 0)),
                  pl.BlockSpec((1, keys, A_WIDTH), lambda b: (b, 0, 0)),
                  pl.BlockSpec((1, keys, A_WIDTH), lambda b: (b, 0, 0)),
                  pl.BlockSpec(bias.shape, lambda b: (0, 0, 0))],
        out_specs=pl.BlockSpec((t, A_WIDTH), lambda b: (b, 0)),
        out_shape=jax.ShapeDtypeStruct((bsz * t, A_WIDTH), f32),
        compiler_params=pltpu.CompilerParams(dimension_semantics=("parallel",), vmem_limit_bytes=VMEM_LIMIT),
        name="attn_sample",
    )(q, kk, vv, bias)


def _mlstm_chunk(a, vall, o_in, gate_in, cw_ref, cb_ref, bg_ref, mhg_ref, c_s, n_s, m_s, prev_s):
    lc = a.shape[0]
    ext = jnp.concatenate([prev_s[...], a], axis=0)
    conv = cb_ref[...]
    for j in range(CONV_W):
        lo = SUBLANES - (CONV_W - 1) + j
        conv = conv + cw_ref[j:j + 1, :] * ext[lo:lo + lc]
    prev_s[...] = a[lc - SUBLANES:lc]
    qk = conv * jax.nn.sigmoid(conv)

    z = gate_in + bg_ref[...]
    lane = lax.broadcasted_iota(i32, (lc, LANES), 1)
    row = lax.broadcasted_iota(i32, (lc, LANES), 0)
    logf = jnp.minimum(z, 0.0) - jnp.log1p(jnp.exp(-jnp.abs(z)))
    cum = jnp.where((lane >= M_HEADS) & (lane < 2 * M_HEADS), logf, 0.0)
    shift = 1
    while shift < lc:
        cum = cum + jnp.where(row >= shift, pltpu.roll(cum, shift, axis=0), 0.0)
        shift *= 2
    zc = jnp.where(lane < M_HEADS, z, cum)
    zt = jnp.concatenate([zc, jnp.zeros((LANES - lc, LANES), f32)], axis=0).T[:, :lc]

    ri = lax.broadcasted_iota(i32, (lc, lc), 0)
    ci = lax.broadcasted_iota(i32, (lc, lc), 1)
    causal = ri >= ci
    state = [(m_s[h:h + 1, 0:1], c_s[h], n_s[h:h + 1, :]) for h in range(M_HEADS)]
    hs, new_state = [], []
    for h in range(M_HEADS):
        sl = slice(h * M_DH, (h + 1) * M_DH)
        q = qk[:, sl]
        k = qk[:, M_WIDTH + h * M_DH:M_WIDTH + (h + 1) * M_DH] * (M_DH ** -0.5)
        v = vall[:, sl]
        i_col = zc[:, h:h + 1]
        b_col = zc[:, M_HEADS + h:M_HEADS + h + 1]
        i_row = zt[h:h + 1, :]
        b_row = zt[M_HEADS + h:M_HEADS + h + 1, :]
        m_prev, c_prev, n_prev = state[h]

        dmat = jnp.where(causal, b_col - b_row + i_row, NEG)
        inter = b_col + m_prev
        mt = jnp.maximum(jnp.max(dmat, axis=-1, keepdims=True), inter)
        w_intra = jnp.exp(dmat - mt)
        w_inter = jnp.exp(inter - mt)
        qb, kb, vb = q.astype(bf16), k.astype(bf16), v.astype(bf16)
        s = lax.dot_general(qb, kb, (((1,), (1,)), ((), ())), preferred_element_type=f32) * w_intra
        num = (w_inter * jnp.dot(qb, c_prev.astype(bf16), preferred_element_type=f32)
               + jnp.dot(s.astype(bf16), vb, preferred_element_type=f32))
        den = w_inter * jnp.sum(q * n_prev, axis=-1, keepdims=True) + jnp.sum(s, axis=-1, keepdims=True)
        hh = num / jnp.maximum(jnp.abs(den), jnp.exp(-mt))
        m_new = mt[lc - 1:lc, :]
        b_last = b_col[lc - 1:lc, :]
        w_s = jnp.exp(b_last - b_col + i_col - m_new)
        decay = jnp.exp(b_last + m_prev - m_new)
        kw = k * w_s
        new_state.append((jnp.broadcast_to(m_new, (1, LANES)),
                          decay * c_prev + lax.dot_general(kw.astype(bf16), vb, (((0,), (0,)), ((), ())),
                                                           preferred_element_type=f32),
                          decay * n_prev + jnp.sum(kw, axis=0, keepdims=True)))
        hs.append(hh * lax.rsqrt(jnp.mean(hh * hh, axis=-1, keepdims=True) + EPS))

    for h, (m_new, c_new, n_new) in enumerate(new_state):
        m_s[h:h + 1, :] = m_new
        c_s[h] = c_new
        n_s[h:h + 1, :] = n_new
    return jnp.concatenate(hs, axis=-1) * mhg_ref[...] * jax.nn.sigmoid(o_in)


def _mlstm_kernel(qk_ref, v_ref, o_ref, gate_ref, c0_ref, n0_ref, m0_ref, cbuf_ref,
                  cw_ref, cb_ref, bg_ref, mhg_ref,
                  h_ref, cout_ref, nout_ref, mout_ref,
                  c_s, n_s, m_s, prev_s):
    c = pl.program_id(1)

    @pl.when(c == 0)
    def _():
        c_s[...] = c0_ref[0]
        n_s[...] = n0_ref[0]
        m_s[...] = m0_ref[0]
        prev_s[...] = cbuf_ref[0]

    h_ref[...] = _mlstm_chunk(qk_ref[...], v_ref[...], o_ref[...], gate_ref[...],
                              cw_ref, cb_ref, bg_ref, mhg_ref, c_s, n_s, m_s, prev_s)

    @pl.when(c == pl.num_programs(1) - 1)
    def _():
        cout_ref[0] = c_s[...]
        nout_ref[0] = n_s[...]
        mout_ref[0] = m_s[...]


def _mlstm(mqk, mv, mo, gates, row0, bsz, t, c0, n0, m0, cbuf, conv_w, conv_b, b_gates, mh_g):
    lc = min(CHUNK, t)
    step = lc
    nc = t // step
    assert t % step == 0 and lc % SUBLANES == 0 and row0 % step == 0
    n0p = jnp.pad(n0.astype(f32), ((0, 0), (0, SUBLANES - M_HEADS), (0, 0)))
    m0p = jnp.pad(jnp.broadcast_to(m0.astype(f32)[:, :, None], (bsz, M_HEADS, LANES)),
                  ((0, 0), (0, SUBLANES - M_HEADS), (0, 0)))
    cbp = jnp.pad(cbuf.astype(f32), ((0, 0), (SUBLANES - (CONV_W - 1), 0), (0, 0)))
    bg = jnp.pad(b_gates.astype(f32), (0, LANES - 2 * M_HEADS)).reshape(1, LANES)
    seq = lambda w: pl.BlockSpec((step, w), lambda b, c: (row0 // step + b * nc + c, 0))
    out_seq = pl.BlockSpec((step, M_WIDTH), lambda b, c: (b * nc + c, 0))
    per_b = lambda shp: pl.BlockSpec((1,) + shp, lambda b, c: (b,) + (0,) * len(shp))
    full = lambda a: pl.BlockSpec(a.shape, lambda b, c: (0,) * a.ndim)
    cb = conv_b.reshape(1, -1)
    mhg = mh_g.reshape(1, -1)
    return pl.pallas_call(
        _mlstm_kernel,
        grid=(bsz, nc),
        in_specs=[seq(2 * M_WIDTH), seq(M_WIDTH), seq(M_WIDTH), seq(LANES),
                  per_b((M_HEADS, M_DH, M_DH)), per_b((SUBLANES, M_DH)), per_b((SUBLANES, LANES)),
                  per_b((SUBLANES, 2 * M_WIDTH)),
                  full(conv_w), full(cb), full(bg), full(mhg)],
        out_specs=[out_seq, per_b((M_HEADS, M_DH, M_DH)), per_b((SUBLANES, M_DH)), per_b((SUBLANES, LANES))],
        out_shape=[jax.ShapeDtypeStruct((bsz * t, M_WIDTH), f32),
                   jax.ShapeDtypeStruct((bsz, M_HEADS, M_DH, M_DH), f32),
                   jax.ShapeDtypeStruct((bsz, SUBLANES, M_DH), f32),
                   jax.ShapeDtypeStruct((bsz, SUBLANES, LANES), f32)],
        scratch_shapes=[pltpu.VMEM((M_HEADS, M_DH, M_DH), f32), pltpu.VMEM((SUBLANES, M_DH), f32),
                        pltpu.VMEM((SUBLANES, LANES), f32), pltpu.VMEM((SUBLANES, 2 * M_WIDTH), f32)],
        compiler_params=pltpu.CompilerParams(dimension_semantics=("parallel", "arbitrary"),
                                             vmem_limit_bytes=VMEM_LIMIT),
        name="mlstm",
    )(mqk, mv, mo, gates, c0.astype(f32), n0p, m0p, cbp, conv_w, cb, bg, mhg)


def _topk_rows(s, k, payload=None):
    n = s.shape[0]
    rows = lax.broadcasted_iota(i32, s.shape, 0).astype(f32)
    vals, ids = [], []
    for _ in range(k):
        m = jnp.max(s, axis=0, keepdims=True)
        pos = jnp.min(jnp.where(s == m, rows, float(n)), axis=0, keepdims=True)
        sel = rows == pos
        vals.append(m)
        ids.append(pos if payload is None else jnp.max(jnp.where(sel, payload, -1.0), axis=0, keepdims=True))
        s = jnp.where(sel, -jnp.inf, s)
    return jnp.concatenate(vals, axis=0), jnp.concatenate(ids, axis=0)


def _mid_kernel(xp_ref, xs_ref, attp_ref, atts_ref, hp_ref, hs_ref, wo_ref, g2_ref, wq_ref, keys_ref, after_ref,
                x1_ref, xn_ref, eidx_ref, eidx_t_ref, gate_ref, *, nbp):
    del after_ref
    cat = jnp.concatenate([_pick(nbp, attp_ref, atts_ref), _pick(nbp, hp_ref, hs_ref)], axis=-1).astype(bf16)
    x1 = _pick(nbp, xp_ref, xs_ref) + jnp.dot(cat, wo_ref[...], preferred_element_type=f32)
    x1_ref[...] = x1
    xn = _rms(x1, g2_ref[...])
    xn_ref[...] = xn
    xb = xn.astype(bf16)
    e_rows, g_rows = [], []
    for h in range(P_HEADS):
        q = jnp.dot(xb, wq_ref[:, h * P_DKEY:(h + 1) * P_DKEY], preferred_element_type=f32).astype(bf16)
        half = []
        for p in range(2):
            st = lax.dot_general(keys_ref[h, p], q[:, p * N_KEYS:(p + 1) * N_KEYS], (((1,), (1,)), ((), ())),
                                 preferred_element_type=f32)
            half.append(_topk_rows(st, P_TOPK))
        (v0, i0), (v1, i1) = half
        width = [P_TOPK // (a + 1) for a in range(P_TOPK)]
        fill = -sum(width) % SUBLANES
        cand = jnp.concatenate([v0[a:a + 1] + v1[:width[a]] for a in range(P_TOPK)]
                               + [jnp.full((fill, v0.shape[1]), -jnp.inf, f32)], axis=0)
        cidx = jnp.concatenate([i0[a:a + 1] * float(N_KEYS) + i1[:width[a]] for a in range(P_TOPK)]
                               + [jnp.zeros((fill, v0.shape[1]), f32)], axis=0)
        top_s, eid = _topk_rows(cand, P_TOPK, payload=cidx)
        ex = jnp.exp(top_s - top_s[0:1])
        e_rows.append(eid)
        g_rows.append(ex / jnp.sum(ex, axis=0, keepdims=True))
    eidx_t = jnp.concatenate(e_rows, axis=0)
    eidx_t_ref[...] = eidx_t.astype(i32)
    eidx_ref[...] = eidx_t.T.astype(i32)
    gate_ref[...] = jnp.concatenate(g_rows, axis=0).T


def _mid(xp, xp_row0, xs, xs_row0, attp, atts, hp, hs, n_a, n_b, after, w_out, g2, wq, keys):
    n = n_a + n_b
    nbp = n_a // ROW_BLOCK
    assert n_a % ROW_BLOCK == 0 and n_b % ROW_BLOCK == 0 and xp_row0 % ROW_BLOCK == 0 and xs_row0 % ROW_BLOCK == 0
    row = lambda w: pl.BlockSpec((ROW_BLOCK, w), lambda i: (i, 0))
    full = lambda a: pl.BlockSpec(a.shape, lambda i: (0,) * a.ndim)
    wo = w_out.astype(bf16)
    wqb = wq.astype(bf16)
    kb = keys.astype(bf16)
    g = g2.reshape(1, D_MODEL)
    return pl.pallas_call(
        functools.partial(_mid_kernel, nbp=nbp),
        grid=(n // ROW_BLOCK,),
        in_specs=[*_pair_specs(nbp, D_MODEL, xp_row0 // ROW_BLOCK, xs_row0 // ROW_BLOCK), *_pair_specs(nbp, A_WIDTH),
                  *_pair_specs(nbp, M_WIDTH), full(wo), full(g), full(wqb), full(kb),
                  pl.BlockSpec((SUBLANES, P_SLOTS), lambda i: (0, 0))],
        out_specs=[row(D_MODEL), row(D_MODEL), row(P_SLOTS), pl.BlockSpec((P_SLOTS, ROW_BLOCK), lambda i: (0, i)),
                   row(P_SLOTS)],
        out_shape=[jax.ShapeDtypeStruct((n, D_MODEL), f32), jax.ShapeDtypeStruct((n, D_MODEL), f32),
                   jax.ShapeDtypeStruct((n, P_SLOTS), i32), jax.ShapeDtypeStruct((P_SLOTS, n), i32),
                   jax.ShapeDtypeStruct((n, P_SLOTS), f32)],
        compiler_params=pltpu.CompilerParams(dimension_semantics=("parallel",), vmem_limit_bytes=VMEM_LIMIT),
        name="outproj_retrieve",
    )(xp, xs, attp, atts, hp, hs, wo, g, wqb, kb, after)


def _gather_rows(eidx_refs, tab_ref, g_ref, t):
    for i in range(SUBLANES):
        for k, eidx_ref in enumerate(eidx_refs):
            r = k * SUBLANES + i
            g_ref[pl.ds(r * ROW_TILES, ROW_TILES), :] = tab_ref[eidx_ref[i, t]]


def _tile_rows(t, rows=SUBLANES):
    return pl.ds(pl.multiple_of(t * rows, rows), rows)


def _pipelined_tokens(nt, eidx_refs, tab_ref, g0_s, g1_s, compute):
    bufs = (g0_s, g1_s)
    _gather_rows(eidx_refs, tab_ref, g0_s, 0)

    def body(j, carry):
        for u in range(TOKEN_UNROLL):
            t = TOKEN_UNROLL * j + u
            _gather_rows(eidx_refs, tab_ref, bufs[(u + 1) % 2], jnp.minimum(t + 1, nt - 1))
            compute(t, bufs[u % 2])
        return carry

    lax.fori_loop(0, nt // TOKEN_UNROLL, body, 0)


def _gelu_tanh(x):
    return 0.5 * x * (1.0 + jnp.tanh(math.sqrt(2.0 / math.pi) * (x + 0.044715 * (x * x * x))))


def _peer_u_kernel(*refs):
    eidx_refs, (xn_ref, g_ref, tab_ref, w_ref, xl_s, r_s, g0_s, g1_s) = refs[:INDEX_GROUPS], refs[INDEX_GROUPS:]
    nt = xn_ref.shape[0]
    xn = xn_ref[...]
    xh = xn.astype(bf16).astype(f32)
    xl = xn - xh
    for k in range(ROW_TILES):
        xl_s[pl.ds(k, nt, stride=2 * SUBLANES), :] = xh[:, k * LANES:(k + 1) * LANES]
        xl_s[pl.ds(SUBLANES + k, nt, stride=2 * SUBLANES), :] = xl[:, k * LANES:(k + 1) * LANES]
    diag = (lax.broadcasted_iota(i32, (SUBLANES, G_ROWS), 1) % ROW_TILES
            == lax.broadcasted_iota(i32, (SUBLANES, G_ROWS), 0))

    def compute(t, g_s):
        lhs = xl_s[_tile_rows(t, 2 * SUBLANES), :].astype(bf16)
        out = lax.dot_general(lhs, g_s[...], (((1,), (1,)), ((), ())), preferred_element_type=f32)
        part = jnp.where(diag, out[:SUBLANES] + out[SUBLANES:], 0.0)
        for c in range(ROW_TILES):
            r_s[c, _tile_rows(t), :] = part[:, c * LANES:(c + 1) * LANES]

    _pipelined_tokens(nt, eidx_refs, tab_ref, g0_s, g1_s, compute)
    cols = []
    for c in range(ROW_TILES):
        acc = r_s[c, pl.ds(0, nt, stride=SUBLANES), :]
        for k in range(1, SUBLANES):
            acc = acc + r_s[c, pl.ds(k, nt, stride=SUBLANES), :]
        cols.append(acc)
    s = jnp.concatenate(cols, axis=-1)
    fold = (lax.broadcasted_iota(i32, (G_ROWS, P_SLOTS), 0) // ROW_TILES
            == lax.broadcasted_iota(i32, (G_ROWS, P_SLOTS), 1)).astype(bf16)
    sh, sl = _split_bf16(s)
    act = jnp.dot(sh, fold, preferred_element_type=f32) + jnp.dot(sl, fold, preferred_element_type=f32)
    w_ref[...] = g_ref[...] * _gelu_tanh(act)


def _peer_v_kernel(*refs):
    eidx_refs, (w_ref, tab_ref, peer_ref, wl_s, o_s, g0_s, g1_s) = refs[:INDEX_GROUPS], refs[INDEX_GROUPS:]
    nt = w_ref.shape[0]
    spread = (lax.broadcasted_iota(i32, (P_SLOTS, G_ROWS), 1) // ROW_TILES
              == lax.broadcasted_iota(i32, (P_SLOTS, G_ROWS), 0)).astype(bf16)
    wexp = jnp.dot(w_ref[...].astype(bf16), spread, preferred_element_type=f32)
    lane = lax.broadcasted_iota(i32, (nt, LANES), 1)
    for c in range(ROW_TILES):
        wc = wexp[:, c * LANES:(c + 1) * LANES]
        for k in range(SUBLANES):
            wl_s[c, pl.ds(k, nt, stride=SUBLANES), :] = jnp.where(lane % ROW_TILES == k, wc, 0.0)

    def compute(t, g_s):
        lhs = jnp.concatenate([wl_s[c, _tile_rows(t), :] for c in range(ROW_TILES)], axis=-1).astype(bf16)
        o_s[_tile_rows(t), :] = jnp.dot(lhs, g_s[...], preferred_element_type=f32)

    _pipelined_tokens(nt, eidx_refs, tab_ref, g0_s, g1_s, compute)
    for k in range(ROW_TILES):
        peer_ref[:, k * LANES:(k + 1) * LANES] = o_s[pl.ds(k, nt, stride=SUBLANES), :]


def _final_kernel(*refs, starts, nbp):
    k = len(starts)
    peer_refs, x_refs, (gf_ref, yp_ref, ys_ref) = refs[:k], refs[k:2 * k], refs[2 * k:]
    resid = x_refs[0][...] + peer_refs[0][...]
    for start, p_ref, x_ref in zip(starts[1:], peer_refs[1:], x_refs[1:]):
        resid = jnp.where(pl.program_id(0) >= start, x_ref[...] + p_ref[...], resid)
    y = _rms(resid, gf_ref[...])

    @pl.when(pl.program_id(0) < nbp)
    def _():
        yp_ref[...] = y

    @pl.when(pl.program_id(0) >= nbp)
    def _():
        ys_ref[...] = y


def _final(peers, x1s, gf, n_p):
    counts = [x.shape[0] // ROW_BLOCK for x in x1s]
    starts = [sum(counts[:k]) for k in range(len(counts))]
    n, nbp = sum(counts) * ROW_BLOCK, n_p // ROW_BLOCK
    g = gf.reshape(1, D_MODEL)
    seg = [pl.BlockSpec((ROW_BLOCK, D_MODEL), lambda i, s=s, c=c: (jnp.clip(i - s, 0, c - 1), 0))
           for s, c in zip(starts, counts)]
    return pl.pallas_call(
        functools.partial(_final_kernel, starts=tuple(starts), nbp=nbp),
        grid=(n // ROW_BLOCK,),
        in_specs=[*seg, *seg, pl.BlockSpec(g.shape, lambda i: (0, 0))],
        out_specs=list(_pair_specs(nbp, D_MODEL)),
        out_shape=[jax.ShapeDtypeStruct((n_p, D_MODEL), f32), jax.ShapeDtypeStruct((n - n_p, D_MODEL), f32)],
        compiler_params=pltpu.CompilerParams(dimension_semantics=("arbitrary",), vmem_limit_bytes=VMEM_LIMIT),
        name="final_norm",
    )(*peers, *x1s, g)


def _peer_specs():
    row = lambda w: pl.BlockSpec((ROW_BLOCK, w), lambda i: (i, 0))
    idx = [pl.BlockSpec((SUBLANES, ROW_BLOCK), lambda i, k=k: (k, i), memory_space=pltpu.SMEM)
           for k in range(INDEX_GROUPS)]
    tab = pl.BlockSpec(memory_space=pltpu.VMEM)
    gscr = pltpu.VMEM((G_ROWS, LANES), bf16)
    params = pltpu.CompilerParams(dimension_semantics=("arbitrary",), vmem_limit_bytes=VMEM_LIMIT)
    return row, idx, tab, gscr, params


def _expert_table(tab):
    return tab.astype(bf16).reshape(tab.shape[0], ROW_TILES, LANES)


def _peer_u(eidx_t, xn, g, utab, n):
    row, idx, tab, gscr, params = _peer_specs()
    return pl.pallas_call(
        _peer_u_kernel,
        grid=(n // ROW_BLOCK,),
        in_specs=[*idx, row(D_MODEL), row(P_SLOTS), tab],
        out_specs=row(P_SLOTS),
        out_shape=jax.ShapeDtypeStruct((n, P_SLOTS), f32),
        scratch_shapes=[pltpu.VMEM((ROW_BLOCK * 2 * SUBLANES, LANES), f32),
                        pltpu.VMEM((ROW_TILES, ROW_BLOCK * SUBLANES, LANES), f32), gscr, gscr],
        compiler_params=params,
        name="peer_u",
    )(*[eidx_t] * INDEX_GROUPS, xn, g, utab)


def _peer_v(eidx_t, w, vtab):
    n = w.shape[0]
    row, idx, tab, gscr, params = _peer_specs()
    return pl.pallas_call(
        _peer_v_kernel,
        grid=(n // ROW_BLOCK,),
        in_specs=[*idx, row(P_SLOTS), tab],
        out_specs=row(D_MODEL),
        out_shape=jax.ShapeDtypeStruct((n, D_MODEL), f32),
        scratch_shapes=[pltpu.VMEM((ROW_TILES, ROW_BLOCK * SUBLANES, LANES), f32),
                        pltpu.VMEM((ROW_BLOCK * SUBLANES, LANES), f32), gscr, gscr],
        compiler_params=params,
        name="peer_v",
    )(*[eidx_t] * INDEX_GROUPS, w, vtab)


def _sc_table_kernel(t_ref, o_ref):
    t = t_ref[...]
    o_ref[...] = pltpu.pack_elementwise([t[:, :SC_WORDS], t[:, SC_WORDS:]], packed_dtype=bf16)


def _sc_table(tab):
    e = tab.shape[0]
    return pl.pallas_call(
        _sc_table_kernel,
        grid=(e // ROW_BLOCK,),
        in_specs=[pl.BlockSpec((ROW_BLOCK, D_MODEL), lambda i: (i, 0))],
        out_specs=pl.BlockSpec((ROW_BLOCK, SC_WORDS), lambda i: (i, 0)),
        out_shape=jax.ShapeDtypeStruct((e, SC_WORDS), jnp.uint32),
        compiler_params=pltpu.CompilerParams(dimension_semantics=("parallel",), vmem_limit_bytes=VMEM_LIMIT),
        name="sc_table",
    )(tab)


def _sc_unpack(words):
    return plsc.unpack(plsc.bitcast(words, bf16), format=plsc.PackFormat.INTERLEAVED)


def _sc_gelu_tanh(x):
    z = math.sqrt(2.0 / math.pi) * (x + 0.044715 * (x * x * x))
    return 0.5 * x * (2.0 - 2.0 / (jnp.exp(2.0 * z) + 1.0))


def _sc_peer(utab32, vtab32, eidx, xn, gate, after):
    n_sc = eidx.shape[0]
    per = n_sc // SC_WORKERS
    units = 4 * per
    assert n_sc % (SC_WORKERS * SUBLANES) == 0

    def body(u_hbm, v_hbm, eidx_hbm, x_hbm, g_hbm, after_hbm, out_hbm,
             idx_v, x_v, g_v, rows_v, acc_v, w_v, w16_v, out_v, row_sems, tok_sems, out_sem):
        del after_hbm
        base = (lax.axis_index("s") * 2 + lax.axis_index("c")) * per
        lanes = lax.broadcasted_iota(i32, (SC_LANES,), 0)

        def token_copies(tok):
            tslot = tok % 2
            return (pltpu.make_async_copy(eidx_hbm.at[base + tok], idx_v.at[tslot], tok_sems.at[0]),
                    pltpu.make_async_copy(x_hbm.at[base + tok], x_v.at[tslot], tok_sems.at[1]),
                    pltpu.make_async_copy(g_hbm.at[base + tok], g_v.at[tslot], tok_sems.at[2]))

        def store_out(tok):
            return pltpu.make_async_copy(out_v, out_hbm.at[base + tok], out_sem)

        def gather(tab_hbm, g):
            tok, k = g // 4, g % 4
            return pltpu.make_async_copy(tab_hbm.at[idx_v.at[tok % 2, pl.ds((k % 2) * SC_UNIT_ROWS, SC_UNIT_ROWS)]],
                                         rows_v.at[k % 2], row_sems.at[k % 2])

        def start(g):
            @pl.when(g % 4 < 2)
            def _():
                gather(u_hbm, g).start()

            @pl.when(g % 4 >= 2)
            def _():
                gather(v_hbm, g).start()

        def compute_u(tslot, half, rows):
            @pl.loop(0, SC_UNIT_ROWS // SC_ROW_GROUP)
            def _(rg):
                slot0 = half * SC_UNIT_ROWS + rg * SC_ROW_GROUP

                @pl.loop(0, SC_CHUNKS // SC_CHUNK_GROUP)
                def _(cg):
                    keep = jnp.where(cg > 0, 1.0, 0.0).astype(f32)
                    accs = [acc_v[pl.ds((slot0 + r) * SC_LANES, SC_LANES)] * keep for r in range(SC_ROW_GROUP)]
                    for c in range(SC_CHUNK_GROUP):
                        ch = cg * SC_CHUNK_GROUP + c
                        xa = x_v[tslot, pl.ds(ch * SC_LANES, SC_LANES)]
                        xb = x_v[tslot, pl.ds(SC_WORDS + ch * SC_LANES, SC_LANES)]
                        for r in range(SC_ROW_GROUP):
                            a, b = _sc_unpack(rows[rg * SC_ROW_GROUP + r, pl.ds(ch * SC_LANES, SC_LANES)])
                            accs[r] = accs[r] + a * xa + b * xb
                    for r in range(SC_ROW_GROUP):
                        acc_v[pl.ds((slot0 + r) * SC_LANES, SC_LANES)] = accs[r]

        def gate_weights(tslot):
            @pl.loop(0, P_SLOTS // SC_LANES)
            def _(sg):
                first = sg * SC_LANES * SC_LANES
                act = jnp.zeros((SC_LANES,), f32)
                for lane in range(SC_LANES):
                    act = act + plsc.load_gather(acc_v, [first + lanes * SC_LANES + lane])
                w_v[pl.ds(sg * SC_LANES, SC_LANES)] = g_v[tslot, pl.ds(sg * SC_LANES, SC_LANES)] * _sc_gelu_tanh(act)

            @pl.loop(0, P_SLOTS // SC_LANES)
            def _(sg):
                for r in range(SC_LANES):
                    w16_v[pl.ds((sg * SC_LANES + r) * SC_LANES, SC_LANES)] = plsc.load_gather(
                        w_v, [jnp.zeros((SC_LANES,), i32) + (sg * SC_LANES + r)])

        def compute_v(half, rows):
            @pl.loop(0, SC_CHUNKS // SC_CHUNK_GROUP)
            def _(cg):
                first = cg * SC_CHUNK_GROUP

                @pl.loop(0, SC_UNIT_ROWS // SC_ROW_GROUP)
                def _(rg):
                    keep = jnp.where(jnp.logical_or(half == 1, rg > 0), 1.0, 0.0).astype(f32)
                    spots = [pl.ds((j % 2) * SC_WORDS + (first + j // 2) * SC_LANES, SC_LANES)
                             for j in range(2 * SC_CHUNK_GROUP)]
                    accs = [out_v[spot] * keep for spot in spots]
                    for r in range(SC_ROW_GROUP):
                        row = rg * SC_ROW_GROUP + r
                        wv = w16_v[pl.ds((half * SC_UNIT_ROWS + row) * SC_LANES, SC_LANES)]
                        for c in range(SC_CHUNK_GROUP):
                            a, b = _sc_unpack(rows[row, pl.ds((first + c) * SC_LANES, SC_LANES)])
                            accs[2 * c] = accs[2 * c] + a * wv
                            accs[2 * c + 1] = accs[2 * c + 1] + b * wv
                    for spot, acc in zip(spots, accs):
                        out_v[spot] = acc

        for cp in token_copies(0):
            cp.start()
        for cp in token_copies(0):
            cp.wait()
        start(0)

        @pl.loop(0, units)
        def _(g):
            tok, k = g // 4, g % 4

            @pl.when(jnp.logical_and(k == 0, tok + 1 < per))
            def _():
                for cp in token_copies(tok + 1):
                    cp.start()

            @pl.when(jnp.logical_and(k == 3, tok + 1 < per))
            def _():
                for cp in token_copies(tok + 1):
                    cp.wait()

            @pl.when(g + 1 < units)
            def _():
                start(g + 1)

            gather(u_hbm, g).wait()
            rows = rows_v.at[k % 2]

            @pl.when(k < 2)
            def _():
                compute_u(tok % 2, k, rows)

            @pl.when(k == 1)
            def _():
                gate_weights(tok % 2)

            @pl.when(jnp.logical_and(k == 2, tok > 0))
            def _():
                store_out(tok - 1).wait()

            @pl.when(k >= 2)
            def _():
                compute_v(k - 2, rows)

            @pl.when(k == 3)
            def _():
                store_out(tok).start()

        store_out(per - 1).wait()

    return pl.kernel(
        body, mesh=plsc.VectorSubcoreMesh(core_axis_name="c", subcore_axis_name="s"),
        out_type=jax.ShapeDtypeStruct((n_sc, D_MODEL), f32),
        scratch_types=[pltpu.VMEM((2, P_SLOTS), i32), pltpu.VMEM((2, D_MODEL), f32), pltpu.VMEM((2, P_SLOTS), f32),
                       pltpu.VMEM((2, SC_UNIT_ROWS, SC_WORDS), jnp.uint32), pltpu.VMEM((P_SLOTS * SC_LANES,), f32),
                       pltpu.VMEM((P_SLOTS,), f32), pltpu.VMEM((P_SLOTS * SC_LANES,), f32), pltpu.VMEM((D_MODEL,), f32),
                       pltpu.SemaphoreType.DMA((2,)), pltpu.SemaphoreType.DMA((3,)), pltpu.SemaphoreType.DMA],
        compiler_params=pltpu.CompilerParams(needs_layout_passes=False),
        name="sc_peer",
    )(utab32, vtab32, eidx, xn, gate, after)


def kernel(x_prompt, x_sample, cache_k, cache_v, state_C, state_n, state_m, state_conv, norm1_g, w_in, b_gates, rel_bias, conv_w, conv_b, mh_norm_g, w_out, norm2_g, peer_wq, peer_keys, peer_u, peer_v, final_g):
    bp, sp, d = x_prompt.shape
    bs, ts, _ = x_sample.shape
    n_p, n_s = bp * sp, bs * ts
    n = n_p + n_s
    assert n_p % ROW_BLOCK == 0 and n_s % ROW_BLOCK == 0 and d == D_MODEL
    depth = w_in.shape[0]
    assert depth == 1, "the final norm is fused into the last layer's PEER pass"
    l = 0
    xp, xs = x_prompt.reshape(n_p, d), x_sample.reshape(n_s, d)

    zeros = lambda *shp: jnp.zeros(shp, f32)
    mparams = (conv_w[l], conv_b[l], b_gates[l], mh_norm_g[l])
    retrieval = (w_out[l], norm2_g[l], peer_wq[l], peer_keys[l])

    def mixers(proj, seq, lo, hi, state):
        aq, ak, av, mqk, mv, mo, gates = proj
        att = _attn_prompt(aq, ak, av, rel_bias[l], seq, sp, lo, hi)
        h, c, nn, mm = _mlstm(mqk, mv, mo, gates, seq * sp + lo, 1, hi - lo, *state, *mparams)
        conv_rows = mqk[seq * sp + hi - (CONV_W - 1):seq * sp + hi][None]
        return att, h, (c, nn[:, :M_HEADS], mm[:, :M_HEADS, 0], conv_rows)

    fresh = (zeros(1, M_HEADS, M_DH, M_DH), zeros(1, M_HEADS, M_DH), zeros(1, M_HEADS), zeros(1, CONV_W - 1, 2 * M_WIDTH))

    states = [fresh] * bp

    def stage(proj, lo, hi, after, tail=None):
        parts = []
        for seq in range(bp):
            a, b = max(lo, seq * sp), min(hi, (seq + 1) * sp)
            if a < b:
                att, h, states[seq] = mixers(proj, seq, a - seq * sp, b - seq * sp, states[seq])
                parts.append((att, h, b - a, xp, a))
        parts += [tail] if tail is not None else []
        assert 1 <= len(parts) <= 2, "the retrieval kernel reads at most two row sources"
        (att0, h0, n0, x0, r0), (att1, h1, n1, x1, r1) = parts[0], parts[-1]
        n1, r1 = (n1, r1) if len(parts) == 2 else (0, 0)
        return _mid(x0, r0, x1, r1, att0, att1, h0, h1, n0, n1, after, *retrieval)

    sc_tables = _sc_table(peer_u[l]), _sc_table(peer_v[l])
    peers, x1s, lo, after = [], [], 0, jnp.zeros((SUBLANES, P_SLOTS), i32)
    proj = _inproj(xp, SC_STAGE_ENDS[0], xs, norm1_g[l], w_in[l], after)
    for hi in SC_STAGE_ENDS:
        x1, xn, eidx, _, gate = stage(proj, lo, hi, after)
        peers.append(_sc_peer(*sc_tables, eidx, xn, gate, peers[-1] if peers else zeros(SUBLANES, D_MODEL)))
        x1s.append(x1)
        if lo == 0:
            proj = _inproj(xp, n_p, xs, norm1_g[l], w_in[l], eidx)
        lo, after = hi, eidx
    aq, ak, av, mqk, mv, mo, gates = proj

    lcache = cache_k.shape[2]
    att_s = _attn_sample(aq, ak, av, cache_k[l].reshape(bs, lcache, A_WIDTH),
                         cache_v[l].reshape(bs, lcache, A_WIDTH), rel_bias[l], n_p, bs, ts)
    h_s, c_s, nn_s, mm_s = _mlstm(mqk, mv, mo, gates, n_p, bs, ts, state_C[l], state_n[l], state_m[l],
                                  state_conv[l], *mparams)
    x1, xn, _, eidx_t, gate = stage(proj, lo, n_p, after, tail=(att_s, h_s, n_s, xs, 0))
    w = _peer_u(eidx_t, xn, gate, _expert_table(peer_u[l]), n - lo)
    peers.append(_peer_v(eidx_t, w, _expert_table(peer_v[l])))
    x1s.append(x1)
    y_p, y_s = _final(peers, x1s, final_g, n_p)
    c_p, nn_p, mm_p = (jnp.concatenate(per_seq, axis=0) for per_seq in zip(*(st[:3] for st in states)))

    def tail(a, row0, bsz, t, keep):
        return jnp.stack([a[row0 + (b + 1) * t - keep:row0 + (b + 1) * t] for b in range(bsz)])

    keep = min(WINDOW, sp)
    heads = lambda a: a.reshape(a.shape[0], a.shape[1], A_HEADS, A_DH)
    ctail = CONV_W - 1
    conv_tail = lambda buf, a, row0, bsz, t: jnp.concatenate([buf.astype(a.dtype), tail(a, row0, bsz, t, min(ctail, t))],
                                                             axis=1)[:, -ctail:]
    st = lambda a: a[None]
    return (y_p.reshape(bp, sp, d), y_s.reshape(bs, ts, d),
            st(heads(tail(ak, 0, bp, sp, keep))), st(heads(tail(av, 0, bp, sp, keep))),
            st(c_p), st(nn_p), st(mm_p),
            st(conv_tail(zeros(bp, ctail, 2 * M_WIDTH), mqk, 0, bp, sp)),
            st(heads(ak[n_p:].reshape(bs, ts, A_WIDTH))), st(heads(av[n_p:].reshape(bs, ts, A_WIDTH))),
            st(c_s), st(nn_s[:, :M_HEADS]), st(mm_s[:, :M_HEADS, 0]),
            st(conv_tail(state_conv[l], mqk, n_p, bs, ts)))
```

```python
import functools
import math

import jax
import jax.numpy as jnp
from jax import lax
from jax.experimental import pallas as pl
from jax.experimental.pallas import tpu as pltpu
from jax.experimental.pallas import tpu_sc as plsc

f32 = jnp.float32
bf16 = jnp.bfloat16
i32 = jnp.int32

D_MODEL = 1024
CHUNK = 64
A_HEADS = 8
A_DH = 64
A_WIDTH = A_HEADS * A_DH
BAND_CHUNKS = 8
WINDOW = BAND_CHUNKS * CHUNK
MAX_REL = 128
ATT_SCALE = A_DH ** -0.5
M_HEADS = 4
M_DH = 128
M_WIDTH = M_HEADS * M_DH
CONV_W = 4
IN_WIDTHS = (A_WIDTH, A_WIDTH, A_WIDTH, 2 * M_WIDTH, M_WIDTH, M_WIDTH)
P_HEADS = 8
P_DKEY = 256
N_KEYS = 128
P_TOPK = 16
P_SLOTS = P_HEADS * P_TOPK
EPS = 1e-6
NEG = -1e30

LANES = 128
SUBLANES = 8
ROW_BLOCK = 256
ATT_TILE = 512
ATT_SUB = 128
ATT_KEYS = ATT_SUB + WINDOW
ROW_TILES = D_MODEL // LANES
G_ROWS = P_SLOTS * ROW_TILES
TOKEN_UNROLL = 16
INDEX_GROUPS = P_SLOTS // SUBLANES
VMEM_LIMIT = 56 * 1024 * 1024

SC_WORKERS = 32
SC_LANES = 16
SC_UNIT_ROWS = P_SLOTS // 2
SC_ROW_GROUP = 16
SC_WORDS = D_MODEL // 2
SC_CHUNKS = SC_WORDS // SC_LANES
SC_CHUNK_GROUP = 8
SC_STAGE_ENDS = (2048, 6144, 20480)


def _rms(x, g):
    return x * lax.rsqrt(jnp.mean(x * x, axis=-1, keepdims=True) + EPS) * g


def _split_bf16(x):
    hi = x.astype(bf16)
    lo = (x - hi.astype(f32)).astype(bf16)
    return hi, lo


def _pair_specs(nbp, width, first=0, second=0):
    return (pl.BlockSpec((ROW_BLOCK, width), lambda i: (first + jnp.minimum(i, nbp - 1), 0)),
            pl.BlockSpec((ROW_BLOCK, width), lambda i: (second + jnp.maximum(i - nbp, 0), 0)))


def _pick(nbp, p_ref, s_ref):
    return jnp.where(pl.program_id(0) < nbp, p_ref[...], s_ref[...])


def _inproj_kernel(xp_ref, xs_ref, g_ref, w_ref, wgh_ref, wgl_ref, after_ref,
                   aq_ref, ak_ref, av_ref, mqk_ref, mv_ref, mo_ref, gate_ref, *, nbp):
    del after_ref
    xn = _rms(_pick(nbp, xp_ref, xs_ref), g_ref[...])
    xh, xl = _split_bf16(xn)

    lo = 0
    for out_ref, width in zip((aq_ref, ak_ref, av_ref, mqk_ref, mv_ref, mo_ref), IN_WIDTHS):
        out_ref[...] = jnp.dot(xh, w_ref[:, lo:lo + width], preferred_element_type=f32)
        lo += width
    gate_ref[...] = (jnp.dot(xh, wgh_ref[...], preferred_element_type=f32)
                     + jnp.dot(xl, wgh_ref[...], preferred_element_type=f32)
                     + jnp.dot(xh, wgl_ref[...], preferred_element_type=f32))


def _inproj(xp, n_p, xs, g1, w_in, after):
    n = n_p + xs.shape[0]
    nbp = n_p // ROW_BLOCK
    main = sum(IN_WIDTHS)
    w_main = w_in[:, :main].astype(bf16)
    wg = jnp.pad(w_in[:, main:], ((0, 0), (0, LANES - 2 * M_HEADS)))
    wgh, wgl = _split_bf16(wg)
    widths = IN_WIDTHS + (LANES,)
    row = lambda w: pl.BlockSpec((ROW_BLOCK, w), lambda i: (i, 0))
    full = lambda a: pl.BlockSpec(a.shape, lambda i: (0,) * a.ndim)
    g = g1.reshape(1, D_MODEL)
    return pl.pallas_call(
        functools.partial(_inproj_kernel, nbp=nbp),
        grid=(n // ROW_BLOCK,),
        in_specs=[*_pair_specs(nbp, D_MODEL), full(g), full(w_main), full(wgh), full(wgl),
                  pl.BlockSpec((SUBLANES, P_SLOTS), lambda i: (0, 0))],
        out_specs=[row(w) for w in widths],
        out_shape=[jax.ShapeDtypeStruct((n, w), f32) for w in widths],
        compiler_params=pltpu.CompilerParams(dimension_semantics=("parallel",), vmem_limit_bytes=VMEM_LIMIT),
        name="inproj",
    )(xp, xs, g, w_main, wgh, wgl, after)


def _attn_heads(q, k, v, bias_ref, key_ok):
    outs = []
    for h in range(A_HEADS):
        sl = slice(h * A_DH, (h + 1) * A_DH)
        s = lax.dot_general(q[:, sl], k[:, sl], (((1,), (1,)), ((), ())), preferred_element_type=f32)
        s = s * ATT_SCALE + bias_ref[h]
        if key_ok is not None:
            s = jnp.where(key_ok, s, NEG)
        m = jnp.max(s, axis=-1, keepdims=True)
        p = jnp.exp(s - m)
        l = jnp.sum(p, axis=-1, keepdims=True)
        o = jnp.dot(p.astype(bf16), v[:, sl], preferred_element_type=f32)
        outs.append(o / l)
    return jnp.concatenate(outs, axis=-1)


def _attn_prompt_kernel(q_ref, k0_ref, k1_ref, v0_ref, v1_ref, bias_ref, o_ref, *, t0):
    t = t0 + pl.program_id(1)
    q = q_ref[...].astype(bf16)
    k = jnp.concatenate([k0_ref[...], k1_ref[...]], axis=0).astype(bf16)
    v = jnp.concatenate([v0_ref[...], v1_ref[...]], axis=0).astype(bf16)
    col = lax.broadcasted_iota(i32, (1, ATT_KEYS), 1)
    for s in range(ATT_TILE // ATT_SUB):
        lo = s * ATT_SUB
        key_ok = (t * ATT_TILE + lo + col) >= WINDOW
        o_ref[lo:lo + ATT_SUB, :] = _attn_heads(q[lo:lo + ATT_SUB], k[lo:lo + ATT_KEYS], v[lo:lo + ATT_KEYS],
                                                 bias_ref, key_ok)


def _attn_sample_kernel(q_ref, k_ref, v_ref, bias_ref, o_ref):
    o_ref[...] = _attn_heads(q_ref[...].astype(bf16), k_ref[0].astype(bf16), v_ref[0].astype(bf16), bias_ref, None)


def _rel_bias_table(rel_bias, rows, cols, offset, valid):
    span = rows + cols - 1
    rel = offset + rows - 1 - jnp.arange(span)
    diag = rel_bias[:, jnp.clip(rel, -MAX_REL, MAX_REL) + MAX_REL].astype(f32)
    diag = jnp.pad(diag, ((0, 0), (0, 1)))
    flat = jnp.tile(diag, (1, rows))[:, rows - 1:rows - 1 + rows * span]
    return jnp.where(valid[None], flat.reshape(-1, rows, span)[:, :, :cols], NEG)


def _attn_prompt(q, k, v, rel_bias, seq, s, lo, hi):
    assert s % ATT_TILE == 0 and WINDOW == ATT_TILE and lo % ATT_TILE == 0 and hi % ATT_TILE == 0
    nt, t0, cnt = s // ATT_TILE, lo // ATT_TILE, (hi - lo) // ATT_TILE
    i = jnp.arange(ATT_SUB)[:, None]
    j = jnp.arange(ATT_KEYS)[None, :]
    off = j - (i // CHUNK) * CHUNK
    bias = _rel_bias_table(rel_bias, ATT_SUB, ATT_KEYS, WINDOW, (off >= 0) & (off < WINDOW + CHUNK))
    cur = pl.BlockSpec((ATT_TILE, A_WIDTH), lambda b, t: (seq * nt + t0 + t, 0))
    prev = pl.BlockSpec((ATT_TILE, A_WIDTH), lambda b, t: (seq * nt + jnp.maximum(t0 + t - 1, 0), 0))
    return pl.pallas_call(
        functools.partial(_attn_prompt_kernel, t0=t0),
        grid=(1, cnt),
        in_specs=[cur, prev, cur, prev, cur, pl.BlockSpec(bias.shape, lambda b, t: (0, 0, 0))],
        out_specs=pl.BlockSpec((ATT_TILE, A_WIDTH), lambda b, t: (t, 0)),
        out_shape=jax.ShapeDtypeStruct((hi - lo, A_WIDTH), f32),
        compiler_params=pltpu.CompilerParams(dimension_semantics=("parallel", "parallel"),
                                             vmem_limit_bytes=VMEM_LIMIT),
        name="attn_prompt",
    )(q, k, k, v, v, bias)


def _attn_sample(q, k, v, ck, cv, rel_bias, row0, bsz, t):
    l = ck.shape[1]
    assert row0 % t == 0
    keys = -(-(l + t) // LANES) * LANES
    padk = ((0, 0), (0, keys - l - t), (0, 0))
    kk = jnp.pad(jnp.concatenate([ck, k[row0:].reshape(bsz, t, A_WIDTH)], axis=1), padk)
    vv = jnp.pad(jnp.concatenate([cv, v[row0:].reshape(bsz, t, A_WIDTH)], axis=1), padk)
    j = jnp.arange(keys)[None, :]
    bias = _rel_bias_table(rel_bias, t, keys, l, jnp.broadcast_to(j < l + t, (t, keys)))
    return pl.pallas_call(
        _attn_sample_kernel,
        grid=(bsz,),
        in_specs=[pl.BlockSpec((t, A_WIDTH), lambda b: (row0 // t + b, 0)),
                  pl.BlockSpec((1, keys, A_WIDTH), lambda b: (b, 0, 0)),
                  pl.BlockSpec((1, keys, A_WIDTH), lambda b: (b, 0, 0)),
                  pl.BlockSpec(bias.shape, lambda b: (0, 0, 0))],
        out_specs=pl.BlockSpec((t, A_WIDTH), lambda b: (b, 0)),
        out_shape=jax.ShapeDtypeStruct((bsz * t, A_WIDTH), f32),
        compiler_params=pltpu.CompilerParams(dimension_semantics=("parallel",), vmem_limit_bytes=VMEM_LIMIT),
        name="attn_sample",
    )(q, kk, vv, bias)


def _mlstm_chunk(a, vall, o_in, gate_in, cw_ref, cb_ref, bg_ref, mhg_ref, c_s, n_s, m_s, prev_s):
    lc = a.shape[0]
    ext = jnp.concatenate([prev_s[...], a], axis=0)
    conv = cb_ref[...]
    for j in range(CONV_W):
        lo = SUBLANES - (CONV_W - 1) + j
        conv = conv + cw_ref[j:j + 1, :] * ext[lo:lo + lc]
    prev_s[...] = a[lc - SUBLANES:lc]
    qk = conv * jax.nn.sigmoid(conv)

    z = gate_in + bg_ref[...]
    lane = lax.broadcasted_iota(i32, (lc, LANES), 1)
    row = lax.broadcasted_iota(i32, (lc, LANES), 0)
    logf = jnp.minimum(z, 0.0) - jnp.log1p(jnp.exp(-jnp.abs(z)))
    cum = jnp.where((lane >= M_HEADS) & (lane < 2 * M_HEADS), logf, 0.0)
    shift = 1
    while shift < lc:
        cum = cum + jnp.where(row >= shift, pltpu.roll(cum, shift, axis=0), 0.0)
        shift *= 2
    zc = jnp.where(lane < M_HEADS, z, cum)
    zt = jnp.concatenate([zc, jnp.zeros((LANES - lc, LANES), f32)], axis=0).T[:, :lc]

    ri = lax.broadcasted_iota(i32, (lc, lc), 0)
    ci = lax.broadcasted_iota(i32, (lc, lc), 1)
    causal = ri >= ci
    state = [(m_s[h:h + 1, 0:1], c_s[h], n_s[h:h + 1, :]) for h in range(M_HEADS)]
    hs, new_state = [], []
    for h in range(M_HEADS):
        sl = slice(h * M_DH, (h + 1) * M_DH)
        q = qk[:, sl]
        k = qk[:, M_WIDTH + h * M_DH:M_WIDTH + (h + 1) * M_DH] * (M_DH ** -0.5)
        v = vall[:, sl]
        i_col = zc[:, h:h + 1]
        b_col = zc[:, M_HEADS + h:M_HEADS + h + 1]
        i_row = zt[h:h + 1, :]
        b_row = zt[M_HEADS + h:M_HEADS + h + 1, :]
        m_prev, c_prev, n_prev = state[h]

        dmat = jnp.where(causal, b_col - b_row + i_row, NEG)
        inter = b_col + m_prev
        mt = jnp.maximum(jnp.max(dmat, axis=-1, keepdims=True), inter)
        w_intra = jnp.exp(dmat - mt)
        w_inter = jnp.exp(inter - mt)
        qb, kb, vb = q.astype(bf16), k.astype(bf16), v.astype(bf16)
        s = lax.dot_general(qb, kb, (((1,), (1,)), ((), ())), preferred_element_type=f32) * w_intra
        num = (w_inter * jnp.dot(qb, c_prev.astype(bf16), preferred_element_type=f32)
               + jnp.dot(s.astype(bf16), vb, preferred_element_type=f32))
        den = w_inter * jnp.sum(q * n_prev, axis=-1, keepdims=True) + jnp.sum(s, axis=-1, keepdims=True)
        hh = num / jnp.maximum(jnp.abs(den), jnp.exp(-mt))
        m_new = mt[lc - 1:lc, :]
        b_last = b_col[lc - 1:lc, :]
        w_s = jnp.exp(b_last - b_col + i_col - m_new)
        decay = jnp.exp(b_last + m_prev - m_new)
        kw = k * w_s
        new_state.append((jnp.broadcast_to(m_new, (1, LANES)),
                          decay * c_prev + lax.dot_general(kw.astype(bf16), vb, (((0,), (0,)), ((), ())),
                                                           preferred_element_type=f32),
                          decay * n_prev + jnp.sum(kw, axis=0, keepdims=True)))
        hs.append(hh * lax.rsqrt(jnp.mean(hh * hh, axis=-1, keepdims=True) + EPS))

    for h, (m_new, c_new, n_new) in enumerate(new_state):
        m_s[h:h + 1, :] = m_new
        c_s[h] = c_new
        n_s[h:h + 1, :] = n_new
    return jnp.concatenate(hs, axis=-1) * mhg_ref[...] * jax.nn.sigmoid(o_in)


def _mlstm_kernel(qk_ref, v_ref, o_ref, gate_ref, c0_ref, n0_ref, m0_ref, cbuf_ref,
                  cw_ref, cb_ref, bg_ref, mhg_ref,
                  h_ref, cout_ref, nout_ref, mout_ref,
                  c_s, n_s, m_s, prev_s):
    c = pl.program_id(1)

    @pl.when(c == 0)
    def _():
        c_s[...] = c0_ref[0]
        n_s[...] = n0_ref[0]
        m_s[...] = m0_ref[0]
        prev_s[...] = cbuf_ref[0]

    h_ref[...] = _mlstm_chunk(qk_ref[...], v_ref[...], o_ref[...], gate_ref[...],
                              cw_ref, cb_ref, bg_ref, mhg_ref, c_s, n_s, m_s, prev_s)

    @pl.when(c == pl.num_programs(1) - 1)
    def _():
        cout_ref[0] = c_s[...]
        nout_ref[0] = n_s[...]
        mout_ref[0] = m_s[...]


def _mlstm(mqk, mv, mo, gates, row0, bsz, t, c0, n0, m0, cbuf, conv_w, conv_b, b_gates, mh_g):
    lc = min(CHUNK, t)
    step = lc
    nc = t // step
    assert t % step == 0 and lc % SUBLANES == 0 and row0 % step == 0
    n0p = jnp.pad(n0.astype(f32), ((0, 0), (0, SUBLANES - M_HEADS), (0, 0)))
    m0p = jnp.pad(jnp.broadcast_to(m0.astype(f32)[:, :, None], (bsz, M_HEADS, LANES)),
                  ((0, 0), (0, SUBLANES - M_HEADS), (0, 0)))
    cbp = jnp.pad(cbuf.astype(f32), ((0, 0), (SUBLANES - (CONV_W - 1), 0), (0, 0)))
    bg = jnp.pad(b_gates.astype(f32), (0, LANES - 2 * M_HEADS)).reshape(1, LANES)
    seq = lambda w: pl.BlockSpec((step, w), lambda b, c: (row0 // step + b * nc + c, 0))
    out_seq = pl.BlockSpec((step, M_WIDTH), lambda b, c: (b * nc + c, 0))
    per_b = lambda shp: pl.BlockSpec((1,) + shp, lambda b, c: (b,) + (0,) * len(shp))
    full = lambda a: pl.BlockSpec(a.shape, lambda b, c: (0,) * a.ndim)
    cb = conv_b.reshape(1, -1)
    mhg = mh_g.reshape(1, -1)
    return pl.pallas_call(
        _mlstm_kernel,
        grid=(bsz, nc),
        in_specs=[seq(2 * M_WIDTH), seq(M_WIDTH), seq(M_WIDTH), seq(LANES),
                  per_b((M_HEADS, M_DH, M_DH)), per_b((SUBLANES, M_DH)), per_b((SUBLANES, LANES)),
                  per_b((SUBLANES, 2 * M_WIDTH)),
                  full(conv_w), full(cb), full(bg), full(mhg)],
        out_specs=[out_seq, per_b((M_HEADS, M_DH, M_DH)), per_b((SUBLANES, M_DH)), per_b((SUBLANES, LANES))],
        out_shape=[jax.ShapeDtypeStruct((bsz * t, M_WIDTH), f32),
                   jax.ShapeDtypeStruct((bsz, M_HEADS, M_DH, M_DH), f32),
                   jax.ShapeDtypeStruct((bsz, SUBLANES, M_DH), f32),
                   jax.ShapeDtypeStruct((bsz, SUBLANES, LANES), f32)],
        scratch_shapes=[pltpu.VMEM((M_HEADS, M_DH, M_DH), f32), pltpu.VMEM((SUBLANES, M_DH), f32),
                        pltpu.VMEM((SUBLANES, LANES), f32), pltpu.VMEM((SUBLANES, 2 * M_WIDTH), f32)],
        compiler_params=pltpu.CompilerParams(dimension_semantics=("parallel", "arbitrary"),
                                             vmem_limit_bytes=VMEM_LIMIT),
        name="mlstm",
    )(mqk, mv, mo, gates, c0.astype(f32), n0p, m0p, cbp, conv_w, cb, bg, mhg)


def _topk_rows(s, k, payload=None):
    n = s.shape[0]
    rows = lax.broadcasted_iota(i32, s.shape, 0).astype(f32)
    vals, ids = [], []
    for _ in range(k):
        m = jnp.max(s, axis=0, keepdims=True)
        pos = jnp.min(jnp.where(s == m, rows, float(n)), axis=0, keepdims=True)
        sel = rows == pos
        vals.append(m)
        ids.append(pos if payload is None else jnp.max(jnp.where(sel, payload, -1.0), axis=0, keepdims=True))
        s = jnp.where(sel, -jnp.inf, s)
    return jnp.concatenate(vals, axis=0), jnp.concatenate(ids, axis=0)


def _mid_kernel(xp_ref, xs_ref, attp_ref, atts_ref, hp_ref, hs_ref, wo_ref, g2_ref, wq_ref, keys_ref, after_ref,
                x1_ref, xn_ref, eidx_ref, eidx_t_ref, gate_ref, *, nbp):
    del after_ref
    cat = jnp.concatenate([_pick(nbp, attp_ref, atts_ref), _pick(nbp, hp_ref, hs_ref)], axis=-1).astype(bf16)
    x1 = _pick(nbp, xp_ref, xs_ref) + jnp.dot(cat, wo_ref[...], preferred_element_type=f32)
    x1_ref[...] = x1
    xn = _rms(x1, g2_ref[...])
    xn_ref[...] = xn
    xb = xn.astype(bf16)
    e_rows, g_rows = [], []
    for h in range(P_HEADS):
        q = jnp.dot(xb, wq_ref[:, h * P_DKEY:(h + 1) * P_DKEY], preferred_element_type=f32).astype(bf16)
        half = []
        for p in range(2):
            st = lax.dot_general(keys_ref[h, p], q[:, p * N_KEYS:(p + 1) * N_KEYS], (((1,), (1,)), ((), ())),
                                 preferred_element_type=f32)
            half.append(_topk_rows(st, P_TOPK))
        (v0, i0), (v1, i1) = half
        width = [P_TOPK // (a + 1) for a in range(P_TOPK)]
        fill = -sum(width) % SUBLANES
        cand = jnp.concatenate([v0[a:a + 1] + v1[:width[a]] for a in range(P_TOPK)]
                               + [jnp.full((fill, v0.shape[1]), -jnp.inf, f32)], axis=0)
        cidx = jnp.concatenate([i0[a:a + 1] * float(N_KEYS) + i1[:width[a]] for a in range(P_TOPK)]
                               + [jnp.zeros((fill, v0.shape[1]), f32)], axis=0)
        top_s, eid = _topk_rows(cand, P_TOPK, payload=cidx)
        ex = jnp.exp(top_s - top_s[0:1])
        e_rows.append(eid)
        g_rows.append(ex / jnp.sum(ex, axis=0, keepdims=True))
    eidx_t = jnp.concatenate(e_rows, axis=0)
    eidx_t_ref[...] = eidx_t.astype(i32)
    eidx_ref[...] = eidx_t.T.astype(i32)
    gate_ref[...] = jnp.concatenate(g_rows, axis=0).T


def _mid(xp, xp_row0, xs, xs_row0, attp, atts, hp, hs, n_a, n_b, after, w_out, g2, wq, keys):
    n = n_a + n_b
    nbp = n_a // ROW_BLOCK
    assert n_a % ROW_BLOCK == 0 and n_b % ROW_BLOCK == 0 and xp_row0 % ROW_BLOCK == 0 and xs_row0 % ROW_BLOCK == 0
    row = lambda w: pl.BlockSpec((ROW_BLOCK, w), lambda i: (i, 0))
    full = lambda a: pl.BlockSpec(a.shape, lambda i: (0,) * a.ndim)
    wo = w_out.astype(bf16)
    wqb = wq.astype(bf16)
    kb = keys.astype(bf16)
    g = g2.reshape(1, D_MODEL)
    return pl.pallas_call(
        functools.partial(_mid_kernel, nbp=nbp),
        grid=(n // ROW_BLOCK,),
        in_specs=[*_pair_specs(nbp, D_MODEL, xp_row0 // ROW_BLOCK, xs_row0 // ROW_BLOCK), *_pair_specs(nbp, A_WIDTH),
                  *_pair_specs(nbp, M_WIDTH), full(wo), full(g), full(wqb), full(kb),
                  pl.BlockSpec((SUBLANES, P_SLOTS), lambda i: (0, 0))],
        out_specs=[row(D_MODEL), row(D_MODEL), row(P_SLOTS), pl.BlockSpec((P_SLOTS, ROW_BLOCK), lambda i: (0, i)),
                   row(P_SLOTS)],
        out_shape=[jax.ShapeDtypeStruct((n, D_MODEL), f32), jax.ShapeDtypeStruct((n, D_MODEL), f32),
                   jax.ShapeDtypeStruct((n, P_SLOTS), i32), jax.ShapeDtypeStruct((P_SLOTS, n), i32),
                   jax.ShapeDtypeStruct((n, P_SLOTS), f32)],
        compiler_params=pltpu.CompilerParams(dimension_semantics=("parallel",), vmem_limit_bytes=VMEM_LIMIT),
        name="outproj_retrieve",
    )(xp, xs, attp, atts, hp, hs, wo, g, wqb, kb, after)


def _gather_rows(eidx_refs, tab_ref, g_ref, t):
    for i in range(SUBLANES):
        for k, eidx_ref in enumerate(eidx_refs):
            r = k * SUBLANES + i
            g_ref[pl.ds(r * ROW_TILES, ROW_TILES), :] = tab_ref[eidx_ref[i, t]]


def _tile_rows(t, rows=SUBLANES):
    return pl.ds(pl.multiple_of(t * rows, rows), rows)


def _pipelined_tokens(nt, eidx_refs, tab_ref, g0_s, g1_s, compute):
    bufs = (g0_s, g1_s)
    _gather_rows(eidx_refs, tab_ref, g0_s, 0)

    def body(j, carry):
        for u in range(TOKEN_UNROLL):
            t = TOKEN_UNROLL * j + u
            _gather_rows(eidx_refs, tab_ref, bufs[(u + 1) % 2], jnp.minimum(t + 1, nt - 1))
            compute(t, bufs[u % 2])
        return carry

    lax.fori_loop(0, nt // TOKEN_UNROLL, body, 0)


def _gelu_tanh(x):
    return 0.5 * x * (1.0 + jnp.tanh(math.sqrt(2.0 / math.pi) * (x + 0.044715 * (x * x * x))))


def _peer_u_kernel(*refs):
    eidx_refs, (xn_ref, g_ref, tab_ref, w_ref, xl_s, r_s, g0_s, g1_s) = refs[:INDEX_GROUPS], refs[INDEX_GROUPS:]
    nt = xn_ref.shape[0]
    xn = xn_ref[...]
    xh = xn.astype(bf16).astype(f32)
    xl = xn - xh
    for k in range(ROW_TILES):
        xl_s[pl.ds(k, nt, stride=2 * SUBLANES), :] = xh[:, k * LANES:(k + 1) * LANES]
        xl_s[pl.ds(SUBLANES + k, nt, stride=2 * SUBLANES), :] = xl[:, k * LANES:(k + 1) * LANES]
    diag = (lax.broadcasted_iota(i32, (SUBLANES, G_ROWS), 1) % ROW_TILES
            == lax.broadcasted_iota(i32, (SUBLANES, G_ROWS), 0))

    def compute(t, g_s):
        lhs = xl_s[_tile_rows(t, 2 * SUBLANES), :].astype(bf16)
        out = lax.dot_general(lhs, g_s[...], (((1,), (1,)), ((), ())), preferred_element_type=f32)
        part = jnp.where(diag, out[:SUBLANES] + out[SUBLANES:], 0.0)
        for c in range(ROW_TILES):
            r_s[c, _tile_rows(t), :] = part[:, c * LANES:(c + 1) * LANES]

    _pipelined_tokens(nt, eidx_refs, tab_ref, g0_s, g1_s, compute)
    cols = []
    for c in range(ROW_TILES):
        acc = r_s[c, pl.ds(0, nt, stride=SUBLANES), :]
        for k in range(1, SUBLANES):
            acc = acc + r_s[c, pl.ds(k, nt, stride=SUBLANES), :]
        cols.append(acc)
    s = jnp.concatenate(cols, axis=-1)
    fold = (lax.broadcasted_iota(i32, (G_ROWS, P_SLOTS), 0) // ROW_TILES
            == lax.broadcasted_iota(i32, (G_ROWS, P_SLOTS), 1)).astype(bf16)
    sh, sl = _split_bf16(s)
    act = jnp.dot(sh, fold, preferred_element_type=f32) + jnp.dot(sl, fold, preferred_element_type=f32)
    w_ref[...] = g_ref[...] * _gelu_tanh(act)


def _peer_v_kernel(*refs):
    eidx_refs, (w_ref, tab_ref, peer_ref, wl_s, o_s, g0_s, g1_s) = refs[:INDEX_GROUPS], refs[INDEX_GROUPS:]
    nt = w_ref.shape[0]
    spread = (lax.broadcasted_iota(i32, (P_SLOTS, G_ROWS), 1) // ROW_TILES
              == lax.broadcasted_iota(i32, (P_SLOTS, G_ROWS), 0)).astype(bf16)
    wexp = jnp.dot(w_ref[...].astype(bf16), spread, preferred_element_type=f32)
    lane = lax.broadcasted_iota(i32, (nt, LANES), 1)
    for c in range(ROW_TILES):
        wc = wexp[:, c * LANES:(c + 1) * LANES]
        for k in range(SUBLANES):
            wl_s[c, pl.ds(k, nt, stride=SUBLANES), :] = jnp.where(lane % ROW_TILES == k, wc, 0.0)

    def compute(t, g_s):
        lhs = jnp.concatenate([wl_s[c, _tile_rows(t), :] for c in range(ROW_TILES)], axis=-1).astype(bf16)
        o_s[_tile_rows(t), :] = jnp.dot(lhs, g_s[...], preferred_element_type=f32)

    _pipelined_tokens(nt, eidx_refs, tab_ref, g0_s, g1_s, compute)
    for k in range(ROW_TILES):
        peer_ref[:, k * LANES:(k + 1) * LANES] = o_s[pl.ds(k, nt, stride=SUBLANES), :]


def _final_kernel(*refs, starts, nbp):
    k = len(starts)
    peer_refs, x_refs, (gf_ref, yp_ref, ys_ref) = refs[:k], refs[k:2 * k], refs[2 * k:]
    resid = x_refs[0][...] + peer_refs[0][...]
    for start, p_ref, x_ref in zip(starts[1:], peer_refs[1:], x_refs[1:]):
        resid = jnp.where(pl.program_id(0) >= start, x_ref[...] + p_ref[...], resid)
    y = _rms(resid, gf_ref[...])

    @pl.when(pl.program_id(0) < nbp)
    def _():
        yp_ref[...] = y

    @pl.when(pl.program_id(0) >= nbp)
    def _():
        ys_ref[...] = y


def _final(peers, x1s, gf, n_p):
    counts = [x.shape[0] // ROW_BLOCK for x in x1s]
    starts = [sum(counts[:k]) for k in range(len(counts))]
    n, nbp = sum(counts) * ROW_BLOCK, n_p // ROW_BLOCK
    g = gf.reshape(1, D_MODEL)
    seg = [pl.BlockSpec((ROW_BLOCK, D_MODEL), lambda i, s=s, c=c: (jnp.clip(i - s, 0, c - 1), 0))
           for s, c in zip(starts, counts)]
    return pl.pallas_call(
        functools.partial(_final_kernel, starts=tuple(starts), nbp=nbp),
        grid=(n // ROW_BLOCK,),
        in_specs=[*seg, *seg, pl.BlockSpec(g.shape, lambda i: (0, 0))],
        out_specs=list(_pair_specs(nbp, D_MODEL)),
        out_shape=[jax.ShapeDtypeStruct((n_p, D_MODEL), f32), jax.ShapeDtypeStruct((n - n_p, D_MODEL), f32)],
        compiler_params=pltpu.CompilerParams(dimension_semantics=("arbitrary",), vmem_limit_bytes=VMEM_LIMIT),
        name="final_norm",
    )(*peers, *x1s, g)


def _peer_specs():
    row = lambda w: pl.BlockSpec((ROW_BLOCK, w), lambda i: (i, 0))
    idx = [pl.BlockSpec((SUBLANES, ROW_BLOCK), lambda i, k=k: (k, i), memory_space=pltpu.SMEM)
           for k in range(INDEX_GROUPS)]
    tab = pl.BlockSpec(memory_space=pltpu.VMEM)
    gscr = pltpu.VMEM((G_ROWS, LANES), bf16)
    params = pltpu.CompilerParams(dimension_semantics=("arbitrary",), vmem_limit_bytes=VMEM_LIMIT)
    return row, idx, tab, gscr, params


def _expert_table(tab):
    return tab.astype(bf16).reshape(tab.shape[0], ROW_TILES, LANES)


def _peer_u(eidx_t, xn, g, utab, n):
    row, idx, tab, gscr, params = _peer_specs()
    return pl.pallas_call(
        _peer_u_kernel,
        grid=(n // ROW_BLOCK,),
        in_specs=[*idx, row(D_MODEL), row(P_SLOTS), tab],
        out_specs=row(P_SLOTS),
        out_shape=jax.ShapeDtypeStruct((n, P_SLOTS), f32),
        scratch_shapes=[pltpu.VMEM((ROW_BLOCK * 2 * SUBLANES, LANES), f32),
                        pltpu.VMEM((ROW_TILES, ROW_BLOCK * SUBLANES, LANES), f32), gscr, gscr],
        compiler_params=params,
        name="peer_u",
    )(*[eidx_t] * INDEX_GROUPS, xn, g, utab)


def _peer_v(eidx_t, w, vtab):
    n = w.shape[0]
    row, idx, tab, gscr, params = _peer_specs()
    return pl.pallas_call(
        _peer_v_kernel,
        grid=(n // ROW_BLOCK,),
        in_specs=[*idx, row(P_SLOTS), tab],
        out_specs=row(D_MODEL),
        out_shape=jax.ShapeDtypeStruct((n, D_MODEL), f32),
        scratch_shapes=[pltpu.VMEM((ROW_TILES, ROW_BLOCK * SUBLANES, LANES), f32),
                        pltpu.VMEM((ROW_BLOCK * SUBLANES, LANES), f32), gscr, gscr],
        compiler_params=params,
        name="peer_v",
    )(*[eidx_t] * INDEX_GROUPS, w, vtab)


def _sc_table_kernel(t_ref, o_ref):
    t = t_ref[...]
    o_ref[...] = pltpu.pack_elementwise([t[:, :SC_WORDS], t[:, SC_WORDS:]], packed_dtype=bf16)


def _sc_table(tab):
    e = tab.shape[0]
    return pl.pallas_call(
        _sc_table_kernel,
        grid=(e // ROW_BLOCK,),
        in_specs=[pl.BlockSpec((ROW_BLOCK, D_MODEL), lambda i: (i, 0))],
        out_specs=pl.BlockSpec((ROW_BLOCK, SC_WORDS), lambda i: (i, 0)),
        out_shape=jax.ShapeDtypeStruct((e, SC_WORDS), jnp.uint32),
        compiler_params=pltpu.CompilerParams(dimension_semantics=("parallel",), vmem_limit_bytes=VMEM_LIMIT),
        name="sc_table",
    )(tab)


def _sc_unpack(words):
    return plsc.unpack(plsc.bitcast(words, bf16), format=plsc.PackFormat.INTERLEAVED)


def _sc_gelu_tanh(x):
    z = math.sqrt(2.0 / math.pi) * (x + 0.044715 * (x * x * x))
    return 0.5 * x * (2.0 - 2.0 / (jnp.exp(2.0 * z) + 1.0))


def _sc_peer(utab32, vtab32, eidx, xn, gate, after):
    n_sc = eidx.shape[0]
    per = n_sc // SC_WORKERS
    units = 4 * per
    assert n_sc % (SC_WORKERS * SUBLANES) == 0

    def body(u_hbm, v_hbm, eidx_hbm, x_hbm, g_hbm, after_hbm, out_hbm,
             idx_v, x_v, g_v, rows_v, acc_v, w_v, w16_v, out_v, row_sems, tok_sems, out_sem):
        del after_hbm
        base = (lax.axis_index("s") * 2 + lax.axis_index("c")) * per
        lanes = lax.broadcasted_iota(i32, (SC_LANES,), 0)

        def token_copies(tok):
            tslot = tok % 2
            return (pltpu.make_async_copy(eidx_hbm.at[base + tok], idx_v.at[tslot], tok_sems.at[0]),
                    pltpu.make_async_copy(x_hbm.at[base + tok], x_v.at[tslot], tok_sems.at[1]),
                    pltpu.make_async_copy(g_hbm.at[base + tok], g_v.at[tslot], tok_sems.at[2]))

        def store_out(tok):
            return pltpu.make_async_copy(out_v, out_hbm.at[base + tok], out_sem)

        def gather(tab_hbm, g):
            tok, k = g // 4, g % 4
            return pltpu.make_async_copy(tab_hbm.at[idx_v.at[tok % 2, pl.ds((k % 2) * SC_UNIT_ROWS, SC_UNIT_ROWS)]],
                                         rows_v.at[k % 2], row_sems.at[k % 2])

        def start(g):
            @pl.when(g % 4 < 2)
            def _():
                gather(u_hbm, g).start()

            @pl.when(g % 4 >= 2)
            def _():
                gather(v_hbm, g).start()

        def compute_u(tslot, half, rows):
            @pl.loop(0, SC_UNIT_ROWS // SC_ROW_GROUP)
            def _(rg):
                slot0 = half * SC_UNIT_ROWS + rg * SC_ROW_GROUP

                @pl.loop(0, SC_CHUNKS // SC_CHUNK_GROUP)
                def _(cg):
                    keep = jnp.where(cg > 0, 1.0, 0.0).astype(f32)
                    accs = [acc_v[pl.ds((slot0 + r) * SC_LANES, SC_LANES)] * keep for r in range(SC_ROW_GROUP)]
                    for c in range(SC_CHUNK_GROUP):
                        ch = cg * SC_CHUNK_GROUP + c
                        xa = x_v[tslot, pl.ds(ch * SC_LANES, SC_LANES)]
                        xb = x_v[tslot, pl.ds(SC_WORDS + ch * SC_LANES, SC_LANES)]
                        for r in range(SC_ROW_GROUP):
                            a, b = _sc_unpack(rows[rg * SC_ROW_GROUP + r, pl.ds(ch * SC_LANES, SC_LANES)])
                            accs[r] = accs[r] + a * xa + b * xb
                    for r in range(SC_ROW_GROUP):
                        acc_v[pl.ds((slot0 + r) * SC_LANES, SC_LANES)] = accs[r]

        def gate_weights(tslot):
            @pl.loop(0, P_SLOTS // SC_LANES)
            def _(sg):
                first = sg * SC_LANES * SC_LANES
                act = jnp.zeros((SC_LANES,), f32)
                for lane in range(SC_LANES):
                    act = act + plsc.load_gather(acc_v, [first + lanes * SC_LANES + lane])
                w_v[pl.ds(sg * SC_LANES, SC_LANES)] = g_v[tslot, pl.ds(sg * SC_LANES, SC_LANES)] * _sc_gelu_tanh(act)

            @pl.loop(0, P_SLOTS // SC_LANES)
            def _(sg):
                for r in range(SC_LANES):
                    w16_v[pl.ds((sg * SC_LANES + r) * SC_LANES, SC_LANES)] = plsc.load_gather(
                        w_v, [jnp.zeros((SC_LANES,), i32) + (sg * SC_LANES + r)])

        def compute_v(half, rows):
            @pl.loop(0, SC_CHUNKS // SC_CHUNK_GROUP)
            def _(cg):
                first = cg * SC_CHUNK_GROUP

                @pl.loop(0, SC_UNIT_ROWS // SC_ROW_GROUP)
                def _(rg):
                    keep = jnp.where(jnp.logical_or(half == 1, rg > 0), 1.0, 0.0).astype(f32)
                    spots = [pl.ds((j % 2) * SC_WORDS + (first + j // 2) * SC_LANES, SC_LANES)
                             for j in range(2 * SC_CHUNK_GROUP)]
                    accs = [out_v[spot] * keep for spot in spots]
                    for r in range(SC_ROW_GROUP):
                        row = rg * SC_ROW_GROUP + r
                        wv = w16_v[pl.ds((half * SC_UNIT_ROWS + row) * SC_LANES, SC_LANES)]
                        for c in range(SC_CHUNK_GROUP):
                            a, b = _sc_unpack(rows[row, pl.ds((first + c) * SC_LANES, SC_LANES)])
                            accs[2 * c] = accs[2 * c] + a * wv
                            accs[2 * c + 1] = accs[2 * c + 1] + b * wv
                    for spot, acc in zip(spots, accs):
                        out_v[spot] = acc

        for cp in token_copies(0):
            cp.start()
        for cp in token_copies(0):
            cp.wait()
        start(0)

        @pl.loop(0, units)
        def _(g):
            tok, k = g // 4, g % 4

            @pl.when(jnp.logical_and(k == 0, tok + 1 < per))
            def _():
                for cp in token_copies(tok + 1):
                    cp.start()

            @pl.when(jnp.logical_and(k == 3, tok + 1 < per))
            def _():
                for cp in token_copies(tok + 1):
                    cp.wait()

            @pl.when(g + 1 < units)
            def _():
                start(g + 1)

            gather(u_hbm, g).wait()
            rows = rows_v.at[k % 2]

            @pl.when(k < 2)
            def _():
                compute_u(tok % 2, k, rows)

            @pl.when(k == 1)
            def _():
                gate_weights(tok % 2)

            @pl.when(jnp.logical_and(k == 2, tok > 0))
            def _():
                store_out(tok - 1).wait()

            @pl.when(k >= 2)
            def _():
                compute_v(k - 2, rows)

            @pl.when(k == 3)
            def _():
                store_out(tok).start()

        store_out(per - 1).wait()

    return pl.kernel(
        body, mesh=plsc.VectorSubcoreMesh(core_axis_name="c", subcore_axis_name="s"),
        out_type=jax.ShapeDtypeStruct((n_sc, D_MODEL), f32),
        scratch_types=[pltpu.VMEM((2, P_SLOTS), i32), pltpu.VMEM((2, D_MODEL), f32), pltpu.VMEM((2, P_SLOTS), f32),
                       pltpu.VMEM((2, SC_UNIT_ROWS, SC_WORDS), jnp.uint32), pltpu.VMEM((P_SLOTS * SC_LANES,), f32),
                       pltpu.VMEM((P_SLOTS,), f32), pltpu.VMEM((P_SLOTS * SC_LANES,), f32), pltpu.VMEM((D_MODEL,), f32),
                       pltpu.SemaphoreType.DMA((2,)), pltpu.SemaphoreType.DMA((3,)), pltpu.SemaphoreType.DMA],
        compiler_params=pltpu.CompilerParams(needs_layout_passes=False),
        name="sc_peer",
    )(utab32, vtab32, eidx, xn, gate, after)


def kernel(x_prompt, x_sample, cache_k, cache_v, state_C, state_n, state_m, state_conv, norm1_g, w_in, b_gates, rel_bias, conv_w, conv_b, mh_norm_g, w_out, norm2_g, peer_wq, peer_keys, peer_u, peer_v, final_g):
    bp, sp, d = x_prompt.shape
    bs, ts, _ = x_sample.shape
    n_p, n_s = bp * sp, bs * ts
    n = n_p + n_s
    assert n_p % ROW_BLOCK == 0 and n_s % ROW_BLOCK == 0 and d == D_MODEL
    depth = w_in.shape[0]
    assert depth == 1, "the final norm is fused into the last layer's PEER pass"
    l = 0
    xp, xs = x_prompt.reshape(n_p, d), x_sample.reshape(n_s, d)

    zeros = lambda *shp: jnp.zeros(shp, f32)
    mparams = (conv_w[l], conv_b[l], b_gates[l], mh_norm_g[l])
    retrieval = (w_out[l], norm2_g[l], peer_wq[l], peer_keys[l])

    def mixers(proj, seq, lo, hi, state):
        aq, ak, av, mqk, mv, mo, gates = proj
        att = _attn_prompt(aq, ak, av, rel_bias[l], seq, sp, lo, hi)
        h, c, nn, mm = _mlstm(mqk, mv, mo, gates, seq * sp + lo, 1, hi - lo, *state, *mparams)
        conv_rows = mqk[seq * sp + hi - (CONV_W - 1):seq * sp + hi][None]
        return att, h, (c, nn[:, :M_HEADS], mm[:, :M_HEADS, 0], conv_rows)

    fresh = (zeros(1, M_HEADS, M_DH, M_DH), zeros(1, M_HEADS, M_DH), zeros(1, M_HEADS), zeros(1, CONV_W - 1, 2 * M_WIDTH))

    states = [fresh] * bp

    def stage(proj, lo, hi, after, tail=None):
        parts = []
        for seq in range(bp):
            a, b = max(lo, seq * sp), min(hi, (seq + 1) * sp)
            if a < b:
                att, h, states[seq] = mixers(proj, seq, a - seq * sp, b - seq * sp, states[seq])
                parts.append((att, h, b - a, xp, a))
        parts += [tail] if tail is not None else []
        assert 1 <= len(parts) <= 2, "the retrieval kernel reads at most two row sources"
        (att0, h0, n0, x0, r0), (att1, h1, n1, x1, r1) = parts[0], parts[-1]
        n1, r1 = (n1, r1) if len(parts) == 2 else (0, 0)
        return _mid(x0, r0, x1, r1, att0, att1, h0, h1, n0, n1, after, *retrieval)

    sc_tables = _sc_table(peer_u[l]), _sc_table(peer_v[l])
    peers, x1s, lo, after = [], [], 0, jnp.zeros((SUBLANES, P_SLOTS), i32)
    proj = _inproj(xp, SC_STAGE_ENDS[0], xs, norm1_g[l], w_in[l], after)
    for hi in SC_STAGE_ENDS:
        x1, xn, eidx, _, gate = stage(proj, lo, hi, after)
        peers.append(_sc_peer(*sc_tables, eidx, xn, gate, peers[-1] if peers else zeros(SUBLANES, D_MODEL)))
        x1s.append(x1)
        if lo == 0:
            proj = _inproj(xp, n_p, xs, norm1_g[l], w_in[l], eidx)
        lo, after = hi, eidx
    aq, ak, av, mqk, mv, mo, gates = proj

    lcache = cache_k.shape[2]
    att_s = _attn_sample(aq, ak, av, cache_k[l].reshape(bs, lcache, A_WIDTH),
                         cache_v[l].reshape(bs, lcache, A_WIDTH), rel_bias[l], n_p, bs, ts)
    h_s, c_s, nn_s, mm_s = _mlstm(mqk, mv, mo, gates, n_p, bs, ts, state_C[l], state_n[l], state_m[l],
                                  state_conv[l], *mparams)
    x1, xn, _, eidx_t, gate = stage(proj, lo, n_p, after, tail=(att_s, h_s, n_s, xs, 0))
    w = _peer_u(eidx_t, xn, gate, _expert_table(peer_u[l]), n - lo)
    peers.append(_peer_v(eidx_t, w, _expert_table(peer_v[l])))
    x1s.append(x1)
    y_p, y_s = _final(peers, x1s, final_g, n_p)
    c_p, nn_p, mm_p = (jnp.concatenate(per_seq, axis=0) for per_seq in zip(*(st[:3] for st in states)))

    def tail(a, row0, bsz, t, keep):
        return jnp.stack([a[row0 + (b + 1) * t - keep:row0 + (b + 1) * t] for b in range(bsz)])

    keep = min(WINDOW, sp)
    heads = lambda a: a.reshape(a.shape[0], a.shape[1], A_HEADS, A_DH)
    ctail = CONV_W - 1
    conv_tail = lambda buf, a, row0, bsz, t: jnp.concatenate([buf.astype(a.dtype), tail(a, row0, bsz, t, min(ctail, t))],
                                                             axis=1)[:, -ctail:]
    st = lambda a: a[None]
    return (y_p.reshape(bp, sp, d), y_s.reshape(bs, ts, d),
            st(heads(tail(ak, 0, bp, sp, keep))), st(heads(tail(av, 0, bp, sp, keep))),
            st(c_p), st(nn_p), st(mm_p),
            st(conv_tail(zeros(bp, ctail, 2 * M_WIDTH), mqk, 0, bp, sp)),
            st(heads(ak[n_p:].reshape(bs, ts, A_WIDTH))), st(heads(av[n_p:].reshape(bs, ts, A_WIDTH))),
            st(c_s), st(nn_s[:, :M_HEADS]), st(mm_s[:, :M_HEADS, 0]),
            st(conv_tail(state_conv[l], mqk, n_p, bs, ts)))
```

```python
import functools
import math

import jax
import jax.numpy as jnp
from jax import lax
from jax.experimental import pallas as pl
from jax.experimental.pallas import tpu as pltpu
from jax.experimental.pallas import tpu_sc as plsc

f32 = jnp.float32
bf16 = jnp.bfloat16
i32 = jnp.int32

D_MODEL = 1024
CHUNK = 64
A_HEADS = 8
A_DH = 64
A_WIDTH = A_HEADS * A_DH
BAND_CHUNKS = 8
WINDOW = BAND_CHUNKS * CHUNK
MAX_REL = 128
ATT_SCALE = A_DH ** -0.5
M_HEADS = 4
M_DH = 128
M_WIDTH = M_HEADS * M_DH
CONV_W = 4
IN_WIDTHS = (A_WIDTH, A_WIDTH, A_WIDTH, 2 * M_WIDTH, M_WIDTH, M_WIDTH)
P_HEADS = 8
P_DKEY = 256
N_KEYS = 128
P_TOPK = 16
P_SLOTS = P_HEADS * P_TOPK
EPS = 1e-6
NEG = -1e30

LANES = 128
SUBLANES = 8
ROW_BLOCK = 256
ATT_TILE = 512
ATT_SUB = 256
ATT_KEYS = ATT_SUB + WINDOW
ROW_TILES = D_MODEL // LANES
G_ROWS = P_SLOTS * ROW_TILES
TOKEN_UNROLL = 8
INDEX_GROUPS = P_SLOTS // SUBLANES
VMEM_LIMIT = 56 * 1024 * 1024

SC_WORKERS = 32
SC_LANES = 16
SC_UNIT_ROWS = P_SLOTS // 2
SC_ROW_GROUP = 16
SC_WORDS = D_MODEL // 2
SC_CHUNKS = SC_WORDS // SC_LANES
SC_CHUNK_GROUP = 8
SC_STAGE_ENDS = (2048, 6144, 20480)


def _rms(x, g):
    return x * lax.rsqrt(jnp.mean(x * x, axis=-1, keepdims=True) + EPS) * g


def _split_bf16(x):
    hi = x.astype(bf16)
    lo = (x - hi.astype(f32)).astype(bf16)
    return hi, lo


def _pair_specs(nbp, width, first=0, second=0):
    return (pl.BlockSpec((ROW_BLOCK, width), lambda i: (first + jnp.minimum(i, nbp - 1), 0)),
            pl.BlockSpec((ROW_BLOCK, width), lambda i: (second + jnp.maximum(i - nbp, 0), 0)))


def _pick(nbp, p_ref, s_ref):
    return jnp.where(pl.program_id(0) < nbp, p_ref[...], s_ref[...])


def _inproj_kernel(xp_ref, xs_ref, g_ref, w_ref, wgh_ref, wgl_ref, after_ref,
                   aq_ref, ak_ref, av_ref, mqk_ref, mv_ref, mo_ref, gate_ref, *, nbp):
    del after_ref
    xn = _rms(_pick(nbp, xp_ref, xs_ref), g_ref[...])
    xh, xl = _split_bf16(xn)

    lo = 0
    for out_ref, width in zip((aq_ref, ak_ref, av_ref, mqk_ref, mv_ref, mo_ref), IN_WIDTHS):
        out_ref[...] = jnp.dot(xh, w_ref[:, lo:lo + width], preferred_element_type=f32)
        lo += width
    gate_ref[...] = (jnp.dot(xh, wgh_ref[...], preferred_element_type=f32)
                     + jnp.dot(xl, wgh_ref[...], preferred_element_type=f32)
                     + jnp.dot(xh, wgl_ref[...], preferred_element_type=f32))


def _inproj(xp, n_p, xs, g1, w_in, after):
    n = n_p + xs.shape[0]
    nbp = n_p // ROW_BLOCK
    main = sum(IN_WIDTHS)
    w_main = w_in[:, :main].astype(bf16)
    wg = jnp.pad(w_in[:, main:], ((0, 0), (0, LANES - 2 * M_HEADS)))
    wgh, wgl = _split_bf16(wg)
    widths = IN_WIDTHS + (LANES,)
    row = lambda w: pl.BlockSpec((ROW_BLOCK, w), lambda i: (i, 0))
    full = lambda a: pl.BlockSpec(a.shape, lambda i: (0,) * a.ndim)
    g = g1.reshape(1, D_MODEL)
    return pl.pallas_call(
        functools.partial(_inproj_kernel, nbp=nbp),
        grid=(n // ROW_BLOCK,),
        in_specs=[*_pair_specs(nbp, D_MODEL), full(g), full(w_main), full(wgh), full(wgl),
                  pl.BlockSpec((SUBLANES, P_SLOTS), lambda i: (0, 0))],
        out_specs=[row(w) for w in widths],
        out_shape=[jax.ShapeDtypeStruct((n, w), f32) for w in widths],
        compiler_params=pltpu.CompilerParams(dimension_semantics=("parallel",), vmem_limit_bytes=VMEM_LIMIT),
        name="inproj",
    )(xp, xs, g, w_main, wgh, wgl, after)


def _attn_heads(q, k, v, bias_ref, key_ok):
    outs = []
    for h in range(A_HEADS):
        sl = slice(h * A_DH, (h + 1) * A_DH)
        s = lax.dot_general(q[:, sl], k[:, sl], (((1,), (1,)), ((), ())), preferred_element_type=f32)
        s = s * ATT_SCALE + bias_ref[h]
        if key_ok is not None:
            s = jnp.where(key_ok, s, NEG)
        m = jnp.max(s, axis=-1, keepdims=True)
        p = jnp.exp(s - m)
        l = jnp.sum(p, axis=-1, keepdims=True)
        o = jnp.dot(p.astype(bf16), v[:, sl], preferred_element_type=f32)
        outs.append(o / l)
    return jnp.concatenate(outs, axis=-1)


def _attn_prompt_kernel(q_ref, k0_ref, k1_ref, v0_ref, v1_ref, bias_ref, o_ref, *, t0):
    t = t0 + pl.program_id(1)
    q = q_ref[...].astype(bf16)
    k = jnp.concatenate([k0_ref[...], k1_ref[...]], axis=0).astype(bf16)
    v = jnp.concatenate([v0_ref[...], v1_ref[...]], axis=0).astype(bf16)
    col = lax.broadcasted_iota(i32, (1, ATT_KEYS), 1)
    for s in range(ATT_TILE // ATT_SUB):
        lo = s * ATT_SUB
        key_ok = (t * ATT_TILE + lo + col) >= WINDOW
        o_ref[lo:lo + ATT_SUB, :] = _attn_heads(q[lo:lo + ATT_SUB], k[lo:lo + ATT_KEYS], v[lo:lo + ATT_KEYS],
                                                 bias_ref, key_ok)


def _attn_sample_kernel(q_ref, k_ref, v_ref, bias_ref, o_ref):
    o_ref[...] = _attn_heads(q_ref[...].astype(bf16), k_ref[0].astype(bf16), v_ref[0].astype(bf16), bias_ref, None)


def _rel_bias_table(rel_bias, rows, cols, offset, valid):
    span = rows + cols - 1
    rel = offset + rows - 1 - jnp.arange(span)
    diag = rel_bias[:, jnp.clip(rel, -MAX_REL, MAX_REL) + MAX_REL].astype(f32)
    diag = jnp.pad(diag, ((0, 0), (0, 1)))
    flat = jnp.tile(diag, (1, rows))[:, rows - 1:rows - 1 + rows * span]
    return jnp.where(valid[None], flat.reshape(-1, rows, span)[:, :, :cols], NEG)


def _attn_prompt(q, k, v, rel_bias, seq, s, lo, hi):
    assert s % ATT_TILE == 0 and WINDOW == ATT_TILE and lo % ATT_TILE == 0 and hi % ATT_TILE == 0
    nt, t0, cnt = s // ATT_TILE, lo // ATT_TILE, (hi - lo) // ATT_TILE
    i = jnp.arange(ATT_SUB)[:, None]
    j = jnp.arange(ATT_KEYS)[None, :]
    off = j - (i // CHUNK) * CHUNK
    bias = _rel_bias_table(rel_bias, ATT_SUB, ATT_KEYS, WINDOW, (off >= 0) & (off < WINDOW + CHUNK))
    cur = pl.BlockSpec((ATT_TILE, A_WIDTH), lambda b, t: (seq * nt + t0 + t, 0))
    prev = pl.BlockSpec((ATT_TILE, A_WIDTH), lambda b, t: (seq * nt + jnp.maximum(t0 + t - 1, 0), 0))
    return pl.pallas_call(
        functools.partial(_attn_prompt_kernel, t0=t0),
        grid=(1, cnt),
        in_specs=[cur, prev, cur, prev, cur, pl.BlockSpec(bias.shape, lambda b, t: (0, 0, 0))],
        out_specs=pl.BlockSpec((ATT_TILE, A_WIDTH), lambda b, t: (t, 0)),
        out_shape=jax.ShapeDtypeStruct((hi - lo, A_WIDTH), f32),
        compiler_params=pltpu.CompilerParams(dimension_semantics=("parallel", "parallel"),
                                             vmem_limit_bytes=VMEM_LIMIT),
        name="attn_prompt",
    )(q, k, k, v, v, bias)


def _attn_sample(q, k, v, ck, cv, rel_bias, row0, bsz, t):
    l = ck.shape[1]
    assert row0 % t == 0
    keys = -(-(l + t) // LANES) * LANES
    padk = ((0, 0), (0, keys - l - t), (0, 0))
    kk = jnp.pad(jnp.concatenate([ck, k[row0:].reshape(bsz, t, A_WIDTH)], axis=1), padk)
    vv = jnp.pad(jnp.concatenate([cv, v[row0:].reshape(bsz, t, A_WIDTH)], axis=1), padk)
    j = jnp.arange(keys)[None, :]
    bias = _rel_bias_table(rel_bias, t, keys, l, jnp.broadcast_to(j < l + t, (t, keys)))
    return pl.pallas_call(
        _attn_sample_kernel,
        grid=(bsz,),
        in_specs=[pl.BlockSpec((t, A_WIDTH), lambda b: (row0 // t + b, 0)),
                  pl.BlockSpec((1, keys, A_WIDTH), lambda b: (b, 0, 0)),
                  pl.BlockSpec((1, keys, A_WIDTH), lambda b: (b, 0, 0)),
                  pl.BlockSpec(bias.shape, lambda b: (0, 0, 0))],
        out_specs=pl.BlockSpec((t, A_WIDTH), lambda b: (b, 0)),
        out_shape=jax.ShapeDtypeStruct((bsz * t, A_WIDTH), f32),
        compiler_params=pltpu.CompilerParams(dimension_semantics=("parallel",), vmem_limit_bytes=VMEM_LIMIT),
        name="attn_sample",
    )(q, kk, vv, bias)


def _mlstm_chunk(a, vall, o_in, gate_in, cw_ref, cb_ref, bg_ref, mhg_ref, c_s, n_s, m_s, prev_s):
    lc = a.shape[0]
    ext = jnp.concatenate([prev_s[...], a], axis=0)
    conv = cb_ref[...]
    for j in range(CONV_W):
        lo = SUBLANES - (CONV_W - 1) + j
        conv = conv + cw_ref[j:j + 1, :] * ext[lo:lo + lc]
    prev_s[...] = a[lc - SUBLANES:lc]
    qk = conv * jax.nn.sigmoid(conv)

    z = gate_in + bg_ref[...]
    lane = lax.broadcasted_iota(i32, (lc, LANES), 1)
    row = lax.broadcasted_iota(i32, (lc, LANES), 0)
    logf = jnp.minimum(z, 0.0) - jnp.log1p(jnp.exp(-jnp.abs(z)))
    cum = jnp.where((lane >= M_HEADS) & (lane < 2 * M_HEADS), logf, 0.0)
    shift = 1
    while shift < lc:
        cum = cum + jnp.where(row >= shift, pltpu.roll(cum, shift, axis=0), 0.0)
        shift *= 2
    zc = jnp.where(lane < M_HEADS, z, cum)
    zt = jnp.concatenate([zc, jnp.zeros((LANES - lc, LANES), f32)], axis=0).T[:, :lc]

    ri = lax.broadcasted_iota(i32, (lc, lc), 0)
    ci = lax.broadcasted_iota(i32, (lc, lc), 1)
    causal = ri >= ci
    state = [(m_s[h:h + 1, 0:1], c_s[h], n_s[h:h + 1, :]) for h in range(M_HEADS)]
    hs, new_state = [], []
    for h in range(M_HEADS):
        sl = slice(h * M_DH, (h + 1) * M_DH)
        q = qk[:, sl]
        k = qk[:, M_WIDTH + h * M_DH:M_WIDTH + (h + 1) * M_DH] * (M_DH ** -0.5)
        v = vall[:, sl]
        i_col = zc[:, h:h + 1]
        b_col = zc[:, M_HEADS + h:M_HEADS + h + 1]
        i_row = zt[h:h + 1, :]
        b_row = zt[M_HEADS + h:M_HEADS + h + 1, :]
        m_prev, c_prev, n_prev = state[h]

        dmat = jnp.where(causal, b_col - b_row + i_row, NEG)
        inter = b_col + m_prev
        mt = jnp.maximum(jnp.max(dmat, axis=-1, keepdims=True), inter)
        w_intra = jnp.exp(dmat - mt)
        w_inter = jnp.exp(inter - mt)
        qb, kb, vb = q.astype(bf16), k.astype(bf16), v.astype(bf16)
        s = lax.dot_general(qb, kb, (((1,), (1,)), ((), ())), preferred_element_type=f32) * w_intra
        num = (w_inter * jnp.dot(qb, c_prev.astype(bf16), preferred_element_type=f32)
               + jnp.dot(s.astype(bf16), vb, preferred_element_type=f32))
        den = w_inter * jnp.sum(q * n_prev, axis=-1, keepdims=True) + jnp.sum(s, axis=-1, keepdims=True)
        hh = num / jnp.maximum(jnp.abs(den), jnp.exp(-mt))
        m_new = mt[lc - 1:lc, :]
        b_last = b_col[lc - 1:lc, :]
        w_s = jnp.exp(b_last - b_col + i_col - m_new)
        decay = jnp.exp(b_last + m_prev - m_new)
        kw = k * w_s
        new_state.append((jnp.broadcast_to(m_new, (1, LANES)),
                          decay * c_prev + lax.dot_general(kw.astype(bf16), vb, (((0,), (0,)), ((), ())),
                                                           preferred_element_type=f32),
                          decay * n_prev + jnp.sum(kw, axis=0, keepdims=True)))
        hs.append(hh * lax.rsqrt(jnp.mean(hh * hh, axis=-1, keepdims=True) + EPS))

    for h, (m_new, c_new, n_new) in enumerate(new_state):
        m_s[h:h + 1, :] = m_new
        c_s[h] = c_new
        n_s[h:h + 1, :] = n_new
    return jnp.concatenate(hs, axis=-1) * mhg_ref[...] * jax.nn.sigmoid(o_in)


def _mlstm_kernel(qk_ref, v_ref, o_ref, gate_ref, c0_ref, n0_ref, m0_ref, cbuf_ref,
                  cw_ref, cb_ref, bg_ref, mhg_ref,
                  h_ref, cout_ref, nout_ref, mout_ref,
                  c_s, n_s, m_s, prev_s):
    c = pl.program_id(1)

    @pl.when(c == 0)
    def _():
        c_s[...] = c0_ref[0]
        n_s[...] = n0_ref[0]
        m_s[...] = m0_ref[0]
        prev_s[...] = cbuf_ref[0]

    h_ref[...] = _mlstm_chunk(qk_ref[...], v_ref[...], o_ref[...], gate_ref[...],
                              cw_ref, cb_ref, bg_ref, mhg_ref, c_s, n_s, m_s, prev_s)

    @pl.when(c == pl.num_programs(1) - 1)
    def _():
        cout_ref[0] = c_s[...]
        nout_ref[0] = n_s[...]
        mout_ref[0] = m_s[...]


def _mlstm(mqk, mv, mo, gates, row0, bsz, t, c0, n0, m0, cbuf, conv_w, conv_b, b_gates, mh_g):
    lc = min(CHUNK, t)
    step = lc
    nc = t // step
    assert t % step == 0 and lc % SUBLANES == 0 and row0 % step == 0
    n0p = jnp.pad(n0.astype(f32), ((0, 0), (0, SUBLANES - M_HEADS), (0, 0)))
    m0p = jnp.pad(jnp.broadcast_to(m0.astype(f32)[:, :, None], (bsz, M_HEADS, LANES)),
                  ((0, 0), (0, SUBLANES - M_HEADS), (0, 0)))
    cbp = jnp.pad(cbuf.astype(f32), ((0, 0), (SUBLANES - (CONV_W - 1), 0), (0, 0)))
    bg = jnp.pad(b_gates.astype(f32), (0, LANES - 2 * M_HEADS)).reshape(1, LANES)
    seq = lambda w: pl.BlockSpec((step, w), lambda b, c: (row0 // step + b * nc + c, 0))
    out_seq = pl.BlockSpec((step, M_WIDTH), lambda b, c: (b * nc + c, 0))
    per_b = lambda shp: pl.BlockSpec((1,) + shp, lambda b, c: (b,) + (0,) * len(shp))
    full = lambda a: pl.BlockSpec(a.shape, lambda b, c: (0,) * a.ndim)
    cb = conv_b.reshape(1, -1)
    mhg = mh_g.reshape(1, -1)
    return pl.pallas_call(
        _mlstm_kernel,
        grid=(bsz, nc),
        in_specs=[seq(2 * M_WIDTH), seq(M_WIDTH), seq(M_WIDTH), seq(LANES),
                  per_b((M_HEADS, M_DH, M_DH)), per_b((SUBLANES, M_DH)), per_b((SUBLANES, LANES)),
                  per_b((SUBLANES, 2 * M_WIDTH)),
                  full(conv_w), full(cb), full(bg), full(mhg)],
        out_specs=[out_seq, per_b((M_HEADS, M_DH, M_DH)), per_b((SUBLANES, M_DH)), per_b((SUBLANES, LANES))],
        out_shape=[jax.ShapeDtypeStruct((bsz * t, M_WIDTH), f32),
                   jax.ShapeDtypeStruct((bsz, M_HEADS, M_DH, M_DH), f32),
                   jax.ShapeDtypeStruct((bsz, SUBLANES, M_DH), f32),
                   jax.ShapeDtypeStruct((bsz, SUBLANES, LANES), f32)],
        scratch_shapes=[pltpu.VMEM((M_HEADS, M_DH, M_DH), f32), pltpu.VMEM((SUBLANES, M_DH), f32),
                        pltpu.VMEM((SUBLANES, LANES), f32), pltpu.VMEM((SUBLANES, 2 * M_WIDTH), f32)],
        compiler_params=pltpu.CompilerParams(dimension_semantics=("parallel", "arbitrary"),
                                             vmem_limit_bytes=VMEM_LIMIT),
        name="mlstm",
    )(mqk, mv, mo, gates, c0.astype(f32), n0p, m0p, cbp, conv_w, cb, bg, mhg)


def _topk_rows(s, k, payload=None):
    n = s.shape[0]
    rows = lax.broadcasted_iota(i32, s.shape, 0).astype(f32)
    vals, ids = [], []
    for _ in range(k):
        m = jnp.max(s, axis=0, keepdims=True)
        pos = jnp.min(jnp.where(s == m, rows, float(n)), axis=0, keepdims=True)
        sel = rows == pos
        vals.append(m)
        ids.append(pos if payload is None else jnp.max(jnp.where(sel, payload, -1.0), axis=0, keepdims=True))
        s = jnp.where(sel, -jnp.inf, s)
    return jnp.concatenate(vals, axis=0), jnp.concatenate(ids, axis=0)


def _mid_kernel(xp_ref, xs_ref, attp_ref, atts_ref, hp_ref, hs_ref, wo_ref, g2_ref, wq_ref, keys_ref, after_ref,
                x1_ref, xn_ref, eidx_ref, eidx_t_ref, gate_ref, *, nbp):
    del after_ref
    cat = jnp.concatenate([_pick(nbp, attp_ref, atts_ref), _pick(nbp, hp_ref, hs_ref)], axis=-1).astype(bf16)
    x1 = _pick(nbp, xp_ref, xs_ref) + jnp.dot(cat, wo_ref[...], preferred_element_type=f32)
    x1_ref[...] = x1
    xn = _rms(x1, g2_ref[...])
    xn_ref[...] = xn
    xb = xn.astype(bf16)
    e_rows, g_rows = [], []
    for h in range(P_HEADS):
        q = jnp.dot(xb, wq_ref[:, h * P_DKEY:(h + 1) * P_DKEY], preferred_element_type=f32).astype(bf16)
        half = []
        for p in range(2):
            st = lax.dot_general(keys_ref[h, p], q[:, p * N_KEYS:(p + 1) * N_KEYS], (((1,), (1,)), ((), ())),
                                 preferred_element_type=f32)
            half.append(_topk_rows(st, P_TOPK))
        (v0, i0), (v1, i1) = half
        width = [P_TOPK // (a + 1) for a in range(P_TOPK)]
        fill = -sum(width) % SUBLANES
        cand = jnp.concatenate([v0[a:a + 1] + v1[:width[a]] for a in range(P_TOPK)]
                               + [jnp.full((fill, v0.shape[1]), -jnp.inf, f32)], axis=0)
        cidx = jnp.concatenate([i0[a:a + 1] * float(N_KEYS) + i1[:width[a]] for a in range(P_TOPK)]
                               + [jnp.zeros((fill, v0.shape[1]), f32)], axis=0)
        top_s, eid = _topk_rows(cand, P_TOPK, payload=cidx)
        ex = jnp.exp(top_s - top_s[0:1])
        e_rows.append(eid)
        g_rows.append(ex / jnp.sum(ex, axis=0, keepdims=True))
    eidx_t = jnp.concatenate(e_rows, axis=0)
    eidx_t_ref[...] = eidx_t.astype(i32)
    eidx_ref[...] = eidx_t.T.astype(i32)
    gate_ref[...] = jnp.concatenate(g_rows, axis=0).T


def _mid(xp, xp_row0, xs, xs_row0, attp, atts, hp, hs, n_a, n_b, after, w_out, g2, wq, keys):
    n = n_a + n_b
    nbp = n_a // ROW_BLOCK
    assert n_a % ROW_BLOCK == 0 and n_b % ROW_BLOCK == 0 and xp_row0 % ROW_BLOCK == 0 and xs_row0 % ROW_BLOCK == 0
    row = lambda w: pl.BlockSpec((ROW_BLOCK, w), lambda i: (i, 0))
    full = lambda a: pl.BlockSpec(a.shape, lambda i: (0,) * a.ndim)
    wo = w_out.astype(bf16)
    wqb = wq.astype(bf16)
    kb = keys.astype(bf16)
    g = g2.reshape(1, D_MODEL)
    return pl.pallas_call(
        functools.partial(_mid_kernel, nbp=nbp),
        grid=(n // ROW_BLOCK,),
        in_specs=[*_pair_specs(nbp, D_MODEL, xp_row0 // ROW_BLOCK, xs_row0 // ROW_BLOCK), *_pair_specs(nbp, A_WIDTH),
                  *_pair_specs(nbp, M_WIDTH), full(wo), full(g), full(wqb), full(kb),
                  pl.BlockSpec((SUBLANES, P_SLOTS), lambda i: (0, 0))],
        out_specs=[row(D_MODEL), row(D_MODEL), row(P_SLOTS), pl.BlockSpec((P_SLOTS, ROW_BLOCK), lambda i: (0, i)),
                   row(P_SLOTS)],
        out_shape=[jax.ShapeDtypeStruct((n, D_MODEL), f32), jax.ShapeDtypeStruct((n, D_MODEL), f32),
                   jax.ShapeDtypeStruct((n, P_SLOTS), i32), jax.ShapeDtypeStruct((P_SLOTS, n), i32),
                   jax.ShapeDtypeStruct((n, P_SLOTS), f32)],
        compiler_params=pltpu.CompilerParams(dimension_semantics=("parallel",), vmem_limit_bytes=VMEM_LIMIT),
        name="outproj_retrieve",
    )(xp, xs, attp, atts, hp, hs, wo, g, wqb, kb, after)


def _gather_rows(eidx_refs, tab_ref, g_ref, t):
    for i in range(SUBLANES):
        for k, eidx_ref in enumerate(eidx_refs):
            r = k * SUBLANES + i
            g_ref[pl.ds(r * ROW_TILES, ROW_TILES), :] = tab_ref[eidx_ref[i, t]]


def _tile_rows(t, rows=SUBLANES):
    return pl.ds(pl.multiple_of(t * rows, rows), rows)


def _pipelined_tokens(nt, eidx_refs, tab_ref, g0_s, g1_s, compute):
    bufs = (g0_s, g1_s)
    _gather_rows(eidx_refs, tab_ref, g0_s, 0)

    def body(j, carry):
        for u in range(TOKEN_UNROLL):
            t = TOKEN_UNROLL * j + u
            _gather_rows(eidx_refs, tab_ref, bufs[(u + 1) % 2], jnp.minimum(t + 1, nt - 1))
            compute(t, bufs[u % 2])
        return carry

    lax.fori_loop(0, nt // TOKEN_UNROLL, body, 0)


def _gelu_tanh(x):
    return 0.5 * x * (1.0 + jnp.tanh(math.sqrt(2.0 / math.pi) * (x + 0.044715 * (x * x * x))))


def _peer_u_kernel(*refs):
    eidx_refs, (xn_ref, g_ref, tab_ref, w_ref, xl_s, r_s, g0_s, g1_s) = refs[:INDEX_GROUPS], refs[INDEX_GROUPS:]
    nt = xn_ref.shape[0]
    xn = xn_ref[...]
    xh = xn.astype(bf16).astype(f32)
    xl = xn - xh
    for k in range(ROW_TILES):
        xl_s[pl.ds(k, nt, stride=2 * SUBLANES), :] = xh[:, k * LANES:(k + 1) * LANES]
        xl_s[pl.ds(SUBLANES + k, nt, stride=2 * SUBLANES), :] = xl[:, k * LANES:(k + 1) * LANES]
    diag = (lax.broadcasted_iota(i32, (SUBLANES, G_ROWS), 1) % ROW_TILES
            == lax.broadcasted_iota(i32, (SUBLANES, G_ROWS), 0))

    def compute(t, g_s):
        lhs = xl_s[_tile_rows(t, 2 * SUBLANES), :].astype(bf16)
        out = lax.dot_general(lhs, g_s[...], (((1,), (1,)), ((), ())), preferred_element_type=f32)
        part = jnp.where(diag, out[:SUBLANES] + out[SUBLANES:], 0.0)
        for c in range(ROW_TILES):
            r_s[c, _tile_rows(t), :] = part[:, c * LANES:(c + 1) * LANES]

    _pipelined_tokens(nt, eidx_refs, tab_ref, g0_s, g1_s, compute)
    cols = []
    for c in range(ROW_TILES):
        acc = r_s[c, pl.ds(0, nt, stride=SUBLANES), :]
        for k in range(1, SUBLANES):
            acc = acc + r_s[c, pl.ds(k, nt, stride=SUBLANES), :]
        cols.append(acc)
    s = jnp.concatenate(cols, axis=-1)
    fold = (lax.broadcasted_iota(i32, (G_ROWS, P_SLOTS), 0) // ROW_TILES
            == lax.broadcasted_iota(i32, (G_ROWS, P_SLOTS), 1)).astype(bf16)
    sh, sl = _split_bf16(s)
    act = jnp.dot(sh, fold, preferred_element_type=f32) + jnp.dot(sl, fold, preferred_element_type=f32)
    w_ref[...] = g_ref[...] * _gelu_tanh(act)


def _peer_v_kernel(*refs):
    eidx_refs, (w_ref, tab_ref, peer_ref, wl_s, o_s, g0_s, g1_s) = refs[:INDEX_GROUPS], refs[INDEX_GROUPS:]
    nt = w_ref.shape[0]
    spread = (lax.broadcasted_iota(i32, (P_SLOTS, G_ROWS), 1) // ROW_TILES
              == lax.broadcasted_iota(i32, (P_SLOTS, G_ROWS), 0)).astype(bf16)
    wexp = jnp.dot(w_ref[...].astype(bf16), spread, preferred_element_type=f32)
    lane = lax.broadcasted_iota(i32, (nt, LANES), 1)
    for c in range(ROW_TILES):
        wc = wexp[:, c * LANES:(c + 1) * LANES]
        for k in range(SUBLANES):
            wl_s[c, pl.ds(k, nt, stride=SUBLANES), :] = jnp.where(lane % ROW_TILES == k, wc, 0.0)

    def compute(t, g_s):
        lhs = jnp.concatenate([wl_s[c, _tile_rows(t), :] for c in range(ROW_TILES)], axis=-1).astype(bf16)
        o_s[_tile_rows(t), :] = jnp.dot(lhs, g_s[...], preferred_element_type=f32)

    _pipelined_tokens(nt, eidx_refs, tab_ref, g0_s, g1_s, compute)
    for k in range(ROW_TILES):
        peer_ref[:, k * LANES:(k + 1) * LANES] = o_s[pl.ds(k, nt, stride=SUBLANES), :]


def _final_kernel(*refs, starts, nbp):
    k = len(starts)
    peer_refs, x_refs, (gf_ref, yp_ref, ys_ref) = refs[:k], refs[k:2 * k], refs[2 * k:]
    resid = x_refs[0][...] + peer_refs[0][...]
    for start, p_ref, x_ref in zip(starts[1:], peer_refs[1:], x_refs[1:]):
        resid = jnp.where(pl.program_id(0) >= start, x_ref[...] + p_ref[...], resid)
    y = _rms(resid, gf_ref[...])

    @pl.when(pl.program_id(0) < nbp)
    def _():
        yp_ref[...] = y

    @pl.when(pl.program_id(0) >= nbp)
    def _():
        ys_ref[...] = y


def _final(peers, x1s, gf, n_p):
    counts = [x.shape[0] // ROW_BLOCK for x in x1s]
    starts = [sum(counts[:k]) for k in range(len(counts))]
    n, nbp = sum(counts) * ROW_BLOCK, n_p // ROW_BLOCK
    g = gf.reshape(1, D_MODEL)
    seg = [pl.BlockSpec((ROW_BLOCK, D_MODEL), lambda i, s=s, c=c: (jnp.clip(i - s, 0, c - 1), 0))
           for s, c in zip(starts, counts)]
    return pl.pallas_call(
        functools.partial(_final_kernel, starts=tuple(starts), nbp=nbp),
        grid=(n // ROW_BLOCK,),
        in_specs=[*seg, *seg, pl.BlockSpec(g.shape, lambda i: (0, 0))],
        out_specs=list(_pair_specs(nbp, D_MODEL)),
        out_shape=[jax.ShapeDtypeStruct((n_p, D_MODEL), f32), jax.ShapeDtypeStruct((n - n_p, D_MODEL), f32)],
        compiler_params=pltpu.CompilerParams(dimension_semantics=("arbitrary",), vmem_limit_bytes=VMEM_LIMIT),
        name="final_norm",
    )(*peers, *x1s, g)


def _peer_specs():
    row = lambda w: pl.BlockSpec((ROW_BLOCK, w), lambda i: (i, 0))
    idx = [pl.BlockSpec((SUBLANES, ROW_BLOCK), lambda i, k=k: (k, i), memory_space=pltpu.SMEM)
           for k in range(INDEX_GROUPS)]
    tab = pl.BlockSpec(memory_space=pltpu.VMEM)
    gscr = pltpu.VMEM((G_ROWS, LANES), bf16)
    params = pltpu.CompilerParams(dimension_semantics=("arbitrary",), vmem_limit_bytes=VMEM_LIMIT)
    return row, idx, tab, gscr, params


def _expert_table(tab):
    return tab.astype(bf16).reshape(tab.shape[0], ROW_TILES, LANES)


def _peer_u(eidx_t, xn, g, utab, n):
    row, idx, tab, gscr, params = _peer_specs()
    return pl.pallas_call(
        _peer_u_kernel,
        grid=(n // ROW_BLOCK,),
        in_specs=[*idx, row(D_MODEL), row(P_SLOTS), tab],
        out_specs=row(P_SLOTS),
        out_shape=jax.ShapeDtypeStruct((n, P_SLOTS), f32),
        scratch_shapes=[pltpu.VMEM((ROW_BLOCK * 2 * SUBLANES, LANES), f32),
                        pltpu.VMEM((ROW_TILES, ROW_BLOCK * SUBLANES, LANES), f32), gscr, gscr],
        compiler_params=params,
        name="peer_u",
    )(*[eidx_t] * INDEX_GROUPS, xn, g, utab)


def _peer_v(eidx_t, w, vtab):
    n = w.shape[0]
    row, idx, tab, gscr, params = _peer_specs()
    return pl.pallas_call(
        _peer_v_kernel,
        grid=(n // ROW_BLOCK,),
        in_specs=[*idx, row(P_SLOTS), tab],
        out_specs=row(D_MODEL),
        out_shape=jax.ShapeDtypeStruct((n, D_MODEL), f32),
        scratch_shapes=[pltpu.VMEM((ROW_TILES, ROW_BLOCK * SUBLANES, LANES), f32),
                        pltpu.VMEM((ROW_BLOCK * SUBLANES, LANES), f32), gscr, gscr],
        compiler_params=params,
        name="peer_v",
    )(*[eidx_t] * INDEX_GROUPS, w, vtab)


def _sc_table_kernel(t_ref, o_ref):
    t = t_ref[...]
    o_ref[...] = pltpu.pack_elementwise([t[:, :SC_WORDS], t[:, SC_WORDS:]], packed_dtype=bf16)


def _sc_table(tab):
    e = tab.shape[0]
    return pl.pallas_call(
        _sc_table_kernel,
        grid=(e // ROW_BLOCK,),
        in_specs=[pl.BlockSpec((ROW_BLOCK, D_MODEL), lambda i: (i, 0))],
        out_specs=pl.BlockSpec((ROW_BLOCK, SC_WORDS), lambda i: (i, 0)),
        out_shape=jax.ShapeDtypeStruct((e, SC_WORDS), jnp.uint32),
        compiler_params=pltpu.CompilerParams(dimension_semantics=("parallel",), vmem_limit_bytes=VMEM_LIMIT),
        name="sc_table",
    )(tab)


def _sc_unpack(words):
    return plsc.unpack(plsc.bitcast(words, bf16), format=plsc.PackFormat.INTERLEAVED)


def _sc_gelu_tanh(x):
    z = math.sqrt(2.0 / math.pi) * (x + 0.044715 * (x * x * x))
    return 0.5 * x * (2.0 - 2.0 / (jnp.exp(2.0 * z) + 1.0))


def _sc_peer(utab32, vtab32, eidx, xn, gate, after):
    n_sc = eidx.shape[0]
    per = n_sc // SC_WORKERS
    units = 4 * per
    assert n_sc % (SC_WORKERS * SUBLANES) == 0

    def body(u_hbm, v_hbm, eidx_hbm, x_hbm, g_hbm, after_hbm, out_hbm,
             idx_v, x_v, g_v, rows_v, acc_v, w_v, w16_v, out_v, row_sems, tok_sems, out_sem):
        del after_hbm
        base = (lax.axis_index("s") * 2 + lax.axis_index("c")) * per
        lanes = lax.broadcasted_iota(i32, (SC_LANES,), 0)

        def token_copies(tok):
            tslot = tok % 2
            return (pltpu.make_async_copy(eidx_hbm.at[base + tok], idx_v.at[tslot], tok_sems.at[0]),
                    pltpu.make_async_copy(x_hbm.at[base + tok], x_v.at[tslot], tok_sems.at[1]),
                    pltpu.make_async_copy(g_hbm.at[base + tok], g_v.at[tslot], tok_sems.at[2]))

        def store_out(tok):
            return pltpu.make_async_copy(out_v, out_hbm.at[base + tok], out_sem)

        def gather(tab_hbm, g):
            tok, k = g // 4, g % 4
            return pltpu.make_async_copy(tab_hbm.at[idx_v.at[tok % 2, pl.ds((k % 2) * SC_UNIT_ROWS, SC_UNIT_ROWS)]],
                                         rows_v.at[k % 2], row_sems.at[k % 2])

        def start(g):
            @pl.when(g % 4 < 2)
            def _():
                gather(u_hbm, g).start()

            @pl.when(g % 4 >= 2)
            def _():
                gather(v_hbm, g).start()

        def compute_u(tslot, half, rows):
            @pl.loop(0, SC_UNIT_ROWS // SC_ROW_GROUP)
            def _(rg):
                slot0 = half * SC_UNIT_ROWS + rg * SC_ROW_GROUP

                @pl.loop(0, SC_CHUNKS // SC_CHUNK_GROUP)
                def _(cg):
                    keep = jnp.where(cg > 0, 1.0, 0.0).astype(f32)
                    accs = [acc_v[pl.ds((slot0 + r) * SC_LANES, SC_LANES)] * keep for r in range(SC_ROW_GROUP)]
                    for c in range(SC_CHUNK_GROUP):
                        ch = cg * SC_CHUNK_GROUP + c
                        xa = x_v[tslot, pl.ds(ch * SC_LANES, SC_LANES)]
                        xb = x_v[tslot, pl.ds(SC_WORDS + ch * SC_LANES, SC_LANES)]
                        for r in range(SC_ROW_GROUP):
                            a, b = _sc_unpack(rows[rg * SC_ROW_GROUP + r, pl.ds(ch * SC_LANES, SC_LANES)])
                            accs[r] = accs[r] + a * xa + b * xb
                    for r in range(SC_ROW_GROUP):
                        acc_v[pl.ds((slot0 + r) * SC_LANES, SC_LANES)] = accs[r]

        def gate_weights(tslot):
            @pl.loop(0, P_SLOTS // SC_LANES)
            def _(sg):
                first = sg * SC_LANES * SC_LANES
                act = jnp.zeros((SC_LANES,), f32)
                for lane in range(SC_LANES):
                    act = act + plsc.load_gather(acc_v, [first + lanes * SC_LANES + lane])
                w_v[pl.ds(sg * SC_LANES, SC_LANES)] = g_v[tslot, pl.ds(sg * SC_LANES, SC_LANES)] * _sc_gelu_tanh(act)

            @pl.loop(0, P_SLOTS // SC_LANES)
            def _(sg):
                for r in range(SC_LANES):
                    w16_v[pl.ds((sg * SC_LANES + r) * SC_LANES, SC_LANES)] = plsc.load_gather(
                        w_v, [jnp.zeros((SC_LANES,), i32) + (sg * SC_LANES + r)])

        def compute_v(half, rows):
            @pl.loop(0, SC_CHUNKS // SC_CHUNK_GROUP)
            def _(cg):
                first = cg * SC_CHUNK_GROUP

                @pl.loop(0, SC_UNIT_ROWS // SC_ROW_GROUP)
                def _(rg):
                    keep = jnp.where(jnp.logical_or(half == 1, rg > 0), 1.0, 0.0).astype(f32)
                    spots = [pl.ds((j % 2) * SC_WORDS + (first + j // 2) * SC_LANES, SC_LANES)
                             for j in range(2 * SC_CHUNK_GROUP)]
                    accs = [out_v[spot] * keep for spot in spots]
                    for r in range(SC_ROW_GROUP):
                        row = rg * SC_ROW_GROUP + r
                        wv = w16_v[pl.ds((half * SC_UNIT_ROWS + row) * SC_LANES, SC_LANES)]
                        for c in range(SC_CHUNK_GROUP):
                            a, b = _sc_unpack(rows[row, pl.ds((first + c) * SC_LANES, SC_LANES)])
                            accs[2 * c] = accs[2 * c] + a * wv
                            accs[2 * c + 1] = accs[2 * c + 1] + b * wv
                    for spot, acc in zip(spots, accs):
                        out_v[spot] = acc

        for cp in token_copies(0):
            cp.start()
        for cp in token_copies(0):
            cp.wait()
        start(0)

        @pl.loop(0, units)
        def _(g):
            tok, k = g // 4, g % 4

            @pl.when(jnp.logical_and(k == 0, tok + 1 < per))
            def _():
                for cp in token_copies(tok + 1):
                    cp.start()

            @pl.when(jnp.logical_and(k == 3, tok + 1 < per))
            def _():
                for cp in token_copies(tok + 1):
                    cp.wait()

            @pl.when(g + 1 < units)
            def _():
                start(g + 1)

            gather(u_hbm, g).wait()
            rows = rows_v.at[k % 2]

            @pl.when(k < 2)
            def _():
                compute_u(tok % 2, k, rows)

            @pl.when(k == 1)
            def _():
                gate_weights(tok % 2)

            @pl.when(jnp.logical_and(k == 2, tok > 0))
            def _():
                store_out(tok - 1).wait()

            @pl.when(k >= 2)
            def _():
                compute_v(k - 2, rows)

            @pl.when(k == 3)
            def _():
                store_out(tok).start()

        store_out(per - 1).wait()

    return pl.kernel(
        body, mesh=plsc.VectorSubcoreMesh(core_axis_name="c", subcore_axis_name="s"),
        out_type=jax.ShapeDtypeStruct((n_sc, D_MODEL), f32),
        scratch_types=[pltpu.VMEM((2, P_SLOTS), i32), pltpu.VMEM((2, D_MODEL), f32), pltpu.VMEM((2, P_SLOTS), f32),
                       pltpu.VMEM((2, SC_UNIT_ROWS, SC_WORDS), jnp.uint32), pltpu.VMEM((P_SLOTS * SC_LANES,), f32),
                       pltpu.VMEM((P_SLOTS,), f32), pltpu.VMEM((P_SLOTS * SC_LANES,), f32), pltpu.VMEM((D_MODEL,), f32),
                       pltpu.SemaphoreType.DMA((2,)), pltpu.SemaphoreType.DMA((3,)), pltpu.SemaphoreType.DMA],
        compiler_params=pltpu.CompilerParams(needs_layout_passes=False),
        name="sc_peer",
    )(utab32, vtab32, eidx, xn, gate, after)


def kernel(x_prompt, x_sample, cache_k, cache_v, state_C, state_n, state_m, state_conv, norm1_g, w_in, b_gates, rel_bias, conv_w, conv_b, mh_norm_g, w_out, norm2_g, peer_wq, peer_keys, peer_u, peer_v, final_g):
    bp, sp, d = x_prompt.shape
    bs, ts, _ = x_sample.shape
    n_p, n_s = bp * sp, bs * ts
    n = n_p + n_s
    assert n_p % ROW_BLOCK == 0 and n_s % ROW_BLOCK == 0 and d == D_MODEL
    depth = w_in.shape[0]
    assert depth == 1, "the final norm is fused into the last layer's PEER pass"
    l = 0
    xp, xs = x_prompt.reshape(n_p, d), x_sample.reshape(n_s, d)

    zeros = lambda *shp: jnp.zeros(shp, f32)
    mparams = (conv_w[l], conv_b[l], b_gates[l], mh_norm_g[l])
    retrieval = (w_out[l], norm2_g[l], peer_wq[l], peer_keys[l])

    def mixers(proj, seq, lo, hi, state):
        aq, ak, av, mqk, mv, mo, gates = proj
        att = _attn_prompt(aq, ak, av, rel_bias[l], seq, sp, lo, hi)
        h, c, nn, mm = _mlstm(mqk, mv, mo, gates, seq * sp + lo, 1, hi - lo, *state, *mparams)
        conv_rows = mqk[seq * sp + hi - (CONV_W - 1):seq * sp + hi][None]
        return att, h, (c, nn[:, :M_HEADS], mm[:, :M_HEADS, 0], conv_rows)

    fresh = (zeros(1, M_HEADS, M_DH, M_DH), zeros(1, M_HEADS, M_DH), zeros(1, M_HEADS), zeros(1, CONV_W - 1, 2 * M_WIDTH))

    states = [fresh] * bp

    def stage(proj, lo, hi, after, tail=None):
        parts = []
        for seq in range(bp):
            a, b = max(lo, seq * sp), min(hi, (seq + 1) * sp)
            if a < b:
                att, h, states[seq] = mixers(proj, seq, a - seq * sp, b - seq * sp, states[seq])
                parts.append((att, h, b - a, xp, a))
        parts += [tail] if tail is not None else []
        assert 1 <= len(parts) <= 2, "the retrieval kernel reads at most two row sources"
        (att0, h0, n0, x0, r0), (att1, h1, n1, x1, r1) = parts[0], parts[-1]
        n1, r1 = (n1, r1) if len(parts) == 2 else (0, 0)
        return _mid(x0, r0, x1, r1, att0, att1, h0, h1, n0, n1, after, *retrieval)

    sc_tables = _sc_table(peer_u[l]), _sc_table(peer_v[l])
    peers, x1s, lo, after = [], [], 0, jnp.zeros((SUBLANES, P_SLOTS), i32)
    proj = _inproj(xp, SC_STAGE_ENDS[0], xs, norm1_g[l], w_in[l], after)
    for hi in SC_STAGE_ENDS:
        x1, xn, eidx, _, gate = stage(proj, lo, hi, after)
        peers.append(_sc_peer(*sc_tables, eidx, xn, gate, peers[-1] if peers else zeros(SUBLANES, D_MODEL)))
        x1s.append(x1)
        if lo == 0:
            proj = _inproj(xp, n_p, xs, norm1_g[l], w_in[l], eidx)
        lo, after = hi, eidx
    aq, ak, av, mqk, mv, mo, gates = proj

    lcache = cache_k.shape[2]
    att_s = _attn_sample(aq, ak, av, cache_k[l].reshape(bs, lcache, A_WIDTH),
                         cache_v[l].reshape(bs, lcache, A_WIDTH), rel_bias[l], n_p, bs, ts)
    h_s, c_s, nn_s, mm_s = _mlstm(mqk, mv, mo, gates, n_p, bs, ts, state_C[l], state_n[l], state_m[l],
                                  state_conv[l], *mparams)
    x1, xn, _, eidx_t, gate = stage(proj, lo, n_p, after, tail=(att_s, h_s, n_s, xs, 0))
    w = _peer_u(eidx_t, xn, gate, _expert_table(peer_u[l]), n - lo)
    peers.append(_peer_v(eidx_t, w, _expert_table(peer_v[l])))
    x1s.append(x1)
    y_p, y_s = _final(peers, x1s, final_g, n_p)
    c_p, nn_p, mm_p = (jnp.concatenate(per_seq, axis=0) for per_seq in zip(*(st[:3] for st in states)))

    def tail(a, row0, bsz, t, keep):
        return jnp.stack([a[row0 + (b + 1) * t - keep:row0 + (b + 1) * t] for b in range(bsz)])

    keep = min(WINDOW, sp)
    heads = lambda a: a.reshape(a.shape[0], a.shape[1], A_HEADS, A_DH)
    ctail = CONV_W - 1
    conv_tail = lambda buf, a, row0, bsz, t: jnp.concatenate([buf.astype(a.dtype), tail(a, row0, bsz, t, min(ctail, t))],
                                                             axis=1)[:, -ctail:]
    st = lambda a: a[None]
    return (y_p.reshape(bp, sp, d), y_s.reshape(bs, ts, d),
            st(heads(tail(ak, 0, bp, sp, keep))), st(heads(tail(av, 0, bp, sp, keep))),
            st(c_p), st(nn_p), st(mm_p),
            st(conv_tail(zeros(bp, ctail, 2 * M_WIDTH), mqk, 0, bp, sp)),
            st(heads(ak[n_p:].reshape(bs, ts, A_WIDTH))), st(heads(av[n_p:].reshape(bs, ts, A_WIDTH))),
            st(c_s), st(nn_s[:, :M_HEADS]), st(mm_s[:, :M_HEADS, 0]),
            st(conv_tail(state_conv[l], mqk, n_p, bs, ts)))
```

```python
import functools
import math

import jax
import jax.numpy as jnp
from jax import lax
from jax.experimental import pallas as pl
from jax.experimental.pallas import tpu as pltpu
from jax.experimental.pallas import tpu_sc as plsc

f32 = jnp.float32
bf16 = jnp.bfloat16
MIXER_DTYPE = bf16
i32 = jnp.int32

D_MODEL = 1024
CHUNK = 64
A_HEADS = 8
A_DH = 64
A_WIDTH = A_HEADS * A_DH
BAND_CHUNKS = 8
WINDOW = BAND_CHUNKS * CHUNK
MAX_REL = 128
ATT_SCALE = A_DH ** -0.5
M_HEADS = 4
M_DH = 128
M_WIDTH = M_HEADS * M_DH
CONV_W = 4
IN_WIDTHS = (A_WIDTH, A_WIDTH, A_WIDTH, 2 * M_WIDTH, M_WIDTH, M_WIDTH)
P_HEADS = 8
P_DKEY = 256
N_KEYS = 128
P_TOPK = 16
P_SLOTS = P_HEADS * P_TOPK
EPS = 1e-6
NEG = -1e30

LANES = 128
SUBLANES = 8
ROW_BLOCK = 256
ATT_TILE = 512
ATT_SUB = 256
ATT_KEYS = ATT_SUB + WINDOW
ROW_TILES = D_MODEL // LANES
G_ROWS = P_SLOTS * ROW_TILES
TOKEN_UNROLL = 8
INDEX_GROUPS = P_SLOTS // SUBLANES
VMEM_LIMIT = 56 * 1024 * 1024

SC_WORKERS = 32
SC_LANES = 16
SC_UNIT_ROWS = P_SLOTS // 2
SC_ROW_GROUP = 16
SC_WORDS = D_MODEL // 2
SC_CHUNKS = SC_WORDS // SC_LANES
SC_CHUNK_GROUP = 8
SC_STAGE_ENDS = (2048, 6144, 20480)


def _rms(x, g):
    return x * lax.rsqrt(jnp.mean(x * x, axis=-1, keepdims=True) + EPS) * g


def _split_bf16(x):
    hi = x.astype(bf16)
    lo = (x - hi.astype(f32)).astype(bf16)
    return hi, lo


def _pair_specs(nbp, width, first=0, second=0):
    return (pl.BlockSpec((ROW_BLOCK, width), lambda i: (first + jnp.minimum(i, nbp - 1), 0)),
            pl.BlockSpec((ROW_BLOCK, width), lambda i: (second + jnp.maximum(i - nbp, 0), 0)))


def _pick(nbp, p_ref, s_ref):
    return jnp.where(pl.program_id(0) < nbp, p_ref[...], s_ref[...])


def _inproj_kernel(xp_ref, xs_ref, g_ref, w_ref, wgh_ref, wgl_ref, after_ref,
                   aq_ref, ak_ref, av_ref, mqk_ref, mv_ref, mo_ref, gate_ref, *, nbp):
    del after_ref
    xn = _rms(_pick(nbp, xp_ref, xs_ref), g_ref[...])
    xh, xl = _split_bf16(xn)

    lo = 0
    for out_ref, width in zip((aq_ref, ak_ref, av_ref, mqk_ref, mv_ref, mo_ref), IN_WIDTHS):
        out_ref[...] = jnp.dot(xh, w_ref[:, lo:lo + width], preferred_element_type=f32)
        lo += width
    gate_ref[...] = (jnp.dot(xh, wgh_ref[...], preferred_element_type=f32)
                     + jnp.dot(xl, wgh_ref[...], preferred_element_type=f32)
                     + jnp.dot(xh, wgl_ref[...], preferred_element_type=f32))


def _inproj(xp, n_p, xs, g1, w_in, after):
    n = n_p + xs.shape[0]
    nbp = n_p // ROW_BLOCK
    main = sum(IN_WIDTHS)
    w_main = w_in[:, :main].astype(bf16)
    wg = jnp.pad(w_in[:, main:], ((0, 0), (0, LANES - 2 * M_HEADS)))
    wgh, wgl = _split_bf16(wg)
    widths = IN_WIDTHS + (LANES,)
    row = lambda w: pl.BlockSpec((ROW_BLOCK, w), lambda i: (i, 0))
    full = lambda a: pl.BlockSpec(a.shape, lambda i: (0,) * a.ndim)
    g = g1.reshape(1, D_MODEL)
    return pl.pallas_call(
        functools.partial(_inproj_kernel, nbp=nbp),
        grid=(n // ROW_BLOCK,),
        in_specs=[*_pair_specs(nbp, D_MODEL), full(g), full(w_main), full(wgh), full(wgl),
                  pl.BlockSpec((SUBLANES, P_SLOTS), lambda i: (0, 0))],
        out_specs=[row(w) for w in widths],
        out_shape=[jax.ShapeDtypeStruct((n, w), f32) for w in widths],
        compiler_params=pltpu.CompilerParams(dimension_semantics=("parallel",), vmem_limit_bytes=VMEM_LIMIT),
        name="inproj",
    )(xp, xs, g, w_main, wgh, wgl, after)


def _attn_heads(q, k, v, bias_ref, key_ok):
    outs = []
    for h in range(A_HEADS):
        sl = slice(h * A_DH, (h + 1) * A_DH)
        s = lax.dot_general(q[:, sl], k[:, sl], (((1,), (1,)), ((), ())), preferred_element_type=f32)
        s = s * ATT_SCALE + bias_ref[h]
        if key_ok is not None:
            s = jnp.where(key_ok, s, NEG)
        m = jnp.max(s, axis=-1, keepdims=True)
        p = jnp.exp(s - m)
        l = jnp.sum(p, axis=-1, keepdims=True)
        o = jnp.dot(p.astype(bf16), v[:, sl], preferred_element_type=f32)
        outs.append(o / l)
    return jnp.concatenate(outs, axis=-1)


def _attn_prompt_kernel(q_ref, k0_ref, k1_ref, v0_ref, v1_ref, bias_ref, o_ref, *, t0):
    t = t0 + pl.program_id(1)
    q = q_ref[...].astype(bf16)
    k = jnp.concatenate([k0_ref[...], k1_ref[...]], axis=0).astype(bf16)
    v = jnp.concatenate([v0_ref[...], v1_ref[...]], axis=0).astype(bf16)
    col = lax.broadcasted_iota(i32, (1, ATT_KEYS), 1)
    for s in range(ATT_TILE // ATT_SUB):
        lo = s * ATT_SUB
        key_ok = (t * ATT_TILE + lo + col) >= WINDOW
        o_ref[lo:lo + ATT_SUB, :] = _attn_heads(q[lo:lo + ATT_SUB], k[lo:lo + ATT_KEYS], v[lo:lo + ATT_KEYS],
                                                 bias_ref, key_ok).astype(o_ref.dtype)


def _attn_sample_kernel(q_ref, k_ref, v_ref, bias_ref, o_ref):
    o_ref[...] = _attn_heads(q_ref[...].astype(bf16), k_ref[0].astype(bf16), v_ref[0].astype(bf16), bias_ref,
                             None).astype(o_ref.dtype)


def _rel_bias_table(rel_bias, rows, cols, offset, valid):
    span = rows + cols - 1
    rel = offset + rows - 1 - jnp.arange(span)
    diag = rel_bias[:, jnp.clip(rel, -MAX_REL, MAX_REL) + MAX_REL].astype(f32)
    diag = jnp.pad(diag, ((0, 0), (0, 1)))
    flat = jnp.tile(diag, (1, rows))[:, rows - 1:rows - 1 + rows * span]
    return jnp.where(valid[None], flat.reshape(-1, rows, span)[:, :, :cols], NEG)


def _attn_prompt(q, k, v, rel_bias, seq, s, lo, hi):
    assert s % ATT_TILE == 0 and WINDOW == ATT_TILE and lo % ATT_TILE == 0 and hi % ATT_TILE == 0
    nt, t0, cnt = s // ATT_TILE, lo // ATT_TILE, (hi - lo) // ATT_TILE
    i = jnp.arange(ATT_SUB)[:, None]
    j = jnp.arange(ATT_KEYS)[None, :]
    off = j - (i // CHUNK) * CHUNK
    bias = _rel_bias_table(rel_bias, ATT_SUB, ATT_KEYS, WINDOW, (off >= 0) & (off < WINDOW + CHUNK))
    cur = pl.BlockSpec((ATT_TILE, A_WIDTH), lambda b, t: (seq * nt + t0 + t, 0))
    prev = pl.BlockSpec((ATT_TILE, A_WIDTH), lambda b, t: (seq * nt + jnp.maximum(t0 + t - 1, 0), 0))
    return pl.pallas_call(
        functools.partial(_attn_prompt_kernel, t0=t0),
        grid=(1, cnt),
        in_specs=[cur, prev, cur, prev, cur, pl.BlockSpec(bias.shape, lambda b, t: (0, 0, 0))],
        out_specs=pl.BlockSpec((ATT_TILE, A_WIDTH), lambda b, t: (t, 0)),
        out_shape=jax.ShapeDtypeStruct((hi - lo, A_WIDTH), MIXER_DTYPE),
        compiler_params=pltpu.CompilerParams(dimension_semantics=("parallel", "parallel"),
                                             vmem_limit_bytes=VMEM_LIMIT),
        name="attn_prompt",
    )(q, k, k, v, v, bias)


def _attn_sample(q, k, v, ck, cv, rel_bias, row0, bsz, t):
    l = ck.shape[1]
    assert row0 % t == 0
    keys = -(-(l + t) // LANES) * LANES
    padk = ((0, 0), (0, keys - l - t), (0, 0))
    kk = jnp.pad(jnp.concatenate([ck, k[row0:].reshape(bsz, t, A_WIDTH)], axis=1), padk)
    vv = jnp.pad(jnp.concatenate([cv, v[row0:].reshape(bsz, t, A_WIDTH)], axis=1), padk)
    j = jnp.arange(keys)[None, :]
    bias = _rel_bias_table(rel_bias, t, keys, l, jnp.broadcast_to(j < l + t, (t, keys)))
    return pl.pallas_call(
        _attn_sample_kernel,
        grid=(bsz,),
        in_specs=[pl.BlockSpec((t, A_WIDTH), lambda b: (row0 // t + b, 0)),
                  pl.BlockSpec((1, keys, A_WIDTH), lambda b: (b, 0, 0)),
                  pl.BlockSpec((1, keys, A_WIDTH), lambda b: (b, 0, 0)),
                  pl.BlockSpec(bias.shape, lambda b: (0, 0, 0))],
        out_specs=pl.BlockSpec((t, A_WIDTH), lambda b: (b, 0)),
        out_shape=jax.ShapeDtypeStruct((bsz * t, A_WIDTH), MIXER_DTYPE),
        compiler_params=pltpu.CompilerParams(dimension_semantics=("parallel",), vmem_limit_bytes=VMEM_LIMIT),
        name="attn_sample",
    )(q, kk, vv, bias)


def _mlstm_chunk(a, vall, o_in, gate_in, cw_ref, cb_ref, bg_ref, mhg_ref, c_s, n_s, m_s, prev_s):
    lc = a.shape[0]
    ext = jnp.concatenate([prev_s[...], a], axis=0)
    conv = cb_ref[...]
    for j in range(CONV_W):
        lo = SUBLANES - (CONV_W - 1) + j
        conv = conv + cw_ref[j:j + 1, :] * ext[lo:lo + lc]
    prev_s[...] = a[lc - SUBLANES:lc]
    qk = conv * jax.nn.sigmoid(conv)

    z = gate_in + bg_ref[...]
    lane = lax.broadcasted_iota(i32, (lc, LANES), 1)
    row = lax.broadcasted_iota(i32, (lc, LANES), 0)
    logf = jnp.minimum(z, 0.0) - jnp.log1p(jnp.exp(-jnp.abs(z)))
    cum = jnp.where((lane >= M_HEADS) & (lane < 2 * M_HEADS), logf, 0.0)
    shift = 1
    while shift < lc:
        cum = cum + jnp.where(row >= shift, pltpu.roll(cum, shift, axis=0), 0.0)
        shift *= 2
    zc = jnp.where(lane < M_HEADS, z, cum)
    zt = jnp.concatenate([zc, jnp.zeros((LANES - lc, LANES), f32)], axis=0).T[:, :lc]

    ri = lax.broadcasted_iota(i32, (lc, lc), 0)
    ci = lax.broadcasted_iota(i32, (lc, lc), 1)
    causal = ri >= ci
    state = [(m_s[h:h + 1, 0:1], c_s[h], n_s[h:h + 1, :]) for h in range(M_HEADS)]
    hs, new_state = [], []
    for h in range(M_HEADS):
        sl = slice(h * M_DH, (h + 1) * M_DH)
        q = qk[:, sl]
        k = qk[:, M_WIDTH + h * M_DH:M_WIDTH + (h + 1) * M_DH] * (M_DH ** -0.5)
        v = vall[:, sl]
        i_col = zc[:, h:h + 1]
        b_col = zc[:, M_HEADS + h:M_HEADS + h + 1]
        i_row = zt[h:h + 1, :]
        b_row = zt[M_HEADS + h:M_HEADS + h + 1, :]
        m_prev, c_prev, n_prev = state[h]

        dmat = jnp.where(causal, b_col - b_row + i_row, NEG)
        inter = b_col + m_prev
        mt = jnp.maximum(jnp.max(dmat, axis=-1, keepdims=True), inter)
        w_intra = jnp.exp(dmat - mt)
        w_inter = jnp.exp(inter - mt)
        qb, kb, vb = q.astype(bf16), k.astype(bf16), v.astype(bf16)
        s = lax.dot_general(qb, kb, (((1,), (1,)), ((), ())), preferred_element_type=f32) * w_intra
        num = (w_inter * jnp.dot(qb, c_prev.astype(bf16), preferred_element_type=f32)
               + jnp.dot(s.astype(bf16), vb, preferred_element_type=f32))
        den = w_inter * jnp.sum(q * n_prev, axis=-1, keepdims=True) + jnp.sum(s, axis=-1, keepdims=True)
        hh = num / jnp.maximum(jnp.abs(den), jnp.exp(-mt))
        m_new = mt[lc - 1:lc, :]
        b_last = b_col[lc - 1:lc, :]
        w_s = jnp.exp(b_last - b_col + i_col - m_new)
        decay = jnp.exp(b_last + m_prev - m_new)
        kw = k * w_s
        new_state.append((jnp.broadcast_to(m_new, (1, LANES)),
                          decay * c_prev + lax.dot_general(kw.astype(bf16), vb, (((0,), (0,)), ((), ())),
                                                           preferred_element_type=f32),
                          decay * n_prev + jnp.sum(kw, axis=0, keepdims=True)))
        hs.append(hh * lax.rsqrt(jnp.mean(hh * hh, axis=-1, keepdims=True) + EPS))

    for h, (m_new, c_new, n_new) in enumerate(new_state):
        m_s[h:h + 1, :] = m_new
        c_s[h] = c_new
        n_s[h:h + 1, :] = n_new
    return jnp.concatenate(hs, axis=-1) * mhg_ref[...] * jax.nn.sigmoid(o_in)


def _mlstm_kernel(qk_ref, v_ref, o_ref, gate_ref, c0_ref, n0_ref, m0_ref, cbuf_ref,
                  cw_ref, cb_ref, bg_ref, mhg_ref,
                  h_ref, cout_ref, nout_ref, mout_ref,
                  c_s, n_s, m_s, prev_s):
    c = pl.program_id(1)

    @pl.when(c == 0)
    def _():
        c_s[...] = c0_ref[0]
        n_s[...] = n0_ref[0]
        m_s[...] = m0_ref[0]
        prev_s[...] = cbuf_ref[0]

    h_ref[...] = _mlstm_chunk(qk_ref[...], v_ref[...], o_ref[...], gate_ref[...],
                              cw_ref, cb_ref, bg_ref, mhg_ref, c_s, n_s, m_s, prev_s).astype(h_ref.dtype)

    @pl.when(c == pl.num_programs(1) - 1)
    def _():
        cout_ref[0] = c_s[...]
        nout_ref[0] = n_s[...]
        mout_ref[0] = m_s[...]


def _mlstm(mqk, mv, mo, gates, row0, bsz, t, c0, n0, m0, cbuf, conv_w, conv_b, b_gates, mh_g):
    lc = min(CHUNK, t)
    step = lc
    nc = t // step
    assert t % step == 0 and lc % SUBLANES == 0 and row0 % step == 0
    n0p = jnp.pad(n0.astype(f32), ((0, 0), (0, SUBLANES - M_HEADS), (0, 0)))
    m0p = jnp.pad(jnp.broadcast_to(m0.astype(f32)[:, :, None], (bsz, M_HEADS, LANES)),
                  ((0, 0), (0, SUBLANES - M_HEADS), (0, 0)))
    cbp = jnp.pad(cbuf.astype(f32), ((0, 0), (SUBLANES - (CONV_W - 1), 0), (0, 0)))
    bg = jnp.pad(b_gates.astype(f32), (0, LANES - 2 * M_HEADS)).reshape(1, LANES)
    seq = lambda w: pl.BlockSpec((step, w), lambda b, c: (row0 // step + b * nc + c, 0))
    out_seq = pl.BlockSpec((step, M_WIDTH), lambda b, c: (b * nc + c, 0))
    per_b = lambda shp: pl.BlockSpec((1,) + shp, lambda b, c: (b,) + (0,) * len(shp))
    full = lambda a: pl.BlockSpec(a.shape, lambda b, c: (0,) * a.ndim)
    cb = conv_b.reshape(1, -1)
    mhg = mh_g.reshape(1, -1)
    return pl.pallas_call(
        _mlstm_kernel,
        grid=(bsz, nc),
        in_specs=[seq(2 * M_WIDTH), seq(M_WIDTH), seq(M_WIDTH), seq(LANES),
                  per_b((M_HEADS, M_DH, M_DH)), per_b((SUBLANES, M_DH)), per_b((SUBLANES, LANES)),
                  per_b((SUBLANES, 2 * M_WIDTH)),
                  full(conv_w), full(cb), full(bg), full(mhg)],
        out_specs=[out_seq, per_b((M_HEADS, M_DH, M_DH)), per_b((SUBLANES, M_DH)), per_b((SUBLANES, LANES))],
        out_shape=[jax.ShapeDtypeStruct((bsz * t, M_WIDTH), MIXER_DTYPE),
                   jax.ShapeDtypeStruct((bsz, M_HEADS, M_DH, M_DH), f32),
                   jax.ShapeDtypeStruct((bsz, SUBLANES, M_DH), f32),
                   jax.ShapeDtypeStruct((bsz, SUBLANES, LANES), f32)],
        scratch_shapes=[pltpu.VMEM((M_HEADS, M_DH, M_DH), f32), pltpu.VMEM((SUBLANES, M_DH), f32),
                        pltpu.VMEM((SUBLANES, LANES), f32), pltpu.VMEM((SUBLANES, 2 * M_WIDTH), f32)],
        compiler_params=pltpu.CompilerParams(dimension_semantics=("parallel", "arbitrary"),
                                             vmem_limit_bytes=VMEM_LIMIT),
        name="mlstm",
    )(mqk, mv, mo, gates, c0.astype(f32), n0p, m0p, cbp, conv_w, cb, bg, mhg)


def _topk_rows(s, k, payload=None):
    n = s.shape[0]
    rows = lax.broadcasted_iota(i32, s.shape, 0).astype(f32)
    vals, ids = [], []
    for _ in range(k):
        m = jnp.max(s, axis=0, keepdims=True)
        pos = jnp.min(jnp.where(s == m, rows, float(n)), axis=0, keepdims=True)
        sel = rows == pos
        vals.append(m)
        ids.append(pos if payload is None else jnp.max(jnp.where(sel, payload, -1.0), axis=0, keepdims=True))
        s = jnp.where(sel, -jnp.inf, s)
    return jnp.concatenate(vals, axis=0), jnp.concatenate(ids, axis=0)


def _mid_kernel(xp_ref, xs_ref, attp_ref, atts_ref, hp_ref, hs_ref, wo_ref, g2_ref, wq_ref, keys_ref, after_ref,
                x1_ref, xn_ref, eidx_ref, eidx_t_ref, gate_ref, *, nbp):
    del after_ref
    cat = jnp.concatenate([_pick(nbp, attp_ref, atts_ref), _pick(nbp, hp_ref, hs_ref)], axis=-1).astype(bf16)
    x1 = _pick(nbp, xp_ref, xs_ref) + jnp.dot(cat, wo_ref[...], preferred_element_type=f32)
    x1_ref[...] = x1
    xn = _rms(x1, g2_ref[...])
    xn_ref[...] = xn
    xb = xn.astype(bf16)
    e_rows, g_rows = [], []
    for h in range(P_HEADS):
        q = jnp.dot(xb, wq_ref[:, h * P_DKEY:(h + 1) * P_DKEY], preferred_element_type=f32).astype(bf16)
        half = []
        for p in range(2):
            st = lax.dot_general(keys_ref[h, p], q[:, p * N_KEYS:(p + 1) * N_KEYS], (((1,), (1,)), ((), ())),
                                 preferred_element_type=f32)
            half.append(_topk_rows(st, P_TOPK))
        (v0, i0), (v1, i1) = half
        width = [P_TOPK // (a + 1) for a in range(P_TOPK)]
        fill = -sum(width) % SUBLANES
        cand = jnp.concatenate([v0[a:a + 1] + v1[:width[a]] for a in range(P_TOPK)]
                               + [jnp.full((fill, v0.shape[1]), -jnp.inf, f32)], axis=0)
        cidx = jnp.concatenate([i0[a:a + 1] * float(N_KEYS) + i1[:width[a]] for a in range(P_TOPK)]
                               + [jnp.zeros((fill, v0.shape[1]), f32)], axis=0)
        top_s, eid = _topk_rows(cand, P_TOPK, payload=cidx)
        ex = jnp.exp(top_s - top_s[0:1])
        e_rows.append(eid)
        g_rows.append(ex / jnp.sum(ex, axis=0, keepdims=True))
    eidx_t = jnp.concatenate(e_rows, axis=0)
    eidx_t_ref[...] = eidx_t.astype(i32)
    eidx_ref[...] = eidx_t.T.astype(i32)
    gate_ref[...] = jnp.concatenate(g_rows, axis=0).T


def _mid(xp, xp_row0, xs, xs_row0, attp, atts, hp, hs, n_a, n_b, after, w_out, g2, wq, keys):
    n = n_a + n_b
    nbp = n_a // ROW_BLOCK
    assert n_a % ROW_BLOCK == 0 and n_b % ROW_BLOCK == 0 and xp_row0 % ROW_BLOCK == 0 and xs_row0 % ROW_BLOCK == 0
    row = lambda w: pl.BlockSpec((ROW_BLOCK, w), lambda i: (i, 0))
    full = lambda a: pl.BlockSpec(a.shape, lambda i: (0,) * a.ndim)
    wo = w_out.astype(bf16)
    wqb = wq.astype(bf16)
    kb = keys.astype(bf16)
    g = g2.reshape(1, D_MODEL)
    return pl.pallas_call(
        functools.partial(_mid_kernel, nbp=nbp),
        grid=(n // ROW_BLOCK,),
        in_specs=[*_pair_specs(nbp, D_MODEL, xp_row0 // ROW_BLOCK, xs_row0 // ROW_BLOCK), *_pair_specs(nbp, A_WIDTH),
                  *_pair_specs(nbp, M_WIDTH), full(wo), full(g), full(wqb), full(kb),
                  pl.BlockSpec((SUBLANES, P_SLOTS), lambda i: (0, 0))],
        out_specs=[row(D_MODEL), row(D_MODEL), row(P_SLOTS), pl.BlockSpec((P_SLOTS, ROW_BLOCK), lambda i: (0, i)),
                   row(P_SLOTS)],
        out_shape=[jax.ShapeDtypeStruct((n, D_MODEL), f32), jax.ShapeDtypeStruct((n, D_MODEL), f32),
                   jax.ShapeDtypeStruct((n, P_SLOTS), i32), jax.ShapeDtypeStruct((P_SLOTS, n), i32),
                   jax.ShapeDtypeStruct((n, P_SLOTS), f32)],
        compiler_params=pltpu.CompilerParams(dimension_semantics=("parallel",), vmem_limit_bytes=VMEM_LIMIT),
        name="outproj_retrieve",
    )(xp, xs, attp, atts, hp, hs, wo, g, wqb, kb, after)


def _gather_rows(eidx_refs, tab_ref, g_ref, t):
    for i in range(SUBLANES):
        for k, eidx_ref in enumerate(eidx_refs):
            r = k * SUBLANES + i
            g_ref[pl.ds(r * ROW_TILES, ROW_TILES), :] = tab_ref[eidx_ref[i, t]]


def _tile_rows(t, rows=SUBLANES):
    return pl.ds(pl.multiple_of(t * rows, rows), rows)


def _pipelined_tokens(nt, eidx_refs, tab_ref, g0_s, g1_s, compute):
    bufs = (g0_s, g1_s)
    _gather_rows(eidx_refs, tab_ref, g0_s, 0)

    def body(j, carry):
        for u in range(TOKEN_UNROLL):
            t = TOKEN_UNROLL * j + u
            _gather_rows(eidx_refs, tab_ref, bufs[(u + 1) % 2], jnp.minimum(t + 1, nt - 1))
            compute(t, bufs[u % 2])
        return carry

    lax.fori_loop(0, nt // TOKEN_UNROLL, body, 0)


def _gelu_tanh(x):
    return 0.5 * x * (1.0 + jnp.tanh(math.sqrt(2.0 / math.pi) * (x + 0.044715 * (x * x * x))))


def _peer_u_kernel(*refs):
    eidx_refs, (xn_ref, g_ref, tab_ref, w_ref, xl_s, r_s, g0_s, g1_s) = refs[:INDEX_GROUPS], refs[INDEX_GROUPS:]
    nt = xn_ref.shape[0]
    xn = xn_ref[...]
    xh = xn.astype(bf16).astype(f32)
    xl = xn - xh
    for k in range(ROW_TILES):
        xl_s[pl.ds(k, nt, stride=2 * SUBLANES), :] = xh[:, k * LANES:(k + 1) * LANES]
        xl_s[pl.ds(SUBLANES + k, nt, stride=2 * SUBLANES), :] = xl[:, k * LANES:(k + 1) * LANES]
    diag = (lax.broadcasted_iota(i32, (SUBLANES, G_ROWS), 1) % ROW_TILES
            == lax.broadcasted_iota(i32, (SUBLANES, G_ROWS), 0))

    def compute(t, g_s):
        lhs = xl_s[_tile_rows(t, 2 * SUBLANES), :].astype(bf16)
        out = lax.dot_general(lhs, g_s[...], (((1,), (1,)), ((), ())), preferred_element_type=f32)
        part = jnp.where(diag, out[:SUBLANES] + out[SUBLANES:], 0.0)
        for c in range(ROW_TILES):
            r_s[c, _tile_rows(t), :] = part[:, c * LANES:(c + 1) * LANES]

    _pipelined_tokens(nt, eidx_refs, tab_ref, g0_s, g1_s, compute)
    cols = []
    for c in range(ROW_TILES):
        acc = r_s[c, pl.ds(0, nt, stride=SUBLANES), :]
        for k in range(1, SUBLANES):
            acc = acc + r_s[c, pl.ds(k, nt, stride=SUBLANES), :]
        cols.append(acc)
    s = jnp.concatenate(cols, axis=-1)
    fold = (lax.broadcasted_iota(i32, (G_ROWS, P_SLOTS), 0) // ROW_TILES
            == lax.broadcasted_iota(i32, (G_ROWS, P_SLOTS), 1)).astype(bf16)
    sh, sl = _split_bf16(s)
    act = jnp.dot(sh, fold, preferred_element_type=f32) + jnp.dot(sl, fold, preferred_element_type=f32)
    w_ref[...] = g_ref[...] * _gelu_tanh(act)


def _peer_v_kernel(*refs):
    eidx_refs, (w_ref, tab_ref, peer_ref, wl_s, o_s, g0_s, g1_s) = refs[:INDEX_GROUPS], refs[INDEX_GROUPS:]
    nt = w_ref.shape[0]
    spread = (lax.broadcasted_iota(i32, (P_SLOTS, G_ROWS), 1) // ROW_TILES
              == lax.broadcasted_iota(i32, (P_SLOTS, G_ROWS), 0)).astype(bf16)
    wexp = jnp.dot(w_ref[...].astype(bf16), spread, preferred_element_type=f32)
    lane = lax.broadcasted_iota(i32, (nt, LANES), 1)
    for c in range(ROW_TILES):
        wc = wexp[:, c * LANES:(c + 1) * LANES]
        for k in range(SUBLANES):
            wl_s[c, pl.ds(k, nt, stride=SUBLANES), :] = jnp.where(lane % ROW_TILES == k, wc, 0.0)

    def compute(t, g_s):
        lhs = jnp.concatenate([wl_s[c, _tile_rows(t), :] for c in range(ROW_TILES)], axis=-1).astype(bf16)
        o_s[_tile_rows(t), :] = jnp.dot(lhs, g_s[...], preferred_element_type=f32)

    _pipelined_tokens(nt, eidx_refs, tab_ref, g0_s, g1_s, compute)
    for k in range(ROW_TILES):
        peer_ref[:, k * LANES:(k + 1) * LANES] = o_s[pl.ds(k, nt, stride=SUBLANES), :]


def _final_kernel(*refs, starts, nbp):
    k = len(starts)
    peer_refs, x_refs, (gf_ref, yp_ref, ys_ref) = refs[:k], refs[k:2 * k], refs[2 * k:]
    resid = x_refs[0][...] + peer_refs[0][...]
    for start, p_ref, x_ref in zip(starts[1:], peer_refs[1:], x_refs[1:]):
        resid = jnp.where(pl.program_id(0) >= start, x_ref[...] + p_ref[...], resid)
    y = _rms(resid, gf_ref[...])

    @pl.when(pl.program_id(0) < nbp)
    def _():
        yp_ref[...] = y

    @pl.when(pl.program_id(0) >= nbp)
    def _():
        ys_ref[...] = y


def _final(peers, x1s, gf, n_p):
    counts = [x.shape[0] // ROW_BLOCK for x in x1s]
    starts = [sum(counts[:k]) for k in range(len(counts))]
    n, nbp = sum(counts) * ROW_BLOCK, n_p // ROW_BLOCK
    g = gf.reshape(1, D_MODEL)
    seg = [pl.BlockSpec((ROW_BLOCK, D_MODEL), lambda i, s=s, c=c: (jnp.clip(i - s, 0, c - 1), 0))
           for s, c in zip(starts, counts)]
    return pl.pallas_call(
        functools.partial(_final_kernel, starts=tuple(starts), nbp=nbp),
        grid=(n // ROW_BLOCK,),
        in_specs=[*seg, *seg, pl.BlockSpec(g.shape, lambda i: (0, 0))],
        out_specs=list(_pair_specs(nbp, D_MODEL)),
        out_shape=[jax.ShapeDtypeStruct((n_p, D_MODEL), f32), jax.ShapeDtypeStruct((n - n_p, D_MODEL), f32)],
        compiler_params=pltpu.CompilerParams(dimension_semantics=("arbitrary",), vmem_limit_bytes=VMEM_LIMIT),
        name="final_norm",
    )(*peers, *x1s, g)


def _peer_specs():
    row = lambda w: pl.BlockSpec((ROW_BLOCK, w), lambda i: (i, 0))
    idx = [pl.BlockSpec((SUBLANES, ROW_BLOCK), lambda i, k=k: (k, i), memory_space=pltpu.SMEM)
           for k in range(INDEX_GROUPS)]
    tab = pl.BlockSpec(memory_space=pltpu.VMEM)
    gscr = pltpu.VMEM((G_ROWS, LANES), bf16)
    params = pltpu.CompilerParams(dimension_semantics=("arbitrary",), vmem_limit_bytes=VMEM_LIMIT)
    return row, idx, tab, gscr, params


def _expert_table(tab):
    return tab.astype(bf16).reshape(tab.shape[0], ROW_TILES, LANES)


def _peer_u(eidx_t, xn, g, utab, n):
    row, idx, tab, gscr, params = _peer_specs()
    return pl.pallas_call(
        _peer_u_kernel,
        grid=(n // ROW_BLOCK,),
        in_specs=[*idx, row(D_MODEL), row(P_SLOTS), tab],
        out_specs=row(P_SLOTS),
        out_shape=jax.ShapeDtypeStruct((n, P_SLOTS), f32),
        scratch_shapes=[pltpu.VMEM((ROW_BLOCK * 2 * SUBLANES, LANES), f32),
                        pltpu.VMEM((ROW_TILES, ROW_BLOCK * SUBLANES, LANES), f32), gscr, gscr],
        compiler_params=params,
        name="peer_u",
    )(*[eidx_t] * INDEX_GROUPS, xn, g, utab)


def _peer_v(eidx_t, w, vtab):
    n = w.shape[0]
    row, idx, tab, gscr, params = _peer_specs()
    return pl.pallas_call(
        _peer_v_kernel,
        grid=(n // ROW_BLOCK,),
        in_specs=[*idx, row(P_SLOTS), tab],
        out_specs=row(D_MODEL),
        out_shape=jax.ShapeDtypeStruct((n, D_MODEL), f32),
        scratch_shapes=[pltpu.VMEM((ROW_TILES, ROW_BLOCK * SUBLANES, LANES), f32),
                        pltpu.VMEM((ROW_BLOCK * SUBLANES, LANES), f32), gscr, gscr],
        compiler_params=params,
        name="peer_v",
    )(*[eidx_t] * INDEX_GROUPS, w, vtab)


def _sc_table_kernel(t_ref, o_ref):
    t = t_ref[...]
    o_ref[...] = pltpu.pack_elementwise([t[:, :SC_WORDS], t[:, SC_WORDS:]], packed_dtype=bf16)


def _sc_table(tab):
    e = tab.shape[0]
    return pl.pallas_call(
        _sc_table_kernel,
        grid=(e // ROW_BLOCK,),
        in_specs=[pl.BlockSpec((ROW_BLOCK, D_MODEL), lambda i: (i, 0))],
        out_specs=pl.BlockSpec((ROW_BLOCK, SC_WORDS), lambda i: (i, 0)),
        out_shape=jax.ShapeDtypeStruct((e, SC_WORDS), jnp.uint32),
        compiler_params=pltpu.CompilerParams(dimension_semantics=("parallel",), vmem_limit_bytes=VMEM_LIMIT),
        name="sc_table",
    )(tab)


def _sc_unpack(words):
    return plsc.unpack(plsc.bitcast(words, bf16), format=plsc.PackFormat.INTERLEAVED)


def _sc_gelu_tanh(x):
    z = math.sqrt(2.0 / math.pi) * (x + 0.044715 * (x * x * x))
    return 0.5 * x * (2.0 - 2.0 / (jnp.exp(2.0 * z) + 1.0))


def _sc_peer(utab32, vtab32, eidx, xn, gate, after):
    n_sc = eidx.shape[0]
    per = n_sc // SC_WORKERS
    units = 4 * per
    assert n_sc % (SC_WORKERS * SUBLANES) == 0

    def body(u_hbm, v_hbm, eidx_hbm, x_hbm, g_hbm, after_hbm, out_hbm,
             idx_v, x_v, g_v, rows_v, acc_v, w_v, w16_v, out_v, row_sems, tok_sems, out_sem):
        del after_hbm
        base = (lax.axis_index("s") * 2 + lax.axis_index("c")) * per
        lanes = lax.broadcasted_iota(i32, (SC_LANES,), 0)

        def token_copies(tok):
            tslot = tok % 2
            return (pltpu.make_async_copy(eidx_hbm.at[base + tok], idx_v.at[tslot], tok_sems.at[0]),
                    pltpu.make_async_copy(x_hbm.at[base + tok], x_v.at[tslot], tok_sems.at[1]),
                    pltpu.make_async_copy(g_hbm.at[base + tok], g_v.at[tslot], tok_sems.at[2]))

        def store_out(tok):
            return pltpu.make_async_copy(out_v, out_hbm.at[base + tok], out_sem)

        def gather(tab_hbm, g):
            tok, k = g // 4, g % 4
            return pltpu.make_async_copy(tab_hbm.at[idx_v.at[tok % 2, pl.ds((k % 2) * SC_UNIT_ROWS, SC_UNIT_ROWS)]],
                                         rows_v.at[k % 2], row_sems.at[k % 2])

        def start(g):
            @pl.when(g % 4 < 2)
            def _():
                gather(u_hbm, g).start()

            @pl.when(g % 4 >= 2)
            def _():
                gather(v_hbm, g).start()

        def compute_u(tslot, half, rows):
            @pl.loop(0, SC_UNIT_ROWS // SC_ROW_GROUP)
            def _(rg):
                slot0 = half * SC_UNIT_ROWS + rg * SC_ROW_GROUP

                @pl.loop(0, SC_CHUNKS // SC_CHUNK_GROUP)
                def _(cg):
                    keep = jnp.where(cg > 0, 1.0, 0.0).astype(f32)
                    accs = [acc_v[pl.ds((slot0 + r) * SC_LANES, SC_LANES)] * keep for r in range(SC_ROW_GROUP)]
                    for c in range(SC_CHUNK_GROUP):
                        ch = cg * SC_CHUNK_GROUP + c
                        xa = x_v[tslot, pl.ds(ch * SC_LANES, SC_LANES)]
                        xb = x_v[tslot, pl.ds(SC_WORDS + ch * SC_LANES, SC_LANES)]
                        for r in range(SC_ROW_GROUP):
                            a, b = _sc_unpack(rows[rg * SC_ROW_GROUP + r, pl.ds(ch * SC_LANES, SC_LANES)])
                            accs[r] = accs[r] + a * xa + b * xb
                    for r in range(SC_ROW_GROUP):
                        acc_v[pl.ds((slot0 + r) * SC_LANES, SC_LANES)] = accs[r]

        def gate_weights(tslot):
            @pl.loop(0, P_SLOTS // SC_LANES)
            def _(sg):
                first = sg * SC_LANES * SC_LANES
                act = jnp.zeros((SC_LANES,), f32)
                for lane in range(SC_LANES):
                    act = act + plsc.load_gather(acc_v, [first + lanes * SC_LANES + lane])
                w_v[pl.ds(sg * SC_LANES, SC_LANES)] = g_v[tslot, pl.ds(sg * SC_LANES, SC_LANES)] * _sc_gelu_tanh(act)

            @pl.loop(0, P_SLOTS // SC_LANES)
            def _(sg):
                for r in range(SC_LANES):
                    w16_v[pl.ds((sg * SC_LANES + r) * SC_LANES, SC_LANES)] = plsc.load_gather(
                        w_v, [jnp.zeros((SC_LANES,), i32) + (sg * SC_LANES + r)])

        def compute_v(half, rows):
            @pl.loop(0, SC_CHUNKS // SC_CHUNK_GROUP)
            def _(cg):
                first = cg * SC_CHUNK_GROUP

                @pl.loop(0, SC_UNIT_ROWS // SC_ROW_GROUP)
                def _(rg):
                    keep = jnp.where(jnp.logical_or(half == 1, rg > 0), 1.0, 0.0).astype(f32)
                    spots = [pl.ds((j % 2) * SC_WORDS + (first + j // 2) * SC_LANES, SC_LANES)
                             for j in range(2 * SC_CHUNK_GROUP)]
                    accs = [out_v[spot] * keep for spot in spots]
                    for r in range(SC_ROW_GROUP):
                        row = rg * SC_ROW_GROUP + r
                        wv = w16_v[pl.ds((half * SC_UNIT_ROWS + row) * SC_LANES, SC_LANES)]
                        for c in range(SC_CHUNK_GROUP):
                            a, b = _sc_unpack(rows[row, pl.ds((first + c) * SC_LANES, SC_LANES)])
                            accs[2 * c] = accs[2 * c] + a * wv
                            accs[2 * c + 1] = accs[2 * c + 1] + b * wv
                    for spot, acc in zip(spots, accs):
                        out_v[spot] = acc

        for cp in token_copies(0):
            cp.start()
        for cp in token_copies(0):
            cp.wait()
        start(0)

        @pl.loop(0, units)
        def _(g):
            tok, k = g // 4, g % 4

            @pl.when(jnp.logical_and(k == 0, tok + 1 < per))
            def _():
                for cp in token_copies(tok + 1):
                    cp.start()

            @pl.when(jnp.logical_and(k == 3, tok + 1 < per))
            def _():
                for cp in token_copies(tok + 1):
                    cp.wait()

            @pl.when(g + 1 < units)
            def _():
                start(g + 1)

            gather(u_hbm, g).wait()
            rows = rows_v.at[k % 2]

            @pl.when(k < 2)
            def _():
                compute_u(tok % 2, k, rows)

            @pl.when(k == 1)
            def _():
                gate_weights(tok % 2)

            @pl.when(jnp.logical_and(k == 2, tok > 0))
            def _():
                store_out(tok - 1).wait()

            @pl.when(k >= 2)
            def _():
                compute_v(k - 2, rows)

            @pl.when(k == 3)
            def _():
                store_out(tok).start()

        store_out(per - 1).wait()

    return pl.kernel(
        body, mesh=plsc.VectorSubcoreMesh(core_axis_name="c", subcore_axis_name="s"),
        out_type=jax.ShapeDtypeStruct((n_sc, D_MODEL), f32),
        scratch_types=[pltpu.VMEM((2, P_SLOTS), i32), pltpu.VMEM((2, D_MODEL), f32), pltpu.VMEM((2, P_SLOTS), f32),
                       pltpu.VMEM((2, SC_UNIT_ROWS, SC_WORDS), jnp.uint32), pltpu.VMEM((P_SLOTS * SC_LANES,), f32),
                       pltpu.VMEM((P_SLOTS,), f32), pltpu.VMEM((P_SLOTS * SC_LANES,), f32), pltpu.VMEM((D_MODEL,), f32),
                       pltpu.SemaphoreType.DMA((2,)), pltpu.SemaphoreType.DMA((3,)), pltpu.SemaphoreType.DMA],
        compiler_params=pltpu.CompilerParams(needs_layout_passes=False),
        name="sc_peer",
    )(utab32, vtab32, eidx, xn, gate, after)


def kernel(x_prompt, x_sample, cache_k, cache_v, state_C, state_n, state_m, state_conv, norm1_g, w_in, b_gates, rel_bias, conv_w, conv_b, mh_norm_g, w_out, norm2_g, peer_wq, peer_keys, peer_u, peer_v, final_g):
    bp, sp, d = x_prompt.shape
    bs, ts, _ = x_sample.shape
    n_p, n_s = bp * sp, bs * ts
    n = n_p + n_s
    assert n_p % ROW_BLOCK == 0 and n_s % ROW_BLOCK == 0 and d == D_MODEL
    depth = w_in.shape[0]
    assert depth == 1, "the final norm is fused into the last layer's PEER pass"
    l = 0
    xp, xs = x_prompt.reshape(n_p, d), x_sample.reshape(n_s, d)

    zeros = lambda *shp: jnp.zeros(shp, f32)
    mparams = (conv_w[l], conv_b[l], b_gates[l], mh_norm_g[l])
    retrieval = (w_out[l], norm2_g[l], peer_wq[l], peer_keys[l])

    def mixers(proj, seq, lo, hi, state):
        aq, ak, av, mqk, mv, mo, gates = proj
        att = _attn_prompt(aq, ak, av, rel_bias[l], seq, sp, lo, hi)
        h, c, nn, mm = _mlstm(mqk, mv, mo, gates, seq * sp + lo, 1, hi - lo, *state, *mparams)
        conv_rows = mqk[seq * sp + hi - (CONV_W - 1):seq * sp + hi][None]
        return att, h, (c, nn[:, :M_HEADS], mm[:, :M_HEADS, 0], conv_rows)

    fresh = (zeros(1, M_HEADS, M_DH, M_DH), zeros(1, M_HEADS, M_DH), zeros(1, M_HEADS), zeros(1, CONV_W - 1, 2 * M_WIDTH))

    states = [fresh] * bp

    def stage(proj, lo, hi, after, tail=None):
        parts = []
        for seq in range(bp):
            a, b = max(lo, seq * sp), min(hi, (seq + 1) * sp)
            if a < b:
                att, h, states[seq] = mixers(proj, seq, a - seq * sp, b - seq * sp, states[seq])
                parts.append((att, h, b - a, xp, a))
        parts += [tail] if tail is not None else []
        assert 1 <= len(parts) <= 2, "the retrieval kernel reads at most two row sources"
        (att0, h0, n0, x0, r0), (att1, h1, n1, x1, r1) = parts[0], parts[-1]
        n1, r1 = (n1, r1) if len(parts) == 2 else (0, 0)
        return _mid(x0, r0, x1, r1, att0, att1, h0, h1, n0, n1, after, *retrieval)

    sc_tables = _sc_table(peer_u[l]), _sc_table(peer_v[l])
    peers, x1s, lo, after = [], [], 0, jnp.zeros((SUBLANES, P_SLOTS), i32)
    proj = _inproj(xp, SC_STAGE_ENDS[0], xs, norm1_g[l], w_in[l], after)
    for hi in SC_STAGE_ENDS:
        x1, xn, eidx, _, gate = stage(proj, lo, hi, after)
        peers.append(_sc_peer(*sc_tables, eidx, xn, gate, peers[-1] if peers else zeros(SUBLANES, D_MODEL)))
        x1s.append(x1)
        if lo == 0:
            proj = _inproj(xp, n_p, xs, norm1_g[l], w_in[l], eidx)
        lo, after = hi, eidx
    aq, ak, av, mqk, mv, mo, gates = proj

    lcache = cache_k.shape[2]
    att_s = _attn_sample(aq, ak, av, cache_k[l].reshape(bs, lcache, A_WIDTH),
                         cache_v[l].reshape(bs, lcache, A_WIDTH), rel_bias[l], n_p, bs, ts)
    h_s, c_s, nn_s, mm_s = _mlstm(mqk, mv, mo, gates, n_p, bs, ts, state_C[l], state_n[l], state_m[l],
                                  state_conv[l], *mparams)
    x1, xn, _, eidx_t, gate = stage(proj, lo, n_p, after, tail=(att_s, h_s, n_s, xs, 0))
    w = _peer_u(eidx_t, xn, gate, _expert_table(peer_u[l]), n - lo)
    peers.append(_peer_v(eidx_t, w, _expert_table(peer_v[l])))
    x1s.append(x1)
    y_p, y_s = _final(peers, x1s, final_g, n_p)
    c_p, nn_p, mm_p = (jnp.concatenate(per_seq, axis=0) for per_seq in zip(*(st[:3] for st in states)))

    def tail(a, row0, bsz, t, keep):
        return jnp.stack([a[row0 + (b + 1) * t - keep:row0 + (b + 1) * t] for b in range(bsz)])

    keep = min(WINDOW, sp)
    heads = lambda a: a.reshape(a.shape[0], a.shape[1], A_HEADS, A_DH)
    ctail = CONV_W - 1
    conv_tail = lambda buf, a, row0, bsz, t: jnp.concatenate([buf.astype(a.dtype), tail(a, row0, bsz, t, min(ctail, t))],
                                                             axis=1)[:, -ctail:]
    st = lambda a: a[None]
    return (y_p.reshape(bp, sp, d), y_s.reshape(bs, ts, d),
            st(heads(tail(ak, 0, bp, sp, keep))), st(heads(tail(av, 0, bp, sp, keep))),
            st(c_p), st(nn_p), st(mm_p),
            st(conv_tail(zeros(bp, ctail, 2 * M_WIDTH), mqk, 0, bp, sp)),
            st(heads(ak[n_p:].reshape(bs, ts, A_WIDTH))), st(heads(av[n_p:].reshape(bs, ts, A_WIDTH))),
            st(c_s), st(nn_s[:, :M_HEADS]), st(mm_s[:, :M_HEADS, 0]),
            st(conv_tail(state_conv[l], mqk, n_p, bs, ts)))
```
